```python
import jax, jax.numpy as jnp
from jax import lax
import numpy as np

D_MODEL = 1024
BATCH = 4
SEQ = 8192
DEPTH = 1

RET_HEADS = 4
RET_DK = 64
RET_DV = 128
RET_CHUNK = 128
DSA_HEADS = 8
DSA_KV_HEADS = 2
DSA_HD = 64
IDX_HEADS = 8
IDX_HD = 64
TOPK_MAX = 256
Q_BLOCK = 128
N_EXPERTS = 32
TOP_K = 4
D_EXPERT = D_MODEL
SWIGLU_LIMIT = 7.0
SWIGLU_ALPHA = 1.702
MOE_BLOCK = 128
EPS = 1e-6

RET_W = RET_HEADS * RET_DV
DSA_W = DSA_HEADS * DSA_HD
MIX_W = RET_W + DSA_W
IN_SPLITS = (RET_HEADS * RET_DK,
             RET_HEADS * RET_DK,
             RET_W,
             RET_W,
             DSA_HEADS * DSA_HD,
             DSA_KV_HEADS * DSA_HD,
             DSA_KV_HEADS * DSA_HD,
             IDX_HEADS * IDX_HD,
             IDX_HD,
             IDX_HEADS)
IN_COLS = sum(IN_SPLITS)

kernel_name = 'hybrid_retention_dsa_moe_layer'


def rms_norm(x, g):
    xf = x.astype(jnp.float32)
    y = xf * lax.rsqrt(jnp.mean(xf * xf, axis=-1, keepdims=True) + EPS)
    return (y * g.astype(jnp.float32)).astype(x.dtype)


def retention(q, k, v):
    B, S, H, dk = q.shape
    dv = v.shape[-1]
    dt = q.dtype
    C = min(RET_CHUNK, S)
    N = S // C
    log_g = jnp.log1p(-jnp.exp2(-5.0 - jnp.arange(H, dtype=jnp.float32)))
    pos = jnp.arange(C, dtype=jnp.float32)
    diff = pos[:, None] - pos[None, :]
    d_inner = jnp.where(diff[None] >= 0,
                        jnp.exp(jnp.maximum(diff, 0.0)[None] * log_g[:, None, None]), 0.0)
    q_decay = jnp.exp((pos + 1.0)[None] * log_g[:, None])
    k_decay = jnp.exp((C - 1.0 - pos)[None] * log_g[:, None])
    chunk_decay = jnp.exp(C * log_g).astype(dt)
    qc = q.reshape(B, N, C, H, dk)
    kc = (k * (dk ** -0.5)).reshape(B, N, C, H, dk)
    vc = v.reshape(B, N, C, H, dv)
    s = jnp.einsum('bnihd,bnjhd->bnhij', qc, kc) * d_inner.astype(dt)
    o_in = jnp.einsum('bnhij,bnjhe->bnihe', s, vc)
    kv = jnp.einsum('bnjhd,bnjhe,hj->nbhde', kc, vc, k_decay.astype(dt))

    def step(R, kv_n):
        return R * chunk_decay[None, :, None, None] + kv_n, R

    _, r_prev = lax.scan(step, jnp.zeros((B, H, dk, dv), dt), kv)
    o_cross = jnp.einsum('bnihd,nbhde,hi->bnihe', qc, r_prev, q_decay.astype(dt))
    return (o_in + o_cross).reshape(B, S, H, dv)


def sparse_attention(q, k, v, q_idx, k_idx, w_idx):
    B, S, H, d = q.shape
    G = k.shape[2]
    R = H // G
    n_sel = min(TOPK_MAX, S // 4)
    QB = min(Q_BLOCK, S)
    NB = S // QB
    slopes = jnp.exp2(-8.0 * (jnp.arange(H, dtype=jnp.float32) + 1.0) / H).reshape(G, R)
    key_pos = jnp.arange(S)

    def to_blocks(a):
        return a.reshape((B, NB, QB) + a.shape[2:]).swapaxes(0, 1)

    def block(args):
        qb, qib, wb, blk = args
        t = blk * QB + jnp.arange(QB)
        rel = jnp.einsum('bqhe,bse->bqhs', qib, k_idx)
        score = jnp.einsum('bqhs,bqh->bqs', jax.nn.relu(rel), wb).astype(jnp.float32)
        causal = key_pos[None, :] <= t[:, None]
        score = jnp.where(causal[None], score, -jnp.inf)
        _, sel = lax.top_k(score, n_sel)
        k_sel = jax.vmap(lambda kk, ii: kk[ii])(k, sel)
        v_sel = jax.vmap(lambda vv, ii: vv[ii])(v, sel)
        qg = qb.reshape(B, QB, G, R, d)
        logits = jnp.einsum('bqgrd,bqkgd->bqgrk', qg, k_sel).astype(jnp.float32) * (d ** -0.5)
        dist = (t[None, :, None] - sel).astype(jnp.float32)
        logits = logits - slopes[None, None, :, :, None] * dist[:, :, None, None, :]
        logits = jnp.where((dist >= 0)[:, :, None, None, :], logits, -jnp.inf)
        p = jax.nn.softmax(logits, axis=-1).astype(v.dtype)
        o = jnp.einsum('bqgrk,bqkgd->bqgrd', p, v_sel)
        return o.reshape(B, QB, H * d)

    out = lax.map(block, (to_blocks(q), to_blocks(q_idx), to_blocks(w_idx), jnp.arange(NB)))
    return out.swapaxes(0, 1).reshape(B, S, H * d)


def moe(h, router_w, router_b, w_gu, b_gu, w_down, b_down):
    N, D = h.shape
    logits = (h @ router_w + router_b).astype(jnp.float32)
    top_vals, top_idx = lax.top_k(logits, TOP_K)
    gates = jax.nn.softmax(top_vals, axis=-1).astype(h.dtype)
    NK = N * TOP_K
    flat_e = top_idx.reshape(-1)
    flat_tok = jnp.repeat(jnp.arange(N), TOP_K)
    flat_g = gates.reshape(-1)
    order = jnp.argsort(flat_e)
    sorted_e = flat_e[order]
    counts = jnp.bincount(flat_e, length=N_EXPERTS)
    starts = jnp.cumsum(counts) - counts
    padded_counts = (counts + MOE_BLOCK - 1) // MOE_BLOCK * MOE_BLOCK
    padded_ends = jnp.cumsum(padded_counts)
    padded_starts = padded_ends - padded_counts
    dest = padded_starts[sorted_e] + (jnp.arange(NK) - starts[sorted_e])
    P = (NK + N_EXPERTS * (MOE_BLOCK - 1) + MOE_BLOCK - 1) // MOE_BLOCK * MOE_BLOCK
    n_blocks = P // MOE_BLOCK
    rows_tok = jnp.zeros((P,), jnp.int32).at[dest].set(flat_tok[order].astype(jnp.int32))
    rows_g = jnp.zeros((P,), h.dtype).at[dest].set(flat_g[order])
    block_e = jnp.clip(jnp.searchsorted(padded_ends, jnp.arange(n_blocks) * MOE_BLOCK, side='right'),
                       0, N_EXPERTS - 1)
    xs = h[rows_tok].reshape(n_blocks, MOE_BLOCK, D)

    def expert_block(args):
        xb, e = args
        gu = xb @ w_gu[e] + b_gu[e]
        gate, up = jnp.split(gu, 2, axis=-1)
        gate = jnp.minimum(gate, SWIGLU_LIMIT)
        up = jnp.clip(up, -SWIGLU_LIMIT, SWIGLU_LIMIT)
        glu = gate * jax.nn.sigmoid(SWIGLU_ALPHA * gate)
        return ((up + 1.0) * glu) @ w_down[e] + b_down[e]

    ys = lax.map(expert_block, (xs, block_e)).reshape(P, D) * rows_g[:, None]
    return jax.ops.segment_sum(ys, rows_tok, num_segments=N)


def hybrid_layer(x, c, ada_w, ada_b, norm1_g, w_in, mix_scale, w_o, norm2_g,
                 router_w, router_b, w_gu, b_gu, w_down, b_down):
    B, S, D = x.shape
    mod = jax.nn.silu(c) @ ada_w + ada_b
    sh1, sc1, g1, sh2, sc2, g2 = jnp.split(mod, 6, axis=-1)
    h = rms_norm(x, norm1_g) * (1.0 + sc1[:, None, :]) + sh1[:, None, :]
    proj = h @ w_in
    rq, rk, rv, rg, aq, ak, av, iq, ik, iw = jnp.split(proj, np.cumsum(IN_SPLITS)[:-1].tolist(), axis=-1)
    ret = retention(rq.reshape(B, S, RET_HEADS, RET_DK),
                    rk.reshape(B, S, RET_HEADS, RET_DK),
                    rv.reshape(B, S, RET_HEADS, RET_DV))
    ret32 = ret.astype(jnp.float32)
    ret = (ret32 * lax.rsqrt(jnp.mean(ret32 * ret32, axis=-1, keepdims=True) + EPS)).astype(x.dtype)
    ret = jax.nn.silu(rg) * ret.reshape(B, S, RET_W)
    att = sparse_attention(aq.reshape(B, S, DSA_HEADS, DSA_HD),
                           ak.reshape(B, S, DSA_KV_HEADS, DSA_HD),
                           av.reshape(B, S, DSA_KV_HEADS, DSA_HD),
                           iq.reshape(B, S, IDX_HEADS, IDX_HD),
                           ik,
                           iw * ((IDX_HD ** -0.5) * (IDX_HEADS ** -0.5)))
    mix = jnp.concatenate([ret, att], axis=-1) * mix_scale
    x = x + g1[:, None, :] * (mix @ w_o)
    h2 = rms_norm(x, norm2_g) * (1.0 + sc2[:, None, :]) + sh2[:, None, :]
    y = moe(h2.reshape(B * S, D), router_w, router_b, w_gu, b_gu, w_down, b_down).reshape(B, S, D)
    return x + g2[:, None, :] * y


def setup_inputs(seed: int = 0) -> dict:
    key = jax.random.key(seed)
    ks = jax.random.split(key, 16)
    f32 = jnp.float32

    def nrm(k, shape, scale):
        return jax.random.normal(k, shape, f32) * scale

    return {
        'x': nrm(ks[0], (BATCH, SEQ, D_MODEL), 1.0),
        'c': nrm(ks[1], (BATCH, D_MODEL), 1.0),
        'ada_w': nrm(ks[2], (DEPTH, D_MODEL, 6 * D_MODEL), 0.5 * D_MODEL ** -0.5),
        'ada_b': nrm(ks[3], (DEPTH, 6 * D_MODEL), 0.02),
        'norm1_g': 1.0 + nrm(ks[4], (DEPTH, D_MODEL), 0.02),
        'w_in': nrm(ks[5], (DEPTH, D_MODEL, IN_COLS), D_MODEL ** -0.5),
        'mix_scale': 1.0 + nrm(ks[6], (DEPTH, MIX_W), 0.02),
        'w_o': nrm(ks[7], (DEPTH, MIX_W, D_MODEL), MIX_W ** -0.5),
        'norm2_g': 1.0 + nrm(ks[8], (DEPTH, D_MODEL), 0.02),
        'router_w': nrm(ks[9], (DEPTH, D_MODEL, N_EXPERTS), D_MODEL ** -0.5),
        'router_b': nrm(ks[10], (DEPTH, N_EXPERTS), 0.01),
        'w_gu': nrm(ks[11], (DEPTH, N_EXPERTS, D_MODEL, 2 * D_EXPERT), D_MODEL ** -0.5),
        'b_gu': nrm(ks[12], (DEPTH, N_EXPERTS, 2 * D_EXPERT), 0.01),
        'w_down': nrm(ks[13], (DEPTH, N_EXPERTS, D_EXPERT, D_MODEL), D_EXPERT ** -0.5),
        'b_down': nrm(ks[14], (DEPTH, N_EXPERTS, D_MODEL), 0.01),
        'final_g': 1.0 + nrm(ks[15], (D_MODEL,), 0.02),
    }


def reference(x, c, ada_w, ada_b, norm1_g, w_in, mix_scale, w_o, norm2_g,
              router_w, router_b, w_gu, b_gu, w_down, b_down, final_g):
    for l in range(DEPTH):
        x = hybrid_layer(x, c, ada_w[l], ada_b[l], norm1_g[l], w_in[l], mix_scale[l], w_o[l],
                         norm2_g[l], router_w[l], router_b[l], w_gu[l], b_gu[l],
                         w_down[l], b_down[l])
    return rms_norm(x, final_g)
```

```python
import functools

import numpy as np
import jax
import jax.numpy as jnp
from jax import lax
from jax.experimental import pallas as pl
from jax.experimental.pallas import tpu as pltpu

F32 = jnp.float32
BF16 = jnp.bfloat16
I32 = jnp.int32

D_MODEL = 1024
RET_HEADS = 4
RET_DK = 64
RET_DV = 128
RET_CHUNK = 128
DSA_HEADS = 8
DSA_KV_HEADS = 2
DSA_HD = 64
IDX_HEADS = 8
IDX_HD = 64
TOPK_MAX = 256
N_EXPERTS = 32
TOP_K = 4
D_EXPERT = D_MODEL
SWIGLU_LIMIT = 7.0
SWIGLU_ALPHA = 1.702
EPS = 1e-6

RET_W = RET_HEADS * RET_DV
DSA_W = DSA_HEADS * DSA_HD
IN_COLS = 2888
IN_COLS_PAD = 2944

LANES = 128
VMEM_LIMIT = 56 * 1024 * 1024
NEG_BIG = -1e30
F32_LOWEST = float(np.finfo(np.float32).min)


def _cparams(sem):
    return pltpu.CompilerParams(dimension_semantics=sem, vmem_limit_bytes=VMEM_LIMIT)


def _mod_kernel(c_ref, w_ref, b_ref, o_ref):
    c = c_ref[...]
    s = c * (1.0 / (1.0 + jnp.exp(-c)))
    o_ref[...] = jnp.dot(s, w_ref[...], preferred_element_type=F32,
                         precision=lax.Precision.HIGHEST) + b_ref[...]


def _mod(c, ada_w, ada_b):
    B, D = c.shape
    n_out = ada_w.shape[1]
    rows = 8
    c8 = jnp.zeros((rows, D), F32).at[:B].set(c)
    out = pl.pallas_call(
        _mod_kernel,
        grid=(n_out // D,),
        in_specs=[pl.BlockSpec((rows, D), lambda j: (0, 0)),
                  pl.BlockSpec((D, D), lambda j: (0, j)),
                  pl.BlockSpec((1, D), lambda j: (0, j))],
        out_specs=pl.BlockSpec((rows, D), lambda j: (0, j)),
        out_shape=jax.ShapeDtypeStruct((rows, n_out), F32),
        compiler_params=_cparams(("arbitrary",)),
        name="mod",
    )(c8, ada_w, ada_b.reshape(1, n_out))
    return out[:B]


def _inproj_kernel(x_ref, g_ref, sc_ref, sh_ref, w_ref,
                   rq_ref, rk_ref, rv_ref, rg_ref, aq_ref, akt_ref, av_ref, iq_ref, ikt_ref, iw_ref):
    x = x_ref[...]
    ms = jnp.mean(x * x, axis=-1, keepdims=True)
    y = x * lax.rsqrt(ms + EPS) * g_ref[...]
    hb = (y * (1.0 + sc_ref[...]) + sh_ref[...]).astype(BF16)

    def proj(lo, hi):
        return jnp.dot(hb, w_ref[:, lo:hi], preferred_element_type=F32)

    rq_ref[...] = proj(0, 256).astype(BF16)
    rk_ref[...] = (proj(256, 512) * (RET_DK ** -0.5)).astype(BF16)
    rv_ref[...] = proj(512, 1024).astype(BF16)
    rg_ref[...] = proj(1024, 1536).astype(BF16)
    aq_ref[...] = (proj(1536, 2048) * (DSA_HD ** -0.5)).astype(BF16)
    akt_ref[...] = proj(2048, 2176).T.astype(BF16)
    av_ref[...] = proj(2176, 2304).astype(BF16)
    iq_ref[...] = proj(2304, 2816).astype(BF16)
    last = proj(2816, 2944)
    ikt_ref[...] = last.T[:IDX_HD, :].astype(BF16)
    iw_ref[...] = last[:, IDX_HD:IDX_HD + IDX_HEADS] * ((IDX_HD ** -0.5) * (IDX_HEADS ** -0.5))


def _inproj(x, norm_g, sc, sh, w_pad, tm):
    B, S, D = x.shape
    row = lambda w: pl.BlockSpec((None, tm, w), lambda b, i: (b, i, 0))
    colT = lambda h: pl.BlockSpec((None, h, tm), lambda b, i: (b, 0, i))
    vec = pl.BlockSpec((None, 1, D), lambda b, i: (b, 0, 0))
    sd = lambda shape, dt: jax.ShapeDtypeStruct(shape, dt)
    return pl.pallas_call(
        _inproj_kernel,
        grid=(B, S // tm),
        in_specs=[row(D), pl.BlockSpec((1, D), lambda b, i: (0, 0)), vec, vec,
                  pl.BlockSpec((D, IN_COLS_PAD), lambda b, i: (0, 0))],
        out_specs=[row(256), row(256), row(512), row(512), row(512), colT(128), row(128), row(512),
                   colT(IDX_HD), row(IDX_HEADS)],
        out_shape=[sd((B, S, 256), BF16), sd((B, S, 256), BF16), sd((B, S, 512), BF16),
                   sd((B, S, 512), BF16), sd((B, S, 512), BF16), sd((B, 128, S), BF16),
                   sd((B, S, 128), BF16), sd((B, S, 512), BF16), sd((B, IDX_HD, S), BF16),
                   sd((B, S, IDX_HEADS), F32)],
        compiler_params=_cparams(("parallel", "parallel")),
        name="inproj",
    )(x, norm_g.reshape(1, D), sc, sh, w_pad)


def _ret_kernel(rq_ref, rk_ref, rv_ref, rg_ref, din_ref, qd_ref, kd_ref, cd_ref, ms_ref, o_ref, state_ref):
    @pl.when(pl.program_id(1) == 0)
    def _():
        state_ref[...] = jnp.zeros_like(state_ref)

    for h in range(RET_HEADS):
        q = rq_ref[:, h * RET_DK:(h + 1) * RET_DK]
        k = rk_ref[:, h * RET_DK:(h + 1) * RET_DK]
        v = rv_ref[:, h * RET_DV:(h + 1) * RET_DV]
        r_prev = state_ref[h]
        s = lax.dot_general(q, k, (((1,), (1,)), ((), ())), preferred_element_type=F32) * din_ref[h]
        o = jnp.dot(s.astype(BF16), v, preferred_element_type=F32)
        o = o + jnp.dot(q, r_prev.astype(BF16), preferred_element_type=F32) * qd_ref[h]
        vd = (v.astype(F32) * kd_ref[h]).astype(BF16)
        kv = lax.dot_general(k, vd, (((0,), (0,)), ((), ())), preferred_element_type=F32)
        state_ref[h] = r_prev * cd_ref[h] + kv
        o = o * lax.rsqrt(jnp.mean(o * o, axis=-1, keepdims=True) + EPS)
        g = rg_ref[:, h * RET_DV:(h + 1) * RET_DV].astype(F32)
        gate = g * (1.0 / (1.0 + jnp.exp(-g)))
        o_ref[:, h * RET_DV:(h + 1) * RET_DV] = (gate * o * ms_ref[:, h * RET_DV:(h + 1) * RET_DV]).astype(BF16)


def _ret_consts(C):
    H = RET_HEADS
    log_g = np.log1p(-np.exp2(-5.0 - np.arange(H, dtype=np.float64)))
    pos = np.arange(C, dtype=np.float64)
    diff = pos[:, None] - pos[None, :]
    d_inner = np.where(diff[None] >= 0, np.exp(np.maximum(diff, 0.0)[None] * log_g[:, None, None]), 0.0)
    q_decay = np.exp((pos + 1.0)[None] * log_g[:, None])
    k_decay = np.exp((C - 1.0 - pos)[None] * log_g[:, None])
    chunk_decay = np.exp(C * log_g)
    qd = np.broadcast_to(q_decay[:, :, None], (H, C, RET_DV))
    kd = np.broadcast_to(k_decay[:, :, None], (H, C, RET_DV))
    cd = np.broadcast_to(chunk_decay[:, None, None], (H, 1, RET_DV))
    f = lambda a: jnp.asarray(np.ascontiguousarray(a), F32)
    return f(d_inner), f(qd), f(kd), f(cd)


def _retention(rq, rk, rv, rg, ms_ret):
    B, S, _ = rq.shape
    C = min(RET_CHUNK, S)
    din, qd, kd, cd = _ret_consts(C)
    row = lambda w: pl.BlockSpec((None, C, w), lambda b, n: (b, n, 0))
    full = lambda a: pl.BlockSpec(a.shape, lambda b, n: (0,) * a.ndim)
    return pl.pallas_call(
        _ret_kernel,
        grid=(B, S // C),
        in_specs=[row(256), row(256), row(512), row(512), full(din), full(qd), full(kd), full(cd),
                  pl.BlockSpec((1, RET_W), lambda b, n: (0, 0))],
        out_specs=row(RET_W),
        out_shape=jax.ShapeDtypeStruct((B, S, RET_W), BF16),
        scratch_shapes=[pltpu.VMEM((RET_HEADS, RET_DK, RET_DV), F32)],
        compiler_params=_cparams(("parallel", "arbitrary")),
        name="ret",
    )(rq, rk, rv, rg, din, qd, kd, cd, ms_ret)


def _dsa_kernel(iq_ref, iw_ref, aq_ref, ikt_ref, akt_ref, av_ref, ms_ref, o_ref, score_ref, wb_ref,
                *, tq, tk, n_sel):
    H, G, R, d = DSA_HEADS, DSA_KV_HEADS, DSA_HEADS // DSA_KV_HEADS, DSA_HD
    t0 = pl.program_id(1) * tq
    nk = (t0 + tq + tk - 1) // tk
    sub = tk // LANES

    iw = iw_ref[...]
    for h in range(IDX_HEADS):
        wb_ref[h] = jnp.broadcast_to(iw[:, h:h + 1], (tq, LANES))
    qi = [iq_ref[:, h * IDX_HD:(h + 1) * IDX_HD] for h in range(IDX_HEADS)]
    row_t = t0 + lax.broadcasted_iota(I32, (tq, tk), 0)
    lane_k = lax.broadcasted_iota(I32, (tq, tk), 1)

    def score_chunk(c, carry):
        k0 = pl.multiple_of(c * tk, tk)
        kc = ikt_ref[:, pl.ds(k0, tk)]
        acc = jnp.zeros((tq, tk), F32)
        for h in range(IDX_HEADS):
            rel = jnp.dot(qi[h], kc, preferred_element_type=F32)
            w = jnp.concatenate([wb_ref[h]] * sub, axis=1)
            acc = acc + jnp.maximum(rel, 0.0) * w
        score_ref[:, pl.ds(k0, tk)] = jnp.where(k0 + lane_k <= row_t, acc, -jnp.inf)
        return carry

    lax.fori_loop(0, nk, score_chunk, 0)

    def count(th, strict):
        thb = jnp.broadcast_to(th, (tq, LANES))

        def body(c, acc):
            k0 = pl.multiple_of(c * tk, tk)
            for j in range(sub):
                s = score_ref[:, pl.ds(k0 + j * LANES, LANES)]
                hit = (s > thb) if strict else (s >= thb)
                acc = acc + jnp.where(hit, 1.0, 0.0)
            return acc

        acc = lax.fori_loop(0, nk, body, jnp.zeros((tq, LANES), F32))
        return jnp.sum(acc, axis=1, keepdims=True)

    def minmax(c, carry):
        mx, mn = carry
        k0 = pl.multiple_of(c * tk, tk)
        for j in range(sub):
            s = score_ref[:, pl.ds(k0 + j * LANES, LANES)]
            mx = jnp.maximum(mx, s)
            mn = jnp.minimum(mn, jnp.where(s == -jnp.inf, jnp.inf, s))
        return mx, mn

    mx, mn = lax.fori_loop(0, nk, minmax, (jnp.full((tq, LANES), -jnp.inf, F32),
                                           jnp.full((tq, LANES), jnp.inf, F32)))
    hi0 = jnp.max(mx, axis=1, keepdims=True)
    lo0 = jnp.min(mn, axis=1, keepdims=True)
    kf = float(n_sel)

    def active(lo, hi):
        mid = lo + (hi - lo) * 0.5
        return jnp.max(jnp.where((mid != lo) & (mid != hi), 1.0, 0.0))

    def bis_cond(c):
        return (c[2] > 0.0) & (c[3] < 4096)

    def bis_body(c):
        lo, hi, _, it = c
        mid = lo + (hi - lo) * 0.5
        cnt = count(mid, False)
        ge = cnt >= kf
        nlo = jnp.where(ge, mid, lo)
        nhi = jnp.where(cnt == kf, mid, jnp.where(ge, hi, mid))
        return nlo, nhi, active(nlo, nhi), it + 1

    lo, hi, _, _ = lax.while_loop(bis_cond, bis_body, (lo0, hi0, active(lo0, hi0), jnp.int32(0)))
    thr = jnp.where(count(hi, False) >= kf, hi, lo)
    row1 = t0 + lax.broadcasted_iota(I32, (tq, 1), 0)
    thr = jnp.where(row1 + 1 <= n_sel, F32_LOWEST, thr)
    excess = count(thr, False) - kf

    @pl.when(jnp.max(excess) > 0.0)
    def _():
        budget = kf - count(thr, True)
        before = (lax.broadcasted_iota(I32, (tk, tk), 0) < lax.broadcasted_iota(I32, (tk, tk), 1))
        before = jnp.where(before, 1.0, 0.0).astype(BF16)

        def fix(c, seen):
            k0 = pl.multiple_of(c * tk, tk)
            s = score_ref[:, pl.ds(k0, tk)]
            eq = s == thr
            eqf = jnp.where(eq, 1.0, 0.0)
            rank = jnp.dot(eqf.astype(BF16), before, preferred_element_type=F32) + seen
            score_ref[:, pl.ds(k0, tk)] = jnp.where(eq & (rank >= budget), -jnp.inf, s)
            return seen + jnp.sum(eqf, axis=1, keepdims=True)

        lax.fori_loop(0, nk, fix, jnp.zeros((tq, 1), F32))

    aq = [aq_ref[:, h * d:(h + 1) * d] for h in range(H)]
    slopes = [float(2.0 ** (-8.0 * (h + 1) / H)) for h in range(H)]
    lane1 = lax.broadcasted_iota(I32, (1, tk), 1)

    def att_chunk(c, carry):
        ms, ls, accs = carry
        k0 = pl.multiple_of(c * tk, tk)
        sel = score_ref[:, pl.ds(k0, tk)] >= thr
        kpos = (k0 + lane1 - t0).astype(F32)
        v = av_ref[pl.ds(k0, tk), :]
        nms, nls, naccs = [], [], []
        for h in range(H):
            g = h // R
            kt = akt_ref[g * d:(g + 1) * d, pl.ds(k0, tk)]
            logit = jnp.dot(aq[h], kt, preferred_element_type=F32) + slopes[h] * kpos
            logit = jnp.where(sel, logit, NEG_BIG)
            m_new = jnp.maximum(ms[h], jnp.max(logit, axis=1, keepdims=True))
            p = jnp.exp(logit - m_new)
            alpha = jnp.exp(ms[h] - m_new)
            nls.append(ls[h] * alpha + jnp.sum(p, axis=1, keepdims=True))
            naccs.append(accs[h] * alpha + jnp.dot(p.astype(BF16), v, preferred_element_type=F32))
            nms.append(m_new)
        return tuple(nms), tuple(nls), tuple(naccs)

    init = (tuple(jnp.full((tq, 1), NEG_BIG, F32) for _ in range(H)),
            tuple(jnp.zeros((tq, 1), F32) for _ in range(H)),
            tuple(jnp.zeros((tq, G * d), F32) for _ in range(H)))
    _, ls, accs = lax.fori_loop(0, nk, att_chunk, init)
    for h in range(H):
        g = h // R
        o = accs[h][:, g * d:(g + 1) * d] / ls[h]
        o_ref[:, h * d:(h + 1) * d] = (o * ms_ref[:, h * d:(h + 1) * d]).astype(BF16)


def _dsa(iq, iw, aq, ikt, akt, av, ms_att, tq, tk):
    B, S, _ = iq.shape
    n_sel = min(TOPK_MAX, S // 4)
    row = lambda w: pl.BlockSpec((None, tq, w), lambda b, i: (b, i, 0))
    return pl.pallas_call(
        functools.partial(_dsa_kernel, tq=tq, tk=tk, n_sel=n_sel),
        grid=(B, S // tq),
        in_specs=[row(512), row(IDX_HEADS), row(512),
                  pl.BlockSpec((None, IDX_HD, S), lambda b, i: (b, 0, 0)),
                  pl.BlockSpec((None, 128, S), lambda b, i: (b, 0, 0)),
                  pl.BlockSpec((None, S, 128), lambda b, i: (b, 0, 0)),
                  pl.BlockSpec((1, DSA_W), lambda b, i: (0, 0))],
        out_specs=row(DSA_W),
        out_shape=jax.ShapeDtypeStruct((B, S, DSA_W), BF16),
        scratch_shapes=[pltpu.VMEM((tq, S), F32), pltpu.VMEM((IDX_HEADS, tq, LANES), F32)],
        compiler_params=_cparams(("parallel", "arbitrary")),
        name="dsa",
    )(iq, iw, aq, ikt, akt, av, ms_att)


def _oproj_kernel(ret_ref, att_ref, x_ref, wo_ref, g1_ref, n2_ref, sc_ref, sh_ref, rw_ref, rb_ref,
                  x1_ref, h2_ref, sel_ref, idx_ref, gate_ref, cnt_ref):
    mixo = jnp.dot(ret_ref[...], wo_ref[:RET_W, :], preferred_element_type=F32)
    mixo = mixo + jnp.dot(att_ref[...], wo_ref[RET_W:, :], preferred_element_type=F32)
    x1 = x_ref[...] + g1_ref[...] * mixo
    x1_ref[...] = x1
    y = x1 * lax.rsqrt(jnp.mean(x1 * x1, axis=-1, keepdims=True) + EPS) * n2_ref[...]
    h2 = y * (1.0 + sc_ref[...]) + sh_ref[...]
    h2_ref[...] = h2
    logits = jnp.dot(h2, rw_ref[...], preferred_element_type=F32,
                     precision=lax.Precision.HIGHEST) + rb_ref[...]
    tm = logits.shape[0]
    lane = lax.broadcasted_iota(I32, (tm, LANES), 1).astype(F32)
    work = jnp.where(lane < N_EXPERTS, logits, -jnp.inf)
    sel = jnp.zeros((tm, LANES), F32)
    idx_tab = jnp.zeros((tm, LANES), F32)
    vals = []
    for k in range(TOP_K):
        m = jnp.max(work, axis=1, keepdims=True)
        idx = jnp.min(jnp.where(work == m, lane, float(LANES)), axis=1, keepdims=True)
        hit = lane == idx
        sel = jnp.where(hit, 1.0, sel)
        idx_tab = jnp.where(lane == k, idx, idx_tab)
        work = jnp.where(hit, -jnp.inf, work)
        vals.append(m)
    es = [jnp.exp(v - vals[0]) for v in vals]
    den = es[0] + es[1] + es[2] + es[3]
    gate_tab = jnp.zeros((tm, LANES), F32)
    for k in range(TOP_K):
        gate_tab = jnp.where(lane == k, es[k] / den, gate_tab)
    sel_ref[...] = sel
    idx_ref[...] = idx_tab
    gate_ref[...] = gate_tab

    @pl.when((pl.program_id(0) == 0) & (pl.program_id(1) == 0))
    def _():
        cnt_ref[...] = jnp.zeros_like(cnt_ref)

    cnt_ref[...] += jnp.sum(sel, axis=0, keepdims=True)


def _oproj(ret, att, x, wo, g1, n2g, sc2, sh2, rw_pad, rb_pad, tm):
    B, S, D = x.shape
    nt = S // tm
    row = lambda w: pl.BlockSpec((None, tm, w), lambda b, i: (b, i, 0))
    flat = lambda w: pl.BlockSpec((tm, w), lambda b, i: (b * nt + i, 0))
    vec = pl.BlockSpec((None, 1, D), lambda b, i: (b, 0, 0))
    cst = lambda shape: pl.BlockSpec(shape, lambda b, i: (0, 0))
    sd = lambda shape, dt: jax.ShapeDtypeStruct(shape, dt)
    N = B * S
    return pl.pallas_call(
        _oproj_kernel,
        grid=(B, nt),
        in_specs=[row(RET_W), row(DSA_W), row(D), cst((D, D)), vec, cst((1, D)), vec, vec,
                  cst((D, LANES)), cst((1, LANES))],
        out_specs=[flat(D), flat(D), flat(LANES), flat(LANES), flat(LANES), cst((1, LANES))],
        out_shape=[sd((N, D), F32), sd((N, D), F32), sd((N, LANES), F32), sd((N, LANES), F32),
                   sd((N, LANES), F32), sd((1, LANES), F32)],
        compiler_params=_cparams(("arbitrary", "arbitrary")),
        name="oproj",
    )(ret, att, x, wo, g1, n2g.reshape(1, D), sc2, sh2, rw_pad, rb_pad)


def _dest_kernel(sel_ref, idx_ref, pstart_ref, dest_ref, seen_ref):
    @pl.when(pl.program_id(0) == 0)
    def _():
        seen_ref[...] = jnp.zeros_like(seen_ref)

    sel = sel_ref[...]
    tm = sel.shape[0]
    earlier = lax.broadcasted_iota(I32, (tm, tm), 1) < lax.broadcasted_iota(I32, (tm, tm), 0)
    earlier = jnp.where(earlier, 1.0, 0.0).astype(BF16)
    rank = jnp.dot(earlier, sel.astype(BF16), preferred_element_type=F32) + seen_ref[...]
    dest = pstart_ref[...] + rank
    lane = lax.broadcasted_iota(I32, (tm, LANES), 1).astype(F32)
    idx_tab = idx_ref[...]
    out = jnp.zeros((tm, LANES), F32)
    for k in range(TOP_K):
        e_k = jnp.sum(jnp.where(lane == k, idx_tab, 0.0), axis=1, keepdims=True)
        d_k = jnp.sum(jnp.where(lane == e_k, dest, 0.0), axis=1, keepdims=True)
        out = jnp.where(lane == k, d_k, out)
    dest_ref[...] = out.astype(I32)
    seen_ref[...] += jnp.sum(sel, axis=0, keepdims=True)


def _dest(sel, idx_tab, pstart, tm):
    N = sel.shape[0]
    blk = pl.BlockSpec((tm, LANES), lambda i: (i, 0))
    return pl.pallas_call(
        _dest_kernel,
        grid=(N // tm,),
        in_specs=[blk, blk, pl.BlockSpec((1, LANES), lambda i: (0, 0))],
        out_specs=blk,
        out_shape=jax.ShapeDtypeStruct((N, LANES), I32),
        scratch_shapes=[pltpu.VMEM((1, LANES), F32)],
        compiler_params=_cparams(("arbitrary",)),
        name="dest",
    )(sel, idx_tab, pstart)


def _disp_kernel(h2_ref, dest_hbm, xs_in, xs_hbm, dsm, sem_idx, sem_row, *, tm):
    del xs_in
    i = pl.program_id(0)
    n = tm * TOP_K
    cp = pltpu.make_async_copy(dest_hbm.at[pl.ds(i * n, n)], dsm, sem_idx)
    cp.start()
    cp.wait()

    def row_copy(t, dst):
        return pltpu.make_async_copy(h2_ref.at[pl.ds(t, 1)], xs_hbm.at[pl.ds(dst, 1)], sem_row)

    def issue(t, c):
        for k in range(TOP_K):
            row_copy(t, dsm[t * TOP_K + k]).start()
        return c

    lax.fori_loop(0, tm, issue, 0)

    def drain(t, c):
        for k in range(TOP_K):
            row_copy(0, 0).wait()
        return c

    lax.fori_loop(0, tm, drain, 0)


def _dispatch(h2, dest_flat, n_rows, tm):
    N, D = h2.shape
    xs0 = jnp.zeros((n_rows, D), F32)
    return pl.pallas_call(
        functools.partial(_disp_kernel, tm=tm),
        grid=(N // tm,),
        in_specs=[pl.BlockSpec((tm, D), lambda i: (i, 0)),
                  pl.BlockSpec(memory_space=pl.ANY),
                  pl.BlockSpec(memory_space=pl.ANY)],
        out_specs=pl.BlockSpec(memory_space=pl.ANY),
        out_shape=jax.ShapeDtypeStruct((n_rows, D), F32),
        scratch_shapes=[pltpu.SMEM((tm * TOP_K,), I32), pltpu.SemaphoreType.DMA, pltpu.SemaphoreType.DMA],
        input_output_aliases={2: 0},
        compiler_params=_cparams(("arbitrary",)),
        name="disp",
    )(h2, dest_flat, xs0)


def _ffn_kernel(be_ref, nb_ref, xs_ref, wgu_ref, bgu_ref, wd_ref, bd_ref, ys_ref):
    @pl.when(pl.program_id(0) < nb_ref[0])
    def _():
        xb = xs_ref[...].astype(BF16)
        gu = jnp.dot(xb, wgu_ref[...], preferred_element_type=F32) + bgu_ref[...]
        gate = jnp.minimum(gu[:, :D_EXPERT], SWIGLU_LIMIT)
        up = jnp.clip(gu[:, D_EXPERT:], -SWIGLU_LIMIT, SWIGLU_LIMIT)
        glu = gate * (1.0 / (1.0 + jnp.exp(-SWIGLU_ALPHA * gate)))
        act = ((up + 1.0) * glu).astype(BF16)
        ys_ref[...] = jnp.dot(act, wd_ref[...], preferred_element_type=F32) + bd_ref[...]

    @pl.when(pl.program_id(0) >= nb_ref[0])
    def _():
        ys_ref[...] = jnp.zeros_like(ys_ref)


def _ffn(xs, block_e, n_blocks, w_gu, b_gu, w_down, b_down, tmb):
    P, D = xs.shape
    E = w_gu.shape[0]
    blk = lambda i, be, nb: (jnp.minimum(i, nb[0] - 1), 0)
    wsel = lambda i, be, nb: (be[jnp.minimum(i, nb[0] - 1)], 0, 0)
    grid_spec = pltpu.PrefetchScalarGridSpec(
        num_scalar_prefetch=2,
        grid=(P // tmb,),
        in_specs=[pl.BlockSpec((tmb, D), blk),
                  pl.BlockSpec((None, D, 2 * D_EXPERT), wsel),
                  pl.BlockSpec((None, 1, 2 * D_EXPERT), wsel),
                  pl.BlockSpec((None, D_EXPERT, D), wsel),
                  pl.BlockSpec((None, 1, D), wsel)],
        out_specs=pl.BlockSpec((tmb, D), lambda i, be, nb: (i, 0)),
    )
    return pl.pallas_call(
        _ffn_kernel,
        grid_spec=grid_spec,
        out_shape=jax.ShapeDtypeStruct((P, D), F32),
        compiler_params=_cparams(("arbitrary",)),
        name="ffn",
    )(block_e, n_blocks, xs, w_gu, b_gu.reshape(E, 1, 2 * D_EXPERT), w_down, b_down.reshape(E, 1, D))


def _comb_kernel(x1_ref, gate_ref, g2_ref, fg_ref, dest_hbm, ys_hbm, o_ref, buf, dsm, sem_idx, sem_row, *, tm):
    i = pl.program_id(0)
    n = tm * TOP_K
    cp = pltpu.make_async_copy(dest_hbm.at[pl.ds(i * n, n)], dsm, sem_idx)
    cp.start()
    cp.wait()

    def row_copy(t, k, src):
        return pltpu.make_async_copy(ys_hbm.at[pl.ds(src, 1)], buf.at[k, pl.ds(t, 1)], sem_row)

    def issue(t, c):
        for k in range(TOP_K):
            row_copy(t, k, dsm[t * TOP_K + k]).start()
        return c

    lax.fori_loop(0, tm, issue, 0)

    def drain(t, c):
        for k in range(TOP_K):
            row_copy(0, k, 0).wait()
        return c

    lax.fori_loop(0, tm, drain, 0)

    gates = gate_ref[...]
    y = jnp.zeros(x1_ref.shape, F32)
    for k in range(TOP_K):
        y = y + gates[:, k:k + 1] * buf[k]
    v = x1_ref[...] + g2_ref[...] * y
    o_ref[...] = v * lax.rsqrt(jnp.mean(v * v, axis=-1, keepdims=True) + EPS) * fg_ref[...]


def _combine(x1, gate_tab, g2, final_g, dest_flat, ys, S, tm):
    N, D = x1.shape
    per_b = S // tm
    return pl.pallas_call(
        functools.partial(_comb_kernel, tm=tm),
        grid=(N // tm,),
        in_specs=[pl.BlockSpec((tm, D), lambda i: (i, 0)),
                  pl.BlockSpec((tm, LANES), lambda i: (i, 0)),
                  pl.BlockSpec((None, 1, D), lambda i: (i // per_b, 0, 0)),
                  pl.BlockSpec((1, D), lambda i: (0, 0)),
                  pl.BlockSpec(memory_space=pl.ANY),
                  pl.BlockSpec(memory_space=pl.ANY)],
        out_specs=pl.BlockSpec((tm, D), lambda i: (i, 0)),
        out_shape=jax.ShapeDtypeStruct((N, D), F32),
        scratch_shapes=[pltpu.VMEM((TOP_K, tm, D), F32), pltpu.SMEM((tm * TOP_K,), I32),
                        pltpu.SemaphoreType.DMA, pltpu.SemaphoreType.DMA],
        compiler_params=_cparams(("arbitrary",)),
        name="comb",
    )(x1, gate_tab, g2, final_g.reshape(1, D), dest_flat, ys)


def _tile(n, pref):
    t = min(pref, n)
    assert n % t == 0, (n, t)
    return t


def _layer(x, c, ada_w, ada_b, norm1_g, w_in, mix_scale, w_o, norm2_g,
           router_w, router_b, w_gu, b_gu, w_down, b_down, final_g):
    B, S, D = x.shape
    N = B * S
    mod = _mod(c, ada_w, ada_b).reshape(B, 6, 1, D)
    sh1, sc1, g1, sh2, sc2, g2 = (mod[:, j] for j in range(6))

    w_pad = jnp.pad(w_in, ((0, 0), (0, IN_COLS_PAD - IN_COLS))).astype(BF16)
    rq, rk, rv, rg, aq, akt, av, iq, ikt, iw = _inproj(x, norm1_g, sc1, sh1, w_pad, _tile(S, 512))
    ms = mix_scale.reshape(1, RET_W + DSA_W)
    ret = _retention(rq, rk, rv, rg, ms[:, :RET_W])
    att = _dsa(iq, iw, aq, ikt, akt, av, ms[:, RET_W:], _tile(S, 128), _tile(S, 512))

    rw_pad = jnp.pad(router_w, ((0, 0), (0, LANES - N_EXPERTS)))
    rb_pad = jnp.pad(router_b, (0, LANES - N_EXPERTS)).reshape(1, LANES)
    x1, h2, sel, idx_tab, gate_tab, counts = _oproj(ret, att, x, w_o.astype(BF16), g1, norm2_g, sc2, sh2,
                                                    rw_pad, rb_pad, _tile(S, 512))

    tmb = 512
    n_rows = (N * TOP_K + N_EXPERTS * (tmb - 1)) // tmb * tmb
    cnt = counts[0, :N_EXPERTS].astype(I32)
    padded = (cnt + tmb - 1) // tmb * tmb
    ends = jnp.cumsum(padded)
    pstart = jnp.pad((ends - padded).astype(F32), (0, LANES - N_EXPERTS)).reshape(1, LANES)
    n_blocks = (ends[-1] // tmb).reshape(1)
    block_e = jnp.clip(jnp.searchsorted(ends, jnp.arange(n_rows // tmb, dtype=I32) * tmb, side='right'),
                       0, N_EXPERTS - 1).astype(I32)

    tmd = _tile(N, 256)
    dest_tab = _dest(sel, idx_tab, pstart, tmd)
    dest_flat = dest_tab[:, :TOP_K].reshape(N * TOP_K)
    xs = _dispatch(h2, dest_flat, n_rows, tmd)
    ys = _ffn(xs, block_e, n_blocks, w_gu.astype(BF16), b_gu, w_down.astype(BF16), b_down, tmb)
    out = _combine(x1, gate_tab, g2, final_g, dest_flat, ys, S, _tile(S, 256))
    return out.reshape(B, S, D)


def kernel(x, c, ada_w, ada_b, norm1_g, w_in, mix_scale, w_o, norm2_g, router_w, router_b, w_gu, b_gu,
           w_down, b_down, final_g):
    assert ada_w.shape[0] == 1, "single-layer stack"
    return _layer(x, c, ada_w[0], ada_b[0], norm1_g[0], w_in[0], mix_scale[0], w_o[0], norm2_g[0],
                  router_w[0], router_b[0], w_gu[0], b_gu[0], w_down[0], b_down[0], final_g)
```

```python
import functools

import numpy as np
import jax
import jax.numpy as jnp
from jax import lax
from jax.experimental import pallas as pl
from jax.experimental.pallas import tpu as pltpu

F32 = jnp.float32
BF16 = jnp.bfloat16
I32 = jnp.int32

D_MODEL = 1024
RET_HEADS = 4
RET_DK = 64
RET_DV = 128
RET_CHUNK = 128
DSA_HEADS = 8
DSA_KV_HEADS = 2
DSA_HD = 64
IDX_HEADS = 8
IDX_HD = 64
TOPK_MAX = 256
N_EXPERTS = 32
TOP_K = 4
D_EXPERT = D_MODEL
SWIGLU_LIMIT = 7.0
SWIGLU_ALPHA = 1.702
EPS = 1e-6

RET_W = RET_HEADS * RET_DV
DSA_W = DSA_HEADS * DSA_HD
IN_COLS = 2888
IN_COLS_PAD = 2944

KAUG = 128
VAUG = 80
ALIBI_SPLIT = 64
BISECT_VALUE_STEPS = 12
BISECT_MAX_STEPS = 64

LANES = 128
VMEM_LIMIT = 56 * 1024 * 1024
NEG_BIG = -1e30
F32_LOWEST = float(np.finfo(np.float32).min)


def _cparams(sem):
    return pltpu.CompilerParams(dimension_semantics=sem, vmem_limit_bytes=VMEM_LIMIT)


def _mod_kernel(c_ref, w_ref, b_ref, o_ref):
    c = c_ref[...]
    s = c * (1.0 / (1.0 + jnp.exp(-c)))
    o_ref[...] = jnp.dot(s, w_ref[...], preferred_element_type=F32,
                         precision=lax.Precision.HIGHEST) + b_ref[...]


def _mod(c, ada_w, ada_b):
    B, D = c.shape
    n_out = ada_w.shape[1]
    rows = 8
    c8 = jnp.zeros((rows, D), F32).at[:B].set(c)
    out = pl.pallas_call(
        _mod_kernel,
        grid=(n_out // D,),
        in_specs=[pl.BlockSpec((rows, D), lambda j: (0, 0)),
                  pl.BlockSpec((D, D), lambda j: (0, j)),
                  pl.BlockSpec((1, D), lambda j: (0, j))],
        out_specs=pl.BlockSpec((rows, D), lambda j: (0, j)),
        out_shape=jax.ShapeDtypeStruct((rows, n_out), F32),
        compiler_params=_cparams(("arbitrary",)),
        name="mod",
    )(c8, ada_w, ada_b.reshape(1, n_out))
    return out[:B]


def _inproj_kernel(x_ref, g_ref, sc_ref, sh_ref, w_ref,
                   rq_ref, rk_ref, rv_ref, rg_ref, aqt_ref, ak_ref, avt_ref, iqt_ref, ik_ref, iwt_ref):
    x = x_ref[...]
    ms = jnp.mean(x * x, axis=-1, keepdims=True)
    y = x * lax.rsqrt(ms + EPS) * g_ref[...]
    hb = (y * (1.0 + sc_ref[...]) + sh_ref[...]).astype(BF16)

    def proj(lo, hi):
        return jnp.dot(hb, w_ref[:, lo:hi], preferred_element_type=F32)

    tm = x.shape[0]
    d = DSA_HD
    rq_ref[...] = proj(0, 256).astype(BF16)
    rk_ref[...] = (proj(256, 512) * (RET_DK ** -0.5)).astype(BF16)
    rv_ref[...] = proj(512, 1024).astype(BF16)
    rg_ref[...] = proj(1024, 1536).astype(BF16)
    aqt_ref[...] = (proj(1536, 2048) * (d ** -0.5)).T.astype(BF16)
    kk = proj(2048, 2176)
    pos = pl.program_id(1) * tm + lax.broadcasted_iota(I32, (tm, d), 0)
    col = lax.broadcasted_iota(I32, (tm, d), 1)
    posblk = jnp.where(col == 0, pos // ALIBI_SPLIT, jnp.where(col == 1, pos % ALIBI_SPLIT, 0)).astype(F32)
    for g in range(DSA_KV_HEADS):
        ak_ref[:, g * KAUG:g * KAUG + d] = kk[:, g * d:(g + 1) * d].astype(BF16)
        ak_ref[:, g * KAUG + d:(g + 1) * KAUG] = posblk.astype(BF16)
    vt = proj(2176, 2304).T
    r16 = lax.broadcasted_iota(I32, (VAUG - d, tm), 0)
    onesblk = jnp.where(r16 == 0, 1.0, 0.0).astype(BF16)
    for g in range(DSA_KV_HEADS):
        avt_ref[g * VAUG:g * VAUG + d, :] = vt[g * d:(g + 1) * d, :].astype(BF16)
        avt_ref[g * VAUG + d:(g + 1) * VAUG, :] = onesblk
    iqt_ref[...] = proj(2304, 2816).T.astype(BF16)
    last = proj(2816, 2944)
    ik_ref[...] = last[:, :IDX_HD].astype(BF16)
    iwt_ref[...] = last.T[IDX_HD:IDX_HD + IDX_HEADS, :] * ((IDX_HD ** -0.5) * (IDX_HEADS ** -0.5))


def _inproj(x, norm_g, sc, sh, w_pad, tm):
    B, S, D = x.shape
    row = lambda w: pl.BlockSpec((None, tm, w), lambda b, i: (b, i, 0))
    colT = lambda h: pl.BlockSpec((None, h, tm), lambda b, i: (b, 0, i))
    vec = pl.BlockSpec((None, 1, D), lambda b, i: (b, 0, 0))
    sd = lambda shape, dt: jax.ShapeDtypeStruct(shape, dt)
    G = DSA_KV_HEADS
    return pl.pallas_call(
        _inproj_kernel,
        grid=(B, S // tm),
        in_specs=[row(D), pl.BlockSpec((1, D), lambda b, i: (0, 0)), vec, vec,
                  pl.BlockSpec((D, IN_COLS_PAD), lambda b, i: (0, 0))],
        out_specs=[row(256), row(256), row(512), row(512), colT(DSA_W), row(G * KAUG), colT(G * VAUG),
                   colT(IDX_HEADS * IDX_HD), row(IDX_HD), colT(IDX_HEADS)],
        out_shape=[sd((B, S, 256), BF16), sd((B, S, 256), BF16), sd((B, S, 512), BF16),
                   sd((B, S, 512), BF16), sd((B, DSA_W, S), BF16), sd((B, S, G * KAUG), BF16),
                   sd((B, G * VAUG, S), BF16), sd((B, IDX_HEADS * IDX_HD, S), BF16),
                   sd((B, S, IDX_HD), BF16), sd((B, IDX_HEADS, S), F32)],
        compiler_params=_cparams(("parallel", "parallel")),
        name="inproj",
    )(x, norm_g.reshape(1, D), sc, sh, w_pad)


def _ret_kernel(rq_ref, rk_ref, rv_ref, rg_ref, din_ref, qd_ref, kd_ref, cd_ref, ms_ref, o_ref, state_ref):
    @pl.when(pl.program_id(1) == 0)
    def _():
        state_ref[...] = jnp.zeros_like(state_ref)

    for h in range(RET_HEADS):
        q = rq_ref[:, h * RET_DK:(h + 1) * RET_DK]
        k = rk_ref[:, h * RET_DK:(h + 1) * RET_DK]
        v = rv_ref[:, h * RET_DV:(h + 1) * RET_DV]
        r_prev = state_ref[h]
        s = lax.dot_general(q, k, (((1,), (1,)), ((), ())), preferred_element_type=F32) * din_ref[h]
        o = jnp.dot(s.astype(BF16), v, preferred_element_type=F32)
        o = o + jnp.dot(q, r_prev.astype(BF16), preferred_element_type=F32) * qd_ref[h]
        vd = (v.astype(F32) * kd_ref[h]).astype(BF16)
        kv = lax.dot_general(k, vd, (((0,), (0,)), ((), ())), preferred_element_type=F32)
        state_ref[h] = r_prev * cd_ref[h] + kv
        o = o * lax.rsqrt(jnp.mean(o * o, axis=-1, keepdims=True) + EPS)
        g = rg_ref[:, h * RET_DV:(h + 1) * RET_DV].astype(F32)
        gate = g * (1.0 / (1.0 + jnp.exp(-g)))
        o_ref[:, h * RET_DV:(h + 1) * RET_DV] = (gate * o * ms_ref[:, h * RET_DV:(h + 1) * RET_DV]).astype(BF16)


def _ret_consts(C):
    H = RET_HEADS
    log_g = np.log1p(-np.exp2(-5.0 - np.arange(H, dtype=np.float64)))
    pos = np.arange(C, dtype=np.float64)
    diff = pos[:, None] - pos[None, :]
    d_inner = np.where(diff[None] >= 0, np.exp(np.maximum(diff, 0.0)[None] * log_g[:, None, None]), 0.0)
    q_decay = np.exp((pos + 1.0)[None] * log_g[:, None])
    k_decay = np.exp((C - 1.0 - pos)[None] * log_g[:, None])
    chunk_decay = np.exp(C * log_g)
    qd = np.broadcast_to(q_decay[:, :, None], (H, C, RET_DV))
    kd = np.broadcast_to(k_decay[:, :, None], (H, C, RET_DV))
    cd = np.broadcast_to(chunk_decay[:, None, None], (H, 1, RET_DV))
    f = lambda a: jnp.asarray(np.ascontiguousarray(a), F32)
    return f(d_inner), f(qd), f(kd), f(cd)


def _retention(rq, rk, rv, rg, ms_ret):
    B, S, _ = rq.shape
    C = min(RET_CHUNK, S)
    din, qd, kd, cd = _ret_consts(C)
    row = lambda w: pl.BlockSpec((None, C, w), lambda b, n: (b, n, 0))
    full = lambda a: pl.BlockSpec(a.shape, lambda b, n: (0,) * a.ndim)
    return pl.pallas_call(
        _ret_kernel,
        grid=(B, S // C),
        in_specs=[row(256), row(256), row(512), row(512), full(din), full(qd), full(kd), full(cd),
                  pl.BlockSpec((1, RET_W), lambda b, n: (0, 0))],
        out_specs=row(RET_W),
        out_shape=jax.ShapeDtypeStruct((B, S, RET_W), BF16),
        scratch_shapes=[pltpu.VMEM((RET_HEADS, RET_DK, RET_DV), F32)],
        compiler_params=_cparams(("parallel", "arbitrary")),
        name="ret",
    )(rq, rk, rv, rg, din, qd, kd, cd, ms_ret)


def _f32_key(x):
    i = lax.bitcast_convert_type(x, I32)
    return i ^ ((i >> 31) & 0x7FFFFFFF)


def _key_f32(k):
    return lax.bitcast_convert_type(k ^ ((k >> 31) & 0x7FFFFFFF), F32)


def _dsa_kernel(iqt_ref, iwt_ref, aqt_ref, ik_ref, ak_ref, avt_ref, ms_ref, o_ref, score_ref, qa_ref, *acc_refs,
                tq, tks, n_sel):
    H, G, R, d = DSA_HEADS, DSA_KV_HEADS, DSA_HEADS // DSA_KV_HEADS, DSA_HD
    t0 = pl.program_id(1) * tq
    nsub = (t0 + tq) // tks
    kf = float(n_sel)
    qpos = t0 + lax.broadcasted_iota(I32, (1, tq), 1)
    krow = lax.broadcasted_iota(I32, (tks, tq), 0)

    wrow = [iwt_ref[h:h + 1, :] for h in range(IDX_HEADS)]

    def score_sub(j, carry):
        r0 = pl.multiple_of(j * tks, tks)
        kc = ik_ref[pl.ds(r0, tks), :]
        acc = jnp.zeros((tks, tq), F32)
        for h in range(IDX_HEADS):
            rel = jnp.dot(kc, iqt_ref[h * IDX_HD:(h + 1) * IDX_HD, :], preferred_element_type=F32)
            acc = acc + jnp.maximum(rel, 0.0) * wrow[h]
        score_ref[pl.ds(r0, tks), :] = jnp.where(r0 + krow <= qpos, acc, -jnp.inf)
        return carry

    lax.fori_loop(0, nsub, score_sub, 0)

    def fold8(x, op):
        acc = x[0:8, :]
        for i in range(1, tks // 8):
            acc = op(acc, x[8 * i:8 * (i + 1), :])
        return acc

    def count(th, strict):
        def body(j, acc):
            s = score_ref[pl.ds(pl.multiple_of(j * tks, tks), tks), :]
            hit = (s > th) if strict else (s >= th)
            return acc + fold8(jnp.where(hit, 1.0, 0.0), jnp.add)

        acc = lax.fori_loop(0, nsub, body, jnp.zeros((8, tq), F32))
        return jnp.sum(acc, axis=0, keepdims=True)

    def minmax(j, carry):
        mx, mn = carry
        s = score_ref[pl.ds(pl.multiple_of(j * tks, tks), tks), :]
        mx = jnp.maximum(mx, fold8(s, jnp.maximum))
        mn = jnp.minimum(mn, fold8(jnp.where(s == -jnp.inf, jnp.inf, s), jnp.minimum))
        return mx, mn

    mx, mn = lax.fori_loop(0, nsub, minmax, (jnp.full((8, tq), -jnp.inf, F32), jnp.full((8, tq), jnp.inf, F32)))
    hi0 = jnp.max(mx, axis=0, keepdims=True)
    lo0 = jnp.min(mn, axis=0, keepdims=True)

    def midpoint(lo, hi, it):
        lk, hk = _f32_key(lo), _f32_key(hi)
        mk = (lk >> 1) + (hk >> 1) + (lk & hk & 1)
        mv = lo + (hi - lo) * 0.5
        early = (jnp.zeros((1, tq), I32) + it) < BISECT_VALUE_STEPS
        mid = jnp.where(early & (mv > lo) & (mv < hi), mv, _key_f32(mk))
        return mid, jnp.max(jnp.where(mk != lk, 1.0, 0.0))

    def bis_cond(c):
        return (c[2] > 0.0) & (c[3] < BISECT_MAX_STEPS)

    def bis_body(c):
        lo, hi, _, it = c
        mid, _ = midpoint(lo, hi, it)
        cnt = count(mid, False)
        ge = cnt >= kf
        nlo = jnp.where(ge, mid, lo)
        nhi = jnp.where(cnt == kf, mid, jnp.where(ge, hi, mid))
        _, active = midpoint(nlo, nhi, it + 1)
        return nlo, nhi, active, it + 1

    _, active0 = midpoint(lo0, hi0, jnp.int32(0))
    lo, hi, _, _ = lax.while_loop(bis_cond, bis_body, (lo0, hi0, active0, jnp.int32(0)))
    thr = jnp.where(count(hi, False) >= kf, hi, lo)
    thr = jnp.where(qpos + 1 <= n_sel, F32_LOWEST, thr)
    excess = count(thr, False) - kf

    @pl.when(jnp.max(excess) > 0.0)
    def _():
        budget = kf - count(thr, True)
        earlier = lax.broadcasted_iota(I32, (tks, tks), 1) < lax.broadcasted_iota(I32, (tks, tks), 0)
        earlier = jnp.where(earlier, 1.0, 0.0).astype(BF16)

        def fix(j, seen):
            r0 = pl.multiple_of(j * tks, tks)
            s = score_ref[pl.ds(r0, tks), :]
            eq = s == thr
            eqf = jnp.where(eq, 1.0, 0.0)
            rank = jnp.dot(earlier, eqf.astype(BF16), preferred_element_type=F32) + seen
            score_ref[pl.ds(r0, tks), :] = jnp.where(eq & (rank >= budget), -jnp.inf, s)
            return seen + jnp.sum(eqf, axis=0, keepdims=True)

        lax.fori_loop(0, nsub, fix, jnp.zeros((1, tq), F32))

    arow = lax.broadcasted_iota(I32, (KAUG - d, tq), 0)
    for h in range(H):
        slope = float(2.0 ** (-8.0 * (h + 1) / H))
        qa_ref[h, 0:d, :] = aqt_ref[h * d:(h + 1) * d, :]
        qa_ref[h, d:KAUG, :] = jnp.where(arow == 0, slope * ALIBI_SPLIT, jnp.where(arow == 1, slope, 0.0)).astype(BF16)
    for acc in acc_refs:
        acc[...] = jnp.zeros_like(acc)

    def att_sub(j, ms):
        r0 = pl.multiple_of(j * tks, tks)
        sel = score_ref[pl.ds(r0, tks), :] >= thr
        def logits(h):
            ka = ak_ref[pl.ds(r0, tks), (h // R) * KAUG:(h // R + 1) * KAUG]
            return jnp.dot(ka, qa_ref[h], preferred_element_type=F32)

        nms = []
        s_next = logits(0)
        for h in range(H):
            g = h // R
            s, s_next = s_next, (logits(h + 1) if h + 1 < H else None)
            s = jnp.where(sel, s, NEG_BIG)
            m_new = jnp.maximum(ms[h], jnp.max(s, axis=0, keepdims=True))
            p = jnp.exp(s - m_new).astype(BF16)
            va = avt_ref[g * VAUG:(g + 1) * VAUG, pl.ds(r0, tks)]
            acc = acc_refs[h]
            acc[...] = acc[...] * jnp.exp(ms[h] - m_new) + jnp.dot(va, p, preferred_element_type=F32)
            nms.append(m_new)
        return tuple(nms)

    lax.fori_loop(0, nsub, att_sub, tuple(jnp.full((1, tq), NEG_BIG, F32) for _ in range(H)))
    for h in range(H):
        a = acc_refs[h][...]
        o = a[0:d, :] / a[d:d + 1, :]
        o_ref[h * d:(h + 1) * d, :] = (o * ms_ref[h * d:(h + 1) * d, :]).astype(BF16)


def _dsa(iqt, iwt, aqt, ik, ak, avt, ms_att, tq, tks):
    B, _, S = iqt.shape
    assert S <= ALIBI_SPLIT * 256 and tq % tks == 0
    n_sel = min(TOPK_MAX, S // 4)
    G = DSA_KV_HEADS
    colT = lambda h: pl.BlockSpec((None, h, tq), lambda b, i: (b, 0, i))
    msb = jnp.broadcast_to(ms_att.reshape(DSA_W, 1), (DSA_W, tq))
    return pl.pallas_call(
        functools.partial(_dsa_kernel, tq=tq, tks=tks, n_sel=n_sel),
        grid=(B, S // tq),
        in_specs=[colT(IDX_HEADS * IDX_HD), colT(IDX_HEADS), colT(DSA_W),
                  pl.BlockSpec((None, S, IDX_HD), lambda b, i: (b, 0, 0)),
                  pl.BlockSpec((None, S, G * KAUG), lambda b, i: (b, 0, 0)),
                  pl.BlockSpec((None, G * VAUG, S), lambda b, i: (b, 0, 0)),
                  pl.BlockSpec((DSA_W, tq), lambda b, i: (0, 0))],
        out_specs=colT(DSA_W),
        out_shape=jax.ShapeDtypeStruct((B, DSA_W, S), BF16),
        scratch_shapes=[pltpu.VMEM((S, tq), F32), pltpu.VMEM((DSA_HEADS, KAUG, tq), BF16)]
        + [pltpu.VMEM((VAUG, tq), F32) for _ in range(DSA_HEADS)],
        compiler_params=_cparams(("parallel", "arbitrary")),
        name="dsa",
    )(iqt, iwt, aqt, ik, ak, avt, msb)


def _oproj_kernel(ret_ref, att_ref, x_ref, wo_ref, g1_ref, n2_ref, sc_ref, sh_ref, rw_ref, rb_ref,
                  x1_ref, h2_ref, sel_ref, idx_ref, gate_ref, cnt_ref):
    mixo = jnp.dot(ret_ref[...], wo_ref[:RET_W, :], preferred_element_type=F32)
    mixo = mixo + lax.dot_general(att_ref[...], wo_ref[RET_W:, :], (((0,), (0,)), ((), ())),
                                  preferred_element_type=F32)
    x1 = x_ref[...] + g1_ref[...] * mixo
    x1_ref[...] = x1
    y = x1 * lax.rsqrt(jnp.mean(x1 * x1, axis=-1, keepdims=True) + EPS) * n2_ref[...]
    h2 = y * (1.0 + sc_ref[...]) + sh_ref[...]
    h2_ref[...] = h2
    logits = jnp.dot(h2, rw_ref[...], preferred_element_type=F32,
                     precision=lax.Precision.HIGHEST) + rb_ref[...]
    tm = logits.shape[0]
    lane = lax.broadcasted_iota(I32, (tm, LANES), 1).astype(F32)
    work = jnp.where(lane < N_EXPERTS, logits, -jnp.inf)
    sel = jnp.zeros((tm, LANES), F32)
    idx_tab = jnp.zeros((tm, LANES), F32)
    vals = []
    for k in range(TOP_K):
        m = jnp.max(work, axis=1, keepdims=True)
        idx = jnp.min(jnp.where(work == m, lane, float(LANES)), axis=1, keepdims=True)
        hit = lane == idx
        sel = jnp.where(hit, 1.0, sel)
        idx_tab = jnp.where(lane == k, idx, idx_tab)
        work = jnp.where(hit, -jnp.inf, work)
        vals.append(m)
    es = [jnp.exp(v - vals[0]) for v in vals]
    den = es[0] + es[1] + es[2] + es[3]
    gate_tab = jnp.zeros((tm, LANES), F32)
    for k in range(TOP_K):
        gate_tab = jnp.where(lane == k, es[k] / den, gate_tab)
    sel_ref[...] = sel
    idx_ref[...] = idx_tab
    gate_ref[...] = gate_tab

    @pl.when((pl.program_id(0) == 0) & (pl.program_id(1) == 0))
    def _():
        cnt_ref[...] = jnp.zeros_like(cnt_ref)

    cnt_ref[...] += jnp.sum(sel, axis=0, keepdims=True)


def _oproj(ret, att, x, wo, g1, n2g, sc2, sh2, rw_pad, rb_pad, tm):
    B, S, D = x.shape
    nt = S // tm
    row = lambda w: pl.BlockSpec((None, tm, w), lambda b, i: (b, i, 0))
    flat = lambda w: pl.BlockSpec((tm, w), lambda b, i: (b * nt + i, 0))
    vec = pl.BlockSpec((None, 1, D), lambda b, i: (b, 0, 0))
    cst = lambda shape: pl.BlockSpec(shape, lambda b, i: (0, 0))
    sd = lambda shape, dt: jax.ShapeDtypeStruct(shape, dt)
    N = B * S
    return pl.pallas_call(
        _oproj_kernel,
        grid=(B, nt),
        in_specs=[row(RET_W), pl.BlockSpec((None, DSA_W, tm), lambda b, i: (b, 0, i)), row(D), cst((D, D)), vec,
                  cst((1, D)), vec, vec,
                  cst((D, LANES)), cst((1, LANES))],
        out_specs=[flat(D), flat(D), flat(LANES), flat(LANES), flat(LANES), cst((1, LANES))],
        out_shape=[sd((N, D), F32), sd((N, D), F32), sd((N, LANES), F32), sd((N, LANES), F32),
                   sd((N, LANES), F32), sd((1, LANES), F32)],
        compiler_params=_cparams(("arbitrary", "arbitrary")),
        name="oproj",
    )(ret, att, x, wo, g1, n2g.reshape(1, D), sc2, sh2, rw_pad, rb_pad)


def _dest_kernel(sel_ref, idx_ref, pstart_ref, dest_ref, seen_ref):
    @pl.when(pl.program_id(0) == 0)
    def _():
        seen_ref[...] = jnp.zeros_like(seen_ref)

    sel = sel_ref[...]
    tm = sel.shape[0]
    earlier = lax.broadcasted_iota(I32, (tm, tm), 1) < lax.broadcasted_iota(I32, (tm, tm), 0)
    earlier = jnp.where(earlier, 1.0, 0.0).astype(BF16)
    rank = jnp.dot(earlier, sel.astype(BF16), preferred_element_type=F32) + seen_ref[...]
    dest = pstart_ref[...] + rank
    lane = lax.broadcasted_iota(I32, (tm, LANES), 1).astype(F32)
    idx_tab = idx_ref[...]
    out = jnp.zeros((tm, LANES), F32)
    for k in range(TOP_K):
        e_k = jnp.sum(jnp.where(lane == k, idx_tab, 0.0), axis=1, keepdims=True)
        d_k = jnp.sum(jnp.where(lane == e_k, dest, 0.0), axis=1, keepdims=True)
        out = jnp.where(lane == k, d_k, out)
    dest_ref[...] = out.astype(I32)
    seen_ref[...] += jnp.sum(sel, axis=0, keepdims=True)


def _dest(sel, idx_tab, pstart, tm):
    N = sel.shape[0]
    blk = pl.BlockSpec((tm, LANES), lambda i: (i, 0))
    return pl.pallas_call(
        _dest_kernel,
        grid=(N // tm,),
        in_specs=[blk, blk, pl.BlockSpec((1, LANES), lambda i: (0, 0))],
        out_specs=blk,
        out_shape=jax.ShapeDtypeStruct((N, LANES), I32),
        scratch_shapes=[pltpu.VMEM((1, LANES), F32)],
        compiler_params=_cparams(("arbitrary",)),
        name="dest",
    )(sel, idx_tab, pstart)


def _disp_kernel(h2_ref, dest_hbm, xs_in, xs_hbm, dsm, sem_idx, sem_row, *, tm):
    del xs_in
    i = pl.program_id(0)
    n = tm * TOP_K
    cp = pltpu.make_async_copy(dest_hbm.at[pl.ds(i * n, n)], dsm, sem_idx)
    cp.start()
    cp.wait()

    def row_copy(t, dst):
        return pltpu.make_async_copy(h2_ref.at[pl.ds(t, 1)], xs_hbm.at[pl.ds(dst, 1)], sem_row)

    def issue(t, c):
        for k in range(TOP_K):
            row_copy(t, dsm[t * TOP_K + k]).start()
        return c

    lax.fori_loop(0, tm, issue, 0)

    def drain(t, c):
        for k in range(TOP_K):
            row_copy(0, 0).wait()
        return c

    lax.fori_loop(0, tm, drain, 0)


def _dispatch(h2, dest_flat, n_rows, tm):
    N, D = h2.shape
    xs0 = jnp.zeros((n_rows, D), F32)
    return pl.pallas_call(
        functools.partial(_disp_kernel, tm=tm),
        grid=(N // tm,),
        in_specs=[pl.BlockSpec((tm, D), lambda i: (i, 0)),
                  pl.BlockSpec(memory_space=pl.ANY),
                  pl.BlockSpec(memory_space=pl.ANY)],
        out_specs=pl.BlockSpec(memory_space=pl.ANY),
        out_shape=jax.ShapeDtypeStruct((n_rows, D), F32),
        scratch_shapes=[pltpu.SMEM((tm * TOP_K,), I32), pltpu.SemaphoreType.DMA, pltpu.SemaphoreType.DMA],
        input_output_aliases={2: 0},
        compiler_params=_cparams(("arbitrary",)),
        name="disp",
    )(h2, dest_flat, xs0)


def _ffn_kernel(be_ref, nb_ref, xs_ref, wgu_ref, bgu_ref, wd_ref, bd_ref, ys_ref):
    @pl.when(pl.program_id(0) < nb_ref[0])
    def _():
        xb = xs_ref[...].astype(BF16)
        gu = jnp.dot(xb, wgu_ref[...], preferred_element_type=F32) + bgu_ref[...]
        gate = jnp.minimum(gu[:, :D_EXPERT], SWIGLU_LIMIT)
        up = jnp.clip(gu[:, D_EXPERT:], -SWIGLU_LIMIT, SWIGLU_LIMIT)
        glu = gate * (1.0 / (1.0 + jnp.exp(-SWIGLU_ALPHA * gate)))
        act = ((up + 1.0) * glu).astype(BF16)
        ys_ref[...] = jnp.dot(act, wd_ref[...], preferred_element_type=F32) + bd_ref[...]

    @pl.when(pl.program_id(0) >= nb_ref[0])
    def _():
        ys_ref[...] = jnp.zeros_like(ys_ref)


def _ffn(xs, block_e, n_blocks, w_gu, b_gu, w_down, b_down, tmb):
    P, D = xs.shape
    E = w_gu.shape[0]
    blk = lambda i, be, nb: (jnp.minimum(i, nb[0] - 1), 0)
    wsel = lambda i, be, nb: (be[jnp.minimum(i, nb[0] - 1)], 0, 0)
    grid_spec = pltpu.PrefetchScalarGridSpec(
        num_scalar_prefetch=2,
        grid=(P // tmb,),
        in_specs=[pl.BlockSpec((tmb, D), blk),
                  pl.BlockSpec((None, D, 2 * D_EXPERT), wsel),
                  pl.BlockSpec((None, 1, 2 * D_EXPERT), wsel),
                  pl.BlockSpec((None, D_EXPERT, D), wsel),
                  pl.BlockSpec((None, 1, D), wsel)],
        out_specs=pl.BlockSpec((tmb, D), lambda i, be, nb: (i, 0)),
    )
    return pl.pallas_call(
        _ffn_kernel,
        grid_spec=grid_spec,
        out_shape=jax.ShapeDtypeStruct((P, D), F32),
        compiler_params=_cparams(("arbitrary",)),
        name="ffn",
    )(block_e, n_blocks, xs, w_gu, b_gu.reshape(E, 1, 2 * D_EXPERT), w_down, b_down.reshape(E, 1, D))


def _comb_kernel(x1_ref, gate_ref, g2_ref, fg_ref, dest_hbm, ys_hbm, o_ref, buf, dsm, sem_idx, sem_row, *, tm):
    i = pl.program_id(0)
    n = tm * TOP_K
    cp = pltpu.make_async_copy(dest_hbm.at[pl.ds(i * n, n)], dsm, sem_idx)
    cp.start()
    cp.wait()

    def row_copy(t, k, src):
        return pltpu.make_async_copy(ys_hbm.at[pl.ds(src, 1)], buf.at[k, pl.ds(t, 1)], sem_row)

    def issue(t, c):
        for k in range(TOP_K):
            row_copy(t, k, dsm[t * TOP_K + k]).start()
        return c

    lax.fori_loop(0, tm, issue, 0)

    def drain(t, c):
        for k in range(TOP_K):
            row_copy(0, k, 0).wait()
        return c

    lax.fori_loop(0, tm, drain, 0)

    gates = gate_ref[...]
    y = jnp.zeros(x1_ref.shape, F32)
    for k in range(TOP_K):
        y = y + gates[:, k:k + 1] * buf[k]
    v = x1_ref[...] + g2_ref[...] * y
    o_ref[...] = v * lax.rsqrt(jnp.mean(v * v, axis=-1, keepdims=True) + EPS) * fg_ref[...]


def _combine(x1, gate_tab, g2, final_g, dest_flat, ys, S, tm):
    N, D = x1.shape
    per_b = S // tm
    return pl.pallas_call(
        functools.partial(_comb_kernel, tm=tm),
        grid=(N // tm,),
        in_specs=[pl.BlockSpec((tm, D), lambda i: (i, 0)),
                  pl.BlockSpec((tm, LANES), lambda i: (i, 0)),
                  pl.BlockSpec((None, 1, D), lambda i: (i // per_b, 0, 0)),
                  pl.BlockSpec((1, D), lambda i: (0, 0)),
                  pl.BlockSpec(memory_space=pl.ANY),
                  pl.BlockSpec(memory_space=pl.ANY)],
        out_specs=pl.BlockSpec((tm, D), lambda i: (i, 0)),
        out_shape=jax.ShapeDtypeStruct((N, D), F32),
        scratch_shapes=[pltpu.VMEM((TOP_K, tm, D), F32), pltpu.SMEM((tm * TOP_K,), I32),
                        pltpu.SemaphoreType.DMA, pltpu.SemaphoreType.DMA],
        compiler_params=_cparams(("arbitrary",)),
        name="comb",
    )(x1, gate_tab, g2, final_g.reshape(1, D), dest_flat, ys)


def _tile(n, pref):
    t = min(pref, n)
    assert n % t == 0, (n, t)
    return t


def _layer(x, c, ada_w, ada_b, norm1_g, w_in, mix_scale, w_o, norm2_g,
           router_w, router_b, w_gu, b_gu, w_down, b_down, final_g):
    B, S, D = x.shape
    N = B * S
    mod = _mod(c, ada_w, ada_b).reshape(B, 6, 1, D)
    sh1, sc1, g1, sh2, sc2, g2 = (mod[:, j] for j in range(6))

    w_pad = jnp.pad(w_in, ((0, 0), (0, IN_COLS_PAD - IN_COLS))).astype(BF16)
    rq, rk, rv, rg, aqt, ak, avt, iqt, ik, iwt = _inproj(x, norm1_g, sc1, sh1, w_pad, _tile(S, 512))
    ms = mix_scale.reshape(1, RET_W + DSA_W)
    ret = _retention(rq, rk, rv, rg, ms[:, :RET_W])
    att = _dsa(iqt, iwt, aqt, ik, ak, avt, ms[:, RET_W:], _tile(S, 256), 128)

    rw_pad = jnp.pad(router_w, ((0, 0), (0, LANES - N_EXPERTS)))
    rb_pad = jnp.pad(router_b, (0, LANES - N_EXPERTS)).reshape(1, LANES)
    x1, h2, sel, idx_tab, gate_tab, counts = _oproj(ret, att, x, w_o.astype(BF16), g1, norm2_g, sc2, sh2,
                                                    rw_pad, rb_pad, _tile(S, 512))

    tmb = 512
    n_rows = (N * TOP_K + N_EXPERTS * (tmb - 1)) // tmb * tmb
    cnt = counts[0, :N_EXPERTS].astype(I32)
    padded = (cnt + tmb - 1) // tmb * tmb
    ends = jnp.cumsum(padded)
    pstart = jnp.pad((ends - padded).astype(F32), (0, LANES - N_EXPERTS)).reshape(1, LANES)
    n_blocks = (ends[-1] // tmb).reshape(1)
    block_e = jnp.clip(jnp.searchsorted(ends, jnp.arange(n_rows // tmb, dtype=I32) * tmb, side='right'),
                       0, N_EXPERTS - 1).astype(I32)

    tmd = _tile(N, 256)
    dest_tab = _dest(sel, idx_tab, pstart, tmd)
    dest_flat = dest_tab[:, :TOP_K].reshape(N * TOP_K)
    xs = _dispatch(h2, dest_flat, n_rows, tmd)
    ys = _ffn(xs, block_e, n_blocks, w_gu.astype(BF16), b_gu, w_down.astype(BF16), b_down, tmb)
    out = _combine(x1, gate_tab, g2, final_g, dest_flat, ys, S, _tile(S, 256))
    return out.reshape(B, S, D)


def kernel(x, c, ada_w, ada_b, norm1_g, w_in, mix_scale, w_o, norm2_g, router_w, router_b, w_gu, b_gu,
           w_down, b_down, final_g):
    assert ada_w.shape[0] == 1, "single-layer stack"
    return _layer(x, c, ada_w[0], ada_b[0], norm1_g[0], w_in[0], mix_scale[0], w_o[0], norm2_g[0],
                  router_w[0], router_b[0], w_gu[0], b_gu[0], w_down[0], b_down[0], final_g)
```

```python
import functools

import numpy as np
import jax
import jax.numpy as jnp
from jax import lax
from jax.experimental import pallas as pl
from jax.experimental.pallas import tpu as pltpu

F32 = jnp.float32
BF16 = jnp.bfloat16
I32 = jnp.int32

D_MODEL = 1024
RET_HEADS = 4
RET_DK = 64
RET_DV = 128
RET_CHUNK = 128
DSA_HEADS = 8
DSA_KV_HEADS = 2
DSA_HD = 64
IDX_HEADS = 8
IDX_HD = 64
TOPK_MAX = 256
N_EXPERTS = 32
TOP_K = 4
D_EXPERT = D_MODEL
SWIGLU_LIMIT = 7.0
SWIGLU_ALPHA = 1.702
EPS = 1e-6

RET_W = RET_HEADS * RET_DV
DSA_W = DSA_HEADS * DSA_HD
IN_COLS = 2888
IN_COLS_PAD = 2944

KAUG = 128
VAUG = 80
ALIBI_SPLIT = 64
BISECT_VALUE_STEPS = 12
BISECT_MAX_STEPS = 64

LANES = 128
VMEM_LIMIT = 56 * 1024 * 1024
NEG_BIG = -1e30
F32_LOWEST = float(np.finfo(np.float32).min)


def _cparams(sem):
    return pltpu.CompilerParams(dimension_semantics=sem, vmem_limit_bytes=VMEM_LIMIT)


def _mod_kernel(c_ref, w_ref, b_ref, o_ref):
    c = c_ref[...]
    s = c * (1.0 / (1.0 + jnp.exp(-c)))
    o_ref[...] = jnp.dot(s, w_ref[...], preferred_element_type=F32,
                         precision=lax.Precision.HIGHEST) + b_ref[...]


def _mod(c, ada_w, ada_b):
    B, D = c.shape
    n_out = ada_w.shape[1]
    rows = 8
    c8 = jnp.zeros((rows, D), F32).at[:B].set(c)
    out = pl.pallas_call(
        _mod_kernel,
        grid=(n_out // D,),
        in_specs=[pl.BlockSpec((rows, D), lambda j: (0, 0)),
                  pl.BlockSpec((D, D), lambda j: (0, j)),
                  pl.BlockSpec((1, D), lambda j: (0, j))],
        out_specs=pl.BlockSpec((rows, D), lambda j: (0, j)),
        out_shape=jax.ShapeDtypeStruct((rows, n_out), F32),
        compiler_params=_cparams(("arbitrary",)),
        name="mod",
    )(c8, ada_w, ada_b.reshape(1, n_out))
    return out[:B]


def _inproj_kernel(x_ref, g_ref, sc_ref, sh_ref, w_ref,
                   rq_ref, rk_ref, rv_ref, rg_ref, aqt_ref, ak_ref, avt_ref, iqt_ref, ik_ref, iwt_ref):
    x = x_ref[...]
    ms = jnp.mean(x * x, axis=-1, keepdims=True)
    y = x * lax.rsqrt(ms + EPS) * g_ref[...]
    hb = (y * (1.0 + sc_ref[...]) + sh_ref[...]).astype(BF16)

    def proj(lo, hi):
        return jnp.dot(hb, w_ref[:, lo:hi], preferred_element_type=F32)

    tm = x.shape[0]
    d = DSA_HD
    rq_ref[...] = proj(0, 256).astype(BF16)
    rk_ref[...] = (proj(256, 512) * (RET_DK ** -0.5)).astype(BF16)
    rv_ref[...] = proj(512, 1024).astype(BF16)
    rg_ref[...] = proj(1024, 1536).astype(BF16)
    aqt_ref[...] = (proj(1536, 2048) * (d ** -0.5)).T.astype(BF16)
    kk = proj(2048, 2176)
    pos = pl.program_id(1) * tm + lax.broadcasted_iota(I32, (tm, d), 0)
    col = lax.broadcasted_iota(I32, (tm, d), 1)
    posblk = jnp.where(col == 0, pos // ALIBI_SPLIT, jnp.where(col == 1, pos % ALIBI_SPLIT, 0)).astype(F32)
    for g in range(DSA_KV_HEADS):
        ak_ref[:, g * KAUG:g * KAUG + d] = kk[:, g * d:(g + 1) * d].astype(BF16)
        ak_ref[:, g * KAUG + d:(g + 1) * KAUG] = posblk.astype(BF16)
    vt = proj(2176, 2304).T
    r16 = lax.broadcasted_iota(I32, (VAUG - d, tm), 0)
    onesblk = jnp.where(r16 == 0, 1.0, 0.0).astype(BF16)
    for g in range(DSA_KV_HEADS):
        avt_ref[g * VAUG:g * VAUG + d, :] = vt[g * d:(g + 1) * d, :].astype(BF16)
        avt_ref[g * VAUG + d:(g + 1) * VAUG, :] = onesblk
    iqt_ref[...] = proj(2304, 2816).T.astype(BF16)
    last = proj(2816, 2944)
    ik_ref[...] = last[:, :IDX_HD].astype(BF16)
    iwt_ref[...] = last.T[IDX_HD:IDX_HD + IDX_HEADS, :] * ((IDX_HD ** -0.5) * (IDX_HEADS ** -0.5))


def _inproj(x, norm_g, sc, sh, w_pad, tm):
    B, S, D = x.shape
    row = lambda w: pl.BlockSpec((None, tm, w), lambda b, i: (b, i, 0))
    colT = lambda h: pl.BlockSpec((None, h, tm), lambda b, i: (b, 0, i))
    vec = pl.BlockSpec((None, 1, D), lambda b, i: (b, 0, 0))
    sd = lambda shape, dt: jax.ShapeDtypeStruct(shape, dt)
    G = DSA_KV_HEADS
    return pl.pallas_call(
        _inproj_kernel,
        grid=(B, S // tm),
        in_specs=[row(D), pl.BlockSpec((1, D), lambda b, i: (0, 0)), vec, vec,
                  pl.BlockSpec((D, IN_COLS_PAD), lambda b, i: (0, 0))],
        out_specs=[row(256), row(256), row(512), row(512), colT(DSA_W), row(G * KAUG), colT(G * VAUG),
                   colT(IDX_HEADS * IDX_HD), row(IDX_HD), colT(IDX_HEADS)],
        out_shape=[sd((B, S, 256), BF16), sd((B, S, 256), BF16), sd((B, S, 512), BF16),
                   sd((B, S, 512), BF16), sd((B, DSA_W, S), BF16), sd((B, S, G * KAUG), BF16),
                   sd((B, G * VAUG, S), BF16), sd((B, IDX_HEADS * IDX_HD, S), BF16),
                   sd((B, S, IDX_HD), BF16), sd((B, IDX_HEADS, S), F32)],
        compiler_params=_cparams(("parallel", "parallel")),
        name="inproj",
    )(x, norm_g.reshape(1, D), sc, sh, w_pad)


def _ret_kernel(rq_ref, rk_ref, rv_ref, rg_ref, din_ref, qd_ref, kd_ref, cd_ref, ms_ref, o_ref, state_ref):
    @pl.when(pl.program_id(1) == 0)
    def _():
        state_ref[...] = jnp.zeros_like(state_ref)

    for h in range(RET_HEADS):
        q = rq_ref[:, h * RET_DK:(h + 1) * RET_DK]
        k = rk_ref[:, h * RET_DK:(h + 1) * RET_DK]
        v = rv_ref[:, h * RET_DV:(h + 1) * RET_DV]
        r_prev = state_ref[h]
        s = lax.dot_general(q, k, (((1,), (1,)), ((), ())), preferred_element_type=F32) * din_ref[h]
        o = jnp.dot(s.astype(BF16), v, preferred_element_type=F32)
        o = o + jnp.dot(q, r_prev.astype(BF16), preferred_element_type=F32) * qd_ref[h]
        vd = (v.astype(F32) * kd_ref[h]).astype(BF16)
        kv = lax.dot_general(k, vd, (((0,), (0,)), ((), ())), preferred_element_type=F32)
        state_ref[h] = r_prev * cd_ref[h] + kv
        o = o * lax.rsqrt(jnp.mean(o * o, axis=-1, keepdims=True) + EPS)
        g = rg_ref[:, h * RET_DV:(h + 1) * RET_DV].astype(F32)
        gate = g * (1.0 / (1.0 + jnp.exp(-g)))
        o_ref[:, h * RET_DV:(h + 1) * RET_DV] = (gate * o * ms_ref[:, h * RET_DV:(h + 1) * RET_DV]).astype(BF16)


def _ret_consts(C):
    H = RET_HEADS
    log_g = np.log1p(-np.exp2(-5.0 - np.arange(H, dtype=np.float64)))
    pos = np.arange(C, dtype=np.float64)
    diff = pos[:, None] - pos[None, :]
    d_inner = np.where(diff[None] >= 0, np.exp(np.maximum(diff, 0.0)[None] * log_g[:, None, None]), 0.0)
    q_decay = np.exp((pos + 1.0)[None] * log_g[:, None])
    k_decay = np.exp((C - 1.0 - pos)[None] * log_g[:, None])
    chunk_decay = np.exp(C * log_g)
    qd = np.broadcast_to(q_decay[:, :, None], (H, C, RET_DV))
    kd = np.broadcast_to(k_decay[:, :, None], (H, C, RET_DV))
    cd = np.broadcast_to(chunk_decay[:, None, None], (H, 1, RET_DV))
    f = lambda a: jnp.asarray(np.ascontiguousarray(a), F32)
    return f(d_inner), f(qd), f(kd), f(cd)


def _retention(rq, rk, rv, rg, ms_ret):
    B, S, _ = rq.shape
    C = min(RET_CHUNK, S)
    din, qd, kd, cd = _ret_consts(C)
    row = lambda w: pl.BlockSpec((None, C, w), lambda b, n: (b, n, 0))
    full = lambda a: pl.BlockSpec(a.shape, lambda b, n: (0,) * a.ndim)
    return pl.pallas_call(
        _ret_kernel,
        grid=(B, S // C),
        in_specs=[row(256), row(256), row(512), row(512), full(din), full(qd), full(kd), full(cd),
                  pl.BlockSpec((1, RET_W), lambda b, n: (0, 0))],
        out_specs=row(RET_W),
        out_shape=jax.ShapeDtypeStruct((B, S, RET_W), BF16),
        scratch_shapes=[pltpu.VMEM((RET_HEADS, RET_DK, RET_DV), F32)],
        compiler_params=_cparams(("parallel", "arbitrary")),
        name="ret",
    )(rq, rk, rv, rg, din, qd, kd, cd, ms_ret)


def _f32_key(x):
    i = lax.bitcast_convert_type(x, I32)
    return i ^ ((i >> 31) & 0x7FFFFFFF)


def _key_f32(k):
    return lax.bitcast_convert_type(k ^ ((k >> 31) & 0x7FFFFFFF), F32)


def _dsa_kernel(iqt_ref, iwt_ref, aqt_ref, ik_ref, ak_ref, avt_ref, ms_ref, o_ref, score_ref, qa_ref, sa_ref, sb_ref, *acc_refs,
                tq, tks, n_sel):
    H, G, R, d = DSA_HEADS, DSA_KV_HEADS, DSA_HEADS // DSA_KV_HEADS, DSA_HD
    t0 = pl.program_id(1) * tq
    nsub = (t0 + tq) // tks
    kf = float(n_sel)
    qpos = t0 + lax.broadcasted_iota(I32, (1, tq), 1)
    krow = lax.broadcasted_iota(I32, (tks, tq), 0)

    wrow = [iwt_ref[h:h + 1, :] for h in range(IDX_HEADS)]

    def score_sub(j, carry):
        r0 = pl.multiple_of(j * tks, tks)
        kc = ik_ref[pl.ds(r0, tks), :]
        acc = jnp.zeros((tks, tq), F32)
        for h in range(IDX_HEADS):
            rel = jnp.dot(kc, iqt_ref[h * IDX_HD:(h + 1) * IDX_HD, :], preferred_element_type=F32)
            acc = acc + jnp.maximum(rel, 0.0) * wrow[h]
        score_ref[pl.ds(r0, tks), :] = jnp.where(r0 + krow <= qpos, acc, -jnp.inf)
        return carry

    lax.fori_loop(0, nsub, score_sub, 0)

    def fold8(x, op):
        acc = x[0:8, :]
        for i in range(1, tks // 8):
            acc = op(acc, x[8 * i:8 * (i + 1), :])
        return acc

    def count(th, strict):
        def body(j, acc):
            s = score_ref[pl.ds(pl.multiple_of(j * tks, tks), tks), :]
            hit = (s > th) if strict else (s >= th)
            return acc + fold8(jnp.where(hit, 1.0, 0.0), jnp.add)

        acc = lax.fori_loop(0, nsub, body, jnp.zeros((8, tq), F32))
        return jnp.sum(acc, axis=0, keepdims=True)

    def minmax(j, carry):
        mx, mn = carry
        s = score_ref[pl.ds(pl.multiple_of(j * tks, tks), tks), :]
        mx = jnp.maximum(mx, fold8(s, jnp.maximum))
        mn = jnp.minimum(mn, fold8(jnp.where(s == -jnp.inf, jnp.inf, s), jnp.minimum))
        return mx, mn

    mx, mn = lax.fori_loop(0, nsub, minmax, (jnp.full((8, tq), -jnp.inf, F32), jnp.full((8, tq), jnp.inf, F32)))
    hi0 = jnp.max(mx, axis=0, keepdims=True)
    lo0 = jnp.min(mn, axis=0, keepdims=True)

    def midpoint(lo, hi, it):
        lk, hk = _f32_key(lo), _f32_key(hi)
        mk = (lk >> 1) + (hk >> 1) + (lk & hk & 1)
        mv = lo + (hi - lo) * 0.5
        early = (jnp.zeros((1, tq), I32) + it) < BISECT_VALUE_STEPS
        mid = jnp.where(early & (mv > lo) & (mv < hi), mv, _key_f32(mk))
        return mid, jnp.max(jnp.where(mk != lk, 1.0, 0.0))

    def bis_cond(c):
        return (c[2] > 0.0) & (c[3] < BISECT_MAX_STEPS)

    def bis_body(c):
        lo, hi, _, it = c
        mid, _ = midpoint(lo, hi, it)
        cnt = count(mid, False)
        ge = cnt >= kf
        nlo = jnp.where(ge, mid, lo)
        nhi = jnp.where(cnt == kf, mid, jnp.where(ge, hi, mid))
        _, active = midpoint(nlo, nhi, it + 1)
        return nlo, nhi, active, it + 1

    zero = jnp.zeros((1, tq), F32)
    n_pos = count(zero, True)
    keep_all = qpos + 1 <= n_sel
    settled = keep_all | ((count(zero, False) >= kf) & (n_pos < kf))
    lo0 = jnp.where(settled | (n_pos >= kf), zero, lo0)
    hi0 = jnp.where(settled | (n_pos < kf), zero, hi0)
    _, active0 = midpoint(lo0, hi0, jnp.int32(0))
    lo, hi, _, _ = lax.while_loop(bis_cond, bis_body, (lo0, hi0, active0, jnp.int32(0)))
    thr = jnp.where(count(hi, False) >= kf, hi, lo)
    thr = jnp.where(keep_all, F32_LOWEST, thr)
    excess = count(thr, False) - kf

    @pl.when(jnp.max(excess) > 0.0)
    def _():
        budget = kf - count(thr, True)
        earlier = lax.broadcasted_iota(I32, (tks, tks), 1) < lax.broadcasted_iota(I32, (tks, tks), 0)
        earlier = jnp.where(earlier, 1.0, 0.0).astype(BF16)

        def fix(j, seen):
            r0 = pl.multiple_of(j * tks, tks)
            s = score_ref[pl.ds(r0, tks), :]
            eq = s == thr
            eqf = jnp.where(eq, 1.0, 0.0)
            rank = jnp.dot(earlier, eqf.astype(BF16), preferred_element_type=F32) + seen
            score_ref[pl.ds(r0, tks), :] = jnp.where(eq & (rank >= budget), -jnp.inf, s)
            return seen + jnp.sum(eqf, axis=0, keepdims=True)

        lax.fori_loop(0, nsub, fix, jnp.zeros((1, tq), F32))

    arow = lax.broadcasted_iota(I32, (KAUG - d, tq), 0)
    for h in range(H):
        slope = float(2.0 ** (-8.0 * (h + 1) / H))
        qa_ref[h, 0:d, :] = aqt_ref[h * d:(h + 1) * d, :]
        qa_ref[h, d:KAUG, :] = jnp.where(arow == 0, slope * ALIBI_SPLIT, jnp.where(arow == 1, slope, 0.0)).astype(BF16)
    for acc in acc_refs:
        acc[...] = jnp.zeros_like(acc)

    def logits(j, h):
        ka = ak_ref[pl.ds(pl.multiple_of(j * tks, tks), tks), (h // R) * KAUG:(h // R + 1) * KAUG]
        return jnp.dot(ka, qa_ref[h], preferred_element_type=F32)

    def step(j, j_next, cur_ref, next_ref, ms):
        r0 = pl.multiple_of(j * tks, tks)
        sel = score_ref[pl.ds(r0, tks), :] >= thr
        nms = []
        for h in range(H):
            g = h // R
            next_ref[h] = logits(j_next, h)
            s = jnp.where(sel, cur_ref[h], NEG_BIG)
            m_new = jnp.maximum(ms[h], jnp.max(s, axis=0, keepdims=True))
            p = jnp.exp(s - m_new).astype(BF16)
            va = avt_ref[g * VAUG:(g + 1) * VAUG, pl.ds(r0, tks)]
            acc = acc_refs[h]
            acc[...] = acc[...] * jnp.exp(ms[h] - m_new) + jnp.dot(va, p, preferred_element_type=F32)
            nms.append(m_new)
        return tuple(nms)

    for h in range(H):
        sa_ref[h] = logits(0, h)

    def att_pair(i, ms):
        ms = step(2 * i, 2 * i + 1, sa_ref, sb_ref, ms)
        return step(2 * i + 1, jnp.minimum(2 * i + 2, nsub - 1), sb_ref, sa_ref, ms)

    lax.fori_loop(0, nsub // 2, att_pair, tuple(jnp.full((1, tq), NEG_BIG, F32) for _ in range(H)))
    for h in range(H):
        a = acc_refs[h][...]
        o = a[0:d, :] / a[d:d + 1, :]
        o_ref[h * d:(h + 1) * d, :] = (o * ms_ref[h * d:(h + 1) * d, :]).astype(BF16)


def _dsa(iqt, iwt, aqt, ik, ak, avt, ms_att, tq, tks):
    B, _, S = iqt.shape
    assert S <= ALIBI_SPLIT * 256 and tq % (2 * tks) == 0
    n_sel = min(TOPK_MAX, S // 4)
    G = DSA_KV_HEADS
    colT = lambda h: pl.BlockSpec((None, h, tq), lambda b, i: (b, 0, i))
    msb = jnp.broadcast_to(ms_att.reshape(DSA_W, 1), (DSA_W, tq))
    return pl.pallas_call(
        functools.partial(_dsa_kernel, tq=tq, tks=tks, n_sel=n_sel),
        grid=(B, S // tq),
        in_specs=[colT(IDX_HEADS * IDX_HD), colT(IDX_HEADS), colT(DSA_W),
                  pl.BlockSpec((None, S, IDX_HD), lambda b, i: (b, 0, 0)),
                  pl.BlockSpec((None, S, G * KAUG), lambda b, i: (b, 0, 0)),
                  pl.BlockSpec((None, G * VAUG, S), lambda b, i: (b, 0, 0)),
                  pl.BlockSpec((DSA_W, tq), lambda b, i: (0, 0))],
        out_specs=colT(DSA_W),
        out_shape=jax.ShapeDtypeStruct((B, DSA_W, S), BF16),
        scratch_shapes=[pltpu.VMEM((S, tq), F32), pltpu.VMEM((DSA_HEADS, KAUG, tq), BF16),
                        pltpu.VMEM((DSA_HEADS, tks, tq), F32), pltpu.VMEM((DSA_HEADS, tks, tq), F32)]
        + [pltpu.VMEM((VAUG, tq), F32) for _ in range(DSA_HEADS)],
        compiler_params=_cparams(("parallel", "arbitrary")),
        name="dsa",
    )(iqt, iwt, aqt, ik, ak, avt, msb)


def _oproj_kernel(ret_ref, att_ref, x_ref, wo_ref, g1_ref, n2_ref, sc_ref, sh_ref, rw_ref, rb_ref,
                  x1_ref, h2_ref, sel_ref, idx_ref, gate_ref, cnt_ref):
    mixo = jnp.dot(ret_ref[...], wo_ref[:RET_W, :], preferred_element_type=F32)
    mixo = mixo + lax.dot_general(att_ref[...], wo_ref[RET_W:, :], (((0,), (0,)), ((), ())),
                                  preferred_element_type=F32)
    x1 = x_ref[...] + g1_ref[...] * mixo
    x1_ref[...] = x1
    y = x1 * lax.rsqrt(jnp.mean(x1 * x1, axis=-1, keepdims=True) + EPS) * n2_ref[...]
    h2 = y * (1.0 + sc_ref[...]) + sh_ref[...]
    h2_ref[...] = h2
    logits = jnp.dot(h2, rw_ref[...], preferred_element_type=F32,
                     precision=lax.Precision.HIGHEST) + rb_ref[...]
    tm = logits.shape[0]
    lane = lax.broadcasted_iota(I32, (tm, LANES), 1).astype(F32)
    work = jnp.where(lane < N_EXPERTS, logits, -jnp.inf)
    sel = jnp.zeros((tm, LANES), F32)
    idx_tab = jnp.zeros((tm, LANES), F32)
    vals = []
    for k in range(TOP_K):
        m = jnp.max(work, axis=1, keepdims=True)
        idx = jnp.min(jnp.where(work == m, lane, float(LANES)), axis=1, keepdims=True)
        hit = lane == idx
        sel = jnp.where(hit, 1.0, sel)
        idx_tab = jnp.where(lane == k, idx, idx_tab)
        work = jnp.where(hit, -jnp.inf, work)
        vals.append(m)
    es = [jnp.exp(v - vals[0]) for v in vals]
    den = es[0] + es[1] + es[2] + es[3]
    gate_tab = jnp.zeros((tm, LANES), F32)
    for k in range(TOP_K):
        gate_tab = jnp.where(lane == k, es[k] / den, gate_tab)
    sel_ref[...] = sel
    idx_ref[...] = idx_tab
    gate_ref[...] = gate_tab

    @pl.when((pl.program_id(0) == 0) & (pl.program_id(1) == 0))
    def _():
        cnt_ref[...] = jnp.zeros_like(cnt_ref)

    cnt_ref[...] += jnp.sum(sel, axis=0, keepdims=True)


def _oproj(ret, att, x, wo, g1, n2g, sc2, sh2, rw_pad, rb_pad, tm):
    B, S, D = x.shape
    nt = S // tm
    row = lambda w: pl.BlockSpec((None, tm, w), lambda b, i: (b, i, 0))
    flat = lambda w: pl.BlockSpec((tm, w), lambda b, i: (b * nt + i, 0))
    vec = pl.BlockSpec((None, 1, D), lambda b, i: (b, 0, 0))
    cst = lambda shape: pl.BlockSpec(shape, lambda b, i: (0, 0))
    sd = lambda shape, dt: jax.ShapeDtypeStruct(shape, dt)
    N = B * S
    return pl.pallas_call(
        _oproj_kernel,
        grid=(B, nt),
        in_specs=[row(RET_W), pl.BlockSpec((None, DSA_W, tm), lambda b, i: (b, 0, i)), row(D), cst((D, D)), vec,
                  cst((1, D)), vec, vec,
                  cst((D, LANES)), cst((1, LANES))],
        out_specs=[flat(D), flat(D), flat(LANES), flat(LANES), flat(LANES), cst((1, LANES))],
        out_shape=[sd((N, D), F32), sd((N, D), F32), sd((N, LANES), F32), sd((N, LANES), F32),
                   sd((N, LANES), F32), sd((1, LANES), F32)],
        compiler_params=_cparams(("arbitrary", "arbitrary")),
        name="oproj",
    )(ret, att, x, wo, g1, n2g.reshape(1, D), sc2, sh2, rw_pad, rb_pad)


def _dest_kernel(sel_ref, idx_ref, pstart_ref, dest_ref, seen_ref):
    @pl.when(pl.program_id(0) == 0)
    def _():
        seen_ref[...] = jnp.zeros_like(seen_ref)

    sel = sel_ref[...]
    tm = sel.shape[0]
    earlier = lax.broadcasted_iota(I32, (tm, tm), 1) < lax.broadcasted_iota(I32, (tm, tm), 0)
    earlier = jnp.where(earlier, 1.0, 0.0).astype(BF16)
    rank = jnp.dot(earlier, sel.astype(BF16), preferred_element_type=F32) + seen_ref[...]
    dest = pstart_ref[...] + rank
    lane = lax.broadcasted_iota(I32, (tm, LANES), 1).astype(F32)
    idx_tab = idx_ref[...]
    out = jnp.zeros((tm, LANES), F32)
    for k in range(TOP_K):
        e_k = jnp.sum(jnp.where(lane == k, idx_tab, 0.0), axis=1, keepdims=True)
        d_k = jnp.sum(jnp.where(lane == e_k, dest, 0.0), axis=1, keepdims=True)
        out = jnp.where(lane == k, d_k, out)
    dest_ref[...] = out.astype(I32)
    seen_ref[...] += jnp.sum(sel, axis=0, keepdims=True)


def _dest(sel, idx_tab, pstart, tm):
    N = sel.shape[0]
    blk = pl.BlockSpec((tm, LANES), lambda i: (i, 0))
    return pl.pallas_call(
        _dest_kernel,
        grid=(N // tm,),
        in_specs=[blk, blk, pl.BlockSpec((1, LANES), lambda i: (0, 0))],
        out_specs=blk,
        out_shape=jax.ShapeDtypeStruct((N, LANES), I32),
        scratch_shapes=[pltpu.VMEM((1, LANES), F32)],
        compiler_params=_cparams(("arbitrary",)),
        name="dest",
    )(sel, idx_tab, pstart)


def _disp_kernel(h2_ref, dest_hbm, xs_in, xs_hbm, dsm, sem_idx, sem_row, *, tm):
    del xs_in
    i = pl.program_id(0)
    n = tm * TOP_K
    cp = pltpu.make_async_copy(dest_hbm.at[pl.ds(i * n, n)], dsm, sem_idx)
    cp.start()
    cp.wait()

    def row_copy(t, dst):
        return pltpu.make_async_copy(h2_ref.at[pl.ds(t, 1)], xs_hbm.at[pl.ds(dst, 1)], sem_row)

    def issue(t, c):
        for k in range(TOP_K):
            row_copy(t, dsm[t * TOP_K + k]).start()
        return c

    lax.fori_loop(0, tm, issue, 0)

    def drain(t, c):
        for k in range(TOP_K):
            row_copy(0, 0).wait()
        return c

    lax.fori_loop(0, tm, drain, 0)


def _dispatch(h2, dest_flat, n_rows, tm):
    N, D = h2.shape
    xs0 = jnp.zeros((n_rows, D), F32)
    return pl.pallas_call(
        functools.partial(_disp_kernel, tm=tm),
        grid=(N // tm,),
        in_specs=[pl.BlockSpec((tm, D), lambda i: (i, 0)),
                  pl.BlockSpec(memory_space=pl.ANY),
                  pl.BlockSpec(memory_space=pl.ANY)],
        out_specs=pl.BlockSpec(memory_space=pl.ANY),
        out_shape=jax.ShapeDtypeStruct((n_rows, D), F32),
        scratch_shapes=[pltpu.SMEM((tm * TOP_K,), I32), pltpu.SemaphoreType.DMA, pltpu.SemaphoreType.DMA],
        input_output_aliases={2: 0},
        compiler_params=_cparams(("arbitrary",)),
        name="disp",
    )(h2, dest_flat, xs0)


def _ffn_kernel(be_ref, nb_ref, xs_ref, wgu_ref, bgu_ref, wd_ref, bd_ref, ys_ref):
    @pl.when(pl.program_id(0) < nb_ref[0])
    def _():
        xb = xs_ref[...].astype(BF16)
        gu = jnp.dot(xb, wgu_ref[...], preferred_element_type=F32) + bgu_ref[...]
        gate = jnp.minimum(gu[:, :D_EXPERT], SWIGLU_LIMIT)
        up = jnp.clip(gu[:, D_EXPERT:], -SWIGLU_LIMIT, SWIGLU_LIMIT)
        glu = gate * (1.0 / (1.0 + jnp.exp(-SWIGLU_ALPHA * gate)))
        act = ((up + 1.0) * glu).astype(BF16)
        ys_ref[...] = jnp.dot(act, wd_ref[...], preferred_element_type=F32) + bd_ref[...]

    @pl.when(pl.program_id(0) >= nb_ref[0])
    def _():
        ys_ref[...] = jnp.zeros_like(ys_ref)


def _ffn(xs, block_e, n_blocks, w_gu, b_gu, w_down, b_down, tmb):
    P, D = xs.shape
    E = w_gu.shape[0]
    blk = lambda i, be, nb: (jnp.minimum(i, nb[0] - 1), 0)
    wsel = lambda i, be, nb: (be[jnp.minimum(i, nb[0] - 1)], 0, 0)
    grid_spec = pltpu.PrefetchScalarGridSpec(
        num_scalar_prefetch=2,
        grid=(P // tmb,),
        in_specs=[pl.BlockSpec((tmb, D), blk),
                  pl.BlockSpec((None, D, 2 * D_EXPERT), wsel),
                  pl.BlockSpec((None, 1, 2 * D_EXPERT), wsel),
                  pl.BlockSpec((None, D_EXPERT, D), wsel),
                  pl.BlockSpec((None, 1, D), wsel)],
        out_specs=pl.BlockSpec((tmb, D), lambda i, be, nb: (i, 0)),
    )
    return pl.pallas_call(
        _ffn_kernel,
        grid_spec=grid_spec,
        out_shape=jax.ShapeDtypeStruct((P, D), F32),
        compiler_params=_cparams(("arbitrary",)),
        name="ffn",
    )(block_e, n_blocks, xs, w_gu, b_gu.reshape(E, 1, 2 * D_EXPERT), w_down, b_down.reshape(E, 1, D))


def _comb_kernel(x1_ref, gate_ref, g2_ref, fg_ref, dest_hbm, ys_hbm, o_ref, buf, dsm, sem_idx, sem_row, *, tm):
    i = pl.program_id(0)
    n = tm * TOP_K
    cp = pltpu.make_async_copy(dest_hbm.at[pl.ds(i * n, n)], dsm, sem_idx)
    cp.start()
    cp.wait()

    def row_copy(t, k, src):
        return pltpu.make_async_copy(ys_hbm.at[pl.ds(src, 1)], buf.at[k, pl.ds(t, 1)], sem_row)

    def issue(t, c):
        for k in range(TOP_K):
            row_copy(t, k, dsm[t * TOP_K + k]).start()
        return c

    lax.fori_loop(0, tm, issue, 0)

    def drain(t, c):
        for k in range(TOP_K):
            row_copy(0, k, 0).wait()
        return c

    lax.fori_loop(0, tm, drain, 0)

    gates = gate_ref[...]
    y = jnp.zeros(x1_ref.shape, F32)
    for k in range(TOP_K):
        y = y + gates[:, k:k + 1] * buf[k]
    v = x1_ref[...] + g2_ref[...] * y
    o_ref[...] = v * lax.rsqrt(jnp.mean(v * v, axis=-1, keepdims=True) + EPS) * fg_ref[...]


def _combine(x1, gate_tab, g2, final_g, dest_flat, ys, S, tm):
    N, D = x1.shape
    per_b = S // tm
    return pl.pallas_call(
        functools.partial(_comb_kernel, tm=tm),
        grid=(N // tm,),
        in_specs=[pl.BlockSpec((tm, D), lambda i: (i, 0)),
                  pl.BlockSpec((tm, LANES), lambda i: (i, 0)),
                  pl.BlockSpec((None, 1, D), lambda i: (i // per_b, 0, 0)),
                  pl.BlockSpec((1, D), lambda i: (0, 0)),
                  pl.BlockSpec(memory_space=pl.ANY),
                  pl.BlockSpec(memory_space=pl.ANY)],
        out_specs=pl.BlockSpec((tm, D), lambda i: (i, 0)),
        out_shape=jax.ShapeDtypeStruct((N, D), F32),
        scratch_shapes=[pltpu.VMEM((TOP_K, tm, D), F32), pltpu.SMEM((tm * TOP_K,), I32),
                        pltpu.SemaphoreType.DMA, pltpu.SemaphoreType.DMA],
        compiler_params=_cparams(("arbitrary",)),
        name="comb",
    )(x1, gate_tab, g2, final_g.reshape(1, D), dest_flat, ys)


def _tile(n, pref):
    t = min(pref, n)
    assert n % t == 0, (n, t)
    return t


def _layer(x, c, ada_w, ada_b, norm1_g, w_in, mix_scale, w_o, norm2_g,
           router_w, router_b, w_gu, b_gu, w_down, b_down, final_g):
    B, S, D = x.shape
    N = B * S
    mod = _mod(c, ada_w, ada_b).reshape(B, 6, 1, D)
    sh1, sc1, g1, sh2, sc2, g2 = (mod[:, j] for j in range(6))

    w_pad = jnp.pad(w_in, ((0, 0), (0, IN_COLS_PAD - IN_COLS))).astype(BF16)
    rq, rk, rv, rg, aqt, ak, avt, iqt, ik, iwt = _inproj(x, norm1_g, sc1, sh1, w_pad, _tile(S, 512))
    ms = mix_scale.reshape(1, RET_W + DSA_W)
    ret = _retention(rq, rk, rv, rg, ms[:, :RET_W])
    att = _dsa(iqt, iwt, aqt, ik, ak, avt, ms[:, RET_W:], _tile(S, 256), 128)

    rw_pad = jnp.pad(router_w, ((0, 0), (0, LANES - N_EXPERTS)))
    rb_pad = jnp.pad(router_b, (0, LANES - N_EXPERTS)).reshape(1, LANES)
    x1, h2, sel, idx_tab, gate_tab, counts = _oproj(ret, att, x, w_o.astype(BF16), g1, norm2_g, sc2, sh2,
                                                    rw_pad, rb_pad, _tile(S, 512))

    tmb = 512
    n_rows = (N * TOP_K + N_EXPERTS * (tmb - 1)) // tmb * tmb
    cnt = counts[0, :N_EXPERTS].astype(I32)
    padded = (cnt + tmb - 1) // tmb * tmb
    ends = jnp.cumsum(padded)
    pstart = jnp.pad((ends - padded).astype(F32), (0, LANES - N_EXPERTS)).reshape(1, LANES)
    n_blocks = (ends[-1] // tmb).reshape(1)
    block_e = jnp.clip(jnp.searchsorted(ends, jnp.arange(n_rows // tmb, dtype=I32) * tmb, side='right'),
                       0, N_EXPERTS - 1).astype(I32)

    tmd = _tile(N, 256)
    dest_tab = _dest(sel, idx_tab, pstart, tmd)
    dest_flat = dest_tab[:, :TOP_K].reshape(N * TOP_K)
    xs = _dispatch(h2, dest_flat, n_rows, tmd)
    ys = _ffn(xs, block_e, n_blocks, w_gu.astype(BF16), b_gu, w_down.astype(BF16), b_down, tmb)
    out = _combine(x1, gate_tab, g2, final_g, dest_flat, ys, S, _tile(S, 256))
    return out.reshape(B, S, D)


def kernel(x, c, ada_w, ada_b, norm1_g, w_in, mix_scale, w_o, norm2_g, router_w, router_b, w_gu, b_gu,
           w_down, b_down, final_g):
    assert ada_w.shape[0] == 1, "single-layer stack"
    return _layer(x, c, ada_w[0], ada_b[0], norm1_g[0], w_in[0], mix_scale[0], w_o[0], norm2_g[0],
                  router_w[0], router_b[0], w_gu[0], b_gu[0], w_down[0], b_down[0], final_g)
```

```python
import functools

import numpy as np
import jax
import jax.numpy as jnp
from jax import lax
from jax.experimental import pallas as pl
from jax.experimental.pallas import tpu as pltpu

F32 = jnp.float32
BF16 = jnp.bfloat16
I32 = jnp.int32

D_MODEL = 1024
RET_HEADS = 4
RET_DK = 64
RET_DV = 128
RET_CHUNK = 128
DSA_HEADS = 8
DSA_KV_HEADS = 2
DSA_HD = 64
IDX_HEADS = 8
IDX_HD = 64
TOPK_MAX = 256
N_EXPERTS = 32
TOP_K = 4
D_EXPERT = D_MODEL
SWIGLU_LIMIT = 7.0
SWIGLU_ALPHA = 1.702
EPS = 1e-6

RET_W = RET_HEADS * RET_DV
DSA_W = DSA_HEADS * DSA_HD
IN_COLS = 2888
IN_COLS_PAD = 2944

KAUG = 128
VAUG = 80
ALIBI_SPLIT = 64
BISECT_VALUE_STEPS = 8
BISECT_MAX_STEPS = 64

LANES = 128
VMEM_LIMIT = 56 * 1024 * 1024
NEG_BIG = -1e30
F32_LOWEST = float(np.finfo(np.float32).min)


def _cparams(sem):
    return pltpu.CompilerParams(dimension_semantics=sem, vmem_limit_bytes=VMEM_LIMIT)


def _mod_kernel(c_ref, w_ref, b_ref, o_ref):
    c = c_ref[...]
    s = c * (1.0 / (1.0 + jnp.exp(-c)))
    o_ref[...] = jnp.dot(s, w_ref[...], preferred_element_type=F32,
                         precision=lax.Precision.HIGHEST) + b_ref[...]


def _mod(c, ada_w, ada_b):
    B, D = c.shape
    n_out = ada_w.shape[1]
    rows = 8
    c8 = jnp.zeros((rows, D), F32).at[:B].set(c)
    out = pl.pallas_call(
        _mod_kernel,
        grid=(n_out // D,),
        in_specs=[pl.BlockSpec((rows, D), lambda j: (0, 0)),
                  pl.BlockSpec((D, D), lambda j: (0, j)),
                  pl.BlockSpec((1, D), lambda j: (0, j))],
        out_specs=pl.BlockSpec((rows, D), lambda j: (0, j)),
        out_shape=jax.ShapeDtypeStruct((rows, n_out), F32),
        compiler_params=_cparams(("arbitrary",)),
        name="mod",
    )(c8, ada_w, ada_b.reshape(1, n_out))
    return out[:B]


def _inproj_kernel(x_ref, g_ref, sc_ref, sh_ref, w_ref,
                   rq_ref, rk_ref, rv_ref, rg_ref, aqt_ref, ak_ref, avt_ref, iqt_ref, ik_ref, iwt_ref):
    x = x_ref[...]
    ms = jnp.mean(x * x, axis=-1, keepdims=True)
    y = x * lax.rsqrt(ms + EPS) * g_ref[...]
    hb = (y * (1.0 + sc_ref[...]) + sh_ref[...]).astype(BF16)

    def proj(lo, hi):
        return jnp.dot(hb, w_ref[:, lo:hi], preferred_element_type=F32)

    tm = x.shape[0]
    d = DSA_HD
    rq_ref[...] = proj(0, 256).astype(BF16)
    rk_ref[...] = (proj(256, 512) * (RET_DK ** -0.5)).astype(BF16)
    rv_ref[...] = proj(512, 1024).astype(BF16)
    rg_ref[...] = proj(1024, 1536).astype(BF16)
    aqt_ref[...] = (proj(1536, 2048) * (d ** -0.5)).T.astype(BF16)
    kk = proj(2048, 2176)
    pos = pl.program_id(1) * tm + lax.broadcasted_iota(I32, (tm, d), 0)
    col = lax.broadcasted_iota(I32, (tm, d), 1)
    posblk = jnp.where(col == 0, pos // ALIBI_SPLIT, jnp.where(col == 1, pos % ALIBI_SPLIT, 0)).astype(F32)
    for g in range(DSA_KV_HEADS):
        ak_ref[:, g * KAUG:g * KAUG + d] = kk[:, g * d:(g + 1) * d].astype(BF16)
        ak_ref[:, g * KAUG + d:(g + 1) * KAUG] = posblk.astype(BF16)
    vt = proj(2176, 2304).T
    r16 = lax.broadcasted_iota(I32, (VAUG - d, tm), 0)
    onesblk = jnp.where(r16 == 0, 1.0, 0.0).astype(BF16)
    for g in range(DSA_KV_HEADS):
        avt_ref[g * VAUG:g * VAUG + d, :] = vt[g * d:(g + 1) * d, :].astype(BF16)
        avt_ref[g * VAUG + d:(g + 1) * VAUG, :] = onesblk
    iqt_ref[...] = proj(2304, 2816).T.astype(BF16)
    last = proj(2816, 2944)
    ik_ref[...] = last[:, :IDX_HD].astype(BF16)
    iwt_ref[...] = last.T[IDX_HD:IDX_HD + IDX_HEADS, :] * ((IDX_HD ** -0.5) * (IDX_HEADS ** -0.5))


def _inproj(x, norm_g, sc, sh, w_pad, tm):
    B, S, D = x.shape
    row = lambda w: pl.BlockSpec((None, tm, w), lambda b, i: (b, i, 0))
    colT = lambda h: pl.BlockSpec((None, h, tm), lambda b, i: (b, 0, i))
    vec = pl.BlockSpec((None, 1, D), lambda b, i: (b, 0, 0))
    sd = lambda shape, dt: jax.ShapeDtypeStruct(shape, dt)
    G = DSA_KV_HEADS
    return pl.pallas_call(
        _inproj_kernel,
        grid=(B, S // tm),
        in_specs=[row(D), pl.BlockSpec((1, D), lambda b, i: (0, 0)), vec, vec,
                  pl.BlockSpec((D, IN_COLS_PAD), lambda b, i: (0, 0))],
        out_specs=[row(256), row(256), row(512), row(512), colT(DSA_W), row(G * KAUG), colT(G * VAUG),
                   colT(IDX_HEADS * IDX_HD), row(IDX_HD), colT(IDX_HEADS)],
        out_shape=[sd((B, S, 256), BF16), sd((B, S, 256), BF16), sd((B, S, 512), BF16),
                   sd((B, S, 512), BF16), sd((B, DSA_W, S), BF16), sd((B, S, G * KAUG), BF16),
                   sd((B, G * VAUG, S), BF16), sd((B, IDX_HEADS * IDX_HD, S), BF16),
                   sd((B, S, IDX_HD), BF16), sd((B, IDX_HEADS, S), F32)],
        compiler_params=_cparams(("parallel", "parallel")),
        name="inproj",
    )(x, norm_g.reshape(1, D), sc, sh, w_pad)


def _ret_kernel(rq_ref, rk_ref, rv_ref, rg_ref, din_ref, qd_ref, kd_ref, cd_ref, ms_ref, o_ref, state_ref):
    @pl.when(pl.program_id(1) == 0)
    def _():
        state_ref[...] = jnp.zeros_like(state_ref)

    for h in range(RET_HEADS):
        q = rq_ref[:, h * RET_DK:(h + 1) * RET_DK]
        k = rk_ref[:, h * RET_DK:(h + 1) * RET_DK]
        v = rv_ref[:, h * RET_DV:(h + 1) * RET_DV]
        r_prev = state_ref[h]
        s = lax.dot_general(q, k, (((1,), (1,)), ((), ())), preferred_element_type=F32) * din_ref[h]
        o = jnp.dot(s.astype(BF16), v, preferred_element_type=F32)
        o = o + jnp.dot(q, r_prev.astype(BF16), preferred_element_type=F32) * qd_ref[h]
        vd = (v.astype(F32) * kd_ref[h]).astype(BF16)
        kv = lax.dot_general(k, vd, (((0,), (0,)), ((), ())), preferred_element_type=F32)
        state_ref[h] = r_prev * cd_ref[h] + kv
        o = o * lax.rsqrt(jnp.mean(o * o, axis=-1, keepdims=True) + EPS)
        g = rg_ref[:, h * RET_DV:(h + 1) * RET_DV].astype(F32)
        gate = g * (1.0 / (1.0 + jnp.exp(-g)))
        o_ref[:, h * RET_DV:(h + 1) * RET_DV] = (gate * o * ms_ref[:, h * RET_DV:(h + 1) * RET_DV]).astype(BF16)


def _ret_consts(C):
    H = RET_HEADS
    log_g = np.log1p(-np.exp2(-5.0 - np.arange(H, dtype=np.float64)))
    pos = np.arange(C, dtype=np.float64)
    diff = pos[:, None] - pos[None, :]
    d_inner = np.where(diff[None] >= 0, np.exp(np.maximum(diff, 0.0)[None] * log_g[:, None, None]), 0.0)
    q_decay = np.exp((pos + 1.0)[None] * log_g[:, None])
    k_decay = np.exp((C - 1.0 - pos)[None] * log_g[:, None])
    chunk_decay = np.exp(C * log_g)
    qd = np.broadcast_to(q_decay[:, :, None], (H, C, RET_DV))
    kd = np.broadcast_to(k_decay[:, :, None], (H, C, RET_DV))
    cd = np.broadcast_to(chunk_decay[:, None, None], (H, 1, RET_DV))
    f = lambda a: jnp.asarray(np.ascontiguousarray(a), F32)
    return f(d_inner), f(qd), f(kd), f(cd)


def _retention(rq, rk, rv, rg, ms_ret):
    B, S, _ = rq.shape
    C = min(RET_CHUNK, S)
    din, qd, kd, cd = _ret_consts(C)
    row = lambda w: pl.BlockSpec((None, C, w), lambda b, n: (b, n, 0))
    full = lambda a: pl.BlockSpec(a.shape, lambda b, n: (0,) * a.ndim)
    return pl.pallas_call(
        _ret_kernel,
        grid=(B, S // C),
        in_specs=[row(256), row(256), row(512), row(512), full(din), full(qd), full(kd), full(cd),
                  pl.BlockSpec((1, RET_W), lambda b, n: (0, 0))],
        out_specs=row(RET_W),
        out_shape=jax.ShapeDtypeStruct((B, S, RET_W), BF16),
        scratch_shapes=[pltpu.VMEM((RET_HEADS, RET_DK, RET_DV), F32)],
        compiler_params=_cparams(("parallel", "arbitrary")),
        name="ret",
    )(rq, rk, rv, rg, din, qd, kd, cd, ms_ret)


def _f32_key(x):
    i = lax.bitcast_convert_type(x, I32)
    return i ^ ((i >> 31) & 0x7FFFFFFF)


def _key_f32(k):
    return lax.bitcast_convert_type(k ^ ((k >> 31) & 0x7FFFFFFF), F32)


def _dsa_kernel(iqt_ref, iwt_ref, aqt_ref, ik_ref, ak_ref, avt_ref, ms_ref, o_ref, score_ref, qa_ref, sa_ref, sb_ref, mask_ref, *acc_refs,
                tq, tks, n_sel):
    H, G, R, d = DSA_HEADS, DSA_KV_HEADS, DSA_HEADS // DSA_KV_HEADS, DSA_HD
    t0 = pl.program_id(1) * tq
    nsub = (t0 + tq) // tks
    tka = 2 * tks
    npair = ((t0 + tq) // tka + 1) // 2
    kf = float(n_sel)
    qpos = t0 + lax.broadcasted_iota(I32, (1, tq), 1)
    krow = lax.broadcasted_iota(I32, (tks, tq), 0)

    wrow = [iwt_ref[h:h + 1, :] for h in range(IDX_HEADS)]

    def score_pair(i, carry):
        for u in range(2 * tka // tks):
            r0 = pl.multiple_of(i * 2 * tka + u * tks, tks)
            kc = ik_ref[pl.ds(r0, tks), :]
            acc = jnp.zeros((tks, tq), F32)
            for h in range(IDX_HEADS):
                rel = jnp.dot(kc, iqt_ref[h * IDX_HD:(h + 1) * IDX_HD, :], preferred_element_type=F32)
                acc = acc + jnp.maximum(rel, 0.0) * wrow[h]
            score_ref[pl.ds(r0, tks), :] = jnp.where(r0 + krow <= qpos, acc, -jnp.inf)
        return carry

    lax.fori_loop(0, npair, score_pair, 0)

    def fold8(x, op):
        acc = x[0:8, :]
        for i in range(1, tks // 8):
            acc = op(acc, x[8 * i:8 * (i + 1), :])
        return acc

    def count(th, strict):
        def body(j, acc):
            for u in range(2):
                s = score_ref[pl.ds(pl.multiple_of((2 * j + u) * tks, tks), tks), :]
                hit = (s > th) if strict else (s >= th)
                acc = acc + fold8(jnp.where(hit, 1.0, 0.0), jnp.add)
            return acc

        acc = lax.fori_loop(0, nsub // 2, body, jnp.zeros((8, tq), F32))
        return jnp.sum(acc, axis=0, keepdims=True)

    def minmax(j, carry):
        mx, mn = carry
        s = score_ref[pl.ds(pl.multiple_of(j * tks, tks), tks), :]
        mx = jnp.maximum(mx, fold8(s, jnp.maximum))
        mn = jnp.minimum(mn, fold8(jnp.where(s == -jnp.inf, jnp.inf, s), jnp.minimum))
        return mx, mn

    mx, mn = lax.fori_loop(0, nsub, minmax, (jnp.full((8, tq), -jnp.inf, F32), jnp.full((8, tq), jnp.inf, F32)))
    hi0 = jnp.max(mx, axis=0, keepdims=True)
    lo0 = jnp.min(mn, axis=0, keepdims=True)

    def probe(lo, hi, it):
        lk, hk = _f32_key(lo), _f32_key(hi)
        mk = (lk >> 1) + (hk >> 1) + (lk & hk & 1)
        mv = lo + (hi - lo) * 0.5
        early = (jnp.zeros((1, tq), I32) + it) < BISECT_VALUE_STEPS
        mid = jnp.where(early & (mv > lo) & (mv < hi), mv, _key_f32(mk))
        return mid, jnp.max(jnp.where(mk != lk, 1.0, 0.0))

    def bis_cond(c):
        return (c[4] > 0.0) & (c[5] < BISECT_MAX_STEPS)

    def bis_body(c):
        lo, hi, c_lo, c_hi, _, it = c
        mid, _ = probe(lo, hi, it)
        cnt = count(mid, False)
        ge = cnt >= kf
        up = ge | (cnt == kf)
        dn = (~ge) | (cnt == kf)
        lo, c_lo = jnp.where(up, mid, lo), jnp.where(up, cnt, c_lo)
        hi, c_hi = jnp.where(dn, mid, hi), jnp.where(dn, cnt, c_hi)
        _, active = probe(lo, hi, it + 1)
        return lo, hi, c_lo, c_hi, active, it + 1

    zero = jnp.zeros((1, tq), F32)
    n_pos = count(zero, True)
    n_nonneg = count(zero, False)
    keep_all = qpos + 1 <= n_sel
    settled = keep_all | ((n_nonneg >= kf) & (n_pos < kf))
    above = n_pos >= kf
    c_lo0 = jnp.where(settled | above, n_nonneg, (qpos + 1).astype(F32))
    c_hi0 = jnp.where(settled | ~above, n_nonneg, count(hi0, False))
    lo0 = jnp.where(settled | above, zero, lo0)
    hi0 = jnp.where(settled | ~above, zero, hi0)
    _, active0 = probe(lo0, hi0, jnp.int32(0))
    lo, hi, c_lo, c_hi, _, _ = lax.while_loop(bis_cond, bis_body, (lo0, hi0, c_lo0, c_hi0, active0, jnp.int32(0)))
    at_hi = c_hi >= kf
    thr = jnp.where(keep_all, F32_LOWEST, jnp.where(at_hi, hi, lo))
    excess = jnp.where(keep_all, 0.0, jnp.where(at_hi, c_hi, c_lo) - kf)

    @pl.when(jnp.max(excess) > 0.0)
    def _():
        budget = kf - count(thr, True)
        earlier = lax.broadcasted_iota(I32, (tks, tks), 1) < lax.broadcasted_iota(I32, (tks, tks), 0)
        earlier = jnp.where(earlier, 1.0, 0.0).astype(BF16)

        def fix(j, seen):
            r0 = pl.multiple_of(j * tks, tks)
            s = score_ref[pl.ds(r0, tks), :]
            eq = s == thr
            eqf = jnp.where(eq, 1.0, 0.0)
            rank = jnp.dot(earlier, eqf.astype(BF16), preferred_element_type=F32) + seen
            score_ref[pl.ds(r0, tks), :] = jnp.where(eq & (rank >= budget), -jnp.inf, s)
            return seen + jnp.sum(eqf, axis=0, keepdims=True)

        lax.fori_loop(0, nsub, fix, jnp.zeros((1, tq), F32))

    arow = lax.broadcasted_iota(I32, (KAUG - d, tq), 0)
    for h in range(H):
        slope = float(2.0 ** (-8.0 * (h + 1) / H))
        qa_ref[h, 0:d, :] = aqt_ref[h * d:(h + 1) * d, :]
        qa_ref[h, d:KAUG, :] = jnp.where(arow == 0, slope * ALIBI_SPLIT, jnp.where(arow == 1, slope, 0.0)).astype(BF16)
    for acc in acc_refs:
        acc[...] = jnp.zeros_like(acc)

    def logits(j, h):
        ka = ak_ref[pl.ds(pl.multiple_of(j * tka, tka), tka), (h // R) * KAUG:(h // R + 1) * KAUG]
        return jnp.dot(ka, qa_ref[h], preferred_element_type=F32)

    def step(j, j_next, cur_ref, next_ref, ms):
        r0 = pl.multiple_of(j * tka, tka)
        mask_ref[...] = jnp.where(score_ref[pl.ds(r0, tka), :] >= thr, 0.0, NEG_BIG)
        nms = []
        for h in range(H):
            g = h // R
            next_ref[h] = logits(j_next, h)
            s = cur_ref[h] + mask_ref[...]
            m_new = jnp.maximum(ms[h], jnp.max(s, axis=0, keepdims=True))
            p = jnp.exp(s - m_new).astype(BF16)
            va = avt_ref[g * VAUG:(g + 1) * VAUG, pl.ds(r0, tka)]
            acc = acc_refs[h]
            acc[...] = acc[...] * jnp.exp(ms[h] - m_new) + jnp.dot(va, p, preferred_element_type=F32)
            nms.append(m_new)
        return tuple(nms)

    for h in range(H):
        sa_ref[h] = logits(0, h)

    def att_pair(i, ms):
        ms = step(2 * i, 2 * i + 1, sa_ref, sb_ref, ms)
        return step(2 * i + 1, jnp.minimum(2 * i + 2, 2 * npair - 1), sb_ref, sa_ref, ms)

    lax.fori_loop(0, npair, att_pair, tuple(jnp.full((1, tq), NEG_BIG, F32) for _ in range(H)))
    for h in range(H):
        a = acc_refs[h][...]
        o = a[0:d, :] / a[d:d + 1, :]
        o_ref[h * d:(h + 1) * d, :] = (o * ms_ref[h * d:(h + 1) * d, :]).astype(BF16)


def _dsa(iqt, iwt, aqt, ik, ak, avt, ms_att, tq, tks):
    B, _, S = iqt.shape
    assert S <= ALIBI_SPLIT * 256 and tq % (2 * tks) == 0
    assert (S // (2 * tks)) % 2 == 0
    n_sel = min(TOPK_MAX, S // 4)
    G = DSA_KV_HEADS
    colT = lambda h: pl.BlockSpec((None, h, tq), lambda b, i: (b, 0, i))
    msb = jnp.broadcast_to(ms_att.reshape(DSA_W, 1), (DSA_W, tq))
    return pl.pallas_call(
        functools.partial(_dsa_kernel, tq=tq, tks=tks, n_sel=n_sel),
        grid=(B, S // tq),
        in_specs=[colT(IDX_HEADS * IDX_HD), colT(IDX_HEADS), colT(DSA_W),
                  pl.BlockSpec((None, S, IDX_HD), lambda b, i: (b, 0, 0)),
                  pl.BlockSpec((None, S, G * KAUG), lambda b, i: (b, 0, 0)),
                  pl.BlockSpec((None, G * VAUG, S), lambda b, i: (b, 0, 0)),
                  pl.BlockSpec((DSA_W, tq), lambda b, i: (0, 0))],
        out_specs=colT(DSA_W),
        out_shape=jax.ShapeDtypeStruct((B, DSA_W, S), BF16),
        scratch_shapes=[pltpu.VMEM((S, tq), F32), pltpu.VMEM((DSA_HEADS, KAUG, tq), BF16),
                        pltpu.VMEM((DSA_HEADS, 2 * tks, tq), F32), pltpu.VMEM((DSA_HEADS, 2 * tks, tq), F32),
                        pltpu.VMEM((2 * tks, tq), F32)]
        + [pltpu.VMEM((VAUG, tq), F32) for _ in range(DSA_HEADS)],
        compiler_params=_cparams(("parallel", "arbitrary")),
        name="dsa",
    )(iqt, iwt, aqt, ik, ak, avt, msb)


def _oproj_kernel(ret_ref, att_ref, x_ref, wo_ref, g1_ref, n2_ref, sc_ref, sh_ref, rw_ref, rb_ref,
                  x1_ref, h2_ref, sel_ref, idx_ref, gate_ref, cnt_ref):
    mixo = jnp.dot(ret_ref[...], wo_ref[:RET_W, :], preferred_element_type=F32)
    mixo = mixo + lax.dot_general(att_ref[...], wo_ref[RET_W:, :], (((0,), (0,)), ((), ())),
                                  preferred_element_type=F32)
    x1 = x_ref[...] + g1_ref[...] * mixo
    x1_ref[...] = x1
    y = x1 * lax.rsqrt(jnp.mean(x1 * x1, axis=-1, keepdims=True) + EPS) * n2_ref[...]
    h2 = y * (1.0 + sc_ref[...]) + sh_ref[...]
    h2_ref[...] = h2
    logits = jnp.dot(h2, rw_ref[...], preferred_element_type=F32,
                     precision=lax.Precision.HIGHEST) + rb_ref[...]
    tm = logits.shape[0]
    lane = lax.broadcasted_iota(I32, (tm, LANES), 1).astype(F32)
    work = jnp.where(lane < N_EXPERTS, logits, -jnp.inf)
    sel = jnp.zeros((tm, LANES), F32)
    idx_tab = jnp.zeros((tm, LANES), F32)
    vals = []
    for k in range(TOP_K):
        m = jnp.max(work, axis=1, keepdims=True)
        idx = jnp.min(jnp.where(work == m, lane, float(LANES)), axis=1, keepdims=True)
        hit = lane == idx
        sel = jnp.where(hit, 1.0, sel)
        idx_tab = jnp.where(lane == k, idx, idx_tab)
        work = jnp.where(hit, -jnp.inf, work)
        vals.append(m)
    es = [jnp.exp(v - vals[0]) for v in vals]
    den = es[0] + es[1] + es[2] + es[3]
    gate_tab = jnp.zeros((tm, LANES), F32)
    for k in range(TOP_K):
        gate_tab = jnp.where(lane == k, es[k] / den, gate_tab)
    sel_ref[...] = sel
    idx_ref[...] = idx_tab
    gate_ref[...] = gate_tab

    @pl.when((pl.program_id(0) == 0) & (pl.program_id(1) == 0))
    def _():
        cnt_ref[...] = jnp.zeros_like(cnt_ref)

    cnt_ref[...] += jnp.sum(sel, axis=0, keepdims=True)


def _oproj(ret, att, x, wo, g1, n2g, sc2, sh2, rw_pad, rb_pad, tm):
    B, S, D = x.shape
    nt = S // tm
    row = lambda w: pl.BlockSpec((None, tm, w), lambda b, i: (b, i, 0))
    flat = lambda w: pl.BlockSpec((tm, w), lambda b, i: (b * nt + i, 0))
    vec = pl.BlockSpec((None, 1, D), lambda b, i: (b, 0, 0))
    cst = lambda shape: pl.BlockSpec(shape, lambda b, i: (0, 0))
    sd = lambda shape, dt: jax.ShapeDtypeStruct(shape, dt)
    N = B * S
    return pl.pallas_call(
        _oproj_kernel,
        grid=(B, nt),
        in_specs=[row(RET_W), pl.BlockSpec((None, DSA_W, tm), lambda b, i: (b, 0, i)), row(D), cst((D, D)), vec,
                  cst((1, D)), vec, vec,
                  cst((D, LANES)), cst((1, LANES))],
        out_specs=[flat(D), flat(D), flat(LANES), flat(LANES), flat(LANES), cst((1, LANES))],
        out_shape=[sd((N, D), F32), sd((N, D), F32), sd((N, LANES), F32), sd((N, LANES), F32),
                   sd((N, LANES), F32), sd((1, LANES), F32)],
        compiler_params=_cparams(("arbitrary", "arbitrary")),
        name="oproj",
    )(ret, att, x, wo, g1, n2g.reshape(1, D), sc2, sh2, rw_pad, rb_pad)


def _dest_kernel(sel_ref, idx_ref, pstart_ref, dest_ref, seen_ref):
    @pl.when(pl.program_id(0) == 0)
    def _():
        seen_ref[...] = jnp.zeros_like(seen_ref)

    sel = sel_ref[...]
    tm = sel.shape[0]
    earlier = lax.broadcasted_iota(I32, (tm, tm), 1) < lax.broadcasted_iota(I32, (tm, tm), 0)
    earlier = jnp.where(earlier, 1.0, 0.0).astype(BF16)
    rank = jnp.dot(earlier, sel.astype(BF16), preferred_element_type=F32) + seen_ref[...]
    dest = pstart_ref[...] + rank
    lane = lax.broadcasted_iota(I32, (tm, LANES), 1).astype(F32)
    idx_tab = idx_ref[...]
    out = jnp.zeros((tm, LANES), F32)
    for k in range(TOP_K):
        e_k = jnp.sum(jnp.where(lane == k, idx_tab, 0.0), axis=1, keepdims=True)
        d_k = jnp.sum(jnp.where(lane == e_k, dest, 0.0), axis=1, keepdims=True)
        out = jnp.where(lane == k, d_k, out)
    dest_ref[...] = out.astype(I32)
    seen_ref[...] += jnp.sum(sel, axis=0, keepdims=True)


def _dest(sel, idx_tab, pstart, tm):
    N = sel.shape[0]
    blk = pl.BlockSpec((tm, LANES), lambda i: (i, 0))
    return pl.pallas_call(
        _dest_kernel,
        grid=(N // tm,),
        in_specs=[blk, blk, pl.BlockSpec((1, LANES), lambda i: (0, 0))],
        out_specs=blk,
        out_shape=jax.ShapeDtypeStruct((N, LANES), I32),
        scratch_shapes=[pltpu.VMEM((1, LANES), F32)],
        compiler_params=_cparams(("arbitrary",)),
        name="dest",
    )(sel, idx_tab, pstart)


def _disp_kernel(h2_ref, dest_hbm, xs_in, xs_hbm, dsm, sem_idx, sem_row, *, tm):
    del xs_in
    i = pl.program_id(0)
    n = tm * TOP_K
    cp = pltpu.make_async_copy(dest_hbm.at[pl.ds(i * n, n)], dsm, sem_idx)
    cp.start()
    cp.wait()

    def row_copy(t, dst):
        return pltpu.make_async_copy(h2_ref.at[pl.ds(t, 1)], xs_hbm.at[pl.ds(dst, 1)], sem_row)

    def issue(t, c):
        for k in range(TOP_K):
            row_copy(t, dsm[t * TOP_K + k]).start()
        return c

    lax.fori_loop(0, tm, issue, 0)

    def drain(t, c):
        for k in range(TOP_K):
            row_copy(0, 0).wait()
        return c

    lax.fori_loop(0, tm, drain, 0)


def _dispatch(h2, dest_flat, n_rows, tm):
    N, D = h2.shape
    xs0 = jnp.zeros((n_rows, D), F32)
    return pl.pallas_call(
        functools.partial(_disp_kernel, tm=tm),
        grid=(N // tm,),
        in_specs=[pl.BlockSpec((tm, D), lambda i: (i, 0)),
                  pl.BlockSpec(memory_space=pl.ANY),
                  pl.BlockSpec(memory_space=pl.ANY)],
        out_specs=pl.BlockSpec(memory_space=pl.ANY),
        out_shape=jax.ShapeDtypeStruct((n_rows, D), F32),
        scratch_shapes=[pltpu.SMEM((tm * TOP_K,), I32), pltpu.SemaphoreType.DMA, pltpu.SemaphoreType.DMA],
        input_output_aliases={2: 0},
        compiler_params=_cparams(("arbitrary",)),
        name="disp",
    )(h2, dest_flat, xs0)


def _ffn_kernel(be_ref, nb_ref, xs_ref, wgu_ref, bgu_ref, wd_ref, bd_ref, ys_ref):
    @pl.when(pl.program_id(0) < nb_ref[0])
    def _():
        xb = xs_ref[...].astype(BF16)
        gu = jnp.dot(xb, wgu_ref[...], preferred_element_type=F32) + bgu_ref[...]
        gate = jnp.minimum(gu[:, :D_EXPERT], SWIGLU_LIMIT)
        up = jnp.clip(gu[:, D_EXPERT:], -SWIGLU_LIMIT, SWIGLU_LIMIT)
        glu = gate * (1.0 / (1.0 + jnp.exp(-SWIGLU_ALPHA * gate)))
        act = ((up + 1.0) * glu).astype(BF16)
        ys_ref[...] = jnp.dot(act, wd_ref[...], preferred_element_type=F32) + bd_ref[...]

    @pl.when(pl.program_id(0) >= nb_ref[0])
    def _():
        ys_ref[...] = jnp.zeros_like(ys_ref)


def _ffn(xs, block_e, n_blocks, w_gu, b_gu, w_down, b_down, tmb):
    P, D = xs.shape
    E = w_gu.shape[0]
    blk = lambda i, be, nb: (jnp.minimum(i, nb[0] - 1), 0)
    wsel = lambda i, be, nb: (be[jnp.minimum(i, nb[0] - 1)], 0, 0)
    grid_spec = pltpu.PrefetchScalarGridSpec(
        num_scalar_prefetch=2,
        grid=(P // tmb,),
        in_specs=[pl.BlockSpec((tmb, D), blk),
                  pl.BlockSpec((None, D, 2 * D_EXPERT), wsel),
                  pl.BlockSpec((None, 1, 2 * D_EXPERT), wsel),
                  pl.BlockSpec((None, D_EXPERT, D), wsel),
                  pl.BlockSpec((None, 1, D), wsel)],
        out_specs=pl.BlockSpec((tmb, D), lambda i, be, nb: (i, 0)),
    )
    return pl.pallas_call(
        _ffn_kernel,
        grid_spec=grid_spec,
        out_shape=jax.ShapeDtypeStruct((P, D), F32),
        compiler_params=_cparams(("arbitrary",)),
        name="ffn",
    )(block_e, n_blocks, xs, w_gu, b_gu.reshape(E, 1, 2 * D_EXPERT), w_down, b_down.reshape(E, 1, D))


def _comb_kernel(x1_ref, gate_ref, g2_ref, fg_ref, dest_hbm, ys_hbm, o_ref, buf, dsm, sem_idx, sem_row, *, tm):
    i = pl.program_id(0)
    n = tm * TOP_K
    cp = pltpu.make_async_copy(dest_hbm.at[pl.ds(i * n, n)], dsm, sem_idx)
    cp.start()
    cp.wait()

    def row_copy(t, k, src):
        return pltpu.make_async_copy(ys_hbm.at[pl.ds(src, 1)], buf.at[k, pl.ds(t, 1)], sem_row)

    def issue(t, c):
        for k in range(TOP_K):
            row_copy(t, k, dsm[t * TOP_K + k]).start()
        return c

    lax.fori_loop(0, tm, issue, 0)

    def drain(t, c):
        for k in range(TOP_K):
            row_copy(0, k, 0).wait()
        return c

    lax.fori_loop(0, tm, drain, 0)

    gates = gate_ref[...]
    y = jnp.zeros(x1_ref.shape, F32)
    for k in range(TOP_K):
        y = y + gates[:, k:k + 1] * buf[k]
    v = x1_ref[...] + g2_ref[...] * y
    o_ref[...] = v * lax.rsqrt(jnp.mean(v * v, axis=-1, keepdims=True) + EPS) * fg_ref[...]


def _combine(x1, gate_tab, g2, final_g, dest_flat, ys, S, tm):
    N, D = x1.shape
    per_b = S // tm
    return pl.pallas_call(
        functools.partial(_comb_kernel, tm=tm),
        grid=(N // tm,),
        in_specs=[pl.BlockSpec((tm, D), lambda i: (i, 0)),
                  pl.BlockSpec((tm, LANES), lambda i: (i, 0)),
                  pl.BlockSpec((None, 1, D), lambda i: (i // per_b, 0, 0)),
                  pl.BlockSpec((1, D), lambda i: (0, 0)),
                  pl.BlockSpec(memory_space=pl.ANY),
                  pl.BlockSpec(memory_space=pl.ANY)],
        out_specs=pl.BlockSpec((tm, D), lambda i: (i, 0)),
        out_shape=jax.ShapeDtypeStruct((N, D), F32),
        scratch_shapes=[pltpu.VMEM((TOP_K, tm, D), F32), pltpu.SMEM((tm * TOP_K,), I32),
                        pltpu.SemaphoreType.DMA, pltpu.SemaphoreType.DMA],
        compiler_params=_cparams(("arbitrary",)),
        name="comb",
    )(x1, gate_tab, g2, final_g.reshape(1, D), dest_flat, ys)


def _tile(n, pref):
    t = min(pref, n)
    assert n % t == 0, (n, t)
    return t


def _layer(x, c, ada_w, ada_b, norm1_g, w_in, mix_scale, w_o, norm2_g,
           router_w, router_b, w_gu, b_gu, w_down, b_down, final_g):
    B, S, D = x.shape
    N = B * S
    mod = _mod(c, ada_w, ada_b).reshape(B, 6, 1, D)
    sh1, sc1, g1, sh2, sc2, g2 = (mod[:, j] for j in range(6))

    w_pad = jnp.pad(w_in, ((0, 0), (0, IN_COLS_PAD - IN_COLS))).astype(BF16)
    rq, rk, rv, rg, aqt, ak, avt, iqt, ik, iwt = _inproj(x, norm1_g, sc1, sh1, w_pad, _tile(S, 512))
    ms = mix_scale.reshape(1, RET_W + DSA_W)
    ret = _retention(rq, rk, rv, rg, ms[:, :RET_W])
    att = _dsa(iqt, iwt, aqt, ik, ak, avt, ms[:, RET_W:], _tile(S, 256), 128)

    rw_pad = jnp.pad(router_w, ((0, 0), (0, LANES - N_EXPERTS)))
    rb_pad = jnp.pad(router_b, (0, LANES - N_EXPERTS)).reshape(1, LANES)
    x1, h2, sel, idx_tab, gate_tab, counts = _oproj(ret, att, x, w_o.astype(BF16), g1, norm2_g, sc2, sh2,
                                                    rw_pad, rb_pad, _tile(S, 512))

    tmb = 512
    n_rows = (N * TOP_K + N_EXPERTS * (tmb - 1)) // tmb * tmb
    cnt = counts[0, :N_EXPERTS].astype(I32)
    padded = (cnt + tmb - 1) // tmb * tmb
    ends = jnp.cumsum(padded)
    pstart = jnp.pad((ends - padded).astype(F32), (0, LANES - N_EXPERTS)).reshape(1, LANES)
    n_blocks = (ends[-1] // tmb).reshape(1)
    block_e = jnp.clip(jnp.searchsorted(ends, jnp.arange(n_rows // tmb, dtype=I32) * tmb, side='right'),
                       0, N_EXPERTS - 1).astype(I32)

    tmd = _tile(N, 256)
    dest_tab = _dest(sel, idx_tab, pstart, tmd)
    dest_flat = dest_tab[:, :TOP_K].reshape(N * TOP_K)
    xs = _dispatch(h2, dest_flat, n_rows, tmd)
    ys = _ffn(xs, block_e, n_blocks, w_gu.astype(BF16), b_gu, w_down.astype(BF16), b_down, tmb)
    out = _combine(x1, gate_tab, g2, final_g, dest_flat, ys, S, _tile(S, 256))
    return out.reshape(B, S, D)


def kernel(x, c, ada_w, ada_b, norm1_g, w_in, mix_scale, w_o, norm2_g, router_w, router_b, w_gu, b_gu,
           w_down, b_down, final_g):
    assert ada_w.shape[0] == 1, "single-layer stack"
    return _layer(x, c, ada_w[0], ada_b[0], norm1_g[0], w_in[0], mix_scale[0], w_o[0], norm2_g[0],
                  router_w[0], router_b[0], w_gu[0], b_gu[0], w_down[0], b_down[0], final_g)
```

```python
import functools

import numpy as np
import jax
import jax.numpy as jnp
from jax import lax
from jax.experimental import pallas as pl
from jax.experimental.pallas import tpu as pltpu

F32 = jnp.float32
BF16 = jnp.bfloat16
I32 = jnp.int32

D_MODEL = 1024
RET_HEADS = 4
RET_DK = 64
RET_DV = 128
RET_CHUNK = 128
DSA_HEADS = 8
DSA_KV_HEADS = 2
DSA_HD = 64
IDX_HEADS = 8
IDX_HD = 64
TOPK_MAX = 256
N_EXPERTS = 32
TOP_K = 4
D_EXPERT = D_MODEL
SWIGLU_LIMIT = 7.0
SWIGLU_ALPHA = 1.702
EPS = 1e-6

RET_W = RET_HEADS * RET_DV
DSA_W = DSA_HEADS * DSA_HD
IN_COLS = 2888
IN_COLS_PAD = 2944

KAUG = 128
VAUG = 80
ALIBI_SPLIT = 64
FFN_ROWS = 512
BISECT_VALUE_STEPS = 8
BISECT_MAX_STEPS = 64

LANES = 128
SUBLANES = 8
VMEM_LIMIT = 56 * 1024 * 1024
NEG_BIG = -1e30
F32_LOWEST = float(np.finfo(np.float32).min)


def _cparams(sem):
    return pltpu.CompilerParams(dimension_semantics=sem, vmem_limit_bytes=VMEM_LIMIT)


def _mod_kernel(c_ref, w_ref, b_ref, o_ref):
    c = c_ref[...]
    s = c * (1.0 / (1.0 + jnp.exp(-c)))
    o_ref[...] = jnp.dot(s, w_ref[...], preferred_element_type=F32,
                         precision=lax.Precision.HIGHEST) + b_ref[...]


def _mod(c, ada_w, ada_b):
    B, D = c.shape
    n_out = ada_w.shape[1]
    rows = 8
    c8 = jnp.zeros((rows, D), F32).at[:B].set(c)
    out = pl.pallas_call(
        _mod_kernel,
        grid=(n_out // D,),
        in_specs=[pl.BlockSpec((rows, D), lambda j: (0, 0)),
                  pl.BlockSpec((D, D), lambda j: (0, j)),
                  pl.BlockSpec((1, D), lambda j: (0, j))],
        out_specs=pl.BlockSpec((rows, D), lambda j: (0, j)),
        out_shape=jax.ShapeDtypeStruct((rows, n_out), F32),
        compiler_params=_cparams(("arbitrary",)),
        name="mod",
    )(c8, ada_w, ada_b.reshape(1, n_out))
    return out[:B]


def _inproj_kernel(x_ref, g_ref, sc_ref, sh_ref, w_ref,
                   rq_ref, rk_ref, rv_ref, rg_ref, aqt_ref, ak_ref, avt_ref, iqt_ref, ik_ref, iwt_ref):
    x = x_ref[...]
    ms = jnp.mean(x * x, axis=-1, keepdims=True)
    y = x * lax.rsqrt(ms + EPS) * g_ref[...]
    hb = (y * (1.0 + sc_ref[...]) + sh_ref[...]).astype(BF16)

    def proj(lo, hi):
        return jnp.dot(hb, w_ref[:, lo:hi], preferred_element_type=F32)

    tm = x.shape[0]
    d = DSA_HD
    rq_ref[...] = proj(0, 256).astype(BF16)
    rk_ref[...] = (proj(256, 512) * (RET_DK ** -0.5)).astype(BF16)
    rv_ref[...] = proj(512, 1024).astype(BF16)
    rg_ref[...] = proj(1024, 1536).astype(BF16)
    aqt_ref[...] = (proj(1536, 2048) * (d ** -0.5)).T.astype(BF16)
    kk = proj(2048, 2176)
    pos = pl.program_id(1) * tm + lax.broadcasted_iota(I32, (tm, d), 0)
    col = lax.broadcasted_iota(I32, (tm, d), 1)
    posblk = jnp.where(col == 0, pos // ALIBI_SPLIT, jnp.where(col == 1, pos % ALIBI_SPLIT, 0)).astype(F32)
    for g in range(DSA_KV_HEADS):
        ak_ref[:, g * KAUG:g * KAUG + d] = kk[:, g * d:(g + 1) * d].astype(BF16)
        ak_ref[:, g * KAUG + d:(g + 1) * KAUG] = posblk.astype(BF16)
    vt = proj(2176, 2304).T
    r16 = lax.broadcasted_iota(I32, (VAUG - d, tm), 0)
    onesblk = jnp.where(r16 == 0, 1.0, 0.0).astype(BF16)
    for g in range(DSA_KV_HEADS):
        avt_ref[g * VAUG:g * VAUG + d, :] = vt[g * d:(g + 1) * d, :].astype(BF16)
        avt_ref[g * VAUG + d:(g + 1) * VAUG, :] = onesblk
    iqt_ref[...] = proj(2304, 2816).T.astype(BF16)
    last = proj(2816, 2944)
    ik_ref[...] = last[:, :IDX_HD].astype(BF16)
    iwt_ref[...] = last.T[IDX_HD:IDX_HD + IDX_HEADS, :] * ((IDX_HD ** -0.5) * (IDX_HEADS ** -0.5))


def _inproj(x, norm_g, sc, sh, w_pad, tm):
    B, S, D = x.shape
    row = lambda w: pl.BlockSpec((None, tm, w), lambda b, i: (b, i, 0))
    colT = lambda h: pl.BlockSpec((None, h, tm), lambda b, i: (b, 0, i))
    vec = pl.BlockSpec((None, 1, D), lambda b, i: (b, 0, 0))
    sd = lambda shape, dt: jax.ShapeDtypeStruct(shape, dt)
    G = DSA_KV_HEADS
    return pl.pallas_call(
        _inproj_kernel,
        grid=(B, S // tm),
        in_specs=[row(D), pl.BlockSpec((1, D), lambda b, i: (0, 0)), vec, vec,
                  pl.BlockSpec((D, IN_COLS_PAD), lambda b, i: (0, 0))],
        out_specs=[row(256), row(256), row(512), row(512), colT(DSA_W), row(G * KAUG), colT(G * VAUG),
                   colT(IDX_HEADS * IDX_HD), row(IDX_HD), colT(IDX_HEADS)],
        out_shape=[sd((B, S, 256), BF16), sd((B, S, 256), BF16), sd((B, S, 512), BF16),
                   sd((B, S, 512), BF16), sd((B, DSA_W, S), BF16), sd((B, S, G * KAUG), BF16),
                   sd((B, G * VAUG, S), BF16), sd((B, IDX_HEADS * IDX_HD, S), BF16),
                   sd((B, S, IDX_HD), BF16), sd((B, IDX_HEADS, S), F32)],
        compiler_params=_cparams(("parallel", "parallel")),
        name="inproj",
    )(x, norm_g.reshape(1, D), sc, sh, w_pad)


def _ret_kernel(rq_ref, rk_ref, rv_ref, rg_ref, din_ref, qd_ref, kd_ref, cd_ref, ms_ref, o_ref, state_ref):
    @pl.when(pl.program_id(1) == 0)
    def _():
        state_ref[...] = jnp.zeros_like(state_ref)

    for h in range(RET_HEADS):
        q = rq_ref[:, h * RET_DK:(h + 1) * RET_DK]
        k = rk_ref[:, h * RET_DK:(h + 1) * RET_DK]
        v = rv_ref[:, h * RET_DV:(h + 1) * RET_DV]
        r_prev = state_ref[h]
        s = lax.dot_general(q, k, (((1,), (1,)), ((), ())), preferred_element_type=F32) * din_ref[h]
        o = jnp.dot(s.astype(BF16), v, preferred_element_type=F32)
        o = o + jnp.dot(q, r_prev.astype(BF16), preferred_element_type=F32) * qd_ref[h]
        vd = (v.astype(F32) * kd_ref[h]).astype(BF16)
        kv = lax.dot_general(k, vd, (((0,), (0,)), ((), ())), preferred_element_type=F32)
        state_ref[h] = r_prev * cd_ref[h] + kv
        o = o * lax.rsqrt(jnp.mean(o * o, axis=-1, keepdims=True) + EPS)
        g = rg_ref[:, h * RET_DV:(h + 1) * RET_DV].astype(F32)
        gate = g * (1.0 / (1.0 + jnp.exp(-g)))
        o_ref[:, h * RET_DV:(h + 1) * RET_DV] = (gate * o * ms_ref[:, h * RET_DV:(h + 1) * RET_DV]).astype(BF16)


def _ret_consts(C):
    H = RET_HEADS
    log_g = np.log1p(-np.exp2(-5.0 - np.arange(H, dtype=np.float64)))
    pos = np.arange(C, dtype=np.float64)
    diff = pos[:, None] - pos[None, :]
    d_inner = np.where(diff[None] >= 0, np.exp(np.maximum(diff, 0.0)[None] * log_g[:, None, None]), 0.0)
    q_decay = np.exp((pos + 1.0)[None] * log_g[:, None])
    k_decay = np.exp((C - 1.0 - pos)[None] * log_g[:, None])
    chunk_decay = np.exp(C * log_g)
    qd = np.broadcast_to(q_decay[:, :, None], (H, C, RET_DV))
    kd = np.broadcast_to(k_decay[:, :, None], (H, C, RET_DV))
    cd = np.broadcast_to(chunk_decay[:, None, None], (H, 1, RET_DV))
    f = lambda a: jnp.asarray(np.ascontiguousarray(a), F32)
    return f(d_inner), f(qd), f(kd), f(cd)


def _retention(rq, rk, rv, rg, ms_ret):
    B, S, _ = rq.shape
    C = min(RET_CHUNK, S)
    din, qd, kd, cd = _ret_consts(C)
    row = lambda w: pl.BlockSpec((None, C, w), lambda b, n: (b, n, 0))
    full = lambda a: pl.BlockSpec(a.shape, lambda b, n: (0,) * a.ndim)
    return pl.pallas_call(
        _ret_kernel,
        grid=(B, S // C),
        in_specs=[row(256), row(256), row(512), row(512), full(din), full(qd), full(kd), full(cd),
                  pl.BlockSpec((1, RET_W), lambda b, n: (0, 0))],
        out_specs=row(RET_W),
        out_shape=jax.ShapeDtypeStruct((B, S, RET_W), BF16),
        scratch_shapes=[pltpu.VMEM((RET_HEADS, RET_DK, RET_DV), F32)],
        compiler_params=_cparams(("parallel", "arbitrary")),
        name="ret",
    )(rq, rk, rv, rg, din, qd, kd, cd, ms_ret)


def _f32_key(x):
    i = lax.bitcast_convert_type(x, I32)
    return i ^ ((i >> 31) & 0x7FFFFFFF)


def _key_f32(k):
    return lax.bitcast_convert_type(k ^ ((k >> 31) & 0x7FFFFFFF), F32)


def _dsa_kernel(iqt_ref, iwt_ref, aqt_ref, ik_ref, ak_ref, avt_ref, ms_ref, o_ref, score_ref, qa_ref, sa_ref, sb_ref, mask_ref, *acc_refs,
                tq, tks, n_sel):
    H, G, R, d = DSA_HEADS, DSA_KV_HEADS, DSA_HEADS // DSA_KV_HEADS, DSA_HD
    t0 = pl.program_id(1) * tq
    nsub = (t0 + tq) // tks
    tka = 2 * tks
    npair = ((t0 + tq) // tka + 1) // 2
    kf = float(n_sel)
    qpos = t0 + lax.broadcasted_iota(I32, (1, tq), 1)
    krow = lax.broadcasted_iota(I32, (tks, tq), 0)

    wrow = [iwt_ref[h:h + 1, :] for h in range(IDX_HEADS)]

    def score_pair(i, carry):
        for u in range(2 * tka // tks):
            r0 = pl.multiple_of(i * 2 * tka + u * tks, tks)
            kc = ik_ref[pl.ds(r0, tks), :]
            acc = jnp.zeros((tks, tq), F32)
            for h in range(IDX_HEADS):
                rel = jnp.dot(kc, iqt_ref[h * IDX_HD:(h + 1) * IDX_HD, :], preferred_element_type=F32)
                acc = acc + jnp.maximum(rel, 0.0) * wrow[h]
            score_ref[pl.ds(r0, tks), :] = jnp.where(r0 + krow <= qpos, acc, -jnp.inf)
        return carry

    lax.fori_loop(0, npair, score_pair, 0)

    def fold8(x, op):
        acc = x[0:8, :]
        for i in range(1, tks // 8):
            acc = op(acc, x[8 * i:8 * (i + 1), :])
        return acc

    def count(th, strict):
        def body(j, acc):
            for u in range(2):
                s = score_ref[pl.ds(pl.multiple_of((2 * j + u) * tks, tks), tks), :]
                hit = (s > th) if strict else (s >= th)
                acc = acc + fold8(jnp.where(hit, 1.0, 0.0), jnp.add)
            return acc

        acc = lax.fori_loop(0, nsub // 2, body, jnp.zeros((8, tq), F32))
        return jnp.sum(acc, axis=0, keepdims=True)

    def minmax(j, carry):
        mx, mn = carry
        s = score_ref[pl.ds(pl.multiple_of(j * tks, tks), tks), :]
        mx = jnp.maximum(mx, fold8(s, jnp.maximum))
        mn = jnp.minimum(mn, fold8(jnp.where(s == -jnp.inf, jnp.inf, s), jnp.minimum))
        return mx, mn

    mx, mn = lax.fori_loop(0, nsub, minmax, (jnp.full((8, tq), -jnp.inf, F32), jnp.full((8, tq), jnp.inf, F32)))
    hi0 = jnp.max(mx, axis=0, keepdims=True)
    lo0 = jnp.min(mn, axis=0, keepdims=True)

    def probe(lo, hi, it):
        lk, hk = _f32_key(lo), _f32_key(hi)
        mk = (lk >> 1) + (hk >> 1) + (lk & hk & 1)
        mv = lo + (hi - lo) * 0.5
        early = (jnp.zeros((1, tq), I32) + it) < BISECT_VALUE_STEPS
        mid = jnp.where(early & (mv > lo) & (mv < hi), mv, _key_f32(mk))
        return mid, jnp.max(jnp.where(mk != lk, 1.0, 0.0))

    def bis_cond(c):
        return (c[4] > 0.0) & (c[5] < BISECT_MAX_STEPS)

    def bis_body(c):
        lo, hi, c_lo, c_hi, _, it = c
        mid, _ = probe(lo, hi, it)
        cnt = count(mid, False)
        ge = cnt >= kf
        up = ge | (cnt == kf)
        dn = (~ge) | (cnt == kf)
        lo, c_lo = jnp.where(up, mid, lo), jnp.where(up, cnt, c_lo)
        hi, c_hi = jnp.where(dn, mid, hi), jnp.where(dn, cnt, c_hi)
        _, active = probe(lo, hi, it + 1)
        return lo, hi, c_lo, c_hi, active, it + 1

    zero = jnp.zeros((1, tq), F32)
    n_pos = count(zero, True)
    n_nonneg = count(zero, False)
    keep_all = qpos + 1 <= n_sel
    settled = keep_all | ((n_nonneg >= kf) & (n_pos < kf))
    above = n_pos >= kf
    c_lo0 = jnp.where(settled | above, n_nonneg, (qpos + 1).astype(F32))
    c_hi0 = jnp.where(settled | ~above, n_nonneg, count(hi0, False))
    lo0 = jnp.where(settled | above, zero, lo0)
    hi0 = jnp.where(settled | ~above, zero, hi0)
    _, active0 = probe(lo0, hi0, jnp.int32(0))
    lo, hi, c_lo, c_hi, _, _ = lax.while_loop(bis_cond, bis_body, (lo0, hi0, c_lo0, c_hi0, active0, jnp.int32(0)))
    at_hi = c_hi >= kf
    thr = jnp.where(keep_all, F32_LOWEST, jnp.where(at_hi, hi, lo))
    excess = jnp.where(keep_all, 0.0, jnp.where(at_hi, c_hi, c_lo) - kf)

    @pl.when(jnp.max(excess) > 0.0)
    def _():
        budget = kf - count(thr, True)
        earlier = lax.broadcasted_iota(I32, (tks, tks), 1) < lax.broadcasted_iota(I32, (tks, tks), 0)
        earlier = jnp.where(earlier, 1.0, 0.0).astype(BF16)

        def fix(j, seen):
            r0 = pl.multiple_of(j * tks, tks)
            s = score_ref[pl.ds(r0, tks), :]
            eq = s == thr
            eqf = jnp.where(eq, 1.0, 0.0)
            rank = jnp.dot(earlier, eqf.astype(BF16), preferred_element_type=F32) + seen
            score_ref[pl.ds(r0, tks), :] = jnp.where(eq & (rank >= budget), -jnp.inf, s)
            return seen + jnp.sum(eqf, axis=0, keepdims=True)

        lax.fori_loop(0, nsub, fix, jnp.zeros((1, tq), F32))

    arow = lax.broadcasted_iota(I32, (KAUG - d, tq), 0)
    for h in range(H):
        slope = float(2.0 ** (-8.0 * (h + 1) / H))
        qa_ref[h, 0:d, :] = aqt_ref[h * d:(h + 1) * d, :]
        qa_ref[h, d:KAUG, :] = jnp.where(arow == 0, slope * ALIBI_SPLIT, jnp.where(arow == 1, slope, 0.0)).astype(BF16)
    for acc in acc_refs:
        acc[...] = jnp.zeros_like(acc)

    def logits(j, h):
        ka = ak_ref[pl.ds(pl.multiple_of(j * tka, tka), tka), (h // R) * KAUG:(h // R + 1) * KAUG]
        return jnp.dot(ka, qa_ref[h], preferred_element_type=F32)

    def step(j, j_next, cur_ref, next_ref, ms):
        r0 = pl.multiple_of(j * tka, tka)
        mask_ref[...] = jnp.where(score_ref[pl.ds(r0, tka), :] >= thr, 0.0, NEG_BIG)
        nms = []
        for h in range(H):
            g = h // R
            next_ref[h] = logits(j_next, h)
            s = cur_ref[h] + mask_ref[...]
            m_new = jnp.maximum(ms[h], jnp.max(s, axis=0, keepdims=True))
            p = jnp.exp(s - m_new).astype(BF16)
            va = avt_ref[g * VAUG:(g + 1) * VAUG, pl.ds(r0, tka)]
            acc = acc_refs[h]
            acc[...] = acc[...] * jnp.exp(ms[h] - m_new) + jnp.dot(va, p, preferred_element_type=F32)
            nms.append(m_new)
        return tuple(nms)

    for h in range(H):
        sa_ref[h] = logits(0, h)

    def att_pair(i, ms):
        ms = step(2 * i, 2 * i + 1, sa_ref, sb_ref, ms)
        return step(2 * i + 1, jnp.minimum(2 * i + 2, 2 * npair - 1), sb_ref, sa_ref, ms)

    lax.fori_loop(0, npair, att_pair, tuple(jnp.full((1, tq), NEG_BIG, F32) for _ in range(H)))
    for h in range(H):
        a = acc_refs[h][...]
        o = a[0:d, :] / a[d:d + 1, :]
        o_ref[h * d:(h + 1) * d, :] = (o * ms_ref[h * d:(h + 1) * d, :]).astype(BF16)


def _dsa(iqt, iwt, aqt, ik, ak, avt, ms_att, tq, tks):
    B, _, S = iqt.shape
    assert S <= ALIBI_SPLIT * 256 and tq % (2 * tks) == 0
    assert (S // (2 * tks)) % 2 == 0
    n_sel = min(TOPK_MAX, S // 4)
    G = DSA_KV_HEADS
    colT = lambda h: pl.BlockSpec((None, h, tq), lambda b, i: (b, 0, i))
    msb = jnp.broadcast_to(ms_att.reshape(DSA_W, 1), (DSA_W, tq))
    return pl.pallas_call(
        functools.partial(_dsa_kernel, tq=tq, tks=tks, n_sel=n_sel),
        grid=(B, S // tq),
        in_specs=[colT(IDX_HEADS * IDX_HD), colT(IDX_HEADS), colT(DSA_W),
                  pl.BlockSpec((None, S, IDX_HD), lambda b, i: (b, 0, 0)),
                  pl.BlockSpec((None, S, G * KAUG), lambda b, i: (b, 0, 0)),
                  pl.BlockSpec((None, G * VAUG, S), lambda b, i: (b, 0, 0)),
                  pl.BlockSpec((DSA_W, tq), lambda b, i: (0, 0))],
        out_specs=colT(DSA_W),
        out_shape=jax.ShapeDtypeStruct((B, DSA_W, S), BF16),
        scratch_shapes=[pltpu.VMEM((S, tq), F32), pltpu.VMEM((DSA_HEADS, KAUG, tq), BF16),
                        pltpu.VMEM((DSA_HEADS, 2 * tks, tq), F32), pltpu.VMEM((DSA_HEADS, 2 * tks, tq), F32),
                        pltpu.VMEM((2 * tks, tq), F32)]
        + [pltpu.VMEM((VAUG, tq), F32) for _ in range(DSA_HEADS)],
        compiler_params=_cparams(("parallel", "arbitrary")),
        name="dsa",
    )(iqt, iwt, aqt, ik, ak, avt, msb)


def _oproj_kernel(ret_ref, att_ref, x_ref, wo_ref, g1_ref, n2_ref, sc_ref, sh_ref, rw_ref, rb_ref,
                  x1_ref, h2_ref, sel_ref, idx_ref, gate_ref, cnt_ref):
    mixo = jnp.dot(ret_ref[...], wo_ref[:RET_W, :], preferred_element_type=F32)
    mixo = mixo + lax.dot_general(att_ref[...], wo_ref[RET_W:, :], (((0,), (0,)), ((), ())),
                                  preferred_element_type=F32)
    x1 = x_ref[...] + g1_ref[...] * mixo
    x1_ref[...] = x1
    y = x1 * lax.rsqrt(jnp.mean(x1 * x1, axis=-1, keepdims=True) + EPS) * n2_ref[...]
    h2 = y * (1.0 + sc_ref[...]) + sh_ref[...]
    h2_ref[...] = h2
    logits = jnp.dot(h2, rw_ref[...], preferred_element_type=F32,
                     precision=lax.Precision.HIGHEST) + rb_ref[...]
    tm = logits.shape[0]
    lane = lax.broadcasted_iota(I32, (tm, LANES), 1).astype(F32)
    work = jnp.where(lane < N_EXPERTS, logits, -jnp.inf)
    sel = jnp.zeros((tm, LANES), F32)
    idx_tab = jnp.zeros((tm, LANES), F32)
    vals = []
    for k in range(TOP_K):
        m = jnp.max(work, axis=1, keepdims=True)
        idx = jnp.min(jnp.where(work == m, lane, float(LANES)), axis=1, keepdims=True)
        hit = lane == idx
        sel = jnp.where(hit, 1.0, sel)
        idx_tab = jnp.where(lane == k, idx, idx_tab)
        work = jnp.where(hit, -jnp.inf, work)
        vals.append(m)
    es = [jnp.exp(v - vals[0]) for v in vals]
    den = es[0] + es[1] + es[2] + es[3]
    gate_tab = jnp.zeros((tm, LANES), F32)
    for k in range(TOP_K):
        gate_tab = jnp.where(lane == k, es[k] / den, gate_tab)
    sel_ref[...] = sel
    idx_ref[...] = idx_tab
    gate_ref[...] = gate_tab

    @pl.when((pl.program_id(0) == 0) & (pl.program_id(1) == 0))
    def _():
        cnt_ref[...] = jnp.zeros_like(cnt_ref)

    cnt_ref[...] += jnp.sum(sel, axis=0, keepdims=True)


def _oproj(ret, att, x, wo, g1, n2g, sc2, sh2, rw_pad, rb_pad, tm):
    B, S, D = x.shape
    nt = S // tm
    row = lambda w: pl.BlockSpec((None, tm, w), lambda b, i: (b, i, 0))
    flat = lambda w: pl.BlockSpec((tm, w), lambda b, i: (b * nt + i, 0))
    vec = pl.BlockSpec((None, 1, D), lambda b, i: (b, 0, 0))
    cst = lambda shape: pl.BlockSpec(shape, lambda b, i: (0, 0))
    sd = lambda shape, dt: jax.ShapeDtypeStruct(shape, dt)
    N = B * S
    return pl.pallas_call(
        _oproj_kernel,
        grid=(B, nt),
        in_specs=[row(RET_W), pl.BlockSpec((None, DSA_W, tm), lambda b, i: (b, 0, i)), row(D), cst((D, D)), vec,
                  cst((1, D)), vec, vec,
                  cst((D, LANES)), cst((1, LANES))],
        out_specs=[flat(D), flat(D), flat(LANES), flat(LANES), flat(LANES), cst((1, LANES))],
        out_shape=[sd((N, D), F32), sd((N, D), F32), sd((N, LANES), F32), sd((N, LANES), F32),
                   sd((N, LANES), F32), sd((1, LANES), F32)],
        compiler_params=_cparams(("arbitrary", "arbitrary")),
        name="oproj",
    )(ret, att, x, wo, g1, n2g.reshape(1, D), sc2, sh2, rw_pad, rb_pad)


def _dest_kernel(sel_ref, idx_ref, pstart_ref, dest_ref, seen_ref):
    @pl.when(pl.program_id(0) == 0)
    def _():
        seen_ref[...] = jnp.zeros_like(seen_ref)

    sel = sel_ref[...]
    tm = sel.shape[0]
    earlier = lax.broadcasted_iota(I32, (tm, tm), 1) < lax.broadcasted_iota(I32, (tm, tm), 0)
    earlier = jnp.where(earlier, 1.0, 0.0).astype(BF16)
    rank = jnp.dot(earlier, sel.astype(BF16), preferred_element_type=F32) + seen_ref[...]
    dest = pstart_ref[...] + rank
    lane = lax.broadcasted_iota(I32, (tm, LANES), 1).astype(F32)
    idx_tab = idx_ref[...]
    out = jnp.zeros((tm, LANES), F32)
    for k in range(TOP_K):
        e_k = jnp.sum(jnp.where(lane == k, idx_tab, 0.0), axis=1, keepdims=True)
        d_k = jnp.sum(jnp.where(lane == e_k, dest, 0.0), axis=1, keepdims=True)
        out = jnp.where(lane == k, d_k, out)
    dest_ref[...] = out.astype(I32)
    seen_ref[...] += jnp.sum(sel, axis=0, keepdims=True)


def _dest(sel, idx_tab, pstart, tm):
    N = sel.shape[0]
    blk = pl.BlockSpec((tm, LANES), lambda i: (i, 0))
    return pl.pallas_call(
        _dest_kernel,
        grid=(N // tm,),
        in_specs=[blk, blk, pl.BlockSpec((1, LANES), lambda i: (0, 0))],
        out_specs=blk,
        out_shape=jax.ShapeDtypeStruct((N, LANES), I32),
        scratch_shapes=[pltpu.VMEM((1, LANES), F32)],
        compiler_params=_cparams(("arbitrary",)),
        name="dest",
    )(sel, idx_tab, pstart)


def _disp_kernel(zs_ref, h2_ref, dest_hbm, xs_hbm, zbuf, idx_a, idx_b, sem_ia, sem_ib, sem_row, sem_z, *, tm):
    i = pl.program_id(0)
    n = tm * TOP_K
    zrows = zbuf.shape[0]

    def idx_copy(tile, buf, sem):
        return pltpu.make_async_copy(dest_hbm.at[pl.ds(tile * n, n)], buf, sem)

    def row_copy(row, dst):
        return pltpu.make_async_copy(h2_ref.at[pl.ds(row, 1)], xs_hbm.at[pl.ds(dst, 1)], sem_row)

    def issue_rows(first_row, idx):
        def body(t, c):
            for k in range(TOP_K):
                row_copy(first_row + t, idx[t * TOP_K + k]).start()
            return c

        lax.fori_loop(0, tm, body, 0)

    def wait_rows():
        def body(t, c):
            for k in range(TOP_K):
                row_copy(0, 0).wait()
            return c

        lax.fori_loop(0, tm, body, 0)

    @pl.when(i == 0)
    def _():
        zbuf[...] = jnp.zeros_like(zbuf)
        for e in range(N_EXPERTS):
            z0 = pl.multiple_of(zs_ref[e] // SUBLANES * SUBLANES, SUBLANES)
            fill = pltpu.make_async_copy(zbuf, xs_hbm.at[pl.ds(z0, zrows)], sem_z)
            fill.start()
            fill.wait()

    idx_copy(2 * i, idx_a, sem_ia).start()
    idx_copy(2 * i + 1, idx_b, sem_ib).start()
    idx_copy(2 * i, idx_a, sem_ia).wait()
    issue_rows(0, idx_a)
    idx_copy(2 * i + 1, idx_b, sem_ib).wait()
    issue_rows(tm, idx_b)
    wait_rows()
    wait_rows()


def _dispatch(h2, dest_flat, zero_start, n_rows, zrows, tm):
    N, D = h2.shape
    n_steps = N // (2 * tm)
    grid_spec = pltpu.PrefetchScalarGridSpec(
        num_scalar_prefetch=1,
        grid=(n_steps,),
        in_specs=[pl.BlockSpec((2 * tm, D), lambda i, zs: (i, 0)), pl.BlockSpec(memory_space=pl.ANY)],
        out_specs=pl.BlockSpec(memory_space=pl.ANY),
        scratch_shapes=[pltpu.VMEM((zrows, D), F32), pltpu.SMEM((tm * TOP_K,), I32), pltpu.SMEM((tm * TOP_K,), I32),
                        pltpu.SemaphoreType.DMA, pltpu.SemaphoreType.DMA, pltpu.SemaphoreType.DMA,
                        pltpu.SemaphoreType.DMA],
    )
    return pl.pallas_call(
        functools.partial(_disp_kernel, tm=tm),
        grid_spec=grid_spec,
        out_shape=jax.ShapeDtypeStruct((n_rows, D), F32),
        compiler_params=_cparams(("arbitrary",)),
        name="disp",
    )(zero_start, h2, dest_flat)


def _ffn_kernel(be_ref, nb_ref, xs_ref, wgu_ref, bgu_ref, wd_ref, bd_ref, ys_ref, wgu_bf, wd_bf):
    i = pl.program_id(0)
    live = i < nb_ref[0]

    @pl.when(live & ((i == 0) | (be_ref[i] != be_ref[jnp.maximum(i - 1, 0)])))
    def _():
        wgu_bf[...] = wgu_ref[...].astype(BF16)
        wd_bf[...] = wd_ref[...].astype(BF16)

    @pl.when(live)
    def _():
        xb = xs_ref[...].astype(BF16)
        gu = jnp.dot(xb, wgu_bf[...], preferred_element_type=F32) + bgu_ref[...]
        gate = jnp.minimum(gu[:, :D_EXPERT], SWIGLU_LIMIT)
        up = jnp.clip(gu[:, D_EXPERT:], -SWIGLU_LIMIT, SWIGLU_LIMIT)
        glu = gate * (1.0 / (1.0 + jnp.exp(-SWIGLU_ALPHA * gate)))
        act = ((up + 1.0) * glu).astype(BF16)
        ys_ref[...] = jnp.dot(act, wd_bf[...], preferred_element_type=F32) + bd_ref[...]

    @pl.when(jnp.logical_not(live))
    def _():
        ys_ref[...] = jnp.zeros_like(ys_ref)


def _ffn(xs, block_e, n_blocks, w_gu, b_gu, w_down, b_down, tmb):
    P, D = xs.shape
    E = w_gu.shape[0]
    blk = lambda i, be, nb: (jnp.minimum(i, nb[0] - 1), 0)
    wsel = lambda i, be, nb: (be[jnp.minimum(i, nb[0] - 1)], 0, 0)
    grid_spec = pltpu.PrefetchScalarGridSpec(
        num_scalar_prefetch=2,
        grid=(P // tmb,),
        in_specs=[pl.BlockSpec((tmb, D), blk),
                  pl.BlockSpec((None, D, 2 * D_EXPERT), wsel),
                  pl.BlockSpec((None, 1, 2 * D_EXPERT), wsel),
                  pl.BlockSpec((None, D_EXPERT, D), wsel),
                  pl.BlockSpec((None, 1, D), wsel)],
        out_specs=pl.BlockSpec((tmb, D), lambda i, be, nb: (i, 0)),
        scratch_shapes=[pltpu.VMEM((D, 2 * D_EXPERT), BF16), pltpu.VMEM((D_EXPERT, D), BF16)],
    )
    return pl.pallas_call(
        _ffn_kernel,
        grid_spec=grid_spec,
        out_shape=jax.ShapeDtypeStruct((P, D), F32),
        compiler_params=_cparams(("arbitrary",)),
        name="ffn",
    )(block_e, n_blocks, xs, w_gu, b_gu.reshape(E, 1, 2 * D_EXPERT), w_down, b_down.reshape(E, 1, D))


def _comb_kernel(x1_ref, gate_ref, g2_ref, fg_ref, dest_hbm, ys_hbm, o_ref, buf_a, buf_b, idx_a, idx_b,
                 sem_ia, sem_ib, sem_ra, sem_rb, *, tm, n_steps):
    i = pl.program_id(0)
    n = tm * TOP_K

    def idx_copy(tile, buf, sem):
        return pltpu.make_async_copy(dest_hbm.at[pl.ds(tile * n, n)], buf, sem)

    def row_copy(src, buf, k, t, sem):
        return pltpu.make_async_copy(ys_hbm.at[pl.ds(src, 1)], buf.at[k, pl.ds(t, 1)], sem)

    def issue_rows(idx, buf, sem):
        def body(t, c):
            for k in range(TOP_K):
                row_copy(idx[t * TOP_K + k], buf, k, t, sem).start()
            return c

        lax.fori_loop(0, tm, body, 0)

    def wait_rows(buf, sem):
        def body(t, c):
            for k in range(TOP_K):
                row_copy(0, buf, k, 0, sem).wait()
            return c

        lax.fori_loop(0, tm, body, 0)

    def finish(buf, lo):
        gates = gate_ref[lo:lo + tm, :]
        y = jnp.zeros((tm, x1_ref.shape[1]), F32)
        for k in range(TOP_K):
            y = y + gates[:, k:k + 1] * buf[k]
        v = x1_ref[lo:lo + tm, :] + g2_ref[...] * y
        o_ref[lo:lo + tm, :] = v * lax.rsqrt(jnp.mean(v * v, axis=-1, keepdims=True) + EPS) * fg_ref[...]

    @pl.when(i == 0)
    def _():
        first = idx_copy(0, idx_a, sem_ia)
        first.start()
        first.wait()
        issue_rows(idx_a, buf_a, sem_ra)
        idx_copy(1, idx_b, sem_ib).start()

    idx_copy(2 * i + 1, idx_b, sem_ib).wait()
    issue_rows(idx_b, buf_b, sem_rb)

    @pl.when(i + 1 < n_steps)
    def _():
        idx_copy(2 * i + 2, idx_a, sem_ia).start()

    wait_rows(buf_a, sem_ra)
    finish(buf_a, 0)

    @pl.when(i + 1 < n_steps)
    def _():
        idx_copy(2 * i + 2, idx_a, sem_ia).wait()
        issue_rows(idx_a, buf_a, sem_ra)
        idx_copy(2 * i + 3, idx_b, sem_ib).start()

    wait_rows(buf_b, sem_rb)
    finish(buf_b, tm)


def _combine(x1, gate_tab, g2, final_g, dest_flat, ys, S, tm):
    N, D = x1.shape
    n_steps = N // (2 * tm)
    per_b = S // (2 * tm)
    return pl.pallas_call(
        functools.partial(_comb_kernel, tm=tm, n_steps=n_steps),
        grid=(n_steps,),
        in_specs=[pl.BlockSpec((2 * tm, D), lambda i: (i, 0)),
                  pl.BlockSpec((2 * tm, LANES), lambda i: (i, 0)),
                  pl.BlockSpec((None, 1, D), lambda i: (i // per_b, 0, 0)),
                  pl.BlockSpec((1, D), lambda i: (0, 0)),
                  pl.BlockSpec(memory_space=pl.ANY),
                  pl.BlockSpec(memory_space=pl.ANY)],
        out_specs=pl.BlockSpec((2 * tm, D), lambda i: (i, 0)),
        out_shape=jax.ShapeDtypeStruct((N, D), F32),
        scratch_shapes=[pltpu.VMEM((TOP_K, tm, D), F32), pltpu.VMEM((TOP_K, tm, D), F32),
                        pltpu.SMEM((tm * TOP_K,), I32), pltpu.SMEM((tm * TOP_K,), I32),
                        pltpu.SemaphoreType.DMA, pltpu.SemaphoreType.DMA, pltpu.SemaphoreType.DMA,
                        pltpu.SemaphoreType.DMA],
        compiler_params=_cparams(("arbitrary",)),
        name="comb",
    )(x1, gate_tab, g2, final_g.reshape(1, D), dest_flat, ys)


def _tile(n, pref):
    t = min(pref, n)
    assert n % t == 0, (n, t)
    return t


def _layer(x, c, ada_w, ada_b, norm1_g, w_in, mix_scale, w_o, norm2_g,
           router_w, router_b, w_gu, b_gu, w_down, b_down, final_g):
    B, S, D = x.shape
    N = B * S
    mod = _mod(c, ada_w, ada_b).reshape(B, 6, 1, D)
    sh1, sc1, g1, sh2, sc2, g2 = (mod[:, j] for j in range(6))

    w_pad = jnp.pad(w_in, ((0, 0), (0, IN_COLS_PAD - IN_COLS))).astype(BF16)
    rq, rk, rv, rg, aqt, ak, avt, iqt, ik, iwt = _inproj(x, norm1_g, sc1, sh1, w_pad, _tile(S, 512))
    ms = mix_scale.reshape(1, RET_W + DSA_W)
    ret = _retention(rq, rk, rv, rg, ms[:, :RET_W])
    att = _dsa(iqt, iwt, aqt, ik, ak, avt, ms[:, RET_W:], _tile(S, 256), 128)

    rw_pad = jnp.pad(router_w, ((0, 0), (0, LANES - N_EXPERTS)))
    rb_pad = jnp.pad(router_b, (0, LANES - N_EXPERTS)).reshape(1, LANES)
    x1, h2, sel, idx_tab, gate_tab, counts = _oproj(ret, att, x, w_o.astype(BF16), g1, norm2_g, sc2, sh2,
                                                    rw_pad, rb_pad, _tile(S, 512))

    tmb = FFN_ROWS
    n_rows = (N * TOP_K + N_EXPERTS * (tmb - 1)) // tmb * tmb + 2 * tmb
    cnt = counts[0, :N_EXPERTS].astype(I32)
    padded = (cnt + tmb - 1) // tmb * tmb
    ends = jnp.cumsum(padded)
    starts = ends - padded
    pstart = jnp.pad(starts.astype(F32), (0, LANES - N_EXPERTS)).reshape(1, LANES)
    n_blocks = (ends[-1] // tmb).reshape(1)
    first_row = jnp.arange(n_rows // tmb, dtype=I32) * tmb
    block_e = jnp.minimum(jnp.sum((ends[None, :] <= first_row[:, None]).astype(I32), axis=1), N_EXPERTS - 1)

    tmd = _tile(N, 256)
    dest_tab = _dest(sel, idx_tab, pstart, tmd)
    dest_flat = dest_tab[:, :TOP_K].reshape(N * TOP_K)
    xs = _dispatch(h2, dest_flat, starts + cnt, n_rows, tmb + SUBLANES, tmd)
    ys = _ffn(xs, block_e, n_blocks, w_gu, b_gu, w_down, b_down, tmb)
    out = _combine(x1, gate_tab, g2, final_g, dest_flat, ys, S, _tile(S, 256))
    return out.reshape(B, S, D)


def kernel(x, c, ada_w, ada_b, norm1_g, w_in, mix_scale, w_o, norm2_g, router_w, router_b, w_gu, b_gu,
           w_down, b_down, final_g):
    assert ada_w.shape[0] == 1, "single-layer stack"
    return _layer(x, c, ada_w[0], ada_b[0], norm1_g[0], w_in[0], mix_scale[0], w_o[0], norm2_g[0],
                  router_w[0], router_b[0], w_gu[0], b_gu[0], w_down[0], b_down[0], final_g)
```

```python
import functools

import numpy as np
import jax
import jax.numpy as jnp
from jax import lax
from jax.experimental import pallas as pl
from jax.experimental.pallas import tpu as pltpu

F32 = jnp.float32
BF16 = jnp.bfloat16
I32 = jnp.int32

D_MODEL = 1024
RET_HEADS = 4
RET_DK = 64
RET_DV = 128
RET_CHUNK = 128
DSA_HEADS = 8
DSA_KV_HEADS = 2
DSA_HD = 64
IDX_HEADS = 8
IDX_HD = 64
TOPK_MAX = 256
N_EXPERTS = 32
TOP_K = 4
D_EXPERT = D_MODEL
SWIGLU_LIMIT = 7.0
SWIGLU_ALPHA = 1.702
EPS = 1e-6

RET_W = RET_HEADS * RET_DV
DSA_W = DSA_HEADS * DSA_HD
IN_COLS = 2888
IN_COLS_PAD = 2944

KAUG = 128
VAUG = 80
ALIBI_SPLIT = 64
FFN_ROWS = 512
BISECT_VALUE_STEPS = 8
BISECT_MAX_STEPS = 64

LANES = 128
SUBLANES = 8
VMEM_LIMIT = 56 * 1024 * 1024
NEG_BIG = -1e30
F32_LOWEST = float(np.finfo(np.float32).min)


def _cparams(sem):
    return pltpu.CompilerParams(dimension_semantics=sem, vmem_limit_bytes=VMEM_LIMIT)


def _mod_kernel(c_ref, w_ref, b_ref, o_ref):
    c = c_ref[...]
    s = c * (1.0 / (1.0 + jnp.exp(-c)))
    o_ref[...] = jnp.dot(s, w_ref[...], preferred_element_type=F32,
                         precision=lax.Precision.HIGHEST) + b_ref[...]


def _mod(c, ada_w, ada_b):
    B, D = c.shape
    n_out = ada_w.shape[1]
    rows = 8
    c8 = jnp.zeros((rows, D), F32).at[:B].set(c)
    out = pl.pallas_call(
        _mod_kernel,
        grid=(n_out // D,),
        in_specs=[pl.BlockSpec((rows, D), lambda j: (0, 0)),
                  pl.BlockSpec((D, D), lambda j: (0, j)),
                  pl.BlockSpec((1, D), lambda j: (0, j))],
        out_specs=pl.BlockSpec((rows, D), lambda j: (0, j)),
        out_shape=jax.ShapeDtypeStruct((rows, n_out), F32),
        compiler_params=_cparams(("arbitrary",)),
        name="mod",
    )(c8, ada_w, ada_b.reshape(1, n_out))
    return out[:B]


def _inproj_kernel(x_ref, g_ref, sc_ref, sh_ref, w_ref,
                   rq_ref, rk_ref, rv_ref, rg_ref, aqt_ref, ak_ref, avt_ref, iqt_ref, ik_ref, iwt_ref):
    x = x_ref[...]
    ms = jnp.mean(x * x, axis=-1, keepdims=True)
    y = x * lax.rsqrt(ms + EPS) * g_ref[...]
    hb = (y * (1.0 + sc_ref[...]) + sh_ref[...]).astype(BF16)

    def proj(lo, hi):
        return jnp.dot(hb, w_ref[:, lo:hi], preferred_element_type=F32)

    tm = x.shape[0]
    d = DSA_HD
    rq_ref[...] = proj(0, 256).astype(BF16)
    rk_ref[...] = (proj(256, 512) * (RET_DK ** -0.5)).astype(BF16)
    rv_ref[...] = proj(512, 1024).astype(BF16)
    rg_ref[...] = proj(1024, 1536).astype(BF16)
    aqt_ref[...] = (proj(1536, 2048) * (d ** -0.5)).T.astype(BF16)
    kk = proj(2048, 2176)
    pos = pl.program_id(1) * tm + lax.broadcasted_iota(I32, (tm, d), 0)
    col = lax.broadcasted_iota(I32, (tm, d), 1)
    posblk = jnp.where(col == 0, pos // ALIBI_SPLIT, jnp.where(col == 1, pos % ALIBI_SPLIT, 0)).astype(F32)
    for g in range(DSA_KV_HEADS):
        ak_ref[:, g * KAUG:g * KAUG + d] = kk[:, g * d:(g + 1) * d].astype(BF16)
        ak_ref[:, g * KAUG + d:(g + 1) * KAUG] = posblk.astype(BF16)
    vt = proj(2176, 2304).T
    r16 = lax.broadcasted_iota(I32, (VAUG - d, tm), 0)
    onesblk = jnp.where(r16 == 0, 1.0, 0.0).astype(BF16)
    for g in range(DSA_KV_HEADS):
        avt_ref[g * VAUG:g * VAUG + d, :] = vt[g * d:(g + 1) * d, :].astype(BF16)
        avt_ref[g * VAUG + d:(g + 1) * VAUG, :] = onesblk
    iqt_ref[...] = proj(2304, 2816).T.astype(BF16)
    last = proj(2816, 2944)
    ik_ref[...] = last[:, :IDX_HD].astype(BF16)
    iwt_ref[...] = last.T[IDX_HD:IDX_HD + IDX_HEADS, :] * ((IDX_HD ** -0.5) * (IDX_HEADS ** -0.5))


def _inproj(x, norm_g, sc, sh, w_pad, tm):
    B, S, D = x.shape
    row = lambda w: pl.BlockSpec((None, tm, w), lambda b, i: (b, i, 0))
    colT = lambda h: pl.BlockSpec((None, h, tm), lambda b, i: (b, 0, i))
    vec = pl.BlockSpec((None, 1, D), lambda b, i: (b, 0, 0))
    sd = lambda shape, dt: jax.ShapeDtypeStruct(shape, dt)
    G = DSA_KV_HEADS
    return pl.pallas_call(
        _inproj_kernel,
        grid=(B, S // tm),
        in_specs=[row(D), pl.BlockSpec((1, D), lambda b, i: (0, 0)), vec, vec,
                  pl.BlockSpec((D, IN_COLS_PAD), lambda b, i: (0, 0))],
        out_specs=[row(256), row(256), row(512), row(512), colT(DSA_W), row(G * KAUG), colT(G * VAUG),
                   colT(IDX_HEADS * IDX_HD), row(IDX_HD), colT(IDX_HEADS)],
        out_shape=[sd((B, S, 256), BF16), sd((B, S, 256), BF16), sd((B, S, 512), BF16),
                   sd((B, S, 512), BF16), sd((B, DSA_W, S), BF16), sd((B, S, G * KAUG), BF16),
                   sd((B, G * VAUG, S), BF16), sd((B, IDX_HEADS * IDX_HD, S), BF16),
                   sd((B, S, IDX_HD), BF16), sd((B, IDX_HEADS, S), F32)],
        compiler_params=_cparams(("parallel", "parallel")),
        name="inproj",
    )(x, norm_g.reshape(1, D), sc, sh, w_pad)


def _ret_kernel(rq_ref, rk_ref, rv_ref, rg_ref, din_ref, qd_ref, kd_ref, cd_ref, ms_ref, o_ref, state_ref):
    @pl.when(pl.program_id(1) == 0)
    def _():
        state_ref[...] = jnp.zeros_like(state_ref)

    for h in range(RET_HEADS):
        q = rq_ref[:, h * RET_DK:(h + 1) * RET_DK]
        k = rk_ref[:, h * RET_DK:(h + 1) * RET_DK]
        v = rv_ref[:, h * RET_DV:(h + 1) * RET_DV]
        r_prev = state_ref[h]
        s = lax.dot_general(q, k, (((1,), (1,)), ((), ())), preferred_element_type=F32) * din_ref[h]
        o = jnp.dot(s.astype(BF16), v, preferred_element_type=F32)
        o = o + jnp.dot(q, r_prev.astype(BF16), preferred_element_type=F32) * qd_ref[h]
        vd = (v.astype(F32) * kd_ref[h]).astype(BF16)
        kv = lax.dot_general(k, vd, (((0,), (0,)), ((), ())), preferred_element_type=F32)
        state_ref[h] = r_prev * cd_ref[h] + kv
        o = o * lax.rsqrt(jnp.mean(o * o, axis=-1, keepdims=True) + EPS)
        g = rg_ref[:, h * RET_DV:(h + 1) * RET_DV].astype(F32)
        gate = g * (1.0 / (1.0 + jnp.exp(-g)))
        o_ref[:, h * RET_DV:(h + 1) * RET_DV] = (gate * o * ms_ref[:, h * RET_DV:(h + 1) * RET_DV]).astype(BF16)


def _ret_consts(C):
    H = RET_HEADS
    log_g = np.log1p(-np.exp2(-5.0 - np.arange(H, dtype=np.float64)))
    pos = np.arange(C, dtype=np.float64)
    diff = pos[:, None] - pos[None, :]
    d_inner = np.where(diff[None] >= 0, np.exp(np.maximum(diff, 0.0)[None] * log_g[:, None, None]), 0.0)
    q_decay = np.exp((pos + 1.0)[None] * log_g[:, None])
    k_decay = np.exp((C - 1.0 - pos)[None] * log_g[:, None])
    chunk_decay = np.exp(C * log_g)
    qd = np.broadcast_to(q_decay[:, :, None], (H, C, RET_DV))
    kd = np.broadcast_to(k_decay[:, :, None], (H, C, RET_DV))
    cd = np.broadcast_to(chunk_decay[:, None, None], (H, 1, RET_DV))
    f = lambda a: jnp.asarray(np.ascontiguousarray(a), F32)
    return f(d_inner), f(qd), f(kd), f(cd)


def _retention(rq, rk, rv, rg, ms_ret):
    B, S, _ = rq.shape
    C = min(RET_CHUNK, S)
    din, qd, kd, cd = _ret_consts(C)
    row = lambda w: pl.BlockSpec((None, C, w), lambda b, n: (b, n, 0))
    full = lambda a: pl.BlockSpec(a.shape, lambda b, n: (0,) * a.ndim)
    return pl.pallas_call(
        _ret_kernel,
        grid=(B, S // C),
        in_specs=[row(256), row(256), row(512), row(512), full(din), full(qd), full(kd), full(cd),
                  pl.BlockSpec((1, RET_W), lambda b, n: (0, 0))],
        out_specs=row(RET_W),
        out_shape=jax.ShapeDtypeStruct((B, S, RET_W), BF16),
        scratch_shapes=[pltpu.VMEM((RET_HEADS, RET_DK, RET_DV), F32)],
        compiler_params=_cparams(("parallel", "arbitrary")),
        name="ret",
    )(rq, rk, rv, rg, din, qd, kd, cd, ms_ret)


def _f32_key(x):
    i = lax.bitcast_convert_type(x, I32)
    return i ^ ((i >> 31) & 0x7FFFFFFF)


def _key_f32(k):
    return lax.bitcast_convert_type(k ^ ((k >> 31) & 0x7FFFFFFF), F32)


def _dsa_kernel(iqt_ref, iwt_ref, aqt_ref, ik_ref, ak_ref, avt_ref, ms_ref, o_ref, score_ref, qa_ref, sa_ref, sb_ref, mask_ref, *acc_refs,
                tq, tks, n_sel):
    H, G, R, d = DSA_HEADS, DSA_KV_HEADS, DSA_HEADS // DSA_KV_HEADS, DSA_HD
    t0 = pl.program_id(1) * tq
    nsub = (t0 + tq) // tks
    tka = 2 * tks
    npair = ((t0 + tq) // tka + 1) // 2
    kf = float(n_sel)
    qpos = t0 + lax.broadcasted_iota(I32, (1, tq), 1)
    krow = lax.broadcasted_iota(I32, (tks, tq), 0)

    wrow = [iwt_ref[h:h + 1, :] for h in range(IDX_HEADS)]

    def score_pair(i, carry):
        for u in range(2 * tka // tks):
            r0 = pl.multiple_of(i * 2 * tka + u * tks, tks)
            kc = ik_ref[pl.ds(r0, tks), :]
            acc = jnp.zeros((tks, tq), F32)
            for h in range(IDX_HEADS):
                rel = jnp.dot(kc, iqt_ref[h * IDX_HD:(h + 1) * IDX_HD, :], preferred_element_type=F32)
                acc = acc + jnp.maximum(rel, 0.0) * wrow[h]
            score_ref[pl.ds(r0, tks), :] = jnp.where(r0 + krow <= qpos, acc, -jnp.inf)
        return carry

    lax.fori_loop(0, npair, score_pair, 0)

    def fold8(x, op):
        acc = x[0:8, :]
        for i in range(1, tks // 8):
            acc = op(acc, x[8 * i:8 * (i + 1), :])
        return acc

    def count(th, strict):
        def body(j, acc):
            for u in range(2):
                s = score_ref[pl.ds(pl.multiple_of((2 * j + u) * tks, tks), tks), :]
                hit = (s > th) if strict else (s >= th)
                acc = acc + fold8(jnp.where(hit, 1.0, 0.0), jnp.add)
            return acc

        acc = lax.fori_loop(0, nsub // 2, body, jnp.zeros((8, tq), F32))
        return jnp.sum(acc, axis=0, keepdims=True)

    def minmax(j, carry):
        mx, mn = carry
        s = score_ref[pl.ds(pl.multiple_of(j * tks, tks), tks), :]
        mx = jnp.maximum(mx, fold8(s, jnp.maximum))
        mn = jnp.minimum(mn, fold8(jnp.where(s == -jnp.inf, jnp.inf, s), jnp.minimum))
        return mx, mn

    mx, mn = lax.fori_loop(0, nsub, minmax, (jnp.full((8, tq), -jnp.inf, F32), jnp.full((8, tq), jnp.inf, F32)))
    hi0 = jnp.max(mx, axis=0, keepdims=True)
    lo0 = jnp.min(mn, axis=0, keepdims=True)

    def probe(lo, hi, it):
        lk, hk = _f32_key(lo), _f32_key(hi)
        mk = (lk >> 1) + (hk >> 1) + (lk & hk & 1)
        mv = lo + (hi - lo) * 0.5
        early = (jnp.zeros((1, tq), I32) + it) < BISECT_VALUE_STEPS
        mid = jnp.where(early & (mv > lo) & (mv < hi), mv, _key_f32(mk))
        return mid, jnp.max(jnp.where(mk != lk, 1.0, 0.0))

    def bis_cond(c):
        return (c[4] > 0.0) & (c[5] < BISECT_MAX_STEPS)

    def bis_body(c):
        lo, hi, c_lo, c_hi, _, it = c
        mid, _ = probe(lo, hi, it)
        cnt = count(mid, False)
        ge = cnt >= kf
        up = ge | (cnt == kf)
        dn = (~ge) | (cnt == kf)
        lo, c_lo = jnp.where(up, mid, lo), jnp.where(up, cnt, c_lo)
        hi, c_hi = jnp.where(dn, mid, hi), jnp.where(dn, cnt, c_hi)
        _, active = probe(lo, hi, it + 1)
        return lo, hi, c_lo, c_hi, active, it + 1

    zero = jnp.zeros((1, tq), F32)
    n_pos = count(zero, True)
    n_nonneg = count(zero, False)
    keep_all = qpos + 1 <= n_sel
    settled = keep_all | ((n_nonneg >= kf) & (n_pos < kf))
    above = n_pos >= kf
    c_lo0 = jnp.where(settled | above, n_nonneg, (qpos + 1).astype(F32))
    c_hi0 = jnp.where(settled | ~above, n_nonneg, count(hi0, False))
    lo0 = jnp.where(settled | above, zero, lo0)
    hi0 = jnp.where(settled | ~above, zero, hi0)
    _, active0 = probe(lo0, hi0, jnp.int32(0))
    lo, hi, c_lo, c_hi, _, _ = lax.while_loop(bis_cond, bis_body, (lo0, hi0, c_lo0, c_hi0, active0, jnp.int32(0)))
    at_hi = c_hi >= kf
    thr = jnp.where(keep_all, F32_LOWEST, jnp.where(at_hi, hi, lo))
    excess = jnp.where(keep_all, 0.0, jnp.where(at_hi, c_hi, c_lo) - kf)

    @pl.when(jnp.max(excess) > 0.0)
    def _():
        budget = kf - count(thr, True)
        earlier = lax.broadcasted_iota(I32, (tks, tks), 1) < lax.broadcasted_iota(I32, (tks, tks), 0)
        earlier = jnp.where(earlier, 1.0, 0.0).astype(BF16)

        def fix(j, seen):
            r0 = pl.multiple_of(j * tks, tks)
            s = score_ref[pl.ds(r0, tks), :]
            eq = s == thr
            eqf = jnp.where(eq, 1.0, 0.0)
            rank = jnp.dot(earlier, eqf.astype(BF16), preferred_element_type=F32) + seen
            score_ref[pl.ds(r0, tks), :] = jnp.where(eq & (rank >= budget), -jnp.inf, s)
            return seen + jnp.sum(eqf, axis=0, keepdims=True)

        lax.fori_loop(0, nsub, fix, jnp.zeros((1, tq), F32))

    arow = lax.broadcasted_iota(I32, (KAUG - d, tq), 0)
    for h in range(H):
        slope = float(2.0 ** (-8.0 * (h + 1) / H))
        qa_ref[h, 0:d, :] = aqt_ref[h * d:(h + 1) * d, :]
        qa_ref[h, d:KAUG, :] = jnp.where(arow == 0, slope * ALIBI_SPLIT, jnp.where(arow == 1, slope, 0.0)).astype(BF16)
    for acc in acc_refs:
        acc[...] = jnp.zeros_like(acc)

    def logits(j, h):
        ka = ak_ref[pl.ds(pl.multiple_of(j * tka, tka), tka), (h // R) * KAUG:(h // R + 1) * KAUG]
        return jnp.dot(ka, qa_ref[h], preferred_element_type=F32)

    def step(j, j_next, cur_ref, next_ref, ms):
        r0 = pl.multiple_of(j * tka, tka)
        mask_ref[...] = jnp.where(score_ref[pl.ds(r0, tka), :] >= thr, 0.0, NEG_BIG)
        nms = []
        for h in range(H):
            g = h // R
            next_ref[h] = logits(j_next, h)
            s = cur_ref[h] + mask_ref[...]
            m_new = jnp.maximum(ms[h], jnp.max(s, axis=0, keepdims=True))
            p = jnp.exp(s - m_new).astype(BF16)
            va = avt_ref[g * VAUG:(g + 1) * VAUG, pl.ds(r0, tka)]
            acc = acc_refs[h]
            acc[...] = acc[...] * jnp.exp(ms[h] - m_new) + jnp.dot(va, p, preferred_element_type=F32)
            nms.append(m_new)
        return tuple(nms)

    for h in range(H):
        sa_ref[h] = logits(0, h)

    def att_pair(i, ms):
        ms = step(2 * i, 2 * i + 1, sa_ref, sb_ref, ms)
        return step(2 * i + 1, jnp.minimum(2 * i + 2, 2 * npair - 1), sb_ref, sa_ref, ms)

    lax.fori_loop(0, npair, att_pair, tuple(jnp.full((1, tq), NEG_BIG, F32) for _ in range(H)))
    for h in range(H):
        a = acc_refs[h][...]
        o = a[0:d, :] / a[d:d + 1, :]
        o_ref[h * d:(h + 1) * d, :] = (o * ms_ref[h * d:(h + 1) * d, :]).astype(BF16)


def _dsa(iqt, iwt, aqt, ik, ak, avt, ms_att, tq, tks):
    B, _, S = iqt.shape
    assert S <= ALIBI_SPLIT * 256 and tq % (2 * tks) == 0
    assert (S // (2 * tks)) % 2 == 0
    n_sel = min(TOPK_MAX, S // 4)
    G = DSA_KV_HEADS
    colT = lambda h: pl.BlockSpec((None, h, tq), lambda b, i: (b, 0, i))
    msb = jnp.broadcast_to(ms_att.reshape(DSA_W, 1), (DSA_W, tq))
    return pl.pallas_call(
        functools.partial(_dsa_kernel, tq=tq, tks=tks, n_sel=n_sel),
        grid=(B, S // tq),
        in_specs=[colT(IDX_HEADS * IDX_HD), colT(IDX_HEADS), colT(DSA_W),
                  pl.BlockSpec((None, S, IDX_HD), lambda b, i: (b, 0, 0)),
                  pl.BlockSpec((None, S, G * KAUG), lambda b, i: (b, 0, 0)),
                  pl.BlockSpec((None, G * VAUG, S), lambda b, i: (b, 0, 0)),
                  pl.BlockSpec((DSA_W, tq), lambda b, i: (0, 0))],
        out_specs=colT(DSA_W),
        out_shape=jax.ShapeDtypeStruct((B, DSA_W, S), BF16),
        scratch_shapes=[pltpu.VMEM((S, tq), F32), pltpu.VMEM((DSA_HEADS, KAUG, tq), BF16),
                        pltpu.VMEM((DSA_HEADS, 2 * tks, tq), F32), pltpu.VMEM((DSA_HEADS, 2 * tks, tq), F32),
                        pltpu.VMEM((2 * tks, tq), F32)]
        + [pltpu.VMEM((VAUG, tq), F32) for _ in range(DSA_HEADS)],
        compiler_params=_cparams(("parallel", "arbitrary")),
        name="dsa",
    )(iqt, iwt, aqt, ik, ak, avt, msb)


def _oproj_kernel(ret_ref, att_ref, x_ref, wo_ref, g1_ref, n2_ref, sc_ref, sh_ref, rw_ref, rb_ref,
                  x1_ref, h2_ref, sel_ref, idx_ref, gate_ref, cnt_ref):
    mixo = jnp.dot(ret_ref[...], wo_ref[:RET_W, :], preferred_element_type=F32)
    mixo = mixo + lax.dot_general(att_ref[...], wo_ref[RET_W:, :], (((0,), (0,)), ((), ())),
                                  preferred_element_type=F32)
    x1 = x_ref[...] + g1_ref[...] * mixo
    x1_ref[...] = x1
    y = x1 * lax.rsqrt(jnp.mean(x1 * x1, axis=-1, keepdims=True) + EPS) * n2_ref[...]
    h2 = y * (1.0 + sc_ref[...]) + sh_ref[...]
    h2_ref[...] = h2
    logits = jnp.dot(h2, rw_ref[...], preferred_element_type=F32,
                     precision=lax.Precision.HIGHEST) + rb_ref[...]
    tm = logits.shape[0]
    lane = lax.broadcasted_iota(I32, (tm, LANES), 1).astype(F32)
    work = jnp.where(lane < N_EXPERTS, logits, -jnp.inf)
    sel = jnp.zeros((tm, LANES), F32)
    idx_tab = jnp.zeros((tm, LANES), F32)
    vals = []
    for k in range(TOP_K):
        m = jnp.max(work, axis=1, keepdims=True)
        idx = jnp.min(jnp.where(work == m, lane, float(LANES)), axis=1, keepdims=True)
        hit = lane == idx
        sel = jnp.where(hit, 1.0, sel)
        idx_tab = jnp.where(lane == k, idx, idx_tab)
        work = jnp.where(hit, -jnp.inf, work)
        vals.append(m)
    es = [jnp.exp(v - vals[0]) for v in vals]
    den = es[0] + es[1] + es[2] + es[3]
    gate_tab = jnp.zeros((tm, LANES), F32)
    for k in range(TOP_K):
        gate_tab = jnp.where(lane == k, es[k] / den, gate_tab)
    sel_ref[...] = sel
    idx_ref[...] = idx_tab
    gate_ref[...] = gate_tab

    @pl.when((pl.program_id(0) == 0) & (pl.program_id(1) == 0))
    def _():
        cnt_ref[...] = jnp.zeros_like(cnt_ref)

    cnt_ref[...] += jnp.sum(sel, axis=0, keepdims=True)


def _oproj(ret, att, x, wo, g1, n2g, sc2, sh2, rw_pad, rb_pad, tm):
    B, S, D = x.shape
    nt = S // tm
    row = lambda w: pl.BlockSpec((None, tm, w), lambda b, i: (b, i, 0))
    flat = lambda w: pl.BlockSpec((tm, w), lambda b, i: (b * nt + i, 0))
    vec = pl.BlockSpec((None, 1, D), lambda b, i: (b, 0, 0))
    cst = lambda shape: pl.BlockSpec(shape, lambda b, i: (0, 0))
    sd = lambda shape, dt: jax.ShapeDtypeStruct(shape, dt)
    N = B * S
    return pl.pallas_call(
        _oproj_kernel,
        grid=(B, nt),
        in_specs=[row(RET_W), pl.BlockSpec((None, DSA_W, tm), lambda b, i: (b, 0, i)), row(D), cst((D, D)), vec,
                  cst((1, D)), vec, vec,
                  cst((D, LANES)), cst((1, LANES))],
        out_specs=[flat(D), flat(D), flat(LANES), flat(LANES), flat(LANES), cst((1, LANES))],
        out_shape=[sd((N, D), F32), sd((N, D), F32), sd((N, LANES), F32), sd((N, LANES), F32),
                   sd((N, LANES), F32), sd((1, LANES), F32)],
        compiler_params=_cparams(("arbitrary", "arbitrary")),
        name="oproj",
    )(ret, att, x, wo, g1, n2g.reshape(1, D), sc2, sh2, rw_pad, rb_pad)


def _dest_kernel(sel_ref, idx_ref, pstart_ref, dest_ref, seen_ref):
    @pl.when(pl.program_id(0) == 0)
    def _():
        seen_ref[...] = jnp.zeros_like(seen_ref)

    sel = sel_ref[...]
    tm = sel.shape[0]
    earlier = lax.broadcasted_iota(I32, (tm, tm), 1) < lax.broadcasted_iota(I32, (tm, tm), 0)
    earlier = jnp.where(earlier, 1.0, 0.0).astype(BF16)
    rank = jnp.dot(earlier, sel.astype(BF16), preferred_element_type=F32) + seen_ref[...]
    dest = pstart_ref[...] + rank
    lane = lax.broadcasted_iota(I32, (tm, LANES), 1).astype(F32)
    idx_tab = idx_ref[...]
    out = jnp.zeros((tm, LANES), F32)
    for k in range(TOP_K):
        e_k = jnp.sum(jnp.where(lane == k, idx_tab, 0.0), axis=1, keepdims=True)
        d_k = jnp.sum(jnp.where(lane == e_k, dest, 0.0), axis=1, keepdims=True)
        out = jnp.where(lane == k, d_k, out)
    dest_ref[...] = out.astype(I32)
    seen_ref[...] += jnp.sum(sel, axis=0, keepdims=True)


def _dest(sel, idx_tab, pstart, tm):
    N = sel.shape[0]
    blk = pl.BlockSpec((tm, LANES), lambda i: (i, 0))
    return pl.pallas_call(
        _dest_kernel,
        grid=(N // tm,),
        in_specs=[blk, blk, pl.BlockSpec((1, LANES), lambda i: (0, 0))],
        out_specs=blk,
        out_shape=jax.ShapeDtypeStruct((N, LANES), I32),
        scratch_shapes=[pltpu.VMEM((1, LANES), F32)],
        compiler_params=_cparams(("arbitrary",)),
        name="dest",
    )(sel, idx_tab, pstart)


def _disp_kernel(zs_ref, h2_ref, dest_hbm, xs_hbm, zbuf, idx_a, idx_b, sem_ia, sem_ib, sem_row, sem_z, *, tm):
    i = pl.program_id(0)
    n = tm * TOP_K
    zrows = zbuf.shape[0]

    def idx_copy(tile, buf, sem):
        return pltpu.make_async_copy(dest_hbm.at[pl.ds(tile * n, n)], buf, sem)

    def row_copy(row, dst):
        return pltpu.make_async_copy(h2_ref.at[pl.ds(row, 1)], xs_hbm.at[pl.ds(dst, 1)], sem_row)

    def issue_rows(first_row, idx):
        def body(t, c):
            for k in range(TOP_K):
                row_copy(first_row + t, idx[t * TOP_K + k]).start(priority=k % 2)
            return c

        lax.fori_loop(0, tm, body, 0)

    def wait_rows():
        def body(t, c):
            for k in range(TOP_K):
                row_copy(0, 0).wait()
            return c

        lax.fori_loop(0, tm, body, 0)

    @pl.when(i == 0)
    def _():
        zbuf[...] = jnp.zeros_like(zbuf)
        for e in range(N_EXPERTS):
            z0 = pl.multiple_of(zs_ref[e] // SUBLANES * SUBLANES, SUBLANES)
            fill = pltpu.make_async_copy(zbuf, xs_hbm.at[pl.ds(z0, zrows)], sem_z)
            fill.start()
            fill.wait()

    idx_copy(2 * i, idx_a, sem_ia).start()
    idx_copy(2 * i + 1, idx_b, sem_ib).start()
    idx_copy(2 * i, idx_a, sem_ia).wait()
    issue_rows(0, idx_a)
    idx_copy(2 * i + 1, idx_b, sem_ib).wait()
    issue_rows(tm, idx_b)
    wait_rows()
    wait_rows()


def _dispatch(h2, dest_flat, zero_start, n_rows, zrows, tm):
    N, D = h2.shape
    n_steps = N // (2 * tm)
    grid_spec = pltpu.PrefetchScalarGridSpec(
        num_scalar_prefetch=1,
        grid=(n_steps,),
        in_specs=[pl.BlockSpec((2 * tm, D), lambda i, zs: (i, 0)), pl.BlockSpec(memory_space=pl.ANY)],
        out_specs=pl.BlockSpec(memory_space=pl.ANY),
        scratch_shapes=[pltpu.VMEM((zrows, D), F32), pltpu.SMEM((tm * TOP_K,), I32), pltpu.SMEM((tm * TOP_K,), I32),
                        pltpu.SemaphoreType.DMA, pltpu.SemaphoreType.DMA, pltpu.SemaphoreType.DMA,
                        pltpu.SemaphoreType.DMA],
    )
    return pl.pallas_call(
        functools.partial(_disp_kernel, tm=tm),
        grid_spec=grid_spec,
        out_shape=jax.ShapeDtypeStruct((n_rows, D), F32),
        compiler_params=_cparams(("arbitrary",)),
        name="disp",
    )(zero_start, h2, dest_flat)


def _ffn_kernel(be_ref, nb_ref, xs_ref, wgu_ref, bgu_ref, wd_ref, bd_ref, ys_ref, wgu_bf, wd_bf):
    i = pl.program_id(0)
    live = i < nb_ref[0]

    @pl.when(live & ((i == 0) | (be_ref[i] != be_ref[jnp.maximum(i - 1, 0)])))
    def _():
        wgu_bf[...] = wgu_ref[...].astype(BF16)
        wd_bf[...] = wd_ref[...].astype(BF16)

    @pl.when(live)
    def _():
        xb = xs_ref[...].astype(BF16)
        gu = jnp.dot(xb, wgu_bf[...], preferred_element_type=F32) + bgu_ref[...]
        gate = jnp.minimum(gu[:, :D_EXPERT], SWIGLU_LIMIT)
        up = jnp.clip(gu[:, D_EXPERT:], -SWIGLU_LIMIT, SWIGLU_LIMIT)
        glu = gate * (1.0 / (1.0 + jnp.exp(-SWIGLU_ALPHA * gate)))
        act = ((up + 1.0) * glu).astype(BF16)
        ys_ref[...] = jnp.dot(act, wd_bf[...], preferred_element_type=F32) + bd_ref[...]

    @pl.when(jnp.logical_not(live))
    def _():
        ys_ref[...] = jnp.zeros_like(ys_ref)


def _ffn(xs, block_e, n_blocks, w_gu, b_gu, w_down, b_down, tmb):
    P, D = xs.shape
    E = w_gu.shape[0]
    blk = lambda i, be, nb: (jnp.minimum(i, nb[0] - 1), 0)
    wsel = lambda i, be, nb: (be[jnp.minimum(i, nb[0] - 1)], 0, 0)
    grid_spec = pltpu.PrefetchScalarGridSpec(
        num_scalar_prefetch=2,
        grid=(P // tmb,),
        in_specs=[pl.BlockSpec((tmb, D), blk),
                  pl.BlockSpec((None, D, 2 * D_EXPERT), wsel),
                  pl.BlockSpec((None, 1, 2 * D_EXPERT), wsel),
                  pl.BlockSpec((None, D_EXPERT, D), wsel),
                  pl.BlockSpec((None, 1, D), wsel)],
        out_specs=pl.BlockSpec((tmb, D), lambda i, be, nb: (i, 0)),
        scratch_shapes=[pltpu.VMEM((D, 2 * D_EXPERT), BF16), pltpu.VMEM((D_EXPERT, D), BF16)],
    )
    return pl.pallas_call(
        _ffn_kernel,
        grid_spec=grid_spec,
        out_shape=jax.ShapeDtypeStruct((P, D), F32),
        compiler_params=_cparams(("arbitrary",)),
        name="ffn",
    )(block_e, n_blocks, xs, w_gu, b_gu.reshape(E, 1, 2 * D_EXPERT), w_down, b_down.reshape(E, 1, D))


def _comb_kernel(x1_ref, gate_ref, g2_ref, fg_ref, dest_hbm, ys_hbm, o_ref, buf_a, buf_b, idx_a, idx_b,
                 sem_ia, sem_ib, sem_ra, sem_rb, *, tm, n_steps):
    i = pl.program_id(0)
    n = tm * TOP_K

    def idx_copy(tile, buf, sem):
        return pltpu.make_async_copy(dest_hbm.at[pl.ds(tile * n, n)], buf, sem)

    def row_copy(src, buf, k, t, sem):
        return pltpu.make_async_copy(ys_hbm.at[pl.ds(src, 1)], buf.at[k, pl.ds(t, 1)], sem)

    def issue_rows(idx, buf, sem):
        def body(t, c):
            for k in range(TOP_K):
                row_copy(idx[t * TOP_K + k], buf, k, t, sem).start(priority=k % 2)
            return c

        lax.fori_loop(0, tm, body, 0)

    def wait_rows(buf, sem):
        def body(t, c):
            for k in range(TOP_K):
                row_copy(0, buf, k, 0, sem).wait()
            return c

        lax.fori_loop(0, tm, body, 0)

    def finish(buf, lo):
        gates = gate_ref[lo:lo + tm, :]
        y = jnp.zeros((tm, x1_ref.shape[1]), F32)
        for k in range(TOP_K):
            y = y + gates[:, k:k + 1] * buf[k]
        v = x1_ref[lo:lo + tm, :] + g2_ref[...] * y
        o_ref[lo:lo + tm, :] = v * lax.rsqrt(jnp.mean(v * v, axis=-1, keepdims=True) + EPS) * fg_ref[...]

    @pl.when(i == 0)
    def _():
        first = idx_copy(0, idx_a, sem_ia)
        first.start()
        first.wait()
        issue_rows(idx_a, buf_a, sem_ra)
        idx_copy(1, idx_b, sem_ib).start()

    idx_copy(2 * i + 1, idx_b, sem_ib).wait()
    issue_rows(idx_b, buf_b, sem_rb)

    @pl.when(i + 1 < n_steps)
    def _():
        idx_copy(2 * i + 2, idx_a, sem_ia).start()

    wait_rows(buf_a, sem_ra)
    finish(buf_a, 0)

    @pl.when(i + 1 < n_steps)
    def _():
        idx_copy(2 * i + 2, idx_a, sem_ia).wait()
        issue_rows(idx_a, buf_a, sem_ra)
        idx_copy(2 * i + 3, idx_b, sem_ib).start()

    wait_rows(buf_b, sem_rb)
    finish(buf_b, tm)


def _combine(x1, gate_tab, g2, final_g, dest_flat, ys, S, tm):
    N, D = x1.shape
    n_steps = N // (2 * tm)
    per_b = S // (2 * tm)
    return pl.pallas_call(
        functools.partial(_comb_kernel, tm=tm, n_steps=n_steps),
        grid=(n_steps,),
        in_specs=[pl.BlockSpec((2 * tm, D), lambda i: (i, 0)),
                  pl.BlockSpec((2 * tm, LANES), lambda i: (i, 0)),
                  pl.BlockSpec((None, 1, D), lambda i: (i // per_b, 0, 0)),
                  pl.BlockSpec((1, D), lambda i: (0, 0)),
                  pl.BlockSpec(memory_space=pl.ANY),
                  pl.BlockSpec(memory_space=pl.ANY)],
        out_specs=pl.BlockSpec((2 * tm, D), lambda i: (i, 0)),
        out_shape=jax.ShapeDtypeStruct((N, D), F32),
        scratch_shapes=[pltpu.VMEM((TOP_K, tm, D), F32), pltpu.VMEM((TOP_K, tm, D), F32),
                        pltpu.SMEM((tm * TOP_K,), I32), pltpu.SMEM((tm * TOP_K,), I32),
                        pltpu.SemaphoreType.DMA, pltpu.SemaphoreType.DMA, pltpu.SemaphoreType.DMA,
                        pltpu.SemaphoreType.DMA],
        compiler_params=_cparams(("arbitrary",)),
        name="comb",
    )(x1, gate_tab, g2, final_g.reshape(1, D), dest_flat, ys)


def _tile(n, pref):
    t = min(pref, n)
    assert n % t == 0, (n, t)
    return t


def _layer(x, c, ada_w, ada_b, norm1_g, w_in, mix_scale, w_o, norm2_g,
           router_w, router_b, w_gu, b_gu, w_down, b_down, final_g):
    B, S, D = x.shape
    N = B * S
    mod = _mod(c, ada_w, ada_b).reshape(B, 6, 1, D)
    sh1, sc1, g1, sh2, sc2, g2 = (mod[:, j] for j in range(6))

    w_pad = jnp.pad(w_in, ((0, 0), (0, IN_COLS_PAD - IN_COLS))).astype(BF16)
    rq, rk, rv, rg, aqt, ak, avt, iqt, ik, iwt = _inproj(x, norm1_g, sc1, sh1, w_pad, _tile(S, 512))
    ms = mix_scale.reshape(1, RET_W + DSA_W)
    ret = _retention(rq, rk, rv, rg, ms[:, :RET_W])
    att = _dsa(iqt, iwt, aqt, ik, ak, avt, ms[:, RET_W:], _tile(S, 256), 128)

    rw_pad = jnp.pad(router_w, ((0, 0), (0, LANES - N_EXPERTS)))
    rb_pad = jnp.pad(router_b, (0, LANES - N_EXPERTS)).reshape(1, LANES)
    x1, h2, sel, idx_tab, gate_tab, counts = _oproj(ret, att, x, w_o.astype(BF16), g1, norm2_g, sc2, sh2,
                                                    rw_pad, rb_pad, _tile(S, 512))

    tmb = FFN_ROWS
    n_rows = (N * TOP_K + N_EXPERTS * (tmb - 1)) // tmb * tmb + 2 * tmb
    cnt = counts[0, :N_EXPERTS].astype(I32)
    padded = (cnt + tmb - 1) // tmb * tmb
    ends = jnp.cumsum(padded)
    starts = ends - padded
    pstart = jnp.pad(starts.astype(F32), (0, LANES - N_EXPERTS)).reshape(1, LANES)
    n_blocks = (ends[-1] // tmb).reshape(1)
    first_row = jnp.arange(n_rows // tmb, dtype=I32) * tmb
    block_e = jnp.minimum(jnp.sum((ends[None, :] <= first_row[:, None]).astype(I32), axis=1), N_EXPERTS - 1)

    tmd = _tile(N, 256)
    dest_tab = _dest(sel, idx_tab, pstart, tmd)
    dest_flat = dest_tab[:, :TOP_K].reshape(N * TOP_K)
    xs = _dispatch(h2, dest_flat, starts + cnt, n_rows, tmb + SUBLANES, tmd)
    ys = _ffn(xs, block_e, n_blocks, w_gu, b_gu, w_down, b_down, tmb)
    out = _combine(x1, gate_tab, g2, final_g, dest_flat, ys, S, _tile(S, 256))
    return out.reshape(B, S, D)


def kernel(x, c, ada_w, ada_b, norm1_g, w_in, mix_scale, w_o, norm2_g, router_w, router_b, w_gu, b_gu,
           w_down, b_down, final_g):
    assert ada_w.shape[0] == 1, "single-layer stack"
    return _layer(x, c, ada_w[0], ada_b[0], norm1_g[0], w_in[0], mix_scale[0], w_o[0], norm2_g[0],
                  router_w[0], router_b[0], w_gu[0], b_gu[0], w_down[0], b_down[0], final_g)
```

```python
import functools

import numpy as np
import jax
import jax.numpy as jnp
from jax import lax
from jax.experimental import pallas as pl
from jax.experimental.pallas import tpu as pltpu

F32 = jnp.float32
BF16 = jnp.bfloat16
I32 = jnp.int32

D_MODEL = 1024
RET_HEADS = 4
RET_DK = 64
RET_DV = 128
RET_CHUNK = 128
DSA_HEADS = 8
DSA_KV_HEADS = 2
DSA_HD = 64
IDX_HEADS = 8
IDX_HD = 64
TOPK_MAX = 256
N_EXPERTS = 32
TOP_K = 4
D_EXPERT = D_MODEL
SWIGLU_LIMIT = 7.0
SWIGLU_ALPHA = 1.702
EPS = 1e-6

RET_W = RET_HEADS * RET_DV
DSA_W = DSA_HEADS * DSA_HD
IN_COLS = 2888
IN_COLS_PAD = 2944

KAUG = 128
VAUG = 80
ALIBI_SPLIT = 64
FFN_ROWS = 512
BISECT_VALUE_STEPS = 8
BISECT_MAX_STEPS = 64

LANES = 128
SUBLANES = 8
VMEM_LIMIT = 56 * 1024 * 1024
NEG_BIG = -1e30
F32_LOWEST = float(np.finfo(np.float32).min)


def _cparams(sem):
    return pltpu.CompilerParams(dimension_semantics=sem, vmem_limit_bytes=VMEM_LIMIT)


def _mod_kernel(c_ref, w_ref, b_ref, o_ref):
    c = c_ref[...]
    s = c * (1.0 / (1.0 + jnp.exp(-c)))
    o_ref[...] = jnp.dot(s, w_ref[...], preferred_element_type=F32,
                         precision=lax.Precision.HIGHEST) + b_ref[...]


def _mod(c, ada_w, ada_b):
    B, D = c.shape
    n_out = ada_w.shape[1]
    rows = 8
    c8 = jnp.zeros((rows, D), F32).at[:B].set(c)
    out = pl.pallas_call(
        _mod_kernel,
        grid=(n_out // D,),
        in_specs=[pl.BlockSpec((rows, D), lambda j: (0, 0)),
                  pl.BlockSpec((D, D), lambda j: (0, j)),
                  pl.BlockSpec((1, D), lambda j: (0, j))],
        out_specs=pl.BlockSpec((rows, D), lambda j: (0, j)),
        out_shape=jax.ShapeDtypeStruct((rows, n_out), F32),
        compiler_params=_cparams(("arbitrary",)),
        name="mod",
    )(c8, ada_w, ada_b.reshape(1, n_out))
    return out[:B]


def _inproj_kernel(x_ref, g_ref, sc_ref, sh_ref, w_ref,
                   rq_ref, rk_ref, rv_ref, rg_ref, aqt_ref, ak_ref, avt_ref, iqt_ref, ik_ref, iwt_ref):
    x = x_ref[...]
    ms = jnp.mean(x * x, axis=-1, keepdims=True)
    y = x * lax.rsqrt(ms + EPS) * g_ref[...]
    hb = (y * (1.0 + sc_ref[...]) + sh_ref[...]).astype(BF16)

    def proj(lo, hi):
        return jnp.dot(hb, w_ref[:, lo:hi], preferred_element_type=F32)

    tm = x.shape[0]
    d = DSA_HD
    rq_ref[...] = proj(0, 256).astype(BF16)
    rk_ref[...] = (proj(256, 512) * (RET_DK ** -0.5)).astype(BF16)
    rv_ref[...] = proj(512, 1024).astype(BF16)
    rg_ref[...] = proj(1024, 1536).astype(BF16)
    aqt_ref[...] = (proj(1536, 2048) * (d ** -0.5)).T.astype(BF16)
    kk = proj(2048, 2176)
    pos = pl.program_id(1) * tm + lax.broadcasted_iota(I32, (tm, d), 0)
    col = lax.broadcasted_iota(I32, (tm, d), 1)
    posblk = jnp.where(col == 0, pos // ALIBI_SPLIT, jnp.where(col == 1, pos % ALIBI_SPLIT, 0)).astype(F32)
    for g in range(DSA_KV_HEADS):
        ak_ref[:, g * KAUG:g * KAUG + d] = kk[:, g * d:(g + 1) * d].astype(BF16)
        ak_ref[:, g * KAUG + d:(g + 1) * KAUG] = posblk.astype(BF16)
    vt = proj(2176, 2304).T
    r16 = lax.broadcasted_iota(I32, (VAUG - d, tm), 0)
    onesblk = jnp.where(r16 == 0, 1.0, 0.0).astype(BF16)
    for g in range(DSA_KV_HEADS):
        avt_ref[g * VAUG:g * VAUG + d, :] = vt[g * d:(g + 1) * d, :].astype(BF16)
        avt_ref[g * VAUG + d:(g + 1) * VAUG, :] = onesblk
    iqt_ref[...] = proj(2304, 2816).T.astype(BF16)
    last = proj(2816, 2944)
    ik_ref[...] = last[:, :IDX_HD].astype(BF16)
    iwt_ref[...] = last.T[IDX_HD:IDX_HD + IDX_HEADS, :] * ((IDX_HD ** -0.5) * (IDX_HEADS ** -0.5))


def _inproj(x, norm_g, sc, sh, w_pad, tm):
    B, S, D = x.shape
    row = lambda w: pl.BlockSpec((None, tm, w), lambda b, i: (b, i, 0))
    colT = lambda h: pl.BlockSpec((None, h, tm), lambda b, i: (b, 0, i))
    vec = pl.BlockSpec((None, 1, D), lambda b, i: (b, 0, 0))
    sd = lambda shape, dt: jax.ShapeDtypeStruct(shape, dt)
    G = DSA_KV_HEADS
    return pl.pallas_call(
        _inproj_kernel,
        grid=(B, S // tm),
        in_specs=[row(D), pl.BlockSpec((1, D), lambda b, i: (0, 0)), vec, vec,
                  pl.BlockSpec((D, IN_COLS_PAD), lambda b, i: (0, 0))],
        out_specs=[row(256), row(256), row(512), row(512), colT(DSA_W), row(G * KAUG), colT(G * VAUG),
                   colT(IDX_HEADS * IDX_HD), row(IDX_HD), colT(IDX_HEADS)],
        out_shape=[sd((B, S, 256), BF16), sd((B, S, 256), BF16), sd((B, S, 512), BF16),
                   sd((B, S, 512), BF16), sd((B, DSA_W, S), BF16), sd((B, S, G * KAUG), BF16),
                   sd((B, G * VAUG, S), BF16), sd((B, IDX_HEADS * IDX_HD, S), BF16),
                   sd((B, S, IDX_HD), BF16), sd((B, IDX_HEADS, S), F32)],
        compiler_params=_cparams(("parallel", "parallel")),
        name="inproj",
    )(x, norm_g.reshape(1, D), sc, sh, w_pad)


def _ret_kernel(rq_ref, rk_ref, rv_ref, rg_ref, din_ref, qd_ref, kd_ref, cd_ref, ms_ref, o_ref, state_ref):
    @pl.when(pl.program_id(1) == 0)
    def _():
        state_ref[...] = jnp.zeros_like(state_ref)

    for h in range(RET_HEADS):
        q = rq_ref[:, h * RET_DK:(h + 1) * RET_DK]
        k = rk_ref[:, h * RET_DK:(h + 1) * RET_DK]
        v = rv_ref[:, h * RET_DV:(h + 1) * RET_DV]
        r_prev = state_ref[h]
        s = lax.dot_general(q, k, (((1,), (1,)), ((), ())), preferred_element_type=F32) * din_ref[h]
        o = jnp.dot(s.astype(BF16), v, preferred_element_type=F32)
        o = o + jnp.dot(q, r_prev.astype(BF16), preferred_element_type=F32) * qd_ref[h]
        vd = (v.astype(F32) * kd_ref[h]).astype(BF16)
        kv = lax.dot_general(k, vd, (((0,), (0,)), ((), ())), preferred_element_type=F32)
        state_ref[h] = r_prev * cd_ref[h] + kv
        o = o * lax.rsqrt(jnp.mean(o * o, axis=-1, keepdims=True) + EPS)
        g = rg_ref[:, h * RET_DV:(h + 1) * RET_DV].astype(F32)
        gate = g * (1.0 / (1.0 + jnp.exp(-g)))
        o_ref[:, h * RET_DV:(h + 1) * RET_DV] = (gate * o * ms_ref[:, h * RET_DV:(h + 1) * RET_DV]).astype(BF16)


def _ret_consts(C):
    H = RET_HEADS
    log_g = np.log1p(-np.exp2(-5.0 - np.arange(H, dtype=np.float64)))
    pos = np.arange(C, dtype=np.float64)
    diff = pos[:, None] - pos[None, :]
    d_inner = np.where(diff[None] >= 0, np.exp(np.maximum(diff, 0.0)[None] * log_g[:, None, None]), 0.0)
    q_decay = np.exp((pos + 1.0)[None] * log_g[:, None])
    k_decay = np.exp((C - 1.0 - pos)[None] * log_g[:, None])
    chunk_decay = np.exp(C * log_g)
    qd = np.broadcast_to(q_decay[:, :, None], (H, C, RET_DV))
    kd = np.broadcast_to(k_decay[:, :, None], (H, C, RET_DV))
    cd = np.broadcast_to(chunk_decay[:, None, None], (H, 1, RET_DV))
    f = lambda a: jnp.asarray(np.ascontiguousarray(a), F32)
    return f(d_inner), f(qd), f(kd), f(cd)


def _retention(rq, rk, rv, rg, ms_ret):
    B, S, _ = rq.shape
    C = min(RET_CHUNK, S)
    din, qd, kd, cd = _ret_consts(C)
    row = lambda w: pl.BlockSpec((None, C, w), lambda b, n: (b, n, 0))
    full = lambda a: pl.BlockSpec(a.shape, lambda b, n: (0,) * a.ndim)
    return pl.pallas_call(
        _ret_kernel,
        grid=(B, S // C),
        in_specs=[row(256), row(256), row(512), row(512), full(din), full(qd), full(kd), full(cd),
                  pl.BlockSpec((1, RET_W), lambda b, n: (0, 0))],
        out_specs=row(RET_W),
        out_shape=jax.ShapeDtypeStruct((B, S, RET_W), BF16),
        scratch_shapes=[pltpu.VMEM((RET_HEADS, RET_DK, RET_DV), F32)],
        compiler_params=_cparams(("parallel", "arbitrary")),
        name="ret",
    )(rq, rk, rv, rg, din, qd, kd, cd, ms_ret)


def _f32_key(x):
    i = lax.bitcast_convert_type(x, I32)
    return i ^ ((i >> 31) & 0x7FFFFFFF)


def _key_f32(k):
    return lax.bitcast_convert_type(k ^ ((k >> 31) & 0x7FFFFFFF), F32)


def _dsa_kernel(iqt_ref, iwt_ref, aqt_ref, ik_ref, ak_ref, avt_ref, ms_ref, o_ref, score_ref, qa_ref, sa_ref, sb_ref, mask_ref, *acc_refs,
                tq, tks, n_sel):
    H, G, R, d = DSA_HEADS, DSA_KV_HEADS, DSA_HEADS // DSA_KV_HEADS, DSA_HD
    t0 = pl.program_id(1) * tq
    nsub = (t0 + tq) // tks
    tka = 2 * tks
    npair = ((t0 + tq) // tka + 1) // 2
    kf = float(n_sel)
    qpos = t0 + lax.broadcasted_iota(I32, (1, tq), 1)
    krow = lax.broadcasted_iota(I32, (tks, tq), 0)

    wrow = [iwt_ref[h:h + 1, :] for h in range(IDX_HEADS)]

    def score_pair(i, carry):
        for u in range(2 * tka // tks):
            r0 = pl.multiple_of(i * 2 * tka + u * tks, tks)
            kc = ik_ref[pl.ds(r0, tks), :]
            acc = jnp.zeros((tks, tq), F32)
            for h in range(IDX_HEADS):
                rel = jnp.dot(kc, iqt_ref[h * IDX_HD:(h + 1) * IDX_HD, :], preferred_element_type=F32)
                acc = acc + jnp.maximum(rel, 0.0) * wrow[h]
            score_ref[pl.ds(r0, tks), :] = jnp.where(r0 + krow <= qpos, acc, -jnp.inf)
        return carry

    lax.fori_loop(0, npair, score_pair, 0)

    def fold8(x, op):
        acc = x[0:8, :]
        for i in range(1, tks // 8):
            acc = op(acc, x[8 * i:8 * (i + 1), :])
        return acc

    def count(th, strict):
        def body(j, acc):
            for u in range(2):
                s = score_ref[pl.ds(pl.multiple_of((2 * j + u) * tks, tks), tks), :]
                hit = (s > th) if strict else (s >= th)
                acc = acc + fold8(jnp.where(hit, 1.0, 0.0), jnp.add)
            return acc

        acc = lax.fori_loop(0, nsub // 2, body, jnp.zeros((8, tq), F32))
        return jnp.sum(acc, axis=0, keepdims=True)

    def minmax(j, carry):
        mx, mn = carry
        s = score_ref[pl.ds(pl.multiple_of(j * tks, tks), tks), :]
        mx = jnp.maximum(mx, fold8(s, jnp.maximum))
        mn = jnp.minimum(mn, fold8(jnp.where(s == -jnp.inf, jnp.inf, s), jnp.minimum))
        return mx, mn

    mx, mn = lax.fori_loop(0, nsub, minmax, (jnp.full((8, tq), -jnp.inf, F32), jnp.full((8, tq), jnp.inf, F32)))
    hi0 = jnp.max(mx, axis=0, keepdims=True)
    lo0 = jnp.min(mn, axis=0, keepdims=True)

    def probe(lo, hi, it):
        lk, hk = _f32_key(lo), _f32_key(hi)
        mk = (lk >> 1) + (hk >> 1) + (lk & hk & 1)
        mv = lo + (hi - lo) * 0.5
        early = (jnp.zeros((1, tq), I32) + it) < BISECT_VALUE_STEPS
        mid = jnp.where(early & (mv > lo) & (mv < hi), mv, _key_f32(mk))
        return mid, jnp.max(jnp.where(mk != lk, 1.0, 0.0))

    def bis_cond(c):
        return (c[4] > 0.0) & (c[5] < BISECT_MAX_STEPS)

    def bis_body(c):
        lo, hi, c_lo, c_hi, _, it = c
        mid, _ = probe(lo, hi, it)
        cnt = count(mid, False)
        ge = cnt >= kf
        up = ge | (cnt == kf)
        dn = (~ge) | (cnt == kf)
        lo, c_lo = jnp.where(up, mid, lo), jnp.where(up, cnt, c_lo)
        hi, c_hi = jnp.where(dn, mid, hi), jnp.where(dn, cnt, c_hi)
        _, active = probe(lo, hi, it + 1)
        return lo, hi, c_lo, c_hi, active, it + 1

    zero = jnp.zeros((1, tq), F32)
    n_pos = count(zero, True)
    n_nonneg = count(zero, False)
    keep_all = qpos + 1 <= n_sel
    settled = keep_all | ((n_nonneg >= kf) & (n_pos < kf))
    above = n_pos >= kf
    c_lo0 = jnp.where(settled | above, n_nonneg, (qpos + 1).astype(F32))
    c_hi0 = jnp.where(settled | ~above, n_nonneg, count(hi0, False))
    lo0 = jnp.where(settled | above, zero, lo0)
    hi0 = jnp.where(settled | ~above, zero, hi0)
    _, active0 = probe(lo0, hi0, jnp.int32(0))
    lo, hi, c_lo, c_hi, _, _ = lax.while_loop(bis_cond, bis_body, (lo0, hi0, c_lo0, c_hi0, active0, jnp.int32(0)))
    at_hi = c_hi >= kf
    thr = jnp.where(keep_all, F32_LOWEST, jnp.where(at_hi, hi, lo))
    excess = jnp.where(keep_all, 0.0, jnp.where(at_hi, c_hi, c_lo) - kf)

    @pl.when(jnp.max(excess) > 0.0)
    def _():
        budget = kf - count(thr, True)
        earlier = lax.broadcasted_iota(I32, (tks, tks), 1) < lax.broadcasted_iota(I32, (tks, tks), 0)
        earlier = jnp.where(earlier, 1.0, 0.0).astype(BF16)

        def fix(j, seen):
            r0 = pl.multiple_of(j * tks, tks)
            s = score_ref[pl.ds(r0, tks), :]
            eq = s == thr
            eqf = jnp.where(eq, 1.0, 0.0)
            rank = jnp.dot(earlier, eqf.astype(BF16), preferred_element_type=F32) + seen
            score_ref[pl.ds(r0, tks), :] = jnp.where(eq & (rank >= budget), -jnp.inf, s)
            return seen + jnp.sum(eqf, axis=0, keepdims=True)

        lax.fori_loop(0, nsub, fix, jnp.zeros((1, tq), F32))

    arow = lax.broadcasted_iota(I32, (KAUG - d, tq), 0)
    for h in range(H):
        slope = float(2.0 ** (-8.0 * (h + 1) / H))
        qa_ref[h, 0:d, :] = aqt_ref[h * d:(h + 1) * d, :]
        qa_ref[h, d:KAUG, :] = jnp.where(arow == 0, slope * ALIBI_SPLIT, jnp.where(arow == 1, slope, 0.0)).astype(BF16)
    for acc in acc_refs:
        acc[...] = jnp.zeros_like(acc)

    def logits(j, h):
        ka = ak_ref[pl.ds(pl.multiple_of(j * tka, tka), tka), (h // R) * KAUG:(h // R + 1) * KAUG]
        return jnp.dot(ka, qa_ref[h], preferred_element_type=F32)

    def step(j, j_next, cur_ref, next_ref, ms):
        r0 = pl.multiple_of(j * tka, tka)
        mask_ref[...] = jnp.where(score_ref[pl.ds(r0, tka), :] >= thr, 0.0, NEG_BIG)
        nms = []
        for h in range(H):
            g = h // R
            next_ref[h] = logits(j_next, h)
            s = cur_ref[h] + mask_ref[...]
            m_new = jnp.maximum(ms[h], jnp.max(s, axis=0, keepdims=True))
            p = jnp.exp(s - m_new).astype(BF16)
            va = avt_ref[g * VAUG:(g + 1) * VAUG, pl.ds(r0, tka)]
            acc = acc_refs[h]
            acc[...] = acc[...] * jnp.exp(ms[h] - m_new) + jnp.dot(va, p, preferred_element_type=F32)
            nms.append(m_new)
        return tuple(nms)

    for h in range(H):
        sa_ref[h] = logits(0, h)

    def att_pair(i, ms):
        ms = step(2 * i, 2 * i + 1, sa_ref, sb_ref, ms)
        return step(2 * i + 1, jnp.minimum(2 * i + 2, 2 * npair - 1), sb_ref, sa_ref, ms)

    lax.fori_loop(0, npair, att_pair, tuple(jnp.full((1, tq), NEG_BIG, F32) for _ in range(H)))
    for h in range(H):
        a = acc_refs[h][...]
        o = a[0:d, :] / a[d:d + 1, :]
        o_ref[h * d:(h + 1) * d, :] = (o * ms_ref[h * d:(h + 1) * d, :]).astype(BF16)


def _dsa(iqt, iwt, aqt, ik, ak, avt, ms_att, tq, tks):
    B, _, S = iqt.shape
    assert S <= ALIBI_SPLIT * 256 and tq % (2 * tks) == 0
    assert (S // (2 * tks)) % 2 == 0
    n_sel = min(TOPK_MAX, S // 4)
    G = DSA_KV_HEADS
    colT = lambda h: pl.BlockSpec((None, h, tq), lambda b, i: (b, 0, i))
    msb = jnp.broadcast_to(ms_att.reshape(DSA_W, 1), (DSA_W, tq))
    return pl.pallas_call(
        functools.partial(_dsa_kernel, tq=tq, tks=tks, n_sel=n_sel),
        grid=(B, S // tq),
        in_specs=[colT(IDX_HEADS * IDX_HD), colT(IDX_HEADS), colT(DSA_W),
                  pl.BlockSpec((None, S, IDX_HD), lambda b, i: (b, 0, 0)),
                  pl.BlockSpec((None, S, G * KAUG), lambda b, i: (b, 0, 0)),
                  pl.BlockSpec((None, G * VAUG, S), lambda b, i: (b, 0, 0)),
                  pl.BlockSpec((DSA_W, tq), lambda b, i: (0, 0))],
        out_specs=colT(DSA_W),
        out_shape=jax.ShapeDtypeStruct((B, DSA_W, S), BF16),
        scratch_shapes=[pltpu.VMEM((S, tq), F32), pltpu.VMEM((DSA_HEADS, KAUG, tq), BF16),
                        pltpu.VMEM((DSA_HEADS, 2 * tks, tq), F32), pltpu.VMEM((DSA_HEADS, 2 * tks, tq), F32),
                        pltpu.VMEM((2 * tks, tq), F32)]
        + [pltpu.VMEM((VAUG, tq), F32) for _ in range(DSA_HEADS)],
        compiler_params=_cparams(("parallel", "arbitrary")),
        name="dsa",
    )(iqt, iwt, aqt, ik, ak, avt, msb)


def _oproj_kernel(ret_ref, att_ref, x_ref, wo_ref, g1_ref, n2_ref, sc_ref, sh_ref, rw_ref, rb_ref,
                  x1_ref, h2_ref, sel_ref, idx_ref, gate_ref, cnt_ref):
    mixo = jnp.dot(ret_ref[...], wo_ref[:RET_W, :], preferred_element_type=F32)
    mixo = mixo + lax.dot_general(att_ref[...], wo_ref[RET_W:, :], (((0,), (0,)), ((), ())),
                                  preferred_element_type=F32)
    x1 = x_ref[...] + g1_ref[...] * mixo
    x1_ref[...] = x1
    y = x1 * lax.rsqrt(jnp.mean(x1 * x1, axis=-1, keepdims=True) + EPS) * n2_ref[...]
    h2 = y * (1.0 + sc_ref[...]) + sh_ref[...]
    h2_ref[...] = h2
    logits = jnp.dot(h2, rw_ref[...], preferred_element_type=F32,
                     precision=lax.Precision.HIGHEST) + rb_ref[...]
    tm = logits.shape[0]
    lane = lax.broadcasted_iota(I32, (tm, LANES), 1).astype(F32)
    work = jnp.where(lane < N_EXPERTS, logits, -jnp.inf)
    sel = jnp.zeros((tm, LANES), F32)
    idx_tab = jnp.zeros((tm, LANES), F32)
    vals = []
    for k in range(TOP_K):
        m = jnp.max(work, axis=1, keepdims=True)
        idx = jnp.min(jnp.where(work == m, lane, float(LANES)), axis=1, keepdims=True)
        hit = lane == idx
        sel = jnp.where(hit, 1.0, sel)
        idx_tab = jnp.where(lane == k, idx, idx_tab)
        work = jnp.where(hit, -jnp.inf, work)
        vals.append(m)
    es = [jnp.exp(v - vals[0]) for v in vals]
    den = es[0] + es[1] + es[2] + es[3]
    gate_tab = jnp.zeros((tm, LANES), F32)
    for k in range(TOP_K):
        gate_tab = jnp.where(lane == k, es[k] / den, gate_tab)
    sel_ref[...] = sel
    idx_ref[...] = idx_tab
    gate_ref[...] = gate_tab

    @pl.when((pl.program_id(0) == 0) & (pl.program_id(1) == 0))
    def _():
        cnt_ref[...] = jnp.zeros_like(cnt_ref)

    cnt_ref[...] += jnp.sum(sel, axis=0, keepdims=True)


def _oproj(ret, att, x, wo, g1, n2g, sc2, sh2, rw_pad, rb_pad, tm):
    B, S, D = x.shape
    nt = S // tm
    row = lambda w: pl.BlockSpec((None, tm, w), lambda b, i: (b, i, 0))
    flat = lambda w: pl.BlockSpec((tm, w), lambda b, i: (b * nt + i, 0))
    vec = pl.BlockSpec((None, 1, D), lambda b, i: (b, 0, 0))
    cst = lambda shape: pl.BlockSpec(shape, lambda b, i: (0, 0))
    sd = lambda shape, dt: jax.ShapeDtypeStruct(shape, dt)
    N = B * S
    return pl.pallas_call(
        _oproj_kernel,
        grid=(B, nt),
        in_specs=[row(RET_W), pl.BlockSpec((None, DSA_W, tm), lambda b, i: (b, 0, i)), row(D), cst((D, D)), vec,
                  cst((1, D)), vec, vec,
                  cst((D, LANES)), cst((1, LANES))],
        out_specs=[flat(D), flat(D), flat(LANES), flat(LANES), flat(LANES), cst((1, LANES))],
        out_shape=[sd((N, D), F32), sd((N, D), F32), sd((N, LANES), F32), sd((N, LANES), F32),
                   sd((N, LANES), F32), sd((1, LANES), F32)],
        compiler_params=_cparams(("arbitrary", "arbitrary")),
        name="oproj",
    )(ret, att, x, wo, g1, n2g.reshape(1, D), sc2, sh2, rw_pad, rb_pad)


def _dest_kernel(sel_ref, idx_ref, pstart_ref, dest_ref, seen_ref):
    @pl.when(pl.program_id(0) == 0)
    def _():
        seen_ref[...] = jnp.zeros_like(seen_ref)

    sel = sel_ref[...]
    tm = sel.shape[0]
    earlier = lax.broadcasted_iota(I32, (tm, tm), 1) < lax.broadcasted_iota(I32, (tm, tm), 0)
    earlier = jnp.where(earlier, 1.0, 0.0).astype(BF16)
    rank = jnp.dot(earlier, sel.astype(BF16), preferred_element_type=F32) + seen_ref[...]
    dest = pstart_ref[...] + rank
    lane = lax.broadcasted_iota(I32, (tm, LANES), 1).astype(F32)
    idx_tab = idx_ref[...]
    out = jnp.zeros((tm, LANES), F32)
    for k in range(TOP_K):
        e_k = jnp.sum(jnp.where(lane == k, idx_tab, 0.0), axis=1, keepdims=True)
        d_k = jnp.sum(jnp.where(lane == e_k, dest, 0.0), axis=1, keepdims=True)
        out = jnp.where(lane == k, d_k, out)
    dest_ref[...] = out.astype(I32)
    seen_ref[...] += jnp.sum(sel, axis=0, keepdims=True)


def _dest(sel, idx_tab, pstart, tm):
    N = sel.shape[0]
    blk = pl.BlockSpec((tm, LANES), lambda i: (i, 0))
    return pl.pallas_call(
        _dest_kernel,
        grid=(N // tm,),
        in_specs=[blk, blk, pl.BlockSpec((1, LANES), lambda i: (0, 0))],
        out_specs=blk,
        out_shape=jax.ShapeDtypeStruct((N, LANES), I32),
        scratch_shapes=[pltpu.VMEM((1, LANES), F32)],
        compiler_params=_cparams(("arbitrary",)),
        name="dest",
    )(sel, idx_tab, pstart)


def _disp_kernel(zs_ref, h2_ref, dest_hbm, xs_hbm, zbuf, idx_a, idx_b, sem_ia, sem_ib, sem_row, sem_z, *, tm):
    i = pl.program_id(0)
    n = tm * TOP_K
    zrows = zbuf.shape[0]

    def idx_copy(tile, buf, sem):
        return pltpu.make_async_copy(dest_hbm.at[pl.ds(tile * n, n)], buf, sem)

    def row_copy(row, dst):
        return pltpu.make_async_copy(h2_ref.at[pl.ds(row, 1)], xs_hbm.at[pl.ds(dst, 1)], sem_row)

    def issue_rows(first_row, idx):
        def body(t, c):
            for k in range(TOP_K):
                row_copy(first_row + t, idx[t * TOP_K + k]).start(priority=k % 2)
            return c

        lax.fori_loop(0, tm, body, 0)

    def wait_rows():
        def body(t, c):
            for k in range(TOP_K):
                row_copy(0, 0).wait()
            return c

        lax.fori_loop(0, tm, body, 0)

    @pl.when(i == 0)
    def _():
        zbuf[...] = jnp.zeros_like(zbuf)
        for e in range(N_EXPERTS):
            z0 = pl.multiple_of(zs_ref[e] // SUBLANES * SUBLANES, SUBLANES)
            fill = pltpu.make_async_copy(zbuf, xs_hbm.at[pl.ds(z0, zrows)], sem_z)
            fill.start()
            fill.wait()

    idx_copy(2 * i, idx_a, sem_ia).start()
    idx_copy(2 * i + 1, idx_b, sem_ib).start()
    idx_copy(2 * i, idx_a, sem_ia).wait()
    issue_rows(0, idx_a)
    idx_copy(2 * i + 1, idx_b, sem_ib).wait()
    issue_rows(tm, idx_b)
    wait_rows()
    wait_rows()


def _dispatch(h2, dest_flat, zero_start, n_rows, zrows, tm):
    N, D = h2.shape
    n_steps = N // (2 * tm)
    grid_spec = pltpu.PrefetchScalarGridSpec(
        num_scalar_prefetch=1,
        grid=(n_steps,),
        in_specs=[pl.BlockSpec((2 * tm, D), lambda i, zs: (i, 0)), pl.BlockSpec(memory_space=pl.ANY)],
        out_specs=pl.BlockSpec(memory_space=pl.ANY),
        scratch_shapes=[pltpu.VMEM((zrows, D), F32), pltpu.SMEM((tm * TOP_K,), I32), pltpu.SMEM((tm * TOP_K,), I32),
                        pltpu.SemaphoreType.DMA, pltpu.SemaphoreType.DMA, pltpu.SemaphoreType.DMA,
                        pltpu.SemaphoreType.DMA],
    )
    return pl.pallas_call(
        functools.partial(_disp_kernel, tm=tm),
        grid_spec=grid_spec,
        out_shape=jax.ShapeDtypeStruct((n_rows, D), F32),
        compiler_params=_cparams(("arbitrary",)),
        name="disp",
    )(zero_start, h2, dest_flat)


def _store_token_major(ref, x):
    rows, d = x.shape
    nch = d // LANES
    for j in range(nch):
        ref[pl.ds(j, rows, stride=nch), :] = x[:, j * LANES:(j + 1) * LANES]


def _load_token_major(ref, rows, nch):
    return jnp.concatenate([ref[pl.ds(j, rows, stride=nch), :] for j in range(nch)], axis=1)


def _ffn_kernel(be_ref, nb_ref, xs_ref, wgu_ref, bgu_ref, wd_ref, bd_ref, ys_ref, wgu_bf, wd_bf):
    i = pl.program_id(0)
    live = i < nb_ref[0]

    @pl.when(live & ((i == 0) | (be_ref[i] != be_ref[jnp.maximum(i - 1, 0)])))
    def _():
        wgu_bf[...] = wgu_ref[...].astype(BF16)
        wd_bf[...] = wd_ref[...].astype(BF16)

    @pl.when(live)
    def _():
        xb = xs_ref[...].astype(BF16)
        gu = jnp.dot(xb, wgu_bf[...], preferred_element_type=F32) + bgu_ref[...]
        gate = jnp.minimum(gu[:, :D_EXPERT], SWIGLU_LIMIT)
        up = jnp.clip(gu[:, D_EXPERT:], -SWIGLU_LIMIT, SWIGLU_LIMIT)
        glu = gate * (1.0 / (1.0 + jnp.exp(-SWIGLU_ALPHA * gate)))
        act = ((up + 1.0) * glu).astype(BF16)
        ys = jnp.dot(act, wd_bf[...], preferred_element_type=F32) + bd_ref[...]
        _store_token_major(ys_ref, ys)

    @pl.when(jnp.logical_not(live))
    def _():
        ys_ref[...] = jnp.zeros_like(ys_ref)


def _ffn(xs, block_e, n_blocks, w_gu, b_gu, w_down, b_down, tmb):
    P, D = xs.shape
    E = w_gu.shape[0]
    blk = lambda i, be, nb: (jnp.minimum(i, nb[0] - 1), 0)
    wsel = lambda i, be, nb: (be[jnp.minimum(i, nb[0] - 1)], 0, 0)
    grid_spec = pltpu.PrefetchScalarGridSpec(
        num_scalar_prefetch=2,
        grid=(P // tmb,),
        in_specs=[pl.BlockSpec((tmb, D), blk),
                  pl.BlockSpec((None, D, 2 * D_EXPERT), wsel),
                  pl.BlockSpec((None, 1, 2 * D_EXPERT), wsel),
                  pl.BlockSpec((None, D_EXPERT, D), wsel),
                  pl.BlockSpec((None, 1, D), wsel)],
        out_specs=pl.BlockSpec((tmb * (D // LANES), LANES), lambda i, be, nb: (i, 0)),
        scratch_shapes=[pltpu.VMEM((D, 2 * D_EXPERT), BF16), pltpu.VMEM((D_EXPERT, D), BF16)],
    )
    return pl.pallas_call(
        _ffn_kernel,
        grid_spec=grid_spec,
        out_shape=jax.ShapeDtypeStruct((P * (D // LANES), LANES), F32),
        compiler_params=_cparams(("arbitrary",)),
        name="ffn",
    )(block_e, n_blocks, xs, w_gu, b_gu.reshape(E, 1, 2 * D_EXPERT), w_down, b_down.reshape(E, 1, D))


def _comb_kernel(x1_ref, gate_ref, g2_ref, fg_ref, dest_hbm, ys_hbm, o_ref, buf_a, buf_b, idx_a, idx_b,
                 sem_ia, sem_ib, sem_ra, sem_rb, *, tm, n_steps):
    i = pl.program_id(0)
    n = tm * TOP_K

    def idx_copy(tile, buf, sem):
        return pltpu.make_async_copy(dest_hbm.at[pl.ds(tile * n, n)], buf, sem)

    nch = x1_ref.shape[1] // LANES

    def row_copy(src, buf, k, t, sem):
        return pltpu.make_async_copy(ys_hbm.at[pl.ds(pl.multiple_of(src * nch, nch), nch)],
                                     buf.at[k, pl.ds(pl.multiple_of(t * nch, nch), nch)], sem)

    def issue_rows(idx, buf, sem):
        def body(t, c):
            for k in range(TOP_K):
                row_copy(idx[t * TOP_K + k], buf, k, t, sem).start(priority=k % 2)
            return c

        lax.fori_loop(0, tm, body, 0)

    def wait_rows(buf, sem):
        def body(t, c):
            for k in range(TOP_K):
                row_copy(0, buf, k, 0, sem).wait()
            return c

        lax.fori_loop(0, tm, body, 0)

    def finish(buf, lo):
        gates = gate_ref[lo:lo + tm, :]
        y = jnp.zeros((tm, x1_ref.shape[1]), F32)
        for k in range(TOP_K):
            y = y + gates[:, k:k + 1] * _load_token_major(buf.at[k], tm, nch)
        v = x1_ref[lo:lo + tm, :] + g2_ref[...] * y
        o_ref[lo:lo + tm, :] = v * lax.rsqrt(jnp.mean(v * v, axis=-1, keepdims=True) + EPS) * fg_ref[...]

    @pl.when(i == 0)
    def _():
        first = idx_copy(0, idx_a, sem_ia)
        first.start()
        first.wait()
        issue_rows(idx_a, buf_a, sem_ra)
        idx_copy(1, idx_b, sem_ib).start()

    idx_copy(2 * i + 1, idx_b, sem_ib).wait()
    issue_rows(idx_b, buf_b, sem_rb)

    @pl.when(i + 1 < n_steps)
    def _():
        idx_copy(2 * i + 2, idx_a, sem_ia).start()

    wait_rows(buf_a, sem_ra)
    finish(buf_a, 0)

    @pl.when(i + 1 < n_steps)
    def _():
        idx_copy(2 * i + 2, idx_a, sem_ia).wait()
        issue_rows(idx_a, buf_a, sem_ra)
        idx_copy(2 * i + 3, idx_b, sem_ib).start()

    wait_rows(buf_b, sem_rb)
    finish(buf_b, tm)


def _combine(x1, gate_tab, g2, final_g, dest_flat, ys, S, tm):
    N, D = x1.shape
    n_steps = N // (2 * tm)
    per_b = S // (2 * tm)
    return pl.pallas_call(
        functools.partial(_comb_kernel, tm=tm, n_steps=n_steps),
        grid=(n_steps,),
        in_specs=[pl.BlockSpec((2 * tm, D), lambda i: (i, 0)),
                  pl.BlockSpec((2 * tm, LANES), lambda i: (i, 0)),
                  pl.BlockSpec((None, 1, D), lambda i: (i // per_b, 0, 0)),
                  pl.BlockSpec((1, D), lambda i: (0, 0)),
                  pl.BlockSpec(memory_space=pl.ANY),
                  pl.BlockSpec(memory_space=pl.ANY)],
        out_specs=pl.BlockSpec((2 * tm, D), lambda i: (i, 0)),
        out_shape=jax.ShapeDtypeStruct((N, D), F32),
        scratch_shapes=[pltpu.VMEM((TOP_K, tm * (D // LANES), LANES), F32),
                        pltpu.VMEM((TOP_K, tm * (D // LANES), LANES), F32),
                        pltpu.SMEM((tm * TOP_K,), I32), pltpu.SMEM((tm * TOP_K,), I32),
                        pltpu.SemaphoreType.DMA, pltpu.SemaphoreType.DMA, pltpu.SemaphoreType.DMA,
                        pltpu.SemaphoreType.DMA],
        compiler_params=_cparams(("arbitrary",)),
        name="comb",
    )(x1, gate_tab, g2, final_g.reshape(1, D), dest_flat, ys)


def _tile(n, pref):
    t = min(pref, n)
    assert n % t == 0, (n, t)
    return t


def _layer(x, c, ada_w, ada_b, norm1_g, w_in, mix_scale, w_o, norm2_g,
           router_w, router_b, w_gu, b_gu, w_down, b_down, final_g):
    B, S, D = x.shape
    N = B * S
    mod = _mod(c, ada_w, ada_b).reshape(B, 6, 1, D)
    sh1, sc1, g1, sh2, sc2, g2 = (mod[:, j] for j in range(6))

    w_pad = jnp.pad(w_in, ((0, 0), (0, IN_COLS_PAD - IN_COLS))).astype(BF16)
    rq, rk, rv, rg, aqt, ak, avt, iqt, ik, iwt = _inproj(x, norm1_g, sc1, sh1, w_pad, _tile(S, 512))
    ms = mix_scale.reshape(1, RET_W + DSA_W)
    ret = _retention(rq, rk, rv, rg, ms[:, :RET_W])
    att = _dsa(iqt, iwt, aqt, ik, ak, avt, ms[:, RET_W:], _tile(S, 256), 128)

    rw_pad = jnp.pad(router_w, ((0, 0), (0, LANES - N_EXPERTS)))
    rb_pad = jnp.pad(router_b, (0, LANES - N_EXPERTS)).reshape(1, LANES)
    x1, h2, sel, idx_tab, gate_tab, counts = _oproj(ret, att, x, w_o.astype(BF16), g1, norm2_g, sc2, sh2,
                                                    rw_pad, rb_pad, _tile(S, 512))

    tmb = FFN_ROWS
    n_rows = (N * TOP_K + N_EXPERTS * (tmb - 1)) // tmb * tmb + 2 * tmb
    cnt = counts[0, :N_EXPERTS].astype(I32)
    padded = (cnt + tmb - 1) // tmb * tmb
    ends = jnp.cumsum(padded)
    starts = ends - padded
    pstart = jnp.pad(starts.astype(F32), (0, LANES - N_EXPERTS)).reshape(1, LANES)
    n_blocks = (ends[-1] // tmb).reshape(1)
    first_row = jnp.arange(n_rows // tmb, dtype=I32) * tmb
    block_e = jnp.minimum(jnp.sum((ends[None, :] <= first_row[:, None]).astype(I32), axis=1), N_EXPERTS - 1)

    tmd = _tile(N, 256)
    dest_tab = _dest(sel, idx_tab, pstart, tmd)
    dest_flat = dest_tab[:, :TOP_K].reshape(N * TOP_K)
    xs = _dispatch(h2, dest_flat, starts + cnt, n_rows, tmb + SUBLANES, tmd)
    ys = _ffn(xs, block_e, n_blocks, w_gu, b_gu, w_down, b_down, tmb)
    out = _combine(x1, gate_tab, g2, final_g, dest_flat, ys, S, _tile(S, 256))
    return out.reshape(B, S, D)


def kernel(x, c, ada_w, ada_b, norm1_g, w_in, mix_scale, w_o, norm2_g, router_w, router_b, w_gu, b_gu,
           w_down, b_down, final_g):
    assert ada_w.shape[0] == 1, "single-layer stack"
    return _layer(x, c, ada_w[0], ada_b[0], norm1_g[0], w_in[0], mix_scale[0], w_o[0], norm2_g[0],
                  router_w[0], router_b[0], w_gu[0], b_gu[0], w_down[0], b_down[0], final_g)
```

```python
import functools

import numpy as np
import jax
import jax.numpy as jnp
from jax import lax
from jax.experimental import pallas as pl
from jax.experimental.pallas import tpu as pltpu

F32 = jnp.float32
BF16 = jnp.bfloat16
I32 = jnp.int32

D_MODEL = 1024
RET_HEADS = 4
RET_DK = 64
RET_DV = 128
RET_CHUNK = 128
DSA_HEADS = 8
DSA_KV_HEADS = 2
DSA_HD = 64
IDX_HEADS = 8
IDX_HD = 64
TOPK_MAX = 256
N_EXPERTS = 32
TOP_K = 4
D_EXPERT = D_MODEL
SWIGLU_LIMIT = 7.0
SWIGLU_ALPHA = 1.702
EPS = 1e-6

RET_W = RET_HEADS * RET_DV
DSA_W = DSA_HEADS * DSA_HD
IN_COLS = 2888
IN_COLS_PAD = 2944

KAUG = 128
VAUG = 80
ALIBI_SPLIT = 64
FFN_ROWS = 512
BISECT_VALUE_STEPS = 8
BISECT_MAX_STEPS = 64

LANES = 128
NCH = D_MODEL // LANES
VMEM_LIMIT = 56 * 1024 * 1024
NEG_BIG = -1e30
F32_LOWEST = float(np.finfo(np.float32).min)


def _cparams(sem):
    return pltpu.CompilerParams(dimension_semantics=sem, vmem_limit_bytes=VMEM_LIMIT)


def _mod_kernel(c_ref, w_ref, b_ref, o_ref):
    c = c_ref[...]
    s = c * (1.0 / (1.0 + jnp.exp(-c)))
    o_ref[...] = jnp.dot(s, w_ref[...], preferred_element_type=F32,
                         precision=lax.Precision.HIGHEST) + b_ref[...]


def _mod(c, ada_w, ada_b):
    B, D = c.shape
    n_out = ada_w.shape[1]
    rows = 8
    c8 = jnp.zeros((rows, D), F32).at[:B].set(c)
    out = pl.pallas_call(
        _mod_kernel,
        grid=(n_out // D,),
        in_specs=[pl.BlockSpec((rows, D), lambda j: (0, 0)),
                  pl.BlockSpec((D, D), lambda j: (0, j)),
                  pl.BlockSpec((1, D), lambda j: (0, j))],
        out_specs=pl.BlockSpec((rows, D), lambda j: (0, j)),
        out_shape=jax.ShapeDtypeStruct((rows, n_out), F32),
        compiler_params=_cparams(("arbitrary",)),
        name="mod",
    )(c8, ada_w, ada_b.reshape(1, n_out))
    return out[:B]


def _inproj_kernel(x_ref, g_ref, sc_ref, sh_ref, w_ref,
                   rq_ref, rk_ref, rv_ref, rg_ref, aqt_ref, ak_ref, avt_ref, iqt_ref, ik_ref, iwt_ref):
    x = x_ref[...]
    ms = jnp.mean(x * x, axis=-1, keepdims=True)
    y = x * lax.rsqrt(ms + EPS) * g_ref[...]
    hb = (y * (1.0 + sc_ref[...]) + sh_ref[...]).astype(BF16)

    def proj(lo, hi):
        return jnp.dot(hb, w_ref[:, lo:hi], preferred_element_type=F32)

    tm = x.shape[0]
    d = DSA_HD
    rq_ref[...] = proj(0, 256).astype(BF16)
    rk_ref[...] = (proj(256, 512) * (RET_DK ** -0.5)).astype(BF16)
    rv_ref[...] = proj(512, 1024).astype(BF16)
    rg_ref[...] = proj(1024, 1536).astype(BF16)
    aqt_ref[...] = (proj(1536, 2048) * (d ** -0.5)).T.astype(BF16)
    kk = proj(2048, 2176)
    pos = pl.program_id(1) * tm + lax.broadcasted_iota(I32, (tm, d), 0)
    col = lax.broadcasted_iota(I32, (tm, d), 1)
    posblk = jnp.where(col == 0, pos // ALIBI_SPLIT, jnp.where(col == 1, pos % ALIBI_SPLIT, 0)).astype(F32)
    for g in range(DSA_KV_HEADS):
        ak_ref[:, g * KAUG:g * KAUG + d] = kk[:, g * d:(g + 1) * d].astype(BF16)
        ak_ref[:, g * KAUG + d:(g + 1) * KAUG] = posblk.astype(BF16)
    vt = proj(2176, 2304).T
    r16 = lax.broadcasted_iota(I32, (VAUG - d, tm), 0)
    onesblk = jnp.where(r16 == 0, 1.0, 0.0).astype(BF16)
    for g in range(DSA_KV_HEADS):
        avt_ref[g * VAUG:g * VAUG + d, :] = vt[g * d:(g + 1) * d, :].astype(BF16)
        avt_ref[g * VAUG + d:(g + 1) * VAUG, :] = onesblk
    iqt_ref[...] = proj(2304, 2816).T.astype(BF16)
    last = proj(2816, 2944)
    ik_ref[...] = last[:, :IDX_HD].astype(BF16)
    iwt_ref[...] = last.T[IDX_HD:IDX_HD + IDX_HEADS, :] * ((IDX_HD ** -0.5) * (IDX_HEADS ** -0.5))


def _inproj(x, norm_g, sc, sh, w_pad, tm):
    B, S, D = x.shape
    row = lambda w: pl.BlockSpec((None, tm, w), lambda b, i: (b, i, 0))
    colT = lambda h: pl.BlockSpec((None, h, tm), lambda b, i: (b, 0, i))
    vec = pl.BlockSpec((None, 1, D), lambda b, i: (b, 0, 0))
    sd = lambda shape, dt: jax.ShapeDtypeStruct(shape, dt)
    G = DSA_KV_HEADS
    return pl.pallas_call(
        _inproj_kernel,
        grid=(B, S // tm),
        in_specs=[row(D), pl.BlockSpec((1, D), lambda b, i: (0, 0)), vec, vec,
                  pl.BlockSpec((D, IN_COLS_PAD), lambda b, i: (0, 0))],
        out_specs=[row(256), row(256), row(512), row(512), colT(DSA_W), row(G * KAUG), colT(G * VAUG),
                   colT(IDX_HEADS * IDX_HD), row(IDX_HD), colT(IDX_HEADS)],
        out_shape=[sd((B, S, 256), BF16), sd((B, S, 256), BF16), sd((B, S, 512), BF16),
                   sd((B, S, 512), BF16), sd((B, DSA_W, S), BF16), sd((B, S, G * KAUG), BF16),
                   sd((B, G * VAUG, S), BF16), sd((B, IDX_HEADS * IDX_HD, S), BF16),
                   sd((B, S, IDX_HD), BF16), sd((B, IDX_HEADS, S), F32)],
        compiler_params=_cparams(("parallel", "parallel")),
        name="inproj",
    )(x, norm_g.reshape(1, D), sc, sh, w_pad)


def _ret_kernel(rq_ref, rk_ref, rv_ref, rg_ref, din_ref, qd_ref, kd_ref, cd_ref, ms_ref, o_ref, state_ref):
    @pl.when(pl.program_id(1) == 0)
    def _():
        state_ref[...] = jnp.zeros_like(state_ref)

    for h in range(RET_HEADS):
        q = rq_ref[:, h * RET_DK:(h + 1) * RET_DK]
        k = rk_ref[:, h * RET_DK:(h + 1) * RET_DK]
        v = rv_ref[:, h * RET_DV:(h + 1) * RET_DV]
        r_prev = state_ref[h]
        s = lax.dot_general(q, k, (((1,), (1,)), ((), ())), preferred_element_type=F32) * din_ref[h]
        o = jnp.dot(s.astype(BF16), v, preferred_element_type=F32)
        o = o + jnp.dot(q, r_prev.astype(BF16), preferred_element_type=F32) * qd_ref[h]
        vd = (v.astype(F32) * kd_ref[h]).astype(BF16)
        kv = lax.dot_general(k, vd, (((0,), (0,)), ((), ())), preferred_element_type=F32)
        state_ref[h] = r_prev * cd_ref[h] + kv
        o = o * lax.rsqrt(jnp.mean(o * o, axis=-1, keepdims=True) + EPS)
        g = rg_ref[:, h * RET_DV:(h + 1) * RET_DV].astype(F32)
        gate = g * (1.0 / (1.0 + jnp.exp(-g)))
        o_ref[:, h * RET_DV:(h + 1) * RET_DV] = (gate * o * ms_ref[:, h * RET_DV:(h + 1) * RET_DV]).astype(BF16)


def _ret_consts(C):
    H = RET_HEADS
    log_g = np.log1p(-np.exp2(-5.0 - np.arange(H, dtype=np.float64)))
    pos = np.arange(C, dtype=np.float64)
    diff = pos[:, None] - pos[None, :]
    d_inner = np.where(diff[None] >= 0, np.exp(np.maximum(diff, 0.0)[None] * log_g[:, None, None]), 0.0)
    q_decay = np.exp((pos + 1.0)[None] * log_g[:, None])
    k_decay = np.exp((C - 1.0 - pos)[None] * log_g[:, None])
    chunk_decay = np.exp(C * log_g)
    qd = np.broadcast_to(q_decay[:, :, None], (H, C, RET_DV))
    kd = np.broadcast_to(k_decay[:, :, None], (H, C, RET_DV))
    cd = np.broadcast_to(chunk_decay[:, None, None], (H, 1, RET_DV))
    f = lambda a: jnp.asarray(np.ascontiguousarray(a), F32)
    return f(d_inner), f(qd), f(kd), f(cd)


def _retention(rq, rk, rv, rg, ms_ret):
    B, S, _ = rq.shape
    C = min(RET_CHUNK, S)
    din, qd, kd, cd = _ret_consts(C)
    row = lambda w: pl.BlockSpec((None, C, w), lambda b, n: (b, n, 0))
    full = lambda a: pl.BlockSpec(a.shape, lambda b, n: (0,) * a.ndim)
    return pl.pallas_call(
        _ret_kernel,
        grid=(B, S // C),
        in_specs=[row(256), row(256), row(512), row(512), full(din), full(qd), full(kd), full(cd),
                  pl.BlockSpec((1, RET_W), lambda b, n: (0, 0))],
        out_specs=row(RET_W),
        out_shape=jax.ShapeDtypeStruct((B, S, RET_W), BF16),
        scratch_shapes=[pltpu.VMEM((RET_HEADS, RET_DK, RET_DV), F32)],
        compiler_params=_cparams(("parallel", "arbitrary")),
        name="ret",
    )(rq, rk, rv, rg, din, qd, kd, cd, ms_ret)


def _f32_key(x):
    i = lax.bitcast_convert_type(x, I32)
    return i ^ ((i >> 31) & 0x7FFFFFFF)


def _key_f32(k):
    return lax.bitcast_convert_type(k ^ ((k >> 31) & 0x7FFFFFFF), F32)


def _dsa_kernel(iqt_ref, iwt_ref, aqt_ref, ik_ref, ak_ref, avt_ref, ms_ref, o_ref, score_ref, qa_ref, sa_ref, sb_ref, mask_ref, *acc_refs,
                tq, tks, n_sel):
    H, G, R, d = DSA_HEADS, DSA_KV_HEADS, DSA_HEADS // DSA_KV_HEADS, DSA_HD
    t0 = pl.program_id(1) * tq
    nsub = (t0 + tq) // tks
    tka = 2 * tks
    npair = ((t0 + tq) // tka + 1) // 2
    kf = float(n_sel)
    qpos = t0 + lax.broadcasted_iota(I32, (1, tq), 1)
    krow = lax.broadcasted_iota(I32, (tks, tq), 0)

    wrow = [iwt_ref[h:h + 1, :] for h in range(IDX_HEADS)]

    def score_pair(i, carry):
        for u in range(2 * tka // tks):
            r0 = pl.multiple_of(i * 2 * tka + u * tks, tks)
            kc = ik_ref[pl.ds(r0, tks), :]
            acc = jnp.zeros((tks, tq), F32)
            for h in range(IDX_HEADS):
                rel = jnp.dot(kc, iqt_ref[h * IDX_HD:(h + 1) * IDX_HD, :], preferred_element_type=F32)
                acc = acc + jnp.maximum(rel, 0.0) * wrow[h]
            score_ref[pl.ds(r0, tks), :] = jnp.where(r0 + krow <= qpos, acc, -jnp.inf)
        return carry

    lax.fori_loop(0, npair, score_pair, 0)

    def fold8(x, op):
        acc = x[0:8, :]
        for i in range(1, tks // 8):
            acc = op(acc, x[8 * i:8 * (i + 1), :])
        return acc

    def count(th, strict):
        def body(j, acc):
            for u in range(2):
                s = score_ref[pl.ds(pl.multiple_of((2 * j + u) * tks, tks), tks), :]
                hit = (s > th) if strict else (s >= th)
                acc = acc + fold8(jnp.where(hit, 1.0, 0.0), jnp.add)
            return acc

        acc = lax.fori_loop(0, nsub // 2, body, jnp.zeros((8, tq), F32))
        return jnp.sum(acc, axis=0, keepdims=True)

    def minmax(j, carry):
        mx, mn = carry
        s = score_ref[pl.ds(pl.multiple_of(j * tks, tks), tks), :]
        mx = jnp.maximum(mx, fold8(s, jnp.maximum))
        mn = jnp.minimum(mn, fold8(jnp.where(s == -jnp.inf, jnp.inf, s), jnp.minimum))
        return mx, mn

    mx, mn = lax.fori_loop(0, nsub, minmax, (jnp.full((8, tq), -jnp.inf, F32), jnp.full((8, tq), jnp.inf, F32)))
    hi0 = jnp.max(mx, axis=0, keepdims=True)
    lo0 = jnp.min(mn, axis=0, keepdims=True)

    def probe(lo, hi, it):
        lk, hk = _f32_key(lo), _f32_key(hi)
        mk = (lk >> 1) + (hk >> 1) + (lk & hk & 1)
        mv = lo + (hi - lo) * 0.5
        early = (jnp.zeros((1, tq), I32) + it) < BISECT_VALUE_STEPS
        mid = jnp.where(early & (mv > lo) & (mv < hi), mv, _key_f32(mk))
        return mid, jnp.max(jnp.where(mk != lk, 1.0, 0.0))

    def bis_cond(c):
        return (c[4] > 0.0) & (c[5] < BISECT_MAX_STEPS)

    def bis_body(c):
        lo, hi, c_lo, c_hi, _, it = c
        mid, _ = probe(lo, hi, it)
        cnt = count(mid, False)
        ge = cnt >= kf
        up = ge | (cnt == kf)
        dn = (~ge) | (cnt == kf)
        lo, c_lo = jnp.where(up, mid, lo), jnp.where(up, cnt, c_lo)
        hi, c_hi = jnp.where(dn, mid, hi), jnp.where(dn, cnt, c_hi)
        _, active = probe(lo, hi, it + 1)
        return lo, hi, c_lo, c_hi, active, it + 1

    zero = jnp.zeros((1, tq), F32)
    n_pos = count(zero, True)
    n_nonneg = count(zero, False)
    keep_all = qpos + 1 <= n_sel
    settled = keep_all | ((n_nonneg >= kf) & (n_pos < kf))
    above = n_pos >= kf
    c_lo0 = jnp.where(settled | above, n_nonneg, (qpos + 1).astype(F32))
    c_hi0 = jnp.where(settled | ~above, n_nonneg, count(hi0, False))
    lo0 = jnp.where(settled | above, zero, lo0)
    hi0 = jnp.where(settled | ~above, zero, hi0)
    _, active0 = probe(lo0, hi0, jnp.int32(0))
    lo, hi, c_lo, c_hi, _, _ = lax.while_loop(bis_cond, bis_body, (lo0, hi0, c_lo0, c_hi0, active0, jnp.int32(0)))
    at_hi = c_hi >= kf
    thr = jnp.where(keep_all, F32_LOWEST, jnp.where(at_hi, hi, lo))
    excess = jnp.where(keep_all, 0.0, jnp.where(at_hi, c_hi, c_lo) - kf)

    @pl.when(jnp.max(excess) > 0.0)
    def _():
        budget = kf - count(thr, True)
        earlier = lax.broadcasted_iota(I32, (tks, tks), 1) < lax.broadcasted_iota(I32, (tks, tks), 0)
        earlier = jnp.where(earlier, 1.0, 0.0).astype(BF16)

        def fix(j, seen):
            r0 = pl.multiple_of(j * tks, tks)
            s = score_ref[pl.ds(r0, tks), :]
            eq = s == thr
            eqf = jnp.where(eq, 1.0, 0.0)
            rank = jnp.dot(earlier, eqf.astype(BF16), preferred_element_type=F32) + seen
            score_ref[pl.ds(r0, tks), :] = jnp.where(eq & (rank >= budget), -jnp.inf, s)
            return seen + jnp.sum(eqf, axis=0, keepdims=True)

        lax.fori_loop(0, nsub, fix, jnp.zeros((1, tq), F32))

    arow = lax.broadcasted_iota(I32, (KAUG - d, tq), 0)
    for h in range(H):
        slope = float(2.0 ** (-8.0 * (h + 1) / H))
        qa_ref[h, 0:d, :] = aqt_ref[h * d:(h + 1) * d, :]
        qa_ref[h, d:KAUG, :] = jnp.where(arow == 0, slope * ALIBI_SPLIT, jnp.where(arow == 1, slope, 0.0)).astype(BF16)
    for acc in acc_refs:
        acc[...] = jnp.zeros_like(acc)

    def logits(j, h):
        ka = ak_ref[pl.ds(pl.multiple_of(j * tka, tka), tka), (h // R) * KAUG:(h // R + 1) * KAUG]
        return jnp.dot(ka, qa_ref[h], preferred_element_type=F32)

    def step(j, j_next, cur_ref, next_ref, ms):
        r0 = pl.multiple_of(j * tka, tka)
        mask_ref[...] = jnp.where(score_ref[pl.ds(r0, tka), :] >= thr, 0.0, NEG_BIG)
        nms = []
        for h in range(H):
            g = h // R
            next_ref[h] = logits(j_next, h)
            s = cur_ref[h] + mask_ref[...]
            m_new = jnp.maximum(ms[h], jnp.max(s, axis=0, keepdims=True))
            p = jnp.exp(s - m_new).astype(BF16)
            va = avt_ref[g * VAUG:(g + 1) * VAUG, pl.ds(r0, tka)]
            acc = acc_refs[h]
            acc[...] = acc[...] * jnp.exp(ms[h] - m_new) + jnp.dot(va, p, preferred_element_type=F32)
            nms.append(m_new)
        return tuple(nms)

    for h in range(H):
        sa_ref[h] = logits(0, h)

    def att_pair(i, ms):
        ms = step(2 * i, 2 * i + 1, sa_ref, sb_ref, ms)
        return step(2 * i + 1, jnp.minimum(2 * i + 2, 2 * npair - 1), sb_ref, sa_ref, ms)

    lax.fori_loop(0, npair, att_pair, tuple(jnp.full((1, tq), NEG_BIG, F32) for _ in range(H)))
    for h in range(H):
        a = acc_refs[h][...]
        o = a[0:d, :] / a[d:d + 1, :]
        o_ref[h * d:(h + 1) * d, :] = (o * ms_ref[h * d:(h + 1) * d, :]).astype(BF16)


def _dsa(iqt, iwt, aqt, ik, ak, avt, ms_att, tq, tks):
    B, _, S = iqt.shape
    assert S <= ALIBI_SPLIT * 256 and tq % (2 * tks) == 0
    assert (S // (2 * tks)) % 2 == 0
    n_sel = min(TOPK_MAX, S // 4)
    G = DSA_KV_HEADS
    colT = lambda h: pl.BlockSpec((None, h, tq), lambda b, i: (b, 0, i))
    msb = jnp.broadcast_to(ms_att.reshape(DSA_W, 1), (DSA_W, tq))
    return pl.pallas_call(
        functools.partial(_dsa_kernel, tq=tq, tks=tks, n_sel=n_sel),
        grid=(B, S // tq),
        in_specs=[colT(IDX_HEADS * IDX_HD), colT(IDX_HEADS), colT(DSA_W),
                  pl.BlockSpec((None, S, IDX_HD), lambda b, i: (b, 0, 0)),
                  pl.BlockSpec((None, S, G * KAUG), lambda b, i: (b, 0, 0)),
                  pl.BlockSpec((None, G * VAUG, S), lambda b, i: (b, 0, 0)),
                  pl.BlockSpec((DSA_W, tq), lambda b, i: (0, 0))],
        out_specs=colT(DSA_W),
        out_shape=jax.ShapeDtypeStruct((B, DSA_W, S), BF16),
        scratch_shapes=[pltpu.VMEM((S, tq), F32), pltpu.VMEM((DSA_HEADS, KAUG, tq), BF16),
                        pltpu.VMEM((DSA_HEADS, 2 * tks, tq), F32), pltpu.VMEM((DSA_HEADS, 2 * tks, tq), F32),
                        pltpu.VMEM((2 * tks, tq), F32)]
        + [pltpu.VMEM((VAUG, tq), F32) for _ in range(DSA_HEADS)],
        compiler_params=_cparams(("parallel", "arbitrary")),
        name="dsa",
    )(iqt, iwt, aqt, ik, ak, avt, msb)


def _oproj_kernel(ret_ref, att_ref, x_ref, wo_ref, g1_ref, n2_ref, sc_ref, sh_ref, rw_ref, rb_ref,
                  x1_ref, h2_ref, sel_ref, idx_ref, gate_ref, cnt_ref):
    mixo = jnp.dot(ret_ref[...], wo_ref[:RET_W, :], preferred_element_type=F32)
    mixo = mixo + lax.dot_general(att_ref[...], wo_ref[RET_W:, :], (((0,), (0,)), ((), ())),
                                  preferred_element_type=F32)
    x1 = x_ref[...] + g1_ref[...] * mixo
    x1_ref[...] = x1
    y = x1 * lax.rsqrt(jnp.mean(x1 * x1, axis=-1, keepdims=True) + EPS) * n2_ref[...]
    h2 = y * (1.0 + sc_ref[...]) + sh_ref[...]
    _store_token_major(h2_ref, h2)
    logits = jnp.dot(h2, rw_ref[...], preferred_element_type=F32,
                     precision=lax.Precision.HIGHEST) + rb_ref[...]
    tm = logits.shape[0]
    lane = lax.broadcasted_iota(I32, (tm, LANES), 1).astype(F32)
    work = jnp.where(lane < N_EXPERTS, logits, -jnp.inf)
    sel = jnp.zeros((tm, LANES), F32)
    idx_tab = jnp.zeros((tm, LANES), F32)
    vals = []
    for k in range(TOP_K):
        m = jnp.max(work, axis=1, keepdims=True)
        idx = jnp.min(jnp.where(work == m, lane, float(LANES)), axis=1, keepdims=True)
        hit = lane == idx
        sel = jnp.where(hit, 1.0, sel)
        idx_tab = jnp.where(lane == k, idx, idx_tab)
        work = jnp.where(hit, -jnp.inf, work)
        vals.append(m)
    es = [jnp.exp(v - vals[0]) for v in vals]
    den = es[0] + es[1] + es[2] + es[3]
    gate_tab = jnp.zeros((tm, LANES), F32)
    for k in range(TOP_K):
        gate_tab = jnp.where(lane == k, es[k] / den, gate_tab)
    sel_ref[...] = sel
    idx_ref[...] = idx_tab
    gate_ref[...] = gate_tab

    @pl.when((pl.program_id(0) == 0) & (pl.program_id(1) == 0))
    def _():
        cnt_ref[...] = jnp.zeros_like(cnt_ref)

    cnt_ref[...] += jnp.sum(sel, axis=0, keepdims=True)


def _oproj(ret, att, x, wo, g1, n2g, sc2, sh2, rw_pad, rb_pad, tm):
    B, S, D = x.shape
    nt = S // tm
    row = lambda w: pl.BlockSpec((None, tm, w), lambda b, i: (b, i, 0))
    flat = lambda w: pl.BlockSpec((tm, w), lambda b, i: (b * nt + i, 0))
    vec = pl.BlockSpec((None, 1, D), lambda b, i: (b, 0, 0))
    cst = lambda shape: pl.BlockSpec(shape, lambda b, i: (0, 0))
    sd = lambda shape, dt: jax.ShapeDtypeStruct(shape, dt)
    N = B * S
    return pl.pallas_call(
        _oproj_kernel,
        grid=(B, nt),
        in_specs=[row(RET_W), pl.BlockSpec((None, DSA_W, tm), lambda b, i: (b, 0, i)), row(D), cst((D, D)), vec,
                  cst((1, D)), vec, vec,
                  cst((D, LANES)), cst((1, LANES))],
        out_specs=[flat(D), pl.BlockSpec((tm * (D // LANES), LANES), lambda b, i: (b * nt + i, 0)),
                   flat(LANES), flat(LANES), flat(LANES), cst((1, LANES))],
        out_shape=[sd((N, D), F32), sd((N * (D // LANES), LANES), F32), sd((N, LANES), F32), sd((N, LANES), F32),
                   sd((N, LANES), F32), sd((1, LANES), F32)],
        compiler_params=_cparams(("arbitrary", "arbitrary")),
        name="oproj",
    )(ret, att, x, wo, g1, n2g.reshape(1, D), sc2, sh2, rw_pad, rb_pad)


def _dest_kernel(sel_ref, idx_ref, pstart_ref, dest_ref, seen_ref):
    @pl.when(pl.program_id(0) == 0)
    def _():
        seen_ref[...] = jnp.zeros_like(seen_ref)

    sel = sel_ref[...]
    tm = sel.shape[0]
    earlier = lax.broadcasted_iota(I32, (tm, tm), 1) < lax.broadcasted_iota(I32, (tm, tm), 0)
    earlier = jnp.where(earlier, 1.0, 0.0).astype(BF16)
    rank = jnp.dot(earlier, sel.astype(BF16), preferred_element_type=F32) + seen_ref[...]
    dest = pstart_ref[...] + rank
    lane = lax.broadcasted_iota(I32, (tm, LANES), 1).astype(F32)
    idx_tab = idx_ref[...]
    out = jnp.zeros((tm, LANES), F32)
    for k in range(TOP_K):
        e_k = jnp.sum(jnp.where(lane == k, idx_tab, 0.0), axis=1, keepdims=True)
        d_k = jnp.sum(jnp.where(lane == e_k, dest, 0.0), axis=1, keepdims=True)
        out = jnp.where(lane == k, d_k, out)
    dest_ref[...] = out.astype(I32)
    seen_ref[...] += jnp.sum(sel, axis=0, keepdims=True)


def _dest(sel, idx_tab, pstart, tm):
    N = sel.shape[0]
    blk = pl.BlockSpec((tm, LANES), lambda i: (i, 0))
    return pl.pallas_call(
        _dest_kernel,
        grid=(N // tm,),
        in_specs=[blk, blk, pl.BlockSpec((1, LANES), lambda i: (0, 0))],
        out_specs=blk,
        out_shape=jax.ShapeDtypeStruct((N, LANES), I32),
        scratch_shapes=[pltpu.VMEM((1, LANES), F32)],
        compiler_params=_cparams(("arbitrary",)),
        name="dest",
    )(sel, idx_tab, pstart)


def _disp_kernel(zs_ref, h2_ref, dest_hbm, xs_hbm, zbuf, idx_a, idx_b, sem_ia, sem_ib, sem_row, sem_z,
                 *, tm, nch, zrows):
    i = pl.program_id(0)
    n = tm * TOP_K

    def idx_copy(tile, buf, sem):
        return pltpu.make_async_copy(dest_hbm.at[pl.ds(tile * n, n)], buf, sem)

    def rows(ref, first, count):
        return ref.at[pl.ds(pl.multiple_of(first * nch, nch), count * nch)]

    def row_copy(row, dst):
        return pltpu.make_async_copy(rows(h2_ref, row, 1), rows(xs_hbm, dst, 1), sem_row)

    def issue_rows(first_row, idx):
        def body(t, c):
            for k in range(TOP_K):
                row_copy(first_row + t, idx[t * TOP_K + k]).start(priority=k % 2)
            return c

        lax.fori_loop(0, tm, body, 0)

    def wait_rows():
        def body(t, c):
            for k in range(TOP_K):
                row_copy(0, 0).wait()
            return c

        lax.fori_loop(0, tm, body, 0)

    @pl.when(i == 0)
    def _():
        zbuf[...] = jnp.zeros_like(zbuf)
        for e in range(N_EXPERTS):
            fill = pltpu.make_async_copy(zbuf, rows(xs_hbm, zs_ref[e], zrows), sem_z)
            fill.start()
            fill.wait()

    idx_copy(2 * i, idx_a, sem_ia).start()
    idx_copy(2 * i + 1, idx_b, sem_ib).start()
    idx_copy(2 * i, idx_a, sem_ia).wait()
    issue_rows(0, idx_a)
    idx_copy(2 * i + 1, idx_b, sem_ib).wait()
    issue_rows(tm, idx_b)
    wait_rows()
    wait_rows()


def _dispatch(h2, dest_flat, zero_start, n_rows, zrows, tm):
    nch = D_MODEL // LANES
    n_steps = h2.shape[0] // nch // (2 * tm)
    grid_spec = pltpu.PrefetchScalarGridSpec(
        num_scalar_prefetch=1,
        grid=(n_steps,),
        in_specs=[pl.BlockSpec((2 * tm * nch, LANES), lambda i, zs: (i, 0)), pl.BlockSpec(memory_space=pl.ANY)],
        out_specs=pl.BlockSpec(memory_space=pl.ANY),
        scratch_shapes=[pltpu.VMEM((zrows * nch, LANES), F32), pltpu.SMEM((tm * TOP_K,), I32),
                        pltpu.SMEM((tm * TOP_K,), I32), pltpu.SemaphoreType.DMA, pltpu.SemaphoreType.DMA,
                        pltpu.SemaphoreType.DMA, pltpu.SemaphoreType.DMA],
    )
    return pl.pallas_call(
        functools.partial(_disp_kernel, tm=tm, nch=nch, zrows=zrows),
        grid_spec=grid_spec,
        out_shape=jax.ShapeDtypeStruct((n_rows * nch, LANES), F32),
        compiler_params=_cparams(("arbitrary",)),
        name="disp",
    )(zero_start, h2, dest_flat)


def _store_token_major(ref, x):
    rows, d = x.shape
    nch = d // LANES
    for j in range(nch):
        ref[pl.ds(j, rows, stride=nch), :] = x[:, j * LANES:(j + 1) * LANES]


def _load_token_major(ref, rows, nch):
    return jnp.concatenate([ref[pl.ds(j, rows, stride=nch), :] for j in range(nch)], axis=1)


def _ffn_kernel(be_ref, nb_ref, xs_ref, wgu_ref, bgu_ref, wd_ref, bd_ref, ys_ref, wgu_bf, wd_bf):
    i = pl.program_id(0)
    live = i < nb_ref[0]

    @pl.when(live & ((i == 0) | (be_ref[i] != be_ref[jnp.maximum(i - 1, 0)])))
    def _():
        wgu_bf[...] = wgu_ref[...].astype(BF16)
        wd_bf[...] = wd_ref[...].astype(BF16)

    @pl.when(live)
    def _():
        xb = _load_token_major(xs_ref, ys_ref.shape[0] // NCH, NCH).astype(BF16)
        gu = jnp.dot(xb, wgu_bf[...], preferred_element_type=F32) + bgu_ref[...]
        gate = jnp.minimum(gu[:, :D_EXPERT], SWIGLU_LIMIT)
        up = jnp.clip(gu[:, D_EXPERT:], -SWIGLU_LIMIT, SWIGLU_LIMIT)
        glu = gate * (1.0 / (1.0 + jnp.exp(-SWIGLU_ALPHA * gate)))
        act = ((up + 1.0) * glu).astype(BF16)
        ys = jnp.dot(act, wd_bf[...], preferred_element_type=F32) + bd_ref[...]
        _store_token_major(ys_ref, ys)

    @pl.when(jnp.logical_not(live))
    def _():
        ys_ref[...] = jnp.zeros_like(ys_ref)


def _ffn(xs, block_e, n_blocks, w_gu, b_gu, w_down, b_down, tmb):
    D = D_MODEL
    P = xs.shape[0] // NCH
    E = w_gu.shape[0]
    blk = lambda i, be, nb: (jnp.minimum(i, nb[0] - 1), 0)
    wsel = lambda i, be, nb: (be[jnp.minimum(i, nb[0] - 1)], 0, 0)
    grid_spec = pltpu.PrefetchScalarGridSpec(
        num_scalar_prefetch=2,
        grid=(P // tmb,),
        in_specs=[pl.BlockSpec((tmb * NCH, LANES), blk),
                  pl.BlockSpec((None, D, 2 * D_EXPERT), wsel),
                  pl.BlockSpec((None, 1, 2 * D_EXPERT), wsel),
                  pl.BlockSpec((None, D_EXPERT, D), wsel),
                  pl.BlockSpec((None, 1, D), wsel)],
        out_specs=pl.BlockSpec((tmb * (D // LANES), LANES), lambda i, be, nb: (i, 0)),
        scratch_shapes=[pltpu.VMEM((D, 2 * D_EXPERT), BF16), pltpu.VMEM((D_EXPERT, D), BF16)],
    )
    return pl.pallas_call(
        _ffn_kernel,
        grid_spec=grid_spec,
        out_shape=jax.ShapeDtypeStruct((P * (D // LANES), LANES), F32),
        compiler_params=_cparams(("arbitrary",)),
        name="ffn",
    )(block_e, n_blocks, xs, w_gu, b_gu.reshape(E, 1, 2 * D_EXPERT), w_down, b_down.reshape(E, 1, D))


def _comb_kernel(x1_ref, gate_ref, g2_ref, fg_ref, dest_hbm, ys_hbm, o_ref, buf_a, buf_b, idx_a, idx_b,
                 sem_ia, sem_ib, sem_ra, sem_rb, *, tm, n_steps):
    i = pl.program_id(0)
    n = tm * TOP_K

    def idx_copy(tile, buf, sem):
        return pltpu.make_async_copy(dest_hbm.at[pl.ds(tile * n, n)], buf, sem)

    nch = x1_ref.shape[1] // LANES

    def row_copy(src, buf, k, t, sem):
        return pltpu.make_async_copy(ys_hbm.at[pl.ds(pl.multiple_of(src * nch, nch), nch)],
                                     buf.at[k, pl.ds(pl.multiple_of(t * nch, nch), nch)], sem)

    def issue_rows(idx, buf, sem):
        def body(t, c):
            for k in range(TOP_K):
                row_copy(idx[t * TOP_K + k], buf, k, t, sem).start(priority=k % 2)
            return c

        lax.fori_loop(0, tm, body, 0)

    def wait_rows(buf, sem):
        def body(t, c):
            for k in range(TOP_K):
                row_copy(0, buf, k, 0, sem).wait()
            return c

        lax.fori_loop(0, tm, body, 0)

    def finish(buf, lo):
        gates = gate_ref[lo:lo + tm, :]
        y = jnp.zeros((tm, x1_ref.shape[1]), F32)
        for k in range(TOP_K):
            y = y + gates[:, k:k + 1] * _load_token_major(buf.at[k], tm, nch)
        v = x1_ref[lo:lo + tm, :] + g2_ref[...] * y
        o_ref[lo:lo + tm, :] = v * lax.rsqrt(jnp.mean(v * v, axis=-1, keepdims=True) + EPS) * fg_ref[...]

    @pl.when(i == 0)
    def _():
        first = idx_copy(0, idx_a, sem_ia)
        first.start()
        first.wait()
        issue_rows(idx_a, buf_a, sem_ra)
        idx_copy(1, idx_b, sem_ib).start()

    idx_copy(2 * i + 1, idx_b, sem_ib).wait()
    issue_rows(idx_b, buf_b, sem_rb)

    @pl.when(i + 1 < n_steps)
    def _():
        idx_copy(2 * i + 2, idx_a, sem_ia).start()

    wait_rows(buf_a, sem_ra)
    finish(buf_a, 0)

    @pl.when(i + 1 < n_steps)
    def _():
        idx_copy(2 * i + 2, idx_a, sem_ia).wait()
        issue_rows(idx_a, buf_a, sem_ra)
        idx_copy(2 * i + 3, idx_b, sem_ib).start()

    wait_rows(buf_b, sem_rb)
    finish(buf_b, tm)


def _combine(x1, gate_tab, g2, final_g, dest_flat, ys, S, tm):
    N, D = x1.shape
    n_steps = N // (2 * tm)
    per_b = S // (2 * tm)
    return pl.pallas_call(
        functools.partial(_comb_kernel, tm=tm, n_steps=n_steps),
        grid=(n_steps,),
        in_specs=[pl.BlockSpec((2 * tm, D), lambda i: (i, 0)),
                  pl.BlockSpec((2 * tm, LANES), lambda i: (i, 0)),
                  pl.BlockSpec((None, 1, D), lambda i: (i // per_b, 0, 0)),
                  pl.BlockSpec((1, D), lambda i: (0, 0)),
                  pl.BlockSpec(memory_space=pl.ANY),
                  pl.BlockSpec(memory_space=pl.ANY)],
        out_specs=pl.BlockSpec((2 * tm, D), lambda i: (i, 0)),
        out_shape=jax.ShapeDtypeStruct((N, D), F32),
        scratch_shapes=[pltpu.VMEM((TOP_K, tm * (D // LANES), LANES), F32),
                        pltpu.VMEM((TOP_K, tm * (D // LANES), LANES), F32),
                        pltpu.SMEM((tm * TOP_K,), I32), pltpu.SMEM((tm * TOP_K,), I32),
                        pltpu.SemaphoreType.DMA, pltpu.SemaphoreType.DMA, pltpu.SemaphoreType.DMA,
                        pltpu.SemaphoreType.DMA],
        compiler_params=_cparams(("arbitrary",)),
        name="comb",
    )(x1, gate_tab, g2, final_g.reshape(1, D), dest_flat, ys)


def _tile(n, pref):
    t = min(pref, n)
    assert n % t == 0, (n, t)
    return t


def _layer(x, c, ada_w, ada_b, norm1_g, w_in, mix_scale, w_o, norm2_g,
           router_w, router_b, w_gu, b_gu, w_down, b_down, final_g):
    B, S, D = x.shape
    N = B * S
    mod = _mod(c, ada_w, ada_b).reshape(B, 6, 1, D)
    sh1, sc1, g1, sh2, sc2, g2 = (mod[:, j] for j in range(6))

    w_pad = jnp.pad(w_in, ((0, 0), (0, IN_COLS_PAD - IN_COLS))).astype(BF16)
    rq, rk, rv, rg, aqt, ak, avt, iqt, ik, iwt = _inproj(x, norm1_g, sc1, sh1, w_pad, _tile(S, 512))
    ms = mix_scale.reshape(1, RET_W + DSA_W)
    ret = _retention(rq, rk, rv, rg, ms[:, :RET_W])
    att = _dsa(iqt, iwt, aqt, ik, ak, avt, ms[:, RET_W:], _tile(S, 256), 128)

    rw_pad = jnp.pad(router_w, ((0, 0), (0, LANES - N_EXPERTS)))
    rb_pad = jnp.pad(router_b, (0, LANES - N_EXPERTS)).reshape(1, LANES)
    x1, h2, sel, idx_tab, gate_tab, counts = _oproj(ret, att, x, w_o.astype(BF16), g1, norm2_g, sc2, sh2,
                                                    rw_pad, rb_pad, _tile(S, 512))

    tmb = FFN_ROWS
    n_rows = (N * TOP_K + N_EXPERTS * (tmb - 1)) // tmb * tmb + tmb
    cnt = counts[0, :N_EXPERTS].astype(I32)
    padded = (cnt + tmb - 1) // tmb * tmb
    ends = jnp.cumsum(padded)
    starts = ends - padded
    pstart = jnp.pad(starts.astype(F32), (0, LANES - N_EXPERTS)).reshape(1, LANES)
    n_blocks = (ends[-1] // tmb).reshape(1)
    first_row = jnp.arange(n_rows // tmb, dtype=I32) * tmb
    block_e = jnp.minimum(jnp.sum((ends[None, :] <= first_row[:, None]).astype(I32), axis=1), N_EXPERTS - 1)

    tmd = _tile(N, 256)
    dest_tab = _dest(sel, idx_tab, pstart, tmd)
    dest_flat = dest_tab[:, :TOP_K].reshape(N * TOP_K)
    xs = _dispatch(h2, dest_flat, starts + cnt, n_rows, tmb, tmd)
    ys = _ffn(xs, block_e, n_blocks, w_gu, b_gu, w_down, b_down, tmb)
    out = _combine(x1, gate_tab, g2, final_g, dest_flat, ys, S, _tile(S, 256))
    return out.reshape(B, S, D)


def kernel(x, c, ada_w, ada_b, norm1_g, w_in, mix_scale, w_o, norm2_g, router_w, router_b, w_gu, b_gu,
           w_down, b_down, final_g):
    assert ada_w.shape[0] == 1, "single-layer stack"
    return _layer(x, c, ada_w[0], ada_b[0], norm1_g[0], w_in[0], mix_scale[0], w_o[0], norm2_g[0],
                  router_w[0], router_b[0], w_gu[0], b_gu[0], w_down[0], b_down[0], final_g)
```

```python
import functools

import numpy as np
import jax
import jax.numpy as jnp
from jax import lax
from jax.experimental import pallas as pl
from jax.experimental.pallas import tpu as pltpu

F32 = jnp.float32
BF16 = jnp.bfloat16
I32 = jnp.int32

D_MODEL = 1024
RET_HEADS = 4
RET_DK = 64
RET_DV = 128
RET_CHUNK = 128
DSA_HEADS = 8
DSA_KV_HEADS = 2
DSA_HD = 64
IDX_HEADS = 8
IDX_HD = 64
TOPK_MAX = 256
N_EXPERTS = 32
TOP_K = 4
D_EXPERT = D_MODEL
SWIGLU_LIMIT = 7.0
SWIGLU_ALPHA = 1.702
EPS = 1e-6

RET_W = RET_HEADS * RET_DV
DSA_W = DSA_HEADS * DSA_HD
IN_COLS = 2888
IN_COLS_PAD = 2944

KAUG = 128
VAUG = 80
ALIBI_SPLIT = 64
FFN_ROWS = 512
BISECT_VALUE_STEPS = 8
BISECT_MAX_STEPS = 64

LANES = 128
NCH = D_MODEL // LANES
VMEM_LIMIT = 56 * 1024 * 1024
NEG_BIG = -1e30
F32_LOWEST = float(np.finfo(np.float32).min)


def _cparams(sem):
    return pltpu.CompilerParams(dimension_semantics=sem, vmem_limit_bytes=VMEM_LIMIT)


def _mod_kernel(c_ref, w_ref, b_ref, o_ref):
    c = c_ref[...]
    s = c * (1.0 / (1.0 + jnp.exp(-c)))
    o_ref[...] = jnp.dot(s, w_ref[...], preferred_element_type=F32,
                         precision=lax.Precision.HIGHEST) + b_ref[...]


def _mod(c, ada_w, ada_b):
    B, D = c.shape
    n_out = ada_w.shape[1]
    rows = 8
    c8 = jnp.zeros((rows, D), F32).at[:B].set(c)
    out = pl.pallas_call(
        _mod_kernel,
        grid=(n_out // D,),
        in_specs=[pl.BlockSpec((rows, D), lambda j: (0, 0)),
                  pl.BlockSpec((D, D), lambda j: (0, j)),
                  pl.BlockSpec((1, D), lambda j: (0, j))],
        out_specs=pl.BlockSpec((rows, D), lambda j: (0, j)),
        out_shape=jax.ShapeDtypeStruct((rows, n_out), F32),
        compiler_params=_cparams(("arbitrary",)),
        name="mod",
    )(c8, ada_w, ada_b.reshape(1, n_out))
    return out[:B]


def _inproj_kernel(x_ref, g_ref, sc_ref, sh_ref, w_ref,
                   rq_ref, rk_ref, rv_ref, rg_ref, aqt_ref, ak_ref, avt_ref, iqt_ref, ik_ref, iwt_ref):
    x = x_ref[...]
    ms = jnp.mean(x * x, axis=-1, keepdims=True)
    y = x * lax.rsqrt(ms + EPS) * g_ref[...]
    hb = (y * (1.0 + sc_ref[...]) + sh_ref[...]).astype(BF16)

    def proj(lo, hi):
        return jnp.dot(hb, w_ref[:, lo:hi], preferred_element_type=F32)

    tm = x.shape[0]
    d = DSA_HD
    rq_ref[...] = proj(0, 256).astype(BF16)
    rk_ref[...] = (proj(256, 512) * (RET_DK ** -0.5)).astype(BF16)
    rv_ref[...] = proj(512, 1024).astype(BF16)
    rg_ref[...] = proj(1024, 1536).astype(BF16)
    aqt_ref[...] = (proj(1536, 2048) * (d ** -0.5)).T.astype(BF16)
    kk = proj(2048, 2176)
    pos = pl.program_id(1) * tm + lax.broadcasted_iota(I32, (tm, d), 0)
    col = lax.broadcasted_iota(I32, (tm, d), 1)
    posblk = jnp.where(col == 0, pos // ALIBI_SPLIT, jnp.where(col == 1, pos % ALIBI_SPLIT, 0)).astype(F32)
    for g in range(DSA_KV_HEADS):
        ak_ref[:, g * KAUG:g * KAUG + d] = kk[:, g * d:(g + 1) * d].astype(BF16)
        ak_ref[:, g * KAUG + d:(g + 1) * KAUG] = posblk.astype(BF16)
    vt = proj(2176, 2304).T
    r16 = lax.broadcasted_iota(I32, (VAUG - d, tm), 0)
    onesblk = jnp.where(r16 == 0, 1.0, 0.0).astype(BF16)
    for g in range(DSA_KV_HEADS):
        avt_ref[g * VAUG:g * VAUG + d, :] = vt[g * d:(g + 1) * d, :].astype(BF16)
        avt_ref[g * VAUG + d:(g + 1) * VAUG, :] = onesblk
    iqt_ref[...] = proj(2304, 2816).T.astype(BF16)
    last = proj(2816, 2944)
    ik_ref[...] = last[:, :IDX_HD].astype(BF16)
    iwt_ref[...] = last.T[IDX_HD:IDX_HD + IDX_HEADS, :] * ((IDX_HD ** -0.5) * (IDX_HEADS ** -0.5))


def _inproj(x, norm_g, sc, sh, w_pad, tm):
    B, S, D = x.shape
    row = lambda w: pl.BlockSpec((None, tm, w), lambda b, i: (b, i, 0))
    colT = lambda h: pl.BlockSpec((None, h, tm), lambda b, i: (b, 0, i))
    vec = pl.BlockSpec((None, 1, D), lambda b, i: (b, 0, 0))
    sd = lambda shape, dt: jax.ShapeDtypeStruct(shape, dt)
    G = DSA_KV_HEADS
    return pl.pallas_call(
        _inproj_kernel,
        grid=(B, S // tm),
        in_specs=[row(D), pl.BlockSpec((1, D), lambda b, i: (0, 0)), vec, vec,
                  pl.BlockSpec((D, IN_COLS_PAD), lambda b, i: (0, 0))],
        out_specs=[row(256), row(256), row(512), row(512), colT(DSA_W), row(G * KAUG), colT(G * VAUG),
                   colT(IDX_HEADS * IDX_HD), row(IDX_HD), colT(IDX_HEADS)],
        out_shape=[sd((B, S, 256), BF16), sd((B, S, 256), BF16), sd((B, S, 512), BF16),
                   sd((B, S, 512), BF16), sd((B, DSA_W, S), BF16), sd((B, S, G * KAUG), BF16),
                   sd((B, G * VAUG, S), BF16), sd((B, IDX_HEADS * IDX_HD, S), BF16),
                   sd((B, S, IDX_HD), BF16), sd((B, IDX_HEADS, S), F32)],
        compiler_params=_cparams(("parallel", "parallel")),
        name="inproj",
    )(x, norm_g.reshape(1, D), sc, sh, w_pad)


def _ret_kernel(rq_ref, rk_ref, rv_ref, rg_ref, din_ref, qd_ref, kd_ref, cd_ref, ms_ref, o_ref, state_ref):
    @pl.when(pl.program_id(1) == 0)
    def _():
        state_ref[...] = jnp.zeros_like(state_ref)

    for h in range(RET_HEADS):
        q = rq_ref[:, h * RET_DK:(h + 1) * RET_DK]
        k = rk_ref[:, h * RET_DK:(h + 1) * RET_DK]
        v = rv_ref[:, h * RET_DV:(h + 1) * RET_DV]
        r_prev = state_ref[h]
        s = lax.dot_general(q, k, (((1,), (1,)), ((), ())), preferred_element_type=F32) * din_ref[h]
        o = jnp.dot(s.astype(BF16), v, preferred_element_type=F32)
        o = o + jnp.dot(q, r_prev.astype(BF16), preferred_element_type=F32) * qd_ref[h]
        vd = (v.astype(F32) * kd_ref[h]).astype(BF16)
        kv = lax.dot_general(k, vd, (((0,), (0,)), ((), ())), preferred_element_type=F32)
        state_ref[h] = r_prev * cd_ref[h] + kv
        o = o * lax.rsqrt(jnp.mean(o * o, axis=-1, keepdims=True) + EPS)
        g = rg_ref[:, h * RET_DV:(h + 1) * RET_DV].astype(F32)
        gate = g * (1.0 / (1.0 + jnp.exp(-g)))
        o_ref[:, h * RET_DV:(h + 1) * RET_DV] = (gate * o * ms_ref[:, h * RET_DV:(h + 1) * RET_DV]).astype(BF16)


def _ret_consts(C):
    H = RET_HEADS
    log_g = np.log1p(-np.exp2(-5.0 - np.arange(H, dtype=np.float64)))
    pos = np.arange(C, dtype=np.float64)
    diff = pos[:, None] - pos[None, :]
    d_inner = np.where(diff[None] >= 0, np.exp(np.maximum(diff, 0.0)[None] * log_g[:, None, None]), 0.0)
    q_decay = np.exp((pos + 1.0)[None] * log_g[:, None])
    k_decay = np.exp((C - 1.0 - pos)[None] * log_g[:, None])
    chunk_decay = np.exp(C * log_g)
    qd = np.broadcast_to(q_decay[:, :, None], (H, C, RET_DV))
    kd = np.broadcast_to(k_decay[:, :, None], (H, C, RET_DV))
    cd = np.broadcast_to(chunk_decay[:, None, None], (H, 1, RET_DV))
    f = lambda a: jnp.asarray(np.ascontiguousarray(a), F32)
    return f(d_inner), f(qd), f(kd), f(cd)


def _retention(rq, rk, rv, rg, ms_ret):
    B, S, _ = rq.shape
    C = min(RET_CHUNK, S)
    din, qd, kd, cd = _ret_consts(C)
    row = lambda w: pl.BlockSpec((None, C, w), lambda b, n: (b, n, 0))
    full = lambda a: pl.BlockSpec(a.shape, lambda b, n: (0,) * a.ndim)
    return pl.pallas_call(
        _ret_kernel,
        grid=(B, S // C),
        in_specs=[row(256), row(256), row(512), row(512), full(din), full(qd), full(kd), full(cd),
                  pl.BlockSpec((1, RET_W), lambda b, n: (0, 0))],
        out_specs=row(RET_W),
        out_shape=jax.ShapeDtypeStruct((B, S, RET_W), BF16),
        scratch_shapes=[pltpu.VMEM((RET_HEADS, RET_DK, RET_DV), F32)],
        compiler_params=_cparams(("parallel", "arbitrary")),
        name="ret",
    )(rq, rk, rv, rg, din, qd, kd, cd, ms_ret)


def _f32_key(x):
    i = lax.bitcast_convert_type(x, I32)
    return i ^ ((i >> 31) & 0x7FFFFFFF)


def _key_f32(k):
    return lax.bitcast_convert_type(k ^ ((k >> 31) & 0x7FFFFFFF), F32)


def _dsa_kernel(iqt_ref, iwt_ref, aqt_ref, ik_ref, ak_ref, avt_ref, ms_ref, o_ref, score_ref, qa_ref, sa_ref, sb_ref, mask_ref, *acc_refs,
                tq, tks, n_sel):
    H, G, R, d = DSA_HEADS, DSA_KV_HEADS, DSA_HEADS // DSA_KV_HEADS, DSA_HD
    t0 = pl.program_id(1) * tq
    nsub = (t0 + tq) // tks
    tka = 2 * tks
    npair = ((t0 + tq) // tka + 1) // 2
    kf = float(n_sel)
    qpos = t0 + lax.broadcasted_iota(I32, (1, tq), 1)
    krow = lax.broadcasted_iota(I32, (tks, tq), 0)

    wrow = [iwt_ref[h:h + 1, :] for h in range(IDX_HEADS)]

    def score_pair(i, carry):
        for u in range(2 * tka // tks):
            r0 = pl.multiple_of(i * 2 * tka + u * tks, tks)
            kc = ik_ref[pl.ds(r0, tks), :]
            acc = jnp.zeros((tks, tq), F32)
            for h in range(IDX_HEADS):
                rel = jnp.dot(kc, iqt_ref[h * IDX_HD:(h + 1) * IDX_HD, :], preferred_element_type=F32)
                acc = acc + jnp.maximum(rel, 0.0) * wrow[h]
            score_ref[pl.ds(r0, tks), :] = jnp.where(r0 + krow <= qpos, acc, -jnp.inf)
        return carry

    lax.fori_loop(0, npair, score_pair, 0)

    def fold8(x, op):
        acc = x[0:8, :]
        for i in range(1, tks // 8):
            acc = op(acc, x[8 * i:8 * (i + 1), :])
        return acc

    def count(th, strict):
        def body(j, acc):
            for u in range(2):
                s = score_ref[pl.ds(pl.multiple_of((2 * j + u) * tks, tks), tks), :]
                hit = (s > th) if strict else (s >= th)
                acc = acc + fold8(jnp.where(hit, 1.0, 0.0), jnp.add)
            return acc

        acc = lax.fori_loop(0, nsub // 2, body, jnp.zeros((8, tq), F32))
        return jnp.sum(acc, axis=0, keepdims=True)

    def minmax(j, carry):
        mx, mn = carry
        s = score_ref[pl.ds(pl.multiple_of(j * tks, tks), tks), :]
        mx = jnp.maximum(mx, fold8(s, jnp.maximum))
        mn = jnp.minimum(mn, fold8(jnp.where(s == -jnp.inf, jnp.inf, s), jnp.minimum))
        return mx, mn

    mx, mn = lax.fori_loop(0, nsub, minmax, (jnp.full((8, tq), -jnp.inf, F32), jnp.full((8, tq), jnp.inf, F32)))
    hi0 = jnp.max(mx, axis=0, keepdims=True)
    lo0 = jnp.min(mn, axis=0, keepdims=True)

    def probe(lo, hi, it):
        lk, hk = _f32_key(lo), _f32_key(hi)
        mk = (lk >> 1) + (hk >> 1) + (lk & hk & 1)
        mv = lo + (hi - lo) * 0.5
        early = (jnp.zeros((1, tq), I32) + it) < BISECT_VALUE_STEPS
        mid = jnp.where(early & (mv > lo) & (mv < hi), mv, _key_f32(mk))
        return mid, jnp.max(jnp.where(mk != lk, 1.0, 0.0))

    def bis_cond(c):
        return (c[4] > 0.0) & (c[5] < BISECT_MAX_STEPS)

    def bis_body(c):
        lo, hi, c_lo, c_hi, _, it = c
        mid, _ = probe(lo, hi, it)
        cnt = count(mid, False)
        ge = cnt >= kf
        up = ge | (cnt == kf)
        dn = (~ge) | (cnt == kf)
        lo, c_lo = jnp.where(up, mid, lo), jnp.where(up, cnt, c_lo)
        hi, c_hi = jnp.where(dn, mid, hi), jnp.where(dn, cnt, c_hi)
        _, active = probe(lo, hi, it + 1)
        return lo, hi, c_lo, c_hi, active, it + 1

    zero = jnp.zeros((1, tq), F32)
    n_pos = count(zero, True)
    n_nonneg = count(zero, False)
    keep_all = qpos + 1 <= n_sel
    settled = keep_all | ((n_nonneg >= kf) & (n_pos < kf))
    above = n_pos >= kf
    c_lo0 = jnp.where(settled | above, n_nonneg, (qpos + 1).astype(F32))
    c_hi0 = jnp.where(settled | ~above, n_nonneg, count(hi0, False))
    lo0 = jnp.where(settled | above, zero, lo0)
    hi0 = jnp.where(settled | ~above, zero, hi0)
    _, active0 = probe(lo0, hi0, jnp.int32(0))
    lo, hi, c_lo, c_hi, _, _ = lax.while_loop(bis_cond, bis_body, (lo0, hi0, c_lo0, c_hi0, active0, jnp.int32(0)))
    at_hi = c_hi >= kf
    thr = jnp.where(keep_all, F32_LOWEST, jnp.where(at_hi, hi, lo))
    excess = jnp.where(keep_all, 0.0, jnp.where(at_hi, c_hi, c_lo) - kf)

    @pl.when(jnp.max(excess) > 0.0)
    def _():
        budget = kf - count(thr, True)
        earlier = lax.broadcasted_iota(I32, (tks, tks), 1) < lax.broadcasted_iota(I32, (tks, tks), 0)
        earlier = jnp.where(earlier, 1.0, 0.0).astype(BF16)

        def fix(j, seen):
            r0 = pl.multiple_of(j * tks, tks)
            s = score_ref[pl.ds(r0, tks), :]
            eq = s == thr
            eqf = jnp.where(eq, 1.0, 0.0)
            rank = jnp.dot(earlier, eqf.astype(BF16), preferred_element_type=F32) + seen
            score_ref[pl.ds(r0, tks), :] = jnp.where(eq & (rank >= budget), -jnp.inf, s)
            return seen + jnp.sum(eqf, axis=0, keepdims=True)

        lax.fori_loop(0, nsub, fix, jnp.zeros((1, tq), F32))

    arow = lax.broadcasted_iota(I32, (KAUG - d, tq), 0)
    for h in range(H):
        slope = float(2.0 ** (-8.0 * (h + 1) / H))
        qa_ref[h, 0:d, :] = aqt_ref[h * d:(h + 1) * d, :]
        qa_ref[h, d:KAUG, :] = jnp.where(arow == 0, slope * ALIBI_SPLIT, jnp.where(arow == 1, slope, 0.0)).astype(BF16)
    for acc in acc_refs:
        acc[...] = jnp.zeros_like(acc)

    def logits(j, h):
        ka = ak_ref[pl.ds(pl.multiple_of(j * tka, tka), tka), (h // R) * KAUG:(h // R + 1) * KAUG]
        return jnp.dot(ka, qa_ref[h], preferred_element_type=F32)

    def step(j, j_next, cur_ref, next_ref, ms):
        r0 = pl.multiple_of(j * tka, tka)
        mask_ref[...] = jnp.where(score_ref[pl.ds(r0, tka), :] >= thr, 0.0, NEG_BIG)
        nms = []
        for h in range(H):
            g = h // R
            next_ref[h] = logits(j_next, h)
            s = cur_ref[h] + mask_ref[...]
            m_new = jnp.maximum(ms[h], jnp.max(s, axis=0, keepdims=True))
            p = jnp.exp(s - m_new).astype(BF16)
            va = avt_ref[g * VAUG:(g + 1) * VAUG, pl.ds(r0, tka)]
            acc = acc_refs[h]
            acc[...] = acc[...] * jnp.exp(ms[h] - m_new) + jnp.dot(va, p, preferred_element_type=F32)
            nms.append(m_new)
        return tuple(nms)

    for h in range(H):
        sa_ref[h] = logits(0, h)

    def att_pair(i, ms):
        ms = step(2 * i, 2 * i + 1, sa_ref, sb_ref, ms)
        return step(2 * i + 1, jnp.minimum(2 * i + 2, 2 * npair - 1), sb_ref, sa_ref, ms)

    lax.fori_loop(0, npair, att_pair, tuple(jnp.full((1, tq), NEG_BIG, F32) for _ in range(H)))
    for h in range(H):
        a = acc_refs[h][...]
        o = a[0:d, :] / a[d:d + 1, :]
        o_ref[h * d:(h + 1) * d, :] = (o * ms_ref[h * d:(h + 1) * d, :]).astype(BF16)


def _dsa(iqt, iwt, aqt, ik, ak, avt, ms_att, tq, tks):
    B, _, S = iqt.shape
    assert S <= ALIBI_SPLIT * 256 and tq % (2 * tks) == 0
    assert (S // (2 * tks)) % 2 == 0
    n_sel = min(TOPK_MAX, S // 4)
    G = DSA_KV_HEADS
    colT = lambda h: pl.BlockSpec((None, h, tq), lambda b, i: (b, 0, i))
    msb = jnp.broadcast_to(ms_att.reshape(DSA_W, 1), (DSA_W, tq))
    return pl.pallas_call(
        functools.partial(_dsa_kernel, tq=tq, tks=tks, n_sel=n_sel),
        grid=(B, S // tq),
        in_specs=[colT(IDX_HEADS * IDX_HD), colT(IDX_HEADS), colT(DSA_W),
                  pl.BlockSpec((None, S, IDX_HD), lambda b, i: (b, 0, 0)),
                  pl.BlockSpec((None, S, G * KAUG), lambda b, i: (b, 0, 0)),
                  pl.BlockSpec((None, G * VAUG, S), lambda b, i: (b, 0, 0)),
                  pl.BlockSpec((DSA_W, tq), lambda b, i: (0, 0))],
        out_specs=colT(DSA_W),
        out_shape=jax.ShapeDtypeStruct((B, DSA_W, S), BF16),
        scratch_shapes=[pltpu.VMEM((S, tq), F32), pltpu.VMEM((DSA_HEADS, KAUG, tq), BF16),
                        pltpu.VMEM((DSA_HEADS, 2 * tks, tq), F32), pltpu.VMEM((DSA_HEADS, 2 * tks, tq), F32),
                        pltpu.VMEM((2 * tks, tq), F32)]
        + [pltpu.VMEM((VAUG, tq), F32) for _ in range(DSA_HEADS)],
        compiler_params=_cparams(("parallel", "arbitrary")),
        name="dsa",
    )(iqt, iwt, aqt, ik, ak, avt, msb)


def _oproj_kernel(ret_ref, att_ref, x_ref, wo_ref, g1_ref, n2_ref, sc_ref, sh_ref, rw_ref, rb_ref,
                  x1_ref, h2_ref, sel_ref, idx_ref, gate_ref, cnt_ref):
    mixo = jnp.dot(ret_ref[...], wo_ref[:RET_W, :], preferred_element_type=F32)
    mixo = mixo + lax.dot_general(att_ref[...], wo_ref[RET_W:, :], (((0,), (0,)), ((), ())),
                                  preferred_element_type=F32)
    x1 = x_ref[...] + g1_ref[...] * mixo
    x1_ref[...] = x1
    y = x1 * lax.rsqrt(jnp.mean(x1 * x1, axis=-1, keepdims=True) + EPS) * n2_ref[...]
    h2 = y * (1.0 + sc_ref[...]) + sh_ref[...]
    _store_token_major(h2_ref, h2)
    logits = jnp.dot(h2, rw_ref[...], preferred_element_type=F32,
                     precision=lax.Precision.HIGHEST) + rb_ref[...]
    tm = logits.shape[0]
    lane = lax.broadcasted_iota(I32, (tm, LANES), 1).astype(F32)
    work = jnp.where(lane < N_EXPERTS, logits, -jnp.inf)
    sel = jnp.zeros((tm, LANES), F32)
    idx_tab = jnp.zeros((tm, LANES), F32)
    vals = []
    for k in range(TOP_K):
        m = jnp.max(work, axis=1, keepdims=True)
        idx = jnp.min(jnp.where(work == m, lane, float(LANES)), axis=1, keepdims=True)
        hit = lane == idx
        sel = jnp.where(hit, 1.0, sel)
        idx_tab = jnp.where(lane == k, idx, idx_tab)
        work = jnp.where(hit, -jnp.inf, work)
        vals.append(m)
    es = [jnp.exp(v - vals[0]) for v in vals]
    den = es[0] + es[1] + es[2] + es[3]
    gate_tab = jnp.zeros((tm, LANES), F32)
    for k in range(TOP_K):
        gate_tab = jnp.where(lane == k, es[k] / den, gate_tab)
    sel_ref[...] = sel
    idx_ref[...] = idx_tab
    gate_ref[...] = gate_tab

    @pl.when((pl.program_id(0) == 0) & (pl.program_id(1) == 0))
    def _():
        cnt_ref[...] = jnp.zeros_like(cnt_ref)

    cnt_ref[...] += jnp.sum(sel, axis=0, keepdims=True)


def _oproj(ret, att, x, wo, g1, n2g, sc2, sh2, rw_pad, rb_pad, tm):
    B, S, D = x.shape
    nt = S // tm
    row = lambda w: pl.BlockSpec((None, tm, w), lambda b, i: (b, i, 0))
    flat = lambda w: pl.BlockSpec((tm, w), lambda b, i: (b * nt + i, 0))
    vec = pl.BlockSpec((None, 1, D), lambda b, i: (b, 0, 0))
    cst = lambda shape: pl.BlockSpec(shape, lambda b, i: (0, 0))
    sd = lambda shape, dt: jax.ShapeDtypeStruct(shape, dt)
    N = B * S
    return pl.pallas_call(
        _oproj_kernel,
        grid=(B, nt),
        in_specs=[row(RET_W), pl.BlockSpec((None, DSA_W, tm), lambda b, i: (b, 0, i)), row(D), cst((D, D)), vec,
                  cst((1, D)), vec, vec,
                  cst((D, LANES)), cst((1, LANES))],
        out_specs=[flat(D), pl.BlockSpec((tm * (D // LANES), LANES), lambda b, i: (b * nt + i, 0)),
                   flat(LANES), flat(LANES), flat(LANES), cst((1, LANES))],
        out_shape=[sd((N, D), F32), sd((N * (D // LANES), LANES), F32), sd((N, LANES), F32), sd((N, LANES), F32),
                   sd((N, LANES), F32), sd((1, LANES), F32)],
        compiler_params=_cparams(("arbitrary", "arbitrary")),
        name="oproj",
    )(ret, att, x, wo, g1, n2g.reshape(1, D), sc2, sh2, rw_pad, rb_pad)


def _dest_kernel(sel_ref, idx_ref, pstart_ref, dest_ref, seen_ref):
    @pl.when(pl.program_id(0) == 0)
    def _():
        seen_ref[...] = jnp.zeros_like(seen_ref)

    sel = sel_ref[...]
    tm = sel.shape[0]
    earlier = lax.broadcasted_iota(I32, (tm, tm), 1) < lax.broadcasted_iota(I32, (tm, tm), 0)
    earlier = jnp.where(earlier, 1.0, 0.0).astype(BF16)
    rank = jnp.dot(earlier, sel.astype(BF16), preferred_element_type=F32) + seen_ref[...]
    dest = pstart_ref[...] + rank
    lane = lax.broadcasted_iota(I32, (tm, LANES), 1).astype(F32)
    idx_tab = idx_ref[...]
    out = jnp.zeros((tm, LANES), F32)
    for k in range(TOP_K):
        e_k = jnp.sum(jnp.where(lane == k, idx_tab, 0.0), axis=1, keepdims=True)
        d_k = jnp.sum(jnp.where(lane == e_k, dest, 0.0), axis=1, keepdims=True)
        out = jnp.where(lane == k, d_k, out)
    dest_ref[...] = out.astype(I32)
    seen_ref[...] += jnp.sum(sel, axis=0, keepdims=True)


def _dest(sel, idx_tab, pstart, tm):
    N = sel.shape[0]
    blk = pl.BlockSpec((tm, LANES), lambda i: (i, 0))
    return pl.pallas_call(
        _dest_kernel,
        grid=(N // tm,),
        in_specs=[blk, blk, pl.BlockSpec((1, LANES), lambda i: (0, 0))],
        out_specs=blk,
        out_shape=jax.ShapeDtypeStruct((N, LANES), I32),
        scratch_shapes=[pltpu.VMEM((1, LANES), F32)],
        compiler_params=_cparams(("arbitrary",)),
        name="dest",
    )(sel, idx_tab, pstart)


def _disp_kernel(zs_ref, h2_ref, dest_hbm, xs_hbm, zbuf, idx_a, idx_b, sem_ia, sem_ib, sem_row, sem_z,
                 *, tm, nch, zrows):
    i = pl.program_id(0)
    n = tm * TOP_K

    def idx_copy(tile, buf, sem):
        return pltpu.make_async_copy(dest_hbm.at[pl.ds(tile * n, n)], buf, sem)

    def rows(ref, first, count):
        return ref.at[pl.ds(pl.multiple_of(first * nch, nch), count * nch)]

    def row_copy(row, dst):
        return pltpu.make_async_copy(rows(h2_ref, row, 1), rows(xs_hbm, dst, 1), sem_row)

    def issue_rows(first_row, idx):
        def body(t, c):
            for k in range(TOP_K):
                row_copy(first_row + t, idx[t * TOP_K + k]).start(priority=k % 2)
            return c

        lax.fori_loop(0, tm, body, 0)

    def wait_rows():
        pltpu.make_async_copy(rows(xs_hbm, 0, n), rows(xs_hbm, 0, n), sem_row).wait()

    @pl.when(i == 0)
    def _():
        zbuf[...] = jnp.zeros_like(zbuf)
        for e in range(N_EXPERTS):
            fill = pltpu.make_async_copy(zbuf, rows(xs_hbm, zs_ref[e], zrows), sem_z)
            fill.start()
            fill.wait()

    idx_copy(2 * i, idx_a, sem_ia).start()
    idx_copy(2 * i + 1, idx_b, sem_ib).start()
    idx_copy(2 * i, idx_a, sem_ia).wait()
    issue_rows(0, idx_a)
    idx_copy(2 * i + 1, idx_b, sem_ib).wait()
    issue_rows(tm, idx_b)
    wait_rows()
    wait_rows()


def _dispatch(h2, dest_flat, zero_start, n_rows, zrows, tm):
    nch = D_MODEL // LANES
    n_steps = h2.shape[0] // nch // (2 * tm)
    grid_spec = pltpu.PrefetchScalarGridSpec(
        num_scalar_prefetch=1,
        grid=(n_steps,),
        in_specs=[pl.BlockSpec((2 * tm * nch, LANES), lambda i, zs: (i, 0)), pl.BlockSpec(memory_space=pl.ANY)],
        out_specs=pl.BlockSpec(memory_space=pl.ANY),
        scratch_shapes=[pltpu.VMEM((zrows * nch, LANES), F32), pltpu.SMEM((tm * TOP_K,), I32),
                        pltpu.SMEM((tm * TOP_K,), I32), pltpu.SemaphoreType.DMA, pltpu.SemaphoreType.DMA,
                        pltpu.SemaphoreType.DMA, pltpu.SemaphoreType.DMA],
    )
    return pl.pallas_call(
        functools.partial(_disp_kernel, tm=tm, nch=nch, zrows=zrows),
        grid_spec=grid_spec,
        out_shape=jax.ShapeDtypeStruct((n_rows * nch, LANES), F32),
        compiler_params=_cparams(("arbitrary",)),
        name="disp",
    )(zero_start, h2, dest_flat)


def _store_token_major(ref, x):
    rows, d = x.shape
    nch = d // LANES
    for j in range(nch):
        ref[pl.ds(j, rows, stride=nch), :] = x[:, j * LANES:(j + 1) * LANES]


def _load_token_major(ref, rows, nch):
    return jnp.concatenate([ref[pl.ds(j, rows, stride=nch), :] for j in range(nch)], axis=1)


def _ffn_kernel(be_ref, nb_ref, xs_ref, wgu_ref, bgu_ref, wd_ref, bd_ref, ys_ref, wgu_bf, wd_bf):
    i = pl.program_id(0)
    live = i < nb_ref[0]

    @pl.when(live & ((i == 0) | (be_ref[i] != be_ref[jnp.maximum(i - 1, 0)])))
    def _():
        wgu_bf[...] = wgu_ref[...].astype(BF16)
        wd_bf[...] = wd_ref[...].astype(BF16)

    @pl.when(live)
    def _():
        xb = _load_token_major(xs_ref, ys_ref.shape[0] // NCH, NCH).astype(BF16)
        gu = jnp.dot(xb, wgu_bf[...], preferred_element_type=F32) + bgu_ref[...]
        gate = jnp.minimum(gu[:, :D_EXPERT], SWIGLU_LIMIT)
        up = jnp.clip(gu[:, D_EXPERT:], -SWIGLU_LIMIT, SWIGLU_LIMIT)
        glu = gate * (1.0 / (1.0 + jnp.exp(-SWIGLU_ALPHA * gate)))
        act = ((up + 1.0) * glu).astype(BF16)
        ys = jnp.dot(act, wd_bf[...], preferred_element_type=F32) + bd_ref[...]
        _store_token_major(ys_ref, ys)

    @pl.when(jnp.logical_not(live))
    def _():
        ys_ref[...] = jnp.zeros_like(ys_ref)


def _ffn(xs, block_e, n_blocks, w_gu, b_gu, w_down, b_down, tmb):
    D = D_MODEL
    P = xs.shape[0] // NCH
    E = w_gu.shape[0]
    blk = lambda i, be, nb: (jnp.minimum(i, nb[0] - 1), 0)
    wsel = lambda i, be, nb: (be[jnp.minimum(i, nb[0] - 1)], 0, 0)
    grid_spec = pltpu.PrefetchScalarGridSpec(
        num_scalar_prefetch=2,
        grid=(P // tmb,),
        in_specs=[pl.BlockSpec((tmb * NCH, LANES), blk),
                  pl.BlockSpec((None, D, 2 * D_EXPERT), wsel),
                  pl.BlockSpec((None, 1, 2 * D_EXPERT), wsel),
                  pl.BlockSpec((None, D_EXPERT, D), wsel),
                  pl.BlockSpec((None, 1, D), wsel)],
        out_specs=pl.BlockSpec((tmb * (D // LANES), LANES), lambda i, be, nb: (i, 0)),
        scratch_shapes=[pltpu.VMEM((D, 2 * D_EXPERT), BF16), pltpu.VMEM((D_EXPERT, D), BF16)],
    )
    return pl.pallas_call(
        _ffn_kernel,
        grid_spec=grid_spec,
        out_shape=jax.ShapeDtypeStruct((P * (D // LANES), LANES), F32),
        compiler_params=_cparams(("arbitrary",)),
        name="ffn",
    )(block_e, n_blocks, xs, w_gu, b_gu.reshape(E, 1, 2 * D_EXPERT), w_down, b_down.reshape(E, 1, D))


def _comb_kernel(x1_ref, gate_ref, g2_ref, fg_ref, dest_hbm, ys_hbm, o_ref, buf_a, buf_b, idx_a, idx_b,
                 sem_ia, sem_ib, sem_ra, sem_rb, *, tm, n_steps):
    i = pl.program_id(0)
    n = tm * TOP_K

    def idx_copy(tile, buf, sem):
        return pltpu.make_async_copy(dest_hbm.at[pl.ds(tile * n, n)], buf, sem)

    nch = x1_ref.shape[1] // LANES

    def row_copy(src, buf, k, t, sem):
        return pltpu.make_async_copy(ys_hbm.at[pl.ds(pl.multiple_of(src * nch, nch), nch)],
                                     buf.at[k, pl.ds(pl.multiple_of(t * nch, nch), nch)], sem)

    def issue_rows(idx, buf, sem):
        def body(t, c):
            for k in range(TOP_K):
                row_copy(idx[t * TOP_K + k], buf, k, t, sem).start(priority=k % 2)
            return c

        lax.fori_loop(0, tm, body, 0)

    def wait_rows(buf, sem):
        pltpu.make_async_copy(buf, buf, sem).wait()

    def finish(buf, lo):
        gates = gate_ref[lo:lo + tm, :]
        y = jnp.zeros((tm, x1_ref.shape[1]), F32)
        for k in range(TOP_K):
            y = y + gates[:, k:k + 1] * _load_token_major(buf.at[k], tm, nch)
        v = x1_ref[lo:lo + tm, :] + g2_ref[...] * y
        o_ref[lo:lo + tm, :] = v * lax.rsqrt(jnp.mean(v * v, axis=-1, keepdims=True) + EPS) * fg_ref[...]

    @pl.when(i == 0)
    def _():
        first = idx_copy(0, idx_a, sem_ia)
        first.start()
        first.wait()
        issue_rows(idx_a, buf_a, sem_ra)
        idx_copy(1, idx_b, sem_ib).start()

    idx_copy(2 * i + 1, idx_b, sem_ib).wait()
    issue_rows(idx_b, buf_b, sem_rb)

    @pl.when(i + 1 < n_steps)
    def _():
        idx_copy(2 * i + 2, idx_a, sem_ia).start()

    wait_rows(buf_a, sem_ra)
    finish(buf_a, 0)

    @pl.when(i + 1 < n_steps)
    def _():
        idx_copy(2 * i + 2, idx_a, sem_ia).wait()
        issue_rows(idx_a, buf_a, sem_ra)
        idx_copy(2 * i + 3, idx_b, sem_ib).start()

    wait_rows(buf_b, sem_rb)
    finish(buf_b, tm)


def _combine(x1, gate_tab, g2, final_g, dest_flat, ys, S, tm):
    N, D = x1.shape
    n_steps = N // (2 * tm)
    per_b = S // (2 * tm)
    return pl.pallas_call(
        functools.partial(_comb_kernel, tm=tm, n_steps=n_steps),
        grid=(n_steps,),
        in_specs=[pl.BlockSpec((2 * tm, D), lambda i: (i, 0)),
                  pl.BlockSpec((2 * tm, LANES), lambda i: (i, 0)),
                  pl.BlockSpec((None, 1, D), lambda i: (i // per_b, 0, 0)),
                  pl.BlockSpec((1, D), lambda i: (0, 0)),
                  pl.BlockSpec(memory_space=pl.ANY),
                  pl.BlockSpec(memory_space=pl.ANY)],
        out_specs=pl.BlockSpec((2 * tm, D), lambda i: (i, 0)),
        out_shape=jax.ShapeDtypeStruct((N, D), F32),
        scratch_shapes=[pltpu.VMEM((TOP_K, tm * (D // LANES), LANES), F32),
                        pltpu.VMEM((TOP_K, tm * (D // LANES), LANES), F32),
                        pltpu.SMEM((tm * TOP_K,), I32), pltpu.SMEM((tm * TOP_K,), I32),
                        pltpu.SemaphoreType.DMA, pltpu.SemaphoreType.DMA, pltpu.SemaphoreType.DMA,
                        pltpu.SemaphoreType.DMA],
        compiler_params=_cparams(("arbitrary",)),
        name="comb",
    )(x1, gate_tab, g2, final_g.reshape(1, D), dest_flat, ys)


def _tile(n, pref):
    t = min(pref, n)
    assert n % t == 0, (n, t)
    return t


def _layer(x, c, ada_w, ada_b, norm1_g, w_in, mix_scale, w_o, norm2_g,
           router_w, router_b, w_gu, b_gu, w_down, b_down, final_g):
    B, S, D = x.shape
    N = B * S
    mod = _mod(c, ada_w, ada_b).reshape(B, 6, 1, D)
    sh1, sc1, g1, sh2, sc2, g2 = (mod[:, j] for j in range(6))

    w_pad = jnp.pad(w_in, ((0, 0), (0, IN_COLS_PAD - IN_COLS))).astype(BF16)
    rq, rk, rv, rg, aqt, ak, avt, iqt, ik, iwt = _inproj(x, norm1_g, sc1, sh1, w_pad, _tile(S, 512))
    ms = mix_scale.reshape(1, RET_W + DSA_W)
    ret = _retention(rq, rk, rv, rg, ms[:, :RET_W])
    att = _dsa(iqt, iwt, aqt, ik, ak, avt, ms[:, RET_W:], _tile(S, 256), 128)

    rw_pad = jnp.pad(router_w, ((0, 0), (0, LANES - N_EXPERTS)))
    rb_pad = jnp.pad(router_b, (0, LANES - N_EXPERTS)).reshape(1, LANES)
    x1, h2, sel, idx_tab, gate_tab, counts = _oproj(ret, att, x, w_o.astype(BF16), g1, norm2_g, sc2, sh2,
                                                    rw_pad, rb_pad, _tile(S, 512))

    tmb = FFN_ROWS
    n_rows = (N * TOP_K + N_EXPERTS * (tmb - 1)) // tmb * tmb + tmb
    cnt = counts[0, :N_EXPERTS].astype(I32)
    padded = (cnt + tmb - 1) // tmb * tmb
    ends = jnp.cumsum(padded)
    starts = ends - padded
    pstart = jnp.pad(starts.astype(F32), (0, LANES - N_EXPERTS)).reshape(1, LANES)
    n_blocks = (ends[-1] // tmb).reshape(1)
    first_row = jnp.arange(n_rows // tmb, dtype=I32) * tmb
    block_e = jnp.minimum(jnp.sum((ends[None, :] <= first_row[:, None]).astype(I32), axis=1), N_EXPERTS - 1)

    tmd = _tile(N, 256)
    dest_tab = _dest(sel, idx_tab, pstart, tmd)
    dest_flat = dest_tab[:, :TOP_K].reshape(N * TOP_K)
    xs = _dispatch(h2, dest_flat, starts + cnt, n_rows, tmb, tmd)
    ys = _ffn(xs, block_e, n_blocks, w_gu, b_gu, w_down, b_down, tmb)
    out = _combine(x1, gate_tab, g2, final_g, dest_flat, ys, S, _tile(S, 256))
    return out.reshape(B, S, D)


def kernel(x, c, ada_w, ada_b, norm1_g, w_in, mix_scale, w_o, norm2_g, router_w, router_b, w_gu, b_gu,
           w_down, b_down, final_g):
    assert ada_w.shape[0] == 1, "single-layer stack"
    return _layer(x, c, ada_w[0], ada_b[0], norm1_g[0], w_in[0], mix_scale[0], w_o[0], norm2_g[0],
                  router_w[0], router_b[0], w_gu[0], b_gu[0], w_down[0], b_down[0], final_g)
```

```python
import functools

import numpy as np
import jax
import jax.numpy as jnp
from jax import lax
from jax.experimental import pallas as pl
from jax.experimental.pallas import tpu as pltpu

F32 = jnp.float32
BF16 = jnp.bfloat16
I32 = jnp.int32

D_MODEL = 1024
RET_HEADS = 4
RET_DK = 64
RET_DV = 128
RET_CHUNK = 128
DSA_HEADS = 8
DSA_KV_HEADS = 2
DSA_HD = 64
IDX_HEADS = 8
IDX_HD = 64
TOPK_MAX = 256
N_EXPERTS = 32
TOP_K = 4
D_EXPERT = D_MODEL
SWIGLU_LIMIT = 7.0
SWIGLU_ALPHA = 1.702
EPS = 1e-6

RET_W = RET_HEADS * RET_DV
DSA_W = DSA_HEADS * DSA_HD
IN_COLS = 2888
IN_COLS_PAD = 2944

KAUG = 128
VAUG = 80
ALIBI_SPLIT = 64
FFN_ROWS = 512
BISECT_VALUE_STEPS = 8
BISECT_MAX_STEPS = 64

LANES = 128
NCH = D_MODEL // LANES
VMEM_LIMIT = 56 * 1024 * 1024
NEG_BIG = -1e30
F32_LOWEST = float(np.finfo(np.float32).min)


def _cparams(sem):
    return pltpu.CompilerParams(dimension_semantics=sem, vmem_limit_bytes=VMEM_LIMIT)


def _mod_kernel(c_ref, w_ref, b_ref, o_ref):
    c = c_ref[...]
    s = c * (1.0 / (1.0 + jnp.exp(-c)))
    o_ref[...] = jnp.dot(s, w_ref[...], preferred_element_type=F32,
                         precision=lax.Precision.HIGHEST) + b_ref[...]


def _mod(c, ada_w, ada_b):
    B, D = c.shape
    n_out = ada_w.shape[1]
    rows = 8
    c8 = jnp.zeros((rows, D), F32).at[:B].set(c)
    out = pl.pallas_call(
        _mod_kernel,
        grid=(n_out // D,),
        in_specs=[pl.BlockSpec((rows, D), lambda j: (0, 0)),
                  pl.BlockSpec((D, D), lambda j: (0, j)),
                  pl.BlockSpec((1, D), lambda j: (0, j))],
        out_specs=pl.BlockSpec((rows, D), lambda j: (0, j)),
        out_shape=jax.ShapeDtypeStruct((rows, n_out), F32),
        compiler_params=_cparams(("arbitrary",)),
        name="mod",
    )(c8, ada_w, ada_b.reshape(1, n_out))
    return out[:B]


def _inproj_kernel(x_ref, g_ref, sc_ref, sh_ref, w_ref,
                   rq_ref, rk_ref, rv_ref, rg_ref, aqt_ref, ak_ref, avt_ref, iqt_ref, ik_ref, iwt_ref):
    x = x_ref[...]
    ms = jnp.mean(x * x, axis=-1, keepdims=True)
    y = x * lax.rsqrt(ms + EPS) * g_ref[...]
    hb = (y * (1.0 + sc_ref[...]) + sh_ref[...]).astype(BF16)

    def proj(lo, hi):
        return jnp.dot(hb, w_ref[:, lo:hi], preferred_element_type=F32)

    tm = x.shape[0]
    d = DSA_HD
    rq_ref[...] = proj(0, 256).astype(BF16)
    rk_ref[...] = (proj(256, 512) * (RET_DK ** -0.5)).astype(BF16)
    rv_ref[...] = proj(512, 1024).astype(BF16)
    rg_ref[...] = proj(1024, 1536).astype(BF16)
    aqt_ref[...] = (proj(1536, 2048) * (d ** -0.5)).T.astype(BF16)
    kk = proj(2048, 2176)
    pos = pl.program_id(1) * tm + lax.broadcasted_iota(I32, (tm, d), 0)
    col = lax.broadcasted_iota(I32, (tm, d), 1)
    posblk = jnp.where(col == 0, pos // ALIBI_SPLIT, jnp.where(col == 1, pos % ALIBI_SPLIT, 0)).astype(F32)
    for g in range(DSA_KV_HEADS):
        ak_ref[:, g * KAUG:g * KAUG + d] = kk[:, g * d:(g + 1) * d].astype(BF16)
        ak_ref[:, g * KAUG + d:(g + 1) * KAUG] = posblk.astype(BF16)
    vt = proj(2176, 2304).T
    r16 = lax.broadcasted_iota(I32, (VAUG - d, tm), 0)
    onesblk = jnp.where(r16 == 0, 1.0, 0.0).astype(BF16)
    for g in range(DSA_KV_HEADS):
        avt_ref[g * VAUG:g * VAUG + d, :] = vt[g * d:(g + 1) * d, :].astype(BF16)
        avt_ref[g * VAUG + d:(g + 1) * VAUG, :] = onesblk
    iqt_ref[...] = proj(2304, 2816).T.astype(BF16)
    last = proj(2816, 2944)
    ik_ref[...] = last[:, :IDX_HD].astype(BF16)
    iwt_ref[...] = last.T[IDX_HD:IDX_HD + IDX_HEADS, :] * ((IDX_HD ** -0.5) * (IDX_HEADS ** -0.5))


def _inproj(x, norm_g, sc, sh, w_pad, tm):
    B, S, D = x.shape
    row = lambda w: pl.BlockSpec((None, tm, w), lambda b, i: (b, i, 0))
    colT = lambda h: pl.BlockSpec((None, h, tm), lambda b, i: (b, 0, i))
    vec = pl.BlockSpec((None, 1, D), lambda b, i: (b, 0, 0))
    sd = lambda shape, dt: jax.ShapeDtypeStruct(shape, dt)
    G = DSA_KV_HEADS
    return pl.pallas_call(
        _inproj_kernel,
        grid=(B, S // tm),
        in_specs=[row(D), pl.BlockSpec((1, D), lambda b, i: (0, 0)), vec, vec,
                  pl.BlockSpec((D, IN_COLS_PAD), lambda b, i: (0, 0))],
        out_specs=[row(256), row(256), row(512), row(512), colT(DSA_W), row(G * KAUG), colT(G * VAUG),
                   colT(IDX_HEADS * IDX_HD), row(IDX_HD), colT(IDX_HEADS)],
        out_shape=[sd((B, S, 256), BF16), sd((B, S, 256), BF16), sd((B, S, 512), BF16),
                   sd((B, S, 512), BF16), sd((B, DSA_W, S), BF16), sd((B, S, G * KAUG), BF16),
                   sd((B, G * VAUG, S), BF16), sd((B, IDX_HEADS * IDX_HD, S), BF16),
                   sd((B, S, IDX_HD), BF16), sd((B, IDX_HEADS, S), F32)],
        compiler_params=_cparams(("parallel", "parallel")),
        name="inproj",
    )(x, norm_g.reshape(1, D), sc, sh, w_pad)


def _ret_kernel(rq_ref, rk_ref, rv_ref, rg_ref, din_ref, qd_ref, kd_ref, cd_ref, ms_ref, o_ref, state_ref):
    @pl.when(pl.program_id(1) == 0)
    def _():
        state_ref[...] = jnp.zeros_like(state_ref)

    for h in range(RET_HEADS):
        q = rq_ref[:, h * RET_DK:(h + 1) * RET_DK]
        k = rk_ref[:, h * RET_DK:(h + 1) * RET_DK]
        v = rv_ref[:, h * RET_DV:(h + 1) * RET_DV]
        r_prev = state_ref[h]
        s = lax.dot_general(q, k, (((1,), (1,)), ((), ())), preferred_element_type=F32) * din_ref[h]
        o = jnp.dot(s.astype(BF16), v, preferred_element_type=F32)
        o = o + jnp.dot(q, r_prev.astype(BF16), preferred_element_type=F32) * qd_ref[h]
        vd = (v.astype(F32) * kd_ref[h]).astype(BF16)
        kv = lax.dot_general(k, vd, (((0,), (0,)), ((), ())), preferred_element_type=F32)
        state_ref[h] = r_prev * cd_ref[h] + kv
        o = o * lax.rsqrt(jnp.mean(o * o, axis=-1, keepdims=True) + EPS)
        g = rg_ref[:, h * RET_DV:(h + 1) * RET_DV].astype(F32)
        gate = g * (1.0 / (1.0 + jnp.exp(-g)))
        o_ref[:, h * RET_DV:(h + 1) * RET_DV] = (gate * o * ms_ref[:, h * RET_DV:(h + 1) * RET_DV]).astype(BF16)


def _ret_consts(C):
    H = RET_HEADS
    log_g = np.log1p(-np.exp2(-5.0 - np.arange(H, dtype=np.float64)))
    pos = np.arange(C, dtype=np.float64)
    diff = pos[:, None] - pos[None, :]
    d_inner = np.where(diff[None] >= 0, np.exp(np.maximum(diff, 0.0)[None] * log_g[:, None, None]), 0.0)
    q_decay = np.exp((pos + 1.0)[None] * log_g[:, None])
    k_decay = np.exp((C - 1.0 - pos)[None] * log_g[:, None])
    chunk_decay = np.exp(C * log_g)
    qd = np.broadcast_to(q_decay[:, :, None], (H, C, RET_DV))
    kd = np.broadcast_to(k_decay[:, :, None], (H, C, RET_DV))
    cd = np.broadcast_to(chunk_decay[:, None, None], (H, 1, RET_DV))
    f = lambda a: jnp.asarray(np.ascontiguousarray(a), F32)
    return f(d_inner), f(qd), f(kd), f(cd)


def _retention(rq, rk, rv, rg, ms_ret):
    B, S, _ = rq.shape
    C = min(RET_CHUNK, S)
    din, qd, kd, cd = _ret_consts(C)
    row = lambda w: pl.BlockSpec((None, C, w), lambda b, n: (b, n, 0))
    full = lambda a: pl.BlockSpec(a.shape, lambda b, n: (0,) * a.ndim)
    return pl.pallas_call(
        _ret_kernel,
        grid=(B, S // C),
        in_specs=[row(256), row(256), row(512), row(512), full(din), full(qd), full(kd), full(cd),
                  pl.BlockSpec((1, RET_W), lambda b, n: (0, 0))],
        out_specs=row(RET_W),
        out_shape=jax.ShapeDtypeStruct((B, S, RET_W), BF16),
        scratch_shapes=[pltpu.VMEM((RET_HEADS, RET_DK, RET_DV), F32)],
        compiler_params=_cparams(("parallel", "arbitrary")),
        name="ret",
    )(rq, rk, rv, rg, din, qd, kd, cd, ms_ret)


def _f32_key(x):
    i = lax.bitcast_convert_type(x, I32)
    return i ^ ((i >> 31) & 0x7FFFFFFF)


def _key_f32(k):
    return lax.bitcast_convert_type(k ^ ((k >> 31) & 0x7FFFFFFF), F32)


def _dsa_kernel(iqt_ref, iwt_ref, aqt_ref, ik_ref, ak_ref, avt_ref, ms_ref, o_ref, score_ref, qa_ref, sa_ref, sb_ref, mask_ref, *acc_refs,
                tq, tks, n_sel):
    H, G, R, d = DSA_HEADS, DSA_KV_HEADS, DSA_HEADS // DSA_KV_HEADS, DSA_HD
    t0 = pl.program_id(1) * tq
    nsub = (t0 + tq) // tks
    tka = 2 * tks
    npair = ((t0 + tq) // tka + 1) // 2
    kf = float(n_sel)
    qpos = t0 + lax.broadcasted_iota(I32, (1, tq), 1)
    krow = lax.broadcasted_iota(I32, (tks, tq), 0)

    wrow = [iwt_ref[h:h + 1, :] for h in range(IDX_HEADS)]

    def score_pair(i, carry):
        for u in range(2 * tka // tks):
            r0 = pl.multiple_of(i * 2 * tka + u * tks, tks)
            kc = ik_ref[pl.ds(r0, tks), :]
            acc = jnp.zeros((tks, tq), F32)
            for h in range(IDX_HEADS):
                rel = jnp.dot(kc, iqt_ref[h * IDX_HD:(h + 1) * IDX_HD, :], preferred_element_type=F32)
                acc = acc + jnp.maximum(rel, 0.0) * wrow[h]
            score_ref[pl.ds(r0, tks), :] = jnp.where(r0 + krow <= qpos, acc, -jnp.inf)
        return carry

    lax.fori_loop(0, npair, score_pair, 0)

    def fold8(x, op):
        acc = x[0:8, :]
        for i in range(1, tks // 8):
            acc = op(acc, x[8 * i:8 * (i + 1), :])
        return acc

    def count(th, strict):
        def body(j, acc):
            for u in range(2):
                s = score_ref[pl.ds(pl.multiple_of((2 * j + u) * tks, tks), tks), :]
                hit = (s > th) if strict else (s >= th)
                acc = acc + fold8(jnp.where(hit, 1.0, 0.0), jnp.add)
            return acc

        acc = lax.fori_loop(0, nsub // 2, body, jnp.zeros((8, tq), F32))
        return jnp.sum(acc, axis=0, keepdims=True)

    def minmax(j, carry):
        mx, mn = carry
        s = score_ref[pl.ds(pl.multiple_of(j * tks, tks), tks), :]
        mx = jnp.maximum(mx, fold8(s, jnp.maximum))
        mn = jnp.minimum(mn, fold8(jnp.where(s == -jnp.inf, jnp.inf, s), jnp.minimum))
        return mx, mn

    mx, mn = lax.fori_loop(0, nsub, minmax, (jnp.full((8, tq), -jnp.inf, F32), jnp.full((8, tq), jnp.inf, F32)))
    hi0 = jnp.max(mx, axis=0, keepdims=True)
    lo0 = jnp.min(mn, axis=0, keepdims=True)

    def probe(lo, hi, it):
        lk, hk = _f32_key(lo), _f32_key(hi)
        mk = (lk >> 1) + (hk >> 1) + (lk & hk & 1)
        mv = lo + (hi - lo) * 0.5
        early = (jnp.zeros((1, tq), I32) + it) < BISECT_VALUE_STEPS
        mid = jnp.where(early & (mv > lo) & (mv < hi), mv, _key_f32(mk))
        return mid, jnp.max(jnp.where(mk != lk, 1.0, 0.0))

    def bis_cond(c):
        return (c[4] > 0.0) & (c[5] < BISECT_MAX_STEPS)

    def bis_body(c):
        lo, hi, c_lo, c_hi, _, it = c
        mid, _ = probe(lo, hi, it)
        cnt = count(mid, False)
        ge = cnt >= kf
        up = ge | (cnt == kf)
        dn = (~ge) | (cnt == kf)
        lo, c_lo = jnp.where(up, mid, lo), jnp.where(up, cnt, c_lo)
        hi, c_hi = jnp.where(dn, mid, hi), jnp.where(dn, cnt, c_hi)
        _, active = probe(lo, hi, it + 1)
        return lo, hi, c_lo, c_hi, active, it + 1

    zero = jnp.zeros((1, tq), F32)
    n_pos = count(zero, True)
    n_nonneg = count(zero, False)
    keep_all = qpos + 1 <= n_sel
    settled = keep_all | ((n_nonneg >= kf) & (n_pos < kf))
    above = n_pos >= kf
    c_lo0 = jnp.where(settled | above, n_nonneg, (qpos + 1).astype(F32))
    c_hi0 = jnp.where(settled | ~above, n_nonneg, count(hi0, False))
    lo0 = jnp.where(settled | above, zero, lo0)
    hi0 = jnp.where(settled | ~above, zero, hi0)
    _, active0 = probe(lo0, hi0, jnp.int32(0))
    lo, hi, c_lo, c_hi, _, _ = lax.while_loop(bis_cond, bis_body, (lo0, hi0, c_lo0, c_hi0, active0, jnp.int32(0)))
    at_hi = c_hi >= kf
    thr = jnp.where(keep_all, F32_LOWEST, jnp.where(at_hi, hi, lo))
    excess = jnp.where(keep_all, 0.0, jnp.where(at_hi, c_hi, c_lo) - kf)

    @pl.when(jnp.max(excess) > 0.0)
    def _():
        budget = kf - count(thr, True)
        earlier = lax.broadcasted_iota(I32, (tks, tks), 1) < lax.broadcasted_iota(I32, (tks, tks), 0)
        earlier = jnp.where(earlier, 1.0, 0.0).astype(BF16)

        def fix(j, seen):
            r0 = pl.multiple_of(j * tks, tks)
            s = score_ref[pl.ds(r0, tks), :]
            eq = s == thr
            eqf = jnp.where(eq, 1.0, 0.0)
            rank = jnp.dot(earlier, eqf.astype(BF16), preferred_element_type=F32) + seen
            score_ref[pl.ds(r0, tks), :] = jnp.where(eq & (rank >= budget), -jnp.inf, s)
            return seen + jnp.sum(eqf, axis=0, keepdims=True)

        lax.fori_loop(0, nsub, fix, jnp.zeros((1, tq), F32))

    arow = lax.broadcasted_iota(I32, (KAUG - d, tq), 0)
    for h in range(H):
        slope = float(2.0 ** (-8.0 * (h + 1) / H))
        qa_ref[h, 0:d, :] = aqt_ref[h * d:(h + 1) * d, :]
        qa_ref[h, d:KAUG, :] = jnp.where(arow == 0, slope * ALIBI_SPLIT, jnp.where(arow == 1, slope, 0.0)).astype(BF16)
    for acc in acc_refs:
        acc[...] = jnp.zeros_like(acc)

    def logits(j, h):
        ka = ak_ref[pl.ds(pl.multiple_of(j * tka, tka), tka), (h // R) * KAUG:(h // R + 1) * KAUG]
        return jnp.dot(ka, qa_ref[h], preferred_element_type=F32)

    def step(j, j_next, cur_ref, next_ref, ms):
        r0 = pl.multiple_of(j * tka, tka)
        mask_ref[...] = jnp.where(score_ref[pl.ds(r0, tka), :] >= thr, 0.0, NEG_BIG)
        nms = []
        for h in range(H):
            g = h // R
            next_ref[h] = logits(j_next, h)
            s = cur_ref[h] + mask_ref[...]
            m_new = jnp.maximum(ms[h], jnp.max(s, axis=0, keepdims=True))
            p = jnp.exp(s - m_new).astype(BF16)
            va = avt_ref[g * VAUG:(g + 1) * VAUG, pl.ds(r0, tka)]
            acc = acc_refs[h]
            acc[...] = acc[...] * jnp.exp(ms[h] - m_new) + jnp.dot(va, p, preferred_element_type=F32)
            nms.append(m_new)
        return tuple(nms)

    for h in range(H):
        sa_ref[h] = logits(0, h)

    def att_pair(i, ms):
        ms = step(2 * i, 2 * i + 1, sa_ref, sb_ref, ms)
        return step(2 * i + 1, jnp.minimum(2 * i + 2, 2 * npair - 1), sb_ref, sa_ref, ms)

    lax.fori_loop(0, npair, att_pair, tuple(jnp.full((1, tq), NEG_BIG, F32) for _ in range(H)))
    for h in range(H):
        a = acc_refs[h][...]
        o = a[0:d, :] / a[d:d + 1, :]
        o_ref[h * d:(h + 1) * d, :] = (o * ms_ref[h * d:(h + 1) * d, :]).astype(BF16)


def _dsa(iqt, iwt, aqt, ik, ak, avt, ms_att, tq, tks):
    B, _, S = iqt.shape
    assert S <= ALIBI_SPLIT * 256 and tq % (2 * tks) == 0
    assert (S // (2 * tks)) % 2 == 0
    n_sel = min(TOPK_MAX, S // 4)
    G = DSA_KV_HEADS
    colT = lambda h: pl.BlockSpec((None, h, tq), lambda b, i: (b, 0, i))
    msb = jnp.broadcast_to(ms_att.reshape(DSA_W, 1), (DSA_W, tq))
    return pl.pallas_call(
        functools.partial(_dsa_kernel, tq=tq, tks=tks, n_sel=n_sel),
        grid=(B, S // tq),
        in_specs=[colT(IDX_HEADS * IDX_HD), colT(IDX_HEADS), colT(DSA_W),
                  pl.BlockSpec((None, S, IDX_HD), lambda b, i: (b, 0, 0)),
                  pl.BlockSpec((None, S, G * KAUG), lambda b, i: (b, 0, 0)),
                  pl.BlockSpec((None, G * VAUG, S), lambda b, i: (b, 0, 0)),
                  pl.BlockSpec((DSA_W, tq), lambda b, i: (0, 0))],
        out_specs=colT(DSA_W),
        out_shape=jax.ShapeDtypeStruct((B, DSA_W, S), BF16),
        scratch_shapes=[pltpu.VMEM((S, tq), F32), pltpu.VMEM((DSA_HEADS, KAUG, tq), BF16),
                        pltpu.VMEM((DSA_HEADS, 2 * tks, tq), F32), pltpu.VMEM((DSA_HEADS, 2 * tks, tq), F32),
                        pltpu.VMEM((2 * tks, tq), F32)]
        + [pltpu.VMEM((VAUG, tq), F32) for _ in range(DSA_HEADS)],
        compiler_params=_cparams(("parallel", "arbitrary")),
        name="dsa",
    )(iqt, iwt, aqt, ik, ak, avt, msb)


def _oproj_kernel(ret_ref, att_ref, x_ref, wo_ref, g1_ref, n2_ref, sc_ref, sh_ref, rw_ref, rb_ref,
                  x1_ref, h2_ref, sel_ref, idx_ref, gate_ref, cnt_ref):
    mixo = jnp.dot(ret_ref[...], wo_ref[:RET_W, :], preferred_element_type=F32)
    mixo = mixo + lax.dot_general(att_ref[...], wo_ref[RET_W:, :], (((0,), (0,)), ((), ())),
                                  preferred_element_type=F32)
    x1 = x_ref[...] + g1_ref[...] * mixo
    x1_ref[...] = x1
    y = x1 * lax.rsqrt(jnp.mean(x1 * x1, axis=-1, keepdims=True) + EPS) * n2_ref[...]
    h2 = y * (1.0 + sc_ref[...]) + sh_ref[...]
    _store_token_major(h2_ref, h2)
    h_hi = h2.astype(BF16)
    h_lo = (h2 - h_hi.astype(F32)).astype(BF16)
    logits = jnp.dot(h_hi, rw_ref[0], preferred_element_type=F32)
    logits = logits + (jnp.dot(h_hi, rw_ref[1], preferred_element_type=F32)
                       + jnp.dot(h_lo, rw_ref[0], preferred_element_type=F32)) + rb_ref[...]
    tm = logits.shape[0]
    lane = lax.broadcasted_iota(I32, (tm, LANES), 1).astype(F32)
    work = jnp.where(lane < N_EXPERTS, logits, -jnp.inf)
    sel = jnp.zeros((tm, LANES), F32)
    idx_tab = jnp.zeros((tm, LANES), F32)
    vals = []
    for k in range(TOP_K):
        m = jnp.max(work, axis=1, keepdims=True)
        idx = jnp.min(jnp.where(work == m, lane, float(LANES)), axis=1, keepdims=True)
        hit = lane == idx
        sel = jnp.where(hit, 1.0, sel)
        idx_tab = jnp.where(lane == k, idx, idx_tab)
        work = jnp.where(hit, -jnp.inf, work)
        vals.append(m)
    es = [jnp.exp(v - vals[0]) for v in vals]
    den = es[0] + es[1] + es[2] + es[3]
    gate_tab = jnp.zeros((tm, LANES), F32)
    for k in range(TOP_K):
        gate_tab = jnp.where(lane == k, es[k] / den, gate_tab)
    sel_ref[...] = sel
    idx_ref[...] = idx_tab
    gate_ref[...] = gate_tab

    @pl.when((pl.program_id(0) == 0) & (pl.program_id(1) == 0))
    def _():
        cnt_ref[...] = jnp.zeros_like(cnt_ref)

    cnt_ref[...] += jnp.sum(sel, axis=0, keepdims=True)


def _oproj(ret, att, x, wo, g1, n2g, sc2, sh2, rw_pad, rb_pad, tm):
    B, S, D = x.shape
    nt = S // tm
    row = lambda w: pl.BlockSpec((None, tm, w), lambda b, i: (b, i, 0))
    flat = lambda w: pl.BlockSpec((tm, w), lambda b, i: (b * nt + i, 0))
    vec = pl.BlockSpec((None, 1, D), lambda b, i: (b, 0, 0))
    cst = lambda shape: pl.BlockSpec(shape, lambda b, i: (0, 0))
    sd = lambda shape, dt: jax.ShapeDtypeStruct(shape, dt)
    N = B * S
    return pl.pallas_call(
        _oproj_kernel,
        grid=(B, nt),
        in_specs=[row(RET_W), pl.BlockSpec((None, DSA_W, tm), lambda b, i: (b, 0, i)), row(D), cst((D, D)), vec,
                  cst((1, D)), vec, vec,
                  pl.BlockSpec((2, D, LANES), lambda b, i: (0, 0, 0)), cst((1, LANES))],
        out_specs=[flat(D), pl.BlockSpec((tm * (D // LANES), LANES), lambda b, i: (b * nt + i, 0)),
                   flat(LANES), flat(LANES), flat(LANES), cst((1, LANES))],
        out_shape=[sd((N, D), F32), sd((N * (D // LANES), LANES), F32), sd((N, LANES), F32), sd((N, LANES), F32),
                   sd((N, LANES), F32), sd((1, LANES), F32)],
        compiler_params=_cparams(("arbitrary", "arbitrary")),
        name="oproj",
    )(ret, att, x, wo, g1, n2g.reshape(1, D), sc2, sh2, rw_pad, rb_pad)


def _dest_kernel(sel_ref, idx_ref, pstart_ref, dest_ref, seen_ref):
    @pl.when(pl.program_id(0) == 0)
    def _():
        seen_ref[...] = jnp.zeros_like(seen_ref)

    sel = sel_ref[...]
    tm = sel.shape[0]
    earlier = lax.broadcasted_iota(I32, (tm, tm), 1) < lax.broadcasted_iota(I32, (tm, tm), 0)
    earlier = jnp.where(earlier, 1.0, 0.0).astype(BF16)
    rank = jnp.dot(earlier, sel.astype(BF16), preferred_element_type=F32) + seen_ref[...]
    dest = pstart_ref[...] + rank
    lane = lax.broadcasted_iota(I32, (tm, LANES), 1).astype(F32)
    idx_tab = idx_ref[...]
    out = jnp.zeros((tm, LANES), F32)
    for k in range(TOP_K):
        e_k = jnp.sum(jnp.where(lane == k, idx_tab, 0.0), axis=1, keepdims=True)
        d_k = jnp.sum(jnp.where(lane == e_k, dest, 0.0), axis=1, keepdims=True)
        out = jnp.where(lane == k, d_k, out)
    dest_ref[...] = out.astype(I32)
    seen_ref[...] += jnp.sum(sel, axis=0, keepdims=True)


def _dest(sel, idx_tab, pstart, tm):
    N = sel.shape[0]
    blk = pl.BlockSpec((tm, LANES), lambda i: (i, 0))
    return pl.pallas_call(
        _dest_kernel,
        grid=(N // tm,),
        in_specs=[blk, blk, pl.BlockSpec((1, LANES), lambda i: (0, 0))],
        out_specs=blk,
        out_shape=jax.ShapeDtypeStruct((N, LANES), I32),
        scratch_shapes=[pltpu.VMEM((1, LANES), F32)],
        compiler_params=_cparams(("arbitrary",)),
        name="dest",
    )(sel, idx_tab, pstart)


def _disp_kernel(zs_ref, h2_ref, dest_hbm, xs_hbm, zbuf, idx_a, idx_b, sem_ia, sem_ib, sem_row, sem_z,
                 *, tm, nch, zrows):
    i = pl.program_id(0)
    n = tm * TOP_K

    def idx_copy(tile, buf, sem):
        return pltpu.make_async_copy(dest_hbm.at[pl.ds(tile * n, n)], buf, sem)

    def rows(ref, first, count):
        return ref.at[pl.ds(pl.multiple_of(first * nch, nch), count * nch)]

    def row_copy(row, dst):
        return pltpu.make_async_copy(rows(h2_ref, row, 1), rows(xs_hbm, dst, 1), sem_row)

    def issue_rows(first_row, idx):
        def body(t, c):
            for k in range(TOP_K):
                row_copy(first_row + t, idx[t * TOP_K + k]).start(priority=k % 2)
            return c

        lax.fori_loop(0, tm, body, 0)

    def wait_rows():
        pltpu.make_async_copy(rows(xs_hbm, 0, n), rows(xs_hbm, 0, n), sem_row).wait()

    @pl.when(i == 0)
    def _():
        zbuf[...] = jnp.zeros_like(zbuf)
        for e in range(N_EXPERTS):
            fill = pltpu.make_async_copy(zbuf, rows(xs_hbm, zs_ref[e], zrows), sem_z)
            fill.start()
            fill.wait()

    idx_copy(2 * i, idx_a, sem_ia).start()
    idx_copy(2 * i + 1, idx_b, sem_ib).start()
    idx_copy(2 * i, idx_a, sem_ia).wait()
    issue_rows(0, idx_a)
    idx_copy(2 * i + 1, idx_b, sem_ib).wait()
    issue_rows(tm, idx_b)
    wait_rows()
    wait_rows()


def _dispatch(h2, dest_flat, zero_start, n_rows, zrows, tm):
    nch = D_MODEL // LANES
    n_steps = h2.shape[0] // nch // (2 * tm)
    grid_spec = pltpu.PrefetchScalarGridSpec(
        num_scalar_prefetch=1,
        grid=(n_steps,),
        in_specs=[pl.BlockSpec((2 * tm * nch, LANES), lambda i, zs: (i, 0)), pl.BlockSpec(memory_space=pl.ANY)],
        out_specs=pl.BlockSpec(memory_space=pl.ANY),
        scratch_shapes=[pltpu.VMEM((zrows * nch, LANES), F32), pltpu.SMEM((tm * TOP_K,), I32),
                        pltpu.SMEM((tm * TOP_K,), I32), pltpu.SemaphoreType.DMA, pltpu.SemaphoreType.DMA,
                        pltpu.SemaphoreType.DMA, pltpu.SemaphoreType.DMA],
    )
    return pl.pallas_call(
        functools.partial(_disp_kernel, tm=tm, nch=nch, zrows=zrows),
        grid_spec=grid_spec,
        out_shape=jax.ShapeDtypeStruct((n_rows * nch, LANES), F32),
        compiler_params=_cparams(("arbitrary",)),
        name="disp",
    )(zero_start, h2, dest_flat)


def _store_token_major(ref, x):
    rows, d = x.shape
    nch = d // LANES
    for j in range(nch):
        ref[pl.ds(j, rows, stride=nch), :] = x[:, j * LANES:(j + 1) * LANES]


def _load_token_major(ref, rows, nch):
    return jnp.concatenate([ref[pl.ds(j, rows, stride=nch), :] for j in range(nch)], axis=1)


def _ffn_kernel(be_ref, nb_ref, xs_ref, wgu_ref, bgu_ref, wd_ref, bd_ref, ys_ref, wgu_bf, wd_bf):
    i = pl.program_id(0)
    live = i < nb_ref[0]

    @pl.when(live & ((i == 0) | (be_ref[i] != be_ref[jnp.maximum(i - 1, 0)])))
    def _():
        wgu_bf[...] = wgu_ref[...].astype(BF16)
        wd_bf[...] = wd_ref[...].astype(BF16)

    @pl.when(live)
    def _():
        xb = _load_token_major(xs_ref, ys_ref.shape[0] // NCH, NCH).astype(BF16)
        gu = jnp.dot(xb, wgu_bf[...], preferred_element_type=F32) + bgu_ref[...]
        gate = jnp.minimum(gu[:, :D_EXPERT], SWIGLU_LIMIT)
        up = jnp.clip(gu[:, D_EXPERT:], -SWIGLU_LIMIT, SWIGLU_LIMIT)
        glu = gate * (1.0 / (1.0 + jnp.exp(-SWIGLU_ALPHA * gate)))
        act = ((up + 1.0) * glu).astype(BF16)
        ys = jnp.dot(act, wd_bf[...], preferred_element_type=F32) + bd_ref[...]
        _store_token_major(ys_ref, ys)

    @pl.when(jnp.logical_not(live))
    def _():
        ys_ref[...] = jnp.zeros_like(ys_ref)


def _ffn(xs, block_e, n_blocks, w_gu, b_gu, w_down, b_down, tmb):
    D = D_MODEL
    P = xs.shape[0] // NCH
    E = w_gu.shape[0]
    blk = lambda i, be, nb: (jnp.minimum(i, nb[0] - 1), 0)
    wsel = lambda i, be, nb: (be[jnp.minimum(i, nb[0] - 1)], 0, 0)
    grid_spec = pltpu.PrefetchScalarGridSpec(
        num_scalar_prefetch=2,
        grid=(P // tmb,),
        in_specs=[pl.BlockSpec((tmb * NCH, LANES), blk),
                  pl.BlockSpec((None, D, 2 * D_EXPERT), wsel),
                  pl.BlockSpec((None, 1, 2 * D_EXPERT), wsel),
                  pl.BlockSpec((None, D_EXPERT, D), wsel),
                  pl.BlockSpec((None, 1, D), wsel)],
        out_specs=pl.BlockSpec((tmb * (D // LANES), LANES), lambda i, be, nb: (i, 0)),
        scratch_shapes=[pltpu.VMEM((D, 2 * D_EXPERT), BF16), pltpu.VMEM((D_EXPERT, D), BF16)],
    )
    return pl.pallas_call(
        _ffn_kernel,
        grid_spec=grid_spec,
        out_shape=jax.ShapeDtypeStruct((P * (D // LANES), LANES), F32),
        compiler_params=_cparams(("arbitrary",)),
        name="ffn",
    )(block_e, n_blocks, xs, w_gu, b_gu.reshape(E, 1, 2 * D_EXPERT), w_down, b_down.reshape(E, 1, D))


def _comb_kernel(x1_ref, gate_ref, g2_ref, fg_ref, dest_hbm, ys_hbm, o_ref, buf_a, buf_b, idx_a, idx_b,
                 sem_ia, sem_ib, sem_ra, sem_rb, *, tm, n_steps):
    i = pl.program_id(0)
    n = tm * TOP_K

    def idx_copy(tile, buf, sem):
        return pltpu.make_async_copy(dest_hbm.at[pl.ds(tile * n, n)], buf, sem)

    nch = x1_ref.shape[1] // LANES

    def row_copy(src, buf, k, t, sem):
        return pltpu.make_async_copy(ys_hbm.at[pl.ds(pl.multiple_of(src * nch, nch), nch)],
                                     buf.at[k, pl.ds(pl.multiple_of(t * nch, nch), nch)], sem)

    def issue_rows(idx, buf, sem):
        def body(t, c):
            for k in range(TOP_K):
                row_copy(idx[t * TOP_K + k], buf, k, t, sem).start(priority=k % 2)
            return c

        lax.fori_loop(0, tm, body, 0)

    def wait_rows(buf, sem):
        pltpu.make_async_copy(buf, buf, sem).wait()

    def finish(buf, lo):
        gates = gate_ref[lo:lo + tm, :]
        y = jnp.zeros((tm, x1_ref.shape[1]), F32)
        for k in range(TOP_K):
            y = y + gates[:, k:k + 1] * _load_token_major(buf.at[k], tm, nch)
        v = x1_ref[lo:lo + tm, :] + g2_ref[...] * y
        o_ref[lo:lo + tm, :] = v * lax.rsqrt(jnp.mean(v * v, axis=-1, keepdims=True) + EPS) * fg_ref[...]

    @pl.when(i == 0)
    def _():
        first = idx_copy(0, idx_a, sem_ia)
        first.start()
        first.wait()
        issue_rows(idx_a, buf_a, sem_ra)
        idx_copy(1, idx_b, sem_ib).start()

    idx_copy(2 * i + 1, idx_b, sem_ib).wait()
    issue_rows(idx_b, buf_b, sem_rb)

    @pl.when(i + 1 < n_steps)
    def _():
        idx_copy(2 * i + 2, idx_a, sem_ia).start()

    wait_rows(buf_a, sem_ra)
    finish(buf_a, 0)

    @pl.when(i + 1 < n_steps)
    def _():
        idx_copy(2 * i + 2, idx_a, sem_ia).wait()
        issue_rows(idx_a, buf_a, sem_ra)
        idx_copy(2 * i + 3, idx_b, sem_ib).start()

    wait_rows(buf_b, sem_rb)
    finish(buf_b, tm)


def _combine(x1, gate_tab, g2, final_g, dest_flat, ys, S, tm):
    N, D = x1.shape
    n_steps = N // (2 * tm)
    per_b = S // (2 * tm)
    return pl.pallas_call(
        functools.partial(_comb_kernel, tm=tm, n_steps=n_steps),
        grid=(n_steps,),
        in_specs=[pl.BlockSpec((2 * tm, D), lambda i: (i, 0)),
                  pl.BlockSpec((2 * tm, LANES), lambda i: (i, 0)),
                  pl.BlockSpec((None, 1, D), lambda i: (i // per_b, 0, 0)),
                  pl.BlockSpec((1, D), lambda i: (0, 0)),
                  pl.BlockSpec(memory_space=pl.ANY),
                  pl.BlockSpec(memory_space=pl.ANY)],
        out_specs=pl.BlockSpec((2 * tm, D), lambda i: (i, 0)),
        out_shape=jax.ShapeDtypeStruct((N, D), F32),
        scratch_shapes=[pltpu.VMEM((TOP_K, tm * (D // LANES), LANES), F32),
                        pltpu.VMEM((TOP_K, tm * (D // LANES), LANES), F32),
                        pltpu.SMEM((tm * TOP_K,), I32), pltpu.SMEM((tm * TOP_K,), I32),
                        pltpu.SemaphoreType.DMA, pltpu.SemaphoreType.DMA, pltpu.SemaphoreType.DMA,
                        pltpu.SemaphoreType.DMA],
        compiler_params=_cparams(("arbitrary",)),
        name="comb",
    )(x1, gate_tab, g2, final_g.reshape(1, D), dest_flat, ys)


def _tile(n, pref):
    t = min(pref, n)
    assert n % t == 0, (n, t)
    return t


def _layer(x, c, ada_w, ada_b, norm1_g, w_in, mix_scale, w_o, norm2_g,
           router_w, router_b, w_gu, b_gu, w_down, b_down, final_g):
    B, S, D = x.shape
    N = B * S
    mod = _mod(c, ada_w, ada_b).reshape(B, 6, 1, D)
    sh1, sc1, g1, sh2, sc2, g2 = (mod[:, j] for j in range(6))

    w_pad = jnp.pad(w_in, ((0, 0), (0, IN_COLS_PAD - IN_COLS))).astype(BF16)
    rq, rk, rv, rg, aqt, ak, avt, iqt, ik, iwt = _inproj(x, norm1_g, sc1, sh1, w_pad, _tile(S, 512))
    ms = mix_scale.reshape(1, RET_W + DSA_W)
    ret = _retention(rq, rk, rv, rg, ms[:, :RET_W])
    att = _dsa(iqt, iwt, aqt, ik, ak, avt, ms[:, RET_W:], _tile(S, 256), 128)

    rw_pad = jnp.pad(router_w, ((0, 0), (0, LANES - N_EXPERTS)))
    rw_hi = rw_pad.astype(BF16)
    rw_pad = jnp.stack([rw_hi, (rw_pad - rw_hi.astype(F32)).astype(BF16)])
    rb_pad = jnp.pad(router_b, (0, LANES - N_EXPERTS)).reshape(1, LANES)
    x1, h2, sel, idx_tab, gate_tab, counts = _oproj(ret, att, x, w_o.astype(BF16), g1, norm2_g, sc2, sh2,
                                                    rw_pad, rb_pad, _tile(S, 512))

    tmb = FFN_ROWS
    n_rows = (N * TOP_K + N_EXPERTS * (tmb - 1)) // tmb * tmb + tmb
    cnt = counts[0, :N_EXPERTS].astype(I32)
    padded = (cnt + tmb - 1) // tmb * tmb
    ends = jnp.cumsum(padded)
    starts = ends - padded
    pstart = jnp.pad(starts.astype(F32), (0, LANES - N_EXPERTS)).reshape(1, LANES)
    n_blocks = (ends[-1] // tmb).reshape(1)
    first_row = jnp.arange(n_rows // tmb, dtype=I32) * tmb
    block_e = jnp.minimum(jnp.sum((ends[None, :] <= first_row[:, None]).astype(I32), axis=1), N_EXPERTS - 1)

    tmd = _tile(N, 256)
    dest_tab = _dest(sel, idx_tab, pstart, tmd)
    dest_flat = dest_tab[:, :TOP_K].reshape(N * TOP_K)
    xs = _dispatch(h2, dest_flat, starts + cnt, n_rows, tmb, tmd)
    ys = _ffn(xs, block_e, n_blocks, w_gu, b_gu, w_down, b_down, tmb)
    out = _combine(x1, gate_tab, g2, final_g, dest_flat, ys, S, _tile(S, 256))
    return out.reshape(B, S, D)


def kernel(x, c, ada_w, ada_b, norm1_g, w_in, mix_scale, w_o, norm2_g, router_w, router_b, w_gu, b_gu,
           w_down, b_down, final_g):
    assert ada_w.shape[0] == 1, "single-layer stack"
    return _layer(x, c, ada_w[0], ada_b[0], norm1_g[0], w_in[0], mix_scale[0], w_o[0], norm2_g[0],
                  router_w[0], router_b[0], w_gu[0], b_gu[0], w_down[0], b_down[0], final_g)
```

```python
import functools

import numpy as np
import jax
import jax.numpy as jnp
from jax import lax
from jax.experimental import pallas as pl
from jax.experimental.pallas import tpu as pltpu

F32 = jnp.float32
BF16 = jnp.bfloat16
I32 = jnp.int32

D_MODEL = 1024
RET_HEADS = 4
RET_DK = 64
RET_DV = 128
RET_CHUNK = 128
DSA_HEADS = 8
DSA_KV_HEADS = 2
DSA_HD = 64
IDX_HEADS = 8
IDX_HD = 64
TOPK_MAX = 256
N_EXPERTS = 32
TOP_K = 4
D_EXPERT = D_MODEL
SWIGLU_LIMIT = 7.0
SWIGLU_ALPHA = 1.702
EPS = 1e-6

RET_W = RET_HEADS * RET_DV
DSA_W = DSA_HEADS * DSA_HD
IN_COLS = 2888
IN_COLS_PAD = 2944

KAUG = 128
VAUG = 80
ALIBI_SPLIT = 64
FFN_ROWS = 512
BISECT_VALUE_STEPS = 8
BISECT_MAX_STEPS = 64

LANES = 128
NCH = D_MODEL // LANES
VMEM_LIMIT = 56 * 1024 * 1024
NEG_BIG = -1e30
F32_LOWEST = float(np.finfo(np.float32).min)


def _cparams(sem):
    return pltpu.CompilerParams(dimension_semantics=sem, vmem_limit_bytes=VMEM_LIMIT)


def _mod_kernel(c_ref, w_ref, b_ref, o_ref):
    c = c_ref[...]
    s = c * (1.0 / (1.0 + jnp.exp(-c)))
    o_ref[...] = jnp.dot(s, w_ref[...], preferred_element_type=F32,
                         precision=lax.Precision.HIGHEST) + b_ref[...]


def _mod(c, ada_w, ada_b):
    B, D = c.shape
    n_out = ada_w.shape[1]
    rows = 8
    c8 = jnp.zeros((rows, D), F32).at[:B].set(c)
    out = pl.pallas_call(
        _mod_kernel,
        grid=(n_out // D,),
        in_specs=[pl.BlockSpec((rows, D), lambda j: (0, 0)),
                  pl.BlockSpec((D, D), lambda j: (0, j)),
                  pl.BlockSpec((1, D), lambda j: (0, j))],
        out_specs=pl.BlockSpec((rows, D), lambda j: (0, j)),
        out_shape=jax.ShapeDtypeStruct((rows, n_out), F32),
        compiler_params=_cparams(("arbitrary",)),
        name="mod",
    )(c8, ada_w, ada_b.reshape(1, n_out))
    return out[:B]


def _inproj_kernel(x_ref, g_ref, sc_ref, sh_ref, w_ref,
                   rq_ref, rk_ref, rv_ref, rg_ref, aqt_ref, ak_ref, avt_ref, iqt_ref, ik_ref, iwt_ref):
    x = x_ref[...]
    ms = jnp.mean(x * x, axis=-1, keepdims=True)
    y = x * lax.rsqrt(ms + EPS) * g_ref[...]
    hb = (y * (1.0 + sc_ref[...]) + sh_ref[...]).astype(BF16)

    def proj(lo, hi):
        return jnp.dot(hb, w_ref[:, lo:hi], preferred_element_type=F32)

    tm = x.shape[0]
    d = DSA_HD
    rq_ref[...] = proj(0, 256).astype(BF16)
    rk_ref[...] = (proj(256, 512) * (RET_DK ** -0.5)).astype(BF16)
    rv_ref[...] = proj(512, 1024).astype(BF16)
    rg_ref[...] = proj(1024, 1536).astype(BF16)
    aqt_ref[...] = (proj(1536, 2048) * (d ** -0.5)).T.astype(BF16)
    kk = proj(2048, 2176)
    pos = pl.program_id(1) * tm + lax.broadcasted_iota(I32, (tm, d), 0)
    col = lax.broadcasted_iota(I32, (tm, d), 1)
    posblk = jnp.where(col == 0, pos // ALIBI_SPLIT, jnp.where(col == 1, pos % ALIBI_SPLIT, 0)).astype(F32)
    for g in range(DSA_KV_HEADS):
        ak_ref[:, g * KAUG:g * KAUG + d] = kk[:, g * d:(g + 1) * d].astype(BF16)
        ak_ref[:, g * KAUG + d:(g + 1) * KAUG] = posblk.astype(BF16)
    vt = proj(2176, 2304).T
    r16 = lax.broadcasted_iota(I32, (VAUG - d, tm), 0)
    onesblk = jnp.where(r16 == 0, 1.0, 0.0).astype(BF16)
    for g in range(DSA_KV_HEADS):
        avt_ref[g * VAUG:g * VAUG + d, :] = vt[g * d:(g + 1) * d, :].astype(BF16)
        avt_ref[g * VAUG + d:(g + 1) * VAUG, :] = onesblk
    iqt_ref[...] = proj(2304, 2816).T.astype(BF16)
    last = proj(2816, 2944)
    ik_ref[...] = last[:, :IDX_HD].astype(BF16)
    iwt_ref[...] = last.T[IDX_HD:IDX_HD + IDX_HEADS, :] * ((IDX_HD ** -0.5) * (IDX_HEADS ** -0.5))


def _inproj(x, norm_g, sc, sh, w_pad, tm):
    B, S, D = x.shape
    row = lambda w: pl.BlockSpec((None, tm, w), lambda b, i: (b, i, 0))
    colT = lambda h: pl.BlockSpec((None, h, tm), lambda b, i: (b, 0, i))
    vec = pl.BlockSpec((None, 1, D), lambda b, i: (b, 0, 0))
    sd = lambda shape, dt: jax.ShapeDtypeStruct(shape, dt)
    G = DSA_KV_HEADS
    return pl.pallas_call(
        _inproj_kernel,
        grid=(B, S // tm),
        in_specs=[row(D), pl.BlockSpec((1, D), lambda b, i: (0, 0)), vec, vec,
                  pl.BlockSpec((D, IN_COLS_PAD), lambda b, i: (0, 0))],
        out_specs=[row(256), row(256), row(512), row(512), colT(DSA_W), row(G * KAUG), colT(G * VAUG),
                   colT(IDX_HEADS * IDX_HD), row(IDX_HD), colT(IDX_HEADS)],
        out_shape=[sd((B, S, 256), BF16), sd((B, S, 256), BF16), sd((B, S, 512), BF16),
                   sd((B, S, 512), BF16), sd((B, DSA_W, S), BF16), sd((B, S, G * KAUG), BF16),
                   sd((B, G * VAUG, S), BF16), sd((B, IDX_HEADS * IDX_HD, S), BF16),
                   sd((B, S, IDX_HD), BF16), sd((B, IDX_HEADS, S), F32)],
        compiler_params=_cparams(("parallel", "parallel")),
        name="inproj",
    )(x, norm_g.reshape(1, D), sc, sh, w_pad)


def _ret_kernel(rq_ref, rk_ref, rv_ref, rg_ref, din_ref, qd_ref, kd_ref, cd_ref, ms_ref, o_ref, state_ref):
    @pl.when(pl.program_id(1) == 0)
    def _():
        state_ref[...] = jnp.zeros_like(state_ref)

    for h in range(RET_HEADS):
        q = rq_ref[:, h * RET_DK:(h + 1) * RET_DK]
        k = rk_ref[:, h * RET_DK:(h + 1) * RET_DK]
        v = rv_ref[:, h * RET_DV:(h + 1) * RET_DV]
        r_prev = state_ref[h]
        s = lax.dot_general(q, k, (((1,), (1,)), ((), ())), preferred_element_type=F32) * din_ref[h]
        o = jnp.dot(s.astype(BF16), v, preferred_element_type=F32)
        o = o + jnp.dot(q, r_prev.astype(BF16), preferred_element_type=F32) * qd_ref[h]
        vd = (v.astype(F32) * kd_ref[h]).astype(BF16)
        kv = lax.dot_general(k, vd, (((0,), (0,)), ((), ())), preferred_element_type=F32)
        state_ref[h] = r_prev * cd_ref[h] + kv
        o = o * lax.rsqrt(jnp.mean(o * o, axis=-1, keepdims=True) + EPS)
        g = rg_ref[:, h * RET_DV:(h + 1) * RET_DV].astype(F32)
        gate = g * (1.0 / (1.0 + jnp.exp(-g)))
        o_ref[:, h * RET_DV:(h + 1) * RET_DV] = (gate * o * ms_ref[:, h * RET_DV:(h + 1) * RET_DV]).astype(BF16)


def _ret_consts(C):
    H = RET_HEADS
    log_g = np.log1p(-np.exp2(-5.0 - np.arange(H, dtype=np.float64)))
    pos = np.arange(C, dtype=np.float64)
    diff = pos[:, None] - pos[None, :]
    d_inner = np.where(diff[None] >= 0, np.exp(np.maximum(diff, 0.0)[None] * log_g[:, None, None]), 0.0)
    q_decay = np.exp((pos + 1.0)[None] * log_g[:, None])
    k_decay = np.exp((C - 1.0 - pos)[None] * log_g[:, None])
    chunk_decay = np.exp(C * log_g)
    qd = np.broadcast_to(q_decay[:, :, None], (H, C, RET_DV))
    kd = np.broadcast_to(k_decay[:, :, None], (H, C, RET_DV))
    cd = np.broadcast_to(chunk_decay[:, None, None], (H, 1, RET_DV))
    f = lambda a: jnp.asarray(np.ascontiguousarray(a), F32)
    return f(d_inner), f(qd), f(kd), f(cd)


def _retention(rq, rk, rv, rg, ms_ret):
    B, S, _ = rq.shape
    C = min(RET_CHUNK, S)
    din, qd, kd, cd = _ret_consts(C)
    row = lambda w: pl.BlockSpec((None, C, w), lambda b, n: (b, n, 0))
    full = lambda a: pl.BlockSpec(a.shape, lambda b, n: (0,) * a.ndim)
    return pl.pallas_call(
        _ret_kernel,
        grid=(B, S // C),
        in_specs=[row(256), row(256), row(512), row(512), full(din), full(qd), full(kd), full(cd),
                  pl.BlockSpec((1, RET_W), lambda b, n: (0, 0))],
        out_specs=row(RET_W),
        out_shape=jax.ShapeDtypeStruct((B, S, RET_W), BF16),
        scratch_shapes=[pltpu.VMEM((RET_HEADS, RET_DK, RET_DV), F32)],
        compiler_params=_cparams(("parallel", "arbitrary")),
        name="ret",
    )(rq, rk, rv, rg, din, qd, kd, cd, ms_ret)


def _f32_key(x):
    i = lax.bitcast_convert_type(x, I32)
    return i ^ ((i >> 31) & 0x7FFFFFFF)


def _key_f32(k):
    return lax.bitcast_convert_type(k ^ ((k >> 31) & 0x7FFFFFFF), F32)


def _dsa_kernel(iqt_ref, iwt_ref, aqt_ref, ik_ref, ak_ref, avt_ref, ms_ref, o_ref, score_ref, qa_ref, sa_ref, sb_ref, mask_ref, *acc_refs,
                tq, tks, n_sel):
    H, G, R, d = DSA_HEADS, DSA_KV_HEADS, DSA_HEADS // DSA_KV_HEADS, DSA_HD
    t0 = pl.program_id(1) * tq
    nsub = (t0 + tq) // tks
    tka = 2 * tks
    npair = ((t0 + tq) // tka + 1) // 2
    kf = float(n_sel)
    qpos = t0 + lax.broadcasted_iota(I32, (1, tq), 1)
    krow = lax.broadcasted_iota(I32, (tks, tq), 0)

    wrow = [iwt_ref[h:h + 1, :] for h in range(IDX_HEADS)]

    def fold8(x, op):
        acc = x[0:8, :]
        for i in range(1, tks // 8):
            acc = op(acc, x[8 * i:8 * (i + 1), :])
        return acc

    def score_pair(i, carry):
        mx, mn, npos, nnon = carry
        for u in range(2 * tka // tks):
            r0 = pl.multiple_of(i * 2 * tka + u * tks, tks)
            kc = ik_ref[pl.ds(r0, tks), :]
            acc = jnp.zeros((tks, tq), F32)
            for h in range(IDX_HEADS):
                rel = jnp.dot(kc, iqt_ref[h * IDX_HD:(h + 1) * IDX_HD, :], preferred_element_type=F32)
                acc = acc + jnp.maximum(rel, 0.0) * wrow[h]
            causal = r0 + krow <= qpos
            sc = jnp.where(causal, acc, -jnp.inf)
            score_ref[pl.ds(r0, tks), :] = sc
            mx = jnp.maximum(mx, fold8(sc, jnp.maximum))
            mn = jnp.minimum(mn, fold8(jnp.where(causal, acc, jnp.inf), jnp.minimum))
            npos = npos + fold8(jnp.where(sc > 0.0, 1.0, 0.0), jnp.add)
            nnon = nnon + fold8(jnp.where(sc >= 0.0, 1.0, 0.0), jnp.add)
        return mx, mn, npos, nnon

    stat0 = (jnp.full((8, tq), -jnp.inf, F32), jnp.full((8, tq), jnp.inf, F32),
             jnp.zeros((8, tq), F32), jnp.zeros((8, tq), F32))
    mx, mn, npos, nnon = lax.fori_loop(0, npair, score_pair, stat0)
    top = jnp.max(mx, axis=0, keepdims=True)
    lo0 = jnp.min(mn, axis=0, keepdims=True)
    n_pos = jnp.sum(npos, axis=0, keepdims=True)
    n_nonneg = jnp.sum(nnon, axis=0, keepdims=True)


    def count(th, strict):
        def body(j, acc):
            for u in range(2):
                s = score_ref[pl.ds(pl.multiple_of((2 * j + u) * tks, tks), tks), :]
                hit = (s > th) if strict else (s >= th)
                acc = acc + fold8(jnp.where(hit, 1.0, 0.0), jnp.add)
            return acc

        acc = lax.fori_loop(0, nsub // 2, body, jnp.zeros((8, tq), F32))
        return jnp.sum(acc, axis=0, keepdims=True)

    def probe(lo, hi, it):
        lk, hk = _f32_key(lo), _f32_key(hi)
        mk = (lk >> 1) + (hk >> 1) + (lk & hk & 1)
        mv = lo + (hi - lo) * 0.5
        early = (jnp.zeros((1, tq), I32) + it) < BISECT_VALUE_STEPS
        mid = jnp.where(early & (mv > lo) & (mv < hi), mv, _key_f32(mk))
        return mid, jnp.max(jnp.where(mk != lk, 1.0, 0.0))

    def bis_cond(c):
        return (c[5] > 0.0) & (c[6] < BISECT_MAX_STEPS)

    def bis_body(c):
        lo, hi, c_lo, c_hi, mid, _, it = c
        cnt = count(mid, False)
        ge = cnt >= kf
        up = ge | (cnt == kf)
        dn = (~ge) | (cnt == kf)
        lo, c_lo = jnp.where(up, mid, lo), jnp.where(up, cnt, c_lo)
        hi, c_hi = jnp.where(dn, mid, hi), jnp.where(dn, cnt, c_hi)
        mid, active = probe(lo, hi, it + 1)
        return lo, hi, c_lo, c_hi, mid, active, it + 1

    zero = jnp.zeros((1, tq), F32)
    keep_all = qpos + 1 <= n_sel
    settled = keep_all | ((n_nonneg >= kf) & (n_pos < kf))
    above = n_pos >= kf
    c_lo0 = jnp.where(settled | above, n_nonneg, (qpos + 1).astype(F32))
    c_hi0 = jnp.where(settled | ~above, n_nonneg, zero)
    lo0 = jnp.where(settled | above, zero, lo0)
    hi0 = jnp.where(settled | ~above, zero, _key_f32(_f32_key(top) + 1))
    mid0, active0 = probe(lo0, hi0, jnp.int32(0))
    lo, hi, c_lo, c_hi, _, _, _ = lax.while_loop(bis_cond, bis_body,
                                                 (lo0, hi0, c_lo0, c_hi0, mid0, active0, jnp.int32(0)))
    at_hi = c_hi >= kf
    thr = jnp.where(keep_all, F32_LOWEST, jnp.where(at_hi, hi, lo))
    excess = jnp.where(keep_all, 0.0, jnp.where(at_hi, c_hi, c_lo) - kf)

    @pl.when(jnp.max(excess) > 0.0)
    def _():
        budget = kf - count(thr, True)
        earlier = lax.broadcasted_iota(I32, (tks, tks), 1) < lax.broadcasted_iota(I32, (tks, tks), 0)
        earlier = jnp.where(earlier, 1.0, 0.0).astype(BF16)

        def fix(j, seen):
            r0 = pl.multiple_of(j * tks, tks)
            s = score_ref[pl.ds(r0, tks), :]
            eq = s == thr
            eqf = jnp.where(eq, 1.0, 0.0)
            rank = jnp.dot(earlier, eqf.astype(BF16), preferred_element_type=F32) + seen
            score_ref[pl.ds(r0, tks), :] = jnp.where(eq & (rank >= budget), -jnp.inf, s)
            return seen + jnp.sum(eqf, axis=0, keepdims=True)

        lax.fori_loop(0, nsub, fix, jnp.zeros((1, tq), F32))

    arow = lax.broadcasted_iota(I32, (KAUG - d, tq), 0)
    for h in range(H):
        slope = float(2.0 ** (-8.0 * (h + 1) / H))
        qa_ref[h, 0:d, :] = aqt_ref[h * d:(h + 1) * d, :]
        qa_ref[h, d:KAUG, :] = jnp.where(arow == 0, slope * ALIBI_SPLIT, jnp.where(arow == 1, slope, 0.0)).astype(BF16)
    for acc in acc_refs:
        acc[...] = jnp.zeros_like(acc)

    def logits(j, h):
        ka = ak_ref[pl.ds(pl.multiple_of(j * tka, tka), tka), (h // R) * KAUG:(h // R + 1) * KAUG]
        return jnp.dot(ka, qa_ref[h], preferred_element_type=F32)

    def step(j, j_next, cur_ref, next_ref, ms):
        r0 = pl.multiple_of(j * tka, tka)
        mask_ref[...] = jnp.where(score_ref[pl.ds(r0, tka), :] >= thr, 0.0, NEG_BIG)
        nms = []
        for h in range(H):
            g = h // R
            next_ref[h] = logits(j_next, h)
            s = cur_ref[h] + mask_ref[...]
            m_new = jnp.maximum(ms[h], jnp.max(s, axis=0, keepdims=True))
            p = jnp.exp(s - m_new).astype(BF16)
            va = avt_ref[g * VAUG:(g + 1) * VAUG, pl.ds(r0, tka)]
            acc = acc_refs[h]
            acc[...] = acc[...] * jnp.exp(ms[h] - m_new) + jnp.dot(va, p, preferred_element_type=F32)
            nms.append(m_new)
        return tuple(nms)

    for h in range(H):
        sa_ref[h] = logits(0, h)

    def att_pair(i, ms):
        ms = step(2 * i, 2 * i + 1, sa_ref, sb_ref, ms)
        return step(2 * i + 1, jnp.minimum(2 * i + 2, 2 * npair - 1), sb_ref, sa_ref, ms)

    lax.fori_loop(0, npair, att_pair, tuple(jnp.full((1, tq), NEG_BIG, F32) for _ in range(H)))
    for h in range(H):
        a = acc_refs[h][...]
        o = a[0:d, :] / a[d:d + 1, :]
        o_ref[h * d:(h + 1) * d, :] = (o * ms_ref[h * d:(h + 1) * d, :]).astype(BF16)


def _dsa(iqt, iwt, aqt, ik, ak, avt, ms_att, tq, tks):
    B, _, S = iqt.shape
    assert S <= ALIBI_SPLIT * 256 and tq % (2 * tks) == 0
    assert (S // (2 * tks)) % 2 == 0
    n_sel = min(TOPK_MAX, S // 4)
    G = DSA_KV_HEADS
    colT = lambda h: pl.BlockSpec((None, h, tq), lambda b, i: (b, 0, i))
    msb = jnp.broadcast_to(ms_att.reshape(DSA_W, 1), (DSA_W, tq))
    return pl.pallas_call(
        functools.partial(_dsa_kernel, tq=tq, tks=tks, n_sel=n_sel),
        grid=(B, S // tq),
        in_specs=[colT(IDX_HEADS * IDX_HD), colT(IDX_HEADS), colT(DSA_W),
                  pl.BlockSpec((None, S, IDX_HD), lambda b, i: (b, 0, 0)),
                  pl.BlockSpec((None, S, G * KAUG), lambda b, i: (b, 0, 0)),
                  pl.BlockSpec((None, G * VAUG, S), lambda b, i: (b, 0, 0)),
                  pl.BlockSpec((DSA_W, tq), lambda b, i: (0, 0))],
        out_specs=colT(DSA_W),
        out_shape=jax.ShapeDtypeStruct((B, DSA_W, S), BF16),
        scratch_shapes=[pltpu.VMEM((S, tq), F32), pltpu.VMEM((DSA_HEADS, KAUG, tq), BF16),
                        pltpu.VMEM((DSA_HEADS, 2 * tks, tq), F32), pltpu.VMEM((DSA_HEADS, 2 * tks, tq), F32),
                        pltpu.VMEM((2 * tks, tq), F32)]
        + [pltpu.VMEM((VAUG, tq), F32) for _ in range(DSA_HEADS)],
        compiler_params=_cparams(("parallel", "arbitrary")),
        name="dsa",
    )(iqt, iwt, aqt, ik, ak, avt, msb)


def _oproj_kernel(ret_ref, att_ref, x_ref, wo_ref, g1_ref, n2_ref, sc_ref, sh_ref, rw_ref, rb_ref,
                  x1_ref, h2_ref, sel_ref, idx_ref, gate_ref, cnt_ref):
    mixo = jnp.dot(ret_ref[...], wo_ref[:RET_W, :], preferred_element_type=F32)
    mixo = mixo + lax.dot_general(att_ref[...], wo_ref[RET_W:, :], (((0,), (0,)), ((), ())),
                                  preferred_element_type=F32)
    x1 = x_ref[...] + g1_ref[...] * mixo
    x1_ref[...] = x1
    y = x1 * lax.rsqrt(jnp.mean(x1 * x1, axis=-1, keepdims=True) + EPS) * n2_ref[...]
    h2 = y * (1.0 + sc_ref[...]) + sh_ref[...]
    _store_token_major(h2_ref, h2)
    h_hi = h2.astype(BF16)
    h_lo = (h2 - h_hi.astype(F32)).astype(BF16)
    logits = jnp.dot(h_hi, rw_ref[0], preferred_element_type=F32)
    logits = logits + (jnp.dot(h_hi, rw_ref[1], preferred_element_type=F32)
                       + jnp.dot(h_lo, rw_ref[0], preferred_element_type=F32)) + rb_ref[...]
    tm = logits.shape[0]
    lane = lax.broadcasted_iota(I32, (tm, LANES), 1).astype(F32)
    work = jnp.where(lane < N_EXPERTS, logits, -jnp.inf)
    sel = jnp.zeros((tm, LANES), F32)
    idx_tab = jnp.zeros((tm, LANES), F32)
    vals = []
    for k in range(TOP_K):
        m = jnp.max(work, axis=1, keepdims=True)
        idx = jnp.min(jnp.where(work == m, lane, float(LANES)), axis=1, keepdims=True)
        hit = lane == idx
        sel = jnp.where(hit, 1.0, sel)
        idx_tab = jnp.where(lane == k, idx, idx_tab)
        work = jnp.where(hit, -jnp.inf, work)
        vals.append(m)
    es = [jnp.exp(v - vals[0]) for v in vals]
    den = es[0] + es[1] + es[2] + es[3]
    gate_tab = jnp.zeros((tm, LANES), F32)
    for k in range(TOP_K):
        gate_tab = jnp.where(lane == k, es[k] / den, gate_tab)
    sel_ref[...] = sel
    idx_ref[...] = idx_tab
    gate_ref[...] = gate_tab

    @pl.when((pl.program_id(0) == 0) & (pl.program_id(1) == 0))
    def _():
        cnt_ref[...] = jnp.zeros_like(cnt_ref)

    cnt_ref[...] += jnp.sum(sel, axis=0, keepdims=True)


def _oproj(ret, att, x, wo, g1, n2g, sc2, sh2, rw_pad, rb_pad, tm):
    B, S, D = x.shape
    nt = S // tm
    row = lambda w: pl.BlockSpec((None, tm, w), lambda b, i: (b, i, 0))
    flat = lambda w: pl.BlockSpec((tm, w), lambda b, i: (b * nt + i, 0))
    vec = pl.BlockSpec((None, 1, D), lambda b, i: (b, 0, 0))
    cst = lambda shape: pl.BlockSpec(shape, lambda b, i: (0, 0))
    sd = lambda shape, dt: jax.ShapeDtypeStruct(shape, dt)
    N = B * S
    return pl.pallas_call(
        _oproj_kernel,
        grid=(B, nt),
        in_specs=[row(RET_W), pl.BlockSpec((None, DSA_W, tm), lambda b, i: (b, 0, i)), row(D), cst((D, D)), vec,
                  cst((1, D)), vec, vec,
                  pl.BlockSpec((2, D, LANES), lambda b, i: (0, 0, 0)), cst((1, LANES))],
        out_specs=[flat(D), pl.BlockSpec((tm * (D // LANES), LANES), lambda b, i: (b * nt + i, 0)),
                   flat(LANES), flat(LANES), flat(LANES), cst((1, LANES))],
        out_shape=[sd((N, D), F32), sd((N * (D // LANES), LANES), F32), sd((N, LANES), F32), sd((N, LANES), F32),
                   sd((N, LANES), F32), sd((1, LANES), F32)],
        compiler_params=_cparams(("arbitrary", "arbitrary")),
        name="oproj",
    )(ret, att, x, wo, g1, n2g.reshape(1, D), sc2, sh2, rw_pad, rb_pad)


def _dest_kernel(sel_ref, idx_ref, pstart_ref, dest_ref, seen_ref):
    @pl.when(pl.program_id(0) == 0)
    def _():
        seen_ref[...] = jnp.zeros_like(seen_ref)

    sel = sel_ref[...]
    tm = sel.shape[0]
    earlier = lax.broadcasted_iota(I32, (tm, tm), 1) < lax.broadcasted_iota(I32, (tm, tm), 0)
    earlier = jnp.where(earlier, 1.0, 0.0).astype(BF16)
    rank = jnp.dot(earlier, sel.astype(BF16), preferred_element_type=F32) + seen_ref[...]
    dest = pstart_ref[...] + rank
    lane = lax.broadcasted_iota(I32, (tm, LANES), 1).astype(F32)
    idx_tab = idx_ref[...]
    out = jnp.zeros((tm, LANES), F32)
    for k in range(TOP_K):
        e_k = jnp.sum(jnp.where(lane == k, idx_tab, 0.0), axis=1, keepdims=True)
        d_k = jnp.sum(jnp.where(lane == e_k, dest, 0.0), axis=1, keepdims=True)
        out = jnp.where(lane == k, d_k, out)
    dest_ref[...] = out.astype(I32)
    seen_ref[...] += jnp.sum(sel, axis=0, keepdims=True)


def _dest(sel, idx_tab, pstart, tm):
    N = sel.shape[0]
    blk = pl.BlockSpec((tm, LANES), lambda i: (i, 0))
    return pl.pallas_call(
        _dest_kernel,
        grid=(N // tm,),
        in_specs=[blk, blk, pl.BlockSpec((1, LANES), lambda i: (0, 0))],
        out_specs=blk,
        out_shape=jax.ShapeDtypeStruct((N, LANES), I32),
        scratch_shapes=[pltpu.VMEM((1, LANES), F32)],
        compiler_params=_cparams(("arbitrary",)),
        name="dest",
    )(sel, idx_tab, pstart)


def _disp_kernel(zs_ref, h2_ref, dest_hbm, xs_hbm, zbuf, idx_a, idx_b, sem_ia, sem_ib, sem_row, sem_z,
                 *, tm, nch, zrows):
    i = pl.program_id(0)
    n = tm * TOP_K

    def idx_copy(tile, buf, sem):
        return pltpu.make_async_copy(dest_hbm.at[pl.ds(tile * n, n)], buf, sem)

    def rows(ref, first, count):
        return ref.at[pl.ds(pl.multiple_of(first * nch, nch), count * nch)]

    def row_copy(row, dst):
        return pltpu.make_async_copy(rows(h2_ref, row, 1), rows(xs_hbm, dst, 1), sem_row)

    def issue_rows(first_row, idx):
        def body(t, c):
            for k in range(TOP_K):
                row_copy(first_row + t, idx[t * TOP_K + k]).start(priority=k % 2)
            return c

        lax.fori_loop(0, tm, body, 0)

    def wait_rows():
        pltpu.make_async_copy(rows(xs_hbm, 0, n), rows(xs_hbm, 0, n), sem_row).wait()

    @pl.when(i == 0)
    def _():
        zbuf[...] = jnp.zeros_like(zbuf)
        for e in range(N_EXPERTS):
            fill = pltpu.make_async_copy(zbuf, rows(xs_hbm, zs_ref[e], zrows), sem_z)
            fill.start()
            fill.wait()

    idx_copy(2 * i, idx_a, sem_ia).start()
    idx_copy(2 * i + 1, idx_b, sem_ib).start()
    idx_copy(2 * i, idx_a, sem_ia).wait()
    issue_rows(0, idx_a)
    idx_copy(2 * i + 1, idx_b, sem_ib).wait()
    issue_rows(tm, idx_b)
    wait_rows()
    wait_rows()


def _dispatch(h2, dest_flat, zero_start, n_rows, zrows, tm):
    nch = D_MODEL // LANES
    n_steps = h2.shape[0] // nch // (2 * tm)
    grid_spec = pltpu.PrefetchScalarGridSpec(
        num_scalar_prefetch=1,
        grid=(n_steps,),
        in_specs=[pl.BlockSpec((2 * tm * nch, LANES), lambda i, zs: (i, 0)), pl.BlockSpec(memory_space=pl.ANY)],
        out_specs=pl.BlockSpec(memory_space=pl.ANY),
        scratch_shapes=[pltpu.VMEM((zrows * nch, LANES), F32), pltpu.SMEM((tm * TOP_K,), I32),
                        pltpu.SMEM((tm * TOP_K,), I32), pltpu.SemaphoreType.DMA, pltpu.SemaphoreType.DMA,
                        pltpu.SemaphoreType.DMA, pltpu.SemaphoreType.DMA],
    )
    return pl.pallas_call(
        functools.partial(_disp_kernel, tm=tm, nch=nch, zrows=zrows),
        grid_spec=grid_spec,
        out_shape=jax.ShapeDtypeStruct((n_rows * nch, LANES), F32),
        compiler_params=_cparams(("arbitrary",)),
        name="disp",
    )(zero_start, h2, dest_flat)


def _store_token_major(ref, x):
    rows, d = x.shape
    nch = d // LANES
    for j in range(nch):
        ref[pl.ds(j, rows, stride=nch), :] = x[:, j * LANES:(j + 1) * LANES]


def _load_token_major(ref, rows, nch):
    return jnp.concatenate([ref[pl.ds(j, rows, stride=nch), :] for j in range(nch)], axis=1)


def _ffn_kernel(be_ref, nb_ref, xs_ref, wgu_ref, bgu_ref, wd_ref, bd_ref, ys_ref, wgu_bf, wd_bf):
    i = pl.program_id(0)
    live = i < nb_ref[0]

    @pl.when(live & ((i == 0) | (be_ref[i] != be_ref[jnp.maximum(i - 1, 0)])))
    def _():
        wgu_bf[...] = wgu_ref[...].astype(BF16)
        wd_bf[...] = wd_ref[...].astype(BF16)

    @pl.when(live)
    def _():
        xb = _load_token_major(xs_ref, ys_ref.shape[0] // NCH, NCH).astype(BF16)
        gu = jnp.dot(xb, wgu_bf[...], preferred_element_type=F32) + bgu_ref[...]
        gate = jnp.minimum(gu[:, :D_EXPERT], SWIGLU_LIMIT)
        up = jnp.clip(gu[:, D_EXPERT:], -SWIGLU_LIMIT, SWIGLU_LIMIT)
        glu = gate * (1.0 / (1.0 + jnp.exp(-SWIGLU_ALPHA * gate)))
        act = ((up + 1.0) * glu).astype(BF16)
        ys = jnp.dot(act, wd_bf[...], preferred_element_type=F32) + bd_ref[...]
        _store_token_major(ys_ref, ys)

    @pl.when(jnp.logical_not(live))
    def _():
        ys_ref[...] = jnp.zeros_like(ys_ref)


def _ffn(xs, block_e, n_blocks, w_gu, b_gu, w_down, b_down, tmb):
    D = D_MODEL
    P = xs.shape[0] // NCH
    E = w_gu.shape[0]
    blk = lambda i, be, nb: (jnp.minimum(i, nb[0] - 1), 0)
    wsel = lambda i, be, nb: (be[jnp.minimum(i, nb[0] - 1)], 0, 0)
    grid_spec = pltpu.PrefetchScalarGridSpec(
        num_scalar_prefetch=2,
        grid=(P // tmb,),
        in_specs=[pl.BlockSpec((tmb * NCH, LANES), blk),
                  pl.BlockSpec((None, D, 2 * D_EXPERT), wsel),
                  pl.BlockSpec((None, 1, 2 * D_EXPERT), wsel),
                  pl.BlockSpec((None, D_EXPERT, D), wsel),
                  pl.BlockSpec((None, 1, D), wsel)],
        out_specs=pl.BlockSpec((tmb * (D // LANES), LANES), lambda i, be, nb: (i, 0)),
        scratch_shapes=[pltpu.VMEM((D, 2 * D_EXPERT), BF16), pltpu.VMEM((D_EXPERT, D), BF16)],
    )
    return pl.pallas_call(
        _ffn_kernel,
        grid_spec=grid_spec,
        out_shape=jax.ShapeDtypeStruct((P * (D // LANES), LANES), F32),
        compiler_params=_cparams(("arbitrary",)),
        name="ffn",
    )(block_e, n_blocks, xs, w_gu, b_gu.reshape(E, 1, 2 * D_EXPERT), w_down, b_down.reshape(E, 1, D))


def _comb_kernel(x1_ref, gate_ref, g2_ref, fg_ref, dest_hbm, ys_hbm, o_ref, buf_a, buf_b, idx_a, idx_b,
                 sem_ia, sem_ib, sem_ra, sem_rb, *, tm, n_steps):
    i = pl.program_id(0)
    n = tm * TOP_K

    def idx_copy(tile, buf, sem):
        return pltpu.make_async_copy(dest_hbm.at[pl.ds(tile * n, n)], buf, sem)

    nch = x1_ref.shape[1] // LANES

    def row_copy(src, buf, k, t, sem):
        return pltpu.make_async_copy(ys_hbm.at[pl.ds(pl.multiple_of(src * nch, nch), nch)],
                                     buf.at[k, pl.ds(pl.multiple_of(t * nch, nch), nch)], sem)

    def issue_rows(idx, buf, sem):
        def body(t, c):
            for k in range(TOP_K):
                row_copy(idx[t * TOP_K + k], buf, k, t, sem).start(priority=k % 2)
            return c

        lax.fori_loop(0, tm, body, 0)

    def wait_rows(buf, sem):
        pltpu.make_async_copy(buf, buf, sem).wait()

    def finish(buf, lo):
        gates = gate_ref[lo:lo + tm, :]
        y = jnp.zeros((tm, x1_ref.shape[1]), F32)
        for k in range(TOP_K):
            y = y + gates[:, k:k + 1] * _load_token_major(buf.at[k], tm, nch)
        v = x1_ref[lo:lo + tm, :] + g2_ref[...] * y
        o_ref[lo:lo + tm, :] = v * lax.rsqrt(jnp.mean(v * v, axis=-1, keepdims=True) + EPS) * fg_ref[...]

    @pl.when(i == 0)
    def _():
        first = idx_copy(0, idx_a, sem_ia)
        first.start()
        first.wait()
        issue_rows(idx_a, buf_a, sem_ra)
        idx_copy(1, idx_b, sem_ib).start()

    idx_copy(2 * i + 1, idx_b, sem_ib).wait()
    issue_rows(idx_b, buf_b, sem_rb)

    @pl.when(i + 1 < n_steps)
    def _():
        idx_copy(2 * i + 2, idx_a, sem_ia).start()

    wait_rows(buf_a, sem_ra)
    finish(buf_a, 0)

    @pl.when(i + 1 < n_steps)
    def _():
        idx_copy(2 * i + 2, idx_a, sem_ia).wait()
        issue_rows(idx_a, buf_a, sem_ra)
        idx_copy(2 * i + 3, idx_b, sem_ib).start()

    wait_rows(buf_b, sem_rb)
    finish(buf_b, tm)


def _combine(x1, gate_tab, g2, final_g, dest_flat, ys, S, tm):
    N, D = x1.shape
    n_steps = N // (2 * tm)
    per_b = S // (2 * tm)
    return pl.pallas_call(
        functools.partial(_comb_kernel, tm=tm, n_steps=n_steps),
        grid=(n_steps,),
        in_specs=[pl.BlockSpec((2 * tm, D), lambda i: (i, 0)),
                  pl.BlockSpec((2 * tm, LANES), lambda i: (i, 0)),
                  pl.BlockSpec((None, 1, D), lambda i: (i // per_b, 0, 0)),
                  pl.BlockSpec((1, D), lambda i: (0, 0)),
                  pl.BlockSpec(memory_space=pl.ANY),
                  pl.BlockSpec(memory_space=pl.ANY)],
        out_specs=pl.BlockSpec((2 * tm, D), lambda i: (i, 0)),
        out_shape=jax.ShapeDtypeStruct((N, D), F32),
        scratch_shapes=[pltpu.VMEM((TOP_K, tm * (D // LANES), LANES), F32),
                        pltpu.VMEM((TOP_K, tm * (D // LANES), LANES), F32),
                        pltpu.SMEM((tm * TOP_K,), I32), pltpu.SMEM((tm * TOP_K,), I32),
                        pltpu.SemaphoreType.DMA, pltpu.SemaphoreType.DMA, pltpu.SemaphoreType.DMA,
                        pltpu.SemaphoreType.DMA],
        compiler_params=_cparams(("arbitrary",)),
        name="comb",
    )(x1, gate_tab, g2, final_g.reshape(1, D), dest_flat, ys)


def _tile(n, pref):
    t = min(pref, n)
    assert n % t == 0, (n, t)
    return t


def _layer(x, c, ada_w, ada_b, norm1_g, w_in, mix_scale, w_o, norm2_g,
           router_w, router_b, w_gu, b_gu, w_down, b_down, final_g):
    B, S, D = x.shape
    N = B * S
    mod = _mod(c, ada_w, ada_b).reshape(B, 6, 1, D)
    sh1, sc1, g1, sh2, sc2, g2 = (mod[:, j] for j in range(6))

    w_pad = jnp.pad(w_in, ((0, 0), (0, IN_COLS_PAD - IN_COLS))).astype(BF16)
    rq, rk, rv, rg, aqt, ak, avt, iqt, ik, iwt = _inproj(x, norm1_g, sc1, sh1, w_pad, _tile(S, 512))
    ms = mix_scale.reshape(1, RET_W + DSA_W)
    ret = _retention(rq, rk, rv, rg, ms[:, :RET_W])
    att = _dsa(iqt, iwt, aqt, ik, ak, avt, ms[:, RET_W:], _tile(S, 256), 128)

    rw_pad = jnp.pad(router_w, ((0, 0), (0, LANES - N_EXPERTS)))
    rw_hi = rw_pad.astype(BF16)
    rw_pad = jnp.stack([rw_hi, (rw_pad - rw_hi.astype(F32)).astype(BF16)])
    rb_pad = jnp.pad(router_b, (0, LANES - N_EXPERTS)).reshape(1, LANES)
    x1, h2, sel, idx_tab, gate_tab, counts = _oproj(ret, att, x, w_o.astype(BF16), g1, norm2_g, sc2, sh2,
                                                    rw_pad, rb_pad, _tile(S, 512))

    tmb = FFN_ROWS
    n_rows = (N * TOP_K + N_EXPERTS * (tmb - 1)) // tmb * tmb + tmb
    cnt = counts[0, :N_EXPERTS].astype(I32)
    padded = (cnt + tmb - 1) // tmb * tmb
    ends = jnp.cumsum(padded)
    starts = ends - padded
    pstart = jnp.pad(starts.astype(F32), (0, LANES - N_EXPERTS)).reshape(1, LANES)
    n_blocks = (ends[-1] // tmb).reshape(1)
    first_row = jnp.arange(n_rows // tmb, dtype=I32) * tmb
    block_e = jnp.minimum(jnp.sum((ends[None, :] <= first_row[:, None]).astype(I32), axis=1), N_EXPERTS - 1)

    tmd = _tile(N, 256)
    dest_tab = _dest(sel, idx_tab, pstart, tmd)
    dest_flat = dest_tab[:, :TOP_K].reshape(N * TOP_K)
    xs = _dispatch(h2, dest_flat, starts + cnt, n_rows, tmb, tmd)
    ys = _ffn(xs, block_e, n_blocks, w_gu, b_gu, w_down, b_down, tmb)
    out = _combine(x1, gate_tab, g2, final_g, dest_flat, ys, S, _tile(S, 256))
    return out.reshape(B, S, D)


def kernel(x, c, ada_w, ada_b, norm1_g, w_in, mix_scale, w_o, norm2_g, router_w, router_b, w_gu, b_gu,
           w_down, b_down, final_g):
    assert ada_w.shape[0] == 1, "single-layer stack"
    return _layer(x, c, ada_w[0], ada_b[0], norm1_g[0], w_in[0], mix_scale[0], w_o[0], norm2_g[0],
                  router_w[0], router_b[0], w_gu[0], b_gu[0], w_down[0], b_down[0], final_g)
```

```python
import functools

import numpy as np
import jax
import jax.numpy as jnp
from jax import lax
from jax.experimental import pallas as pl
from jax.experimental.pallas import tpu as pltpu

F32 = jnp.float32
BF16 = jnp.bfloat16
I32 = jnp.int32

D_MODEL = 1024
RET_HEADS = 4
RET_DK = 64
RET_DV = 128
RET_CHUNK = 128
DSA_HEADS = 8
DSA_KV_HEADS = 2
DSA_HD = 64
IDX_HEADS = 8
IDX_HD = 64
TOPK_MAX = 256
N_EXPERTS = 32
TOP_K = 4
D_EXPERT = D_MODEL
SWIGLU_LIMIT = 7.0
SWIGLU_ALPHA = 1.702
EPS = 1e-6

RET_W = RET_HEADS * RET_DV
DSA_W = DSA_HEADS * DSA_HD
IN_COLS = 2888
IN_COLS_PAD = 2944

KAUG = 128
VAUG = 80
ALIBI_SPLIT = 64
FFN_ROWS = 512
BISECT_VALUE_STEPS = 8
BISECT_MAX_STEPS = 64

LANES = 128
NCH = D_MODEL // LANES
VMEM_LIMIT = 56 * 1024 * 1024
NEG_BIG = -1e30
F32_LOWEST = float(np.finfo(np.float32).min)


def _cparams(sem):
    return pltpu.CompilerParams(dimension_semantics=sem, vmem_limit_bytes=VMEM_LIMIT)


def _mod_kernel(c_ref, w_ref, b_ref, o_ref):
    c = c_ref[...]
    s = c * (1.0 / (1.0 + jnp.exp(-c)))
    o_ref[...] = jnp.dot(s, w_ref[...], preferred_element_type=F32,
                         precision=lax.Precision.HIGHEST) + b_ref[...]


def _mod(c, ada_w, ada_b):
    B, D = c.shape
    n_out = ada_w.shape[1]
    rows = 8
    c8 = jnp.zeros((rows, D), F32).at[:B].set(c)
    out = pl.pallas_call(
        _mod_kernel,
        grid=(n_out // D,),
        in_specs=[pl.BlockSpec((rows, D), lambda j: (0, 0)),
                  pl.BlockSpec((D, D), lambda j: (0, j)),
                  pl.BlockSpec((1, D), lambda j: (0, j))],
        out_specs=pl.BlockSpec((rows, D), lambda j: (0, j)),
        out_shape=jax.ShapeDtypeStruct((rows, n_out), F32),
        compiler_params=_cparams(("arbitrary",)),
        name="mod",
    )(c8, ada_w, ada_b.reshape(1, n_out))
    return out[:B]


def _inproj_kernel(x_ref, g_ref, sc_ref, sh_ref, w_ref,
                   rq_ref, rk_ref, rv_ref, rg_ref, aqt_ref, ak_ref, avt_ref, iqt_ref, ik_ref, iwt_ref):
    x = x_ref[...]
    ms = jnp.mean(x * x, axis=-1, keepdims=True)
    y = x * lax.rsqrt(ms + EPS) * g_ref[...]
    hb = (y * (1.0 + sc_ref[...]) + sh_ref[...]).astype(BF16)

    def proj(lo, hi):
        return jnp.dot(hb, w_ref[:, lo:hi], preferred_element_type=F32)

    tm = x.shape[0]
    d = DSA_HD
    rq_ref[...] = proj(0, 256).astype(BF16)
    rk_ref[...] = (proj(256, 512) * (RET_DK ** -0.5)).astype(BF16)
    rv_ref[...] = proj(512, 1024).astype(BF16)
    rg_ref[...] = proj(1024, 1536).astype(BF16)
    aqt_ref[...] = (proj(1536, 2048) * (d ** -0.5)).T.astype(BF16)
    kk = proj(2048, 2176)
    pos = pl.program_id(1) * tm + lax.broadcasted_iota(I32, (tm, d), 0)
    col = lax.broadcasted_iota(I32, (tm, d), 1)
    posblk = jnp.where(col == 0, pos // ALIBI_SPLIT, jnp.where(col == 1, pos % ALIBI_SPLIT, 0)).astype(F32)
    for g in range(DSA_KV_HEADS):
        ak_ref[:, g * KAUG:g * KAUG + d] = kk[:, g * d:(g + 1) * d].astype(BF16)
        ak_ref[:, g * KAUG + d:(g + 1) * KAUG] = posblk.astype(BF16)
    vt = proj(2176, 2304).T
    r16 = lax.broadcasted_iota(I32, (VAUG - d, tm), 0)
    onesblk = jnp.where(r16 == 0, 1.0, 0.0).astype(BF16)
    for g in range(DSA_KV_HEADS):
        avt_ref[g * VAUG:g * VAUG + d, :] = vt[g * d:(g + 1) * d, :].astype(BF16)
        avt_ref[g * VAUG + d:(g + 1) * VAUG, :] = onesblk
    iqt_ref[...] = proj(2304, 2816).T.astype(BF16)
    last = proj(2816, 2944)
    ik_ref[...] = last[:, :IDX_HD].astype(BF16)
    iwt_ref[...] = last.T[IDX_HD:IDX_HD + IDX_HEADS, :] * ((IDX_HD ** -0.5) * (IDX_HEADS ** -0.5))


def _inproj(x, norm_g, sc, sh, w_pad, tm):
    B, S, D = x.shape
    row = lambda w: pl.BlockSpec((None, tm, w), lambda b, i: (b, i, 0))
    colT = lambda h: pl.BlockSpec((None, h, tm), lambda b, i: (b, 0, i))
    vec = pl.BlockSpec((None, 1, D), lambda b, i: (b, 0, 0))
    sd = lambda shape, dt: jax.ShapeDtypeStruct(shape, dt)
    G = DSA_KV_HEADS
    return pl.pallas_call(
        _inproj_kernel,
        grid=(B, S // tm),
        in_specs=[row(D), pl.BlockSpec((1, D), lambda b, i: (0, 0)), vec, vec,
                  pl.BlockSpec((D, IN_COLS_PAD), lambda b, i: (0, 0))],
        out_specs=[row(256), row(256), row(512), row(512), colT(DSA_W), row(G * KAUG), colT(G * VAUG),
                   colT(IDX_HEADS * IDX_HD), row(IDX_HD), colT(IDX_HEADS)],
        out_shape=[sd((B, S, 256), BF16), sd((B, S, 256), BF16), sd((B, S, 512), BF16),
                   sd((B, S, 512), BF16), sd((B, DSA_W, S), BF16), sd((B, S, G * KAUG), BF16),
                   sd((B, G * VAUG, S), BF16), sd((B, IDX_HEADS * IDX_HD, S), BF16),
                   sd((B, S, IDX_HD), BF16), sd((B, IDX_HEADS, S), F32)],
        compiler_params=_cparams(("parallel", "parallel")),
        name="inproj",
    )(x, norm_g.reshape(1, D), sc, sh, w_pad)


def _ret_kernel(rq_ref, rk_ref, rv_ref, rg_ref, din_ref, qd_ref, kd_ref, cd_ref, ms_ref, o_ref, state_ref):
    @pl.when(pl.program_id(1) == 0)
    def _():
        state_ref[...] = jnp.zeros_like(state_ref)

    for h in range(RET_HEADS):
        q = rq_ref[:, h * RET_DK:(h + 1) * RET_DK]
        k = rk_ref[:, h * RET_DK:(h + 1) * RET_DK]
        v = rv_ref[:, h * RET_DV:(h + 1) * RET_DV]
        r_prev = state_ref[h]
        s = lax.dot_general(q, k, (((1,), (1,)), ((), ())), preferred_element_type=F32) * din_ref[h]
        o = jnp.dot(s.astype(BF16), v, preferred_element_type=F32)
        o = o + jnp.dot(q, r_prev.astype(BF16), preferred_element_type=F32) * qd_ref[h]
        vd = (v.astype(F32) * kd_ref[h]).astype(BF16)
        kv = lax.dot_general(k, vd, (((0,), (0,)), ((), ())), preferred_element_type=F32)
        state_ref[h] = r_prev * cd_ref[h] + kv
        o = o * lax.rsqrt(jnp.mean(o * o, axis=-1, keepdims=True) + EPS)
        g = rg_ref[:, h * RET_DV:(h + 1) * RET_DV].astype(F32)
        gate = g * (1.0 / (1.0 + jnp.exp(-g)))
        o_ref[:, h * RET_DV:(h + 1) * RET_DV] = (gate * o * ms_ref[:, h * RET_DV:(h + 1) * RET_DV]).astype(BF16)


def _ret_consts(C):
    H = RET_HEADS
    log_g = np.log1p(-np.exp2(-5.0 - np.arange(H, dtype=np.float64)))
    pos = np.arange(C, dtype=np.float64)
    diff = pos[:, None] - pos[None, :]
    d_inner = np.where(diff[None] >= 0, np.exp(np.maximum(diff, 0.0)[None] * log_g[:, None, None]), 0.0)
    q_decay = np.exp((pos + 1.0)[None] * log_g[:, None])
    k_decay = np.exp((C - 1.0 - pos)[None] * log_g[:, None])
    chunk_decay = np.exp(C * log_g)
    qd = np.broadcast_to(q_decay[:, :, None], (H, C, RET_DV))
    kd = np.broadcast_to(k_decay[:, :, None], (H, C, RET_DV))
    cd = np.broadcast_to(chunk_decay[:, None, None], (H, 1, RET_DV))
    f = lambda a: jnp.asarray(np.ascontiguousarray(a), F32)
    return f(d_inner), f(qd), f(kd), f(cd)


def _retention(rq, rk, rv, rg, ms_ret):
    B, S, _ = rq.shape
    C = min(RET_CHUNK, S)
    din, qd, kd, cd = _ret_consts(C)
    row = lambda w: pl.BlockSpec((None, C, w), lambda b, n: (b, n, 0))
    full = lambda a: pl.BlockSpec(a.shape, lambda b, n: (0,) * a.ndim)
    return pl.pallas_call(
        _ret_kernel,
        grid=(B, S // C),
        in_specs=[row(256), row(256), row(512), row(512), full(din), full(qd), full(kd), full(cd),
                  pl.BlockSpec((1, RET_W), lambda b, n: (0, 0))],
        out_specs=row(RET_W),
        out_shape=jax.ShapeDtypeStruct((B, S, RET_W), BF16),
        scratch_shapes=[pltpu.VMEM((RET_HEADS, RET_DK, RET_DV), F32)],
        compiler_params=_cparams(("parallel", "arbitrary")),
        name="ret",
    )(rq, rk, rv, rg, din, qd, kd, cd, ms_ret)


def _f32_key(x):
    i = lax.bitcast_convert_type(x, I32)
    return i ^ ((i >> 31) & 0x7FFFFFFF)


def _key_f32(k):
    return lax.bitcast_convert_type(k ^ ((k >> 31) & 0x7FFFFFFF), F32)


def _dsa_kernel(iqt_ref, iwt_ref, aqt_ref, ik_ref, ak_ref, avt_ref, ms_ref, o_ref, score_ref, qa_ref, sa_ref, sb_ref, mask_ref, *acc_refs,
                tq, tks, n_sel):
    H, G, R, d = DSA_HEADS, DSA_KV_HEADS, DSA_HEADS // DSA_KV_HEADS, DSA_HD
    t0 = pl.program_id(1) * tq
    nsub = (t0 + tq) // tks
    tka = 2 * tks
    npair = ((t0 + tq) // tka + 1) // 2
    kf = float(n_sel)
    qpos = t0 + lax.broadcasted_iota(I32, (1, tq), 1)
    krow = lax.broadcasted_iota(I32, (tks, tq), 0)

    wrow = [iwt_ref[h:h + 1, :] for h in range(IDX_HEADS)]

    def fold8(x, op):
        acc = x[0:8, :]
        for i in range(1, tks // 8):
            acc = op(acc, x[8 * i:8 * (i + 1), :])
        return acc

    def score_pair(i, carry):
        mx, mn, npos, nnon = carry
        for u in range(2 * tka // tks):
            r0 = pl.multiple_of(i * 2 * tka + u * tks, tks)
            kc = ik_ref[pl.ds(r0, tks), :]
            acc = jnp.zeros((tks, tq), F32)
            for h in range(IDX_HEADS):
                rel = jnp.dot(kc, iqt_ref[h * IDX_HD:(h + 1) * IDX_HD, :], preferred_element_type=F32)
                acc = acc + jnp.maximum(rel, 0.0) * wrow[h]
            causal = r0 + krow <= qpos
            sc = jnp.where(causal, acc, -jnp.inf)
            score_ref[pl.ds(r0, tks), :] = sc
            mx = jnp.maximum(mx, fold8(sc, jnp.maximum))
            mn = jnp.minimum(mn, fold8(jnp.where(causal, acc, jnp.inf), jnp.minimum))
            npos = npos + fold8(jnp.where(sc > 0.0, 1.0, 0.0), jnp.add)
            nnon = nnon + fold8(jnp.where(sc >= 0.0, 1.0, 0.0), jnp.add)
        return mx, mn, npos, nnon

    stat0 = (jnp.full((8, tq), -jnp.inf, F32), jnp.full((8, tq), jnp.inf, F32),
             jnp.zeros((8, tq), F32), jnp.zeros((8, tq), F32))
    mx, mn, npos, nnon = lax.fori_loop(0, npair, score_pair, stat0)
    top = jnp.max(mx, axis=0, keepdims=True)
    lo0 = jnp.min(mn, axis=0, keepdims=True)
    n_pos = jnp.sum(npos, axis=0, keepdims=True)
    n_nonneg = jnp.sum(nnon, axis=0, keepdims=True)


    def count(th, strict):
        def body(j, acc):
            for u in range(2):
                s = score_ref[pl.ds(pl.multiple_of((2 * j + u) * tks, tks), tks), :]
                hit = (s > th) if strict else (s >= th)
                acc = acc + fold8(jnp.where(hit, 1.0, 0.0), jnp.add)
            return acc

        acc = lax.fori_loop(0, nsub // 2, body, jnp.zeros((8, tq), F32))
        return jnp.sum(acc, axis=0, keepdims=True)

    def probe(lo, hi, it):
        lk, hk = _f32_key(lo), _f32_key(hi)
        mk = (lk >> 1) + (hk >> 1) + (lk & hk & 1)
        mv = lo + (hi - lo) * 0.5
        early = (jnp.zeros((1, tq), I32) + it) < BISECT_VALUE_STEPS
        mid = jnp.where(early & (mv > lo) & (mv < hi), mv, _key_f32(mk))
        return mid, jnp.max(jnp.where(mk != lk, 1.0, 0.0))

    def bis_cond(c):
        return (c[5] > 0.0) & (c[6] < BISECT_MAX_STEPS)

    def bis_body(c):
        lo, hi, c_lo, c_hi, mid, _, it = c
        cnt = count(mid, False)
        ge = cnt >= kf
        up = ge | (cnt == kf)
        dn = (~ge) | (cnt == kf)
        lo, c_lo = jnp.where(up, mid, lo), jnp.where(up, cnt, c_lo)
        hi, c_hi = jnp.where(dn, mid, hi), jnp.where(dn, cnt, c_hi)
        mid, active = probe(lo, hi, it + 1)
        return lo, hi, c_lo, c_hi, mid, active, it + 1

    zero = jnp.zeros((1, tq), F32)
    keep_all = qpos + 1 <= n_sel
    settled = keep_all | ((n_nonneg >= kf) & (n_pos < kf))
    above = n_pos >= kf
    c_lo0 = jnp.where(settled | above, n_nonneg, (qpos + 1).astype(F32))
    c_hi0 = jnp.where(settled | ~above, n_nonneg, zero)
    lo0 = jnp.where(settled | above, zero, lo0)
    hi0 = jnp.where(settled | ~above, zero, _key_f32(_f32_key(top) + 1))
    mid0, active0 = probe(lo0, hi0, jnp.int32(0))
    lo, hi, c_lo, c_hi, _, _, _ = lax.while_loop(bis_cond, bis_body,
                                                 (lo0, hi0, c_lo0, c_hi0, mid0, active0, jnp.int32(0)))
    at_hi = c_hi >= kf
    thr = jnp.where(keep_all, F32_LOWEST, jnp.where(at_hi, hi, lo))
    excess = jnp.where(keep_all, 0.0, jnp.where(at_hi, c_hi, c_lo) - kf)

    @pl.when(jnp.max(excess) > 0.0)
    def _():
        budget = kf - count(thr, True)
        earlier = lax.broadcasted_iota(I32, (tks, tks), 1) < lax.broadcasted_iota(I32, (tks, tks), 0)
        earlier = jnp.where(earlier, 1.0, 0.0).astype(BF16)

        def fix(j, seen):
            r0 = pl.multiple_of(j * tks, tks)
            s = score_ref[pl.ds(r0, tks), :]
            eq = s == thr
            eqf = jnp.where(eq, 1.0, 0.0)
            rank = jnp.dot(earlier, eqf.astype(BF16), preferred_element_type=F32) + seen
            score_ref[pl.ds(r0, tks), :] = jnp.where(eq & (rank >= budget), -jnp.inf, s)
            return seen + jnp.sum(eqf, axis=0, keepdims=True)

        lax.fori_loop(0, nsub, fix, jnp.zeros((1, tq), F32))

    arow = lax.broadcasted_iota(I32, (KAUG - d, tq), 0)
    for h in range(H):
        slope = float(2.0 ** (-8.0 * (h + 1) / H))
        qa_ref[h, 0:d, :] = aqt_ref[h * d:(h + 1) * d, :]
        qa_ref[h, d:KAUG, :] = jnp.where(arow == 0, slope * ALIBI_SPLIT, jnp.where(arow == 1, slope, 0.0)).astype(BF16)
    for acc in acc_refs:
        acc[...] = jnp.zeros_like(acc)

    def logits(j, h):
        ka = ak_ref[pl.ds(pl.multiple_of(j * tka, tka), tka), (h // R) * KAUG:(h // R + 1) * KAUG]
        return jnp.dot(ka, qa_ref[h], preferred_element_type=F32)

    def step(j, j_next, cur_ref, next_ref, ms):
        r0 = pl.multiple_of(j * tka, tka)
        mask_ref[...] = jnp.where(score_ref[pl.ds(r0, tka), :] >= thr, 0.0, NEG_BIG)
        nms = []
        for h in range(H):
            g = h // R
            next_ref[h] = logits(j_next, h)
            s = cur_ref[h] + mask_ref[...]
            m_new = jnp.maximum(ms[h], jnp.max(s, axis=0, keepdims=True))
            p = jnp.exp(s - m_new).astype(BF16)
            va = avt_ref[g * VAUG:(g + 1) * VAUG, pl.ds(r0, tka)]
            acc = acc_refs[h]
            acc[...] = acc[...] * jnp.exp(ms[h] - m_new) + jnp.dot(va, p, preferred_element_type=F32)
            nms.append(m_new)
        return tuple(nms)

    for h in range(H):
        sa_ref[h] = logits(0, h)

    def att_pair(i, ms):
        ms = step(2 * i, 2 * i + 1, sa_ref, sb_ref, ms)
        return step(2 * i + 1, jnp.minimum(2 * i + 2, 2 * npair - 1), sb_ref, sa_ref, ms)

    lax.fori_loop(0, npair, att_pair, tuple(jnp.full((1, tq), NEG_BIG, F32) for _ in range(H)))
    for h in range(H):
        a = acc_refs[h][...]
        o = a[0:d, :] / a[d:d + 1, :]
        o_ref[h * d:(h + 1) * d, :] = (o * ms_ref[h * d:(h + 1) * d, :]).astype(BF16)


def _dsa(iqt, iwt, aqt, ik, ak, avt, ms_att, tq, tks):
    B, _, S = iqt.shape
    assert S <= ALIBI_SPLIT * 256 and tq % (2 * tks) == 0
    assert (S // (2 * tks)) % 2 == 0
    n_sel = min(TOPK_MAX, S // 4)
    G = DSA_KV_HEADS
    colT = lambda h: pl.BlockSpec((None, h, tq), lambda b, i: (b, 0, i))
    msb = jnp.broadcast_to(ms_att.reshape(DSA_W, 1), (DSA_W, tq))
    return pl.pallas_call(
        functools.partial(_dsa_kernel, tq=tq, tks=tks, n_sel=n_sel),
        grid=(B, S // tq),
        in_specs=[colT(IDX_HEADS * IDX_HD), colT(IDX_HEADS), colT(DSA_W),
                  pl.BlockSpec((None, S, IDX_HD), lambda b, i: (b, 0, 0)),
                  pl.BlockSpec((None, S, G * KAUG), lambda b, i: (b, 0, 0)),
                  pl.BlockSpec((None, G * VAUG, S), lambda b, i: (b, 0, 0)),
                  pl.BlockSpec((DSA_W, tq), lambda b, i: (0, 0))],
        out_specs=colT(DSA_W),
        out_shape=jax.ShapeDtypeStruct((B, DSA_W, S), BF16),
        scratch_shapes=[pltpu.VMEM((S, tq), F32), pltpu.VMEM((DSA_HEADS, KAUG, tq), BF16),
                        pltpu.VMEM((DSA_HEADS, 2 * tks, tq), F32), pltpu.VMEM((DSA_HEADS, 2 * tks, tq), F32),
                        pltpu.VMEM((2 * tks, tq), F32)]
        + [pltpu.VMEM((VAUG, tq), F32) for _ in range(DSA_HEADS)],
        compiler_params=_cparams(("parallel", "arbitrary")),
        name="dsa",
    )(iqt, iwt, aqt, ik, ak, avt, msb)


def _oproj_kernel(ret_ref, att_ref, x_ref, wo_ref, g1_ref, n2_ref, sc_ref, sh_ref, rw_ref, rb_ref,
                  x1_ref, h2_ref, sel_ref, idx_ref, gate_ref, cnt_ref):
    mixo = jnp.dot(ret_ref[...], wo_ref[:RET_W, :], preferred_element_type=F32)
    mixo = mixo + lax.dot_general(att_ref[...], wo_ref[RET_W:, :], (((0,), (0,)), ((), ())),
                                  preferred_element_type=F32)
    x1 = x_ref[...] + g1_ref[...] * mixo
    x1_ref[...] = x1
    y = x1 * lax.rsqrt(jnp.mean(x1 * x1, axis=-1, keepdims=True) + EPS) * n2_ref[...]
    h2 = y * (1.0 + sc_ref[...]) + sh_ref[...]
    _store_token_major(h2_ref, h2)
    h_hi = h2.astype(BF16)
    h_lo = (h2 - h_hi.astype(F32)).astype(BF16)
    logits = jnp.dot(h_hi, rw_ref[0], preferred_element_type=F32)
    logits = logits + (jnp.dot(h_hi, rw_ref[1], preferred_element_type=F32)
                       + jnp.dot(h_lo, rw_ref[0], preferred_element_type=F32)) + rb_ref[...]
    tm = logits.shape[0]
    lane = lax.broadcasted_iota(I32, (tm, LANES), 1).astype(F32)
    work = jnp.where(lane < N_EXPERTS, logits, -jnp.inf)
    sel = jnp.zeros((tm, LANES), F32)
    idx_tab = jnp.zeros((tm, LANES), F32)
    vals = []
    for k in range(TOP_K):
        m = jnp.max(work, axis=1, keepdims=True)
        idx = jnp.min(jnp.where(work == m, lane, float(LANES)), axis=1, keepdims=True)
        hit = lane == idx
        sel = jnp.where(hit, 1.0, sel)
        idx_tab = jnp.where(lane == k, idx, idx_tab)
        work = jnp.where(hit, -jnp.inf, work)
        vals.append(m)
    es = [jnp.exp(v - vals[0]) for v in vals]
    den = es[0] + es[1] + es[2] + es[3]
    gate_tab = jnp.zeros((tm, LANES), F32)
    for k in range(TOP_K):
        gate_tab = jnp.where(lane == k, es[k] / den, gate_tab)
    sel_ref[...] = sel
    idx_ref[...] = idx_tab
    gate_ref[...] = gate_tab

    @pl.when((pl.program_id(0) == 0) & (pl.program_id(1) == 0))
    def _():
        cnt_ref[...] = jnp.zeros_like(cnt_ref)

    cnt_ref[...] += jnp.sum(sel, axis=0, keepdims=True)


def _oproj(ret, att, x, wo, g1, n2g, sc2, sh2, rw_pad, rb_pad, tm):
    B, S, D = x.shape
    nt = S // tm
    row = lambda w: pl.BlockSpec((None, tm, w), lambda b, i: (b, i, 0))
    flat = lambda w: pl.BlockSpec((tm, w), lambda b, i: (b * nt + i, 0))
    vec = pl.BlockSpec((None, 1, D), lambda b, i: (b, 0, 0))
    cst = lambda shape: pl.BlockSpec(shape, lambda b, i: (0, 0))
    sd = lambda shape, dt: jax.ShapeDtypeStruct(shape, dt)
    N = B * S
    return pl.pallas_call(
        _oproj_kernel,
        grid=(B, nt),
        in_specs=[row(RET_W), pl.BlockSpec((None, DSA_W, tm), lambda b, i: (b, 0, i)), row(D), cst((D, D)), vec,
                  cst((1, D)), vec, vec,
                  pl.BlockSpec((2, D, LANES), lambda b, i: (0, 0, 0)), cst((1, LANES))],
        out_specs=[flat(D), pl.BlockSpec((tm * (D // LANES), LANES), lambda b, i: (b * nt + i, 0)),
                   flat(LANES), flat(LANES), flat(LANES), cst((1, LANES))],
        out_shape=[sd((N, D), F32), sd((N * (D // LANES), LANES), F32), sd((N, LANES), F32), sd((N, LANES), F32),
                   sd((N, LANES), F32), sd((1, LANES), F32)],
        compiler_params=_cparams(("arbitrary", "arbitrary")),
        name="oproj",
    )(ret, att, x, wo, g1, n2g.reshape(1, D), sc2, sh2, rw_pad, rb_pad)


def _dest_kernel(sel_ref, idx_ref, pstart_ref, dest_ref, seen_ref):
    @pl.when(pl.program_id(0) == 0)
    def _():
        seen_ref[...] = jnp.zeros_like(seen_ref)

    sel = sel_ref[...]
    tm = sel.shape[0]
    earlier = lax.broadcasted_iota(I32, (tm, tm), 1) < lax.broadcasted_iota(I32, (tm, tm), 0)
    earlier = jnp.where(earlier, 1.0, 0.0).astype(BF16)
    rank = jnp.dot(earlier, sel.astype(BF16), preferred_element_type=F32) + seen_ref[...]
    dest = pstart_ref[...] + rank
    lane = lax.broadcasted_iota(I32, (tm, LANES), 1).astype(F32)
    idx_tab = idx_ref[...]
    out = jnp.zeros((tm, LANES), F32)
    for k in range(TOP_K):
        e_k = jnp.sum(jnp.where(lane == k, idx_tab, 0.0), axis=1, keepdims=True)
        d_k = jnp.sum(jnp.where(lane == e_k, dest, 0.0), axis=1, keepdims=True)
        out = jnp.where(lane == k, d_k, out)
    dest_ref[...] = out.astype(I32)
    seen_ref[...] += jnp.sum(sel, axis=0, keepdims=True)


def _dest(sel, idx_tab, pstart, tm):
    N = sel.shape[0]
    blk = pl.BlockSpec((tm, LANES), lambda i: (i, 0))
    return pl.pallas_call(
        _dest_kernel,
        grid=(N // tm,),
        in_specs=[blk, blk, pl.BlockSpec((1, LANES), lambda i: (0, 0))],
        out_specs=blk,
        out_shape=jax.ShapeDtypeStruct((N, LANES), I32),
        scratch_shapes=[pltpu.VMEM((1, LANES), F32)],
        compiler_params=_cparams(("arbitrary",)),
        name="dest",
    )(sel, idx_tab, pstart)


def _disp_kernel(zs_ref, h2_ref, dest_hbm, xs_hbm, zbuf, idx_a, idx_b, sem_ia, sem_ib, sem_row, sem_z,
                 *, tm, nch, zrows):
    i = pl.program_id(0)
    n = tm * TOP_K

    def idx_copy(tile, buf, sem):
        return pltpu.make_async_copy(dest_hbm.at[pl.ds(tile * n, n)], buf, sem)

    def rows(ref, first, count):
        return ref.at[pl.ds(pl.multiple_of(first * nch, nch), count * nch)]

    def row_copy(row, dst):
        return pltpu.make_async_copy(rows(h2_ref, row, 1), rows(xs_hbm, dst, 1), sem_row)

    def issue_rows(first_row, idx):
        def body(t, c):
            for k in range(TOP_K):
                row_copy(first_row + t, idx[t * TOP_K + k]).start(priority=k % 2)
            return c

        lax.fori_loop(0, tm, body, 0)

    def wait_rows():
        pltpu.make_async_copy(rows(xs_hbm, 0, n), rows(xs_hbm, 0, n), sem_row).wait()

    @pl.when(i == 0)
    def _():
        zbuf[...] = jnp.zeros_like(zbuf)
        for e in range(N_EXPERTS):
            fill = pltpu.make_async_copy(zbuf, rows(xs_hbm, zs_ref[e], zrows), sem_z)
            fill.start()
            fill.wait()

    idx_copy(2 * i, idx_a, sem_ia).start()
    idx_copy(2 * i + 1, idx_b, sem_ib).start()
    idx_copy(2 * i, idx_a, sem_ia).wait()
    issue_rows(0, idx_a)
    idx_copy(2 * i + 1, idx_b, sem_ib).wait()
    issue_rows(tm, idx_b)
    wait_rows()
    wait_rows()


def _dispatch(h2, dest_flat, zero_start, n_rows, zrows, tm):
    nch = D_MODEL // LANES
    n_steps = h2.shape[0] // nch // (2 * tm)
    grid_spec = pltpu.PrefetchScalarGridSpec(
        num_scalar_prefetch=1,
        grid=(n_steps,),
        in_specs=[pl.BlockSpec((2 * tm * nch, LANES), lambda i, zs: (i, 0)), pl.BlockSpec(memory_space=pl.ANY)],
        out_specs=pl.BlockSpec(memory_space=pl.ANY),
        scratch_shapes=[pltpu.VMEM((zrows * nch, LANES), F32), pltpu.SMEM((tm * TOP_K,), I32),
                        pltpu.SMEM((tm * TOP_K,), I32), pltpu.SemaphoreType.DMA, pltpu.SemaphoreType.DMA,
                        pltpu.SemaphoreType.DMA, pltpu.SemaphoreType.DMA],
    )
    return pl.pallas_call(
        functools.partial(_disp_kernel, tm=tm, nch=nch, zrows=zrows),
        grid_spec=grid_spec,
        out_shape=jax.ShapeDtypeStruct((n_rows * nch, LANES), F32),
        compiler_params=_cparams(("arbitrary",)),
        name="disp",
    )(zero_start, h2, dest_flat)


def _store_token_major(ref, x):
    rows, d = x.shape
    nch = d // LANES
    for j in range(nch):
        ref[pl.ds(j, rows, stride=nch), :] = x[:, j * LANES:(j + 1) * LANES]


def _load_token_major(ref, rows, nch):
    return jnp.concatenate([ref[pl.ds(j, rows, stride=nch), :] for j in range(nch)], axis=1)


def _ffn_kernel(be_ref, nb_ref, xs_ref, wgu_ref, bgu_ref, wd_ref, bd_ref, ys_ref, wgu_bf, wd_bf):
    i = pl.program_id(0)
    live = i < nb_ref[0]

    @pl.when(live & ((i == 0) | (be_ref[i] != be_ref[jnp.maximum(i - 1, 0)])))
    def _():
        wgu_bf[...] = wgu_ref[...].astype(BF16)
        wd_bf[...] = wd_ref[...].astype(BF16)

    @pl.when(live)
    def _():
        xb = _load_token_major(xs_ref, ys_ref.shape[0] // NCH, NCH).astype(BF16)
        gu = jnp.dot(xb, wgu_bf[...], preferred_element_type=F32) + bgu_ref[...]
        gate = jnp.minimum(gu[:, :D_EXPERT], SWIGLU_LIMIT)
        up = jnp.clip(gu[:, D_EXPERT:], -SWIGLU_LIMIT, SWIGLU_LIMIT)
        glu = gate * (1.0 / (1.0 + jnp.exp(-SWIGLU_ALPHA * gate)))
        act = ((up + 1.0) * glu).astype(BF16)
        ys = jnp.dot(act, wd_bf[...], preferred_element_type=F32) + bd_ref[...]
        _store_token_major(ys_ref, ys)

    @pl.when(jnp.logical_not(live))
    def _():
        ys_ref[...] = jnp.zeros_like(ys_ref)


def _ffn(xs, block_e, n_blocks, w_gu, b_gu, w_down, b_down, tmb):
    D = D_MODEL
    P = xs.shape[0] // NCH
    E = w_gu.shape[0]
    blk = lambda i, be, nb: (jnp.minimum(i, nb[0] - 1), 0)
    wsel = lambda i, be, nb: (be[jnp.minimum(i, nb[0] - 1)], 0, 0)
    grid_spec = pltpu.PrefetchScalarGridSpec(
        num_scalar_prefetch=2,
        grid=(P // tmb,),
        in_specs=[pl.BlockSpec((tmb * NCH, LANES), blk),
                  pl.BlockSpec((None, D, 2 * D_EXPERT), wsel),
                  pl.BlockSpec((None, 1, 2 * D_EXPERT), wsel),
                  pl.BlockSpec((None, D_EXPERT, D), wsel),
                  pl.BlockSpec((None, 1, D), wsel)],
        out_specs=pl.BlockSpec((tmb * (D // LANES), LANES), lambda i, be, nb: (i, 0)),
        scratch_shapes=[pltpu.VMEM((D, 2 * D_EXPERT), BF16), pltpu.VMEM((D_EXPERT, D), BF16)],
    )
    return pl.pallas_call(
        _ffn_kernel,
        grid_spec=grid_spec,
        out_shape=jax.ShapeDtypeStruct((P * (D // LANES), LANES), F32),
        compiler_params=_cparams(("arbitrary",)),
        name="ffn",
    )(block_e, n_blocks, xs, w_gu, b_gu.reshape(E, 1, 2 * D_EXPERT), w_down, b_down.reshape(E, 1, D))


def _comb_kernel(x1_ref, gate_ref, g2_ref, fg_ref, dest_hbm, ys_hbm, o_ref, buf_a, buf_b, idx_a, idx_b,
                 sem_ia, sem_ib, sem_ra, sem_rb, *, tm, n_steps):
    i = pl.program_id(0)
    n = tm * TOP_K

    def idx_copy(tile, buf, sem):
        return pltpu.make_async_copy(dest_hbm.at[pl.ds(tile * n, n)], buf, sem)

    nch = x1_ref.shape[1] // LANES

    def row_copy(src, buf, k, t, sem):
        return pltpu.make_async_copy(ys_hbm.at[pl.ds(pl.multiple_of(src * nch, nch), nch)],
                                     buf.at[k, pl.ds(pl.multiple_of(t * nch, nch), nch)], sem)

    def issue_rows(idx, buf, sem):
        def body(t, c):
            for k in range(TOP_K):
                row_copy(idx[t * TOP_K + k], buf, k, t, sem).start(priority=k % 2)
            return c

        lax.fori_loop(0, tm, body, 0)

    def wait_rows(buf, sem):
        pltpu.make_async_copy(buf, buf, sem).wait()

    def finish(buf, lo):
        gates = gate_ref[lo:lo + tm, :]
        y = jnp.zeros((tm, x1_ref.shape[1]), F32)
        for k in range(TOP_K):
            y = y + gates[:, k:k + 1] * _load_token_major(buf.at[k], tm, nch)
        v = x1_ref[lo:lo + tm, :] + g2_ref[...] * y
        o_ref[lo:lo + tm, :] = v * lax.rsqrt(jnp.mean(v * v, axis=-1, keepdims=True) + EPS) * fg_ref[...]

    @pl.when(i == 0)
    def _():
        first = idx_copy(0, idx_a, sem_ia)
        first.start()
        first.wait()
        issue_rows(idx_a, buf_a, sem_ra)
        idx_copy(1, idx_b, sem_ib).start()

    idx_copy(2 * i + 1, idx_b, sem_ib).wait()
    issue_rows(idx_b, buf_b, sem_rb)

    @pl.when(i + 1 < n_steps)
    def _():
        idx_copy(2 * i + 2, idx_a, sem_ia).start()

    wait_rows(buf_a, sem_ra)
    finish(buf_a, 0)

    @pl.when(i + 1 < n_steps)
    def _():
        idx_copy(2 * i + 2, idx_a, sem_ia).wait()
        issue_rows(idx_a, buf_a, sem_ra)
        idx_copy(2 * i + 3, idx_b, sem_ib).start()

    wait_rows(buf_b, sem_rb)
    finish(buf_b, tm)


def _combine(x1, gate_tab, g2, final_g, dest_flat, ys, S, tm):
    N, D = x1.shape
    n_steps = N // (2 * tm)
    per_b = S // (2 * tm)
    return pl.pallas_call(
        functools.partial(_comb_kernel, tm=tm, n_steps=n_steps),
        grid=(n_steps,),
        in_specs=[pl.BlockSpec((2 * tm, D), lambda i: (i, 0)),
                  pl.BlockSpec((2 * tm, LANES), lambda i: (i, 0)),
                  pl.BlockSpec((None, 1, D), lambda i: (i // per_b, 0, 0)),
                  pl.BlockSpec((1, D), lambda i: (0, 0)),
                  pl.BlockSpec(memory_space=pl.ANY),
                  pl.BlockSpec(memory_space=pl.ANY)],
        out_specs=pl.BlockSpec((2 * tm, D), lambda i: (i, 0)),
        out_shape=jax.ShapeDtypeStruct((N, D), F32),
        scratch_shapes=[pltpu.VMEM((TOP_K, tm * (D // LANES), LANES), F32),
                        pltpu.VMEM((TOP_K, tm * (D // LANES), LANES), F32),
                        pltpu.SMEM((tm * TOP_K,), I32), pltpu.SMEM((tm * TOP_K,), I32),
                        pltpu.SemaphoreType.DMA, pltpu.SemaphoreType.DMA, pltpu.SemaphoreType.DMA,
                        pltpu.SemaphoreType.DMA],
        compiler_params=_cparams(("arbitrary",)),
        name="comb",
    )(x1, gate_tab, g2, final_g.reshape(1, D), dest_flat, ys)


def _tile(n, pref):
    t = min(pref, n)
    assert n % t == 0, (n, t)
    return t


def _layer(x, c, ada_w, ada_b, norm1_g, w_in, mix_scale, w_o, norm2_g,
           router_w, router_b, w_gu, b_gu, w_down, b_down, final_g):
    B, S, D = x.shape
    N = B * S
    mod = _mod(c, ada_w, ada_b).reshape(B, 6, 1, D)
    sh1, sc1, g1, sh2, sc2, g2 = (mod[:, j] for j in range(6))

    w_pad = jnp.pad(w_in, ((0, 0), (0, IN_COLS_PAD - IN_COLS))).astype(BF16)
    rq, rk, rv, rg, aqt, ak, avt, iqt, ik, iwt = _inproj(x, norm1_g, sc1, sh1, w_pad, _tile(S, 512))
    ms = mix_scale.reshape(1, RET_W + DSA_W)
    ret = _retention(rq, rk, rv, rg, ms[:, :RET_W])
    att = _dsa(iqt, iwt, aqt, ik, ak, avt, ms[:, RET_W:], _tile(S, 512), 128)

    rw_pad = jnp.pad(router_w, ((0, 0), (0, LANES - N_EXPERTS)))
    rw_hi = rw_pad.astype(BF16)
    rw_pad = jnp.stack([rw_hi, (rw_pad - rw_hi.astype(F32)).astype(BF16)])
    rb_pad = jnp.pad(router_b, (0, LANES - N_EXPERTS)).reshape(1, LANES)
    x1, h2, sel, idx_tab, gate_tab, counts = _oproj(ret, att, x, w_o.astype(BF16), g1, norm2_g, sc2, sh2,
                                                    rw_pad, rb_pad, _tile(S, 512))

    tmb = FFN_ROWS
    n_rows = (N * TOP_K + N_EXPERTS * (tmb - 1)) // tmb * tmb + tmb
    cnt = counts[0, :N_EXPERTS].astype(I32)
    padded = (cnt + tmb - 1) // tmb * tmb
    ends = jnp.cumsum(padded)
    starts = ends - padded
    pstart = jnp.pad(starts.astype(F32), (0, LANES - N_EXPERTS)).reshape(1, LANES)
    n_blocks = (ends[-1] // tmb).reshape(1)
    first_row = jnp.arange(n_rows // tmb, dtype=I32) * tmb
    block_e = jnp.minimum(jnp.sum((ends[None, :] <= first_row[:, None]).astype(I32), axis=1), N_EXPERTS - 1)

    tmd = _tile(N, 256)
    dest_tab = _dest(sel, idx_tab, pstart, tmd)
    dest_flat = dest_tab[:, :TOP_K].reshape(N * TOP_K)
    xs = _dispatch(h2, dest_flat, starts + cnt, n_rows, tmb, tmd)
    ys = _ffn(xs, block_e, n_blocks, w_gu, b_gu, w_down, b_down, tmb)
    out = _combine(x1, gate_tab, g2, final_g, dest_flat, ys, S, _tile(S, 256))
    return out.reshape(B, S, D)


def kernel(x, c, ada_w, ada_b, norm1_g, w_in, mix_scale, w_o, norm2_g, router_w, router_b, w_gu, b_gu,
           w_down, b_down, final_g):
    assert ada_w.shape[0] == 1, "single-layer stack"
    return _layer(x, c, ada_w[0], ada_b[0], norm1_g[0], w_in[0], mix_scale[0], w_o[0], norm2_g[0],
                  router_w[0], router_b[0], w_gu[0], b_gu[0], w_down[0], b_down[0], final_g)
```

```python
import functools

import numpy as np
import jax
import jax.numpy as jnp
from jax import lax
from jax.experimental import pallas as pl
from jax.experimental.pallas import tpu as pltpu

F32 = jnp.float32
BF16 = jnp.bfloat16
I32 = jnp.int32

D_MODEL = 1024
RET_HEADS = 4
RET_DK = 64
RET_DV = 128
RET_CHUNK = 128
DSA_HEADS = 8
DSA_KV_HEADS = 2
DSA_HD = 64
IDX_HEADS = 8
IDX_HD = 64
TOPK_MAX = 256
N_EXPERTS = 32
TOP_K = 4
D_EXPERT = D_MODEL
SWIGLU_LIMIT = 7.0
SWIGLU_ALPHA = 1.702
EPS = 1e-6

RET_W = RET_HEADS * RET_DV
DSA_W = DSA_HEADS * DSA_HD
IN_COLS = 2888
IN_COLS_PAD = 2944

KAUG = 128
VAUG = 80
ALIBI_SPLIT = 64
RET_CHUNKS_PER_STEP = 4
FFN_ROWS = 512
BISECT_VALUE_STEPS = 8
BISECT_MAX_STEPS = 64

LANES = 128
NCH = D_MODEL // LANES
VMEM_LIMIT = 56 * 1024 * 1024
NEG_BIG = -1e30
F32_LOWEST = float(np.finfo(np.float32).min)


def _cparams(sem):
    return pltpu.CompilerParams(dimension_semantics=sem, vmem_limit_bytes=VMEM_LIMIT)


def _mod_kernel(c_ref, w_ref, b_ref, o_ref):
    c = c_ref[...]
    s = c * (1.0 / (1.0 + jnp.exp(-c)))
    o_ref[...] = jnp.dot(s, w_ref[...], preferred_element_type=F32,
                         precision=lax.Precision.HIGHEST) + b_ref[...]


def _mod(c, ada_w, ada_b):
    B, D = c.shape
    n_out = ada_w.shape[1]
    rows = 8
    c8 = jnp.zeros((rows, D), F32).at[:B].set(c)
    out = pl.pallas_call(
        _mod_kernel,
        grid=(n_out // D,),
        in_specs=[pl.BlockSpec((rows, D), lambda j: (0, 0)),
                  pl.BlockSpec((D, D), lambda j: (0, j)),
                  pl.BlockSpec((1, D), lambda j: (0, j))],
        out_specs=pl.BlockSpec((rows, D), lambda j: (0, j)),
        out_shape=jax.ShapeDtypeStruct((rows, n_out), F32),
        compiler_params=_cparams(("arbitrary",)),
        name="mod",
    )(c8, ada_w, ada_b.reshape(1, n_out))
    return out[:B]


def _inproj_kernel(x_ref, g_ref, sc_ref, sh_ref, w_ref,
                   rq_ref, rk_ref, rv_ref, rg_ref, aqt_ref, ak_ref, avt_ref, iqt_ref, ik_ref, iwt_ref):
    x = x_ref[...]
    ms = jnp.mean(x * x, axis=-1, keepdims=True)
    y = x * lax.rsqrt(ms + EPS) * g_ref[...]
    hb = (y * (1.0 + sc_ref[...]) + sh_ref[...]).astype(BF16)

    def proj(lo, hi):
        return jnp.dot(hb, w_ref[:, lo:hi], preferred_element_type=F32)

    tm = x.shape[0]
    d = DSA_HD
    rq_ref[...] = proj(0, 256).astype(BF16)
    rk_ref[...] = (proj(256, 512) * (RET_DK ** -0.5)).astype(BF16)
    rv_ref[...] = proj(512, 1024).astype(BF16)
    rg_ref[...] = proj(1024, 1536).astype(BF16)
    aqt_ref[...] = (proj(1536, 2048) * (d ** -0.5)).T.astype(BF16)
    kk = proj(2048, 2176)
    pos = pl.program_id(1) * tm + lax.broadcasted_iota(I32, (tm, d), 0)
    col = lax.broadcasted_iota(I32, (tm, d), 1)
    posblk = jnp.where(col == 0, pos // ALIBI_SPLIT, jnp.where(col == 1, pos % ALIBI_SPLIT, 0)).astype(F32)
    for g in range(DSA_KV_HEADS):
        ak_ref[:, g * KAUG:g * KAUG + d] = kk[:, g * d:(g + 1) * d].astype(BF16)
        ak_ref[:, g * KAUG + d:(g + 1) * KAUG] = posblk.astype(BF16)
    vt = proj(2176, 2304).T
    r16 = lax.broadcasted_iota(I32, (VAUG - d, tm), 0)
    onesblk = jnp.where(r16 == 0, 1.0, 0.0).astype(BF16)
    for g in range(DSA_KV_HEADS):
        avt_ref[g * VAUG:g * VAUG + d, :] = vt[g * d:(g + 1) * d, :].astype(BF16)
        avt_ref[g * VAUG + d:(g + 1) * VAUG, :] = onesblk
    iqt_ref[...] = proj(2304, 2816).T.astype(BF16)
    last = proj(2816, 2944)
    ik_ref[...] = last[:, :IDX_HD].astype(BF16)
    iwt_ref[...] = last.T[IDX_HD:IDX_HD + IDX_HEADS, :] * ((IDX_HD ** -0.5) * (IDX_HEADS ** -0.5))


def _inproj(x, norm_g, sc, sh, w_pad, tm):
    B, S, D = x.shape
    row = lambda w: pl.BlockSpec((None, tm, w), lambda b, i: (b, i, 0))
    colT = lambda h: pl.BlockSpec((None, h, tm), lambda b, i: (b, 0, i))
    vec = pl.BlockSpec((None, 1, D), lambda b, i: (b, 0, 0))
    sd = lambda shape, dt: jax.ShapeDtypeStruct(shape, dt)
    G = DSA_KV_HEADS
    return pl.pallas_call(
        _inproj_kernel,
        grid=(B, S // tm),
        in_specs=[row(D), pl.BlockSpec((1, D), lambda b, i: (0, 0)), vec, vec,
                  pl.BlockSpec((D, IN_COLS_PAD), lambda b, i: (0, 0))],
        out_specs=[row(256), row(256), row(512), row(512), colT(DSA_W), row(G * KAUG), colT(G * VAUG),
                   colT(IDX_HEADS * IDX_HD), row(IDX_HD), colT(IDX_HEADS)],
        out_shape=[sd((B, S, 256), BF16), sd((B, S, 256), BF16), sd((B, S, 512), BF16),
                   sd((B, S, 512), BF16), sd((B, DSA_W, S), BF16), sd((B, S, G * KAUG), BF16),
                   sd((B, G * VAUG, S), BF16), sd((B, IDX_HEADS * IDX_HD, S), BF16),
                   sd((B, S, IDX_HD), BF16), sd((B, IDX_HEADS, S), F32)],
        compiler_params=_cparams(("parallel", "parallel")),
        name="inproj",
    )(x, norm_g.reshape(1, D), sc, sh, w_pad)


def _ret_kernel(rq_ref, rk_ref, rv_ref, rg_ref, din_ref, qd_ref, kd_ref, cd_ref, ms_ref, o_ref, state_ref):
    @pl.when(pl.program_id(1) == 0)
    def _():
        state_ref[...] = jnp.zeros_like(state_ref)

    C = din_ref.shape[1]
    for c in range(rq_ref.shape[0] // C):
        rows = slice(c * C, (c + 1) * C)
        for h in range(RET_HEADS):
            q = rq_ref[rows, h * RET_DK:(h + 1) * RET_DK]
            k = rk_ref[rows, h * RET_DK:(h + 1) * RET_DK]
            v = rv_ref[rows, h * RET_DV:(h + 1) * RET_DV]
            r_prev = state_ref[h]
            s = lax.dot_general(q, k, (((1,), (1,)), ((), ())), preferred_element_type=F32) * din_ref[h]
            o = jnp.dot(s.astype(BF16), v, preferred_element_type=F32)
            o = o + jnp.dot(q, r_prev.astype(BF16), preferred_element_type=F32) * qd_ref[h]
            vd = (v.astype(F32) * kd_ref[h]).astype(BF16)
            kv = lax.dot_general(k, vd, (((0,), (0,)), ((), ())), preferred_element_type=F32)
            state_ref[h] = r_prev * cd_ref[h] + kv
            o = o * lax.rsqrt(jnp.mean(o * o, axis=-1, keepdims=True) + EPS)
            g = rg_ref[rows, h * RET_DV:(h + 1) * RET_DV].astype(F32)
            gate = g * (1.0 / (1.0 + jnp.exp(-g)))
            o_ref[rows, h * RET_DV:(h + 1) * RET_DV] = (
                gate * o * ms_ref[:, h * RET_DV:(h + 1) * RET_DV]).astype(BF16)


def _ret_consts(C):
    H = RET_HEADS
    log_g = np.log1p(-np.exp2(-5.0 - np.arange(H, dtype=np.float64)))
    pos = np.arange(C, dtype=np.float64)
    diff = pos[:, None] - pos[None, :]
    d_inner = np.where(diff[None] >= 0, np.exp(np.maximum(diff, 0.0)[None] * log_g[:, None, None]), 0.0)
    q_decay = np.exp((pos + 1.0)[None] * log_g[:, None])
    k_decay = np.exp((C - 1.0 - pos)[None] * log_g[:, None])
    chunk_decay = np.exp(C * log_g)
    qd = np.broadcast_to(q_decay[:, :, None], (H, C, RET_DV))
    kd = np.broadcast_to(k_decay[:, :, None], (H, C, RET_DV))
    cd = np.broadcast_to(chunk_decay[:, None, None], (H, 1, RET_DV))
    f = lambda a: jnp.asarray(np.ascontiguousarray(a), F32)
    return f(d_inner), f(qd), f(kd), f(cd)


def _retention(rq, rk, rv, rg, ms_ret):
    B, S, _ = rq.shape
    C = min(RET_CHUNK, S)
    din, qd, kd, cd = _ret_consts(C)
    rows = _tile(S, RET_CHUNKS_PER_STEP * C)
    row = lambda w: pl.BlockSpec((None, rows, w), lambda b, n: (b, n, 0))
    full = lambda a: pl.BlockSpec(a.shape, lambda b, n: (0,) * a.ndim)
    return pl.pallas_call(
        _ret_kernel,
        grid=(B, S // rows),
        in_specs=[row(256), row(256), row(512), row(512), full(din), full(qd), full(kd), full(cd),
                  pl.BlockSpec((1, RET_W), lambda b, n: (0, 0))],
        out_specs=row(RET_W),
        out_shape=jax.ShapeDtypeStruct((B, S, RET_W), BF16),
        scratch_shapes=[pltpu.VMEM((RET_HEADS, RET_DK, RET_DV), F32)],
        compiler_params=_cparams(("parallel", "arbitrary")),
        name="ret",
    )(rq, rk, rv, rg, din, qd, kd, cd, ms_ret)


def _f32_key(x):
    i = lax.bitcast_convert_type(x, I32)
    return i ^ ((i >> 31) & 0x7FFFFFFF)


def _key_f32(k):
    return lax.bitcast_convert_type(k ^ ((k >> 31) & 0x7FFFFFFF), F32)


def _dsa_kernel(iqt_ref, iwt_ref, aqt_ref, ik_ref, ak_ref, avt_ref, ms_ref, o_ref, score_ref, qa_ref, sa_ref, sb_ref, mask_ref, *acc_refs,
                tq, tks, n_sel):
    H, G, R, d = DSA_HEADS, DSA_KV_HEADS, DSA_HEADS // DSA_KV_HEADS, DSA_HD
    t0 = pl.program_id(1) * tq
    nsub = (t0 + tq) // tks
    tka = 2 * tks
    npair = ((t0 + tq) // tka + 1) // 2
    kf = float(n_sel)
    qpos = t0 + lax.broadcasted_iota(I32, (1, tq), 1)
    krow = lax.broadcasted_iota(I32, (tks, tq), 0)

    wrow = [iwt_ref[h:h + 1, :] for h in range(IDX_HEADS)]

    def fold8(x, op):
        acc = x[0:8, :]
        for i in range(1, tks // 8):
            acc = op(acc, x[8 * i:8 * (i + 1), :])
        return acc

    def score_pair(i, carry):
        mx, mn, npos, nnon = carry
        for u in range(2 * tka // tks):
            r0 = pl.multiple_of(i * 2 * tka + u * tks, tks)
            kc = ik_ref[pl.ds(r0, tks), :]
            acc = jnp.zeros((tks, tq), F32)
            for h in range(IDX_HEADS):
                rel = jnp.dot(kc, iqt_ref[h * IDX_HD:(h + 1) * IDX_HD, :], preferred_element_type=F32)
                acc = acc + jnp.maximum(rel, 0.0) * wrow[h]
            causal = r0 + krow <= qpos
            sc = jnp.where(causal, acc, -jnp.inf)
            score_ref[pl.ds(r0, tks), :] = sc
            mx = jnp.maximum(mx, fold8(sc, jnp.maximum))
            mn = jnp.minimum(mn, fold8(jnp.where(causal, acc, jnp.inf), jnp.minimum))
            npos = npos + fold8(jnp.where(sc > 0.0, 1.0, 0.0), jnp.add)
            nnon = nnon + fold8(jnp.where(sc >= 0.0, 1.0, 0.0), jnp.add)
        return mx, mn, npos, nnon

    stat0 = (jnp.full((8, tq), -jnp.inf, F32), jnp.full((8, tq), jnp.inf, F32),
             jnp.zeros((8, tq), F32), jnp.zeros((8, tq), F32))
    mx, mn, npos, nnon = lax.fori_loop(0, npair, score_pair, stat0)
    top = jnp.max(mx, axis=0, keepdims=True)
    lo0 = jnp.min(mn, axis=0, keepdims=True)
    n_pos = jnp.sum(npos, axis=0, keepdims=True)
    n_nonneg = jnp.sum(nnon, axis=0, keepdims=True)


    def count(th, strict):
        def body(j, acc):
            for u in range(2):
                s = score_ref[pl.ds(pl.multiple_of((2 * j + u) * tks, tks), tks), :]
                hit = (s > th) if strict else (s >= th)
                acc = acc + fold8(jnp.where(hit, 1.0, 0.0), jnp.add)
            return acc

        acc = lax.fori_loop(0, nsub // 2, body, jnp.zeros((8, tq), F32))
        return jnp.sum(acc, axis=0, keepdims=True)

    def probe(lo, hi, it):
        lk, hk = _f32_key(lo), _f32_key(hi)
        mk = (lk >> 1) + (hk >> 1) + (lk & hk & 1)
        mv = lo + (hi - lo) * 0.5
        early = (jnp.zeros((1, tq), I32) + it) < BISECT_VALUE_STEPS
        mid = jnp.where(early & (mv > lo) & (mv < hi), mv, _key_f32(mk))
        return mid, jnp.max(jnp.where(mk != lk, 1.0, 0.0))

    def bis_cond(c):
        return (c[5] > 0.0) & (c[6] < BISECT_MAX_STEPS)

    def bis_body(c):
        lo, hi, c_lo, c_hi, mid, _, it = c
        cnt = count(mid, False)
        ge = cnt >= kf
        up = ge | (cnt == kf)
        dn = (~ge) | (cnt == kf)
        lo, c_lo = jnp.where(up, mid, lo), jnp.where(up, cnt, c_lo)
        hi, c_hi = jnp.where(dn, mid, hi), jnp.where(dn, cnt, c_hi)
        mid, active = probe(lo, hi, it + 1)
        return lo, hi, c_lo, c_hi, mid, active, it + 1

    zero = jnp.zeros((1, tq), F32)
    keep_all = qpos + 1 <= n_sel
    settled = keep_all | ((n_nonneg >= kf) & (n_pos < kf))
    above = n_pos >= kf
    c_lo0 = jnp.where(settled | above, n_nonneg, (qpos + 1).astype(F32))
    c_hi0 = jnp.where(settled | ~above, n_nonneg, zero)
    lo0 = jnp.where(settled | above, zero, lo0)
    hi0 = jnp.where(settled | ~above, zero, _key_f32(_f32_key(top) + 1))
    mid0, active0 = probe(lo0, hi0, jnp.int32(0))
    lo, hi, c_lo, c_hi, _, _, _ = lax.while_loop(bis_cond, bis_body,
                                                 (lo0, hi0, c_lo0, c_hi0, mid0, active0, jnp.int32(0)))
    at_hi = c_hi >= kf
    thr = jnp.where(keep_all, F32_LOWEST, jnp.where(at_hi, hi, lo))
    excess = jnp.where(keep_all, 0.0, jnp.where(at_hi, c_hi, c_lo) - kf)

    @pl.when(jnp.max(excess) > 0.0)
    def _():
        budget = kf - count(thr, True)
        earlier = lax.broadcasted_iota(I32, (tks, tks), 1) < lax.broadcasted_iota(I32, (tks, tks), 0)
        earlier = jnp.where(earlier, 1.0, 0.0).astype(BF16)

        def fix(j, seen):
            r0 = pl.multiple_of(j * tks, tks)
            s = score_ref[pl.ds(r0, tks), :]
            eq = s == thr
            eqf = jnp.where(eq, 1.0, 0.0)
            rank = jnp.dot(earlier, eqf.astype(BF16), preferred_element_type=F32) + seen
            score_ref[pl.ds(r0, tks), :] = jnp.where(eq & (rank >= budget), -jnp.inf, s)
            return seen + jnp.sum(eqf, axis=0, keepdims=True)

        lax.fori_loop(0, nsub, fix, jnp.zeros((1, tq), F32))

    arow = lax.broadcasted_iota(I32, (KAUG - d, tq), 0)
    for h in range(H):
        slope = float(2.0 ** (-8.0 * (h + 1) / H))
        qa_ref[h, 0:d, :] = aqt_ref[h * d:(h + 1) * d, :]
        qa_ref[h, d:KAUG, :] = jnp.where(arow == 0, slope * ALIBI_SPLIT, jnp.where(arow == 1, slope, 0.0)).astype(BF16)
    for acc in acc_refs:
        acc[...] = jnp.zeros_like(acc)

    def logits(j, h):
        ka = ak_ref[pl.ds(pl.multiple_of(j * tka, tka), tka), (h // R) * KAUG:(h // R + 1) * KAUG]
        return jnp.dot(ka, qa_ref[h], preferred_element_type=F32)

    def step(j, j_next, cur_ref, next_ref, ms):
        r0 = pl.multiple_of(j * tka, tka)
        mask_ref[...] = jnp.where(score_ref[pl.ds(r0, tka), :] >= thr, 0.0, NEG_BIG)
        nms = []
        for h in range(H):
            g = h // R
            next_ref[h] = logits(j_next, h)
            s = cur_ref[h] + mask_ref[...]
            m_new = jnp.maximum(ms[h], jnp.max(s, axis=0, keepdims=True))
            p = jnp.exp(s - m_new).astype(BF16)
            va = avt_ref[g * VAUG:(g + 1) * VAUG, pl.ds(r0, tka)]
            acc = acc_refs[h]
            acc[...] = acc[...] * jnp.exp(ms[h] - m_new) + jnp.dot(va, p, preferred_element_type=F32)
            nms.append(m_new)
        return tuple(nms)

    for h in range(H):
        sa_ref[h] = logits(0, h)

    def att_pair(i, ms):
        ms = step(2 * i, 2 * i + 1, sa_ref, sb_ref, ms)
        return step(2 * i + 1, jnp.minimum(2 * i + 2, 2 * npair - 1), sb_ref, sa_ref, ms)

    lax.fori_loop(0, npair, att_pair, tuple(jnp.full((1, tq), NEG_BIG, F32) for _ in range(H)))
    for h in range(H):
        a = acc_refs[h][...]
        o = a[0:d, :] / a[d:d + 1, :]
        o_ref[h * d:(h + 1) * d, :] = (o * ms_ref[h * d:(h + 1) * d, :]).astype(BF16)


def _dsa(iqt, iwt, aqt, ik, ak, avt, ms_att, tq, tks):
    B, _, S = iqt.shape
    assert S <= ALIBI_SPLIT * 256 and tq % (2 * tks) == 0
    assert (S // (2 * tks)) % 2 == 0
    n_sel = min(TOPK_MAX, S // 4)
    G = DSA_KV_HEADS
    colT = lambda h: pl.BlockSpec((None, h, tq), lambda b, i: (b, 0, i))
    msb = jnp.broadcast_to(ms_att.reshape(DSA_W, 1), (DSA_W, tq))
    return pl.pallas_call(
        functools.partial(_dsa_kernel, tq=tq, tks=tks, n_sel=n_sel),
        grid=(B, S // tq),
        in_specs=[colT(IDX_HEADS * IDX_HD), colT(IDX_HEADS), colT(DSA_W),
                  pl.BlockSpec((None, S, IDX_HD), lambda b, i: (b, 0, 0)),
                  pl.BlockSpec((None, S, G * KAUG), lambda b, i: (b, 0, 0)),
                  pl.BlockSpec((None, G * VAUG, S), lambda b, i: (b, 0, 0)),
                  pl.BlockSpec((DSA_W, tq), lambda b, i: (0, 0))],
        out_specs=colT(DSA_W),
        out_shape=jax.ShapeDtypeStruct((B, DSA_W, S), BF16),
        scratch_shapes=[pltpu.VMEM((S, tq), F32), pltpu.VMEM((DSA_HEADS, KAUG, tq), BF16),
                        pltpu.VMEM((DSA_HEADS, 2 * tks, tq), F32), pltpu.VMEM((DSA_HEADS, 2 * tks, tq), F32),
                        pltpu.VMEM((2 * tks, tq), F32)]
        + [pltpu.VMEM((VAUG, tq), F32) for _ in range(DSA_HEADS)],
        compiler_params=_cparams(("parallel", "arbitrary")),
        name="dsa",
    )(iqt, iwt, aqt, ik, ak, avt, msb)


def _oproj_kernel(ret_ref, att_ref, x_ref, wo_ref, g1_ref, n2_ref, sc_ref, sh_ref, rw_ref, rb_ref,
                  x1_ref, h2_ref, sel_ref, idx_ref, gate_ref, cnt_ref):
    mixo = jnp.dot(ret_ref[...], wo_ref[:RET_W, :], preferred_element_type=F32)
    mixo = mixo + lax.dot_general(att_ref[...], wo_ref[RET_W:, :], (((0,), (0,)), ((), ())),
                                  preferred_element_type=F32)
    x1 = x_ref[...] + g1_ref[...] * mixo
    x1_ref[...] = x1
    y = x1 * lax.rsqrt(jnp.mean(x1 * x1, axis=-1, keepdims=True) + EPS) * n2_ref[...]
    h2 = y * (1.0 + sc_ref[...]) + sh_ref[...]
    _store_token_major(h2_ref, h2)
    h_hi = h2.astype(BF16)
    h_lo = (h2 - h_hi.astype(F32)).astype(BF16)
    logits = jnp.dot(h_hi, rw_ref[0], preferred_element_type=F32)
    logits = logits + (jnp.dot(h_hi, rw_ref[1], preferred_element_type=F32)
                       + jnp.dot(h_lo, rw_ref[0], preferred_element_type=F32)) + rb_ref[...]
    tm = logits.shape[0]
    lane = lax.broadcasted_iota(I32, (tm, LANES), 1).astype(F32)
    work = jnp.where(lane < N_EXPERTS, logits, -jnp.inf)
    sel = jnp.zeros((tm, LANES), F32)
    idx_tab = jnp.zeros((tm, LANES), F32)
    vals = []
    for k in range(TOP_K):
        m = jnp.max(work, axis=1, keepdims=True)
        idx = jnp.min(jnp.where(work == m, lane, float(LANES)), axis=1, keepdims=True)
        hit = lane == idx
        sel = jnp.where(hit, 1.0, sel)
        idx_tab = jnp.where(lane == k, idx, idx_tab)
        work = jnp.where(hit, -jnp.inf, work)
        vals.append(m)
    es = [jnp.exp(v - vals[0]) for v in vals]
    den = es[0] + es[1] + es[2] + es[3]
    gate_tab = jnp.zeros((tm, LANES), F32)
    for k in range(TOP_K):
        gate_tab = jnp.where(lane == k, es[k] / den, gate_tab)
    sel_ref[...] = sel
    idx_ref[...] = idx_tab
    gate_ref[...] = gate_tab

    @pl.when((pl.program_id(0) == 0) & (pl.program_id(1) == 0))
    def _():
        cnt_ref[...] = jnp.zeros_like(cnt_ref)

    cnt_ref[...] += jnp.sum(sel, axis=0, keepdims=True)


def _oproj(ret, att, x, wo, g1, n2g, sc2, sh2, rw_pad, rb_pad, tm):
    B, S, D = x.shape
    nt = S // tm
    row = lambda w: pl.BlockSpec((None, tm, w), lambda b, i: (b, i, 0))
    flat = lambda w: pl.BlockSpec((tm, w), lambda b, i: (b * nt + i, 0))
    vec = pl.BlockSpec((None, 1, D), lambda b, i: (b, 0, 0))
    cst = lambda shape: pl.BlockSpec(shape, lambda b, i: (0, 0))
    sd = lambda shape, dt: jax.ShapeDtypeStruct(shape, dt)
    N = B * S
    return pl.pallas_call(
        _oproj_kernel,
        grid=(B, nt),
        in_specs=[row(RET_W), pl.BlockSpec((None, DSA_W, tm), lambda b, i: (b, 0, i)), row(D), cst((D, D)), vec,
                  cst((1, D)), vec, vec,
                  pl.BlockSpec((2, D, LANES), lambda b, i: (0, 0, 0)), cst((1, LANES))],
        out_specs=[flat(D), pl.BlockSpec((tm * (D // LANES), LANES), lambda b, i: (b * nt + i, 0)),
                   flat(LANES), flat(LANES), flat(LANES), cst((1, LANES))],
        out_shape=[sd((N, D), F32), sd((N * (D // LANES), LANES), F32), sd((N, LANES), F32), sd((N, LANES), F32),
                   sd((N, LANES), F32), sd((1, LANES), F32)],
        compiler_params=_cparams(("arbitrary", "arbitrary")),
        name="oproj",
    )(ret, att, x, wo, g1, n2g.reshape(1, D), sc2, sh2, rw_pad, rb_pad)


def _dest_kernel(sel_ref, idx_ref, pstart_ref, dest_ref, seen_ref):
    @pl.when(pl.program_id(0) == 0)
    def _():
        seen_ref[...] = jnp.zeros_like(seen_ref)

    sel = sel_ref[...]
    tm = sel.shape[0]
    earlier = lax.broadcasted_iota(I32, (tm, tm), 1) < lax.broadcasted_iota(I32, (tm, tm), 0)
    earlier = jnp.where(earlier, 1.0, 0.0).astype(BF16)
    rank = jnp.dot(earlier, sel.astype(BF16), preferred_element_type=F32) + seen_ref[...]
    dest = pstart_ref[...] + rank
    lane = lax.broadcasted_iota(I32, (tm, LANES), 1).astype(F32)
    idx_tab = idx_ref[...]
    out = jnp.zeros((tm, LANES), F32)
    for k in range(TOP_K):
        e_k = jnp.sum(jnp.where(lane == k, idx_tab, 0.0), axis=1, keepdims=True)
        d_k = jnp.sum(jnp.where(lane == e_k, dest, 0.0), axis=1, keepdims=True)
        out = jnp.where(lane == k, d_k, out)
    dest_ref[...] = out.astype(I32)
    seen_ref[...] += jnp.sum(sel, axis=0, keepdims=True)


def _dest(sel, idx_tab, pstart, tm):
    N = sel.shape[0]
    blk = pl.BlockSpec((tm, LANES), lambda i: (i, 0))
    return pl.pallas_call(
        _dest_kernel,
        grid=(N // tm,),
        in_specs=[blk, blk, pl.BlockSpec((1, LANES), lambda i: (0, 0))],
        out_specs=blk,
        out_shape=jax.ShapeDtypeStruct((N, LANES), I32),
        scratch_shapes=[pltpu.VMEM((1, LANES), F32)],
        compiler_params=_cparams(("arbitrary",)),
        name="dest",
    )(sel, idx_tab, pstart)


def _disp_kernel(zs_ref, h2_ref, dest_hbm, xs_hbm, zbuf, idx_a, idx_b, sem_ia, sem_ib, sem_row, sem_z,
                 *, tm, nch, zrows, n_steps):
    i = pl.program_id(0)
    n = tm * TOP_K

    def idx_copy(tile, buf, sem):
        return pltpu.make_async_copy(dest_hbm.at[pl.ds(tile * n, n)], buf, sem)

    def rows(ref, first, count):
        return ref.at[pl.ds(pl.multiple_of(first * nch, nch), count * nch)]

    def row_copy(row, dst):
        return pltpu.make_async_copy(rows(h2_ref, row, 1), rows(xs_hbm, dst, 1), sem_row)

    def issue_rows(first_row, idx):
        def body(t, c):
            for k in range(TOP_K):
                row_copy(first_row + t, idx[t * TOP_K + k]).start(priority=k % 2)
            return c

        lax.fori_loop(0, tm, body, 0)

    def wait_rows():
        pltpu.make_async_copy(rows(xs_hbm, 0, n), rows(xs_hbm, 0, n), sem_row).wait()

    @pl.when(i == 0)
    def _():
        zbuf[...] = jnp.zeros_like(zbuf)
        for e in range(N_EXPERTS):
            fill = pltpu.make_async_copy(zbuf, rows(xs_hbm, zs_ref[e], zrows), sem_z)
            fill.start()
            fill.wait()
        idx_copy(0, idx_a, sem_ia).start()
        idx_copy(1, idx_b, sem_ib).start()

    idx_copy(2 * i, idx_a, sem_ia).wait()
    issue_rows(0, idx_a)
    idx_copy(2 * i + 1, idx_b, sem_ib).wait()
    issue_rows(tm, idx_b)

    @pl.when(i + 1 < n_steps)
    def _():
        idx_copy(2 * i + 2, idx_a, sem_ia).start()
        idx_copy(2 * i + 3, idx_b, sem_ib).start()

    wait_rows()
    wait_rows()


def _dispatch(h2, dest_flat, zero_start, n_rows, zrows, tm):
    nch = D_MODEL // LANES
    n_steps = h2.shape[0] // nch // (2 * tm)
    grid_spec = pltpu.PrefetchScalarGridSpec(
        num_scalar_prefetch=1,
        grid=(n_steps,),
        in_specs=[pl.BlockSpec((2 * tm * nch, LANES), lambda i, zs: (i, 0)), pl.BlockSpec(memory_space=pl.ANY)],
        out_specs=pl.BlockSpec(memory_space=pl.ANY),
        scratch_shapes=[pltpu.VMEM((zrows * nch, LANES), F32), pltpu.SMEM((tm * TOP_K,), I32),
                        pltpu.SMEM((tm * TOP_K,), I32), pltpu.SemaphoreType.DMA, pltpu.SemaphoreType.DMA,
                        pltpu.SemaphoreType.DMA, pltpu.SemaphoreType.DMA],
    )
    return pl.pallas_call(
        functools.partial(_disp_kernel, tm=tm, nch=nch, zrows=zrows, n_steps=n_steps),
        grid_spec=grid_spec,
        out_shape=jax.ShapeDtypeStruct((n_rows * nch, LANES), F32),
        compiler_params=_cparams(("arbitrary",)),
        name="disp",
    )(zero_start, h2, dest_flat)


def _store_token_major(ref, x):
    rows, d = x.shape
    nch = d // LANES
    for j in range(nch):
        ref[pl.ds(j, rows, stride=nch), :] = x[:, j * LANES:(j + 1) * LANES]


def _load_token_major(ref, rows, nch):
    return jnp.concatenate([ref[pl.ds(j, rows, stride=nch), :] for j in range(nch)], axis=1)


def _ffn_kernel(be_ref, nb_ref, nxt_ref, xs_ref, wgu_hbm, bgu_ref, wd_hbm, bd_ref, ys_ref,
                wgu_st, wd_st, wgu_bf, wd_bf, sem_gu, sem_d):
    i = pl.program_id(0)
    live = i < nb_ref[0]
    e = be_ref[i]

    def fetch(ex):
        return (pltpu.make_async_copy(wgu_hbm.at[ex], wgu_st, sem_gu),
                pltpu.make_async_copy(wd_hbm.at[ex], wd_st, sem_d))

    @pl.when(i == 0)
    def _():
        for cp in fetch(e):
            cp.start()

    @pl.when(live & ((i == 0) | (e != be_ref[jnp.maximum(i - 1, 0)])))
    def _():
        for cp in fetch(e):
            cp.wait()
        wgu_bf[...] = wgu_st[...].astype(BF16)
        wd_bf[...] = wd_st[...].astype(BF16)

        @pl.when(nxt_ref[i] >= 0)
        def _():
            for cp in fetch(nxt_ref[i]):
                cp.start()

    @pl.when(live)
    def _():
        xb = _load_token_major(xs_ref, ys_ref.shape[0] // NCH, NCH).astype(BF16)
        gu = jnp.dot(xb, wgu_bf[...], preferred_element_type=F32) + bgu_ref[...]
        gate = jnp.minimum(gu[:, :D_EXPERT], SWIGLU_LIMIT)
        up = jnp.clip(gu[:, D_EXPERT:], -SWIGLU_LIMIT, SWIGLU_LIMIT)
        glu = gate * (1.0 / (1.0 + jnp.exp(-SWIGLU_ALPHA * gate)))
        act = ((up + 1.0) * glu).astype(BF16)
        ys = jnp.dot(act, wd_bf[...], preferred_element_type=F32) + bd_ref[...]
        _store_token_major(ys_ref, ys)

    @pl.when(jnp.logical_not(live))
    def _():
        ys_ref[...] = jnp.zeros_like(ys_ref)


def _ffn(xs, block_e, n_blocks, next_e, w_gu, b_gu, w_down, b_down, tmb):
    D = D_MODEL
    P = xs.shape[0] // NCH
    E = w_gu.shape[0]
    blk = lambda i, be, nb, nx: (jnp.minimum(i, nb[0] - 1), 0)
    wsel = lambda i, be, nb, nx: (be[jnp.minimum(i, nb[0] - 1)], 0, 0)
    grid_spec = pltpu.PrefetchScalarGridSpec(
        num_scalar_prefetch=3,
        grid=(P // tmb,),
        in_specs=[pl.BlockSpec((tmb * NCH, LANES), blk),
                  pl.BlockSpec(memory_space=pl.ANY),
                  pl.BlockSpec((None, 1, 2 * D_EXPERT), wsel),
                  pl.BlockSpec(memory_space=pl.ANY),
                  pl.BlockSpec((None, 1, D), wsel)],
        out_specs=pl.BlockSpec((tmb * (D // LANES), LANES), lambda i, be, nb, nx: (i, 0)),
        scratch_shapes=[pltpu.VMEM((D, 2 * D_EXPERT), F32), pltpu.VMEM((D_EXPERT, D), F32),
                        pltpu.VMEM((D, 2 * D_EXPERT), BF16), pltpu.VMEM((D_EXPERT, D), BF16),
                        pltpu.SemaphoreType.DMA, pltpu.SemaphoreType.DMA],
    )
    return pl.pallas_call(
        _ffn_kernel,
        grid_spec=grid_spec,
        out_shape=jax.ShapeDtypeStruct((P * (D // LANES), LANES), F32),
        compiler_params=_cparams(("arbitrary",)),
        name="ffn",
    )(block_e, n_blocks, next_e, xs, w_gu, b_gu.reshape(E, 1, 2 * D_EXPERT), w_down, b_down.reshape(E, 1, D))


def _comb_kernel(x1_ref, gate_ref, g2_ref, fg_ref, dest_hbm, ys_hbm, o_ref, buf_a, buf_b, idx_a, idx_b,
                 sem_ia, sem_ib, sem_ra, sem_rb, *, tm, n_steps):
    i = pl.program_id(0)
    n = tm * TOP_K

    def idx_copy(tile, buf, sem):
        return pltpu.make_async_copy(dest_hbm.at[pl.ds(tile * n, n)], buf, sem)

    nch = x1_ref.shape[1] // LANES

    def row_copy(src, buf, k, t, sem):
        return pltpu.make_async_copy(ys_hbm.at[pl.ds(pl.multiple_of(src * nch, nch), nch)],
                                     buf.at[k, pl.ds(pl.multiple_of(t * nch, nch), nch)], sem)

    def issue_rows(idx, buf, sem):
        def body(t, c):
            for k in range(TOP_K):
                row_copy(idx[t * TOP_K + k], buf, k, t, sem).start(priority=k % 2)
            return c

        lax.fori_loop(0, tm, body, 0)

    def wait_rows(buf, sem):
        pltpu.make_async_copy(buf, buf, sem).wait()

    def finish(buf, lo):
        gates = gate_ref[lo:lo + tm, :]
        y = jnp.zeros((tm, x1_ref.shape[1]), F32)
        for k in range(TOP_K):
            y = y + gates[:, k:k + 1] * _load_token_major(buf.at[k], tm, nch)
        v = x1_ref[lo:lo + tm, :] + g2_ref[...] * y
        o_ref[lo:lo + tm, :] = v * lax.rsqrt(jnp.mean(v * v, axis=-1, keepdims=True) + EPS) * fg_ref[...]

    @pl.when(i == 0)
    def _():
        first = idx_copy(0, idx_a, sem_ia)
        first.start()
        first.wait()
        issue_rows(idx_a, buf_a, sem_ra)
        idx_copy(1, idx_b, sem_ib).start()

    idx_copy(2 * i + 1, idx_b, sem_ib).wait()
    issue_rows(idx_b, buf_b, sem_rb)

    @pl.when(i + 1 < n_steps)
    def _():
        idx_copy(2 * i + 2, idx_a, sem_ia).start()

    wait_rows(buf_a, sem_ra)
    finish(buf_a, 0)

    @pl.when(i + 1 < n_steps)
    def _():
        idx_copy(2 * i + 2, idx_a, sem_ia).wait()
        issue_rows(idx_a, buf_a, sem_ra)
        idx_copy(2 * i + 3, idx_b, sem_ib).start()

    wait_rows(buf_b, sem_rb)
    finish(buf_b, tm)


def _combine(x1, gate_tab, g2, final_g, dest_flat, ys, S, tm):
    N, D = x1.shape
    n_steps = N // (2 * tm)
    per_b = S // (2 * tm)
    return pl.pallas_call(
        functools.partial(_comb_kernel, tm=tm, n_steps=n_steps),
        grid=(n_steps,),
        in_specs=[pl.BlockSpec((2 * tm, D), lambda i: (i, 0)),
                  pl.BlockSpec((2 * tm, LANES), lambda i: (i, 0)),
                  pl.BlockSpec((None, 1, D), lambda i: (i // per_b, 0, 0)),
                  pl.BlockSpec((1, D), lambda i: (0, 0)),
                  pl.BlockSpec(memory_space=pl.ANY),
                  pl.BlockSpec(memory_space=pl.ANY)],
        out_specs=pl.BlockSpec((2 * tm, D), lambda i: (i, 0)),
        out_shape=jax.ShapeDtypeStruct((N, D), F32),
        scratch_shapes=[pltpu.VMEM((TOP_K, tm * (D // LANES), LANES), F32),
                        pltpu.VMEM((TOP_K, tm * (D // LANES), LANES), F32),
                        pltpu.SMEM((tm * TOP_K,), I32), pltpu.SMEM((tm * TOP_K,), I32),
                        pltpu.SemaphoreType.DMA, pltpu.SemaphoreType.DMA, pltpu.SemaphoreType.DMA,
                        pltpu.SemaphoreType.DMA],
        compiler_params=_cparams(("arbitrary",)),
        name="comb",
    )(x1, gate_tab, g2, final_g.reshape(1, D), dest_flat, ys)


def _tile(n, pref):
    t = min(pref, n)
    assert n % t == 0, (n, t)
    return t


def _layer(x, c, ada_w, ada_b, norm1_g, w_in, mix_scale, w_o, norm2_g,
           router_w, router_b, w_gu, b_gu, w_down, b_down, final_g):
    B, S, D = x.shape
    N = B * S
    mod = _mod(c, ada_w, ada_b).reshape(B, 6, 1, D)
    sh1, sc1, g1, sh2, sc2, g2 = (mod[:, j] for j in range(6))

    w_pad = jnp.pad(w_in, ((0, 0), (0, IN_COLS_PAD - IN_COLS))).astype(BF16)
    rq, rk, rv, rg, aqt, ak, avt, iqt, ik, iwt = _inproj(x, norm1_g, sc1, sh1, w_pad, _tile(S, 512))
    ms = mix_scale.reshape(1, RET_W + DSA_W)
    ret = _retention(rq, rk, rv, rg, ms[:, :RET_W])
    att = _dsa(iqt, iwt, aqt, ik, ak, avt, ms[:, RET_W:], _tile(S, 512), 128)

    rw_pad = jnp.pad(router_w, ((0, 0), (0, LANES - N_EXPERTS)))
    rw_hi = rw_pad.astype(BF16)
    rw_pad = jnp.stack([rw_hi, (rw_pad - rw_hi.astype(F32)).astype(BF16)])
    rb_pad = jnp.pad(router_b, (0, LANES - N_EXPERTS)).reshape(1, LANES)
    x1, h2, sel, idx_tab, gate_tab, counts = _oproj(ret, att, x, w_o.astype(BF16), g1, norm2_g, sc2, sh2,
                                                    rw_pad, rb_pad, _tile(S, 512))

    tmb = FFN_ROWS
    n_rows = (N * TOP_K + N_EXPERTS * (tmb - 1)) // tmb * tmb + tmb
    cnt = counts[0, :N_EXPERTS].astype(I32)
    padded = (cnt + tmb - 1) // tmb * tmb
    ends = jnp.cumsum(padded)
    starts = ends - padded
    pstart = jnp.pad(starts.astype(F32), (0, LANES - N_EXPERTS)).reshape(1, LANES)
    n_blocks = (ends[-1] // tmb).reshape(1)
    first_row = jnp.arange(n_rows // tmb, dtype=I32) * tmb
    block_e = jnp.minimum(jnp.sum((ends[None, :] <= first_row[:, None]).astype(I32), axis=1), N_EXPERTS - 1)

    tmd = _tile(N, 256)
    dest_tab = _dest(sel, idx_tab, pstart, tmd)
    dest_flat = dest_tab[:, :TOP_K].reshape(N * TOP_K)
    xs = _dispatch(h2, dest_flat, starts + cnt, n_rows, tmb, tmd)
    eid = jnp.arange(N_EXPERTS, dtype=I32)
    later_used = (eid[None, :] > eid[:, None]) & (padded[None, :] > 0)
    next_used = jnp.min(jnp.where(later_used, eid[None, :], N_EXPERTS), axis=1)
    next_e = jnp.where(next_used < N_EXPERTS, next_used, -1)[block_e].astype(I32)
    ys = _ffn(xs, block_e, n_blocks, next_e, w_gu, b_gu, w_down, b_down, tmb)
    out = _combine(x1, gate_tab, g2, final_g, dest_flat, ys, S, _tile(S, 256))
    return out.reshape(B, S, D)


def kernel(x, c, ada_w, ada_b, norm1_g, w_in, mix_scale, w_o, norm2_g, router_w, router_b, w_gu, b_gu,
           w_down, b_down, final_g):
    assert ada_w.shape[0] == 1, "single-layer stack"
    return _layer(x, c, ada_w[0], ada_b[0], norm1_g[0], w_in[0], mix_scale[0], w_o[0], norm2_g[0],
                  router_w[0], router_b[0], w_gu[0], b_gu[0], w_down[0], b_down[0], final_g)
```

```python
import functools

import numpy as np
import jax
import jax.numpy as jnp
from jax import lax
from jax.experimental import pallas as pl
from jax.experimental.pallas import tpu as pltpu

F32 = jnp.float32
BF16 = jnp.bfloat16
I32 = jnp.int32

D_MODEL = 1024
RET_HEADS = 4
RET_DK = 64
RET_DV = 128
RET_CHUNK = 128
DSA_HEADS = 8
DSA_KV_HEADS = 2
DSA_HD = 64
IDX_HEADS = 8
IDX_HD = 64
TOPK_MAX = 256
N_EXPERTS = 32
TOP_K = 4
D_EXPERT = D_MODEL
SWIGLU_LIMIT = 7.0
SWIGLU_ALPHA = 1.702
EPS = 1e-6

RET_W = RET_HEADS * RET_DV
DSA_W = DSA_HEADS * DSA_HD
IN_COLS = 2888
IN_COLS_PAD = 2944

KAUG = 128
VAUG = 80
ALIBI_SPLIT = 64
PROJ_ROWS = 512
DSA_QUERIES = 512
DSA_KEYS = 128
MOE_TOKENS = 256
RET_CHUNKS_PER_STEP = 4
FFN_ROWS = 512
BISECT_VALUE_STEPS = 8
BISECT_MAX_STEPS = 64

LANES = 128
NCH = D_MODEL // LANES
VMEM_LIMIT = 56 * 1024 * 1024
NEG_BIG = -1e30
F32_LOWEST = float(np.finfo(np.float32).min)


def _cparams(sem):
    return pltpu.CompilerParams(dimension_semantics=sem, vmem_limit_bytes=VMEM_LIMIT)


def _mod_kernel(c_ref, w_ref, b_ref, o_ref):
    c = c_ref[...]
    s = c * (1.0 / (1.0 + jnp.exp(-c)))
    o_ref[...] = jnp.dot(s, w_ref[...], preferred_element_type=F32,
                         precision=lax.Precision.HIGHEST) + b_ref[...]


def _mod(c, ada_w, ada_b):
    B, D = c.shape
    n_out = ada_w.shape[1]
    rows = 8
    c8 = jnp.zeros((rows, D), F32).at[:B].set(c)
    out = pl.pallas_call(
        _mod_kernel,
        grid=(n_out // D,),
        in_specs=[pl.BlockSpec((rows, D), lambda j: (0, 0)),
                  pl.BlockSpec((D, D), lambda j: (0, j)),
                  pl.BlockSpec((1, D), lambda j: (0, j))],
        out_specs=pl.BlockSpec((rows, D), lambda j: (0, j)),
        out_shape=jax.ShapeDtypeStruct((rows, n_out), F32),
        compiler_params=_cparams(("arbitrary",)),
        name="mod",
    )(c8, ada_w, ada_b.reshape(1, n_out))
    return out[:B]


def _inproj_kernel(x_ref, g_ref, sc_ref, sh_ref, w_ref,
                   rq_ref, rk_ref, rv_ref, rg_ref, aqt_ref, ak_ref, avt_ref, iqt_ref, ik_ref, iwt_ref):
    x = x_ref[...]
    ms = jnp.mean(x * x, axis=-1, keepdims=True)
    y = x * lax.rsqrt(ms + EPS) * g_ref[...]
    hb = (y * (1.0 + sc_ref[...]) + sh_ref[...]).astype(BF16)

    def proj(lo, hi):
        return jnp.dot(hb, w_ref[:, lo:hi], preferred_element_type=F32)

    tm = x.shape[0]
    d = DSA_HD
    rq_ref[...] = proj(0, 256).astype(BF16)
    rk_ref[...] = (proj(256, 512) * (RET_DK ** -0.5)).astype(BF16)
    rv_ref[...] = proj(512, 1024).astype(BF16)
    rg_ref[...] = proj(1024, 1536).astype(BF16)
    aqt_ref[...] = (proj(1536, 2048) * (d ** -0.5)).T.astype(BF16)
    kk = proj(2048, 2176)
    pos = pl.program_id(1) * tm + lax.broadcasted_iota(I32, (tm, d), 0)
    col = lax.broadcasted_iota(I32, (tm, d), 1)
    posblk = jnp.where(col == 0, pos // ALIBI_SPLIT, jnp.where(col == 1, pos % ALIBI_SPLIT, 0)).astype(F32)
    for g in range(DSA_KV_HEADS):
        ak_ref[:, g * KAUG:g * KAUG + d] = kk[:, g * d:(g + 1) * d].astype(BF16)
        ak_ref[:, g * KAUG + d:(g + 1) * KAUG] = posblk.astype(BF16)
    vt = proj(2176, 2304).T
    r16 = lax.broadcasted_iota(I32, (VAUG - d, tm), 0)
    onesblk = jnp.where(r16 == 0, 1.0, 0.0).astype(BF16)
    for g in range(DSA_KV_HEADS):
        avt_ref[g * VAUG:g * VAUG + d, :] = vt[g * d:(g + 1) * d, :].astype(BF16)
        avt_ref[g * VAUG + d:(g + 1) * VAUG, :] = onesblk
    iqt_ref[...] = proj(2304, 2816).T.astype(BF16)
    last = proj(2816, 2944)
    ik_ref[...] = last[:, :IDX_HD].astype(BF16)
    iwt_ref[...] = last.T[IDX_HD:IDX_HD + IDX_HEADS, :] * ((IDX_HD ** -0.5) * (IDX_HEADS ** -0.5))


def _inproj(x, norm_g, sc, sh, w_pad, tm):
    B, S, D = x.shape
    row = lambda w: pl.BlockSpec((None, tm, w), lambda b, i: (b, i, 0))
    colT = lambda h: pl.BlockSpec((None, h, tm), lambda b, i: (b, 0, i))
    vec = pl.BlockSpec((None, 1, D), lambda b, i: (b, 0, 0))
    sd = lambda shape, dt: jax.ShapeDtypeStruct(shape, dt)
    G = DSA_KV_HEADS
    return pl.pallas_call(
        _inproj_kernel,
        grid=(B, S // tm),
        in_specs=[row(D), pl.BlockSpec((1, D), lambda b, i: (0, 0)), vec, vec,
                  pl.BlockSpec((D, IN_COLS_PAD), lambda b, i: (0, 0))],
        out_specs=[row(256), row(256), row(512), row(512), colT(DSA_W), row(G * KAUG), colT(G * VAUG),
                   colT(IDX_HEADS * IDX_HD), row(IDX_HD), colT(IDX_HEADS)],
        out_shape=[sd((B, S, 256), BF16), sd((B, S, 256), BF16), sd((B, S, 512), BF16),
                   sd((B, S, 512), BF16), sd((B, DSA_W, S), BF16), sd((B, S, G * KAUG), BF16),
                   sd((B, G * VAUG, S), BF16), sd((B, IDX_HEADS * IDX_HD, S), BF16),
                   sd((B, S, IDX_HD), BF16), sd((B, IDX_HEADS, S), F32)],
        compiler_params=_cparams(("parallel", "parallel")),
        name="inproj",
    )(x, norm_g.reshape(1, D), sc, sh, w_pad)


def _ret_kernel(rq_ref, rk_ref, rv_ref, rg_ref, din_ref, qd_ref, kd_ref, cd_ref, ms_ref, o_ref, state_ref):
    @pl.when(pl.program_id(1) == 0)
    def _():
        state_ref[...] = jnp.zeros_like(state_ref)

    C = din_ref.shape[1]
    for c in range(rq_ref.shape[0] // C):
        rows = slice(c * C, (c + 1) * C)
        for h in range(RET_HEADS):
            q = rq_ref[rows, h * RET_DK:(h + 1) * RET_DK]
            k = rk_ref[rows, h * RET_DK:(h + 1) * RET_DK]
            v = rv_ref[rows, h * RET_DV:(h + 1) * RET_DV]
            r_prev = state_ref[h]
            s = lax.dot_general(q, k, (((1,), (1,)), ((), ())), preferred_element_type=F32) * din_ref[h]
            o = jnp.dot(s.astype(BF16), v, preferred_element_type=F32)
            o = o + jnp.dot(q, r_prev.astype(BF16), preferred_element_type=F32) * qd_ref[h]
            vd = (v.astype(F32) * kd_ref[h]).astype(BF16)
            kv = lax.dot_general(k, vd, (((0,), (0,)), ((), ())), preferred_element_type=F32)
            state_ref[h] = r_prev * cd_ref[h] + kv
            o = o * lax.rsqrt(jnp.mean(o * o, axis=-1, keepdims=True) + EPS)
            g = rg_ref[rows, h * RET_DV:(h + 1) * RET_DV].astype(F32)
            gate = g * (1.0 / (1.0 + jnp.exp(-g)))
            o_ref[rows, h * RET_DV:(h + 1) * RET_DV] = (
                gate * o * ms_ref[:, h * RET_DV:(h + 1) * RET_DV]).astype(BF16)


def _ret_consts(C):
    H = RET_HEADS
    log_g = np.log1p(-np.exp2(-5.0 - np.arange(H, dtype=np.float64)))
    pos = np.arange(C, dtype=np.float64)
    diff = pos[:, None] - pos[None, :]
    d_inner = np.where(diff[None] >= 0, np.exp(np.maximum(diff, 0.0)[None] * log_g[:, None, None]), 0.0)
    q_decay = np.exp((pos + 1.0)[None] * log_g[:, None])
    k_decay = np.exp((C - 1.0 - pos)[None] * log_g[:, None])
    chunk_decay = np.exp(C * log_g)
    qd = np.broadcast_to(q_decay[:, :, None], (H, C, RET_DV))
    kd = np.broadcast_to(k_decay[:, :, None], (H, C, RET_DV))
    cd = np.broadcast_to(chunk_decay[:, None, None], (H, 1, RET_DV))
    f = lambda a: jnp.asarray(np.ascontiguousarray(a), F32)
    return f(d_inner), f(qd), f(kd), f(cd)


def _retention(rq, rk, rv, rg, ms_ret):
    B, S, _ = rq.shape
    C = min(RET_CHUNK, S)
    din, qd, kd, cd = _ret_consts(C)
    rows = _tile(S, RET_CHUNKS_PER_STEP * C)
    row = lambda w: pl.BlockSpec((None, rows, w), lambda b, n: (b, n, 0))
    full = lambda a: pl.BlockSpec(a.shape, lambda b, n: (0,) * a.ndim)
    return pl.pallas_call(
        _ret_kernel,
        grid=(B, S // rows),
        in_specs=[row(256), row(256), row(512), row(512), full(din), full(qd), full(kd), full(cd),
                  pl.BlockSpec((1, RET_W), lambda b, n: (0, 0))],
        out_specs=row(RET_W),
        out_shape=jax.ShapeDtypeStruct((B, S, RET_W), BF16),
        scratch_shapes=[pltpu.VMEM((RET_HEADS, RET_DK, RET_DV), F32)],
        compiler_params=_cparams(("parallel", "arbitrary")),
        name="ret",
    )(rq, rk, rv, rg, din, qd, kd, cd, ms_ret)


def _f32_key(x):
    i = lax.bitcast_convert_type(x, I32)
    return i ^ ((i >> 31) & 0x7FFFFFFF)


def _key_f32(k):
    return lax.bitcast_convert_type(k ^ ((k >> 31) & 0x7FFFFFFF), F32)


def _dsa_kernel(iqt_ref, iwt_ref, aqt_ref, ik_ref, ak_ref, avt_ref, ms_ref, o_ref, score_ref, qa_ref, sa_ref, sb_ref, mask_ref, *acc_refs,
                tq, tks, n_sel):
    H, G, R, d = DSA_HEADS, DSA_KV_HEADS, DSA_HEADS // DSA_KV_HEADS, DSA_HD
    t0 = pl.program_id(1) * tq
    nsub = (t0 + tq) // tks
    tka = 2 * tks
    npair = ((t0 + tq) // tka + 1) // 2
    kf = float(n_sel)
    qpos = t0 + lax.broadcasted_iota(I32, (1, tq), 1)
    krow = lax.broadcasted_iota(I32, (tks, tq), 0)

    wrow = [iwt_ref[h:h + 1, :] for h in range(IDX_HEADS)]

    def fold8(x, op):
        acc = x[0:8, :]
        for i in range(1, tks // 8):
            acc = op(acc, x[8 * i:8 * (i + 1), :])
        return acc

    def score_pair(i, carry):
        mx, mn, npos, nnon = carry
        for u in range(2 * tka // tks):
            r0 = pl.multiple_of(i * 2 * tka + u * tks, tks)
            kc = ik_ref[pl.ds(r0, tks), :]
            acc = jnp.zeros((tks, tq), F32)
            for h in range(IDX_HEADS):
                rel = jnp.dot(kc, iqt_ref[h * IDX_HD:(h + 1) * IDX_HD, :], preferred_element_type=F32)
                acc = acc + jnp.maximum(rel, 0.0) * wrow[h]
            causal = r0 + krow <= qpos
            sc = jnp.where(causal, acc, -jnp.inf)
            score_ref[pl.ds(r0, tks), :] = sc
            mx = jnp.maximum(mx, fold8(sc, jnp.maximum))
            mn = jnp.minimum(mn, fold8(jnp.where(causal, acc, jnp.inf), jnp.minimum))
            npos = npos + fold8(jnp.where(sc > 0.0, 1.0, 0.0), jnp.add)
            nnon = nnon + fold8(jnp.where(sc >= 0.0, 1.0, 0.0), jnp.add)
        return mx, mn, npos, nnon

    stat0 = (jnp.full((8, tq), -jnp.inf, F32), jnp.full((8, tq), jnp.inf, F32),
             jnp.zeros((8, tq), F32), jnp.zeros((8, tq), F32))
    mx, mn, npos, nnon = lax.fori_loop(0, npair, score_pair, stat0)
    top = jnp.max(mx, axis=0, keepdims=True)
    lo0 = jnp.min(mn, axis=0, keepdims=True)
    n_pos = jnp.sum(npos, axis=0, keepdims=True)
    n_nonneg = jnp.sum(nnon, axis=0, keepdims=True)


    def count(th, strict):
        def body(j, acc):
            for u in range(2):
                s = score_ref[pl.ds(pl.multiple_of((2 * j + u) * tks, tks), tks), :]
                hit = (s > th) if strict else (s >= th)
                acc = acc + fold8(jnp.where(hit, 1.0, 0.0), jnp.add)
            return acc

        acc = lax.fori_loop(0, nsub // 2, body, jnp.zeros((8, tq), F32))
        return jnp.sum(acc, axis=0, keepdims=True)

    def probe(lo, hi, it):
        lk, hk = _f32_key(lo), _f32_key(hi)
        mk = (lk >> 1) + (hk >> 1) + (lk & hk & 1)
        mv = lo + (hi - lo) * 0.5
        early = (jnp.zeros((1, tq), I32) + it) < BISECT_VALUE_STEPS
        mid = jnp.where(early & (mv > lo) & (mv < hi), mv, _key_f32(mk))
        return mid, jnp.max(jnp.where(mk != lk, 1.0, 0.0))

    def bis_cond(c):
        return (c[5] > 0.0) & (c[6] < BISECT_MAX_STEPS)

    def bis_body(c):
        lo, hi, c_lo, c_hi, mid, _, it = c
        cnt = count(mid, False)
        ge = cnt >= kf
        up = ge | (cnt == kf)
        dn = (~ge) | (cnt == kf)
        lo, c_lo = jnp.where(up, mid, lo), jnp.where(up, cnt, c_lo)
        hi, c_hi = jnp.where(dn, mid, hi), jnp.where(dn, cnt, c_hi)
        mid, active = probe(lo, hi, it + 1)
        return lo, hi, c_lo, c_hi, mid, active, it + 1

    zero = jnp.zeros((1, tq), F32)
    keep_all = qpos + 1 <= n_sel
    settled = keep_all | ((n_nonneg >= kf) & (n_pos < kf))
    above = n_pos >= kf
    c_lo0 = jnp.where(settled | above, n_nonneg, (qpos + 1).astype(F32))
    c_hi0 = jnp.where(settled | ~above, n_nonneg, zero)
    lo0 = jnp.where(settled | above, zero, lo0)
    hi0 = jnp.where(settled | ~above, zero, _key_f32(_f32_key(top) + 1))
    mid0, active0 = probe(lo0, hi0, jnp.int32(0))
    lo, hi, c_lo, c_hi, _, _, _ = lax.while_loop(bis_cond, bis_body,
                                                 (lo0, hi0, c_lo0, c_hi0, mid0, active0, jnp.int32(0)))
    at_hi = c_hi >= kf
    thr = jnp.where(keep_all, F32_LOWEST, jnp.where(at_hi, hi, lo))
    excess = jnp.where(keep_all, 0.0, jnp.where(at_hi, c_hi, c_lo) - kf)

    @pl.when(jnp.max(excess) > 0.0)
    def _():
        budget = kf - count(thr, True)
        earlier = lax.broadcasted_iota(I32, (tks, tks), 1) < lax.broadcasted_iota(I32, (tks, tks), 0)
        earlier = jnp.where(earlier, 1.0, 0.0).astype(BF16)

        def fix(j, seen):
            tiles = []
            for u in range(2):
                r0 = pl.multiple_of((2 * j + u) * tks, tks)
                s = score_ref[pl.ds(r0, tks), :]
                eq = s == thr
                eqf = jnp.where(eq, 1.0, 0.0)
                within = jnp.dot(earlier, eqf.astype(BF16), preferred_element_type=F32)
                tiles.append((r0, s, eq, within, jnp.sum(eqf, axis=0, keepdims=True)))
            for r0, s, eq, within, n_eq in tiles:
                score_ref[pl.ds(r0, tks), :] = jnp.where(eq & (within + seen >= budget), -jnp.inf, s)
                seen = seen + n_eq
            return seen

        lax.fori_loop(0, nsub // 2, fix, jnp.zeros((1, tq), F32))

    arow = lax.broadcasted_iota(I32, (KAUG - d, tq), 0)
    for h in range(H):
        slope = float(2.0 ** (-8.0 * (h + 1) / H))
        qa_ref[h, 0:d, :] = aqt_ref[h * d:(h + 1) * d, :]
        qa_ref[h, d:KAUG, :] = jnp.where(arow == 0, slope * ALIBI_SPLIT, jnp.where(arow == 1, slope, 0.0)).astype(BF16)
    for acc in acc_refs:
        acc[...] = jnp.zeros_like(acc)

    def logits(j, h):
        ka = ak_ref[pl.ds(pl.multiple_of(j * tka, tka), tka), (h // R) * KAUG:(h // R + 1) * KAUG]
        return jnp.dot(ka, qa_ref[h], preferred_element_type=F32)

    def step(j, j_next, cur_ref, next_ref, ms):
        r0 = pl.multiple_of(j * tka, tka)
        mask_ref[...] = jnp.where(score_ref[pl.ds(r0, tka), :] >= thr, 0.0, NEG_BIG)
        nms = []
        for h in range(H):
            g = h // R
            next_ref[h] = logits(j_next, h)
            s = cur_ref[h] + mask_ref[...]
            m_new = jnp.maximum(ms[h], jnp.max(s, axis=0, keepdims=True))
            p = jnp.exp(s - m_new).astype(BF16)
            va = avt_ref[g * VAUG:(g + 1) * VAUG, pl.ds(r0, tka)]
            acc = acc_refs[h]
            acc[...] = acc[...] * jnp.exp(ms[h] - m_new) + jnp.dot(va, p, preferred_element_type=F32)
            nms.append(m_new)
        return tuple(nms)

    for h in range(H):
        sa_ref[h] = logits(0, h)

    def att_pair(i, ms):
        ms = step(2 * i, 2 * i + 1, sa_ref, sb_ref, ms)
        return step(2 * i + 1, jnp.minimum(2 * i + 2, 2 * npair - 1), sb_ref, sa_ref, ms)

    lax.fori_loop(0, npair, att_pair, tuple(jnp.full((1, tq), NEG_BIG, F32) for _ in range(H)))
    for h in range(H):
        a = acc_refs[h][...]
        o = a[0:d, :] / a[d:d + 1, :]
        o_ref[h * d:(h + 1) * d, :] = (o * ms_ref[h * d:(h + 1) * d, :]).astype(BF16)


def _dsa(iqt, iwt, aqt, ik, ak, avt, ms_att, tq, tks):
    B, _, S = iqt.shape
    assert S <= ALIBI_SPLIT * 256 and tq % (2 * tks) == 0
    assert (S // (2 * tks)) % 2 == 0
    n_sel = min(TOPK_MAX, S // 4)
    G = DSA_KV_HEADS
    colT = lambda h: pl.BlockSpec((None, h, tq), lambda b, i: (b, 0, i))
    msb = jnp.broadcast_to(ms_att.reshape(DSA_W, 1), (DSA_W, tq))
    return pl.pallas_call(
        functools.partial(_dsa_kernel, tq=tq, tks=tks, n_sel=n_sel),
        grid=(B, S // tq),
        in_specs=[colT(IDX_HEADS * IDX_HD), colT(IDX_HEADS), colT(DSA_W),
                  pl.BlockSpec((None, S, IDX_HD), lambda b, i: (b, 0, 0)),
                  pl.BlockSpec((None, S, G * KAUG), lambda b, i: (b, 0, 0)),
                  pl.BlockSpec((None, G * VAUG, S), lambda b, i: (b, 0, 0)),
                  pl.BlockSpec((DSA_W, tq), lambda b, i: (0, 0))],
        out_specs=colT(DSA_W),
        out_shape=jax.ShapeDtypeStruct((B, DSA_W, S), BF16),
        scratch_shapes=[pltpu.VMEM((S, tq), F32), pltpu.VMEM((DSA_HEADS, KAUG, tq), BF16),
                        pltpu.VMEM((DSA_HEADS, 2 * tks, tq), F32), pltpu.VMEM((DSA_HEADS, 2 * tks, tq), F32),
                        pltpu.VMEM((2 * tks, tq), F32)]
        + [pltpu.VMEM((VAUG, tq), F32) for _ in range(DSA_HEADS)],
        compiler_params=_cparams(("parallel", "arbitrary")),
        name="dsa",
    )(iqt, iwt, aqt, ik, ak, avt, msb)


def _oproj_kernel(ret_ref, att_ref, x_ref, wo_ref, g1_ref, n2_ref, sc_ref, sh_ref, rw_ref, rb_ref,
                  x1_ref, h2_ref, sel_ref, idx_ref, gate_ref, cnt_ref):
    mixo = jnp.dot(ret_ref[...], wo_ref[:RET_W, :], preferred_element_type=F32)
    mixo = mixo + lax.dot_general(att_ref[...], wo_ref[RET_W:, :], (((0,), (0,)), ((), ())),
                                  preferred_element_type=F32)
    x1 = x_ref[...] + g1_ref[...] * mixo
    x1_ref[...] = x1
    y = x1 * lax.rsqrt(jnp.mean(x1 * x1, axis=-1, keepdims=True) + EPS) * n2_ref[...]
    h2 = y * (1.0 + sc_ref[...]) + sh_ref[...]
    _store_token_major(h2_ref, h2)
    h_hi = h2.astype(BF16)
    h_lo = (h2 - h_hi.astype(F32)).astype(BF16)
    logits = jnp.dot(h_hi, rw_ref[0], preferred_element_type=F32)
    logits = logits + (jnp.dot(h_hi, rw_ref[1], preferred_element_type=F32)
                       + jnp.dot(h_lo, rw_ref[0], preferred_element_type=F32)) + rb_ref[...]
    tm = logits.shape[0]
    lane = lax.broadcasted_iota(I32, (tm, LANES), 1).astype(F32)
    work = jnp.where(lane < N_EXPERTS, logits, -jnp.inf)
    sel = jnp.zeros((tm, LANES), F32)
    idx_tab = jnp.zeros((tm, LANES), F32)
    vals = []
    for k in range(TOP_K):
        m = jnp.max(work, axis=1, keepdims=True)
        idx = jnp.min(jnp.where(work == m, lane, float(LANES)), axis=1, keepdims=True)
        hit = lane == idx
        sel = jnp.where(hit, 1.0, sel)
        idx_tab = jnp.where(lane == k, idx, idx_tab)
        work = jnp.where(hit, -jnp.inf, work)
        vals.append(m)
    es = [jnp.exp(v - vals[0]) for v in vals]
    den = es[0] + es[1] + es[2] + es[3]
    gate_tab = jnp.zeros((tm, LANES), F32)
    for k in range(TOP_K):
        gate_tab = jnp.where(lane == k, es[k] / den, gate_tab)
    sel_ref[...] = sel
    idx_ref[...] = idx_tab
    gate_ref[...] = gate_tab

    @pl.when((pl.program_id(0) == 0) & (pl.program_id(1) == 0))
    def _():
        cnt_ref[...] = jnp.zeros_like(cnt_ref)

    cnt_ref[...] += jnp.sum(sel, axis=0, keepdims=True)


def _oproj(ret, att, x, wo, g1, n2g, sc2, sh2, rw_pad, rb_pad, tm):
    B, S, D = x.shape
    nt = S // tm
    row = lambda w: pl.BlockSpec((None, tm, w), lambda b, i: (b, i, 0))
    flat = lambda w: pl.BlockSpec((tm, w), lambda b, i: (b * nt + i, 0))
    vec = pl.BlockSpec((None, 1, D), lambda b, i: (b, 0, 0))
    cst = lambda shape: pl.BlockSpec(shape, lambda b, i: (0, 0))
    sd = lambda shape, dt: jax.ShapeDtypeStruct(shape, dt)
    N = B * S
    return pl.pallas_call(
        _oproj_kernel,
        grid=(B, nt),
        in_specs=[row(RET_W), pl.BlockSpec((None, DSA_W, tm), lambda b, i: (b, 0, i)), row(D), cst((D, D)), vec,
                  cst((1, D)), vec, vec,
                  pl.BlockSpec((2, D, LANES), lambda b, i: (0, 0, 0)), cst((1, LANES))],
        out_specs=[flat(D), pl.BlockSpec((tm * (D // LANES), LANES), lambda b, i: (b * nt + i, 0)),
                   flat(LANES), flat(LANES), flat(LANES), cst((1, LANES))],
        out_shape=[sd((N, D), F32), sd((N * (D // LANES), LANES), F32), sd((N, LANES), F32), sd((N, LANES), F32),
                   sd((N, LANES), F32), sd((1, LANES), F32)],
        compiler_params=_cparams(("arbitrary", "arbitrary")),
        name="oproj",
    )(ret, att, x, wo, g1, n2g.reshape(1, D), sc2, sh2, rw_pad, rb_pad)


def _dest_kernel(sel_ref, idx_ref, pstart_ref, dest_ref, seen_ref):
    @pl.when(pl.program_id(0) == 0)
    def _():
        seen_ref[...] = jnp.zeros_like(seen_ref)

    sel = sel_ref[...]
    tm = sel.shape[0]
    earlier = lax.broadcasted_iota(I32, (tm, tm), 1) < lax.broadcasted_iota(I32, (tm, tm), 0)
    earlier = jnp.where(earlier, 1.0, 0.0).astype(BF16)
    rank = jnp.dot(earlier, sel.astype(BF16), preferred_element_type=F32) + seen_ref[...]
    dest = pstart_ref[...] + rank
    lane = lax.broadcasted_iota(I32, (tm, LANES), 1).astype(F32)
    idx_tab = idx_ref[...]
    out = jnp.zeros((tm, LANES), F32)
    for k in range(TOP_K):
        e_k = jnp.sum(jnp.where(lane == k, idx_tab, 0.0), axis=1, keepdims=True)
        d_k = jnp.sum(jnp.where(lane == e_k, dest, 0.0), axis=1, keepdims=True)
        out = jnp.where(lane == k, d_k, out)
    dest_ref[...] = out.astype(I32)
    seen_ref[...] += jnp.sum(sel, axis=0, keepdims=True)


def _dest(sel, idx_tab, pstart, tm):
    N = sel.shape[0]
    blk = pl.BlockSpec((tm, LANES), lambda i: (i, 0))
    return pl.pallas_call(
        _dest_kernel,
        grid=(N // tm,),
        in_specs=[blk, blk, pl.BlockSpec((1, LANES), lambda i: (0, 0))],
        out_specs=blk,
        out_shape=jax.ShapeDtypeStruct((N, LANES), I32),
        scratch_shapes=[pltpu.VMEM((1, LANES), F32)],
        compiler_params=_cparams(("arbitrary",)),
        name="dest",
    )(sel, idx_tab, pstart)


def _disp_kernel(zs_ref, h2_ref, dest_hbm, xs_hbm, zbuf, idx_a, idx_b, sem_ia, sem_ib, sem_row, sem_z,
                 *, tm, nch, zrows, n_steps):
    i = pl.program_id(0)
    n = tm * TOP_K

    def idx_copy(tile, buf, sem):
        return pltpu.make_async_copy(dest_hbm.at[pl.ds(tile * n, n)], buf, sem)

    def rows(ref, first, count):
        return ref.at[pl.ds(pl.multiple_of(first * nch, nch), count * nch)]

    def row_copy(row, dst):
        return pltpu.make_async_copy(rows(h2_ref, row, 1), rows(xs_hbm, dst, 1), sem_row)

    def issue_rows(first_row, idx):
        def body(t, c):
            for k in range(TOP_K):
                row_copy(first_row + t, idx[t * TOP_K + k]).start(priority=k % 2)
            return c

        lax.fori_loop(0, tm, body, 0)

    def wait_rows():
        pltpu.make_async_copy(rows(xs_hbm, 0, n), rows(xs_hbm, 0, n), sem_row).wait()

    @pl.when(i == 0)
    def _():
        zbuf[...] = jnp.zeros_like(zbuf)
        for e in range(N_EXPERTS):
            fill = pltpu.make_async_copy(zbuf, rows(xs_hbm, zs_ref[e], zrows), sem_z)
            fill.start()
            fill.wait()
        idx_copy(0, idx_a, sem_ia).start()
        idx_copy(1, idx_b, sem_ib).start()

    idx_copy(2 * i, idx_a, sem_ia).wait()
    issue_rows(0, idx_a)
    idx_copy(2 * i + 1, idx_b, sem_ib).wait()
    issue_rows(tm, idx_b)

    @pl.when(i + 1 < n_steps)
    def _():
        idx_copy(2 * i + 2, idx_a, sem_ia).start()
        idx_copy(2 * i + 3, idx_b, sem_ib).start()

    wait_rows()
    wait_rows()


def _dispatch(h2, dest_flat, zero_start, n_rows, zrows, tm):
    nch = D_MODEL // LANES
    n_steps = h2.shape[0] // nch // (2 * tm)
    grid_spec = pltpu.PrefetchScalarGridSpec(
        num_scalar_prefetch=1,
        grid=(n_steps,),
        in_specs=[pl.BlockSpec((2 * tm * nch, LANES), lambda i, zs: (i, 0)), pl.BlockSpec(memory_space=pl.ANY)],
        out_specs=pl.BlockSpec(memory_space=pl.ANY),
        scratch_shapes=[pltpu.VMEM((zrows * nch, LANES), F32), pltpu.SMEM((tm * TOP_K,), I32),
                        pltpu.SMEM((tm * TOP_K,), I32), pltpu.SemaphoreType.DMA, pltpu.SemaphoreType.DMA,
                        pltpu.SemaphoreType.DMA, pltpu.SemaphoreType.DMA],
    )
    return pl.pallas_call(
        functools.partial(_disp_kernel, tm=tm, nch=nch, zrows=zrows, n_steps=n_steps),
        grid_spec=grid_spec,
        out_shape=jax.ShapeDtypeStruct((n_rows * nch, LANES), F32),
        compiler_params=_cparams(("arbitrary",)),
        name="disp",
    )(zero_start, h2, dest_flat)


def _store_token_major(ref, x):
    rows, d = x.shape
    nch = d // LANES
    for j in range(nch):
        ref[pl.ds(j, rows, stride=nch), :] = x[:, j * LANES:(j + 1) * LANES]


def _load_token_major(ref, rows, nch):
    return jnp.concatenate([ref[pl.ds(j, rows, stride=nch), :] for j in range(nch)], axis=1)


def _ffn_kernel(be_ref, nb_ref, nxt_ref, xs_ref, wgu_hbm, bgu_ref, wd_hbm, bd_ref, ys_ref,
                wgu_st, wd_st, wgu_bf, wd_bf, sem_gu, sem_d):
    i = pl.program_id(0)
    live = i < nb_ref[0]
    e = be_ref[i]

    def fetch(ex):
        return (pltpu.make_async_copy(wgu_hbm.at[ex], wgu_st, sem_gu),
                pltpu.make_async_copy(wd_hbm.at[ex], wd_st, sem_d))

    @pl.when(i == 0)
    def _():
        for cp in fetch(e):
            cp.start()

    @pl.when(live & ((i == 0) | (e != be_ref[jnp.maximum(i - 1, 0)])))
    def _():
        for cp in fetch(e):
            cp.wait()
        wgu_bf[...] = wgu_st[...].astype(BF16)
        wd_bf[...] = wd_st[...].astype(BF16)

        @pl.when(nxt_ref[i] >= 0)
        def _():
            for cp in fetch(nxt_ref[i]):
                cp.start()

    @pl.when(live)
    def _():
        xb = _load_token_major(xs_ref, ys_ref.shape[0] // NCH, NCH).astype(BF16)
        gu = jnp.dot(xb, wgu_bf[...], preferred_element_type=F32) + bgu_ref[...]
        gate = jnp.minimum(gu[:, :D_EXPERT], SWIGLU_LIMIT)
        up = jnp.clip(gu[:, D_EXPERT:], -SWIGLU_LIMIT, SWIGLU_LIMIT)
        glu = gate * (1.0 / (1.0 + jnp.exp(-SWIGLU_ALPHA * gate)))
        act = ((up + 1.0) * glu).astype(BF16)
        ys = jnp.dot(act, wd_bf[...], preferred_element_type=F32) + bd_ref[...]
        _store_token_major(ys_ref, ys)

    @pl.when(jnp.logical_not(live))
    def _():
        ys_ref[...] = jnp.zeros_like(ys_ref)


def _ffn(xs, block_e, n_blocks, next_e, w_gu, b_gu, w_down, b_down, tmb):
    D = D_MODEL
    P = xs.shape[0] // NCH
    E = w_gu.shape[0]
    blk = lambda i, be, nb, nx: (jnp.minimum(i, nb[0] - 1), 0)
    wsel = lambda i, be, nb, nx: (be[jnp.minimum(i, nb[0] - 1)], 0, 0)
    grid_spec = pltpu.PrefetchScalarGridSpec(
        num_scalar_prefetch=3,
        grid=(P // tmb,),
        in_specs=[pl.BlockSpec((tmb * NCH, LANES), blk),
                  pl.BlockSpec(memory_space=pl.ANY),
                  pl.BlockSpec((None, 1, 2 * D_EXPERT), wsel),
                  pl.BlockSpec(memory_space=pl.ANY),
                  pl.BlockSpec((None, 1, D), wsel)],
        out_specs=pl.BlockSpec((tmb * (D // LANES), LANES), lambda i, be, nb, nx: (i, 0)),
        scratch_shapes=[pltpu.VMEM((D, 2 * D_EXPERT), F32), pltpu.VMEM((D_EXPERT, D), F32),
                        pltpu.VMEM((D, 2 * D_EXPERT), BF16), pltpu.VMEM((D_EXPERT, D), BF16),
                        pltpu.SemaphoreType.DMA, pltpu.SemaphoreType.DMA],
    )
    return pl.pallas_call(
        _ffn_kernel,
        grid_spec=grid_spec,
        out_shape=jax.ShapeDtypeStruct((P * (D // LANES), LANES), F32),
        compiler_params=_cparams(("arbitrary",)),
        name="ffn",
    )(block_e, n_blocks, next_e, xs, w_gu, b_gu.reshape(E, 1, 2 * D_EXPERT), w_down, b_down.reshape(E, 1, D))


def _comb_kernel(x1_ref, gate_ref, g2_ref, fg_ref, dest_hbm, ys_hbm, o_ref, buf_a, buf_b, idx_a, idx_b,
                 sem_ia, sem_ib, sem_ra, sem_rb, *, tm, n_steps):
    i = pl.program_id(0)
    n = tm * TOP_K

    def idx_copy(tile, buf, sem):
        return pltpu.make_async_copy(dest_hbm.at[pl.ds(tile * n, n)], buf, sem)

    nch = x1_ref.shape[1] // LANES

    def row_copy(src, buf, k, t, sem):
        return pltpu.make_async_copy(ys_hbm.at[pl.ds(pl.multiple_of(src * nch, nch), nch)],
                                     buf.at[k, pl.ds(pl.multiple_of(t * nch, nch), nch)], sem)

    def issue_rows(idx, buf, sem):
        def body(t, c):
            for k in range(TOP_K):
                row_copy(idx[t * TOP_K + k], buf, k, t, sem).start(priority=k % 2)
            return c

        lax.fori_loop(0, tm, body, 0)

    def wait_rows(buf, sem):
        pltpu.make_async_copy(buf, buf, sem).wait()

    def finish(buf, lo):
        gates = gate_ref[lo:lo + tm, :]
        y = jnp.zeros((tm, x1_ref.shape[1]), F32)
        for k in range(TOP_K):
            y = y + gates[:, k:k + 1] * _load_token_major(buf.at[k], tm, nch)
        v = x1_ref[lo:lo + tm, :] + g2_ref[...] * y
        o_ref[lo:lo + tm, :] = v * lax.rsqrt(jnp.mean(v * v, axis=-1, keepdims=True) + EPS) * fg_ref[...]

    @pl.when(i == 0)
    def _():
        first = idx_copy(0, idx_a, sem_ia)
        first.start()
        first.wait()
        issue_rows(idx_a, buf_a, sem_ra)
        idx_copy(1, idx_b, sem_ib).start()

    idx_copy(2 * i + 1, idx_b, sem_ib).wait()
    issue_rows(idx_b, buf_b, sem_rb)

    @pl.when(i + 1 < n_steps)
    def _():
        idx_copy(2 * i + 2, idx_a, sem_ia).start()

    wait_rows(buf_a, sem_ra)
    finish(buf_a, 0)

    @pl.when(i + 1 < n_steps)
    def _():
        idx_copy(2 * i + 2, idx_a, sem_ia).wait()
        issue_rows(idx_a, buf_a, sem_ra)
        idx_copy(2 * i + 3, idx_b, sem_ib).start()

    wait_rows(buf_b, sem_rb)
    finish(buf_b, tm)


def _combine(x1, gate_tab, g2, final_g, dest_flat, ys, S, tm):
    N, D = x1.shape
    n_steps = N // (2 * tm)
    per_b = S // (2 * tm)
    return pl.pallas_call(
        functools.partial(_comb_kernel, tm=tm, n_steps=n_steps),
        grid=(n_steps,),
        in_specs=[pl.BlockSpec((2 * tm, D), lambda i: (i, 0)),
                  pl.BlockSpec((2 * tm, LANES), lambda i: (i, 0)),
                  pl.BlockSpec((None, 1, D), lambda i: (i // per_b, 0, 0)),
                  pl.BlockSpec((1, D), lambda i: (0, 0)),
                  pl.BlockSpec(memory_space=pl.ANY),
                  pl.BlockSpec(memory_space=pl.ANY)],
        out_specs=pl.BlockSpec((2 * tm, D), lambda i: (i, 0)),
        out_shape=jax.ShapeDtypeStruct((N, D), F32),
        scratch_shapes=[pltpu.VMEM((TOP_K, tm * (D // LANES), LANES), F32),
                        pltpu.VMEM((TOP_K, tm * (D // LANES), LANES), F32),
                        pltpu.SMEM((tm * TOP_K,), I32), pltpu.SMEM((tm * TOP_K,), I32),
                        pltpu.SemaphoreType.DMA, pltpu.SemaphoreType.DMA, pltpu.SemaphoreType.DMA,
                        pltpu.SemaphoreType.DMA],
        compiler_params=_cparams(("arbitrary",)),
        name="comb",
    )(x1, gate_tab, g2, final_g.reshape(1, D), dest_flat, ys)


def _tile(n, pref):
    t = min(pref, n)
    assert n % t == 0, (n, t)
    return t


def _layer(x, c, ada_w, ada_b, norm1_g, w_in, mix_scale, w_o, norm2_g,
           router_w, router_b, w_gu, b_gu, w_down, b_down, final_g):
    B, S, D = x.shape
    N = B * S
    mod = _mod(c, ada_w, ada_b).reshape(B, 6, 1, D)
    sh1, sc1, g1, sh2, sc2, g2 = (mod[:, j] for j in range(6))

    w_pad = jnp.pad(w_in, ((0, 0), (0, IN_COLS_PAD - IN_COLS))).astype(BF16)
    rq, rk, rv, rg, aqt, ak, avt, iqt, ik, iwt = _inproj(x, norm1_g, sc1, sh1, w_pad, _tile(S, PROJ_ROWS))
    ms = mix_scale.reshape(1, RET_W + DSA_W)
    ret = _retention(rq, rk, rv, rg, ms[:, :RET_W])
    att = _dsa(iqt, iwt, aqt, ik, ak, avt, ms[:, RET_W:], _tile(S, DSA_QUERIES), DSA_KEYS)

    rw_pad = jnp.pad(router_w, ((0, 0), (0, LANES - N_EXPERTS)))
    rw_hi = rw_pad.astype(BF16)
    rw_pad = jnp.stack([rw_hi, (rw_pad - rw_hi.astype(F32)).astype(BF16)])
    rb_pad = jnp.pad(router_b, (0, LANES - N_EXPERTS)).reshape(1, LANES)
    x1, h2, sel, idx_tab, gate_tab, counts = _oproj(ret, att, x, w_o.astype(BF16), g1, norm2_g, sc2, sh2,
                                                    rw_pad, rb_pad, _tile(S, PROJ_ROWS))

    tmb = FFN_ROWS
    n_rows = (N * TOP_K + N_EXPERTS * (tmb - 1)) // tmb * tmb + tmb
    cnt = counts[0, :N_EXPERTS].astype(I32)
    padded = (cnt + tmb - 1) // tmb * tmb
    ends = jnp.cumsum(padded)
    starts = ends - padded
    pstart = jnp.pad(starts.astype(F32), (0, LANES - N_EXPERTS)).reshape(1, LANES)
    n_blocks = (ends[-1] // tmb).reshape(1)
    first_row = jnp.arange(n_rows // tmb, dtype=I32) * tmb
    block_e = jnp.minimum(jnp.sum((ends[None, :] <= first_row[:, None]).astype(I32), axis=1), N_EXPERTS - 1)

    tmd = _tile(N, MOE_TOKENS)
    dest_tab = _dest(sel, idx_tab, pstart, tmd)
    dest_flat = dest_tab[:, :TOP_K].reshape(N * TOP_K)
    xs = _dispatch(h2, dest_flat, starts + cnt, n_rows, tmb, tmd)
    eid = jnp.arange(N_EXPERTS, dtype=I32)
    later_used = (eid[None, :] > eid[:, None]) & (padded[None, :] > 0)
    next_used = jnp.min(jnp.where(later_used, eid[None, :], N_EXPERTS), axis=1)
    next_e = jnp.where(next_used < N_EXPERTS, next_used, -1)[block_e].astype(I32)
    ys = _ffn(xs, block_e, n_blocks, next_e, w_gu, b_gu, w_down, b_down, tmb)
    out = _combine(x1, gate_tab, g2, final_g, dest_flat, ys, S, _tile(S, MOE_TOKENS))
    return out.reshape(B, S, D)


def kernel(x, c, ada_w, ada_b, norm1_g, w_in, mix_scale, w_o, norm2_g, router_w, router_b, w_gu, b_gu,
           w_down, b_down, final_g):
    assert ada_w.shape[0] == 1, "single-layer stack"
    return _layer(x, c, ada_w[0], ada_b[0], norm1_g[0], w_in[0], mix_scale[0], w_o[0], norm2_g[0],
                  router_w[0], router_b[0], w_gu[0], b_gu[0], w_down[0], b_down[0], final_g)
```

```python
import functools

import numpy as np
import jax
import jax.numpy as jnp
from jax import lax
from jax.experimental import pallas as pl
from jax.experimental.pallas import tpu as pltpu

F32 = jnp.float32
BF16 = jnp.bfloat16
I32 = jnp.int32

D_MODEL = 1024
RET_HEADS = 4
RET_DK = 64
RET_DV = 128
RET_CHUNK = 128
DSA_HEADS = 8
DSA_KV_HEADS = 2
DSA_HD = 64
IDX_HEADS = 8
IDX_HD = 64
TOPK_MAX = 256
N_EXPERTS = 32
TOP_K = 4
D_EXPERT = D_MODEL
SWIGLU_LIMIT = 7.0
SWIGLU_ALPHA = 1.702
EPS = 1e-6

RET_W = RET_HEADS * RET_DV
DSA_W = DSA_HEADS * DSA_HD
IN_COLS = 2888
IN_COLS_PAD = 2944

KAUG = 128
VAUG = 80
ALIBI_SPLIT = 64
PROJ_ROWS = 512
DSA_QUERIES = 512
DSA_KEYS = 128
MOE_TOKENS = 256
RET_CHUNKS_PER_STEP = 4
FFN_ROWS = 512
BISECT_VALUE_STEPS = 8
BISECT_MAX_STEPS = 64

LANES = 128
NCH = D_MODEL // LANES
VMEM_LIMIT = 56 * 1024 * 1024
NEG_BIG = -1e30
F32_LOWEST = float(np.finfo(np.float32).min)


def _cparams(sem):
    return pltpu.CompilerParams(dimension_semantics=sem, vmem_limit_bytes=VMEM_LIMIT)


def _mod_kernel(c_ref, w_ref, b_ref, o_ref):
    c = c_ref[...]
    s = c * (1.0 / (1.0 + jnp.exp(-c)))
    o_ref[...] = jnp.dot(s, w_ref[...], preferred_element_type=F32,
                         precision=lax.Precision.HIGHEST) + b_ref[...]


def _mod(c, ada_w, ada_b):
    B, D = c.shape
    n_out = ada_w.shape[1]
    rows = 8
    c8 = jnp.zeros((rows, D), F32).at[:B].set(c)
    out = pl.pallas_call(
        _mod_kernel,
        grid=(n_out // D,),
        in_specs=[pl.BlockSpec((rows, D), lambda j: (0, 0)),
                  pl.BlockSpec((D, D), lambda j: (0, j)),
                  pl.BlockSpec((1, D), lambda j: (0, j))],
        out_specs=pl.BlockSpec((rows, D), lambda j: (0, j)),
        out_shape=jax.ShapeDtypeStruct((rows, n_out), F32),
        compiler_params=_cparams(("arbitrary",)),
        name="mod",
    )(c8, ada_w, ada_b.reshape(1, n_out))
    return out[:B]


def _inproj_kernel(x_ref, g_ref, sc_ref, sh_ref, w_ref,
                   rq_ref, rk_ref, rv_ref, rg_ref, aqt_ref, ak_ref, avt_ref, iqt_ref, ik_ref, iwt_ref):
    x = x_ref[...]
    ms = jnp.mean(x * x, axis=-1, keepdims=True)
    y = x * lax.rsqrt(ms + EPS) * g_ref[...]
    hb = (y * (1.0 + sc_ref[...]) + sh_ref[...]).astype(BF16)

    def proj(lo, hi):
        return jnp.dot(hb, w_ref[:, lo:hi], preferred_element_type=F32)

    tm = x.shape[0]
    d = DSA_HD
    rq_ref[...] = proj(0, 256).astype(BF16)
    rk_ref[...] = (proj(256, 512) * (RET_DK ** -0.5)).astype(BF16)
    rv_ref[...] = proj(512, 1024).astype(BF16)
    rg_ref[...] = proj(1024, 1536).astype(BF16)
    aqt_ref[...] = (proj(1536, 2048) * (d ** -0.5)).T.astype(BF16)
    kk = proj(2048, 2176)
    pos = pl.program_id(1) * tm + lax.broadcasted_iota(I32, (tm, d), 0)
    col = lax.broadcasted_iota(I32, (tm, d), 1)
    posblk = jnp.where(col == 0, pos // ALIBI_SPLIT, jnp.where(col == 1, pos % ALIBI_SPLIT, 0)).astype(F32)
    for g in range(DSA_KV_HEADS):
        ak_ref[:, g * KAUG:g * KAUG + d] = kk[:, g * d:(g + 1) * d].astype(BF16)
        ak_ref[:, g * KAUG + d:(g + 1) * KAUG] = posblk.astype(BF16)
    vt = proj(2176, 2304).T
    r16 = lax.broadcasted_iota(I32, (VAUG - d, tm), 0)
    onesblk = jnp.where(r16 == 0, 1.0, 0.0).astype(BF16)
    for g in range(DSA_KV_HEADS):
        avt_ref[g * VAUG:g * VAUG + d, :] = vt[g * d:(g + 1) * d, :].astype(BF16)
        avt_ref[g * VAUG + d:(g + 1) * VAUG, :] = onesblk
    iqt_ref[...] = proj(2304, 2816).T.astype(BF16)
    last = proj(2816, 2944)
    ik_ref[...] = last[:, :IDX_HD].astype(BF16)
    iwt_ref[...] = last.T[IDX_HD:IDX_HD + IDX_HEADS, :] * ((IDX_HD ** -0.5) * (IDX_HEADS ** -0.5))


def _inproj(x, norm_g, sc, sh, w_pad, tm):
    B, S, D = x.shape
    row = lambda w: pl.BlockSpec((None, tm, w), lambda b, i: (b, i, 0))
    colT = lambda h: pl.BlockSpec((None, h, tm), lambda b, i: (b, 0, i))
    vec = pl.BlockSpec((None, 1, D), lambda b, i: (b, 0, 0))
    sd = lambda shape, dt: jax.ShapeDtypeStruct(shape, dt)
    G = DSA_KV_HEADS
    return pl.pallas_call(
        _inproj_kernel,
        grid=(B, S // tm),
        in_specs=[row(D), pl.BlockSpec((1, D), lambda b, i: (0, 0)), vec, vec,
                  pl.BlockSpec((D, IN_COLS_PAD), lambda b, i: (0, 0))],
        out_specs=[row(256), row(256), row(512), row(512), colT(DSA_W), row(G * KAUG), colT(G * VAUG),
                   colT(IDX_HEADS * IDX_HD), row(IDX_HD), colT(IDX_HEADS)],
        out_shape=[sd((B, S, 256), BF16), sd((B, S, 256), BF16), sd((B, S, 512), BF16),
                   sd((B, S, 512), BF16), sd((B, DSA_W, S), BF16), sd((B, S, G * KAUG), BF16),
                   sd((B, G * VAUG, S), BF16), sd((B, IDX_HEADS * IDX_HD, S), BF16),
                   sd((B, S, IDX_HD), BF16), sd((B, IDX_HEADS, S), F32)],
        compiler_params=_cparams(("parallel", "parallel")),
        name="inproj",
    )(x, norm_g.reshape(1, D), sc, sh, w_pad)


def _ret_kernel(rq_ref, rk_ref, rv_ref, rg_ref, din_ref, qd_ref, kd_ref, cd_ref, ms_ref, o_ref, state_ref):
    @pl.when(pl.program_id(1) == 0)
    def _():
        state_ref[...] = jnp.zeros_like(state_ref)

    C = din_ref.shape[1]
    for c in range(rq_ref.shape[0] // C):
        rows = slice(c * C, (c + 1) * C)
        for h in range(RET_HEADS):
            q = rq_ref[rows, h * RET_DK:(h + 1) * RET_DK]
            k = rk_ref[rows, h * RET_DK:(h + 1) * RET_DK]
            v = rv_ref[rows, h * RET_DV:(h + 1) * RET_DV]
            r_prev = state_ref[h]
            s = lax.dot_general(q, k, (((1,), (1,)), ((), ())), preferred_element_type=F32) * din_ref[h]
            o = jnp.dot(s.astype(BF16), v, preferred_element_type=F32)
            o = o + jnp.dot(q, r_prev.astype(BF16), preferred_element_type=F32) * qd_ref[h]
            vd = (v.astype(F32) * kd_ref[h]).astype(BF16)
            kv = lax.dot_general(k, vd, (((0,), (0,)), ((), ())), preferred_element_type=F32)
            state_ref[h] = r_prev * cd_ref[h] + kv
            o = o * lax.rsqrt(jnp.mean(o * o, axis=-1, keepdims=True) + EPS)
            g = rg_ref[rows, h * RET_DV:(h + 1) * RET_DV].astype(F32)
            gate = g * (1.0 / (1.0 + jnp.exp(-g)))
            o_ref[rows, h * RET_DV:(h + 1) * RET_DV] = (
                gate * o * ms_ref[:, h * RET_DV:(h + 1) * RET_DV]).astype(BF16)


def _ret_consts(C):
    H = RET_HEADS
    log_g = np.log1p(-np.exp2(-5.0 - np.arange(H, dtype=np.float64)))
    pos = np.arange(C, dtype=np.float64)
    diff = pos[:, None] - pos[None, :]
    d_inner = np.where(diff[None] >= 0, np.exp(np.maximum(diff, 0.0)[None] * log_g[:, None, None]), 0.0)
    q_decay = np.exp((pos + 1.0)[None] * log_g[:, None])
    k_decay = np.exp((C - 1.0 - pos)[None] * log_g[:, None])
    chunk_decay = np.exp(C * log_g)
    qd = np.broadcast_to(q_decay[:, :, None], (H, C, RET_DV))
    kd = np.broadcast_to(k_decay[:, :, None], (H, C, RET_DV))
    cd = np.broadcast_to(chunk_decay[:, None, None], (H, 1, RET_DV))
    f = lambda a: jnp.asarray(np.ascontiguousarray(a), F32)
    return f(d_inner), f(qd), f(kd), f(cd)


def _retention(rq, rk, rv, rg, ms_ret):
    B, S, _ = rq.shape
    C = min(RET_CHUNK, S)
    din, qd, kd, cd = _ret_consts(C)
    rows = _tile(S, RET_CHUNKS_PER_STEP * C)
    row = lambda w: pl.BlockSpec((None, rows, w), lambda b, n: (b, n, 0))
    full = lambda a: pl.BlockSpec(a.shape, lambda b, n: (0,) * a.ndim)
    return pl.pallas_call(
        _ret_kernel,
        grid=(B, S // rows),
        in_specs=[row(256), row(256), row(512), row(512), full(din), full(qd), full(kd), full(cd),
                  pl.BlockSpec((1, RET_W), lambda b, n: (0, 0))],
        out_specs=row(RET_W),
        out_shape=jax.ShapeDtypeStruct((B, S, RET_W), BF16),
        scratch_shapes=[pltpu.VMEM((RET_HEADS, RET_DK, RET_DV), F32)],
        compiler_params=_cparams(("parallel", "arbitrary")),
        name="ret",
    )(rq, rk, rv, rg, din, qd, kd, cd, ms_ret)


def _f32_key(x):
    i = lax.bitcast_convert_type(x, I32)
    return i ^ ((i >> 31) & 0x7FFFFFFF)


def _key_f32(k):
    return lax.bitcast_convert_type(k ^ ((k >> 31) & 0x7FFFFFFF), F32)


def _dsa_kernel(iqt_ref, iwt_ref, aqt_ref, ik_ref, ak_ref, avt_ref, ms_ref, o_ref, score_ref, qa_ref, sa_ref, sb_ref, mask_ref, *acc_refs,
                tq, tks, n_sel):
    H, G, R, d = DSA_HEADS, DSA_KV_HEADS, DSA_HEADS // DSA_KV_HEADS, DSA_HD
    t0 = pl.program_id(1) * tq
    nsub = (t0 + tq) // tks
    tka = 2 * tks
    npair = ((t0 + tq) // tka + 1) // 2
    kf = float(n_sel)
    qpos = t0 + lax.broadcasted_iota(I32, (1, tq), 1)
    krow = lax.broadcasted_iota(I32, (tks, tq), 0)

    wrow = [iwt_ref[h:h + 1, :] for h in range(IDX_HEADS)]

    def fold8(x, op):
        acc = x[0:8, :]
        for i in range(1, tks // 8):
            acc = op(acc, x[8 * i:8 * (i + 1), :])
        return acc

    def score_pair(i, carry):
        mx, mn, npos, nnon = carry
        for u in range(2 * tka // tks):
            r0 = pl.multiple_of(i * 2 * tka + u * tks, tks)
            kc = ik_ref[pl.ds(r0, tks), :]
            acc = jnp.zeros((tks, tq), F32)
            for h in range(IDX_HEADS):
                rel = jnp.dot(kc, iqt_ref[h * IDX_HD:(h + 1) * IDX_HD, :], preferred_element_type=F32)
                acc = acc + jnp.maximum(rel, 0.0) * wrow[h]
            causal = r0 + krow <= qpos
            sc = jnp.where(causal, acc, -jnp.inf)
            score_ref[pl.ds(r0, tks), :] = sc
            mx = jnp.maximum(mx, fold8(sc, jnp.maximum))
            mn = jnp.minimum(mn, fold8(jnp.where(causal, acc, jnp.inf), jnp.minimum))
            npos = npos + fold8(jnp.where(sc > 0.0, 1.0, 0.0), jnp.add)
            nnon = nnon + fold8(jnp.where(sc >= 0.0, 1.0, 0.0), jnp.add)
        return mx, mn, npos, nnon

    stat0 = (jnp.full((8, tq), -jnp.inf, F32), jnp.full((8, tq), jnp.inf, F32),
             jnp.zeros((8, tq), F32), jnp.zeros((8, tq), F32))
    mx, mn, npos, nnon = lax.fori_loop(0, npair, score_pair, stat0)
    top = jnp.max(mx, axis=0, keepdims=True)
    lo0 = jnp.min(mn, axis=0, keepdims=True)
    n_pos = jnp.sum(npos, axis=0, keepdims=True)
    n_nonneg = jnp.sum(nnon, axis=0, keepdims=True)


    def count(th, strict):
        def body(j, acc):
            for u in range(2):
                s = score_ref[pl.ds(pl.multiple_of((2 * j + u) * tks, tks), tks), :]
                hit = (s > th) if strict else (s >= th)
                acc = acc + fold8(jnp.where(hit, 1.0, 0.0), jnp.add)
            return acc

        acc = lax.fori_loop(0, nsub // 2, body, jnp.zeros((8, tq), F32))
        return jnp.sum(acc, axis=0, keepdims=True)

    def probe(lo, hi, it):
        lk, hk = _f32_key(lo), _f32_key(hi)
        mk = (lk >> 1) + (hk >> 1) + (lk & hk & 1)
        mv = lo + (hi - lo) * 0.5
        early = (jnp.zeros((1, tq), I32) + it) < BISECT_VALUE_STEPS
        mid = jnp.where(early & (mv > lo) & (mv < hi), mv, _key_f32(mk))
        return mid, jnp.max(jnp.where(mk != lk, 1.0, 0.0))

    def bis_cond(c):
        return (c[5] > 0.0) & (c[6] < BISECT_MAX_STEPS)

    def bis_body(c):
        lo, hi, c_lo, c_hi, mid, _, it = c
        cnt = count(mid, False)
        ge = cnt >= kf
        up = ge | (cnt == kf)
        dn = (~ge) | (cnt == kf)
        lo, c_lo = jnp.where(up, mid, lo), jnp.where(up, cnt, c_lo)
        hi, c_hi = jnp.where(dn, mid, hi), jnp.where(dn, cnt, c_hi)
        mid, active = probe(lo, hi, it + 1)
        return lo, hi, c_lo, c_hi, mid, active, it + 1

    zero = jnp.zeros((1, tq), F32)
    keep_all = qpos + 1 <= n_sel
    settled = keep_all | ((n_nonneg >= kf) & (n_pos < kf))
    above = n_pos >= kf
    c_lo0 = jnp.where(settled | above, n_nonneg, (qpos + 1).astype(F32))
    c_hi0 = jnp.where(settled | ~above, n_nonneg, zero)
    lo0 = jnp.where(settled | above, zero, lo0)
    hi0 = jnp.where(settled | ~above, zero, _key_f32(_f32_key(top) + 1))
    mid0, active0 = probe(lo0, hi0, jnp.int32(0))
    lo, hi, c_lo, c_hi, _, _, _ = lax.while_loop(bis_cond, bis_body,
                                                 (lo0, hi0, c_lo0, c_hi0, mid0, active0, jnp.int32(0)))
    at_hi = c_hi >= kf
    thr = jnp.where(keep_all, F32_LOWEST, jnp.where(at_hi, hi, lo))
    excess = jnp.where(keep_all, 0.0, jnp.where(at_hi, c_hi, c_lo) - kf)

    @pl.when(jnp.max(excess) > 0.0)
    def _():
        tied_nonzero = jnp.max(jnp.where((excess > 0.0) & (thr != 0.0), 1.0, 0.0)) > 0.0
        n_above = lax.cond(tied_nonzero, lambda: count(thr, True), lambda: n_pos)
        budget = jnp.where(excess > 0.0, kf - jnp.where(thr == 0.0, n_pos, n_above), jnp.inf)
        earlier = lax.broadcasted_iota(I32, (tks, tks), 1) < lax.broadcasted_iota(I32, (tks, tks), 0)
        earlier = jnp.where(earlier, 1.0, 0.0).astype(BF16)

        def fix(j, seen):
            tiles = []
            for u in range(2):
                r0 = pl.multiple_of((2 * j + u) * tks, tks)
                s = score_ref[pl.ds(r0, tks), :]
                eq = s == thr
                eqf = jnp.where(eq, 1.0, 0.0)
                within = jnp.dot(earlier, eqf.astype(BF16), preferred_element_type=F32)
                tiles.append((r0, s, eq, within, jnp.sum(eqf, axis=0, keepdims=True)))
            for r0, s, eq, within, n_eq in tiles:
                score_ref[pl.ds(r0, tks), :] = jnp.where(eq & (within + seen >= budget), -jnp.inf, s)
                seen = seen + n_eq
            return seen

        lax.fori_loop(0, nsub // 2, fix, jnp.zeros((1, tq), F32))

    arow = lax.broadcasted_iota(I32, (KAUG - d, tq), 0)
    for h in range(H):
        slope = float(2.0 ** (-8.0 * (h + 1) / H))
        qa_ref[h, 0:d, :] = aqt_ref[h * d:(h + 1) * d, :]
        qa_ref[h, d:KAUG, :] = jnp.where(arow == 0, slope * ALIBI_SPLIT, jnp.where(arow == 1, slope, 0.0)).astype(BF16)
    for acc in acc_refs:
        acc[...] = jnp.zeros_like(acc)

    def logits(j, h):
        ka = ak_ref[pl.ds(pl.multiple_of(j * tka, tka), tka), (h // R) * KAUG:(h // R + 1) * KAUG]
        return jnp.dot(ka, qa_ref[h], preferred_element_type=F32)

    def step(j, j_next, cur_ref, next_ref, ms):
        r0 = pl.multiple_of(j * tka, tka)
        mask_ref[...] = jnp.where(score_ref[pl.ds(r0, tka), :] >= thr, 0.0, NEG_BIG)
        nms = []
        for h in range(H):
            g = h // R
            next_ref[h] = logits(j_next, h)
            s = cur_ref[h] + mask_ref[...]
            m_new = jnp.maximum(ms[h], jnp.max(s, axis=0, keepdims=True))
            p = jnp.exp(s - m_new).astype(BF16)
            va = avt_ref[g * VAUG:(g + 1) * VAUG, pl.ds(r0, tka)]
            acc = acc_refs[h]
            acc[...] = acc[...] * jnp.exp(ms[h] - m_new) + jnp.dot(va, p, preferred_element_type=F32)
            nms.append(m_new)
        return tuple(nms)

    for h in range(H):
        sa_ref[h] = logits(0, h)

    def att_pair(i, ms):
        ms = step(2 * i, 2 * i + 1, sa_ref, sb_ref, ms)
        return step(2 * i + 1, jnp.minimum(2 * i + 2, 2 * npair - 1), sb_ref, sa_ref, ms)

    lax.fori_loop(0, npair, att_pair, tuple(jnp.full((1, tq), NEG_BIG, F32) for _ in range(H)))
    for h in range(H):
        a = acc_refs[h][...]
        o = a[0:d, :] / a[d:d + 1, :]
        o_ref[h * d:(h + 1) * d, :] = (o * ms_ref[h * d:(h + 1) * d, :]).astype(BF16)


def _dsa(iqt, iwt, aqt, ik, ak, avt, ms_att, tq, tks):
    B, _, S = iqt.shape
    assert S <= ALIBI_SPLIT * 256 and tq % (2 * tks) == 0
    assert (S // (2 * tks)) % 2 == 0
    n_sel = min(TOPK_MAX, S // 4)
    G = DSA_KV_HEADS
    colT = lambda h: pl.BlockSpec((None, h, tq), lambda b, i: (b, 0, i))
    msb = jnp.broadcast_to(ms_att.reshape(DSA_W, 1), (DSA_W, tq))
    return pl.pallas_call(
        functools.partial(_dsa_kernel, tq=tq, tks=tks, n_sel=n_sel),
        grid=(B, S // tq),
        in_specs=[colT(IDX_HEADS * IDX_HD), colT(IDX_HEADS), colT(DSA_W),
                  pl.BlockSpec((None, S, IDX_HD), lambda b, i: (b, 0, 0)),
                  pl.BlockSpec((None, S, G * KAUG), lambda b, i: (b, 0, 0)),
                  pl.BlockSpec((None, G * VAUG, S), lambda b, i: (b, 0, 0)),
                  pl.BlockSpec((DSA_W, tq), lambda b, i: (0, 0))],
        out_specs=colT(DSA_W),
        out_shape=jax.ShapeDtypeStruct((B, DSA_W, S), BF16),
        scratch_shapes=[pltpu.VMEM((S, tq), F32), pltpu.VMEM((DSA_HEADS, KAUG, tq), BF16),
                        pltpu.VMEM((DSA_HEADS, 2 * tks, tq), F32), pltpu.VMEM((DSA_HEADS, 2 * tks, tq), F32),
                        pltpu.VMEM((2 * tks, tq), F32)]
        + [pltpu.VMEM((VAUG, tq), F32) for _ in range(DSA_HEADS)],
        compiler_params=_cparams(("parallel", "arbitrary")),
        name="dsa",
    )(iqt, iwt, aqt, ik, ak, avt, msb)


def _oproj_kernel(ret_ref, att_ref, x_ref, wo_ref, g1_ref, n2_ref, sc_ref, sh_ref, rw_ref, rb_ref,
                  x1_ref, h2_ref, sel_ref, idx_ref, gate_ref, cnt_ref):
    mixo = jnp.dot(ret_ref[...], wo_ref[:RET_W, :], preferred_element_type=F32)
    mixo = mixo + lax.dot_general(att_ref[...], wo_ref[RET_W:, :], (((0,), (0,)), ((), ())),
                                  preferred_element_type=F32)
    x1 = x_ref[...] + g1_ref[...] * mixo
    x1_ref[...] = x1
    y = x1 * lax.rsqrt(jnp.mean(x1 * x1, axis=-1, keepdims=True) + EPS) * n2_ref[...]
    h2 = y * (1.0 + sc_ref[...]) + sh_ref[...]
    _store_token_major(h2_ref, h2)
    h_hi = h2.astype(BF16)
    h_lo = (h2 - h_hi.astype(F32)).astype(BF16)
    logits = jnp.dot(h_hi, rw_ref[0], preferred_element_type=F32)
    logits = logits + (jnp.dot(h_hi, rw_ref[1], preferred_element_type=F32)
                       + jnp.dot(h_lo, rw_ref[0], preferred_element_type=F32)) + rb_ref[...]
    tm = logits.shape[0]
    lane = lax.broadcasted_iota(I32, (tm, LANES), 1).astype(F32)
    work = jnp.where(lane < N_EXPERTS, logits, -jnp.inf)
    sel = jnp.zeros((tm, LANES), F32)
    idx_tab = jnp.zeros((tm, LANES), F32)
    vals = []
    for k in range(TOP_K):
        m = jnp.max(work, axis=1, keepdims=True)
        idx = jnp.min(jnp.where(work == m, lane, float(LANES)), axis=1, keepdims=True)
        hit = lane == idx
        sel = jnp.where(hit, 1.0, sel)
        idx_tab = jnp.where(lane == k, idx, idx_tab)
        work = jnp.where(hit, -jnp.inf, work)
        vals.append(m)
    es = [jnp.exp(v - vals[0]) for v in vals]
    den = es[0] + es[1] + es[2] + es[3]
    gate_tab = jnp.zeros((tm, LANES), F32)
    for k in range(TOP_K):
        gate_tab = jnp.where(lane == k, es[k] / den, gate_tab)
    sel_ref[...] = sel
    idx_ref[...] = idx_tab
    gate_ref[...] = gate_tab

    @pl.when((pl.program_id(0) == 0) & (pl.program_id(1) == 0))
    def _():
        cnt_ref[...] = jnp.zeros_like(cnt_ref)

    cnt_ref[...] += jnp.sum(sel, axis=0, keepdims=True)


def _oproj(ret, att, x, wo, g1, n2g, sc2, sh2, rw_pad, rb_pad, tm):
    B, S, D = x.shape
    nt = S // tm
    row = lambda w: pl.BlockSpec((None, tm, w), lambda b, i: (b, i, 0))
    flat = lambda w: pl.BlockSpec((tm, w), lambda b, i: (b * nt + i, 0))
    vec = pl.BlockSpec((None, 1, D), lambda b, i: (b, 0, 0))
    cst = lambda shape: pl.BlockSpec(shape, lambda b, i: (0, 0))
    sd = lambda shape, dt: jax.ShapeDtypeStruct(shape, dt)
    N = B * S
    return pl.pallas_call(
        _oproj_kernel,
        grid=(B, nt),
        in_specs=[row(RET_W), pl.BlockSpec((None, DSA_W, tm), lambda b, i: (b, 0, i)), row(D), cst((D, D)), vec,
                  cst((1, D)), vec, vec,
                  pl.BlockSpec((2, D, LANES), lambda b, i: (0, 0, 0)), cst((1, LANES))],
        out_specs=[flat(D), pl.BlockSpec((tm * (D // LANES), LANES), lambda b, i: (b * nt + i, 0)),
                   flat(LANES), flat(LANES), flat(LANES), cst((1, LANES))],
        out_shape=[sd((N, D), F32), sd((N * (D // LANES), LANES), F32), sd((N, LANES), F32), sd((N, LANES), F32),
                   sd((N, LANES), F32), sd((1, LANES), F32)],
        compiler_params=_cparams(("arbitrary", "arbitrary")),
        name="oproj",
    )(ret, att, x, wo, g1, n2g.reshape(1, D), sc2, sh2, rw_pad, rb_pad)


def _dest_kernel(sel_ref, idx_ref, pstart_ref, dest_ref, seen_ref):
    @pl.when(pl.program_id(0) == 0)
    def _():
        seen_ref[...] = jnp.zeros_like(seen_ref)

    sel = sel_ref[...]
    tm = sel.shape[0]
    earlier = lax.broadcasted_iota(I32, (tm, tm), 1) < lax.broadcasted_iota(I32, (tm, tm), 0)
    earlier = jnp.where(earlier, 1.0, 0.0).astype(BF16)
    rank = jnp.dot(earlier, sel.astype(BF16), preferred_element_type=F32) + seen_ref[...]
    dest = pstart_ref[...] + rank
    lane = lax.broadcasted_iota(I32, (tm, LANES), 1).astype(F32)
    idx_tab = idx_ref[...]
    out = jnp.zeros((tm, LANES), F32)
    for k in range(TOP_K):
        e_k = jnp.sum(jnp.where(lane == k, idx_tab, 0.0), axis=1, keepdims=True)
        d_k = jnp.sum(jnp.where(lane == e_k, dest, 0.0), axis=1, keepdims=True)
        out = jnp.where(lane == k, d_k, out)
    dest_ref[...] = out.astype(I32)
    seen_ref[...] += jnp.sum(sel, axis=0, keepdims=True)


def _dest(sel, idx_tab, pstart, tm):
    N = sel.shape[0]
    blk = pl.BlockSpec((tm, LANES), lambda i: (i, 0))
    return pl.pallas_call(
        _dest_kernel,
        grid=(N // tm,),
        in_specs=[blk, blk, pl.BlockSpec((1, LANES), lambda i: (0, 0))],
        out_specs=blk,
        out_shape=jax.ShapeDtypeStruct((N, LANES), I32),
        scratch_shapes=[pltpu.VMEM((1, LANES), F32)],
        compiler_params=_cparams(("arbitrary",)),
        name="dest",
    )(sel, idx_tab, pstart)


def _disp_kernel(zs_ref, h2_ref, dest_hbm, xs_hbm, zbuf, idx_a, idx_b, sem_ia, sem_ib, sem_row, sem_z,
                 *, tm, nch, zrows, n_steps):
    i = pl.program_id(0)
    n = tm * TOP_K

    def idx_copy(tile, buf, sem):
        return pltpu.make_async_copy(dest_hbm.at[pl.ds(tile * n, n)], buf, sem)

    def rows(ref, first, count):
        return ref.at[pl.ds(pl.multiple_of(first * nch, nch), count * nch)]

    def row_copy(row, dst):
        return pltpu.make_async_copy(rows(h2_ref, row, 1), rows(xs_hbm, dst, 1), sem_row)

    def issue_rows(first_row, idx):
        def body(t, c):
            for k in range(TOP_K):
                row_copy(first_row + t, idx[t * TOP_K + k]).start(priority=k % 2)
            return c

        lax.fori_loop(0, tm, body, 0)

    def wait_rows():
        pltpu.make_async_copy(rows(xs_hbm, 0, n), rows(xs_hbm, 0, n), sem_row).wait()

    @pl.when(i == 0)
    def _():
        zbuf[...] = jnp.zeros_like(zbuf)
        for e in range(N_EXPERTS):
            fill = pltpu.make_async_copy(zbuf, rows(xs_hbm, zs_ref[e], zrows), sem_z)
            fill.start()
            fill.wait()
        idx_copy(0, idx_a, sem_ia).start()
        idx_copy(1, idx_b, sem_ib).start()

    idx_copy(2 * i, idx_a, sem_ia).wait()
    issue_rows(0, idx_a)
    idx_copy(2 * i + 1, idx_b, sem_ib).wait()
    issue_rows(tm, idx_b)

    @pl.when(i + 1 < n_steps)
    def _():
        idx_copy(2 * i + 2, idx_a, sem_ia).start()
        idx_copy(2 * i + 3, idx_b, sem_ib).start()

    wait_rows()
    wait_rows()


def _dispatch(h2, dest_flat, zero_start, n_rows, zrows, tm):
    nch = D_MODEL // LANES
    n_steps = h2.shape[0] // nch // (2 * tm)
    grid_spec = pltpu.PrefetchScalarGridSpec(
        num_scalar_prefetch=1,
        grid=(n_steps,),
        in_specs=[pl.BlockSpec((2 * tm * nch, LANES), lambda i, zs: (i, 0)), pl.BlockSpec(memory_space=pl.ANY)],
        out_specs=pl.BlockSpec(memory_space=pl.ANY),
        scratch_shapes=[pltpu.VMEM((zrows * nch, LANES), F32), pltpu.SMEM((tm * TOP_K,), I32),
                        pltpu.SMEM((tm * TOP_K,), I32), pltpu.SemaphoreType.DMA, pltpu.SemaphoreType.DMA,
                        pltpu.SemaphoreType.DMA, pltpu.SemaphoreType.DMA],
    )
    return pl.pallas_call(
        functools.partial(_disp_kernel, tm=tm, nch=nch, zrows=zrows, n_steps=n_steps),
        grid_spec=grid_spec,
        out_shape=jax.ShapeDtypeStruct((n_rows * nch, LANES), F32),
        compiler_params=_cparams(("arbitrary",)),
        name="disp",
    )(zero_start, h2, dest_flat)


def _store_token_major(ref, x):
    rows, d = x.shape
    nch = d // LANES
    for j in range(nch):
        ref[pl.ds(j, rows, stride=nch), :] = x[:, j * LANES:(j + 1) * LANES]


def _load_token_major(ref, rows, nch):
    return jnp.concatenate([ref[pl.ds(j, rows, stride=nch), :] for j in range(nch)], axis=1)


def _ffn_kernel(be_ref, nb_ref, nxt_ref, xs_ref, wgu_hbm, bgu_ref, wd_hbm, bd_ref, ys_ref,
                wgu_st, wd_st, wgu_bf, wd_bf, sem_gu, sem_d):
    i = pl.program_id(0)
    live = i < nb_ref[0]
    e = be_ref[i]

    def fetch(ex):
        return (pltpu.make_async_copy(wgu_hbm.at[ex], wgu_st, sem_gu),
                pltpu.make_async_copy(wd_hbm.at[ex], wd_st, sem_d))

    @pl.when(i == 0)
    def _():
        for cp in fetch(e):
            cp.start()

    @pl.when(live & ((i == 0) | (e != be_ref[jnp.maximum(i - 1, 0)])))
    def _():
        for cp in fetch(e):
            cp.wait()
        wgu_bf[...] = wgu_st[...].astype(BF16)
        wd_bf[...] = wd_st[...].astype(BF16)

        @pl.when(nxt_ref[i] >= 0)
        def _():
            for cp in fetch(nxt_ref[i]):
                cp.start()

    @pl.when(live)
    def _():
        xb = _load_token_major(xs_ref, ys_ref.shape[0] // NCH, NCH).astype(BF16)
        gu = jnp.dot(xb, wgu_bf[...], preferred_element_type=F32) + bgu_ref[...]
        gate = jnp.minimum(gu[:, :D_EXPERT], SWIGLU_LIMIT)
        up = jnp.clip(gu[:, D_EXPERT:], -SWIGLU_LIMIT, SWIGLU_LIMIT)
        glu = gate * (1.0 / (1.0 + jnp.exp(-SWIGLU_ALPHA * gate)))
        act = ((up + 1.0) * glu).astype(BF16)
        ys = jnp.dot(act, wd_bf[...], preferred_element_type=F32) + bd_ref[...]
        _store_token_major(ys_ref, ys)

    @pl.when(jnp.logical_not(live))
    def _():
        ys_ref[...] = jnp.zeros_like(ys_ref)


def _ffn(xs, block_e, n_blocks, next_e, w_gu, b_gu, w_down, b_down, tmb):
    D = D_MODEL
    P = xs.shape[0] // NCH
    E = w_gu.shape[0]
    blk = lambda i, be, nb, nx: (jnp.minimum(i, nb[0] - 1), 0)
    wsel = lambda i, be, nb, nx: (be[jnp.minimum(i, nb[0] - 1)], 0, 0)
    grid_spec = pltpu.PrefetchScalarGridSpec(
        num_scalar_prefetch=3,
        grid=(P // tmb,),
        in_specs=[pl.BlockSpec((tmb * NCH, LANES), blk),
                  pl.BlockSpec(memory_space=pl.ANY),
                  pl.BlockSpec((None, 1, 2 * D_EXPERT), wsel),
                  pl.BlockSpec(memory_space=pl.ANY),
                  pl.BlockSpec((None, 1, D), wsel)],
        out_specs=pl.BlockSpec((tmb * (D // LANES), LANES), lambda i, be, nb, nx: (i, 0)),
        scratch_shapes=[pltpu.VMEM((D, 2 * D_EXPERT), F32), pltpu.VMEM((D_EXPERT, D), F32),
                        pltpu.VMEM((D, 2 * D_EXPERT), BF16), pltpu.VMEM((D_EXPERT, D), BF16),
                        pltpu.SemaphoreType.DMA, pltpu.SemaphoreType.DMA],
    )
    return pl.pallas_call(
        _ffn_kernel,
        grid_spec=grid_spec,
        out_shape=jax.ShapeDtypeStruct((P * (D // LANES), LANES), F32),
        compiler_params=_cparams(("arbitrary",)),
        name="ffn",
    )(block_e, n_blocks, next_e, xs, w_gu, b_gu.reshape(E, 1, 2 * D_EXPERT), w_down, b_down.reshape(E, 1, D))


def _comb_kernel(x1_ref, gate_ref, g2_ref, fg_ref, dest_hbm, ys_hbm, o_ref, buf_a, buf_b, idx_a, idx_b,
                 sem_ia, sem_ib, sem_ra, sem_rb, *, tm, n_steps):
    i = pl.program_id(0)
    n = tm * TOP_K

    def idx_copy(tile, buf, sem):
        return pltpu.make_async_copy(dest_hbm.at[pl.ds(tile * n, n)], buf, sem)

    nch = x1_ref.shape[1] // LANES

    def row_copy(src, buf, k, t, sem):
        return pltpu.make_async_copy(ys_hbm.at[pl.ds(pl.multiple_of(src * nch, nch), nch)],
                                     buf.at[k, pl.ds(pl.multiple_of(t * nch, nch), nch)], sem)

    def issue_rows(idx, buf, sem):
        def body(t, c):
            for k in range(TOP_K):
                row_copy(idx[t * TOP_K + k], buf, k, t, sem).start(priority=k % 2)
            return c

        lax.fori_loop(0, tm, body, 0)

    def wait_rows(buf, sem):
        pltpu.make_async_copy(buf, buf, sem).wait()

    def finish(buf, lo):
        gates = gate_ref[lo:lo + tm, :]
        y = jnp.zeros((tm, x1_ref.shape[1]), F32)
        for k in range(TOP_K):
            y = y + gates[:, k:k + 1] * _load_token_major(buf.at[k], tm, nch)
        v = x1_ref[lo:lo + tm, :] + g2_ref[...] * y
        o_ref[lo:lo + tm, :] = v * lax.rsqrt(jnp.mean(v * v, axis=-1, keepdims=True) + EPS) * fg_ref[...]

    @pl.when(i == 0)
    def _():
        first = idx_copy(0, idx_a, sem_ia)
        first.start()
        first.wait()
        issue_rows(idx_a, buf_a, sem_ra)
        idx_copy(1, idx_b, sem_ib).start()

    idx_copy(2 * i + 1, idx_b, sem_ib).wait()
    issue_rows(idx_b, buf_b, sem_rb)

    @pl.when(i + 1 < n_steps)
    def _():
        idx_copy(2 * i + 2, idx_a, sem_ia).start()

    wait_rows(buf_a, sem_ra)
    finish(buf_a, 0)

    @pl.when(i + 1 < n_steps)
    def _():
        idx_copy(2 * i + 2, idx_a, sem_ia).wait()
        issue_rows(idx_a, buf_a, sem_ra)
        idx_copy(2 * i + 3, idx_b, sem_ib).start()

    wait_rows(buf_b, sem_rb)
    finish(buf_b, tm)


def _combine(x1, gate_tab, g2, final_g, dest_flat, ys, S, tm):
    N, D = x1.shape
    n_steps = N // (2 * tm)
    per_b = S // (2 * tm)
    return pl.pallas_call(
        functools.partial(_comb_kernel, tm=tm, n_steps=n_steps),
        grid=(n_steps,),
        in_specs=[pl.BlockSpec((2 * tm, D), lambda i: (i, 0)),
                  pl.BlockSpec((2 * tm, LANES), lambda i: (i, 0)),
                  pl.BlockSpec((None, 1, D), lambda i: (i // per_b, 0, 0)),
                  pl.BlockSpec((1, D), lambda i: (0, 0)),
                  pl.BlockSpec(memory_space=pl.ANY),
                  pl.BlockSpec(memory_space=pl.ANY)],
        out_specs=pl.BlockSpec((2 * tm, D), lambda i: (i, 0)),
        out_shape=jax.ShapeDtypeStruct((N, D), F32),
        scratch_shapes=[pltpu.VMEM((TOP_K, tm * (D // LANES), LANES), F32),
                        pltpu.VMEM((TOP_K, tm * (D // LANES), LANES), F32),
                        pltpu.SMEM((tm * TOP_K,), I32), pltpu.SMEM((tm * TOP_K,), I32),
                        pltpu.SemaphoreType.DMA, pltpu.SemaphoreType.DMA, pltpu.SemaphoreType.DMA,
                        pltpu.SemaphoreType.DMA],
        compiler_params=_cparams(("arbitrary",)),
        name="comb",
    )(x1, gate_tab, g2, final_g.reshape(1, D), dest_flat, ys)


def _tile(n, pref):
    t = min(pref, n)
    assert n % t == 0, (n, t)
    return t


def _layer(x, c, ada_w, ada_b, norm1_g, w_in, mix_scale, w_o, norm2_g,
           router_w, router_b, w_gu, b_gu, w_down, b_down, final_g):
    B, S, D = x.shape
    N = B * S
    mod = _mod(c, ada_w, ada_b).reshape(B, 6, 1, D)
    sh1, sc1, g1, sh2, sc2, g2 = (mod[:, j] for j in range(6))

    w_pad = jnp.pad(w_in, ((0, 0), (0, IN_COLS_PAD - IN_COLS))).astype(BF16)
    rq, rk, rv, rg, aqt, ak, avt, iqt, ik, iwt = _inproj(x, norm1_g, sc1, sh1, w_pad, _tile(S, PROJ_ROWS))
    ms = mix_scale.reshape(1, RET_W + DSA_W)
    ret = _retention(rq, rk, rv, rg, ms[:, :RET_W])
    att = _dsa(iqt, iwt, aqt, ik, ak, avt, ms[:, RET_W:], _tile(S, DSA_QUERIES), DSA_KEYS)

    rw_pad = jnp.pad(router_w, ((0, 0), (0, LANES - N_EXPERTS)))
    rw_hi = rw_pad.astype(BF16)
    rw_pad = jnp.stack([rw_hi, (rw_pad - rw_hi.astype(F32)).astype(BF16)])
    rb_pad = jnp.pad(router_b, (0, LANES - N_EXPERTS)).reshape(1, LANES)
    x1, h2, sel, idx_tab, gate_tab, counts = _oproj(ret, att, x, w_o.astype(BF16), g1, norm2_g, sc2, sh2,
                                                    rw_pad, rb_pad, _tile(S, PROJ_ROWS))

    tmb = FFN_ROWS
    n_rows = (N * TOP_K + N_EXPERTS * (tmb - 1)) // tmb * tmb + tmb
    cnt = counts[0, :N_EXPERTS].astype(I32)
    padded = (cnt + tmb - 1) // tmb * tmb
    ends = jnp.cumsum(padded)
    starts = ends - padded
    pstart = jnp.pad(starts.astype(F32), (0, LANES - N_EXPERTS)).reshape(1, LANES)
    n_blocks = (ends[-1] // tmb).reshape(1)
    first_row = jnp.arange(n_rows // tmb, dtype=I32) * tmb
    block_e = jnp.minimum(jnp.sum((ends[None, :] <= first_row[:, None]).astype(I32), axis=1), N_EXPERTS - 1)

    tmd = _tile(N, MOE_TOKENS)
    dest_tab = _dest(sel, idx_tab, pstart, tmd)
    dest_flat = dest_tab[:, :TOP_K].reshape(N * TOP_K)
    xs = _dispatch(h2, dest_flat, starts + cnt, n_rows, tmb, tmd)
    eid = jnp.arange(N_EXPERTS, dtype=I32)
    later_used = (eid[None, :] > eid[:, None]) & (padded[None, :] > 0)
    next_used = jnp.min(jnp.where(later_used, eid[None, :], N_EXPERTS), axis=1)
    next_e = jnp.where(next_used < N_EXPERTS, next_used, -1)[block_e].astype(I32)
    ys = _ffn(xs, block_e, n_blocks, next_e, w_gu, b_gu, w_down, b_down, tmb)
    out = _combine(x1, gate_tab, g2, final_g, dest_flat, ys, S, _tile(S, MOE_TOKENS))
    return out.reshape(B, S, D)


def kernel(x, c, ada_w, ada_b, norm1_g, w_in, mix_scale, w_o, norm2_g, router_w, router_b, w_gu, b_gu,
           w_down, b_down, final_g):
    assert ada_w.shape[0] == 1, "single-layer stack"
    return _layer(x, c, ada_w[0], ada_b[0], norm1_g[0], w_in[0], mix_scale[0], w_o[0], norm2_g[0],
                  router_w[0], router_b[0], w_gu[0], b_gu[0], w_down[0], b_down[0], final_g)
```

```python
import functools

import numpy as np
import jax
import jax.numpy as jnp
from jax import lax
from jax.experimental import pallas as pl
from jax.experimental.pallas import tpu as pltpu

F32 = jnp.float32
BF16 = jnp.bfloat16
I32 = jnp.int32

D_MODEL = 1024
RET_HEADS = 4
RET_DK = 64
RET_DV = 128
RET_CHUNK = 128
DSA_HEADS = 8
DSA_KV_HEADS = 2
DSA_HD = 64
IDX_HEADS = 8
IDX_HD = 64
TOPK_MAX = 256
N_EXPERTS = 32
TOP_K = 4
D_EXPERT = D_MODEL
SWIGLU_LIMIT = 7.0
SWIGLU_ALPHA = 1.702
EPS = 1e-6

RET_W = RET_HEADS * RET_DV
DSA_W = DSA_HEADS * DSA_HD
IN_COLS = 2888
IN_COLS_PAD = 2944

KAUG = 128
VAUG = 80
ALIBI_SPLIT = 64
PROJ_ROWS = 512
DSA_QUERIES = 512
DSA_KEYS = 128
MOE_TOKENS = 256
RET_CHUNKS_PER_STEP = 4
FFN_ROWS = 512
BISECT_VALUE_STEPS = 8
BISECT_MAX_STEPS = 64

LANES = 128
NCH = D_MODEL // LANES
VMEM_LIMIT = 56 * 1024 * 1024
NEG_BIG = -1e30
F32_LOWEST = float(np.finfo(np.float32).min)


def _cparams(sem):
    return pltpu.CompilerParams(dimension_semantics=sem, vmem_limit_bytes=VMEM_LIMIT)


def _mod_kernel(c_ref, w_ref, b_ref, o_ref):
    c = c_ref[...]
    s = c * (1.0 / (1.0 + jnp.exp(-c)))
    o_ref[...] = jnp.dot(s, w_ref[...], preferred_element_type=F32,
                         precision=lax.Precision.HIGHEST) + b_ref[...]


def _mod(c, ada_w, ada_b):
    B, D = c.shape
    n_out = ada_w.shape[1]
    rows = 8
    c8 = jnp.zeros((rows, D), F32).at[:B].set(c)
    out = pl.pallas_call(
        _mod_kernel,
        grid=(n_out // D,),
        in_specs=[pl.BlockSpec((rows, D), lambda j: (0, 0)),
                  pl.BlockSpec((D, D), lambda j: (0, j)),
                  pl.BlockSpec((1, D), lambda j: (0, j))],
        out_specs=pl.BlockSpec((rows, D), lambda j: (0, j)),
        out_shape=jax.ShapeDtypeStruct((rows, n_out), F32),
        compiler_params=_cparams(("arbitrary",)),
        name="mod",
    )(c8, ada_w, ada_b.reshape(1, n_out))
    return out[:B]


def _inproj_kernel(x_ref, g_ref, sc_ref, sh_ref, w_ref,
                   rq_ref, rk_ref, rv_ref, rg_ref, aqt_ref, ak_ref, avt_ref, iqt_ref, ik_ref, iwt_ref):
    x = x_ref[...]
    ms = jnp.mean(x * x, axis=-1, keepdims=True)
    y = x * lax.rsqrt(ms + EPS) * g_ref[...]
    hb = (y * (1.0 + sc_ref[...]) + sh_ref[...]).astype(BF16)

    def proj(lo, hi):
        return jnp.dot(hb, w_ref[:, lo:hi], preferred_element_type=F32)

    tm = x.shape[0]
    d = DSA_HD
    rq_ref[...] = proj(0, 256).astype(BF16)
    rk_ref[...] = (proj(256, 512) * (RET_DK ** -0.5)).astype(BF16)
    rv_ref[...] = proj(512, 1024).astype(BF16)
    rg_ref[...] = proj(1024, 1536).astype(BF16)
    aqt_ref[...] = (proj(1536, 2048) * (d ** -0.5)).T.astype(BF16)
    kk = proj(2048, 2176)
    pos = pl.program_id(1) * tm + lax.broadcasted_iota(I32, (tm, d), 0)
    col = lax.broadcasted_iota(I32, (tm, d), 1)
    posblk = jnp.where(col == 0, pos // ALIBI_SPLIT, jnp.where(col == 1, pos % ALIBI_SPLIT, 0)).astype(F32)
    for g in range(DSA_KV_HEADS):
        ak_ref[:, g * KAUG:g * KAUG + d] = kk[:, g * d:(g + 1) * d].astype(BF16)
        ak_ref[:, g * KAUG + d:(g + 1) * KAUG] = posblk.astype(BF16)
    vt = proj(2176, 2304).T
    r16 = lax.broadcasted_iota(I32, (VAUG - d, tm), 0)
    onesblk = jnp.where(r16 == 0, 1.0, 0.0).astype(BF16)
    for g in range(DSA_KV_HEADS):
        avt_ref[g * VAUG:g * VAUG + d, :] = vt[g * d:(g + 1) * d, :].astype(BF16)
        avt_ref[g * VAUG + d:(g + 1) * VAUG, :] = onesblk
    iqt_ref[...] = proj(2304, 2816).T.astype(BF16)
    last = proj(2816, 2944)
    ik_ref[...] = last[:, :IDX_HD].astype(BF16)
    iwt_ref[...] = last.T[IDX_HD:IDX_HD + IDX_HEADS, :] * ((IDX_HD ** -0.5) * (IDX_HEADS ** -0.5))


def _inproj(x, norm_g, sc, sh, w_pad, tm):
    B, S, D = x.shape
    row = lambda w: pl.BlockSpec((None, tm, w), lambda b, i: (b, i, 0))
    colT = lambda h: pl.BlockSpec((None, h, tm), lambda b, i: (b, 0, i))
    vec = pl.BlockSpec((None, 1, D), lambda b, i: (b, 0, 0))
    sd = lambda shape, dt: jax.ShapeDtypeStruct(shape, dt)
    G = DSA_KV_HEADS
    return pl.pallas_call(
        _inproj_kernel,
        grid=(B, S // tm),
        in_specs=[row(D), pl.BlockSpec((1, D), lambda b, i: (0, 0)), vec, vec,
                  pl.BlockSpec((D, IN_COLS_PAD), lambda b, i: (0, 0))],
        out_specs=[row(256), row(256), row(512), row(512), colT(DSA_W), row(G * KAUG), colT(G * VAUG),
                   colT(IDX_HEADS * IDX_HD), row(IDX_HD), colT(IDX_HEADS)],
        out_shape=[sd((B, S, 256), BF16), sd((B, S, 256), BF16), sd((B, S, 512), BF16),
                   sd((B, S, 512), BF16), sd((B, DSA_W, S), BF16), sd((B, S, G * KAUG), BF16),
                   sd((B, G * VAUG, S), BF16), sd((B, IDX_HEADS * IDX_HD, S), BF16),
                   sd((B, S, IDX_HD), BF16), sd((B, IDX_HEADS, S), F32)],
        compiler_params=_cparams(("parallel", "parallel")),
        name="inproj",
    )(x, norm_g.reshape(1, D), sc, sh, w_pad)


def _ret_kernel(rq_ref, rk_ref, rv_ref, rg_ref, din_ref, qd_ref, kd_ref, cd_ref, ms_ref, o_ref, state_ref):
    @pl.when(pl.program_id(1) == 0)
    def _():
        state_ref[...] = jnp.zeros_like(state_ref)

    C = din_ref.shape[1]
    for c in range(rq_ref.shape[0] // C):
        rows = slice(c * C, (c + 1) * C)
        for h in range(RET_HEADS):
            q = rq_ref[rows, h * RET_DK:(h + 1) * RET_DK]
            k = rk_ref[rows, h * RET_DK:(h + 1) * RET_DK]
            v = rv_ref[rows, h * RET_DV:(h + 1) * RET_DV]
            r_prev = state_ref[h]
            s = lax.dot_general(q, k, (((1,), (1,)), ((), ())), preferred_element_type=F32) * din_ref[h]
            o = jnp.dot(s.astype(BF16), v, preferred_element_type=F32)
            o = o + jnp.dot(q, r_prev.astype(BF16), preferred_element_type=F32) * qd_ref[h]
            vd = (v.astype(F32) * kd_ref[h]).astype(BF16)
            kv = lax.dot_general(k, vd, (((0,), (0,)), ((), ())), preferred_element_type=F32)
            state_ref[h] = r_prev * cd_ref[h] + kv
            o = o * lax.rsqrt(jnp.mean(o * o, axis=-1, keepdims=True) + EPS)
            g = rg_ref[rows, h * RET_DV:(h + 1) * RET_DV].astype(F32)
            gate = g * (1.0 / (1.0 + jnp.exp(-g)))
            o_ref[rows, h * RET_DV:(h + 1) * RET_DV] = (
                gate * o * ms_ref[:, h * RET_DV:(h + 1) * RET_DV]).astype(BF16)


def _ret_consts(C):
    H = RET_HEADS
    log_g = np.log1p(-np.exp2(-5.0 - np.arange(H, dtype=np.float64)))
    pos = np.arange(C, dtype=np.float64)
    diff = pos[:, None] - pos[None, :]
    d_inner = np.where(diff[None] >= 0, np.exp(np.maximum(diff, 0.0)[None] * log_g[:, None, None]), 0.0)
    q_decay = np.exp((pos + 1.0)[None] * log_g[:, None])
    k_decay = np.exp((C - 1.0 - pos)[None] * log_g[:, None])
    chunk_decay = np.exp(C * log_g)
    qd = np.broadcast_to(q_decay[:, :, None], (H, C, RET_DV))
    kd = np.broadcast_to(k_decay[:, :, None], (H, C, RET_DV))
    cd = np.broadcast_to(chunk_decay[:, None, None], (H, 1, RET_DV))
    f = lambda a: jnp.asarray(np.ascontiguousarray(a), F32)
    return f(d_inner), f(qd), f(kd), f(cd)


def _retention(rq, rk, rv, rg, ms_ret):
    B, S, _ = rq.shape
    C = min(RET_CHUNK, S)
    din, qd, kd, cd = _ret_consts(C)
    rows = _tile(S, RET_CHUNKS_PER_STEP * C)
    row = lambda w: pl.BlockSpec((None, rows, w), lambda b, n: (b, n, 0))
    full = lambda a: pl.BlockSpec(a.shape, lambda b, n: (0,) * a.ndim)
    return pl.pallas_call(
        _ret_kernel,
        grid=(B, S // rows),
        in_specs=[row(256), row(256), row(512), row(512), full(din), full(qd), full(kd), full(cd),
                  pl.BlockSpec((1, RET_W), lambda b, n: (0, 0))],
        out_specs=row(RET_W),
        out_shape=jax.ShapeDtypeStruct((B, S, RET_W), BF16),
        scratch_shapes=[pltpu.VMEM((RET_HEADS, RET_DK, RET_DV), F32)],
        compiler_params=_cparams(("parallel", "arbitrary")),
        name="ret",
    )(rq, rk, rv, rg, din, qd, kd, cd, ms_ret)


def _f32_key(x):
    i = lax.bitcast_convert_type(x, I32)
    return i ^ ((i >> 31) & 0x7FFFFFFF)


def _key_f32(k):
    return lax.bitcast_convert_type(k ^ ((k >> 31) & 0x7FFFFFFF), F32)


def _dsa_kernel(iqt_ref, iwt_ref, aqt_ref, ik_ref, ak_ref, avt_ref, ms_ref, o_ref, score_ref, qa_ref, sa_ref, sb_ref, mask_ref, *acc_refs,
                tq, tks, n_sel):
    H, G, R, d = DSA_HEADS, DSA_KV_HEADS, DSA_HEADS // DSA_KV_HEADS, DSA_HD
    t0 = pl.program_id(1) * tq
    nsub = (t0 + tq) // tks
    tka = 2 * tks
    npair = ((t0 + tq) // tka + 1) // 2
    kf = float(n_sel)
    qpos = t0 + lax.broadcasted_iota(I32, (1, tq), 1)
    krow = lax.broadcasted_iota(I32, (tks, tq), 0)

    wrow = [iwt_ref[h:h + 1, :] for h in range(IDX_HEADS)]

    def fold8(x, op):
        acc = x[0:8, :]
        for i in range(1, tks // 8):
            acc = op(acc, x[8 * i:8 * (i + 1), :])
        return acc

    def score_pair(i, carry):
        mx, mn, npos, nnon = carry
        for u in range(2 * tka // tks):
            r0 = pl.multiple_of(i * 2 * tka + u * tks, tks)
            kc = ik_ref[pl.ds(r0, tks), :]
            acc = jnp.zeros((tks, tq), F32)
            for h in range(IDX_HEADS):
                rel = jnp.dot(kc, iqt_ref[h * IDX_HD:(h + 1) * IDX_HD, :], preferred_element_type=F32)
                acc = acc + jnp.maximum(rel, 0.0) * wrow[h]
            causal = r0 + krow <= qpos
            sc = jnp.where(causal, acc, -jnp.inf)
            score_ref[pl.ds(r0, tks), :] = sc
            mx = jnp.maximum(mx, fold8(sc, jnp.maximum))
            mn = jnp.minimum(mn, fold8(jnp.where(causal, acc, jnp.inf), jnp.minimum))
            npos = npos + fold8(jnp.where(sc > 0.0, 1.0, 0.0), jnp.add)
            nnon = nnon + fold8(jnp.where(sc >= 0.0, 1.0, 0.0), jnp.add)
        return mx, mn, npos, nnon

    stat0 = (jnp.full((8, tq), -jnp.inf, F32), jnp.full((8, tq), jnp.inf, F32),
             jnp.zeros((8, tq), F32), jnp.zeros((8, tq), F32))
    mx, mn, npos, nnon = lax.fori_loop(0, npair, score_pair, stat0)
    top = jnp.max(mx, axis=0, keepdims=True)
    lo0 = jnp.min(mn, axis=0, keepdims=True)
    n_pos = jnp.sum(npos, axis=0, keepdims=True)
    n_nonneg = jnp.sum(nnon, axis=0, keepdims=True)


    def count(th, strict):
        def body(j, acc):
            for u in range(2):
                s = score_ref[pl.ds(pl.multiple_of((2 * j + u) * tks, tks), tks), :]
                hit = (s > th) if strict else (s >= th)
                acc = acc + fold8(jnp.where(hit, 1.0, 0.0), jnp.add)
            return acc

        acc = lax.fori_loop(0, nsub // 2, body, jnp.zeros((8, tq), F32))
        return jnp.sum(acc, axis=0, keepdims=True)

    def probe(lo, hi, it):
        lk, hk = _f32_key(lo), _f32_key(hi)
        mk = (lk >> 1) + (hk >> 1) + (lk & hk & 1)
        mv = lo + (hi - lo) * 0.5
        early = (jnp.zeros((1, tq), I32) + it) < BISECT_VALUE_STEPS
        mid = jnp.where(early & (mv > lo) & (mv < hi), mv, _key_f32(mk))
        return mid, jnp.max(jnp.where(mk != lk, 1.0, 0.0))

    def bis_cond(c):
        return (c[5] > 0.0) & (c[6] < BISECT_MAX_STEPS)

    def bis_body(c):
        lo, hi, c_lo, c_hi, mid, _, it = c
        cnt = count(mid, False)
        ge = cnt >= kf
        up = ge | (cnt == kf)
        dn = (~ge) | (cnt == kf)
        lo, c_lo = jnp.where(up, mid, lo), jnp.where(up, cnt, c_lo)
        hi, c_hi = jnp.where(dn, mid, hi), jnp.where(dn, cnt, c_hi)
        mid, active = probe(lo, hi, it + 1)
        return lo, hi, c_lo, c_hi, mid, active, it + 1

    zero = jnp.zeros((1, tq), F32)
    keep_all = qpos + 1 <= n_sel
    settled = keep_all | ((n_nonneg >= kf) & (n_pos < kf))
    above = n_pos >= kf
    c_lo0 = jnp.where(settled | above, n_nonneg, (qpos + 1).astype(F32))
    c_hi0 = jnp.where(settled | ~above, n_nonneg, zero)
    lo0 = jnp.where(settled | above, zero, lo0)
    hi0 = jnp.where(settled | ~above, zero, _key_f32(_f32_key(top) + 1))
    mid0, active0 = probe(lo0, hi0, jnp.int32(0))
    lo, hi, c_lo, c_hi, _, _, _ = lax.while_loop(bis_cond, bis_body,
                                                 (lo0, hi0, c_lo0, c_hi0, mid0, active0, jnp.int32(0)))
    at_hi = c_hi >= kf
    thr = jnp.where(keep_all, F32_LOWEST, jnp.where(at_hi, hi, lo))
    excess = jnp.where(keep_all, 0.0, jnp.where(at_hi, c_hi, c_lo) - kf)

    @pl.when(jnp.max(excess) > 0.0)
    def _():
        tied_nonzero = jnp.max(jnp.where((excess > 0.0) & (thr != 0.0), 1.0, 0.0)) > 0.0
        n_above = lax.cond(tied_nonzero, lambda: count(thr, True), lambda: n_pos)
        budget = jnp.where(excess > 0.0, kf - jnp.where(thr == 0.0, n_pos, n_above), jnp.inf)
        earlier = lax.broadcasted_iota(I32, (tks, tks), 1) < lax.broadcasted_iota(I32, (tks, tks), 0)
        earlier = jnp.where(earlier, 1.0, 0.0).astype(BF16)

        def fix(j, seen):
            tiles = []
            for u in range(2):
                r0 = pl.multiple_of((2 * j + u) * tks, tks)
                s = score_ref[pl.ds(r0, tks), :]
                eq = s == thr
                eqf = jnp.where(eq, 1.0, 0.0)
                within = jnp.dot(earlier, eqf.astype(BF16), preferred_element_type=F32)
                tiles.append((r0, s, eq, within, jnp.sum(eqf, axis=0, keepdims=True)))
            for r0, s, eq, within, n_eq in tiles:
                score_ref[pl.ds(r0, tks), :] = jnp.where(eq & (within + seen >= budget), -jnp.inf, s)
                seen = seen + n_eq
            return seen

        lax.fori_loop(0, nsub // 2, fix, jnp.zeros((1, tq), F32))

    arow = lax.broadcasted_iota(I32, (KAUG - d, tq), 0)
    for h in range(H):
        slope = float(2.0 ** (-8.0 * (h + 1) / H))
        qa_ref[h, 0:d, :] = aqt_ref[h * d:(h + 1) * d, :]
        qa_ref[h, d:KAUG, :] = jnp.where(arow == 0, slope * ALIBI_SPLIT, jnp.where(arow == 1, slope, 0.0)).astype(BF16)
    for acc in acc_refs:
        acc[...] = jnp.zeros_like(acc)

    def logits(j, h):
        ka = ak_ref[pl.ds(pl.multiple_of(j * tka, tka), tka), (h // R) * KAUG:(h // R + 1) * KAUG]
        return jnp.dot(ka, qa_ref[h], preferred_element_type=F32)

    def step(j, j_next, cur_ref, next_ref, ms):
        r0 = pl.multiple_of(j * tka, tka)
        mask_ref[...] = jnp.where(score_ref[pl.ds(r0, tka), :] >= thr, 0.0, NEG_BIG)
        nms = []
        for h in range(H):
            g = h // R
            next_ref[h] = logits(j_next, h)
            s = cur_ref[h] + mask_ref[...]
            m_new = jnp.maximum(ms[h], jnp.max(s, axis=0, keepdims=True))
            p = jnp.exp(s - m_new).astype(BF16)
            va = avt_ref[g * VAUG:(g + 1) * VAUG, pl.ds(r0, tka)]
            acc = acc_refs[h]
            acc[...] = acc[...] * jnp.exp(ms[h] - m_new) + jnp.dot(va, p, preferred_element_type=F32)
            nms.append(m_new)
        return tuple(nms)

    for h in range(H):
        sa_ref[h] = logits(0, h)

    def att_pair(i, ms):
        ms = step(2 * i, 2 * i + 1, sa_ref, sb_ref, ms)
        return step(2 * i + 1, jnp.minimum(2 * i + 2, 2 * npair - 1), sb_ref, sa_ref, ms)

    lax.fori_loop(0, npair, att_pair, tuple(jnp.full((1, tq), NEG_BIG, F32) for _ in range(H)))
    for h in range(H):
        a = acc_refs[h][...]
        o = a[0:d, :] / a[d:d + 1, :]
        o_ref[h * d:(h + 1) * d, :] = (o * ms_ref[h * d:(h + 1) * d, :]).astype(BF16)


def _dsa(iqt, iwt, aqt, ik, ak, avt, ms_att, tq, tks):
    B, _, S = iqt.shape
    assert S <= ALIBI_SPLIT * 256 and tq % (2 * tks) == 0
    assert (S // (2 * tks)) % 2 == 0
    n_sel = min(TOPK_MAX, S // 4)
    G = DSA_KV_HEADS
    colT = lambda h: pl.BlockSpec((None, h, tq), lambda b, i: (b, 0, i))
    msb = jnp.broadcast_to(ms_att.reshape(DSA_W, 1), (DSA_W, tq))
    return pl.pallas_call(
        functools.partial(_dsa_kernel, tq=tq, tks=tks, n_sel=n_sel),
        grid=(B, S // tq),
        in_specs=[colT(IDX_HEADS * IDX_HD), colT(IDX_HEADS), colT(DSA_W),
                  pl.BlockSpec((None, S, IDX_HD), lambda b, i: (b, 0, 0)),
                  pl.BlockSpec((None, S, G * KAUG), lambda b, i: (b, 0, 0)),
                  pl.BlockSpec((None, G * VAUG, S), lambda b, i: (b, 0, 0)),
                  pl.BlockSpec((DSA_W, tq), lambda b, i: (0, 0))],
        out_specs=colT(DSA_W),
        out_shape=jax.ShapeDtypeStruct((B, DSA_W, S), BF16),
        scratch_shapes=[pltpu.VMEM((S, tq), F32), pltpu.VMEM((DSA_HEADS, KAUG, tq), BF16),
                        pltpu.VMEM((DSA_HEADS, 2 * tks, tq), F32), pltpu.VMEM((DSA_HEADS, 2 * tks, tq), F32),
                        pltpu.VMEM((2 * tks, tq), F32)]
        + [pltpu.VMEM((VAUG, tq), F32) for _ in range(DSA_HEADS)],
        compiler_params=_cparams(("parallel", "arbitrary")),
        name="dsa",
    )(iqt, iwt, aqt, ik, ak, avt, msb)


def _oproj_kernel(ret_ref, att_ref, x_ref, wo_ref, g1_ref, n2_ref, sc_ref, sh_ref, rw_ref, rb_ref,
                  x1_ref, h2_ref, sel_ref, idx_ref, gate_ref, cnt_ref):
    mixo = jnp.dot(ret_ref[...], wo_ref[:RET_W, :], preferred_element_type=F32)
    mixo = mixo + lax.dot_general(att_ref[...], wo_ref[RET_W:, :], (((0,), (0,)), ((), ())),
                                  preferred_element_type=F32)
    x1 = x_ref[...] + g1_ref[...] * mixo
    x1_ref[...] = x1
    y = x1 * lax.rsqrt(jnp.mean(x1 * x1, axis=-1, keepdims=True) + EPS) * n2_ref[...]
    h2 = y * (1.0 + sc_ref[...]) + sh_ref[...]
    _store_token_major(h2_ref, h2)
    h_hi = h2.astype(BF16)
    h_lo = (h2 - h_hi.astype(F32)).astype(BF16)
    logits = jnp.dot(h_hi, rw_ref[0], preferred_element_type=F32)
    logits = logits + (jnp.dot(h_hi, rw_ref[1], preferred_element_type=F32)
                       + jnp.dot(h_lo, rw_ref[0], preferred_element_type=F32)) + rb_ref[...]
    tm = logits.shape[0]
    lane = lax.broadcasted_iota(I32, (tm, LANES), 1).astype(F32)
    work = jnp.where(lane < N_EXPERTS, logits, -jnp.inf)
    sel = jnp.zeros((tm, LANES), F32)
    idx_tab = jnp.zeros((tm, LANES), F32)
    vals = []
    for k in range(TOP_K):
        m = jnp.max(work, axis=1, keepdims=True)
        idx = jnp.min(jnp.where(work == m, lane, float(LANES)), axis=1, keepdims=True)
        hit = lane == idx
        sel = jnp.where(hit, 1.0, sel)
        idx_tab = jnp.where(lane == k, idx, idx_tab)
        work = jnp.where(hit, -jnp.inf, work)
        vals.append(m)
    es = [jnp.exp(v - vals[0]) for v in vals]
    den = es[0] + es[1] + es[2] + es[3]
    gate_tab = jnp.zeros((tm, LANES), F32)
    for k in range(TOP_K):
        gate_tab = jnp.where(lane == k, es[k] / den, gate_tab)
    sel_ref[...] = sel
    idx_ref[...] = idx_tab
    gate_ref[...] = gate_tab

    @pl.when((pl.program_id(0) == 0) & (pl.program_id(1) == 0))
    def _():
        cnt_ref[...] = jnp.zeros_like(cnt_ref)

    cnt_ref[...] += jnp.sum(sel, axis=0, keepdims=True)


def _oproj(ret, att, x, wo, g1, n2g, sc2, sh2, rw_pad, rb_pad, tm):
    B, S, D = x.shape
    nt = S // tm
    row = lambda w: pl.BlockSpec((None, tm, w), lambda b, i: (b, i, 0))
    flat = lambda w: pl.BlockSpec((tm, w), lambda b, i: (b * nt + i, 0))
    vec = pl.BlockSpec((None, 1, D), lambda b, i: (b, 0, 0))
    cst = lambda shape: pl.BlockSpec(shape, lambda b, i: (0, 0))
    sd = lambda shape, dt: jax.ShapeDtypeStruct(shape, dt)
    N = B * S
    return pl.pallas_call(
        _oproj_kernel,
        grid=(B, nt),
        in_specs=[row(RET_W), pl.BlockSpec((None, DSA_W, tm), lambda b, i: (b, 0, i)), row(D), cst((D, D)), vec,
                  cst((1, D)), vec, vec,
                  pl.BlockSpec((2, D, LANES), lambda b, i: (0, 0, 0)), cst((1, LANES))],
        out_specs=[flat(D), pl.BlockSpec((tm * (D // LANES), LANES), lambda b, i: (b * nt + i, 0)),
                   flat(LANES), flat(LANES), flat(LANES), cst((1, LANES))],
        out_shape=[sd((N, D), F32), sd((N * (D // LANES), LANES), F32), sd((N, LANES), F32), sd((N, LANES), F32),
                   sd((N, LANES), F32), sd((1, LANES), F32)],
        compiler_params=_cparams(("arbitrary", "arbitrary")),
        name="oproj",
    )(ret, att, x, wo, g1, n2g.reshape(1, D), sc2, sh2, rw_pad, rb_pad)


def _dest_kernel(sel_ref, idx_ref, pstart_ref, dest_ref, seen_ref):
    @pl.when(pl.program_id(0) == 0)
    def _():
        seen_ref[...] = jnp.zeros_like(seen_ref)

    sel = sel_ref[...]
    tm = sel.shape[0]
    earlier = lax.broadcasted_iota(I32, (tm, tm), 1) < lax.broadcasted_iota(I32, (tm, tm), 0)
    earlier = jnp.where(earlier, 1.0, 0.0).astype(BF16)
    rank = jnp.dot(earlier, sel.astype(BF16), preferred_element_type=F32) + seen_ref[...]
    dest = pstart_ref[...] + rank
    lane = lax.broadcasted_iota(I32, (tm, LANES), 1).astype(F32)
    idx_tab = idx_ref[...]
    out = jnp.zeros((tm, LANES), F32)
    for k in range(TOP_K):
        e_k = jnp.sum(jnp.where(lane == k, idx_tab, 0.0), axis=1, keepdims=True)
        d_k = jnp.sum(jnp.where(lane == e_k, dest, 0.0), axis=1, keepdims=True)
        out = jnp.where(lane == k, d_k, out)
    dest_ref[...] = out.astype(I32)
    seen_ref[...] += jnp.sum(sel, axis=0, keepdims=True)


def _dest(sel, idx_tab, pstart, tm):
    N = sel.shape[0]
    blk = pl.BlockSpec((tm, LANES), lambda i: (i, 0))
    return pl.pallas_call(
        _dest_kernel,
        grid=(N // tm,),
        in_specs=[blk, blk, pl.BlockSpec((1, LANES), lambda i: (0, 0))],
        out_specs=blk,
        out_shape=jax.ShapeDtypeStruct((N, LANES), I32),
        scratch_shapes=[pltpu.VMEM((1, LANES), F32)],
        compiler_params=_cparams(("arbitrary",)),
        name="dest",
    )(sel, idx_tab, pstart)


def _disp_kernel(zs_ref, h2_ref, dest_hbm, xs_hbm, zbuf, idx_a, idx_b, sem_ia, sem_ib, sem_row, sem_z,
                 *, tm, nch, zrows, n_steps):
    i = pl.program_id(0)
    n = tm * TOP_K

    def idx_copy(tile, buf, sem):
        return pltpu.make_async_copy(dest_hbm.at[pl.ds(tile * n, n)], buf, sem)

    def rows(ref, first, count):
        return ref.at[pl.ds(pl.multiple_of(first * nch, nch), count * nch)]

    def row_copy(row, dst):
        return pltpu.make_async_copy(rows(h2_ref, row, 1), rows(xs_hbm, dst, 1), sem_row)

    def issue_rows(first_row, idx):
        def body(t, c):
            for k in range(TOP_K):
                row_copy(first_row + t, idx[t * TOP_K + k]).start()
            return c

        lax.fori_loop(0, tm, body, 0)

    def wait_rows():
        pltpu.make_async_copy(rows(xs_hbm, 0, n), rows(xs_hbm, 0, n), sem_row).wait()

    @pl.when(i == 0)
    def _():
        zbuf[...] = jnp.zeros_like(zbuf)
        for e in range(N_EXPERTS):
            fill = pltpu.make_async_copy(zbuf, rows(xs_hbm, zs_ref[e], zrows), sem_z)
            fill.start()
            fill.wait()
        idx_copy(0, idx_a, sem_ia).start()
        idx_copy(1, idx_b, sem_ib).start()

    idx_copy(2 * i, idx_a, sem_ia).wait()
    issue_rows(0, idx_a)
    idx_copy(2 * i + 1, idx_b, sem_ib).wait()
    issue_rows(tm, idx_b)

    @pl.when(i + 1 < n_steps)
    def _():
        idx_copy(2 * i + 2, idx_a, sem_ia).start()
        idx_copy(2 * i + 3, idx_b, sem_ib).start()

    wait_rows()
    wait_rows()


def _dispatch(h2, dest_flat, zero_start, n_rows, zrows, tm):
    nch = D_MODEL // LANES
    n_steps = h2.shape[0] // nch // (2 * tm)
    grid_spec = pltpu.PrefetchScalarGridSpec(
        num_scalar_prefetch=1,
        grid=(n_steps,),
        in_specs=[pl.BlockSpec((2 * tm * nch, LANES), lambda i, zs: (i, 0)), pl.BlockSpec(memory_space=pl.ANY)],
        out_specs=pl.BlockSpec(memory_space=pl.ANY),
        scratch_shapes=[pltpu.VMEM((zrows * nch, LANES), F32), pltpu.SMEM((tm * TOP_K,), I32),
                        pltpu.SMEM((tm * TOP_K,), I32), pltpu.SemaphoreType.DMA, pltpu.SemaphoreType.DMA,
                        pltpu.SemaphoreType.DMA, pltpu.SemaphoreType.DMA],
    )
    return pl.pallas_call(
        functools.partial(_disp_kernel, tm=tm, nch=nch, zrows=zrows, n_steps=n_steps),
        grid_spec=grid_spec,
        out_shape=jax.ShapeDtypeStruct((n_rows * nch, LANES), F32),
        compiler_params=_cparams(("arbitrary",)),
        name="disp",
    )(zero_start, h2, dest_flat)


def _store_token_major(ref, x):
    rows, d = x.shape
    nch = d // LANES
    for j in range(nch):
        ref[pl.ds(j, rows, stride=nch), :] = x[:, j * LANES:(j + 1) * LANES]


def _load_token_major(ref, rows, nch):
    return jnp.concatenate([ref[pl.ds(j, rows, stride=nch), :] for j in range(nch)], axis=1)


def _ffn_kernel(be_ref, nb_ref, nxt_ref, xs_ref, wgu_hbm, bgu_ref, wd_hbm, bd_ref, ys_ref,
                wgu_st, wd_st, wgu_bf, wd_bf, sem_gu, sem_d):
    i = pl.program_id(0)
    live = i < nb_ref[0]
    e = be_ref[i]

    def fetch(ex):
        return (pltpu.make_async_copy(wgu_hbm.at[ex], wgu_st, sem_gu),
                pltpu.make_async_copy(wd_hbm.at[ex], wd_st, sem_d))

    @pl.when(i == 0)
    def _():
        for cp in fetch(e):
            cp.start()

    @pl.when(live & ((i == 0) | (e != be_ref[jnp.maximum(i - 1, 0)])))
    def _():
        for cp in fetch(e):
            cp.wait()
        wgu_bf[...] = wgu_st[...].astype(BF16)
        wd_bf[...] = wd_st[...].astype(BF16)

        @pl.when(nxt_ref[i] >= 0)
        def _():
            for cp in fetch(nxt_ref[i]):
                cp.start()

    @pl.when(live)
    def _():
        xb = _load_token_major(xs_ref, ys_ref.shape[0] // NCH, NCH).astype(BF16)
        gu = jnp.dot(xb, wgu_bf[...], preferred_element_type=F32) + bgu_ref[...]
        gate = jnp.minimum(gu[:, :D_EXPERT], SWIGLU_LIMIT)
        up = jnp.clip(gu[:, D_EXPERT:], -SWIGLU_LIMIT, SWIGLU_LIMIT)
        glu = gate * (1.0 / (1.0 + jnp.exp(-SWIGLU_ALPHA * gate)))
        act = ((up + 1.0) * glu).astype(BF16)
        ys = jnp.dot(act, wd_bf[...], preferred_element_type=F32) + bd_ref[...]
        _store_token_major(ys_ref, ys)

    @pl.when(jnp.logical_not(live))
    def _():
        ys_ref[...] = jnp.zeros_like(ys_ref)


def _ffn(xs, block_e, n_blocks, next_e, w_gu, b_gu, w_down, b_down, tmb):
    D = D_MODEL
    P = xs.shape[0] // NCH
    E = w_gu.shape[0]
    blk = lambda i, be, nb, nx: (jnp.minimum(i, nb[0] - 1), 0)
    wsel = lambda i, be, nb, nx: (be[jnp.minimum(i, nb[0] - 1)], 0, 0)
    grid_spec = pltpu.PrefetchScalarGridSpec(
        num_scalar_prefetch=3,
        grid=(P // tmb,),
        in_specs=[pl.BlockSpec((tmb * NCH, LANES), blk),
                  pl.BlockSpec(memory_space=pl.ANY),
                  pl.BlockSpec((None, 1, 2 * D_EXPERT), wsel),
                  pl.BlockSpec(memory_space=pl.ANY),
                  pl.BlockSpec((None, 1, D), wsel)],
        out_specs=pl.BlockSpec((tmb * (D // LANES), LANES), lambda i, be, nb, nx: (i, 0)),
        scratch_shapes=[pltpu.VMEM((D, 2 * D_EXPERT), F32), pltpu.VMEM((D_EXPERT, D), F32),
                        pltpu.VMEM((D, 2 * D_EXPERT), BF16), pltpu.VMEM((D_EXPERT, D), BF16),
                        pltpu.SemaphoreType.DMA, pltpu.SemaphoreType.DMA],
    )
    return pl.pallas_call(
        _ffn_kernel,
        grid_spec=grid_spec,
        out_shape=jax.ShapeDtypeStruct((P * (D // LANES), LANES), F32),
        compiler_params=_cparams(("arbitrary",)),
        name="ffn",
    )(block_e, n_blocks, next_e, xs, w_gu, b_gu.reshape(E, 1, 2 * D_EXPERT), w_down, b_down.reshape(E, 1, D))


def _comb_kernel(x1_ref, gate_ref, g2_ref, fg_ref, dest_hbm, ys_hbm, o_ref, buf_a, buf_b, idx_a, idx_b,
                 sem_ia, sem_ib, sem_ra, sem_rb, *, tm, n_steps):
    i = pl.program_id(0)
    n = tm * TOP_K

    def idx_copy(tile, buf, sem):
        return pltpu.make_async_copy(dest_hbm.at[pl.ds(tile * n, n)], buf, sem)

    nch = x1_ref.shape[1] // LANES

    def row_copy(src, buf, k, t, sem):
        return pltpu.make_async_copy(ys_hbm.at[pl.ds(pl.multiple_of(src * nch, nch), nch)],
                                     buf.at[k, pl.ds(pl.multiple_of(t * nch, nch), nch)], sem)

    def issue_rows(idx, buf, sem):
        def body(t, c):
            for k in range(TOP_K):
                row_copy(idx[t * TOP_K + k], buf, k, t, sem).start()
            return c

        lax.fori_loop(0, tm, body, 0)

    def wait_rows(buf, sem):
        pltpu.make_async_copy(buf, buf, sem).wait()

    def finish(buf, lo):
        gates = gate_ref[lo:lo + tm, :]
        y = jnp.zeros((tm, x1_ref.shape[1]), F32)
        for k in range(TOP_K):
            y = y + gates[:, k:k + 1] * _load_token_major(buf.at[k], tm, nch)
        v = x1_ref[lo:lo + tm, :] + g2_ref[...] * y
        o_ref[lo:lo + tm, :] = v * lax.rsqrt(jnp.mean(v * v, axis=-1, keepdims=True) + EPS) * fg_ref[...]

    @pl.when(i == 0)
    def _():
        first = idx_copy(0, idx_a, sem_ia)
        first.start()
        first.wait()
        issue_rows(idx_a, buf_a, sem_ra)
        idx_copy(1, idx_b, sem_ib).start()

    idx_copy(2 * i + 1, idx_b, sem_ib).wait()
    issue_rows(idx_b, buf_b, sem_rb)

    @pl.when(i + 1 < n_steps)
    def _():
        idx_copy(2 * i + 2, idx_a, sem_ia).start()

    wait_rows(buf_a, sem_ra)
    finish(buf_a, 0)

    @pl.when(i + 1 < n_steps)
    def _():
        idx_copy(2 * i + 2, idx_a, sem_ia).wait()
        issue_rows(idx_a, buf_a, sem_ra)
        idx_copy(2 * i + 3, idx_b, sem_ib).start()

    wait_rows(buf_b, sem_rb)
    finish(buf_b, tm)


def _combine(x1, gate_tab, g2, final_g, dest_flat, ys, S, tm):
    N, D = x1.shape
    n_steps = N // (2 * tm)
    per_b = S // (2 * tm)
    return pl.pallas_call(
        functools.partial(_comb_kernel, tm=tm, n_steps=n_steps),
        grid=(n_steps,),
        in_specs=[pl.BlockSpec((2 * tm, D), lambda i: (i, 0)),
                  pl.BlockSpec((2 * tm, LANES), lambda i: (i, 0)),
                  pl.BlockSpec((None, 1, D), lambda i: (i // per_b, 0, 0)),
                  pl.BlockSpec((1, D), lambda i: (0, 0)),
                  pl.BlockSpec(memory_space=pl.ANY),
                  pl.BlockSpec(memory_space=pl.ANY)],
        out_specs=pl.BlockSpec((2 * tm, D), lambda i: (i, 0)),
        out_shape=jax.ShapeDtypeStruct((N, D), F32),
        scratch_shapes=[pltpu.VMEM((TOP_K, tm * (D // LANES), LANES), F32),
                        pltpu.VMEM((TOP_K, tm * (D // LANES), LANES), F32),
                        pltpu.SMEM((tm * TOP_K,), I32), pltpu.SMEM((tm * TOP_K,), I32),
                        pltpu.SemaphoreType.DMA, pltpu.SemaphoreType.DMA, pltpu.SemaphoreType.DMA,
                        pltpu.SemaphoreType.DMA],
        compiler_params=_cparams(("arbitrary",)),
        name="comb",
    )(x1, gate_tab, g2, final_g.reshape(1, D), dest_flat, ys)


def _tile(n, pref):
    t = min(pref, n)
    assert n % t == 0, (n, t)
    return t


def _layer(x, c, ada_w, ada_b, norm1_g, w_in, mix_scale, w_o, norm2_g,
           router_w, router_b, w_gu, b_gu, w_down, b_down, final_g):
    B, S, D = x.shape
    N = B * S
    mod = _mod(c, ada_w, ada_b).reshape(B, 6, 1, D)
    sh1, sc1, g1, sh2, sc2, g2 = (mod[:, j] for j in range(6))

    w_pad = jnp.pad(w_in, ((0, 0), (0, IN_COLS_PAD - IN_COLS))).astype(BF16)
    rq, rk, rv, rg, aqt, ak, avt, iqt, ik, iwt = _inproj(x, norm1_g, sc1, sh1, w_pad, _tile(S, PROJ_ROWS))
    ms = mix_scale.reshape(1, RET_W + DSA_W)
    ret = _retention(rq, rk, rv, rg, ms[:, :RET_W])
    att = _dsa(iqt, iwt, aqt, ik, ak, avt, ms[:, RET_W:], _tile(S, DSA_QUERIES), DSA_KEYS)

    rw_pad = jnp.pad(router_w, ((0, 0), (0, LANES - N_EXPERTS)))
    rw_hi = rw_pad.astype(BF16)
    rw_pad = jnp.stack([rw_hi, (rw_pad - rw_hi.astype(F32)).astype(BF16)])
    rb_pad = jnp.pad(router_b, (0, LANES - N_EXPERTS)).reshape(1, LANES)
    x1, h2, sel, idx_tab, gate_tab, counts = _oproj(ret, att, x, w_o.astype(BF16), g1, norm2_g, sc2, sh2,
                                                    rw_pad, rb_pad, _tile(S, PROJ_ROWS))

    tmb = FFN_ROWS
    n_rows = (N * TOP_K + N_EXPERTS * (tmb - 1)) // tmb * tmb + tmb
    cnt = counts[0, :N_EXPERTS].astype(I32)
    padded = (cnt + tmb - 1) // tmb * tmb
    ends = jnp.cumsum(padded)
    starts = ends - padded
    pstart = jnp.pad(starts.astype(F32), (0, LANES - N_EXPERTS)).reshape(1, LANES)
    n_blocks = (ends[-1] // tmb).reshape(1)
    first_row = jnp.arange(n_rows // tmb, dtype=I32) * tmb
    block_e = jnp.minimum(jnp.sum((ends[None, :] <= first_row[:, None]).astype(I32), axis=1), N_EXPERTS - 1)

    tmd = _tile(N, MOE_TOKENS)
    dest_tab = _dest(sel, idx_tab, pstart, tmd)
    dest_flat = dest_tab[:, :TOP_K].reshape(N * TOP_K)
    xs = _dispatch(h2, dest_flat, starts + cnt, n_rows, tmb, tmd)
    eid = jnp.arange(N_EXPERTS, dtype=I32)
    later_used = (eid[None, :] > eid[:, None]) & (padded[None, :] > 0)
    next_used = jnp.min(jnp.where(later_used, eid[None, :], N_EXPERTS), axis=1)
    next_e = jnp.where(next_used < N_EXPERTS, next_used, -1)[block_e].astype(I32)
    ys = _ffn(xs, block_e, n_blocks, next_e, w_gu, b_gu, w_down, b_down, tmb)
    out = _combine(x1, gate_tab, g2, final_g, dest_flat, ys, S, _tile(S, MOE_TOKENS))
    return out.reshape(B, S, D)


def kernel(x, c, ada_w, ada_b, norm1_g, w_in, mix_scale, w_o, norm2_g, router_w, router_b, w_gu, b_gu,
           w_down, b_down, final_g):
    assert ada_w.shape[0] == 1, "single-layer stack"
    return _layer(x, c, ada_w[0], ada_b[0], norm1_g[0], w_in[0], mix_scale[0], w_o[0], norm2_g[0],
                  router_w[0], router_b[0], w_gu[0], b_gu[0], w_down[0], b_down[0], final_g)
```

```python
import functools

import numpy as np
import jax
import jax.numpy as jnp
from jax import lax
from jax.experimental import pallas as pl
from jax.experimental.pallas import tpu as pltpu

F32 = jnp.float32
BF16 = jnp.bfloat16
I32 = jnp.int32

D_MODEL = 1024
RET_HEADS = 4
RET_DK = 64
RET_DV = 128
RET_CHUNK = 128
DSA_HEADS = 8
DSA_KV_HEADS = 2
DSA_HD = 64
IDX_HEADS = 8
IDX_HD = 64
TOPK_MAX = 256
N_EXPERTS = 32
TOP_K = 4
D_EXPERT = D_MODEL
SWIGLU_LIMIT = 7.0
SWIGLU_ALPHA = 1.702
EPS = 1e-6

RET_W = RET_HEADS * RET_DV
DSA_W = DSA_HEADS * DSA_HD
IN_COLS = 2888
IN_COLS_PAD = 2944

KAUG = 128
VAUG = 80
ALIBI_SPLIT = 64
PROJ_ROWS = 512
DSA_QUERIES = 512
DSA_KEYS = 128
MOE_TOKENS = 256
RET_CHUNKS_PER_STEP = 4
FFN_ROWS = 512
BISECT_VALUE_STEPS = 8
BISECT_MAX_STEPS = 64

LANES = 128
NCH = D_MODEL // LANES
VMEM_LIMIT = 56 * 1024 * 1024
NEG_BIG = -1e30
F32_LOWEST = float(np.finfo(np.float32).min)


def _cparams(sem):
    return pltpu.CompilerParams(dimension_semantics=sem, vmem_limit_bytes=VMEM_LIMIT)


def _mod_kernel(c_ref, w_ref, b_ref, o_ref):
    c = c_ref[...]
    s = c * (1.0 / (1.0 + jnp.exp(-c)))
    o_ref[...] = jnp.dot(s, w_ref[...], preferred_element_type=F32,
                         precision=lax.Precision.HIGHEST) + b_ref[...]


def _mod(c, ada_w, ada_b):
    B, D = c.shape
    n_out = ada_w.shape[1]
    rows = 8
    c8 = jnp.zeros((rows, D), F32).at[:B].set(c)
    out = pl.pallas_call(
        _mod_kernel,
        grid=(n_out // D,),
        in_specs=[pl.BlockSpec((rows, D), lambda j: (0, 0)),
                  pl.BlockSpec((D, D), lambda j: (0, j)),
                  pl.BlockSpec((1, D), lambda j: (0, j))],
        out_specs=pl.BlockSpec((rows, D), lambda j: (0, j)),
        out_shape=jax.ShapeDtypeStruct((rows, n_out), F32),
        compiler_params=_cparams(("arbitrary",)),
        name="mod",
    )(c8, ada_w, ada_b.reshape(1, n_out))
    return out[:B]


def _inproj_kernel(x_ref, g_ref, sc_ref, sh_ref, w_ref,
                   rq_ref, rk_ref, rv_ref, rg_ref, aqt_ref, ak_ref, avt_ref, iqt_ref, ik_ref, iwt_ref):
    x = x_ref[...]
    ms = jnp.mean(x * x, axis=-1, keepdims=True)
    y = x * lax.rsqrt(ms + EPS) * g_ref[...]
    hb = (y * (1.0 + sc_ref[...]) + sh_ref[...]).astype(BF16)

    def proj(lo, hi):
        return jnp.dot(hb, w_ref[:, lo:hi], preferred_element_type=F32)

    tm = x.shape[0]
    d = DSA_HD
    rq_ref[...] = proj(0, 256).astype(BF16)
    rk_ref[...] = (proj(256, 512) * (RET_DK ** -0.5)).astype(BF16)
    rv_ref[...] = proj(512, 1024).astype(BF16)
    rg_ref[...] = proj(1024, 1536).astype(BF16)
    aqt_ref[...] = (proj(1536, 2048) * (d ** -0.5)).T.astype(BF16)
    kk = proj(2048, 2176)
    pos = pl.program_id(1) * tm + lax.broadcasted_iota(I32, (tm, d), 0)
    col = lax.broadcasted_iota(I32, (tm, d), 1)
    posblk = jnp.where(col == 0, pos // ALIBI_SPLIT, jnp.where(col == 1, pos % ALIBI_SPLIT, 0)).astype(F32)
    for g in range(DSA_KV_HEADS):
        ak_ref[:, g * KAUG:g * KAUG + d] = kk[:, g * d:(g + 1) * d].astype(BF16)
        ak_ref[:, g * KAUG + d:(g + 1) * KAUG] = posblk.astype(BF16)
    vt = proj(2176, 2304).T
    r16 = lax.broadcasted_iota(I32, (VAUG - d, tm), 0)
    onesblk = jnp.where(r16 == 0, 1.0, 0.0).astype(BF16)
    for g in range(DSA_KV_HEADS):
        avt_ref[g * VAUG:g * VAUG + d, :] = vt[g * d:(g + 1) * d, :].astype(BF16)
        avt_ref[g * VAUG + d:(g + 1) * VAUG, :] = onesblk
    iqt_ref[...] = proj(2304, 2816).T.astype(BF16)
    last = proj(2816, 2944)
    ik_ref[...] = last[:, :IDX_HD].astype(BF16)
    iwt_ref[...] = last.T[IDX_HD:IDX_HD + IDX_HEADS, :] * ((IDX_HD ** -0.5) * (IDX_HEADS ** -0.5))


def _inproj(x, norm_g, sc, sh, w_pad, tm):
    B, S, D = x.shape
    row = lambda w: pl.BlockSpec((None, tm, w), lambda b, i: (b, i, 0))
    colT = lambda h: pl.BlockSpec((None, h, tm), lambda b, i: (b, 0, i))
    vec = pl.BlockSpec((None, 1, D), lambda b, i: (b, 0, 0))
    sd = lambda shape, dt: jax.ShapeDtypeStruct(shape, dt)
    G = DSA_KV_HEADS
    return pl.pallas_call(
        _inproj_kernel,
        grid=(B, S // tm),
        in_specs=[row(D), pl.BlockSpec((1, D), lambda b, i: (0, 0)), vec, vec,
                  pl.BlockSpec((D, IN_COLS_PAD), lambda b, i: (0, 0))],
        out_specs=[row(256), row(256), row(512), row(512), colT(DSA_W), row(G * KAUG), colT(G * VAUG),
                   colT(IDX_HEADS * IDX_HD), row(IDX_HD), colT(IDX_HEADS)],
        out_shape=[sd((B, S, 256), BF16), sd((B, S, 256), BF16), sd((B, S, 512), BF16),
                   sd((B, S, 512), BF16), sd((B, DSA_W, S), BF16), sd((B, S, G * KAUG), BF16),
                   sd((B, G * VAUG, S), BF16), sd((B, IDX_HEADS * IDX_HD, S), BF16),
                   sd((B, S, IDX_HD), BF16), sd((B, IDX_HEADS, S), F32)],
        compiler_params=_cparams(("parallel", "parallel")),
        name="inproj",
    )(x, norm_g.reshape(1, D), sc, sh, w_pad)


def _ret_kernel(rq_ref, rk_ref, rv_ref, rg_ref, din_ref, qd_ref, kd_ref, cd_ref, ms_ref, o_ref, state_ref):
    @pl.when(pl.program_id(1) == 0)
    def _():
        state_ref[...] = jnp.zeros_like(state_ref)

    C = din_ref.shape[1]
    for c in range(rq_ref.shape[0] // C):
        rows = slice(c * C, (c + 1) * C)
        for h in range(RET_HEADS):
            q = rq_ref[rows, h * RET_DK:(h + 1) * RET_DK]
            k = rk_ref[rows, h * RET_DK:(h + 1) * RET_DK]
            v = rv_ref[rows, h * RET_DV:(h + 1) * RET_DV]
            r_prev = state_ref[h]
            s = lax.dot_general(q, k, (((1,), (1,)), ((), ())), preferred_element_type=F32) * din_ref[h]
            o = jnp.dot(s.astype(BF16), v, preferred_element_type=F32)
            o = o + jnp.dot(q, r_prev.astype(BF16), preferred_element_type=F32) * qd_ref[h]
            vd = (v.astype(F32) * kd_ref[h]).astype(BF16)
            kv = lax.dot_general(k, vd, (((0,), (0,)), ((), ())), preferred_element_type=F32)
            state_ref[h] = r_prev * cd_ref[h] + kv
            o = o * lax.rsqrt(jnp.mean(o * o, axis=-1, keepdims=True) + EPS)
            g = rg_ref[rows, h * RET_DV:(h + 1) * RET_DV].astype(F32)
            gate = g * (1.0 / (1.0 + jnp.exp(-g)))
            o_ref[rows, h * RET_DV:(h + 1) * RET_DV] = (
                gate * o * ms_ref[:, h * RET_DV:(h + 1) * RET_DV]).astype(BF16)


def _ret_consts(C):
    H = RET_HEADS
    log_g = np.log1p(-np.exp2(-5.0 - np.arange(H, dtype=np.float64)))
    pos = np.arange(C, dtype=np.float64)
    diff = pos[:, None] - pos[None, :]
    d_inner = np.where(diff[None] >= 0, np.exp(np.maximum(diff, 0.0)[None] * log_g[:, None, None]), 0.0)
    q_decay = np.exp((pos + 1.0)[None] * log_g[:, None])
    k_decay = np.exp((C - 1.0 - pos)[None] * log_g[:, None])
    chunk_decay = np.exp(C * log_g)
    qd = np.broadcast_to(q_decay[:, :, None], (H, C, RET_DV))
    kd = np.broadcast_to(k_decay[:, :, None], (H, C, RET_DV))
    cd = np.broadcast_to(chunk_decay[:, None, None], (H, 1, RET_DV))
    f = lambda a: jnp.asarray(np.ascontiguousarray(a), F32)
    return f(d_inner), f(qd), f(kd), f(cd)


def _retention(rq, rk, rv, rg, ms_ret):
    B, S, _ = rq.shape
    C = min(RET_CHUNK, S)
    din, qd, kd, cd = _ret_consts(C)
    rows = _tile(S, RET_CHUNKS_PER_STEP * C)
    row = lambda w: pl.BlockSpec((None, rows, w), lambda b, n: (b, n, 0))
    full = lambda a: pl.BlockSpec(a.shape, lambda b, n: (0,) * a.ndim)
    return pl.pallas_call(
        _ret_kernel,
        grid=(B, S // rows),
        in_specs=[row(256), row(256), row(512), row(512), full(din), full(qd), full(kd), full(cd),
                  pl.BlockSpec((1, RET_W), lambda b, n: (0, 0))],
        out_specs=row(RET_W),
        out_shape=jax.ShapeDtypeStruct((B, S, RET_W), BF16),
        scratch_shapes=[pltpu.VMEM((RET_HEADS, RET_DK, RET_DV), F32)],
        compiler_params=_cparams(("parallel", "arbitrary")),
        name="ret",
    )(rq, rk, rv, rg, din, qd, kd, cd, ms_ret)


def _f32_key(x):
    i = lax.bitcast_convert_type(x, I32)
    return i ^ ((i >> 31) & 0x7FFFFFFF)


def _key_f32(k):
    return lax.bitcast_convert_type(k ^ ((k >> 31) & 0x7FFFFFFF), F32)


def _dsa_kernel(iqt_ref, iwt_ref, aqt_ref, ik_ref, ak_ref, avt_ref, ms_ref, o_ref, score_ref, qa_ref, sa_ref, sb_ref, mask_ref, *acc_refs,
                tq, tks, n_sel):
    H, G, R, d = DSA_HEADS, DSA_KV_HEADS, DSA_HEADS // DSA_KV_HEADS, DSA_HD
    t0 = pl.program_id(1) * tq
    nsub = (t0 + tq) // tks
    tka = 2 * tks
    npair = ((t0 + tq) // tka + 1) // 2
    kf = float(n_sel)
    qpos = t0 + lax.broadcasted_iota(I32, (1, tq), 1)
    krow = lax.broadcasted_iota(I32, (tks, tq), 0)

    wrow = [iwt_ref[h:h + 1, :] for h in range(IDX_HEADS)]

    def fold8(x, op):
        acc = x[0:8, :]
        for i in range(1, tks // 8):
            acc = op(acc, x[8 * i:8 * (i + 1), :])
        return acc

    def score_pair(i, carry, masked):
        mx, mn, npos, nnon = carry
        for u in range(2 * tka // tks):
            r0 = pl.multiple_of(i * 2 * tka + u * tks, tks)
            kc = ik_ref[pl.ds(r0, tks), :]
            acc = jnp.zeros((tks, tq), F32)
            for h in range(IDX_HEADS):
                rel = jnp.dot(kc, iqt_ref[h * IDX_HD:(h + 1) * IDX_HD, :], preferred_element_type=F32)
                acc = acc + jnp.maximum(rel, 0.0) * wrow[h]
            if masked:
                causal = r0 + krow <= qpos
                sc = jnp.where(causal, acc, -jnp.inf)
                lowest = jnp.where(causal, acc, jnp.inf)
            else:
                sc = lowest = acc
            score_ref[pl.ds(r0, tks), :] = sc
            mx = jnp.maximum(mx, fold8(sc, jnp.maximum))
            mn = jnp.minimum(mn, fold8(lowest, jnp.minimum))
            npos = npos + fold8(jnp.where(sc > 0.0, 1.0, 0.0), jnp.add)
            nnon = nnon + fold8(jnp.where(sc >= 0.0, 1.0, 0.0), jnp.add)
        return mx, mn, npos, nnon

    stat0 = (jnp.full((8, tq), -jnp.inf, F32), jnp.full((8, tq), jnp.inf, F32),
             jnp.zeros((8, tq), F32), jnp.zeros((8, tq), F32))
    n_inner = (t0 + 1) // (2 * tka)
    stat = lax.fori_loop(0, n_inner, functools.partial(score_pair, masked=False), stat0)
    mx, mn, npos, nnon = lax.fori_loop(n_inner, npair, functools.partial(score_pair, masked=True), stat)
    top = jnp.max(mx, axis=0, keepdims=True)
    lo0 = jnp.min(mn, axis=0, keepdims=True)
    n_pos = jnp.sum(npos, axis=0, keepdims=True)
    n_nonneg = jnp.sum(nnon, axis=0, keepdims=True)


    def count(th, strict):
        def body(j, acc):
            for u in range(2):
                s = score_ref[pl.ds(pl.multiple_of((2 * j + u) * tks, tks), tks), :]
                hit = (s > th) if strict else (s >= th)
                acc = acc + fold8(jnp.where(hit, 1.0, 0.0), jnp.add)
            return acc

        acc = lax.fori_loop(0, nsub // 2, body, jnp.zeros((8, tq), F32))
        return jnp.sum(acc, axis=0, keepdims=True)

    def probe(lo, hi, it):
        lk, hk = _f32_key(lo), _f32_key(hi)
        mk = (lk >> 1) + (hk >> 1) + (lk & hk & 1)
        mv = lo + (hi - lo) * 0.5
        early = (jnp.zeros((1, tq), I32) + it) < BISECT_VALUE_STEPS
        mid = jnp.where(early & (mv > lo) & (mv < hi), mv, _key_f32(mk))
        return mid, jnp.max(jnp.where(mk != lk, 1.0, 0.0))

    def bis_cond(c):
        return (c[5] > 0.0) & (c[6] < BISECT_MAX_STEPS)

    def bis_body(c):
        lo, hi, c_lo, c_hi, mid, _, it = c
        cnt = count(mid, False)
        ge = cnt >= kf
        up = ge | (cnt == kf)
        dn = (~ge) | (cnt == kf)
        lo, c_lo = jnp.where(up, mid, lo), jnp.where(up, cnt, c_lo)
        hi, c_hi = jnp.where(dn, mid, hi), jnp.where(dn, cnt, c_hi)
        mid, active = probe(lo, hi, it + 1)
        return lo, hi, c_lo, c_hi, mid, active, it + 1

    zero = jnp.zeros((1, tq), F32)
    keep_all = qpos + 1 <= n_sel
    settled = keep_all | ((n_nonneg >= kf) & (n_pos < kf))
    above = n_pos >= kf
    c_lo0 = jnp.where(settled | above, n_nonneg, (qpos + 1).astype(F32))
    c_hi0 = jnp.where(settled | ~above, n_nonneg, zero)
    lo0 = jnp.where(settled | above, zero, lo0)
    hi0 = jnp.where(settled | ~above, zero, _key_f32(_f32_key(top) + 1))
    mid0, active0 = probe(lo0, hi0, jnp.int32(0))
    lo, hi, c_lo, c_hi, _, _, _ = lax.while_loop(bis_cond, bis_body,
                                                 (lo0, hi0, c_lo0, c_hi0, mid0, active0, jnp.int32(0)))
    at_hi = c_hi >= kf
    thr = jnp.where(keep_all, F32_LOWEST, jnp.where(at_hi, hi, lo))
    excess = jnp.where(keep_all, 0.0, jnp.where(at_hi, c_hi, c_lo) - kf)

    @pl.when(jnp.max(excess) > 0.0)
    def _():
        tied_nonzero = jnp.max(jnp.where((excess > 0.0) & (thr != 0.0), 1.0, 0.0)) > 0.0
        n_above = lax.cond(tied_nonzero, lambda: count(thr, True), lambda: n_pos)
        budget = jnp.where(excess > 0.0, kf - jnp.where(thr == 0.0, n_pos, n_above), jnp.inf)
        earlier = lax.broadcasted_iota(I32, (tks, tks), 1) < lax.broadcasted_iota(I32, (tks, tks), 0)
        earlier = jnp.where(earlier, 1.0, 0.0).astype(BF16)

        def fix(j, seen):
            tiles = []
            for u in range(2):
                r0 = pl.multiple_of((2 * j + u) * tks, tks)
                s = score_ref[pl.ds(r0, tks), :]
                eq = s == thr
                eqf = jnp.where(eq, 1.0, 0.0)
                within = jnp.dot(earlier, eqf.astype(BF16), preferred_element_type=F32)
                tiles.append((r0, s, eq, within, jnp.sum(eqf, axis=0, keepdims=True)))
            for r0, s, eq, within, n_eq in tiles:
                score_ref[pl.ds(r0, tks), :] = jnp.where(eq & (within + seen >= budget), -jnp.inf, s)
                seen = seen + n_eq
            return seen

        lax.fori_loop(0, nsub // 2, fix, jnp.zeros((1, tq), F32))

    arow = lax.broadcasted_iota(I32, (KAUG - d, tq), 0)
    for h in range(H):
        slope = float(2.0 ** (-8.0 * (h + 1) / H))
        qa_ref[h, 0:d, :] = aqt_ref[h * d:(h + 1) * d, :]
        qa_ref[h, d:KAUG, :] = jnp.where(arow == 0, slope * ALIBI_SPLIT, jnp.where(arow == 1, slope, 0.0)).astype(BF16)
    for acc in acc_refs:
        acc[...] = jnp.zeros_like(acc)

    def logits(j, h):
        ka = ak_ref[pl.ds(pl.multiple_of(j * tka, tka), tka), (h // R) * KAUG:(h // R + 1) * KAUG]
        return jnp.dot(ka, qa_ref[h], preferred_element_type=F32)

    def step(j, j_next, cur_ref, next_ref, ms):
        r0 = pl.multiple_of(j * tka, tka)
        mask_ref[...] = jnp.where(score_ref[pl.ds(r0, tka), :] >= thr, 0.0, NEG_BIG)
        nms = []
        for h in range(H):
            g = h // R
            next_ref[h] = logits(j_next, h)
            s = cur_ref[h] + mask_ref[...]
            m_new = jnp.maximum(ms[h], jnp.max(s, axis=0, keepdims=True))
            p = jnp.exp(s - m_new).astype(BF16)
            va = avt_ref[g * VAUG:(g + 1) * VAUG, pl.ds(r0, tka)]
            acc = acc_refs[h]
            acc[...] = acc[...] * jnp.exp(ms[h] - m_new) + jnp.dot(va, p, preferred_element_type=F32)
            nms.append(m_new)
        return tuple(nms)

    for h in range(H):
        sa_ref[h] = logits(0, h)

    def att_pair(i, ms):
        ms = step(2 * i, 2 * i + 1, sa_ref, sb_ref, ms)
        return step(2 * i + 1, jnp.minimum(2 * i + 2, 2 * npair - 1), sb_ref, sa_ref, ms)

    lax.fori_loop(0, npair, att_pair, tuple(jnp.full((1, tq), NEG_BIG, F32) for _ in range(H)))
    for h in range(H):
        a = acc_refs[h][...]
        o = a[0:d, :] / a[d:d + 1, :]
        o_ref[h * d:(h + 1) * d, :] = (o * ms_ref[h * d:(h + 1) * d, :]).astype(BF16)


def _dsa(iqt, iwt, aqt, ik, ak, avt, ms_att, tq, tks):
    B, _, S = iqt.shape
    assert S <= ALIBI_SPLIT * 256 and tq % (2 * tks) == 0
    assert (S // (2 * tks)) % 2 == 0
    n_sel = min(TOPK_MAX, S // 4)
    G = DSA_KV_HEADS
    colT = lambda h: pl.BlockSpec((None, h, tq), lambda b, i: (b, 0, i))
    msb = jnp.broadcast_to(ms_att.reshape(DSA_W, 1), (DSA_W, tq))
    return pl.pallas_call(
        functools.partial(_dsa_kernel, tq=tq, tks=tks, n_sel=n_sel),
        grid=(B, S // tq),
        in_specs=[colT(IDX_HEADS * IDX_HD), colT(IDX_HEADS), colT(DSA_W),
                  pl.BlockSpec((None, S, IDX_HD), lambda b, i: (b, 0, 0)),
                  pl.BlockSpec((None, S, G * KAUG), lambda b, i: (b, 0, 0)),
                  pl.BlockSpec((None, G * VAUG, S), lambda b, i: (b, 0, 0)),
                  pl.BlockSpec((DSA_W, tq), lambda b, i: (0, 0))],
        out_specs=colT(DSA_W),
        out_shape=jax.ShapeDtypeStruct((B, DSA_W, S), BF16),
        scratch_shapes=[pltpu.VMEM((S, tq), F32), pltpu.VMEM((DSA_HEADS, KAUG, tq), BF16),
                        pltpu.VMEM((DSA_HEADS, 2 * tks, tq), F32), pltpu.VMEM((DSA_HEADS, 2 * tks, tq), F32),
                        pltpu.VMEM((2 * tks, tq), F32)]
        + [pltpu.VMEM((VAUG, tq), F32) for _ in range(DSA_HEADS)],
        compiler_params=_cparams(("parallel", "arbitrary")),
        name="dsa",
    )(iqt, iwt, aqt, ik, ak, avt, msb)


def _oproj_kernel(ret_ref, att_ref, x_ref, wo_ref, g1_ref, n2_ref, sc_ref, sh_ref, rw_ref, rb_ref,
                  x1_ref, h2_ref, sel_ref, idx_ref, gate_ref, cnt_ref):
    mixo = jnp.dot(ret_ref[...], wo_ref[:RET_W, :], preferred_element_type=F32)
    mixo = mixo + lax.dot_general(att_ref[...], wo_ref[RET_W:, :], (((0,), (0,)), ((), ())),
                                  preferred_element_type=F32)
    x1 = x_ref[...] + g1_ref[...] * mixo
    x1_ref[...] = x1
    y = x1 * lax.rsqrt(jnp.mean(x1 * x1, axis=-1, keepdims=True) + EPS) * n2_ref[...]
    h2 = y * (1.0 + sc_ref[...]) + sh_ref[...]
    _store_token_major(h2_ref, h2)
    h_hi = h2.astype(BF16)
    h_lo = (h2 - h_hi.astype(F32)).astype(BF16)
    logits = jnp.dot(h_hi, rw_ref[0], preferred_element_type=F32)
    logits = logits + (jnp.dot(h_hi, rw_ref[1], preferred_element_type=F32)
                       + jnp.dot(h_lo, rw_ref[0], preferred_element_type=F32)) + rb_ref[...]
    tm = logits.shape[0]
    lane = lax.broadcasted_iota(I32, (tm, LANES), 1).astype(F32)
    work = jnp.where(lane < N_EXPERTS, logits, -jnp.inf)
    sel = jnp.zeros((tm, LANES), F32)
    idx_tab = jnp.zeros((tm, LANES), F32)
    vals = []
    for k in range(TOP_K):
        m = jnp.max(work, axis=1, keepdims=True)
        idx = jnp.min(jnp.where(work == m, lane, float(LANES)), axis=1, keepdims=True)
        hit = lane == idx
        sel = jnp.where(hit, 1.0, sel)
        idx_tab = jnp.where(lane == k, idx, idx_tab)
        work = jnp.where(hit, -jnp.inf, work)
        vals.append(m)
    es = [jnp.exp(v - vals[0]) for v in vals]
    den = es[0] + es[1] + es[2] + es[3]
    gate_tab = jnp.zeros((tm, LANES), F32)
    for k in range(TOP_K):
        gate_tab = jnp.where(lane == k, es[k] / den, gate_tab)
    sel_ref[...] = sel
    idx_ref[...] = idx_tab
    gate_ref[...] = gate_tab

    @pl.when((pl.program_id(0) == 0) & (pl.program_id(1) == 0))
    def _():
        cnt_ref[...] = jnp.zeros_like(cnt_ref)

    cnt_ref[...] += jnp.sum(sel, axis=0, keepdims=True)


def _oproj(ret, att, x, wo, g1, n2g, sc2, sh2, rw_pad, rb_pad, tm):
    B, S, D = x.shape
    nt = S // tm
    row = lambda w: pl.BlockSpec((None, tm, w), lambda b, i: (b, i, 0))
    flat = lambda w: pl.BlockSpec((tm, w), lambda b, i: (b * nt + i, 0))
    vec = pl.BlockSpec((None, 1, D), lambda b, i: (b, 0, 0))
    cst = lambda shape: pl.BlockSpec(shape, lambda b, i: (0, 0))
    sd = lambda shape, dt: jax.ShapeDtypeStruct(shape, dt)
    N = B * S
    return pl.pallas_call(
        _oproj_kernel,
        grid=(B, nt),
        in_specs=[row(RET_W), pl.BlockSpec((None, DSA_W, tm), lambda b, i: (b, 0, i)), row(D), cst((D, D)), vec,
                  cst((1, D)), vec, vec,
                  pl.BlockSpec((2, D, LANES), lambda b, i: (0, 0, 0)), cst((1, LANES))],
        out_specs=[flat(D), pl.BlockSpec((tm * (D // LANES), LANES), lambda b, i: (b * nt + i, 0)),
                   flat(LANES), flat(LANES), flat(LANES), cst((1, LANES))],
        out_shape=[sd((N, D), F32), sd((N * (D // LANES), LANES), F32), sd((N, LANES), F32), sd((N, LANES), F32),
                   sd((N, LANES), F32), sd((1, LANES), F32)],
        compiler_params=_cparams(("arbitrary", "arbitrary")),
        name="oproj",
    )(ret, att, x, wo, g1, n2g.reshape(1, D), sc2, sh2, rw_pad, rb_pad)


def _dest_kernel(sel_ref, idx_ref, pstart_ref, dest_ref, seen_ref):
    @pl.when(pl.program_id(0) == 0)
    def _():
        seen_ref[...] = jnp.zeros_like(seen_ref)

    sel = sel_ref[...]
    tm = sel.shape[0]
    earlier = lax.broadcasted_iota(I32, (tm, tm), 1) < lax.broadcasted_iota(I32, (tm, tm), 0)
    earlier = jnp.where(earlier, 1.0, 0.0).astype(BF16)
    rank = jnp.dot(earlier, sel.astype(BF16), preferred_element_type=F32) + seen_ref[...]
    dest = pstart_ref[...] + rank
    lane = lax.broadcasted_iota(I32, (tm, LANES), 1).astype(F32)
    idx_tab = idx_ref[...]
    out = jnp.zeros((tm, LANES), F32)
    for k in range(TOP_K):
        e_k = jnp.sum(jnp.where(lane == k, idx_tab, 0.0), axis=1, keepdims=True)
        d_k = jnp.sum(jnp.where(lane == e_k, dest, 0.0), axis=1, keepdims=True)
        out = jnp.where(lane == k, d_k, out)
    dest_ref[...] = out.astype(I32)
    seen_ref[...] += jnp.sum(sel, axis=0, keepdims=True)


def _dest(sel, idx_tab, pstart, tm):
    N = sel.shape[0]
    blk = pl.BlockSpec((tm, LANES), lambda i: (i, 0))
    return pl.pallas_call(
        _dest_kernel,
        grid=(N // tm,),
        in_specs=[blk, blk, pl.BlockSpec((1, LANES), lambda i: (0, 0))],
        out_specs=blk,
        out_shape=jax.ShapeDtypeStruct((N, LANES), I32),
        scratch_shapes=[pltpu.VMEM((1, LANES), F32)],
        compiler_params=_cparams(("arbitrary",)),
        name="dest",
    )(sel, idx_tab, pstart)


def _disp_kernel(zs_ref, h2_ref, dest_hbm, xs_hbm, zbuf, idx_a, idx_b, sem_ia, sem_ib, sem_row, sem_z,
                 *, tm, nch, zrows, n_steps):
    i = pl.program_id(0)
    n = tm * TOP_K

    def idx_copy(tile, buf, sem):
        return pltpu.make_async_copy(dest_hbm.at[pl.ds(tile * n, n)], buf, sem)

    def rows(ref, first, count):
        return ref.at[pl.ds(pl.multiple_of(first * nch, nch), count * nch)]

    def row_copy(row, dst):
        return pltpu.make_async_copy(rows(h2_ref, row, 1), rows(xs_hbm, dst, 1), sem_row)

    def issue_rows(first_row, idx):
        def body(t, c):
            for k in range(TOP_K):
                row_copy(first_row + t, idx[t * TOP_K + k]).start(priority=k % 2)
            return c

        lax.fori_loop(0, tm, body, 0)

    def wait_rows():
        pltpu.make_async_copy(rows(xs_hbm, 0, n), rows(xs_hbm, 0, n), sem_row).wait()

    @pl.when(i == 0)
    def _():
        zbuf[...] = jnp.zeros_like(zbuf)
        for e in range(N_EXPERTS):
            fill = pltpu.make_async_copy(zbuf, rows(xs_hbm, zs_ref[e], zrows), sem_z)
            fill.start()
            fill.wait()
        idx_copy(0, idx_a, sem_ia).start()
        idx_copy(1, idx_b, sem_ib).start()

    idx_copy(2 * i, idx_a, sem_ia).wait()
    issue_rows(0, idx_a)
    idx_copy(2 * i + 1, idx_b, sem_ib).wait()
    issue_rows(tm, idx_b)

    @pl.when(i + 1 < n_steps)
    def _():
        idx_copy(2 * i + 2, idx_a, sem_ia).start()
        idx_copy(2 * i + 3, idx_b, sem_ib).start()

    wait_rows()
    wait_rows()


def _dispatch(h2, dest_flat, zero_start, n_rows, zrows, tm):
    nch = D_MODEL // LANES
    n_steps = h2.shape[0] // nch // (2 * tm)
    grid_spec = pltpu.PrefetchScalarGridSpec(
        num_scalar_prefetch=1,
        grid=(n_steps,),
        in_specs=[pl.BlockSpec((2 * tm * nch, LANES), lambda i, zs: (i, 0)), pl.BlockSpec(memory_space=pl.ANY)],
        out_specs=pl.BlockSpec(memory_space=pl.ANY),
        scratch_shapes=[pltpu.VMEM((zrows * nch, LANES), F32), pltpu.SMEM((tm * TOP_K,), I32),
                        pltpu.SMEM((tm * TOP_K,), I32), pltpu.SemaphoreType.DMA, pltpu.SemaphoreType.DMA,
                        pltpu.SemaphoreType.DMA, pltpu.SemaphoreType.DMA],
    )
    return pl.pallas_call(
        functools.partial(_disp_kernel, tm=tm, nch=nch, zrows=zrows, n_steps=n_steps),
        grid_spec=grid_spec,
        out_shape=jax.ShapeDtypeStruct((n_rows * nch, LANES), F32),
        compiler_params=_cparams(("arbitrary",)),
        name="disp",
    )(zero_start, h2, dest_flat)


def _store_token_major(ref, x):
    rows, d = x.shape
    nch = d // LANES
    for j in range(nch):
        ref[pl.ds(j, rows, stride=nch), :] = x[:, j * LANES:(j + 1) * LANES]


def _load_token_major(ref, rows, nch):
    return jnp.concatenate([ref[pl.ds(j, rows, stride=nch), :] for j in range(nch)], axis=1)


def _ffn_kernel(be_ref, nb_ref, nxt_ref, xs_ref, wgu_hbm, bgu_ref, wd_hbm, bd_ref, ys_ref,
                wgu_st, wd_st, wgu_bf, wd_bf, sem_gu, sem_d):
    i = pl.program_id(0)
    live = i < nb_ref[0]
    e = be_ref[i]

    def fetch(ex):
        return (pltpu.make_async_copy(wgu_hbm.at[ex], wgu_st, sem_gu),
                pltpu.make_async_copy(wd_hbm.at[ex], wd_st, sem_d))

    @pl.when(i == 0)
    def _():
        for cp in fetch(e):
            cp.start()

    @pl.when(live & ((i == 0) | (e != be_ref[jnp.maximum(i - 1, 0)])))
    def _():
        for cp in fetch(e):
            cp.wait()
        wgu_bf[...] = wgu_st[...].astype(BF16)
        wd_bf[...] = wd_st[...].astype(BF16)

        @pl.when(nxt_ref[i] >= 0)
        def _():
            for cp in fetch(nxt_ref[i]):
                cp.start()

    @pl.when(live)
    def _():
        xb = _load_token_major(xs_ref, ys_ref.shape[0] // NCH, NCH).astype(BF16)
        gu = jnp.dot(xb, wgu_bf[...], preferred_element_type=F32) + bgu_ref[...]
        gate = jnp.minimum(gu[:, :D_EXPERT], SWIGLU_LIMIT)
        up = jnp.clip(gu[:, D_EXPERT:], -SWIGLU_LIMIT, SWIGLU_LIMIT)
        glu = gate * (1.0 / (1.0 + jnp.exp(-SWIGLU_ALPHA * gate)))
        act = ((up + 1.0) * glu).astype(BF16)
        ys = jnp.dot(act, wd_bf[...], preferred_element_type=F32) + bd_ref[...]
        _store_token_major(ys_ref, ys)

    @pl.when(jnp.logical_not(live))
    def _():
        ys_ref[...] = jnp.zeros_like(ys_ref)


def _ffn(xs, block_e, n_blocks, next_e, w_gu, b_gu, w_down, b_down, tmb):
    D = D_MODEL
    P = xs.shape[0] // NCH
    E = w_gu.shape[0]
    blk = lambda i, be, nb, nx: (jnp.minimum(i, nb[0] - 1), 0)
    wsel = lambda i, be, nb, nx: (be[jnp.minimum(i, nb[0] - 1)], 0, 0)
    grid_spec = pltpu.PrefetchScalarGridSpec(
        num_scalar_prefetch=3,
        grid=(P // tmb,),
        in_specs=[pl.BlockSpec((tmb * NCH, LANES), blk),
                  pl.BlockSpec(memory_space=pl.ANY),
                  pl.BlockSpec((None, 1, 2 * D_EXPERT), wsel),
                  pl.BlockSpec(memory_space=pl.ANY),
                  pl.BlockSpec((None, 1, D), wsel)],
        out_specs=pl.BlockSpec((tmb * (D // LANES), LANES), lambda i, be, nb, nx: (i, 0)),
        scratch_shapes=[pltpu.VMEM((D, 2 * D_EXPERT), F32), pltpu.VMEM((D_EXPERT, D), F32),
                        pltpu.VMEM((D, 2 * D_EXPERT), BF16), pltpu.VMEM((D_EXPERT, D), BF16),
                        pltpu.SemaphoreType.DMA, pltpu.SemaphoreType.DMA],
    )
    return pl.pallas_call(
        _ffn_kernel,
        grid_spec=grid_spec,
        out_shape=jax.ShapeDtypeStruct((P * (D // LANES), LANES), F32),
        compiler_params=_cparams(("arbitrary",)),
        name="ffn",
    )(block_e, n_blocks, next_e, xs, w_gu, b_gu.reshape(E, 1, 2 * D_EXPERT), w_down, b_down.reshape(E, 1, D))


def _comb_kernel(x1_ref, gate_ref, g2_ref, fg_ref, dest_hbm, ys_hbm, o_ref, buf_a, buf_b, idx_a, idx_b,
                 sem_ia, sem_ib, sem_ra, sem_rb, *, tm, n_steps):
    i = pl.program_id(0)
    n = tm * TOP_K

    def idx_copy(tile, buf, sem):
        return pltpu.make_async_copy(dest_hbm.at[pl.ds(tile * n, n)], buf, sem)

    nch = x1_ref.shape[1] // LANES

    def row_copy(src, buf, k, t, sem):
        return pltpu.make_async_copy(ys_hbm.at[pl.ds(pl.multiple_of(src * nch, nch), nch)],
                                     buf.at[k, pl.ds(pl.multiple_of(t * nch, nch), nch)], sem)

    def issue_rows(idx, buf, sem):
        def body(t, c):
            for k in range(TOP_K):
                row_copy(idx[t * TOP_K + k], buf, k, t, sem).start()
            return c

        lax.fori_loop(0, tm, body, 0)

    def wait_rows(buf, sem):
        pltpu.make_async_copy(buf, buf, sem).wait()

    def finish(buf, lo):
        gates = gate_ref[lo:lo + tm, :]
        y = jnp.zeros((tm, x1_ref.shape[1]), F32)
        for k in range(TOP_K):
            y = y + gates[:, k:k + 1] * _load_token_major(buf.at[k], tm, nch)
        v = x1_ref[lo:lo + tm, :] + g2_ref[...] * y
        o_ref[lo:lo + tm, :] = v * lax.rsqrt(jnp.mean(v * v, axis=-1, keepdims=True) + EPS) * fg_ref[...]

    @pl.when(i == 0)
    def _():
        first = idx_copy(0, idx_a, sem_ia)
        first.start()
        first.wait()
        issue_rows(idx_a, buf_a, sem_ra)
        idx_copy(1, idx_b, sem_ib).start()

    idx_copy(2 * i + 1, idx_b, sem_ib).wait()
    issue_rows(idx_b, buf_b, sem_rb)

    @pl.when(i + 1 < n_steps)
    def _():
        idx_copy(2 * i + 2, idx_a, sem_ia).start()

    wait_rows(buf_a, sem_ra)
    finish(buf_a, 0)

    @pl.when(i + 1 < n_steps)
    def _():
        idx_copy(2 * i + 2, idx_a, sem_ia).wait()
        issue_rows(idx_a, buf_a, sem_ra)
        idx_copy(2 * i + 3, idx_b, sem_ib).start()

    wait_rows(buf_b, sem_rb)
    finish(buf_b, tm)


def _combine(x1, gate_tab, g2, final_g, dest_flat, ys, S, tm):
    N, D = x1.shape
    n_steps = N // (2 * tm)
    per_b = S // (2 * tm)
    return pl.pallas_call(
        functools.partial(_comb_kernel, tm=tm, n_steps=n_steps),
        grid=(n_steps,),
        in_specs=[pl.BlockSpec((2 * tm, D), lambda i: (i, 0)),
                  pl.BlockSpec((2 * tm, LANES), lambda i: (i, 0)),
                  pl.BlockSpec((None, 1, D), lambda i: (i // per_b, 0, 0)),
                  pl.BlockSpec((1, D), lambda i: (0, 0)),
                  pl.BlockSpec(memory_space=pl.ANY),
                  pl.BlockSpec(memory_space=pl.ANY)],
        out_specs=pl.BlockSpec((2 * tm, D), lambda i: (i, 0)),
        out_shape=jax.ShapeDtypeStruct((N, D), F32),
        scratch_shapes=[pltpu.VMEM((TOP_K, tm * (D // LANES), LANES), F32),
                        pltpu.VMEM((TOP_K, tm * (D // LANES), LANES), F32),
                        pltpu.SMEM((tm * TOP_K,), I32), pltpu.SMEM((tm * TOP_K,), I32),
                        pltpu.SemaphoreType.DMA, pltpu.SemaphoreType.DMA, pltpu.SemaphoreType.DMA,
                        pltpu.SemaphoreType.DMA],
        compiler_params=_cparams(("arbitrary",)),
        name="comb",
    )(x1, gate_tab, g2, final_g.reshape(1, D), dest_flat, ys)


def _tile(n, pref):
    t = min(pref, n)
    assert n % t == 0, (n, t)
    return t


def _layer(x, c, ada_w, ada_b, norm1_g, w_in, mix_scale, w_o, norm2_g,
           router_w, router_b, w_gu, b_gu, w_down, b_down, final_g):
    B, S, D = x.shape
    N = B * S
    mod = _mod(c, ada_w, ada_b).reshape(B, 6, 1, D)
    sh1, sc1, g1, sh2, sc2, g2 = (mod[:, j] for j in range(6))

    w_pad = jnp.pad(w_in, ((0, 0), (0, IN_COLS_PAD - IN_COLS))).astype(BF16)
    rq, rk, rv, rg, aqt, ak, avt, iqt, ik, iwt = _inproj(x, norm1_g, sc1, sh1, w_pad, _tile(S, PROJ_ROWS))
    ms = mix_scale.reshape(1, RET_W + DSA_W)
    ret = _retention(rq, rk, rv, rg, ms[:, :RET_W])
    att = _dsa(iqt, iwt, aqt, ik, ak, avt, ms[:, RET_W:], _tile(S, DSA_QUERIES), DSA_KEYS)

    rw_pad = jnp.pad(router_w, ((0, 0), (0, LANES - N_EXPERTS)))
    rw_hi = rw_pad.astype(BF16)
    rw_pad = jnp.stack([rw_hi, (rw_pad - rw_hi.astype(F32)).astype(BF16)])
    rb_pad = jnp.pad(router_b, (0, LANES - N_EXPERTS)).reshape(1, LANES)
    x1, h2, sel, idx_tab, gate_tab, counts = _oproj(ret, att, x, w_o.astype(BF16), g1, norm2_g, sc2, sh2,
                                                    rw_pad, rb_pad, _tile(S, PROJ_ROWS))

    tmb = FFN_ROWS
    n_rows = (N * TOP_K + N_EXPERTS * (tmb - 1)) // tmb * tmb + tmb
    cnt = counts[0, :N_EXPERTS].astype(I32)
    padded = (cnt + tmb - 1) // tmb * tmb
    ends = jnp.cumsum(padded)
    starts = ends - padded
    pstart = jnp.pad(starts.astype(F32), (0, LANES - N_EXPERTS)).reshape(1, LANES)
    n_blocks = (ends[-1] // tmb).reshape(1)
    first_row = jnp.arange(n_rows // tmb, dtype=I32) * tmb
    block_e = jnp.minimum(jnp.sum((ends[None, :] <= first_row[:, None]).astype(I32), axis=1), N_EXPERTS - 1)

    tmd = _tile(N, MOE_TOKENS)
    dest_tab = _dest(sel, idx_tab, pstart, tmd)
    dest_flat = dest_tab[:, :TOP_K].reshape(N * TOP_K)
    xs = _dispatch(h2, dest_flat, starts + cnt, n_rows, tmb, tmd)
    eid = jnp.arange(N_EXPERTS, dtype=I32)
    later_used = (eid[None, :] > eid[:, None]) & (padded[None, :] > 0)
    next_used = jnp.min(jnp.where(later_used, eid[None, :], N_EXPERTS), axis=1)
    next_e = jnp.where(next_used < N_EXPERTS, next_used, -1)[block_e].astype(I32)
    ys = _ffn(xs, block_e, n_blocks, next_e, w_gu, b_gu, w_down, b_down, tmb)
    out = _combine(x1, gate_tab, g2, final_g, dest_flat, ys, S, _tile(S, MOE_TOKENS))
    return out.reshape(B, S, D)


def kernel(x, c, ada_w, ada_b, norm1_g, w_in, mix_scale, w_o, norm2_g, router_w, router_b, w_gu, b_gu,
           w_down, b_down, final_g):
    assert ada_w.shape[0] == 1, "single-layer stack"
    return _layer(x, c, ada_w[0], ada_b[0], norm1_g[0], w_in[0], mix_scale[0], w_o[0], norm2_g[0],
                  router_w[0], router_b[0], w_gu[0], b_gu[0], w_down[0], b_down[0], final_g)
```

```python
import functools

import numpy as np
import jax
import jax.numpy as jnp
from jax import lax
from jax.experimental import pallas as pl
from jax.experimental.pallas import tpu as pltpu

F32 = jnp.float32
BF16 = jnp.bfloat16
I32 = jnp.int32

D_MODEL = 1024
RET_HEADS = 4
RET_DK = 64
RET_DV = 128
RET_CHUNK = 128
DSA_HEADS = 8
DSA_KV_HEADS = 2
DSA_HD = 64
IDX_HEADS = 8
IDX_HD = 64
TOPK_MAX = 256
N_EXPERTS = 32
TOP_K = 4
D_EXPERT = D_MODEL
SWIGLU_LIMIT = 7.0
SWIGLU_ALPHA = 1.702
EPS = 1e-6

RET_W = RET_HEADS * RET_DV
DSA_W = DSA_HEADS * DSA_HD
IN_COLS = 2888
IN_COLS_PAD = 2944

KAUG = 128
VAUG = 80
ALIBI_SPLIT = 64
PROJ_ROWS = 512
DSA_QUERIES = 512
DSA_KEYS = 128
MOE_TOKENS = 256
RET_CHUNKS_PER_STEP = 4
FFN_ROWS = 512
BISECT_VALUE_STEPS = 8
BISECT_MAX_STEPS = 64

LANES = 128
NCH = D_MODEL // LANES
VMEM_LIMIT = 56 * 1024 * 1024
NEG_BIG = -1e30
F32_LOWEST = float(np.finfo(np.float32).min)


def _cparams(sem):
    return pltpu.CompilerParams(dimension_semantics=sem, vmem_limit_bytes=VMEM_LIMIT)


def _mod_kernel(c_ref, w_ref, b_ref, o_ref):
    c = c_ref[...]
    s = c * (1.0 / (1.0 + jnp.exp(-c)))
    o_ref[...] = jnp.dot(s, w_ref[...], preferred_element_type=F32,
                         precision=lax.Precision.HIGHEST) + b_ref[...]


def _mod(c, ada_w, ada_b):
    B, D = c.shape
    n_out = ada_w.shape[1]
    rows = 8
    c8 = jnp.zeros((rows, D), F32).at[:B].set(c)
    out = pl.pallas_call(
        _mod_kernel,
        grid=(n_out // D,),
        in_specs=[pl.BlockSpec((rows, D), lambda j: (0, 0)),
                  pl.BlockSpec((D, D), lambda j: (0, j)),
                  pl.BlockSpec((1, D), lambda j: (0, j))],
        out_specs=pl.BlockSpec((rows, D), lambda j: (0, j)),
        out_shape=jax.ShapeDtypeStruct((rows, n_out), F32),
        compiler_params=_cparams(("arbitrary",)),
        name="mod",
    )(c8, ada_w, ada_b.reshape(1, n_out))
    return out[:B]


def _inproj_kernel(x_ref, g_ref, sc_ref, sh_ref, w_ref,
                   rq_ref, rk_ref, rv_ref, rg_ref, aqt_ref, ak_ref, avt_ref, iqt_ref, ik_ref, iwt_ref):
    x = x_ref[...]
    ms = jnp.mean(x * x, axis=-1, keepdims=True)
    y = x * lax.rsqrt(ms + EPS) * g_ref[...]
    hb = (y * (1.0 + sc_ref[...]) + sh_ref[...]).astype(BF16)

    def proj(lo, hi):
        return jnp.dot(hb, w_ref[:, lo:hi], preferred_element_type=F32)

    tm = x.shape[0]
    d = DSA_HD
    rq_ref[...] = proj(0, 256).astype(BF16)
    rk_ref[...] = (proj(256, 512) * (RET_DK ** -0.5)).astype(BF16)
    rv_ref[...] = proj(512, 1024).astype(BF16)
    rg_ref[...] = proj(1024, 1536).astype(BF16)
    aqt_ref[...] = (proj(1536, 2048) * (d ** -0.5)).T.astype(BF16)
    kk = proj(2048, 2176)
    pos = pl.program_id(1) * tm + lax.broadcasted_iota(I32, (tm, d), 0)
    col = lax.broadcasted_iota(I32, (tm, d), 1)
    posblk = jnp.where(col == 0, pos // ALIBI_SPLIT, jnp.where(col == 1, pos % ALIBI_SPLIT, 0)).astype(F32)
    for g in range(DSA_KV_HEADS):
        ak_ref[:, g * KAUG:g * KAUG + d] = kk[:, g * d:(g + 1) * d].astype(BF16)
        ak_ref[:, g * KAUG + d:(g + 1) * KAUG] = posblk.astype(BF16)
    vt = proj(2176, 2304).T
    r16 = lax.broadcasted_iota(I32, (VAUG - d, tm), 0)
    onesblk = jnp.where(r16 == 0, 1.0, 0.0).astype(BF16)
    for g in range(DSA_KV_HEADS):
        avt_ref[g * VAUG:g * VAUG + d, :] = vt[g * d:(g + 1) * d, :].astype(BF16)
        avt_ref[g * VAUG + d:(g + 1) * VAUG, :] = onesblk
    iqt_ref[...] = proj(2304, 2816).T.astype(BF16)
    last = proj(2816, 2944)
    ik_ref[...] = last[:, :IDX_HD].astype(BF16)
    iwt_ref[...] = last.T[IDX_HD:IDX_HD + IDX_HEADS, :] * ((IDX_HD ** -0.5) * (IDX_HEADS ** -0.5))


def _inproj(x, norm_g, sc, sh, w_pad, tm):
    B, S, D = x.shape
    row = lambda w: pl.BlockSpec((None, tm, w), lambda b, i: (b, i, 0))
    colT = lambda h: pl.BlockSpec((None, h, tm), lambda b, i: (b, 0, i))
    vec = pl.BlockSpec((None, 1, D), lambda b, i: (b, 0, 0))
    sd = lambda shape, dt: jax.ShapeDtypeStruct(shape, dt)
    G = DSA_KV_HEADS
    return pl.pallas_call(
        _inproj_kernel,
        grid=(B, S // tm),
        in_specs=[row(D), pl.BlockSpec((1, D), lambda b, i: (0, 0)), vec, vec,
                  pl.BlockSpec((D, IN_COLS_PAD), lambda b, i: (0, 0))],
        out_specs=[row(256), row(256), row(512), row(512), colT(DSA_W), row(G * KAUG), colT(G * VAUG),
                   colT(IDX_HEADS * IDX_HD), row(IDX_HD), colT(IDX_HEADS)],
        out_shape=[sd((B, S, 256), BF16), sd((B, S, 256), BF16), sd((B, S, 512), BF16),
                   sd((B, S, 512), BF16), sd((B, DSA_W, S), BF16), sd((B, S, G * KAUG), BF16),
                   sd((B, G * VAUG, S), BF16), sd((B, IDX_HEADS * IDX_HD, S), BF16),
                   sd((B, S, IDX_HD), BF16), sd((B, IDX_HEADS, S), F32)],
        compiler_params=_cparams(("parallel", "parallel")),
        name="inproj",
    )(x, norm_g.reshape(1, D), sc, sh, w_pad)


def _ret_kernel(rq_ref, rk_ref, rv_ref, rg_ref, din_ref, qd_ref, kd_ref, cd_ref, ms_ref, o_ref, state_ref):
    @pl.when(pl.program_id(1) == 0)
    def _():
        state_ref[...] = jnp.zeros_like(state_ref)

    C = din_ref.shape[1]
    for c in range(rq_ref.shape[0] // C):
        rows = slice(c * C, (c + 1) * C)
        for h in range(RET_HEADS):
            q = rq_ref[rows, h * RET_DK:(h + 1) * RET_DK]
            k = rk_ref[rows, h * RET_DK:(h + 1) * RET_DK]
            v = rv_ref[rows, h * RET_DV:(h + 1) * RET_DV]
            r_prev = state_ref[h]
            s = lax.dot_general(q, k, (((1,), (1,)), ((), ())), preferred_element_type=F32) * din_ref[h]
            o = jnp.dot(s.astype(BF16), v, preferred_element_type=F32)
            o = o + jnp.dot(q, r_prev.astype(BF16), preferred_element_type=F32) * qd_ref[h]
            vd = (v.astype(F32) * kd_ref[h]).astype(BF16)
            kv = lax.dot_general(k, vd, (((0,), (0,)), ((), ())), preferred_element_type=F32)
            state_ref[h] = r_prev * cd_ref[h] + kv
            o = o * lax.rsqrt(jnp.mean(o * o, axis=-1, keepdims=True) + EPS)
            g = rg_ref[rows, h * RET_DV:(h + 1) * RET_DV].astype(F32)
            gate = g * (1.0 / (1.0 + jnp.exp(-g)))
            o_ref[rows, h * RET_DV:(h + 1) * RET_DV] = (
                gate * o * ms_ref[:, h * RET_DV:(h + 1) * RET_DV]).astype(BF16)


def _ret_consts(C):
    H = RET_HEADS
    log_g = np.log1p(-np.exp2(-5.0 - np.arange(H, dtype=np.float64)))
    pos = np.arange(C, dtype=np.float64)
    diff = pos[:, None] - pos[None, :]
    d_inner = np.where(diff[None] >= 0, np.exp(np.maximum(diff, 0.0)[None] * log_g[:, None, None]), 0.0)
    q_decay = np.exp((pos + 1.0)[None] * log_g[:, None])
    k_decay = np.exp((C - 1.0 - pos)[None] * log_g[:, None])
    chunk_decay = np.exp(C * log_g)
    qd = np.broadcast_to(q_decay[:, :, None], (H, C, RET_DV))
    kd = np.broadcast_to(k_decay[:, :, None], (H, C, RET_DV))
    cd = np.broadcast_to(chunk_decay[:, None, None], (H, 1, RET_DV))
    f = lambda a: jnp.asarray(np.ascontiguousarray(a), F32)
    return f(d_inner), f(qd), f(kd), f(cd)


def _retention(rq, rk, rv, rg, ms_ret):
    B, S, _ = rq.shape
    C = min(RET_CHUNK, S)
    din, qd, kd, cd = _ret_consts(C)
    rows = _tile(S, RET_CHUNKS_PER_STEP * C)
    row = lambda w: pl.BlockSpec((None, rows, w), lambda b, n: (b, n, 0))
    full = lambda a: pl.BlockSpec(a.shape, lambda b, n: (0,) * a.ndim)
    return pl.pallas_call(
        _ret_kernel,
        grid=(B, S // rows),
        in_specs=[row(256), row(256), row(512), row(512), full(din), full(qd), full(kd), full(cd),
                  pl.BlockSpec((1, RET_W), lambda b, n: (0, 0))],
        out_specs=row(RET_W),
        out_shape=jax.ShapeDtypeStruct((B, S, RET_W), BF16),
        scratch_shapes=[pltpu.VMEM((RET_HEADS, RET_DK, RET_DV), F32)],
        compiler_params=_cparams(("parallel", "arbitrary")),
        name="ret",
    )(rq, rk, rv, rg, din, qd, kd, cd, ms_ret)


def _f32_key(x):
    i = lax.bitcast_convert_type(x, I32)
    return i ^ ((i >> 31) & 0x7FFFFFFF)


def _key_f32(k):
    return lax.bitcast_convert_type(k ^ ((k >> 31) & 0x7FFFFFFF), F32)


def _dsa_kernel(iqt_ref, iwt_ref, aqt_ref, ik_ref, ak_ref, avt_ref, ms_ref, o_ref, score_ref, qa_ref, sa_ref, sb_ref, mask_ref, *acc_refs,
                tq, tks, n_sel):
    H, G, R, d = DSA_HEADS, DSA_KV_HEADS, DSA_HEADS // DSA_KV_HEADS, DSA_HD
    t0 = pl.program_id(1) * tq
    nsub = (t0 + tq) // tks
    tka = 2 * tks
    npair = ((t0 + tq) // tka + 1) // 2
    kf = float(n_sel)
    qpos = t0 + lax.broadcasted_iota(I32, (1, tq), 1)
    krow = lax.broadcasted_iota(I32, (tks, tq), 0)

    wrow = [iwt_ref[h:h + 1, :] for h in range(IDX_HEADS)]

    def fold8(x, op):
        acc = x[0:8, :]
        for i in range(1, tks // 8):
            acc = op(acc, x[8 * i:8 * (i + 1), :])
        return acc

    def score_pair(i, carry, masked):
        mx, mn, npos, nnon = carry
        for u in range(2 * tka // tks):
            r0 = pl.multiple_of(i * 2 * tka + u * tks, tks)
            kc = ik_ref[pl.ds(r0, tks), :]
            acc = jnp.zeros((tks, tq), F32)
            for h in range(IDX_HEADS):
                rel = jnp.dot(kc, iqt_ref[h * IDX_HD:(h + 1) * IDX_HD, :], preferred_element_type=F32)
                acc = acc + jnp.maximum(rel, 0.0) * wrow[h]
            if masked:
                causal = r0 + krow <= qpos
                sc = jnp.where(causal, acc, -jnp.inf)
                lowest = jnp.where(causal, acc, jnp.inf)
            else:
                sc = lowest = acc
            score_ref[pl.ds(r0, tks), :] = sc
            mx = jnp.maximum(mx, fold8(sc, jnp.maximum))
            mn = jnp.minimum(mn, fold8(lowest, jnp.minimum))
            npos = npos + fold8(jnp.where(sc > 0.0, 1.0, 0.0), jnp.add)
            nnon = nnon + fold8(jnp.where(sc >= 0.0, 1.0, 0.0), jnp.add)
        return mx, mn, npos, nnon

    stat0 = (jnp.full((8, tq), -jnp.inf, F32), jnp.full((8, tq), jnp.inf, F32),
             jnp.zeros((8, tq), F32), jnp.zeros((8, tq), F32))
    n_inner = (t0 + 1) // (2 * tka)
    stat = lax.fori_loop(0, n_inner, functools.partial(score_pair, masked=False), stat0)
    mx, mn, npos, nnon = lax.fori_loop(n_inner, npair, functools.partial(score_pair, masked=True), stat)
    top = jnp.max(mx, axis=0, keepdims=True)
    lo0 = jnp.min(mn, axis=0, keepdims=True)
    n_pos = jnp.sum(npos, axis=0, keepdims=True)
    n_nonneg = jnp.sum(nnon, axis=0, keepdims=True)


    def count(th, strict):
        def body(j, acc):
            for u in range(2):
                s = score_ref[pl.ds(pl.multiple_of((2 * j + u) * tks, tks), tks), :]
                hit = (s > th) if strict else (s >= th)
                acc = acc + fold8(jnp.where(hit, 1.0, 0.0), jnp.add)
            return acc

        acc = lax.fori_loop(0, nsub // 2, body, jnp.zeros((8, tq), F32))
        return jnp.sum(acc, axis=0, keepdims=True)

    def probe(lo, hi, it):
        lk, hk = _f32_key(lo), _f32_key(hi)
        mk = (lk >> 1) + (hk >> 1) + (lk & hk & 1)
        mv = lo + (hi - lo) * 0.5
        early = (jnp.zeros((1, tq), I32) + it) < BISECT_VALUE_STEPS
        mid = jnp.where(early & (mv > lo) & (mv < hi), mv, _key_f32(mk))
        return mid, jnp.max(jnp.where(mk != lk, 1.0, 0.0))

    def bis_cond(c):
        return (c[5] > 0.0) & (c[6] < BISECT_MAX_STEPS)

    def bis_body(c):
        lo, hi, c_lo, c_hi, mid, _, it = c
        cnt = count(mid, False)
        ge = cnt >= kf
        up = ge | (cnt == kf)
        dn = (~ge) | (cnt == kf)
        lo, c_lo = jnp.where(up, mid, lo), jnp.where(up, cnt, c_lo)
        hi, c_hi = jnp.where(dn, mid, hi), jnp.where(dn, cnt, c_hi)
        mid, active = probe(lo, hi, it + 1)
        return lo, hi, c_lo, c_hi, mid, active, it + 1

    zero = jnp.zeros((1, tq), F32)
    keep_all = qpos + 1 <= n_sel
    settled = keep_all | ((n_nonneg >= kf) & (n_pos < kf))
    above = n_pos >= kf
    c_lo0 = jnp.where(settled | above, n_nonneg, (qpos + 1).astype(F32))
    c_hi0 = jnp.where(settled | ~above, n_nonneg, zero)
    lo0 = jnp.where(settled | above, zero, lo0)
    hi0 = jnp.where(settled | ~above, zero, _key_f32(_f32_key(top) + 1))
    mid0, active0 = probe(lo0, hi0, jnp.int32(0))
    lo, hi, c_lo, c_hi, _, _, _ = lax.while_loop(bis_cond, bis_body,
                                                 (lo0, hi0, c_lo0, c_hi0, mid0, active0, jnp.int32(0)))
    at_hi = c_hi >= kf
    thr = jnp.where(keep_all, F32_LOWEST, jnp.where(at_hi, hi, lo))
    excess = jnp.where(keep_all, 0.0, jnp.where(at_hi, c_hi, c_lo) - kf)

    @pl.when(jnp.max(excess) > 0.0)
    def _():
        tied_nonzero = jnp.max(jnp.where((excess > 0.0) & (thr != 0.0), 1.0, 0.0)) > 0.0
        n_above = lax.cond(tied_nonzero, lambda: count(thr, True), lambda: n_pos)
        budget = jnp.where(excess > 0.0, kf - jnp.where(thr == 0.0, n_pos, n_above), jnp.inf)
        earlier = lax.broadcasted_iota(I32, (tks, tks), 1) < lax.broadcasted_iota(I32, (tks, tks), 0)
        earlier = jnp.where(earlier, 1.0, 0.0).astype(BF16)

        def fix(j, seen):
            tiles = []
            for u in range(2):
                r0 = pl.multiple_of((2 * j + u) * tks, tks)
                s = score_ref[pl.ds(r0, tks), :]
                eq = s == thr
                eqf = jnp.where(eq, 1.0, 0.0)
                within = jnp.dot(earlier, eqf.astype(BF16), preferred_element_type=F32)
                tiles.append((r0, s, eq, within, jnp.sum(eqf, axis=0, keepdims=True)))
            for r0, s, eq, within, n_eq in tiles:
                score_ref[pl.ds(r0, tks), :] = jnp.where(eq & (within + seen >= budget), -jnp.inf, s)
                seen = seen + n_eq
            return seen

        lax.fori_loop(0, nsub // 2, fix, jnp.zeros((1, tq), F32))

    arow = lax.broadcasted_iota(I32, (KAUG - d, tq), 0)
    for h in range(H):
        slope = float(2.0 ** (-8.0 * (h + 1) / H))
        qa_ref[h, 0:d, :] = aqt_ref[h * d:(h + 1) * d, :]
        qa_ref[h, d:KAUG, :] = jnp.where(arow == 0, slope * ALIBI_SPLIT, jnp.where(arow == 1, slope, 0.0)).astype(BF16)
    for acc in acc_refs:
        acc[...] = jnp.zeros_like(acc)

    def logits(j, h):
        ka = ak_ref[pl.ds(pl.multiple_of(j * tka, tka), tka), (h // R) * KAUG:(h // R + 1) * KAUG]
        return jnp.dot(ka, qa_ref[h], preferred_element_type=F32)

    def step(j, j_next, cur_ref, next_ref, ms):
        r0 = pl.multiple_of(j * tka, tka)
        mask_ref[...] = jnp.where(score_ref[pl.ds(r0, tka), :] >= thr, 0.0, NEG_BIG)
        nms = []
        ahead = logits(j_next, 0)
        for h in range(H):
            g = h // R
            next_ref[h] = ahead
            if h + 1 < H:
                ahead = logits(j_next, h + 1)
            s = cur_ref[h] + mask_ref[...]
            m_new = jnp.maximum(ms[h], jnp.max(s, axis=0, keepdims=True))
            p = jnp.exp(s - m_new).astype(BF16)
            va = avt_ref[g * VAUG:(g + 1) * VAUG, pl.ds(r0, tka)]
            acc = acc_refs[h]
            acc[...] = acc[...] * jnp.exp(ms[h] - m_new) + jnp.dot(va, p, preferred_element_type=F32)
            nms.append(m_new)
        return tuple(nms)

    for h in range(H):
        sa_ref[h] = logits(0, h)

    def att_pair(i, ms):
        ms = step(2 * i, 2 * i + 1, sa_ref, sb_ref, ms)
        return step(2 * i + 1, jnp.minimum(2 * i + 2, 2 * npair - 1), sb_ref, sa_ref, ms)

    lax.fori_loop(0, npair, att_pair, tuple(jnp.full((1, tq), NEG_BIG, F32) for _ in range(H)))
    for h in range(H):
        a = acc_refs[h][...]
        o = a[0:d, :] / a[d:d + 1, :]
        o_ref[h * d:(h + 1) * d, :] = (o * ms_ref[h * d:(h + 1) * d, :]).astype(BF16)


def _dsa(iqt, iwt, aqt, ik, ak, avt, ms_att, tq, tks):
    B, _, S = iqt.shape
    assert S <= ALIBI_SPLIT * 256 and tq % (2 * tks) == 0
    assert (S // (2 * tks)) % 2 == 0
    n_sel = min(TOPK_MAX, S // 4)
    G = DSA_KV_HEADS
    colT = lambda h: pl.BlockSpec((None, h, tq), lambda b, i: (b, 0, i))
    msb = jnp.broadcast_to(ms_att.reshape(DSA_W, 1), (DSA_W, tq))
    return pl.pallas_call(
        functools.partial(_dsa_kernel, tq=tq, tks=tks, n_sel=n_sel),
        grid=(B, S // tq),
        in_specs=[colT(IDX_HEADS * IDX_HD), colT(IDX_HEADS), colT(DSA_W),
                  pl.BlockSpec((None, S, IDX_HD), lambda b, i: (b, 0, 0)),
                  pl.BlockSpec((None, S, G * KAUG), lambda b, i: (b, 0, 0)),
                  pl.BlockSpec((None, G * VAUG, S), lambda b, i: (b, 0, 0)),
                  pl.BlockSpec((DSA_W, tq), lambda b, i: (0, 0))],
        out_specs=colT(DSA_W),
        out_shape=jax.ShapeDtypeStruct((B, DSA_W, S), BF16),
        scratch_shapes=[pltpu.VMEM((S, tq), F32), pltpu.VMEM((DSA_HEADS, KAUG, tq), BF16),
                        pltpu.VMEM((DSA_HEADS, 2 * tks, tq), F32), pltpu.VMEM((DSA_HEADS, 2 * tks, tq), F32),
                        pltpu.VMEM((2 * tks, tq), F32)]
        + [pltpu.VMEM((VAUG, tq), F32) for _ in range(DSA_HEADS)],
        compiler_params=_cparams(("parallel", "arbitrary")),
        name="dsa",
    )(iqt, iwt, aqt, ik, ak, avt, msb)


def _oproj_kernel(ret_ref, att_ref, x_ref, wo_ref, g1_ref, n2_ref, sc_ref, sh_ref, rw_ref, rb_ref,
                  x1_ref, h2_ref, sel_ref, idx_ref, gate_ref, cnt_ref):
    mixo = jnp.dot(ret_ref[...], wo_ref[:RET_W, :], preferred_element_type=F32)
    mixo = mixo + lax.dot_general(att_ref[...], wo_ref[RET_W:, :], (((0,), (0,)), ((), ())),
                                  preferred_element_type=F32)
    x1 = x_ref[...] + g1_ref[...] * mixo
    x1_ref[...] = x1
    y = x1 * lax.rsqrt(jnp.mean(x1 * x1, axis=-1, keepdims=True) + EPS) * n2_ref[...]
    h2 = y * (1.0 + sc_ref[...]) + sh_ref[...]
    _store_token_major(h2_ref, h2)
    h_hi = h2.astype(BF16)
    h_lo = (h2 - h_hi.astype(F32)).astype(BF16)
    logits = jnp.dot(h_hi, rw_ref[0], preferred_element_type=F32)
    logits = logits + (jnp.dot(h_hi, rw_ref[1], preferred_element_type=F32)
                       + jnp.dot(h_lo, rw_ref[0], preferred_element_type=F32)) + rb_ref[...]
    tm = logits.shape[0]
    lane = lax.broadcasted_iota(I32, (tm, LANES), 1).astype(F32)
    work = jnp.where(lane < N_EXPERTS, logits, -jnp.inf)
    sel = jnp.zeros((tm, LANES), F32)
    idx_tab = jnp.zeros((tm, LANES), F32)
    vals = []
    for k in range(TOP_K):
        m = jnp.max(work, axis=1, keepdims=True)
        idx = jnp.min(jnp.where(work == m, lane, float(LANES)), axis=1, keepdims=True)
        hit = lane == idx
        sel = jnp.where(hit, 1.0, sel)
        idx_tab = jnp.where(lane == k, idx, idx_tab)
        work = jnp.where(hit, -jnp.inf, work)
        vals.append(m)
    es = [jnp.exp(v - vals[0]) for v in vals]
    den = es[0] + es[1] + es[2] + es[3]
    gate_tab = jnp.zeros((tm, LANES), F32)
    for k in range(TOP_K):
        gate_tab = jnp.where(lane == k, es[k] / den, gate_tab)
    sel_ref[...] = sel
    idx_ref[...] = idx_tab
    gate_ref[...] = gate_tab

    @pl.when((pl.program_id(0) == 0) & (pl.program_id(1) == 0))
    def _():
        cnt_ref[...] = jnp.zeros_like(cnt_ref)

    cnt_ref[...] += jnp.sum(sel, axis=0, keepdims=True)


def _oproj(ret, att, x, wo, g1, n2g, sc2, sh2, rw_pad, rb_pad, tm):
    B, S, D = x.shape
    nt = S // tm
    row = lambda w: pl.BlockSpec((None, tm, w), lambda b, i: (b, i, 0))
    flat = lambda w: pl.BlockSpec((tm, w), lambda b, i: (b * nt + i, 0))
    vec = pl.BlockSpec((None, 1, D), lambda b, i: (b, 0, 0))
    cst = lambda shape: pl.BlockSpec(shape, lambda b, i: (0, 0))
    sd = lambda shape, dt: jax.ShapeDtypeStruct(shape, dt)
    N = B * S
    return pl.pallas_call(
        _oproj_kernel,
        grid=(B, nt),
        in_specs=[row(RET_W), pl.BlockSpec((None, DSA_W, tm), lambda b, i: (b, 0, i)), row(D), cst((D, D)), vec,
                  cst((1, D)), vec, vec,
                  pl.BlockSpec((2, D, LANES), lambda b, i: (0, 0, 0)), cst((1, LANES))],
        out_specs=[flat(D), pl.BlockSpec((tm * (D // LANES), LANES), lambda b, i: (b * nt + i, 0)),
                   flat(LANES), flat(LANES), flat(LANES), cst((1, LANES))],
        out_shape=[sd((N, D), F32), sd((N * (D // LANES), LANES), F32), sd((N, LANES), F32), sd((N, LANES), F32),
                   sd((N, LANES), F32), sd((1, LANES), F32)],
        compiler_params=_cparams(("arbitrary", "arbitrary")),
        name="oproj",
    )(ret, att, x, wo, g1, n2g.reshape(1, D), sc2, sh2, rw_pad, rb_pad)


def _dest_kernel(sel_ref, idx_ref, pstart_ref, dest_ref, seen_ref):
    @pl.when(pl.program_id(0) == 0)
    def _():
        seen_ref[...] = jnp.zeros_like(seen_ref)

    sel = sel_ref[...]
    tm = sel.shape[0]
    earlier = lax.broadcasted_iota(I32, (tm, tm), 1) < lax.broadcasted_iota(I32, (tm, tm), 0)
    earlier = jnp.where(earlier, 1.0, 0.0).astype(BF16)
    rank = jnp.dot(earlier, sel.astype(BF16), preferred_element_type=F32) + seen_ref[...]
    dest = pstart_ref[...] + rank
    lane = lax.broadcasted_iota(I32, (tm, LANES), 1).astype(F32)
    idx_tab = idx_ref[...]
    out = jnp.zeros((tm, LANES), F32)
    for k in range(TOP_K):
        e_k = jnp.sum(jnp.where(lane == k, idx_tab, 0.0), axis=1, keepdims=True)
        d_k = jnp.sum(jnp.where(lane == e_k, dest, 0.0), axis=1, keepdims=True)
        out = jnp.where(lane == k, d_k, out)
    dest_ref[...] = out.astype(I32)
    seen_ref[...] += jnp.sum(sel, axis=0, keepdims=True)


def _dest(sel, idx_tab, pstart, tm):
    N = sel.shape[0]
    blk = pl.BlockSpec((tm, LANES), lambda i: (i, 0))
    return pl.pallas_call(
        _dest_kernel,
        grid=(N // tm,),
        in_specs=[blk, blk, pl.BlockSpec((1, LANES), lambda i: (0, 0))],
        out_specs=blk,
        out_shape=jax.ShapeDtypeStruct((N, LANES), I32),
        scratch_shapes=[pltpu.VMEM((1, LANES), F32)],
        compiler_params=_cparams(("arbitrary",)),
        name="dest",
    )(sel, idx_tab, pstart)


def _disp_kernel(zs_ref, h2_ref, dest_hbm, xs_hbm, zbuf, idx_a, idx_b, sem_ia, sem_ib, sem_row, sem_z,
                 *, tm, nch, zrows, n_steps):
    i = pl.program_id(0)
    n = tm * TOP_K

    def idx_copy(tile, buf, sem):
        return pltpu.make_async_copy(dest_hbm.at[pl.ds(tile * n, n)], buf, sem)

    def rows(ref, first, count):
        return ref.at[pl.ds(pl.multiple_of(first * nch, nch), count * nch)]

    def row_copy(row, dst):
        return pltpu.make_async_copy(rows(h2_ref, row, 1), rows(xs_hbm, dst, 1), sem_row)

    def issue_rows(first_row, idx):
        def body(t, c):
            for k in range(TOP_K):
                row_copy(first_row + t, idx[t * TOP_K + k]).start(priority=k % 2)
            return c

        lax.fori_loop(0, tm, body, 0)

    def wait_rows():
        pltpu.make_async_copy(rows(xs_hbm, 0, n), rows(xs_hbm, 0, n), sem_row).wait()

    @pl.when(i == 0)
    def _():
        zbuf[...] = jnp.zeros_like(zbuf)
        for e in range(N_EXPERTS):
            fill = pltpu.make_async_copy(zbuf, rows(xs_hbm, zs_ref[e], zrows), sem_z)
            fill.start()
            fill.wait()
        idx_copy(0, idx_a, sem_ia).start()
        idx_copy(1, idx_b, sem_ib).start()

    idx_copy(2 * i, idx_a, sem_ia).wait()
    issue_rows(0, idx_a)
    idx_copy(2 * i + 1, idx_b, sem_ib).wait()
    issue_rows(tm, idx_b)

    @pl.when(i + 1 < n_steps)
    def _():
        idx_copy(2 * i + 2, idx_a, sem_ia).start()
        idx_copy(2 * i + 3, idx_b, sem_ib).start()

    wait_rows()
    wait_rows()


def _dispatch(h2, dest_flat, zero_start, n_rows, zrows, tm):
    nch = D_MODEL // LANES
    n_steps = h2.shape[0] // nch // (2 * tm)
    grid_spec = pltpu.PrefetchScalarGridSpec(
        num_scalar_prefetch=1,
        grid=(n_steps,),
        in_specs=[pl.BlockSpec((2 * tm * nch, LANES), lambda i, zs: (i, 0)), pl.BlockSpec(memory_space=pl.ANY)],
        out_specs=pl.BlockSpec(memory_space=pl.ANY),
        scratch_shapes=[pltpu.VMEM((zrows * nch, LANES), F32), pltpu.SMEM((tm * TOP_K,), I32),
                        pltpu.SMEM((tm * TOP_K,), I32), pltpu.SemaphoreType.DMA, pltpu.SemaphoreType.DMA,
                        pltpu.SemaphoreType.DMA, pltpu.SemaphoreType.DMA],
    )
    return pl.pallas_call(
        functools.partial(_disp_kernel, tm=tm, nch=nch, zrows=zrows, n_steps=n_steps),
        grid_spec=grid_spec,
        out_shape=jax.ShapeDtypeStruct((n_rows * nch, LANES), F32),
        compiler_params=_cparams(("arbitrary",)),
        name="disp",
    )(zero_start, h2, dest_flat)


def _store_token_major(ref, x):
    rows, d = x.shape
    nch = d // LANES
    for j in range(nch):
        ref[pl.ds(j, rows, stride=nch), :] = x[:, j * LANES:(j + 1) * LANES]


def _load_token_major(ref, rows, nch):
    return jnp.concatenate([ref[pl.ds(j, rows, stride=nch), :] for j in range(nch)], axis=1)


def _ffn_kernel(be_ref, nb_ref, nxt_ref, xs_ref, wgu_hbm, bgu_ref, wd_hbm, bd_ref, ys_ref,
                wgu_st, wd_st, wgu_bf, wd_bf, sem_gu, sem_d):
    i = pl.program_id(0)
    live = i < nb_ref[0]
    e = be_ref[i]

    def fetch(ex):
        return (pltpu.make_async_copy(wgu_hbm.at[ex], wgu_st, sem_gu),
                pltpu.make_async_copy(wd_hbm.at[ex], wd_st, sem_d))

    @pl.when(i == 0)
    def _():
        for cp in fetch(e):
            cp.start()

    @pl.when(live & ((i == 0) | (e != be_ref[jnp.maximum(i - 1, 0)])))
    def _():
        for cp in fetch(e):
            cp.wait()
        wgu_bf[...] = wgu_st[...].astype(BF16)
        wd_bf[...] = wd_st[...].astype(BF16)

        @pl.when(nxt_ref[i] >= 0)
        def _():
            for cp in fetch(nxt_ref[i]):
                cp.start()

    @pl.when(live)
    def _():
        xb = _load_token_major(xs_ref, ys_ref.shape[0] // NCH, NCH).astype(BF16)
        gu = jnp.dot(xb, wgu_bf[...], preferred_element_type=F32) + bgu_ref[...]
        gate = jnp.minimum(gu[:, :D_EXPERT], SWIGLU_LIMIT)
        up = jnp.clip(gu[:, D_EXPERT:], -SWIGLU_LIMIT, SWIGLU_LIMIT)
        glu = gate * (1.0 / (1.0 + jnp.exp(-SWIGLU_ALPHA * gate)))
        act = ((up + 1.0) * glu).astype(BF16)
        ys = jnp.dot(act, wd_bf[...], preferred_element_type=F32) + bd_ref[...]
        _store_token_major(ys_ref, ys)

    @pl.when(jnp.logical_not(live))
    def _():
        ys_ref[...] = jnp.zeros_like(ys_ref)


def _ffn(xs, block_e, n_blocks, next_e, w_gu, b_gu, w_down, b_down, tmb):
    D = D_MODEL
    P = xs.shape[0] // NCH
    E = w_gu.shape[0]
    blk = lambda i, be, nb, nx: (jnp.minimum(i, nb[0] - 1), 0)
    wsel = lambda i, be, nb, nx: (be[jnp.minimum(i, nb[0] - 1)], 0, 0)
    grid_spec = pltpu.PrefetchScalarGridSpec(
        num_scalar_prefetch=3,
        grid=(P // tmb,),
        in_specs=[pl.BlockSpec((tmb * NCH, LANES), blk),
                  pl.BlockSpec(memory_space=pl.ANY),
                  pl.BlockSpec((None, 1, 2 * D_EXPERT), wsel),
                  pl.BlockSpec(memory_space=pl.ANY),
                  pl.BlockSpec((None, 1, D), wsel)],
        out_specs=pl.BlockSpec((tmb * (D // LANES), LANES), lambda i, be, nb, nx: (i, 0)),
        scratch_shapes=[pltpu.VMEM((D, 2 * D_EXPERT), F32), pltpu.VMEM((D_EXPERT, D), F32),
                        pltpu.VMEM((D, 2 * D_EXPERT), BF16), pltpu.VMEM((D_EXPERT, D), BF16),
                        pltpu.SemaphoreType.DMA, pltpu.SemaphoreType.DMA],
    )
    return pl.pallas_call(
        _ffn_kernel,
        grid_spec=grid_spec,
        out_shape=jax.ShapeDtypeStruct((P * (D // LANES), LANES), F32),
        compiler_params=_cparams(("arbitrary",)),
        name="ffn",
    )(block_e, n_blocks, next_e, xs, w_gu, b_gu.reshape(E, 1, 2 * D_EXPERT), w_down, b_down.reshape(E, 1, D))


def _comb_kernel(x1_ref, gate_ref, g2_ref, fg_ref, dest_hbm, ys_hbm, o_ref, buf_a, buf_b, idx_a, idx_b,
                 sem_ia, sem_ib, sem_ra, sem_rb, *, tm, n_steps):
    i = pl.program_id(0)
    n = tm * TOP_K

    def idx_copy(tile, buf, sem):
        return pltpu.make_async_copy(dest_hbm.at[pl.ds(tile * n, n)], buf, sem)

    nch = x1_ref.shape[1] // LANES

    def row_copy(src, buf, k, t, sem):
        return pltpu.make_async_copy(ys_hbm.at[pl.ds(pl.multiple_of(src * nch, nch), nch)],
                                     buf.at[k, pl.ds(pl.multiple_of(t * nch, nch), nch)], sem)

    def issue_rows(idx, buf, sem):
        def body(t, c):
            for k in range(TOP_K):
                row_copy(idx[t * TOP_K + k], buf, k, t, sem).start()
            return c

        lax.fori_loop(0, tm, body, 0)

    def wait_rows(buf, sem):
        pltpu.make_async_copy(buf, buf, sem).wait()

    def finish(buf, lo):
        gates = gate_ref[lo:lo + tm, :]
        y = jnp.zeros((tm, x1_ref.shape[1]), F32)
        for k in range(TOP_K):
            y = y + gates[:, k:k + 1] * _load_token_major(buf.at[k], tm, nch)
        v = x1_ref[lo:lo + tm, :] + g2_ref[...] * y
        o_ref[lo:lo + tm, :] = v * lax.rsqrt(jnp.mean(v * v, axis=-1, keepdims=True) + EPS) * fg_ref[...]

    @pl.when(i == 0)
    def _():
        first = idx_copy(0, idx_a, sem_ia)
        first.start()
        first.wait()
        issue_rows(idx_a, buf_a, sem_ra)
        idx_copy(1, idx_b, sem_ib).start()

    idx_copy(2 * i + 1, idx_b, sem_ib).wait()
    issue_rows(idx_b, buf_b, sem_rb)

    @pl.when(i + 1 < n_steps)
    def _():
        idx_copy(2 * i + 2, idx_a, sem_ia).start()

    wait_rows(buf_a, sem_ra)
    finish(buf_a, 0)

    @pl.when(i + 1 < n_steps)
    def _():
        idx_copy(2 * i + 2, idx_a, sem_ia).wait()
        issue_rows(idx_a, buf_a, sem_ra)
        idx_copy(2 * i + 3, idx_b, sem_ib).start()

    wait_rows(buf_b, sem_rb)
    finish(buf_b, tm)


def _combine(x1, gate_tab, g2, final_g, dest_flat, ys, S, tm):
    N, D = x1.shape
    n_steps = N // (2 * tm)
    per_b = S // (2 * tm)
    return pl.pallas_call(
        functools.partial(_comb_kernel, tm=tm, n_steps=n_steps),
        grid=(n_steps,),
        in_specs=[pl.BlockSpec((2 * tm, D), lambda i: (i, 0)),
                  pl.BlockSpec((2 * tm, LANES), lambda i: (i, 0)),
                  pl.BlockSpec((None, 1, D), lambda i: (i // per_b, 0, 0)),
                  pl.BlockSpec((1, D), lambda i: (0, 0)),
                  pl.BlockSpec(memory_space=pl.ANY),
                  pl.BlockSpec(memory_space=pl.ANY)],
        out_specs=pl.BlockSpec((2 * tm, D), lambda i: (i, 0)),
        out_shape=jax.ShapeDtypeStruct((N, D), F32),
        scratch_shapes=[pltpu.VMEM((TOP_K, tm * (D // LANES), LANES), F32),
                        pltpu.VMEM((TOP_K, tm * (D // LANES), LANES), F32),
                        pltpu.SMEM((tm * TOP_K,), I32), pltpu.SMEM((tm * TOP_K,), I32),
                        pltpu.SemaphoreType.DMA, pltpu.SemaphoreType.DMA, pltpu.SemaphoreType.DMA,
                        pltpu.SemaphoreType.DMA],
        compiler_params=_cparams(("arbitrary",)),
        name="comb",
    )(x1, gate_tab, g2, final_g.reshape(1, D), dest_flat, ys)


def _tile(n, pref):
    t = min(pref, n)
    assert n % t == 0, (n, t)
    return t


def _layer(x, c, ada_w, ada_b, norm1_g, w_in, mix_scale, w_o, norm2_g,
           router_w, router_b, w_gu, b_gu, w_down, b_down, final_g):
    B, S, D = x.shape
    N = B * S
    mod = _mod(c, ada_w, ada_b).reshape(B, 6, 1, D)
    sh1, sc1, g1, sh2, sc2, g2 = (mod[:, j] for j in range(6))

    w_pad = jnp.pad(w_in, ((0, 0), (0, IN_COLS_PAD - IN_COLS))).astype(BF16)
    rq, rk, rv, rg, aqt, ak, avt, iqt, ik, iwt = _inproj(x, norm1_g, sc1, sh1, w_pad, _tile(S, PROJ_ROWS))
    ms = mix_scale.reshape(1, RET_W + DSA_W)
    ret = _retention(rq, rk, rv, rg, ms[:, :RET_W])
    att = _dsa(iqt, iwt, aqt, ik, ak, avt, ms[:, RET_W:], _tile(S, DSA_QUERIES), DSA_KEYS)

    rw_pad = jnp.pad(router_w, ((0, 0), (0, LANES - N_EXPERTS)))
    rw_hi = rw_pad.astype(BF16)
    rw_pad = jnp.stack([rw_hi, (rw_pad - rw_hi.astype(F32)).astype(BF16)])
    rb_pad = jnp.pad(router_b, (0, LANES - N_EXPERTS)).reshape(1, LANES)
    x1, h2, sel, idx_tab, gate_tab, counts = _oproj(ret, att, x, w_o.astype(BF16), g1, norm2_g, sc2, sh2,
                                                    rw_pad, rb_pad, _tile(S, PROJ_ROWS))

    tmb = FFN_ROWS
    n_rows = (N * TOP_K + N_EXPERTS * (tmb - 1)) // tmb * tmb + tmb
    cnt = counts[0, :N_EXPERTS].astype(I32)
    padded = (cnt + tmb - 1) // tmb * tmb
    ends = jnp.cumsum(padded)
    starts = ends - padded
    pstart = jnp.pad(starts.astype(F32), (0, LANES - N_EXPERTS)).reshape(1, LANES)
    n_blocks = (ends[-1] // tmb).reshape(1)
    first_row = jnp.arange(n_rows // tmb, dtype=I32) * tmb
    block_e = jnp.minimum(jnp.sum((ends[None, :] <= first_row[:, None]).astype(I32), axis=1), N_EXPERTS - 1)

    tmd = _tile(N, MOE_TOKENS)
    dest_tab = _dest(sel, idx_tab, pstart, tmd)
    dest_flat = dest_tab[:, :TOP_K].reshape(N * TOP_K)
    xs = _dispatch(h2, dest_flat, starts + cnt, n_rows, tmb, tmd)
    eid = jnp.arange(N_EXPERTS, dtype=I32)
    later_used = (eid[None, :] > eid[:, None]) & (padded[None, :] > 0)
    next_used = jnp.min(jnp.where(later_used, eid[None, :], N_EXPERTS), axis=1)
    next_e = jnp.where(next_used < N_EXPERTS, next_used, -1)[block_e].astype(I32)
    ys = _ffn(xs, block_e, n_blocks, next_e, w_gu, b_gu, w_down, b_down, tmb)
    out = _combine(x1, gate_tab, g2, final_g, dest_flat, ys, S, _tile(S, MOE_TOKENS))
    return out.reshape(B, S, D)


def kernel(x, c, ada_w, ada_b, norm1_g, w_in, mix_scale, w_o, norm2_g, router_w, router_b, w_gu, b_gu,
           w_down, b_down, final_g):
    assert ada_w.shape[0] == 1, "single-layer stack"
    return _layer(x, c, ada_w[0], ada_b[0], norm1_g[0], w_in[0], mix_scale[0], w_o[0], norm2_g[0],
                  router_w[0], router_b[0], w_gu[0], b_gu[0], w_down[0], b_down[0], final_g)
```

```python
import functools

import numpy as np
import jax
import jax.numpy as jnp
from jax import lax
from jax.experimental import pallas as pl
from jax.experimental.pallas import tpu as pltpu

F32 = jnp.float32
BF16 = jnp.bfloat16
I32 = jnp.int32

D_MODEL = 1024
RET_HEADS = 4
RET_DK = 64
RET_DV = 128
RET_CHUNK = 128
DSA_HEADS = 8
DSA_KV_HEADS = 2
DSA_HD = 64
IDX_HEADS = 8
IDX_HD = 64
TOPK_MAX = 256
N_EXPERTS = 32
TOP_K = 4
D_EXPERT = D_MODEL
SWIGLU_LIMIT = 7.0
SWIGLU_ALPHA = 1.702
EPS = 1e-6

RET_W = RET_HEADS * RET_DV
DSA_W = DSA_HEADS * DSA_HD
IN_COLS = 2888
IN_COLS_PAD = 2944

KAUG = 128
VAUG = 80
ALIBI_SPLIT = 64
PROJ_ROWS = 512
DSA_QUERIES = 512
DSA_KEYS = 128
MOE_TOKENS = 256
RET_CHUNKS_PER_STEP = 4
FFN_ROWS = 512
BISECT_VALUE_STEPS = 8
BISECT_MAX_STEPS = 64

LANES = 128
NCH = D_MODEL // LANES
VMEM_LIMIT = 56 * 1024 * 1024
NEG_BIG = -1e30
F32_LOWEST = float(np.finfo(np.float32).min)


def _cparams(sem):
    return pltpu.CompilerParams(dimension_semantics=sem, vmem_limit_bytes=VMEM_LIMIT)


def _mod_kernel(c_ref, w_ref, b_ref, o_ref):
    c = c_ref[...]
    s = c * (1.0 / (1.0 + jnp.exp(-c)))
    o_ref[...] = jnp.dot(s, w_ref[...], preferred_element_type=F32,
                         precision=lax.Precision.HIGHEST) + b_ref[...]


def _mod(c, ada_w, ada_b):
    B, D = c.shape
    n_out = ada_w.shape[1]
    rows = 8
    c8 = jnp.zeros((rows, D), F32).at[:B].set(c)
    out = pl.pallas_call(
        _mod_kernel,
        grid=(n_out // D,),
        in_specs=[pl.BlockSpec((rows, D), lambda j: (0, 0)),
                  pl.BlockSpec((D, D), lambda j: (0, j)),
                  pl.BlockSpec((1, D), lambda j: (0, j))],
        out_specs=pl.BlockSpec((rows, D), lambda j: (0, j)),
        out_shape=jax.ShapeDtypeStruct((rows, n_out), F32),
        compiler_params=_cparams(("arbitrary",)),
        name="mod",
    )(c8, ada_w, ada_b.reshape(1, n_out))
    return out[:B]


def _inproj_kernel(x_ref, g_ref, sc_ref, sh_ref, w_ref,
                   rq_ref, rk_ref, rv_ref, rg_ref, aqt_ref, ak_ref, avt_ref, iqt_ref, ik_ref, iwt_ref):
    x = x_ref[...]
    ms = jnp.mean(x * x, axis=-1, keepdims=True)
    y = x * lax.rsqrt(ms + EPS) * g_ref[...]
    hb = (y * (1.0 + sc_ref[...]) + sh_ref[...]).astype(BF16)

    def proj(lo, hi):
        return jnp.dot(hb, w_ref[:, lo:hi], preferred_element_type=F32)

    tm = x.shape[0]
    d = DSA_HD
    rq_ref[...] = proj(0, 256).astype(BF16)
    rk_ref[...] = (proj(256, 512) * (RET_DK ** -0.5)).astype(BF16)
    rv_ref[...] = proj(512, 1024).astype(BF16)
    rg_ref[...] = proj(1024, 1536).astype(BF16)
    aqt_ref[...] = (proj(1536, 2048) * (d ** -0.5)).T.astype(BF16)
    kk = proj(2048, 2176)
    pos = pl.program_id(1) * tm + lax.broadcasted_iota(I32, (tm, d), 0)
    col = lax.broadcasted_iota(I32, (tm, d), 1)
    posblk = jnp.where(col == 0, pos // ALIBI_SPLIT, jnp.where(col == 1, pos % ALIBI_SPLIT, 0)).astype(F32)
    for g in range(DSA_KV_HEADS):
        ak_ref[:, g * KAUG:g * KAUG + d] = kk[:, g * d:(g + 1) * d].astype(BF16)
        ak_ref[:, g * KAUG + d:(g + 1) * KAUG] = posblk.astype(BF16)
    vt = proj(2176, 2304).T
    r16 = lax.broadcasted_iota(I32, (VAUG - d, tm), 0)
    onesblk = jnp.where(r16 == 0, 1.0, 0.0).astype(BF16)
    for g in range(DSA_KV_HEADS):
        avt_ref[g * VAUG:g * VAUG + d, :] = vt[g * d:(g + 1) * d, :].astype(BF16)
        avt_ref[g * VAUG + d:(g + 1) * VAUG, :] = onesblk
    iqt_ref[...] = proj(2304, 2816).T.astype(BF16)
    last = proj(2816, 2944)
    ik_ref[...] = last[:, :IDX_HD].astype(BF16)
    iwt_ref[...] = last.T[IDX_HD:IDX_HD + IDX_HEADS, :] * ((IDX_HD ** -0.5) * (IDX_HEADS ** -0.5))


def _inproj(x, norm_g, sc, sh, w_pad, tm):
    B, S, D = x.shape
    row = lambda w: pl.BlockSpec((None, tm, w), lambda b, i: (b, i, 0))
    colT = lambda h: pl.BlockSpec((None, h, tm), lambda b, i: (b, 0, i))
    vec = pl.BlockSpec((None, 1, D), lambda b, i: (b, 0, 0))
    sd = lambda shape, dt: jax.ShapeDtypeStruct(shape, dt)
    G = DSA_KV_HEADS
    return pl.pallas_call(
        _inproj_kernel,
        grid=(B, S // tm),
        in_specs=[row(D), pl.BlockSpec((1, D), lambda b, i: (0, 0)), vec, vec,
                  pl.BlockSpec((D, IN_COLS_PAD), lambda b, i: (0, 0))],
        out_specs=[row(256), row(256), row(512), row(512), colT(DSA_W), row(G * KAUG), colT(G * VAUG),
                   colT(IDX_HEADS * IDX_HD), row(IDX_HD), colT(IDX_HEADS)],
        out_shape=[sd((B, S, 256), BF16), sd((B, S, 256), BF16), sd((B, S, 512), BF16),
                   sd((B, S, 512), BF16), sd((B, DSA_W, S), BF16), sd((B, S, G * KAUG), BF16),
                   sd((B, G * VAUG, S), BF16), sd((B, IDX_HEADS * IDX_HD, S), BF16),
                   sd((B, S, IDX_HD), BF16), sd((B, IDX_HEADS, S), F32)],
        compiler_params=_cparams(("parallel", "parallel")),
        name="inproj",
    )(x, norm_g.reshape(1, D), sc, sh, w_pad)


def _ret_kernel(rq_ref, rk_ref, rv_ref, rg_ref, din_ref, qd_ref, kd_ref, cd_ref, ms_ref, o_ref, state_ref):
    @pl.when(pl.program_id(1) == 0)
    def _():
        state_ref[...] = jnp.zeros_like(state_ref)

    C = din_ref.shape[1]
    for c in range(rq_ref.shape[0] // C):
        rows = slice(c * C, (c + 1) * C)
        for h in range(RET_HEADS):
            q = rq_ref[rows, h * RET_DK:(h + 1) * RET_DK]
            k = rk_ref[rows, h * RET_DK:(h + 1) * RET_DK]
            v = rv_ref[rows, h * RET_DV:(h + 1) * RET_DV]
            r_prev = state_ref[h]
            s = lax.dot_general(q, k, (((1,), (1,)), ((), ())), preferred_element_type=F32) * din_ref[h]
            o = jnp.dot(s.astype(BF16), v, preferred_element_type=F32)
            o = o + jnp.dot(q, r_prev.astype(BF16), preferred_element_type=F32) * qd_ref[h]
            vd = (v.astype(F32) * kd_ref[h]).astype(BF16)
            kv = lax.dot_general(k, vd, (((0,), (0,)), ((), ())), preferred_element_type=F32)
            state_ref[h] = r_prev * cd_ref[h] + kv
            o = o * lax.rsqrt(jnp.mean(o * o, axis=-1, keepdims=True) + EPS)
            g = rg_ref[rows, h * RET_DV:(h + 1) * RET_DV].astype(F32)
            gate = g * (1.0 / (1.0 + jnp.exp(-g)))
            o_ref[rows, h * RET_DV:(h + 1) * RET_DV] = (
                gate * o * ms_ref[:, h * RET_DV:(h + 1) * RET_DV]).astype(BF16)


def _ret_consts(C):
    H = RET_HEADS
    log_g = np.log1p(-np.exp2(-5.0 - np.arange(H, dtype=np.float64)))
    pos = np.arange(C, dtype=np.float64)
    diff = pos[:, None] - pos[None, :]
    d_inner = np.where(diff[None] >= 0, np.exp(np.maximum(diff, 0.0)[None] * log_g[:, None, None]), 0.0)
    q_decay = np.exp((pos + 1.0)[None] * log_g[:, None])
    k_decay = np.exp((C - 1.0 - pos)[None] * log_g[:, None])
    chunk_decay = np.exp(C * log_g)
    qd = np.broadcast_to(q_decay[:, :, None], (H, C, RET_DV))
    kd = np.broadcast_to(k_decay[:, :, None], (H, C, RET_DV))
    cd = np.broadcast_to(chunk_decay[:, None, None], (H, 1, RET_DV))
    f = lambda a: jnp.asarray(np.ascontiguousarray(a), F32)
    return f(d_inner), f(qd), f(kd), f(cd)


def _retention(rq, rk, rv, rg, ms_ret):
    B, S, _ = rq.shape
    C = min(RET_CHUNK, S)
    din, qd, kd, cd = _ret_consts(C)
    rows = _tile(S, RET_CHUNKS_PER_STEP * C)
    row = lambda w: pl.BlockSpec((None, rows, w), lambda b, n: (b, n, 0))
    full = lambda a: pl.BlockSpec(a.shape, lambda b, n: (0,) * a.ndim)
    return pl.pallas_call(
        _ret_kernel,
        grid=(B, S // rows),
        in_specs=[row(256), row(256), row(512), row(512), full(din), full(qd), full(kd), full(cd),
                  pl.BlockSpec((1, RET_W), lambda b, n: (0, 0))],
        out_specs=row(RET_W),
        out_shape=jax.ShapeDtypeStruct((B, S, RET_W), BF16),
        scratch_shapes=[pltpu.VMEM((RET_HEADS, RET_DK, RET_DV), F32)],
        compiler_params=_cparams(("parallel", "arbitrary")),
        name="ret",
    )(rq, rk, rv, rg, din, qd, kd, cd, ms_ret)


def _f32_key(x):
    i = lax.bitcast_convert_type(x, I32)
    return i ^ ((i >> 31) & 0x7FFFFFFF)


def _key_f32(k):
    return lax.bitcast_convert_type(k ^ ((k >> 31) & 0x7FFFFFFF), F32)


def _dsa_kernel(iqt_ref, iwt_ref, aqt_ref, ik_ref, ak_ref, avt_ref, ms_ref, o_ref, score_ref, qa_ref, sa_ref, sb_ref, mask_ref, *acc_refs,
                tq, tks, n_sel):
    H, G, R, d = DSA_HEADS, DSA_KV_HEADS, DSA_HEADS // DSA_KV_HEADS, DSA_HD
    t0 = pl.program_id(1) * tq
    nsub = (t0 + tq) // tks
    tka = 2 * tks
    npair = ((t0 + tq) // tka + 1) // 2
    kf = float(n_sel)
    qpos = t0 + lax.broadcasted_iota(I32, (1, tq), 1)
    krow = lax.broadcasted_iota(I32, (tks, tq), 0)

    wrow = [iwt_ref[h:h + 1, :] for h in range(IDX_HEADS)]

    def fold8(x, op):
        acc = x[0:8, :]
        for i in range(1, tks // 8):
            acc = op(acc, x[8 * i:8 * (i + 1), :])
        return acc

    def score_pair(i, carry, masked):
        mx, mn, npos, nnon = carry
        for u in range(2 * tka // tks):
            r0 = pl.multiple_of(i * 2 * tka + u * tks, tks)
            kc = ik_ref[pl.ds(r0, tks), :]
            acc = jnp.zeros((tks, tq), F32)
            for h in range(IDX_HEADS):
                rel = jnp.dot(kc, iqt_ref[h * IDX_HD:(h + 1) * IDX_HD, :], preferred_element_type=F32)
                acc = acc + jnp.maximum(rel, 0.0) * wrow[h]
            if masked:
                causal = r0 + krow <= qpos
                sc = jnp.where(causal, acc, -jnp.inf)
                lowest = jnp.where(causal, acc, jnp.inf)
            else:
                sc = lowest = acc
            score_ref[pl.ds(r0, tks), :] = sc
            mx = jnp.maximum(mx, fold8(sc, jnp.maximum))
            mn = jnp.minimum(mn, fold8(lowest, jnp.minimum))
            npos = npos + fold8(jnp.where(sc > 0.0, 1.0, 0.0), jnp.add)
            nnon = nnon + fold8(jnp.where(sc >= 0.0, 1.0, 0.0), jnp.add)
        return mx, mn, npos, nnon

    stat0 = (jnp.full((8, tq), -jnp.inf, F32), jnp.full((8, tq), jnp.inf, F32),
             jnp.zeros((8, tq), F32), jnp.zeros((8, tq), F32))
    n_inner = (t0 + 1) // (2 * tka)
    stat = lax.fori_loop(0, n_inner, functools.partial(score_pair, masked=False), stat0)
    mx, mn, npos, nnon = lax.fori_loop(n_inner, npair, functools.partial(score_pair, masked=True), stat)
    top = jnp.max(mx, axis=0, keepdims=True)
    lo0 = jnp.min(mn, axis=0, keepdims=True)
    n_pos = jnp.sum(npos, axis=0, keepdims=True)
    n_nonneg = jnp.sum(nnon, axis=0, keepdims=True)


    def count(th, strict):
        def body(j, acc):
            for u in range(2):
                s = score_ref[pl.ds(pl.multiple_of((2 * j + u) * tks, tks), tks), :]
                hit = (s > th) if strict else (s >= th)
                acc = acc + fold8(jnp.where(hit, 1.0, 0.0), jnp.add)
            return acc

        acc = lax.fori_loop(0, nsub // 2, body, jnp.zeros((8, tq), F32))
        return jnp.sum(acc, axis=0, keepdims=True)

    def probe(lo, hi, it):
        lk, hk = _f32_key(lo), _f32_key(hi)
        mk = (lk >> 1) + (hk >> 1) + (lk & hk & 1)
        mv = lo + (hi - lo) * 0.5
        early = (jnp.zeros((1, tq), I32) + it) < BISECT_VALUE_STEPS
        mid = jnp.where(early & (mv > lo) & (mv < hi), mv, _key_f32(mk))
        return mid, jnp.max(jnp.where(mk != lk, 1.0, 0.0))

    def bis_cond(c):
        return (c[5] > 0.0) & (c[6] < BISECT_MAX_STEPS)

    def bis_body(c):
        lo, hi, c_lo, c_hi, mid, _, it = c
        cnt = count(mid, False)
        ge = cnt >= kf
        up = ge | (cnt == kf)
        dn = (~ge) | (cnt == kf)
        lo, c_lo = jnp.where(up, mid, lo), jnp.where(up, cnt, c_lo)
        hi, c_hi = jnp.where(dn, mid, hi), jnp.where(dn, cnt, c_hi)
        mid, active = probe(lo, hi, it + 1)
        return lo, hi, c_lo, c_hi, mid, active, it + 1

    zero = jnp.zeros((1, tq), F32)
    keep_all = qpos + 1 <= n_sel
    settled = keep_all | ((n_nonneg >= kf) & (n_pos < kf))
    above = n_pos >= kf
    c_lo0 = jnp.where(settled | above, n_nonneg, (qpos + 1).astype(F32))
    c_hi0 = jnp.where(settled | ~above, n_nonneg, zero)
    lo0 = jnp.where(settled | above, zero, lo0)
    hi0 = jnp.where(settled | ~above, zero, _key_f32(_f32_key(top) + 1))
    mid0, active0 = probe(lo0, hi0, jnp.int32(0))
    lo, hi, c_lo, c_hi, _, _, _ = lax.while_loop(bis_cond, bis_body,
                                                 (lo0, hi0, c_lo0, c_hi0, mid0, active0, jnp.int32(0)))
    at_hi = c_hi >= kf
    thr = jnp.where(keep_all, F32_LOWEST, jnp.where(at_hi, hi, lo))
    excess = jnp.where(keep_all, 0.0, jnp.where(at_hi, c_hi, c_lo) - kf)

    @pl.when(jnp.max(excess) > 0.0)
    def _():
        tied_nonzero = jnp.max(jnp.where((excess > 0.0) & (thr != 0.0), 1.0, 0.0)) > 0.0
        n_above = lax.cond(tied_nonzero, lambda: count(thr, True), lambda: n_pos)
        budget = jnp.where(excess > 0.0, kf - jnp.where(thr == 0.0, n_pos, n_above), jnp.inf)
        earlier = lax.broadcasted_iota(I32, (tks, tks), 1) < lax.broadcasted_iota(I32, (tks, tks), 0)
        earlier = jnp.where(earlier, 1.0, 0.0).astype(BF16)

        def fix(j, seen):
            tiles = []
            for u in range(2):
                r0 = pl.multiple_of((2 * j + u) * tks, tks)
                s = score_ref[pl.ds(r0, tks), :]
                eq = s == thr
                eqf = jnp.where(eq, 1.0, 0.0)
                within = jnp.dot(earlier, eqf.astype(BF16), preferred_element_type=F32)
                tiles.append((r0, s, eq, within, jnp.sum(eqf, axis=0, keepdims=True)))
            for r0, s, eq, within, n_eq in tiles:
                score_ref[pl.ds(r0, tks), :] = jnp.where(eq & (within + seen >= budget), -jnp.inf, s)
                seen = seen + n_eq
            return seen

        lax.fori_loop(0, nsub // 2, fix, jnp.zeros((1, tq), F32))

    arow = lax.broadcasted_iota(I32, (KAUG - d, tq), 0)
    for h in range(H):
        slope = float(2.0 ** (-8.0 * (h + 1) / H))
        qa_ref[h, 0:d, :] = aqt_ref[h * d:(h + 1) * d, :]
        qa_ref[h, d:KAUG, :] = jnp.where(arow == 0, slope * ALIBI_SPLIT, jnp.where(arow == 1, slope, 0.0)).astype(BF16)
    for acc in acc_refs:
        acc[...] = jnp.zeros_like(acc)

    def logits(j, h):
        ka = ak_ref[pl.ds(pl.multiple_of(j * tka, tka), tka), (h // R) * KAUG:(h // R + 1) * KAUG]
        return jnp.dot(ka, qa_ref[h], preferred_element_type=F32)

    def step(j, j_next, cur_ref, next_ref, ms):
        r0 = pl.multiple_of(j * tka, tka)
        mask_ref[...] = jnp.where(score_ref[pl.ds(r0, tka), :] >= thr, 0.0, NEG_BIG)
        nms = []
        ahead = logits(j_next, 0)
        for h in range(H):
            g = h // R
            next_ref[h] = ahead
            if h + 1 < H:
                ahead = logits(j_next, h + 1)
            s = cur_ref[h] + mask_ref[...]
            m_new = jnp.maximum(ms[h], jnp.max(s, axis=0, keepdims=True))
            p = jnp.exp(s - m_new).astype(BF16)
            va = avt_ref[g * VAUG:(g + 1) * VAUG, pl.ds(r0, tka)]
            acc = acc_refs[h]
            acc[...] = acc[...] * jnp.exp(ms[h] - m_new) + jnp.dot(va, p, preferred_element_type=F32)
            nms.append(m_new)
        return tuple(nms)

    for h in range(H):
        sa_ref[h] = logits(0, h)

    def att_pair(i, ms):
        ms = step(2 * i, 2 * i + 1, sa_ref, sb_ref, ms)
        return step(2 * i + 1, jnp.minimum(2 * i + 2, 2 * npair - 1), sb_ref, sa_ref, ms)

    lax.fori_loop(0, npair, att_pair, tuple(jnp.full((1, tq), NEG_BIG, F32) for _ in range(H)))
    for h in range(H):
        a = acc_refs[h][...]
        o = a[0:d, :] / a[d:d + 1, :]
        o_ref[h * d:(h + 1) * d, :] = (o * ms_ref[h * d:(h + 1) * d, :]).astype(BF16)


def _dsa(iqt, iwt, aqt, ik, ak, avt, ms_att, tq, tks):
    B, _, S = iqt.shape
    assert S <= ALIBI_SPLIT * 256 and tq % (2 * tks) == 0
    assert (S // (2 * tks)) % 2 == 0
    n_sel = min(TOPK_MAX, S // 4)
    G = DSA_KV_HEADS
    colT = lambda h: pl.BlockSpec((None, h, tq), lambda b, i: (b, 0, i))
    msb = jnp.broadcast_to(ms_att.reshape(DSA_W, 1), (DSA_W, tq))
    return pl.pallas_call(
        functools.partial(_dsa_kernel, tq=tq, tks=tks, n_sel=n_sel),
        grid=(B, S // tq),
        in_specs=[colT(IDX_HEADS * IDX_HD), colT(IDX_HEADS), colT(DSA_W),
                  pl.BlockSpec((None, S, IDX_HD), lambda b, i: (b, 0, 0)),
                  pl.BlockSpec((None, S, G * KAUG), lambda b, i: (b, 0, 0)),
                  pl.BlockSpec((None, G * VAUG, S), lambda b, i: (b, 0, 0)),
                  pl.BlockSpec((DSA_W, tq), lambda b, i: (0, 0))],
        out_specs=colT(DSA_W),
        out_shape=jax.ShapeDtypeStruct((B, DSA_W, S), BF16),
        scratch_shapes=[pltpu.VMEM((S, tq), F32), pltpu.VMEM((DSA_HEADS, KAUG, tq), BF16),
                        pltpu.VMEM((DSA_HEADS, 2 * tks, tq), F32), pltpu.VMEM((DSA_HEADS, 2 * tks, tq), F32),
                        pltpu.VMEM((2 * tks, tq), F32)]
        + [pltpu.VMEM((VAUG, tq), F32) for _ in range(DSA_HEADS)],
        compiler_params=_cparams(("parallel", "arbitrary")),
        name="dsa",
    )(iqt, iwt, aqt, ik, ak, avt, msb)


def _oproj_kernel(ret_ref, att_ref, x_ref, wo_ref, g1_ref, n2_ref, sc_ref, sh_ref, rw_ref, rb_ref,
                  x1_ref, h2_ref, sel_ref, idx_ref, gate_ref, cnt_ref):
    mixo = jnp.dot(ret_ref[...], wo_ref[:RET_W, :], preferred_element_type=F32)
    mixo = mixo + lax.dot_general(att_ref[...], wo_ref[RET_W:, :], (((0,), (0,)), ((), ())),
                                  preferred_element_type=F32)
    x1 = x_ref[...] + g1_ref[...] * mixo
    x1_ref[...] = x1
    y = x1 * lax.rsqrt(jnp.mean(x1 * x1, axis=-1, keepdims=True) + EPS) * n2_ref[...]
    h2 = y * (1.0 + sc_ref[...]) + sh_ref[...]
    _store_token_major(h2_ref, h2)
    h_hi = h2.astype(BF16)
    h_lo = (h2 - h_hi.astype(F32)).astype(BF16)
    logits = jnp.dot(h_hi, rw_ref[0], preferred_element_type=F32)
    logits = logits + (jnp.dot(h_hi, rw_ref[1], preferred_element_type=F32)
                       + jnp.dot(h_lo, rw_ref[0], preferred_element_type=F32)) + rb_ref[...]
    tm = logits.shape[0]
    lane = lax.broadcasted_iota(I32, (tm, LANES), 1).astype(F32)
    work = jnp.where(lane < N_EXPERTS, logits, -jnp.inf)
    sel = jnp.zeros((tm, LANES), F32)
    idx_tab = jnp.zeros((tm, LANES), F32)
    vals = []
    for k in range(TOP_K):
        m = jnp.max(work, axis=1, keepdims=True)
        idx = jnp.min(jnp.where(work == m, lane, float(LANES)), axis=1, keepdims=True)
        hit = lane == idx
        sel = jnp.where(hit, 1.0, sel)
        idx_tab = jnp.where(lane == k, idx, idx_tab)
        work = jnp.where(hit, -jnp.inf, work)
        vals.append(m)
    es = [jnp.exp(v - vals[0]) for v in vals]
    den = es[0] + es[1] + es[2] + es[3]
    gate_tab = jnp.zeros((tm, LANES), F32)
    for k in range(TOP_K):
        gate_tab = jnp.where(lane == k, es[k] / den, gate_tab)
    sel_ref[...] = sel
    idx_ref[...] = idx_tab
    gate_ref[...] = gate_tab

    @pl.when((pl.program_id(0) == 0) & (pl.program_id(1) == 0))
    def _():
        cnt_ref[...] = jnp.zeros_like(cnt_ref)

    cnt_ref[...] += jnp.sum(sel, axis=0, keepdims=True)


def _oproj(ret, att, x, wo, g1, n2g, sc2, sh2, rw_pad, rb_pad, tm):
    B, S, D = x.shape
    nt = S // tm
    row = lambda w: pl.BlockSpec((None, tm, w), lambda b, i: (b, i, 0))
    flat = lambda w: pl.BlockSpec((tm, w), lambda b, i: (b * nt + i, 0))
    vec = pl.BlockSpec((None, 1, D), lambda b, i: (b, 0, 0))
    cst = lambda shape: pl.BlockSpec(shape, lambda b, i: (0, 0))
    sd = lambda shape, dt: jax.ShapeDtypeStruct(shape, dt)
    N = B * S
    return pl.pallas_call(
        _oproj_kernel,
        grid=(B, nt),
        in_specs=[row(RET_W), pl.BlockSpec((None, DSA_W, tm), lambda b, i: (b, 0, i)), row(D), cst((D, D)), vec,
                  cst((1, D)), vec, vec,
                  pl.BlockSpec((2, D, LANES), lambda b, i: (0, 0, 0)), cst((1, LANES))],
        out_specs=[flat(D), pl.BlockSpec((tm * (D // LANES), LANES), lambda b, i: (b * nt + i, 0)),
                   flat(LANES), flat(LANES), flat(LANES), cst((1, LANES))],
        out_shape=[sd((N, D), F32), sd((N * (D // LANES), LANES), F32), sd((N, LANES), F32), sd((N, LANES), F32),
                   sd((N, LANES), F32), sd((1, LANES), F32)],
        compiler_params=_cparams(("arbitrary", "arbitrary")),
        name="oproj",
    )(ret, att, x, wo, g1, n2g.reshape(1, D), sc2, sh2, rw_pad, rb_pad)


def _dest_kernel(sel_ref, idx_ref, pstart_ref, dest_ref, seen_ref):
    @pl.when(pl.program_id(0) == 0)
    def _():
        seen_ref[...] = jnp.zeros_like(seen_ref)

    sel = sel_ref[...]
    tm = sel.shape[0]
    earlier = lax.broadcasted_iota(I32, (tm, tm), 1) < lax.broadcasted_iota(I32, (tm, tm), 0)
    earlier = jnp.where(earlier, 1.0, 0.0).astype(BF16)
    rank = jnp.dot(earlier, sel.astype(BF16), preferred_element_type=F32) + seen_ref[...]
    dest = pstart_ref[...] + rank
    lane = lax.broadcasted_iota(I32, (tm, LANES), 1).astype(F32)
    idx_tab = idx_ref[...]
    out = jnp.zeros((tm, LANES), F32)
    for k in range(TOP_K):
        e_k = jnp.sum(jnp.where(lane == k, idx_tab, 0.0), axis=1, keepdims=True)
        d_k = jnp.sum(jnp.where(lane == e_k, dest, 0.0), axis=1, keepdims=True)
        out = jnp.where(lane == k, d_k, out)
    dest_ref[...] = out.astype(I32)
    seen_ref[...] += jnp.sum(sel, axis=0, keepdims=True)


def _dest(sel, idx_tab, pstart, tm):
    N = sel.shape[0]
    blk = pl.BlockSpec((tm, LANES), lambda i: (i, 0))
    return pl.pallas_call(
        _dest_kernel,
        grid=(N // tm,),
        in_specs=[blk, blk, pl.BlockSpec((1, LANES), lambda i: (0, 0))],
        out_specs=blk,
        out_shape=jax.ShapeDtypeStruct((N, LANES), I32),
        scratch_shapes=[pltpu.VMEM((1, LANES), F32)],
        compiler_params=_cparams(("arbitrary",)),
        name="dest",
    )(sel, idx_tab, pstart)


def _disp_kernel(zs_ref, h2_ref, dest_hbm, xs_hbm, zbuf, idx_a, idx_b, sem_ia, sem_ib, sem_row, sem_z,
                 *, tm, nch, zrows, n_steps):
    i = pl.program_id(0)
    n = tm * TOP_K

    def idx_copy(tile, buf, sem):
        return pltpu.make_async_copy(dest_hbm.at[pl.ds(tile * n, n)], buf, sem)

    def rows(ref, first, count):
        return ref.at[pl.ds(pl.multiple_of(first * nch, nch), count * nch)]

    def row_copy(row, dst):
        return pltpu.make_async_copy(rows(h2_ref, row, 1), rows(xs_hbm, dst, 1), sem_row)

    def issue_rows(first_row, idx):
        def body(r, c):
            for s in range(8):
                t = r * 8 + s
                for k in range(TOP_K):
                    row_copy(first_row + t, idx[t * TOP_K + k]).start(priority=k % 2)
            return c

        lax.fori_loop(0, tm // 8, body, 0)

    def wait_rows():
        pltpu.make_async_copy(rows(xs_hbm, 0, n), rows(xs_hbm, 0, n), sem_row).wait()

    @pl.when(i == 0)
    def _():
        zbuf[...] = jnp.zeros_like(zbuf)
        for e in range(N_EXPERTS):
            fill = pltpu.make_async_copy(zbuf, rows(xs_hbm, zs_ref[e], zrows), sem_z)
            fill.start()
            fill.wait()
        idx_copy(0, idx_a, sem_ia).start()
        idx_copy(1, idx_b, sem_ib).start()

    idx_copy(2 * i, idx_a, sem_ia).wait()
    issue_rows(0, idx_a)
    idx_copy(2 * i + 1, idx_b, sem_ib).wait()
    issue_rows(tm, idx_b)

    @pl.when(i + 1 < n_steps)
    def _():
        idx_copy(2 * i + 2, idx_a, sem_ia).start()
        idx_copy(2 * i + 3, idx_b, sem_ib).start()

    wait_rows()
    wait_rows()


def _dispatch(h2, dest_flat, zero_start, n_rows, zrows, tm):
    nch = D_MODEL // LANES
    n_steps = h2.shape[0] // nch // (2 * tm)
    grid_spec = pltpu.PrefetchScalarGridSpec(
        num_scalar_prefetch=1,
        grid=(n_steps,),
        in_specs=[pl.BlockSpec((2 * tm * nch, LANES), lambda i, zs: (i, 0)), pl.BlockSpec(memory_space=pl.ANY)],
        out_specs=pl.BlockSpec(memory_space=pl.ANY),
        scratch_shapes=[pltpu.VMEM((zrows * nch, LANES), F32), pltpu.SMEM((tm * TOP_K,), I32),
                        pltpu.SMEM((tm * TOP_K,), I32), pltpu.SemaphoreType.DMA, pltpu.SemaphoreType.DMA,
                        pltpu.SemaphoreType.DMA, pltpu.SemaphoreType.DMA],
    )
    return pl.pallas_call(
        functools.partial(_disp_kernel, tm=tm, nch=nch, zrows=zrows, n_steps=n_steps),
        grid_spec=grid_spec,
        out_shape=jax.ShapeDtypeStruct((n_rows * nch, LANES), F32),
        compiler_params=_cparams(("arbitrary",)),
        name="disp",
    )(zero_start, h2, dest_flat)


def _store_token_major(ref, x):
    rows, d = x.shape
    nch = d // LANES
    for j in range(nch):
        ref[pl.ds(j, rows, stride=nch), :] = x[:, j * LANES:(j + 1) * LANES]


def _load_token_major(ref, rows, nch):
    return jnp.concatenate([ref[pl.ds(j, rows, stride=nch), :] for j in range(nch)], axis=1)


def _ffn_kernel(be_ref, nb_ref, nxt_ref, xs_ref, wgu_hbm, bgu_ref, wd_hbm, bd_ref, ys_ref,
                wgu_st, wd_st, wgu_bf, wd_bf, sem_gu, sem_d):
    i = pl.program_id(0)
    live = i < nb_ref[0]
    e = be_ref[i]

    def fetch(ex):
        return (pltpu.make_async_copy(wgu_hbm.at[ex], wgu_st, sem_gu),
                pltpu.make_async_copy(wd_hbm.at[ex], wd_st, sem_d))

    @pl.when(i == 0)
    def _():
        for cp in fetch(e):
            cp.start()

    @pl.when(live & ((i == 0) | (e != be_ref[jnp.maximum(i - 1, 0)])))
    def _():
        for cp in fetch(e):
            cp.wait()
        wgu_bf[...] = wgu_st[...].astype(BF16)
        wd_bf[...] = wd_st[...].astype(BF16)

        @pl.when(nxt_ref[i] >= 0)
        def _():
            for cp in fetch(nxt_ref[i]):
                cp.start()

    @pl.when(live)
    def _():
        xb = _load_token_major(xs_ref, ys_ref.shape[0] // NCH, NCH).astype(BF16)
        gu = jnp.dot(xb, wgu_bf[...], preferred_element_type=F32) + bgu_ref[...]
        gate = jnp.minimum(gu[:, :D_EXPERT], SWIGLU_LIMIT)
        up = jnp.clip(gu[:, D_EXPERT:], -SWIGLU_LIMIT, SWIGLU_LIMIT)
        glu = gate * (1.0 / (1.0 + jnp.exp(-SWIGLU_ALPHA * gate)))
        act = ((up + 1.0) * glu).astype(BF16)
        ys = jnp.dot(act, wd_bf[...], preferred_element_type=F32) + bd_ref[...]
        _store_token_major(ys_ref, ys)

    @pl.when(jnp.logical_not(live))
    def _():
        ys_ref[...] = jnp.zeros_like(ys_ref)


def _ffn(xs, block_e, n_blocks, next_e, w_gu, b_gu, w_down, b_down, tmb):
    D = D_MODEL
    P = xs.shape[0] // NCH
    E = w_gu.shape[0]
    blk = lambda i, be, nb, nx: (jnp.minimum(i, nb[0] - 1), 0)
    wsel = lambda i, be, nb, nx: (be[jnp.minimum(i, nb[0] - 1)], 0, 0)
    grid_spec = pltpu.PrefetchScalarGridSpec(
        num_scalar_prefetch=3,
        grid=(P // tmb,),
        in_specs=[pl.BlockSpec((tmb * NCH, LANES), blk),
                  pl.BlockSpec(memory_space=pl.ANY),
                  pl.BlockSpec((None, 1, 2 * D_EXPERT), wsel),
                  pl.BlockSpec(memory_space=pl.ANY),
                  pl.BlockSpec((None, 1, D), wsel)],
        out_specs=pl.BlockSpec((tmb * (D // LANES), LANES), lambda i, be, nb, nx: (i, 0)),
        scratch_shapes=[pltpu.VMEM((D, 2 * D_EXPERT), F32), pltpu.VMEM((D_EXPERT, D), F32),
                        pltpu.VMEM((D, 2 * D_EXPERT), BF16), pltpu.VMEM((D_EXPERT, D), BF16),
                        pltpu.SemaphoreType.DMA, pltpu.SemaphoreType.DMA],
    )
    return pl.pallas_call(
        _ffn_kernel,
        grid_spec=grid_spec,
        out_shape=jax.ShapeDtypeStruct((P * (D // LANES), LANES), F32),
        compiler_params=_cparams(("arbitrary",)),
        name="ffn",
    )(block_e, n_blocks, next_e, xs, w_gu, b_gu.reshape(E, 1, 2 * D_EXPERT), w_down, b_down.reshape(E, 1, D))


def _comb_kernel(x1_ref, gate_ref, g2_ref, fg_ref, dest_hbm, ys_hbm, o_ref, buf_a, buf_b, idx_a, idx_b,
                 sem_ia, sem_ib, sem_ra, sem_rb, *, tm, n_steps):
    i = pl.program_id(0)
    n = tm * TOP_K

    def idx_copy(tile, buf, sem):
        return pltpu.make_async_copy(dest_hbm.at[pl.ds(tile * n, n)], buf, sem)

    nch = x1_ref.shape[1] // LANES

    def row_copy(src, buf, k, t, sem):
        return pltpu.make_async_copy(ys_hbm.at[pl.ds(pl.multiple_of(src * nch, nch), nch)],
                                     buf.at[k, t[0], :, t[1]], sem)

    def issue_rows(idx, buf, sem):
        def body(r, c):
            for s in range(8):
                for k in range(TOP_K):
                    row_copy(idx[(r * 8 + s) * TOP_K + k], buf, k, (r, s), sem).start()
            return c

        lax.fori_loop(0, tm // 8, body, 0)

    def wait_rows(buf, sem):
        pltpu.make_async_copy(buf, buf, sem).wait()

    def finish(buf, lo):
        gates = gate_ref[lo:lo + tm, :]
        y = jnp.zeros((tm, x1_ref.shape[1]), F32)
        for k in range(TOP_K):
            rows = jnp.concatenate([buf[k, :, j].reshape(tm, LANES) for j in range(nch)], axis=1)
            y = y + gates[:, k:k + 1] * rows
        v = x1_ref[lo:lo + tm, :] + g2_ref[...] * y
        o_ref[lo:lo + tm, :] = v * lax.rsqrt(jnp.mean(v * v, axis=-1, keepdims=True) + EPS) * fg_ref[...]

    @pl.when(i == 0)
    def _():
        first = idx_copy(0, idx_a, sem_ia)
        first.start()
        first.wait()
        issue_rows(idx_a, buf_a, sem_ra)
        idx_copy(1, idx_b, sem_ib).start()

    idx_copy(2 * i + 1, idx_b, sem_ib).wait()
    issue_rows(idx_b, buf_b, sem_rb)

    @pl.when(i + 1 < n_steps)
    def _():
        idx_copy(2 * i + 2, idx_a, sem_ia).start()

    wait_rows(buf_a, sem_ra)
    finish(buf_a, 0)

    @pl.when(i + 1 < n_steps)
    def _():
        idx_copy(2 * i + 2, idx_a, sem_ia).wait()
        issue_rows(idx_a, buf_a, sem_ra)
        idx_copy(2 * i + 3, idx_b, sem_ib).start()

    wait_rows(buf_b, sem_rb)
    finish(buf_b, tm)


def _combine(x1, gate_tab, g2, final_g, dest_flat, ys, S, tm):
    N, D = x1.shape
    n_steps = N // (2 * tm)
    per_b = S // (2 * tm)
    return pl.pallas_call(
        functools.partial(_comb_kernel, tm=tm, n_steps=n_steps),
        grid=(n_steps,),
        in_specs=[pl.BlockSpec((2 * tm, D), lambda i: (i, 0)),
                  pl.BlockSpec((2 * tm, LANES), lambda i: (i, 0)),
                  pl.BlockSpec((None, 1, D), lambda i: (i // per_b, 0, 0)),
                  pl.BlockSpec((1, D), lambda i: (0, 0)),
                  pl.BlockSpec(memory_space=pl.ANY),
                  pl.BlockSpec(memory_space=pl.ANY)],
        out_specs=pl.BlockSpec((2 * tm, D), lambda i: (i, 0)),
        out_shape=jax.ShapeDtypeStruct((N, D), F32),
        scratch_shapes=[pltpu.VMEM((TOP_K, tm // 8, D // LANES, 8, LANES), F32),
                        pltpu.VMEM((TOP_K, tm // 8, D // LANES, 8, LANES), F32),
                        pltpu.SMEM((tm * TOP_K,), I32), pltpu.SMEM((tm * TOP_K,), I32),
                        pltpu.SemaphoreType.DMA, pltpu.SemaphoreType.DMA, pltpu.SemaphoreType.DMA,
                        pltpu.SemaphoreType.DMA],
        compiler_params=_cparams(("arbitrary",)),
        name="comb",
    )(x1, gate_tab, g2, final_g.reshape(1, D), dest_flat, ys)


def _tile(n, pref):
    t = min(pref, n)
    assert n % t == 0, (n, t)
    return t


def _layer(x, c, ada_w, ada_b, norm1_g, w_in, mix_scale, w_o, norm2_g,
           router_w, router_b, w_gu, b_gu, w_down, b_down, final_g):
    B, S, D = x.shape
    N = B * S
    mod = _mod(c, ada_w, ada_b).reshape(B, 6, 1, D)
    sh1, sc1, g1, sh2, sc2, g2 = (mod[:, j] for j in range(6))

    w_pad = jnp.pad(w_in, ((0, 0), (0, IN_COLS_PAD - IN_COLS))).astype(BF16)
    rq, rk, rv, rg, aqt, ak, avt, iqt, ik, iwt = _inproj(x, norm1_g, sc1, sh1, w_pad, _tile(S, PROJ_ROWS))
    ms = mix_scale.reshape(1, RET_W + DSA_W)
    ret = _retention(rq, rk, rv, rg, ms[:, :RET_W])
    att = _dsa(iqt, iwt, aqt, ik, ak, avt, ms[:, RET_W:], _tile(S, DSA_QUERIES), DSA_KEYS)

    rw_pad = jnp.pad(router_w, ((0, 0), (0, LANES - N_EXPERTS)))
    rw_hi = rw_pad.astype(BF16)
    rw_pad = jnp.stack([rw_hi, (rw_pad - rw_hi.astype(F32)).astype(BF16)])
    rb_pad = jnp.pad(router_b, (0, LANES - N_EXPERTS)).reshape(1, LANES)
    x1, h2, sel, idx_tab, gate_tab, counts = _oproj(ret, att, x, w_o.astype(BF16), g1, norm2_g, sc2, sh2,
                                                    rw_pad, rb_pad, _tile(S, PROJ_ROWS))

    tmb = FFN_ROWS
    n_rows = (N * TOP_K + N_EXPERTS * (tmb - 1)) // tmb * tmb + tmb
    cnt = counts[0, :N_EXPERTS].astype(I32)
    padded = (cnt + tmb - 1) // tmb * tmb
    ends = jnp.cumsum(padded)
    starts = ends - padded
    pstart = jnp.pad(starts.astype(F32), (0, LANES - N_EXPERTS)).reshape(1, LANES)
    n_blocks = (ends[-1] // tmb).reshape(1)
    first_row = jnp.arange(n_rows // tmb, dtype=I32) * tmb
    block_e = jnp.minimum(jnp.sum((ends[None, :] <= first_row[:, None]).astype(I32), axis=1), N_EXPERTS - 1)

    tmd = _tile(N, MOE_TOKENS)
    dest_tab = _dest(sel, idx_tab, pstart, tmd)
    dest_flat = dest_tab[:, :TOP_K].reshape(N * TOP_K)
    xs = _dispatch(h2, dest_flat, starts + cnt, n_rows, tmb, tmd)
    eid = jnp.arange(N_EXPERTS, dtype=I32)
    later_used = (eid[None, :] > eid[:, None]) & (padded[None, :] > 0)
    next_used = jnp.min(jnp.where(later_used, eid[None, :], N_EXPERTS), axis=1)
    next_e = jnp.where(next_used < N_EXPERTS, next_used, -1)[block_e].astype(I32)
    ys = _ffn(xs, block_e, n_blocks, next_e, w_gu, b_gu, w_down, b_down, tmb)
    out = _combine(x1, gate_tab, g2, final_g, dest_flat, ys, S, _tile(S, MOE_TOKENS))
    return out.reshape(B, S, D)


def kernel(x, c, ada_w, ada_b, norm1_g, w_in, mix_scale, w_o, norm2_g, router_w, router_b, w_gu, b_gu,
           w_down, b_down, final_g):
    assert ada_w.shape[0] == 1, "single-layer stack"
    return _layer(x, c, ada_w[0], ada_b[0], norm1_g[0], w_in[0], mix_scale[0], w_o[0], norm2_g[0],
                  router_w[0], router_b[0], w_gu[0], b_gu[0], w_down[0], b_down[0], final_g)
```

```python
import functools

import numpy as np
import jax
import jax.numpy as jnp
from jax import lax
from jax.experimental import pallas as pl
from jax.experimental.pallas import tpu as pltpu

F32 = jnp.float32
BF16 = jnp.bfloat16
I32 = jnp.int32

D_MODEL = 1024
RET_HEADS = 4
RET_DK = 64
RET_DV = 128
RET_CHUNK = 128
DSA_HEADS = 8
DSA_KV_HEADS = 2
DSA_HD = 64
IDX_HEADS = 8
IDX_HD = 64
TOPK_MAX = 256
N_EXPERTS = 32
TOP_K = 4
D_EXPERT = D_MODEL
SWIGLU_LIMIT = 7.0
SWIGLU_ALPHA = 1.702
EPS = 1e-6

RET_W = RET_HEADS * RET_DV
DSA_W = DSA_HEADS * DSA_HD
IN_COLS = 2888
IN_COLS_PAD = 2944

KAUG = 128
VAUG = 80
ALIBI_SPLIT = 64
PROJ_ROWS = 512
DSA_QUERIES = 512
DSA_KEYS = 128
MOE_TOKENS = 256
RET_CHUNKS_PER_STEP = 4
FFN_ROWS = 512
BISECT_VALUE_STEPS = 8
BISECT_MAX_STEPS = 64

LANES = 128
NCH = D_MODEL // LANES
VMEM_LIMIT = 56 * 1024 * 1024
NEG_BIG = -1e30
F32_LOWEST = float(np.finfo(np.float32).min)


def _cparams(sem):
    return pltpu.CompilerParams(dimension_semantics=sem, vmem_limit_bytes=VMEM_LIMIT)


def _mod_kernel(c_ref, w_ref, b_ref, o_ref):
    c = c_ref[...]
    s = c * (1.0 / (1.0 + jnp.exp(-c)))
    o_ref[...] = jnp.dot(s, w_ref[...], preferred_element_type=F32,
                         precision=lax.Precision.HIGHEST) + b_ref[...]


def _mod(c, ada_w, ada_b):
    B, D = c.shape
    n_out = ada_w.shape[1]
    rows = 8
    c8 = jnp.zeros((rows, D), F32).at[:B].set(c)
    out = pl.pallas_call(
        _mod_kernel,
        grid=(n_out // D,),
        in_specs=[pl.BlockSpec((rows, D), lambda j: (0, 0)),
                  pl.BlockSpec((D, D), lambda j: (0, j)),
                  pl.BlockSpec((1, D), lambda j: (0, j))],
        out_specs=pl.BlockSpec((rows, D), lambda j: (0, j)),
        out_shape=jax.ShapeDtypeStruct((rows, n_out), F32),
        compiler_params=_cparams(("arbitrary",)),
        name="mod",
    )(c8, ada_w, ada_b.reshape(1, n_out))
    return out[:B]


def _inproj_kernel(x_ref, g_ref, sc_ref, sh_ref, w_ref,
                   rq_ref, rk_ref, rv_ref, rg_ref, aqt_ref, ak_ref, avt_ref, iqt_ref, ik_ref, iwt_ref):
    x = x_ref[...]
    ms = jnp.mean(x * x, axis=-1, keepdims=True)
    y = x * lax.rsqrt(ms + EPS) * g_ref[...]
    hb = (y * (1.0 + sc_ref[...]) + sh_ref[...]).astype(BF16)

    def proj(lo, hi):
        return jnp.dot(hb, w_ref[:, lo:hi], preferred_element_type=F32)

    tm = x.shape[0]
    d = DSA_HD
    rq_ref[...] = proj(0, 256).astype(BF16)
    rk_ref[...] = (proj(256, 512) * (RET_DK ** -0.5)).astype(BF16)
    rv_ref[...] = proj(512, 1024).astype(BF16)
    rg_ref[...] = proj(1024, 1536).astype(BF16)
    aqt_ref[...] = (proj(1536, 2048) * (d ** -0.5)).T.astype(BF16)
    kk = proj(2048, 2176)
    pos = pl.program_id(1) * tm + lax.broadcasted_iota(I32, (tm, d), 0)
    col = lax.broadcasted_iota(I32, (tm, d), 1)
    posblk = jnp.where(col == 0, pos // ALIBI_SPLIT, jnp.where(col == 1, pos % ALIBI_SPLIT, 0)).astype(F32)
    for g in range(DSA_KV_HEADS):
        ak_ref[:, g * KAUG:g * KAUG + d] = kk[:, g * d:(g + 1) * d].astype(BF16)
        ak_ref[:, g * KAUG + d:(g + 1) * KAUG] = posblk.astype(BF16)
    vt = proj(2176, 2304).T
    r16 = lax.broadcasted_iota(I32, (VAUG - d, tm), 0)
    onesblk = jnp.where(r16 == 0, 1.0, 0.0).astype(BF16)
    for g in range(DSA_KV_HEADS):
        avt_ref[g * VAUG:g * VAUG + d, :] = vt[g * d:(g + 1) * d, :].astype(BF16)
        avt_ref[g * VAUG + d:(g + 1) * VAUG, :] = onesblk
    iqt_ref[...] = proj(2304, 2816).T.astype(BF16)
    last = proj(2816, 2944)
    ik_ref[...] = last[:, :IDX_HD].astype(BF16)
    iwt_ref[...] = last.T[IDX_HD:IDX_HD + IDX_HEADS, :] * ((IDX_HD ** -0.5) * (IDX_HEADS ** -0.5))


def _inproj(x, norm_g, sc, sh, w_pad, tm):
    B, S, D = x.shape
    row = lambda w: pl.BlockSpec((None, tm, w), lambda b, i: (b, i, 0))
    colT = lambda h: pl.BlockSpec((None, h, tm), lambda b, i: (b, 0, i))
    vec = pl.BlockSpec((None, 1, D), lambda b, i: (b, 0, 0))
    sd = lambda shape, dt: jax.ShapeDtypeStruct(shape, dt)
    G = DSA_KV_HEADS
    return pl.pallas_call(
        _inproj_kernel,
        grid=(B, S // tm),
        in_specs=[row(D), pl.BlockSpec((1, D), lambda b, i: (0, 0)), vec, vec,
                  pl.BlockSpec((D, IN_COLS_PAD), lambda b, i: (0, 0))],
        out_specs=[row(256), row(256), row(512), row(512), colT(DSA_W), row(G * KAUG), colT(G * VAUG),
                   colT(IDX_HEADS * IDX_HD), row(IDX_HD), colT(IDX_HEADS)],
        out_shape=[sd((B, S, 256), BF16), sd((B, S, 256), BF16), sd((B, S, 512), BF16),
                   sd((B, S, 512), BF16), sd((B, DSA_W, S), BF16), sd((B, S, G * KAUG), BF16),
                   sd((B, G * VAUG, S), BF16), sd((B, IDX_HEADS * IDX_HD, S), BF16),
                   sd((B, S, IDX_HD), BF16), sd((B, IDX_HEADS, S), F32)],
        compiler_params=_cparams(("parallel", "parallel")),
        name="inproj",
    )(x, norm_g.reshape(1, D), sc, sh, w_pad)


def _ret_kernel(rq_ref, rk_ref, rv_ref, rg_ref, din_ref, qd_ref, kd_ref, cd_ref, ms_ref, o_ref, state_ref):
    @pl.when(pl.program_id(1) == 0)
    def _():
        state_ref[...] = jnp.zeros_like(state_ref)

    C = din_ref.shape[1]
    for c in range(rq_ref.shape[0] // C):
        rows = slice(c * C, (c + 1) * C)
        for h in range(RET_HEADS):
            q = rq_ref[rows, h * RET_DK:(h + 1) * RET_DK]
            k = rk_ref[rows, h * RET_DK:(h + 1) * RET_DK]
            v = rv_ref[rows, h * RET_DV:(h + 1) * RET_DV]
            r_prev = state_ref[h]
            s = lax.dot_general(q, k, (((1,), (1,)), ((), ())), preferred_element_type=F32) * din_ref[h]
            o = jnp.dot(s.astype(BF16), v, preferred_element_type=F32)
            o = o + jnp.dot(q, r_prev.astype(BF16), preferred_element_type=F32) * qd_ref[h]
            vd = (v.astype(F32) * kd_ref[h]).astype(BF16)
            kv = lax.dot_general(k, vd, (((0,), (0,)), ((), ())), preferred_element_type=F32)
            state_ref[h] = r_prev * cd_ref[h] + kv
            o = o * lax.rsqrt(jnp.mean(o * o, axis=-1, keepdims=True) + EPS)
            g = rg_ref[rows, h * RET_DV:(h + 1) * RET_DV].astype(F32)
            gate = g * (1.0 / (1.0 + jnp.exp(-g)))
            o_ref[rows, h * RET_DV:(h + 1) * RET_DV] = (
                gate * o * ms_ref[:, h * RET_DV:(h + 1) * RET_DV]).astype(BF16)


def _ret_consts(C):
    H = RET_HEADS
    log_g = np.log1p(-np.exp2(-5.0 - np.arange(H, dtype=np.float64)))
    pos = np.arange(C, dtype=np.float64)
    diff = pos[:, None] - pos[None, :]
    d_inner = np.where(diff[None] >= 0, np.exp(np.maximum(diff, 0.0)[None] * log_g[:, None, None]), 0.0)
    q_decay = np.exp((pos + 1.0)[None] * log_g[:, None])
    k_decay = np.exp((C - 1.0 - pos)[None] * log_g[:, None])
    chunk_decay = np.exp(C * log_g)
    qd = np.broadcast_to(q_decay[:, :, None], (H, C, RET_DV))
    kd = np.broadcast_to(k_decay[:, :, None], (H, C, RET_DV))
    cd = np.broadcast_to(chunk_decay[:, None, None], (H, 1, RET_DV))
    f = lambda a: jnp.asarray(np.ascontiguousarray(a), F32)
    return f(d_inner), f(qd), f(kd), f(cd)


def _retention(rq, rk, rv, rg, ms_ret):
    B, S, _ = rq.shape
    C = min(RET_CHUNK, S)
    din, qd, kd, cd = _ret_consts(C)
    rows = _tile(S, RET_CHUNKS_PER_STEP * C)
    row = lambda w: pl.BlockSpec((None, rows, w), lambda b, n: (b, n, 0))
    full = lambda a: pl.BlockSpec(a.shape, lambda b, n: (0,) * a.ndim)
    return pl.pallas_call(
        _ret_kernel,
        grid=(B, S // rows),
        in_specs=[row(256), row(256), row(512), row(512), full(din), full(qd), full(kd), full(cd),
                  pl.BlockSpec((1, RET_W), lambda b, n: (0, 0))],
        out_specs=row(RET_W),
        out_shape=jax.ShapeDtypeStruct((B, S, RET_W), BF16),
        scratch_shapes=[pltpu.VMEM((RET_HEADS, RET_DK, RET_DV), F32)],
        compiler_params=_cparams(("parallel", "arbitrary")),
        name="ret",
    )(rq, rk, rv, rg, din, qd, kd, cd, ms_ret)


def _f32_key(x):
    i = lax.bitcast_convert_type(x, I32)
    return i ^ ((i >> 31) & 0x7FFFFFFF)


def _key_f32(k):
    return lax.bitcast_convert_type(k ^ ((k >> 31) & 0x7FFFFFFF), F32)


def _dsa_kernel(iqt_ref, iwt_ref, aqt_ref, ik_ref, ak_ref, avt_ref, ms_ref, o_ref, score_ref, qa_ref, sa_ref, sb_ref, mask_ref, *acc_refs,
                tq, tks, n_sel):
    H, G, R, d = DSA_HEADS, DSA_KV_HEADS, DSA_HEADS // DSA_KV_HEADS, DSA_HD
    t0 = pl.program_id(1) * tq
    nsub = (t0 + tq) // tks
    tka = 2 * tks
    npair = ((t0 + tq) // tka + 1) // 2
    kf = float(n_sel)
    qpos = t0 + lax.broadcasted_iota(I32, (1, tq), 1)
    krow = lax.broadcasted_iota(I32, (tks, tq), 0)

    wrow = [iwt_ref[h:h + 1, :] for h in range(IDX_HEADS)]

    def fold8(x, op):
        acc = x[0:8, :]
        for i in range(1, tks // 8):
            acc = op(acc, x[8 * i:8 * (i + 1), :])
        return acc

    def score_pair(i, carry, masked):
        mx, mn, npos, nnon = carry
        for u in range(2 * tka // tks):
            r0 = pl.multiple_of(i * 2 * tka + u * tks, tks)
            kc = ik_ref[pl.ds(r0, tks), :]
            acc = jnp.zeros((tks, tq), F32)
            for h in range(IDX_HEADS):
                rel = jnp.dot(kc, iqt_ref[h * IDX_HD:(h + 1) * IDX_HD, :], preferred_element_type=F32)
                acc = acc + jnp.maximum(rel, 0.0) * wrow[h]
            if masked:
                causal = r0 + krow <= qpos
                sc = jnp.where(causal, acc, -jnp.inf)
                lowest = jnp.where(causal, acc, jnp.inf)
            else:
                sc = lowest = acc
            score_ref[pl.ds(r0, tks), :] = sc
            mx = jnp.maximum(mx, fold8(sc, jnp.maximum))
            mn = jnp.minimum(mn, fold8(lowest, jnp.minimum))
            npos = npos + fold8(jnp.where(sc > 0.0, 1.0, 0.0), jnp.add)
            nnon = nnon + fold8(jnp.where(sc >= 0.0, 1.0, 0.0), jnp.add)
        return mx, mn, npos, nnon

    stat0 = (jnp.full((8, tq), -jnp.inf, F32), jnp.full((8, tq), jnp.inf, F32),
             jnp.zeros((8, tq), F32), jnp.zeros((8, tq), F32))
    n_inner = (t0 + 1) // (2 * tka)
    stat = lax.fori_loop(0, n_inner, functools.partial(score_pair, masked=False), stat0)
    mx, mn, npos, nnon = lax.fori_loop(n_inner, npair, functools.partial(score_pair, masked=True), stat)
    top = jnp.max(mx, axis=0, keepdims=True)
    lo0 = jnp.min(mn, axis=0, keepdims=True)
    n_pos = jnp.sum(npos, axis=0, keepdims=True)
    n_nonneg = jnp.sum(nnon, axis=0, keepdims=True)


    def count(th, strict):
        def body(j, acc):
            for u in range(2):
                s = score_ref[pl.ds(pl.multiple_of((2 * j + u) * tks, tks), tks), :]
                hit = (s > th) if strict else (s >= th)
                acc = acc + fold8(jnp.where(hit, 1.0, 0.0), jnp.add)
            return acc

        acc = lax.fori_loop(0, nsub // 2, body, jnp.zeros((8, tq), F32))
        return jnp.sum(acc, axis=0, keepdims=True)

    def probe(lo, hi, it):
        lk, hk = _f32_key(lo), _f32_key(hi)
        mk = (lk >> 1) + (hk >> 1) + (lk & hk & 1)
        mv = lo + (hi - lo) * 0.5
        early = (jnp.zeros((1, tq), I32) + it) < BISECT_VALUE_STEPS
        mid = jnp.where(early & (mv > lo) & (mv < hi), mv, _key_f32(mk))
        return mid, jnp.max(jnp.where(mk != lk, 1.0, 0.0))

    def bis_cond(c):
        return (c[5] > 0.0) & (c[6] < BISECT_MAX_STEPS)

    def bis_body(c):
        lo, hi, c_lo, c_hi, mid, _, it = c
        cnt = count(mid, False)
        ge = cnt >= kf
        up = ge | (cnt == kf)
        dn = (~ge) | (cnt == kf)
        lo, c_lo = jnp.where(up, mid, lo), jnp.where(up, cnt, c_lo)
        hi, c_hi = jnp.where(dn, mid, hi), jnp.where(dn, cnt, c_hi)
        mid, active = probe(lo, hi, it + 1)
        return lo, hi, c_lo, c_hi, mid, active, it + 1

    zero = jnp.zeros((1, tq), F32)
    keep_all = qpos + 1 <= n_sel
    settled = keep_all | ((n_nonneg >= kf) & (n_pos < kf))
    above = n_pos >= kf
    c_lo0 = jnp.where(settled | above, n_nonneg, (qpos + 1).astype(F32))
    c_hi0 = jnp.where(settled | ~above, n_nonneg, zero)
    lo0 = jnp.where(settled | above, zero, lo0)
    hi0 = jnp.where(settled | ~above, zero, _key_f32(_f32_key(top) + 1))
    mid0, active0 = probe(lo0, hi0, jnp.int32(0))
    lo, hi, c_lo, c_hi, _, _, _ = lax.while_loop(bis_cond, bis_body,
                                                 (lo0, hi0, c_lo0, c_hi0, mid0, active0, jnp.int32(0)))
    at_hi = c_hi >= kf
    thr = jnp.where(keep_all, F32_LOWEST, jnp.where(at_hi, hi, lo))
    excess = jnp.where(keep_all, 0.0, jnp.where(at_hi, c_hi, c_lo) - kf)

    @pl.when(jnp.max(excess) > 0.0)
    def _():
        tied_nonzero = jnp.max(jnp.where((excess > 0.0) & (thr != 0.0), 1.0, 0.0)) > 0.0
        n_above = lax.cond(tied_nonzero, lambda: count(thr, True), lambda: n_pos)
        budget = jnp.where(excess > 0.0, kf - jnp.where(thr == 0.0, n_pos, n_above), jnp.inf)
        earlier = lax.broadcasted_iota(I32, (tks, tks), 1) < lax.broadcasted_iota(I32, (tks, tks), 0)
        earlier = jnp.where(earlier, 1.0, 0.0).astype(BF16)

        def fix(j, seen):
            tiles = []
            for u in range(2):
                r0 = pl.multiple_of((2 * j + u) * tks, tks)
                s = score_ref[pl.ds(r0, tks), :]
                eq = s == thr
                eqf = jnp.where(eq, 1.0, 0.0)
                within = jnp.dot(earlier, eqf.astype(BF16), preferred_element_type=F32)
                tiles.append((r0, s, eq, within, jnp.sum(eqf, axis=0, keepdims=True)))
            for r0, s, eq, within, n_eq in tiles:
                score_ref[pl.ds(r0, tks), :] = jnp.where(eq & (within + seen >= budget), -jnp.inf, s)
                seen = seen + n_eq
            return seen

        lax.fori_loop(0, nsub // 2, fix, jnp.zeros((1, tq), F32))

    arow = lax.broadcasted_iota(I32, (KAUG - d, tq), 0)
    for h in range(H):
        slope = float(2.0 ** (-8.0 * (h + 1) / H))
        qa_ref[h, 0:d, :] = aqt_ref[h * d:(h + 1) * d, :]
        qa_ref[h, d:KAUG, :] = jnp.where(arow == 0, slope * ALIBI_SPLIT, jnp.where(arow == 1, slope, 0.0)).astype(BF16)
    for acc in acc_refs:
        acc[...] = jnp.zeros_like(acc)

    def logits(j, h):
        ka = ak_ref[pl.ds(pl.multiple_of(j * tka, tka), tka), (h // R) * KAUG:(h // R + 1) * KAUG]
        return jnp.dot(ka, qa_ref[h], preferred_element_type=F32)

    def step(j, j_next, cur_ref, next_ref, ms):
        r0 = pl.multiple_of(j * tka, tka)
        mask_ref[...] = jnp.where(score_ref[pl.ds(r0, tka), :] >= thr, 0.0, NEG_BIG)
        nms = []
        ahead = logits(j_next, 0)
        for h in range(H):
            g = h // R
            next_ref[h] = ahead
            if h + 1 < H:
                ahead = logits(j_next, h + 1)
            s = cur_ref[h] + mask_ref[...]
            m_new = jnp.maximum(ms[h], jnp.max(s, axis=0, keepdims=True))
            p = jnp.exp(s - m_new).astype(BF16)
            va = avt_ref[g * VAUG:(g + 1) * VAUG, pl.ds(r0, tka)]
            acc = acc_refs[h]
            acc[...] = acc[...] * jnp.exp(ms[h] - m_new) + jnp.dot(va, p, preferred_element_type=F32)
            nms.append(m_new)
        return tuple(nms)

    for h in range(H):
        sa_ref[h] = logits(0, h)

    def att_pair(i, ms):
        ms = step(2 * i, 2 * i + 1, sa_ref, sb_ref, ms)
        return step(2 * i + 1, jnp.minimum(2 * i + 2, 2 * npair - 1), sb_ref, sa_ref, ms)

    lax.fori_loop(0, npair, att_pair, tuple(jnp.full((1, tq), NEG_BIG, F32) for _ in range(H)))
    for h in range(H):
        a = acc_refs[h][...]
        o = a[0:d, :] / a[d:d + 1, :]
        o_ref[h * d:(h + 1) * d, :] = (o * ms_ref[h * d:(h + 1) * d, :]).astype(BF16)


def _dsa(iqt, iwt, aqt, ik, ak, avt, ms_att, tq, tks):
    B, _, S = iqt.shape
    assert S <= ALIBI_SPLIT * 256 and tq % (2 * tks) == 0
    assert (S // (2 * tks)) % 2 == 0
    n_sel = min(TOPK_MAX, S // 4)
    G = DSA_KV_HEADS
    colT = lambda h: pl.BlockSpec((None, h, tq), lambda b, i: (b, 0, i))
    msb = jnp.broadcast_to(ms_att.reshape(DSA_W, 1), (DSA_W, tq))
    return pl.pallas_call(
        functools.partial(_dsa_kernel, tq=tq, tks=tks, n_sel=n_sel),
        grid=(B, S // tq),
        in_specs=[colT(IDX_HEADS * IDX_HD), colT(IDX_HEADS), colT(DSA_W),
                  pl.BlockSpec((None, S, IDX_HD), lambda b, i: (b, 0, 0)),
                  pl.BlockSpec((None, S, G * KAUG), lambda b, i: (b, 0, 0)),
                  pl.BlockSpec((None, G * VAUG, S), lambda b, i: (b, 0, 0)),
                  pl.BlockSpec((DSA_W, tq), lambda b, i: (0, 0))],
        out_specs=colT(DSA_W),
        out_shape=jax.ShapeDtypeStruct((B, DSA_W, S), BF16),
        scratch_shapes=[pltpu.VMEM((S, tq), F32), pltpu.VMEM((DSA_HEADS, KAUG, tq), BF16),
                        pltpu.VMEM((DSA_HEADS, 2 * tks, tq), F32), pltpu.VMEM((DSA_HEADS, 2 * tks, tq), F32),
                        pltpu.VMEM((2 * tks, tq), F32)]
        + [pltpu.VMEM((VAUG, tq), F32) for _ in range(DSA_HEADS)],
        compiler_params=_cparams(("parallel", "arbitrary")),
        name="dsa",
    )(iqt, iwt, aqt, ik, ak, avt, msb)


def _oproj_kernel(ret_ref, att_ref, x_ref, wo_ref, g1_ref, n2_ref, sc_ref, sh_ref, rw_ref, rb_ref,
                  x1_ref, h2_ref, sel_ref, idx_ref, gate_ref, cnt_ref):
    mixo = jnp.dot(ret_ref[...], wo_ref[:RET_W, :], preferred_element_type=F32)
    mixo = mixo + lax.dot_general(att_ref[...], wo_ref[RET_W:, :], (((0,), (0,)), ((), ())),
                                  preferred_element_type=F32)
    x1 = x_ref[...] + g1_ref[...] * mixo
    x1_ref[...] = x1
    y = x1 * lax.rsqrt(jnp.mean(x1 * x1, axis=-1, keepdims=True) + EPS) * n2_ref[...]
    h2 = y * (1.0 + sc_ref[...]) + sh_ref[...]
    _store_token_major(h2_ref, h2)
    h_hi = h2.astype(BF16)
    h_lo = (h2 - h_hi.astype(F32)).astype(BF16)
    logits = jnp.dot(h_hi, rw_ref[0], preferred_element_type=F32)
    logits = logits + (jnp.dot(h_hi, rw_ref[1], preferred_element_type=F32)
                       + jnp.dot(h_lo, rw_ref[0], preferred_element_type=F32)) + rb_ref[...]
    tm = logits.shape[0]
    lane = lax.broadcasted_iota(I32, (tm, LANES), 1).astype(F32)
    work = jnp.where(lane < N_EXPERTS, logits, -jnp.inf)
    sel = jnp.zeros((tm, LANES), F32)
    idx_tab = jnp.zeros((tm, LANES), F32)
    vals = []
    for k in range(TOP_K):
        m = jnp.max(work, axis=1, keepdims=True)
        idx = jnp.min(jnp.where(work == m, lane, float(LANES)), axis=1, keepdims=True)
        hit = lane == idx
        sel = jnp.where(hit, 1.0, sel)
        idx_tab = jnp.where(lane == k, idx, idx_tab)
        work = jnp.where(hit, -jnp.inf, work)
        vals.append(m)
    es = [jnp.exp(v - vals[0]) for v in vals]
    den = es[0] + es[1] + es[2] + es[3]
    gate_tab = jnp.zeros((tm, LANES), F32)
    for k in range(TOP_K):
        gate_tab = jnp.where(lane == k, es[k] / den, gate_tab)
    sel_ref[...] = sel
    idx_ref[...] = idx_tab
    gate_ref[...] = gate_tab

    @pl.when((pl.program_id(0) == 0) & (pl.program_id(1) == 0))
    def _():
        cnt_ref[...] = jnp.zeros_like(cnt_ref)

    cnt_ref[...] += jnp.sum(sel, axis=0, keepdims=True)


def _oproj(ret, att, x, wo, g1, n2g, sc2, sh2, rw_pad, rb_pad, tm):
    B, S, D = x.shape
    nt = S // tm
    row = lambda w: pl.BlockSpec((None, tm, w), lambda b, i: (b, i, 0))
    flat = lambda w: pl.BlockSpec((tm, w), lambda b, i: (b * nt + i, 0))
    vec = pl.BlockSpec((None, 1, D), lambda b, i: (b, 0, 0))
    cst = lambda shape: pl.BlockSpec(shape, lambda b, i: (0, 0))
    sd = lambda shape, dt: jax.ShapeDtypeStruct(shape, dt)
    N = B * S
    return pl.pallas_call(
        _oproj_kernel,
        grid=(B, nt),
        in_specs=[row(RET_W), pl.BlockSpec((None, DSA_W, tm), lambda b, i: (b, 0, i)), row(D), cst((D, D)), vec,
                  cst((1, D)), vec, vec,
                  pl.BlockSpec((2, D, LANES), lambda b, i: (0, 0, 0)), cst((1, LANES))],
        out_specs=[flat(D), pl.BlockSpec((tm * (D // LANES), LANES), lambda b, i: (b * nt + i, 0)),
                   flat(LANES), flat(LANES), flat(LANES), cst((1, LANES))],
        out_shape=[sd((N, D), F32), sd((N * (D // LANES), LANES), F32), sd((N, LANES), F32), sd((N, LANES), F32),
                   sd((N, LANES), F32), sd((1, LANES), F32)],
        compiler_params=_cparams(("arbitrary", "arbitrary")),
        name="oproj",
    )(ret, att, x, wo, g1, n2g.reshape(1, D), sc2, sh2, rw_pad, rb_pad)


def _dest_kernel(sel_ref, idx_ref, pstart_ref, dest_ref, seen_ref):
    @pl.when(pl.program_id(0) == 0)
    def _():
        seen_ref[...] = jnp.zeros_like(seen_ref)

    sel = sel_ref[...]
    tm = sel.shape[0]
    earlier = lax.broadcasted_iota(I32, (tm, tm), 1) < lax.broadcasted_iota(I32, (tm, tm), 0)
    earlier = jnp.where(earlier, 1.0, 0.0).astype(BF16)
    rank = jnp.dot(earlier, sel.astype(BF16), preferred_element_type=F32) + seen_ref[...]
    dest = pstart_ref[...] + rank
    lane = lax.broadcasted_iota(I32, (tm, LANES), 1).astype(F32)
    idx_tab = idx_ref[...]
    out = jnp.zeros((tm, LANES), F32)
    for k in range(TOP_K):
        e_k = jnp.sum(jnp.where(lane == k, idx_tab, 0.0), axis=1, keepdims=True)
        d_k = jnp.sum(jnp.where(lane == e_k, dest, 0.0), axis=1, keepdims=True)
        out = jnp.where(lane == k, d_k, out)
    dest_ref[...] = out.astype(I32)
    seen_ref[...] += jnp.sum(sel, axis=0, keepdims=True)


def _dest(sel, idx_tab, pstart, tm):
    N = sel.shape[0]
    blk = pl.BlockSpec((tm, LANES), lambda i: (i, 0))
    return pl.pallas_call(
        _dest_kernel,
        grid=(N // tm,),
        in_specs=[blk, blk, pl.BlockSpec((1, LANES), lambda i: (0, 0))],
        out_specs=blk,
        out_shape=jax.ShapeDtypeStruct((N, LANES), I32),
        scratch_shapes=[pltpu.VMEM((1, LANES), F32)],
        compiler_params=_cparams(("arbitrary",)),
        name="dest",
    )(sel, idx_tab, pstart)


def _disp_kernel(zs_ref, h2_ref, dest_hbm, xs_hbm, zbuf, idx_a, idx_b, sem_ia, sem_ib, sem_row, sem_z,
                 *, tm, nch, zrows, n_steps):
    i = pl.program_id(0)
    n = tm * TOP_K

    def idx_copy(tile, buf, sem):
        return pltpu.make_async_copy(dest_hbm.at[pl.ds(tile * n, n)], buf, sem)

    def rows(ref, first, count):
        return ref.at[pl.ds(pl.multiple_of(first * nch, nch), count * nch)]

    def row_copy(row, dst):
        return pltpu.make_async_copy(rows(h2_ref, row, 1), rows(xs_hbm, dst, 1), sem_row)

    def issue_rows(first_row, idx):
        def body(r, c):
            for s in range(8):
                t = r * 8 + s
                for k in range(TOP_K):
                    row_copy(first_row + t, idx[t * TOP_K + k]).start(priority=k % 2)
            return c

        lax.fori_loop(0, tm // 8, body, 0)

    def wait_rows():
        pltpu.make_async_copy(rows(xs_hbm, 0, n), rows(xs_hbm, 0, n), sem_row).wait()

    @pl.when(i == 0)
    def _():
        zbuf[...] = jnp.zeros_like(zbuf)
        for e in range(N_EXPERTS):
            fill = pltpu.make_async_copy(zbuf, rows(xs_hbm, zs_ref[e], zrows), sem_z)
            fill.start()
            fill.wait()
        idx_copy(0, idx_a, sem_ia).start()
        idx_copy(1, idx_b, sem_ib).start()

    idx_copy(2 * i, idx_a, sem_ia).wait()
    issue_rows(0, idx_a)
    idx_copy(2 * i + 1, idx_b, sem_ib).wait()
    issue_rows(tm, idx_b)

    @pl.when(i + 1 < n_steps)
    def _():
        idx_copy(2 * i + 2, idx_a, sem_ia).start()
        idx_copy(2 * i + 3, idx_b, sem_ib).start()

    wait_rows()
    wait_rows()


def _dispatch(h2, dest_flat, zero_start, n_rows, zrows, tm):
    nch = D_MODEL // LANES
    n_steps = h2.shape[0] // nch // (2 * tm)
    grid_spec = pltpu.PrefetchScalarGridSpec(
        num_scalar_prefetch=1,
        grid=(n_steps,),
        in_specs=[pl.BlockSpec((2 * tm * nch, LANES), lambda i, zs: (i, 0)), pl.BlockSpec(memory_space=pl.ANY)],
        out_specs=pl.BlockSpec(memory_space=pl.ANY),
        scratch_shapes=[pltpu.VMEM((zrows * nch, LANES), F32), pltpu.SMEM((tm * TOP_K,), I32),
                        pltpu.SMEM((tm * TOP_K,), I32), pltpu.SemaphoreType.DMA, pltpu.SemaphoreType.DMA,
                        pltpu.SemaphoreType.DMA, pltpu.SemaphoreType.DMA],
    )
    return pl.pallas_call(
        functools.partial(_disp_kernel, tm=tm, nch=nch, zrows=zrows, n_steps=n_steps),
        grid_spec=grid_spec,
        out_shape=jax.ShapeDtypeStruct((n_rows * nch, LANES), F32),
        compiler_params=_cparams(("arbitrary",)),
        name="disp",
    )(zero_start, h2, dest_flat)


def _store_token_major(ref, x):
    rows, d = x.shape
    nch = d // LANES
    for j in range(nch):
        ref[pl.ds(j, rows, stride=nch), :] = x[:, j * LANES:(j + 1) * LANES]


def _load_token_major(ref, rows, nch):
    return jnp.concatenate([ref[pl.ds(j, rows, stride=nch), :] for j in range(nch)], axis=1)


def _ffn_kernel(be_ref, nb_ref, nxt_ref, xs_ref, wgu_hbm, bgu_ref, wd_hbm, bd_ref, ys_ref,
                wgu_st, wd_st, wgu_bf, wd_bf, sem_gu, sem_d):
    i = pl.program_id(0)
    live = i < nb_ref[0]
    e = be_ref[i]

    def fetch(ex):
        return (pltpu.make_async_copy(wgu_hbm.at[ex], wgu_st, sem_gu),
                pltpu.make_async_copy(wd_hbm.at[ex], wd_st, sem_d))

    @pl.when(i == 0)
    def _():
        for cp in fetch(e):
            cp.start()

    @pl.when(live & ((i == 0) | (e != be_ref[jnp.maximum(i - 1, 0)])))
    def _():
        for cp in fetch(e):
            cp.wait()
        wgu_bf[...] = wgu_st[...].astype(BF16)
        wd_bf[...] = wd_st[...].astype(BF16)

        @pl.when(nxt_ref[i] >= 0)
        def _():
            for cp in fetch(nxt_ref[i]):
                cp.start()

    @pl.when(live)
    def _():
        xb = _load_token_major(xs_ref, ys_ref.shape[0] // NCH, NCH).astype(BF16)
        gu = jnp.dot(xb, wgu_bf[...], preferred_element_type=F32) + bgu_ref[...]
        gate = jnp.minimum(gu[:, :D_EXPERT], SWIGLU_LIMIT)
        up = jnp.clip(gu[:, D_EXPERT:], -SWIGLU_LIMIT, SWIGLU_LIMIT)
        glu = gate * (1.0 / (1.0 + jnp.exp(-SWIGLU_ALPHA * gate)))
        act = ((up + 1.0) * glu).astype(BF16)
        ys = jnp.dot(act, wd_bf[...], preferred_element_type=F32) + bd_ref[...]
        _store_token_major(ys_ref, ys)

    @pl.when(jnp.logical_not(live))
    def _():
        ys_ref[...] = jnp.zeros_like(ys_ref)


def _ffn(xs, block_e, n_blocks, next_e, w_gu, b_gu, w_down, b_down, tmb):
    D = D_MODEL
    P = xs.shape[0] // NCH
    E = w_gu.shape[0]
    blk = lambda i, be, nb, nx: (jnp.minimum(i, nb[0] - 1), 0)
    wsel = lambda i, be, nb, nx: (be[jnp.minimum(i, nb[0] - 1)], 0, 0)
    grid_spec = pltpu.PrefetchScalarGridSpec(
        num_scalar_prefetch=3,
        grid=(P // tmb,),
        in_specs=[pl.BlockSpec((tmb * NCH, LANES), blk),
                  pl.BlockSpec(memory_space=pl.ANY),
                  pl.BlockSpec((None, 1, 2 * D_EXPERT), wsel),
                  pl.BlockSpec(memory_space=pl.ANY),
                  pl.BlockSpec((None, 1, D), wsel)],
        out_specs=pl.BlockSpec((tmb * (D // LANES), LANES), lambda i, be, nb, nx: (i, 0)),
        scratch_shapes=[pltpu.VMEM((D, 2 * D_EXPERT), F32), pltpu.VMEM((D_EXPERT, D), F32),
                        pltpu.VMEM((D, 2 * D_EXPERT), BF16), pltpu.VMEM((D_EXPERT, D), BF16),
                        pltpu.SemaphoreType.DMA, pltpu.SemaphoreType.DMA],
    )
    return pl.pallas_call(
        _ffn_kernel,
        grid_spec=grid_spec,
        out_shape=jax.ShapeDtypeStruct((P * (D // LANES), LANES), F32),
        compiler_params=_cparams(("arbitrary",)),
        name="ffn",
    )(block_e, n_blocks, next_e, xs, w_gu, b_gu.reshape(E, 1, 2 * D_EXPERT), w_down, b_down.reshape(E, 1, D))


def _comb_kernel(x1_ref, gate_ref, g2_ref, fg_ref, dest_hbm, ys_hbm, o_ref, buf_a, buf_b, idx_a, idx_b,
                 sem_ia, sem_ib, sem_ra, sem_rb, *, tm, n_steps):
    i = pl.program_id(0)
    n = tm * TOP_K

    def idx_copy(tile, buf, sem):
        return pltpu.make_async_copy(dest_hbm.at[pl.ds(tile * n, n)], buf, sem)

    nch = x1_ref.shape[1] // LANES

    def row_copy(src, buf, k, t, sem):
        return pltpu.make_async_copy(ys_hbm.at[pl.ds(pl.multiple_of(src * nch, nch), nch)],
                                     buf.at[k, t[0], :, t[1]], sem)

    def issue_rows(idx, buf, sem):
        def body(r, c):
            for s in range(8):
                for k in range(TOP_K):
                    row_copy(idx[(r * 8 + s) * TOP_K + k], buf, k, (r, s), sem).start(priority=k % 2)
            return c

        lax.fori_loop(0, tm // 8, body, 0)

    def wait_rows(buf, sem):
        pltpu.make_async_copy(buf, buf, sem).wait()

    def finish(buf, lo):
        gates = gate_ref[lo:lo + tm, :]
        y = jnp.zeros((tm, x1_ref.shape[1]), F32)
        for k in range(TOP_K):
            rows = jnp.concatenate([buf[k, :, j].reshape(tm, LANES) for j in range(nch)], axis=1)
            y = y + gates[:, k:k + 1] * rows
        v = x1_ref[lo:lo + tm, :] + g2_ref[...] * y
        o_ref[lo:lo + tm, :] = v * lax.rsqrt(jnp.mean(v * v, axis=-1, keepdims=True) + EPS) * fg_ref[...]

    @pl.when(i == 0)
    def _():
        first = idx_copy(0, idx_a, sem_ia)
        first.start()
        first.wait()
        issue_rows(idx_a, buf_a, sem_ra)
        idx_copy(1, idx_b, sem_ib).start()

    idx_copy(2 * i + 1, idx_b, sem_ib).wait()
    issue_rows(idx_b, buf_b, sem_rb)

    @pl.when(i + 1 < n_steps)
    def _():
        idx_copy(2 * i + 2, idx_a, sem_ia).start()

    wait_rows(buf_a, sem_ra)
    finish(buf_a, 0)

    @pl.when(i + 1 < n_steps)
    def _():
        idx_copy(2 * i + 2, idx_a, sem_ia).wait()
        issue_rows(idx_a, buf_a, sem_ra)
        idx_copy(2 * i + 3, idx_b, sem_ib).start()

    wait_rows(buf_b, sem_rb)
    finish(buf_b, tm)


def _combine(x1, gate_tab, g2, final_g, dest_flat, ys, S, tm):
    N, D = x1.shape
    n_steps = N // (2 * tm)
    per_b = S // (2 * tm)
    return pl.pallas_call(
        functools.partial(_comb_kernel, tm=tm, n_steps=n_steps),
        grid=(n_steps,),
        in_specs=[pl.BlockSpec((2 * tm, D), lambda i: (i, 0)),
                  pl.BlockSpec((2 * tm, LANES), lambda i: (i, 0)),
                  pl.BlockSpec((None, 1, D), lambda i: (i // per_b, 0, 0)),
                  pl.BlockSpec((1, D), lambda i: (0, 0)),
                  pl.BlockSpec(memory_space=pl.ANY),
                  pl.BlockSpec(memory_space=pl.ANY)],
        out_specs=pl.BlockSpec((2 * tm, D), lambda i: (i, 0)),
        out_shape=jax.ShapeDtypeStruct((N, D), F32),
        scratch_shapes=[pltpu.VMEM((TOP_K, tm // 8, D // LANES, 8, LANES), F32),
                        pltpu.VMEM((TOP_K, tm // 8, D // LANES, 8, LANES), F32),
                        pltpu.SMEM((tm * TOP_K,), I32), pltpu.SMEM((tm * TOP_K,), I32),
                        pltpu.SemaphoreType.DMA, pltpu.SemaphoreType.DMA, pltpu.SemaphoreType.DMA,
                        pltpu.SemaphoreType.DMA],
        compiler_params=_cparams(("arbitrary",)),
        name="comb",
    )(x1, gate_tab, g2, final_g.reshape(1, D), dest_flat, ys)


def _tile(n, pref):
    t = min(pref, n)
    assert n % t == 0, (n, t)
    return t


def _layer(x, c, ada_w, ada_b, norm1_g, w_in, mix_scale, w_o, norm2_g,
           router_w, router_b, w_gu, b_gu, w_down, b_down, final_g):
    B, S, D = x.shape
    N = B * S
    mod = _mod(c, ada_w, ada_b).reshape(B, 6, 1, D)
    sh1, sc1, g1, sh2, sc2, g2 = (mod[:, j] for j in range(6))

    w_pad = jnp.pad(w_in, ((0, 0), (0, IN_COLS_PAD - IN_COLS))).astype(BF16)
    rq, rk, rv, rg, aqt, ak, avt, iqt, ik, iwt = _inproj(x, norm1_g, sc1, sh1, w_pad, _tile(S, PROJ_ROWS))
    ms = mix_scale.reshape(1, RET_W + DSA_W)
    ret = _retention(rq, rk, rv, rg, ms[:, :RET_W])
    att = _dsa(iqt, iwt, aqt, ik, ak, avt, ms[:, RET_W:], _tile(S, DSA_QUERIES), DSA_KEYS)

    rw_pad = jnp.pad(router_w, ((0, 0), (0, LANES - N_EXPERTS)))
    rw_hi = rw_pad.astype(BF16)
    rw_pad = jnp.stack([rw_hi, (rw_pad - rw_hi.astype(F32)).astype(BF16)])
    rb_pad = jnp.pad(router_b, (0, LANES - N_EXPERTS)).reshape(1, LANES)
    x1, h2, sel, idx_tab, gate_tab, counts = _oproj(ret, att, x, w_o.astype(BF16), g1, norm2_g, sc2, sh2,
                                                    rw_pad, rb_pad, _tile(S, PROJ_ROWS))

    tmb = FFN_ROWS
    n_rows = (N * TOP_K + N_EXPERTS * (tmb - 1)) // tmb * tmb + tmb
    cnt = counts[0, :N_EXPERTS].astype(I32)
    padded = (cnt + tmb - 1) // tmb * tmb
    ends = jnp.cumsum(padded)
    starts = ends - padded
    pstart = jnp.pad(starts.astype(F32), (0, LANES - N_EXPERTS)).reshape(1, LANES)
    n_blocks = (ends[-1] // tmb).reshape(1)
    first_row = jnp.arange(n_rows // tmb, dtype=I32) * tmb
    block_e = jnp.minimum(jnp.sum((ends[None, :] <= first_row[:, None]).astype(I32), axis=1), N_EXPERTS - 1)

    tmd = _tile(N, MOE_TOKENS)
    dest_tab = _dest(sel, idx_tab, pstart, tmd)
    dest_flat = dest_tab[:, :TOP_K].reshape(N * TOP_K)
    xs = _dispatch(h2, dest_flat, starts + cnt, n_rows, tmb, tmd)
    eid = jnp.arange(N_EXPERTS, dtype=I32)
    later_used = (eid[None, :] > eid[:, None]) & (padded[None, :] > 0)
    next_used = jnp.min(jnp.where(later_used, eid[None, :], N_EXPERTS), axis=1)
    next_e = jnp.where(next_used < N_EXPERTS, next_used, -1)[block_e].astype(I32)
    ys = _ffn(xs, block_e, n_blocks, next_e, w_gu, b_gu, w_down, b_down, tmb)
    out = _combine(x1, gate_tab, g2, final_g, dest_flat, ys, S, _tile(S, MOE_TOKENS))
    return out.reshape(B, S, D)


def kernel(x, c, ada_w, ada_b, norm1_g, w_in, mix_scale, w_o, norm2_g, router_w, router_b, w_gu, b_gu,
           w_down, b_down, final_g):
    assert ada_w.shape[0] == 1, "single-layer stack"
    return _layer(x, c, ada_w[0], ada_b[0], norm1_g[0], w_in[0], mix_scale[0], w_o[0], norm2_g[0],
                  router_w[0], router_b[0], w_gu[0], b_gu[0], w_down[0], b_down[0], final_g)
```

```python
import functools

import numpy as np
import jax
import jax.numpy as jnp
from jax import lax
from jax.experimental import pallas as pl
from jax.experimental.pallas import tpu as pltpu

F32 = jnp.float32
BF16 = jnp.bfloat16
I32 = jnp.int32

D_MODEL = 1024
RET_HEADS = 4
RET_DK = 64
RET_DV = 128
RET_CHUNK = 128
DSA_HEADS = 8
DSA_KV_HEADS = 2
DSA_HD = 64
IDX_HEADS = 8
IDX_HD = 64
TOPK_MAX = 256
N_EXPERTS = 32
TOP_K = 4
D_EXPERT = D_MODEL
SWIGLU_LIMIT = 7.0
SWIGLU_ALPHA = 1.702
EPS = 1e-6

RET_W = RET_HEADS * RET_DV
DSA_W = DSA_HEADS * DSA_HD
IN_COLS = 2888
IN_COLS_PAD = 2944

KAUG = 128
VAUG = 80
ALIBI_SPLIT = 64
ALIBI_TERMS = 3
LOG2E = 1.4426950408889634
PROJ_ROWS = 512
DSA_QUERIES = 512
DSA_KEYS = 128
MOE_TOKENS = 256
RET_CHUNKS_PER_STEP = 4
FFN_ROWS = 512
BISECT_VALUE_STEPS = 8
BISECT_MAX_STEPS = 64

LANES = 128
NCH = D_MODEL // LANES
VMEM_LIMIT = 56 * 1024 * 1024
NEG_BIG = -1e30
F32_LOWEST = float(np.finfo(np.float32).min)


def _cparams(sem):
    return pltpu.CompilerParams(dimension_semantics=sem, vmem_limit_bytes=VMEM_LIMIT)


def _mod_kernel(c_ref, w_ref, b_ref, o_ref):
    c = c_ref[...]
    s = c * (1.0 / (1.0 + jnp.exp(-c)))
    o_ref[...] = jnp.dot(s, w_ref[...], preferred_element_type=F32,
                         precision=lax.Precision.HIGHEST) + b_ref[...]


def _mod(c, ada_w, ada_b):
    B, D = c.shape
    n_out = ada_w.shape[1]
    rows = 8
    c8 = jnp.zeros((rows, D), F32).at[:B].set(c)
    out = pl.pallas_call(
        _mod_kernel,
        grid=(n_out // D,),
        in_specs=[pl.BlockSpec((rows, D), lambda j: (0, 0)),
                  pl.BlockSpec((D, D), lambda j: (0, j)),
                  pl.BlockSpec((1, D), lambda j: (0, j))],
        out_specs=pl.BlockSpec((rows, D), lambda j: (0, j)),
        out_shape=jax.ShapeDtypeStruct((rows, n_out), F32),
        compiler_params=_cparams(("arbitrary",)),
        name="mod",
    )(c8, ada_w, ada_b.reshape(1, n_out))
    return out[:B]


def _inproj_kernel(x_ref, g_ref, sc_ref, sh_ref, w_ref,
                   rq_ref, rk_ref, rv_ref, rg_ref, aqt_ref, ak_ref, avt_ref, iqt_ref, ik_ref, iwt_ref):
    x = x_ref[...]
    ms = jnp.mean(x * x, axis=-1, keepdims=True)
    y = x * lax.rsqrt(ms + EPS) * g_ref[...]
    hb = (y * (1.0 + sc_ref[...]) + sh_ref[...]).astype(BF16)

    def proj(lo, hi):
        return jnp.dot(hb, w_ref[:, lo:hi], preferred_element_type=F32)

    tm = x.shape[0]
    d = DSA_HD
    rq_ref[...] = proj(0, 256).astype(BF16)
    rk_ref[...] = (proj(256, 512) * (RET_DK ** -0.5)).astype(BF16)
    rv_ref[...] = proj(512, 1024).astype(BF16)
    rg_ref[...] = proj(1024, 1536).astype(BF16)
    aqt_ref[...] = (proj(1536, 2048) * (d ** -0.5 * LOG2E)).T.astype(BF16)
    kk = proj(2048, 2176)
    pos = pl.program_id(1) * tm + lax.broadcasted_iota(I32, (tm, d), 0)
    col = lax.broadcasted_iota(I32, (tm, d), 1)
    posblk = jnp.where(col < ALIBI_TERMS, pos // ALIBI_SPLIT,
                       jnp.where(col < 2 * ALIBI_TERMS, pos % ALIBI_SPLIT, 0)).astype(F32)
    for g in range(DSA_KV_HEADS):
        ak_ref[:, g * KAUG:g * KAUG + d] = kk[:, g * d:(g + 1) * d].astype(BF16)
        ak_ref[:, g * KAUG + d:(g + 1) * KAUG] = posblk.astype(BF16)
    vt = proj(2176, 2304).T
    r16 = lax.broadcasted_iota(I32, (VAUG - d, tm), 0)
    onesblk = jnp.where(r16 == 0, 1.0, 0.0).astype(BF16)
    for g in range(DSA_KV_HEADS):
        avt_ref[g * VAUG:g * VAUG + d, :] = vt[g * d:(g + 1) * d, :].astype(BF16)
        avt_ref[g * VAUG + d:(g + 1) * VAUG, :] = onesblk
    iqt_ref[...] = proj(2304, 2816).T.astype(BF16)
    last = proj(2816, 2944)
    ik_ref[...] = last[:, :IDX_HD].astype(BF16)
    iwt_ref[...] = last.T[IDX_HD:IDX_HD + IDX_HEADS, :] * ((IDX_HD ** -0.5) * (IDX_HEADS ** -0.5))


def _inproj(x, norm_g, sc, sh, w_pad, tm):
    B, S, D = x.shape
    row = lambda w: pl.BlockSpec((None, tm, w), lambda b, i: (b, i, 0))
    colT = lambda h: pl.BlockSpec((None, h, tm), lambda b, i: (b, 0, i))
    vec = pl.BlockSpec((None, 1, D), lambda b, i: (b, 0, 0))
    sd = lambda shape, dt: jax.ShapeDtypeStruct(shape, dt)
    G = DSA_KV_HEADS
    return pl.pallas_call(
        _inproj_kernel,
        grid=(B, S // tm),
        in_specs=[row(D), pl.BlockSpec((1, D), lambda b, i: (0, 0)), vec, vec,
                  pl.BlockSpec((D, IN_COLS_PAD), lambda b, i: (0, 0))],
        out_specs=[row(256), row(256), row(512), row(512), colT(DSA_W), row(G * KAUG), colT(G * VAUG),
                   colT(IDX_HEADS * IDX_HD), row(IDX_HD), colT(IDX_HEADS)],
        out_shape=[sd((B, S, 256), BF16), sd((B, S, 256), BF16), sd((B, S, 512), BF16),
                   sd((B, S, 512), BF16), sd((B, DSA_W, S), BF16), sd((B, S, G * KAUG), BF16),
                   sd((B, G * VAUG, S), BF16), sd((B, IDX_HEADS * IDX_HD, S), BF16),
                   sd((B, S, IDX_HD), BF16), sd((B, IDX_HEADS, S), F32)],
        compiler_params=_cparams(("parallel", "parallel")),
        name="inproj",
    )(x, norm_g.reshape(1, D), sc, sh, w_pad)


def _ret_kernel(rq_ref, rk_ref, rv_ref, rg_ref, din_ref, qd_ref, kd_ref, cd_ref, ms_ref, o_ref, state_ref):
    @pl.when(pl.program_id(1) == 0)
    def _():
        state_ref[...] = jnp.zeros_like(state_ref)

    C = din_ref.shape[1]
    for c in range(rq_ref.shape[0] // C):
        rows = slice(c * C, (c + 1) * C)
        for h in range(RET_HEADS):
            q = rq_ref[rows, h * RET_DK:(h + 1) * RET_DK]
            k = rk_ref[rows, h * RET_DK:(h + 1) * RET_DK]
            v = rv_ref[rows, h * RET_DV:(h + 1) * RET_DV]
            r_prev = state_ref[h]
            s = lax.dot_general(q, k, (((1,), (1,)), ((), ())), preferred_element_type=F32) * din_ref[h]
            o = jnp.dot(s.astype(BF16), v, preferred_element_type=F32)
            o = o + jnp.dot(q, r_prev.astype(BF16), preferred_element_type=F32) * qd_ref[h]
            vd = (v.astype(F32) * kd_ref[h]).astype(BF16)
            kv = lax.dot_general(k, vd, (((0,), (0,)), ((), ())), preferred_element_type=F32)
            state_ref[h] = r_prev * cd_ref[h] + kv
            o = o * lax.rsqrt(jnp.mean(o * o, axis=-1, keepdims=True) + EPS)
            g = rg_ref[rows, h * RET_DV:(h + 1) * RET_DV].astype(F32)
            gate = g * (1.0 / (1.0 + jnp.exp(-g)))
            o_ref[rows, h * RET_DV:(h + 1) * RET_DV] = (
                gate * o * ms_ref[:, h * RET_DV:(h + 1) * RET_DV]).astype(BF16)


def _ret_consts(C):
    H = RET_HEADS
    log_g = np.log1p(-np.exp2(-5.0 - np.arange(H, dtype=np.float64)))
    pos = np.arange(C, dtype=np.float64)
    diff = pos[:, None] - pos[None, :]
    d_inner = np.where(diff[None] >= 0, np.exp(np.maximum(diff, 0.0)[None] * log_g[:, None, None]), 0.0)
    q_decay = np.exp((pos + 1.0)[None] * log_g[:, None])
    k_decay = np.exp((C - 1.0 - pos)[None] * log_g[:, None])
    chunk_decay = np.exp(C * log_g)
    qd = np.broadcast_to(q_decay[:, :, None], (H, C, RET_DV))
    kd = np.broadcast_to(k_decay[:, :, None], (H, C, RET_DV))
    cd = np.broadcast_to(chunk_decay[:, None, None], (H, 1, RET_DV))
    f = lambda a: jnp.asarray(np.ascontiguousarray(a), F32)
    return f(d_inner), f(qd), f(kd), f(cd)


def _retention(rq, rk, rv, rg, ms_ret):
    B, S, _ = rq.shape
    C = min(RET_CHUNK, S)
    din, qd, kd, cd = _ret_consts(C)
    rows = _tile(S, RET_CHUNKS_PER_STEP * C)
    row = lambda w: pl.BlockSpec((None, rows, w), lambda b, n: (b, n, 0))
    full = lambda a: pl.BlockSpec(a.shape, lambda b, n: (0,) * a.ndim)
    return pl.pallas_call(
        _ret_kernel,
        grid=(B, S // rows),
        in_specs=[row(256), row(256), row(512), row(512), full(din), full(qd), full(kd), full(cd),
                  pl.BlockSpec((1, RET_W), lambda b, n: (0, 0))],
        out_specs=row(RET_W),
        out_shape=jax.ShapeDtypeStruct((B, S, RET_W), BF16),
        scratch_shapes=[pltpu.VMEM((RET_HEADS, RET_DK, RET_DV), F32)],
        compiler_params=_cparams(("parallel", "arbitrary")),
        name="ret",
    )(rq, rk, rv, rg, din, qd, kd, cd, ms_ret)


def _f32_key(x):
    i = lax.bitcast_convert_type(x, I32)
    return i ^ ((i >> 31) & 0x7FFFFFFF)


def _key_f32(k):
    return lax.bitcast_convert_type(k ^ ((k >> 31) & 0x7FFFFFFF), F32)


def _dsa_kernel(iqt_ref, iwt_ref, aqt_ref, ik_ref, ak_ref, avt_ref, ms_ref, o_ref, score_ref, qa_ref, sa_ref, sb_ref, mask_ref, *acc_refs,
                tq, tks, n_sel):
    H, G, R, d = DSA_HEADS, DSA_KV_HEADS, DSA_HEADS // DSA_KV_HEADS, DSA_HD
    t0 = pl.program_id(1) * tq
    nsub = (t0 + tq) // tks
    tka = 2 * tks
    npair = ((t0 + tq) // tka + 1) // 2
    kf = float(n_sel)
    qpos = t0 + lax.broadcasted_iota(I32, (1, tq), 1)
    krow = lax.broadcasted_iota(I32, (tks, tq), 0)

    wrow = [iwt_ref[h:h + 1, :] for h in range(IDX_HEADS)]

    def fold8(x, op):
        acc = x[0:8, :]
        for i in range(1, tks // 8):
            acc = op(acc, x[8 * i:8 * (i + 1), :])
        return acc

    def score_pair(i, carry, masked):
        mx, mn, npos, nnon = carry
        for u in range(2 * tka // tks):
            r0 = pl.multiple_of(i * 2 * tka + u * tks, tks)
            kc = ik_ref[pl.ds(r0, tks), :]
            acc = jnp.zeros((tks, tq), F32)
            for h in range(IDX_HEADS):
                rel = jnp.dot(kc, iqt_ref[h * IDX_HD:(h + 1) * IDX_HD, :], preferred_element_type=F32)
                acc = acc + jnp.maximum(rel, 0.0) * wrow[h]
            if masked:
                causal = r0 + krow <= qpos
                sc = jnp.where(causal, acc, -jnp.inf)
                lowest = jnp.where(causal, acc, jnp.inf)
            else:
                sc = lowest = acc
            score_ref[pl.ds(r0, tks), :] = sc
            mx = jnp.maximum(mx, fold8(sc, jnp.maximum))
            mn = jnp.minimum(mn, fold8(lowest, jnp.minimum))
            npos = npos + fold8(jnp.where(sc > 0.0, 1.0, 0.0), jnp.add)
            nnon = nnon + fold8(jnp.where(sc >= 0.0, 1.0, 0.0), jnp.add)
        return mx, mn, npos, nnon

    stat0 = (jnp.full((8, tq), -jnp.inf, F32), jnp.full((8, tq), jnp.inf, F32),
             jnp.zeros((8, tq), F32), jnp.zeros((8, tq), F32))
    n_inner = (t0 + 1) // (2 * tka)
    stat = lax.fori_loop(0, n_inner, functools.partial(score_pair, masked=False), stat0)
    mx, mn, npos, nnon = lax.fori_loop(n_inner, npair, functools.partial(score_pair, masked=True), stat)
    top = jnp.max(mx, axis=0, keepdims=True)
    lo0 = jnp.min(mn, axis=0, keepdims=True)
    n_pos = jnp.sum(npos, axis=0, keepdims=True)
    n_nonneg = jnp.sum(nnon, axis=0, keepdims=True)


    def count(th, strict):
        def body(j, acc):
            for u in range(2):
                s = score_ref[pl.ds(pl.multiple_of((2 * j + u) * tks, tks), tks), :]
                hit = (s > th) if strict else (s >= th)
                acc = acc + fold8(jnp.where(hit, 1.0, 0.0), jnp.add)
            return acc

        acc = lax.fori_loop(0, nsub // 2, body, jnp.zeros((8, tq), F32))
        return jnp.sum(acc, axis=0, keepdims=True)

    def probe(lo, hi, it):
        lk, hk = _f32_key(lo), _f32_key(hi)
        mk = (lk >> 1) + (hk >> 1) + (lk & hk & 1)
        mv = lo + (hi - lo) * 0.5
        early = (jnp.zeros((1, tq), I32) + it) < BISECT_VALUE_STEPS
        mid = jnp.where(early & (mv > lo) & (mv < hi), mv, _key_f32(mk))
        return mid, jnp.max(jnp.where(mk != lk, 1.0, 0.0))

    def bis_cond(c):
        return (c[5] > 0.0) & (c[6] < BISECT_MAX_STEPS)

    def bis_body(c):
        lo, hi, c_lo, c_hi, mid, _, it = c
        cnt = count(mid, False)
        ge = cnt >= kf
        up = ge | (cnt == kf)
        dn = (~ge) | (cnt == kf)
        lo, c_lo = jnp.where(up, mid, lo), jnp.where(up, cnt, c_lo)
        hi, c_hi = jnp.where(dn, mid, hi), jnp.where(dn, cnt, c_hi)
        mid, active = probe(lo, hi, it + 1)
        return lo, hi, c_lo, c_hi, mid, active, it + 1

    zero = jnp.zeros((1, tq), F32)
    keep_all = qpos + 1 <= n_sel
    settled = keep_all | ((n_nonneg >= kf) & (n_pos < kf))
    above = n_pos >= kf
    c_lo0 = jnp.where(settled | above, n_nonneg, (qpos + 1).astype(F32))
    c_hi0 = jnp.where(settled | ~above, n_nonneg, zero)
    lo0 = jnp.where(settled | above, zero, lo0)
    hi0 = jnp.where(settled | ~above, zero, _key_f32(_f32_key(top) + 1))
    mid0, active0 = probe(lo0, hi0, jnp.int32(0))
    lo, hi, c_lo, c_hi, _, _, _ = lax.while_loop(bis_cond, bis_body,
                                                 (lo0, hi0, c_lo0, c_hi0, mid0, active0, jnp.int32(0)))
    at_hi = c_hi >= kf
    thr = jnp.where(keep_all, F32_LOWEST, jnp.where(at_hi, hi, lo))
    excess = jnp.where(keep_all, 0.0, jnp.where(at_hi, c_hi, c_lo) - kf)

    @pl.when(jnp.max(excess) > 0.0)
    def _():
        tied_nonzero = jnp.max(jnp.where((excess > 0.0) & (thr != 0.0), 1.0, 0.0)) > 0.0
        n_above = lax.cond(tied_nonzero, lambda: count(thr, True), lambda: n_pos)
        budget = jnp.where(excess > 0.0, kf - jnp.where(thr == 0.0, n_pos, n_above), jnp.inf)
        earlier = lax.broadcasted_iota(I32, (tks, tks), 1) < lax.broadcasted_iota(I32, (tks, tks), 0)
        earlier = jnp.where(earlier, 1.0, 0.0).astype(BF16)

        def fix(j, seen):
            tiles = []
            for u in range(2):
                r0 = pl.multiple_of((2 * j + u) * tks, tks)
                s = score_ref[pl.ds(r0, tks), :]
                eq = s == thr
                eqf = jnp.where(eq, 1.0, 0.0)
                within = jnp.dot(earlier, eqf.astype(BF16), preferred_element_type=F32)
                tiles.append((r0, s, eq, within, jnp.sum(eqf, axis=0, keepdims=True)))
            for r0, s, eq, within, n_eq in tiles:
                score_ref[pl.ds(r0, tks), :] = jnp.where(eq & (within + seen >= budget), -jnp.inf, s)
                seen = seen + n_eq
            return seen

        lax.fori_loop(0, nsub // 2, fix, jnp.zeros((1, tq), F32))

    arow = lax.broadcasted_iota(I32, (KAUG - d, tq), 0)
    for h in range(H):
        terms, rest = [], 2.0 ** (-8.0 * (h + 1) / H) * LOG2E
        for _ in range(ALIBI_TERMS):
            terms.append(float(np.asarray(rest, np.float32).astype(BF16).astype(np.float64)))
            rest -= terms[-1]
        rows = jnp.zeros((KAUG - d, tq), F32)
        for i, c in enumerate(terms):
            rows = jnp.where(arow == i, c * ALIBI_SPLIT, jnp.where(arow == ALIBI_TERMS + i, c, rows))
        qa_ref[h, 0:d, :] = aqt_ref[h * d:(h + 1) * d, :]
        qa_ref[h, d:KAUG, :] = rows.astype(BF16)
    for acc in acc_refs:
        acc[...] = jnp.zeros_like(acc)

    def logits(j, h):
        ka = ak_ref[pl.ds(pl.multiple_of(j * tka, tka), tka), (h // R) * KAUG:(h // R + 1) * KAUG]
        return jnp.dot(ka, qa_ref[h], preferred_element_type=F32)

    def step(j, j_next, cur_ref, next_ref, ms):
        r0 = pl.multiple_of(j * tka, tka)
        mask_ref[...] = jnp.where(score_ref[pl.ds(r0, tka), :] >= thr, 0.0, NEG_BIG)
        nms = []
        ahead = logits(j_next, 0)
        for h in range(H):
            g = h // R
            next_ref[h] = ahead
            if h + 1 < H:
                ahead = logits(j_next, h + 1)
            s = cur_ref[h] + mask_ref[...]
            m_new = jnp.maximum(ms[h], jnp.max(s, axis=0, keepdims=True))
            p = jnp.exp2(s - m_new).astype(BF16)
            va = avt_ref[g * VAUG:(g + 1) * VAUG, pl.ds(r0, tka)]
            acc = acc_refs[h]
            acc[...] = acc[...] * jnp.exp2(ms[h] - m_new) + jnp.dot(va, p, preferred_element_type=F32)
            nms.append(m_new)
        return tuple(nms)

    for h in range(H):
        sa_ref[h] = logits(0, h)

    def att_pair(i, ms):
        ms = step(2 * i, 2 * i + 1, sa_ref, sb_ref, ms)
        return step(2 * i + 1, jnp.minimum(2 * i + 2, 2 * npair - 1), sb_ref, sa_ref, ms)

    lax.fori_loop(0, npair, att_pair, tuple(jnp.full((1, tq), NEG_BIG, F32) for _ in range(H)))
    for h in range(H):
        a = acc_refs[h][...]
        o = a[0:d, :] / a[d:d + 1, :]
        o_ref[h * d:(h + 1) * d, :] = (o * ms_ref[h * d:(h + 1) * d, :]).astype(BF16)


def _dsa(iqt, iwt, aqt, ik, ak, avt, ms_att, tq, tks):
    B, _, S = iqt.shape
    assert S <= ALIBI_SPLIT * 256 and tq % (2 * tks) == 0
    assert (S // (2 * tks)) % 2 == 0
    n_sel = min(TOPK_MAX, S // 4)
    G = DSA_KV_HEADS
    colT = lambda h: pl.BlockSpec((None, h, tq), lambda b, i: (b, 0, i))
    msb = jnp.broadcast_to(ms_att.reshape(DSA_W, 1), (DSA_W, tq))
    return pl.pallas_call(
        functools.partial(_dsa_kernel, tq=tq, tks=tks, n_sel=n_sel),
        grid=(B, S // tq),
        in_specs=[colT(IDX_HEADS * IDX_HD), colT(IDX_HEADS), colT(DSA_W),
                  pl.BlockSpec((None, S, IDX_HD), lambda b, i: (b, 0, 0)),
                  pl.BlockSpec((None, S, G * KAUG), lambda b, i: (b, 0, 0)),
                  pl.BlockSpec((None, G * VAUG, S), lambda b, i: (b, 0, 0)),
                  pl.BlockSpec((DSA_W, tq), lambda b, i: (0, 0))],
        out_specs=colT(DSA_W),
        out_shape=jax.ShapeDtypeStruct((B, DSA_W, S), BF16),
        scratch_shapes=[pltpu.VMEM((S, tq), F32), pltpu.VMEM((DSA_HEADS, KAUG, tq), BF16),
                        pltpu.VMEM((DSA_HEADS, 2 * tks, tq), F32), pltpu.VMEM((DSA_HEADS, 2 * tks, tq), F32),
                        pltpu.VMEM((2 * tks, tq), F32)]
        + [pltpu.VMEM((VAUG, tq), F32) for _ in range(DSA_HEADS)],
        compiler_params=_cparams(("parallel", "arbitrary")),
        name="dsa",
    )(iqt, iwt, aqt, ik, ak, avt, msb)


def _oproj_kernel(ret_ref, att_ref, x_ref, wo_ref, g1_ref, n2_ref, sc_ref, sh_ref, rw_ref, rb_ref,
                  x1_ref, h2_ref, sel_ref, idx_ref, gate_ref, cnt_ref):
    mixo = jnp.dot(ret_ref[...], wo_ref[:RET_W, :], preferred_element_type=F32)
    mixo = mixo + lax.dot_general(att_ref[...], wo_ref[RET_W:, :], (((0,), (0,)), ((), ())),
                                  preferred_element_type=F32)
    x1 = x_ref[...] + g1_ref[...] * mixo
    x1_ref[...] = x1
    y = x1 * lax.rsqrt(jnp.mean(x1 * x1, axis=-1, keepdims=True) + EPS) * n2_ref[...]
    h2 = y * (1.0 + sc_ref[...]) + sh_ref[...]
    _store_token_major(h2_ref, h2)
    h_hi = h2.astype(BF16)
    h_lo = (h2 - h_hi.astype(F32)).astype(BF16)
    logits = jnp.dot(h_hi, rw_ref[0], preferred_element_type=F32)
    logits = logits + (jnp.dot(h_hi, rw_ref[1], preferred_element_type=F32)
                       + jnp.dot(h_lo, rw_ref[0], preferred_element_type=F32)) + rb_ref[...]
    tm = logits.shape[0]
    lane = lax.broadcasted_iota(I32, (tm, LANES), 1).astype(F32)
    work = jnp.where(lane < N_EXPERTS, logits, -jnp.inf)
    sel = jnp.zeros((tm, LANES), F32)
    idx_tab = jnp.zeros((tm, LANES), F32)
    vals = []
    for k in range(TOP_K):
        m = jnp.max(work, axis=1, keepdims=True)
        idx = jnp.min(jnp.where(work == m, lane, float(LANES)), axis=1, keepdims=True)
        hit = lane == idx
        sel = jnp.where(hit, 1.0, sel)
        idx_tab = jnp.where(lane == k, idx, idx_tab)
        work = jnp.where(hit, -jnp.inf, work)
        vals.append(m)
    es = [jnp.exp(v - vals[0]) for v in vals]
    den = es[0] + es[1] + es[2] + es[3]
    gate_tab = jnp.zeros((tm, LANES), F32)
    for k in range(TOP_K):
        gate_tab = jnp.where(lane == k, es[k] / den, gate_tab)
    sel_ref[...] = sel
    idx_ref[...] = idx_tab
    gate_ref[...] = gate_tab

    @pl.when((pl.program_id(0) == 0) & (pl.program_id(1) == 0))
    def _():
        cnt_ref[...] = jnp.zeros_like(cnt_ref)

    cnt_ref[...] += jnp.sum(sel, axis=0, keepdims=True)


def _oproj(ret, att, x, wo, g1, n2g, sc2, sh2, rw_pad, rb_pad, tm):
    B, S, D = x.shape
    nt = S // tm
    row = lambda w: pl.BlockSpec((None, tm, w), lambda b, i: (b, i, 0))
    flat = lambda w: pl.BlockSpec((tm, w), lambda b, i: (b * nt + i, 0))
    vec = pl.BlockSpec((None, 1, D), lambda b, i: (b, 0, 0))
    cst = lambda shape: pl.BlockSpec(shape, lambda b, i: (0, 0))
    sd = lambda shape, dt: jax.ShapeDtypeStruct(shape, dt)
    N = B * S
    return pl.pallas_call(
        _oproj_kernel,
        grid=(B, nt),
        in_specs=[row(RET_W), pl.BlockSpec((None, DSA_W, tm), lambda b, i: (b, 0, i)), row(D), cst((D, D)), vec,
                  cst((1, D)), vec, vec,
                  pl.BlockSpec((2, D, LANES), lambda b, i: (0, 0, 0)), cst((1, LANES))],
        out_specs=[flat(D), pl.BlockSpec((tm * (D // LANES), LANES), lambda b, i: (b * nt + i, 0)),
                   flat(LANES), flat(LANES), flat(LANES), cst((1, LANES))],
        out_shape=[sd((N, D), F32), sd((N * (D // LANES), LANES), F32), sd((N, LANES), F32), sd((N, LANES), F32),
                   sd((N, LANES), F32), sd((1, LANES), F32)],
        compiler_params=_cparams(("arbitrary", "arbitrary")),
        name="oproj",
    )(ret, att, x, wo, g1, n2g.reshape(1, D), sc2, sh2, rw_pad, rb_pad)


def _dest_kernel(sel_ref, idx_ref, pstart_ref, dest_ref, seen_ref):
    @pl.when(pl.program_id(0) == 0)
    def _():
        seen_ref[...] = jnp.zeros_like(seen_ref)

    sel = sel_ref[...]
    tm = sel.shape[0]
    earlier = lax.broadcasted_iota(I32, (tm, tm), 1) < lax.broadcasted_iota(I32, (tm, tm), 0)
    earlier = jnp.where(earlier, 1.0, 0.0).astype(BF16)
    rank = jnp.dot(earlier, sel.astype(BF16), preferred_element_type=F32) + seen_ref[...]
    dest = pstart_ref[...] + rank
    lane = lax.broadcasted_iota(I32, (tm, LANES), 1).astype(F32)
    idx_tab = idx_ref[...]
    out = jnp.zeros((tm, LANES), F32)
    for k in range(TOP_K):
        e_k = jnp.sum(jnp.where(lane == k, idx_tab, 0.0), axis=1, keepdims=True)
        d_k = jnp.sum(jnp.where(lane == e_k, dest, 0.0), axis=1, keepdims=True)
        out = jnp.where(lane == k, d_k, out)
    dest_ref[...] = out.astype(I32)
    seen_ref[...] += jnp.sum(sel, axis=0, keepdims=True)


def _dest(sel, idx_tab, pstart, tm):
    N = sel.shape[0]
    blk = pl.BlockSpec((tm, LANES), lambda i: (i, 0))
    return pl.pallas_call(
        _dest_kernel,
        grid=(N // tm,),
        in_specs=[blk, blk, pl.BlockSpec((1, LANES), lambda i: (0, 0))],
        out_specs=blk,
        out_shape=jax.ShapeDtypeStruct((N, LANES), I32),
        scratch_shapes=[pltpu.VMEM((1, LANES), F32)],
        compiler_params=_cparams(("arbitrary",)),
        name="dest",
    )(sel, idx_tab, pstart)


def _disp_kernel(zs_ref, h2_ref, dest_hbm, xs_hbm, zbuf, idx_a, idx_b, sem_ia, sem_ib, sem_row, sem_z,
                 *, tm, nch, zrows, n_steps):
    i = pl.program_id(0)
    n = tm * TOP_K

    def idx_copy(tile, buf, sem):
        return pltpu.make_async_copy(dest_hbm.at[pl.ds(tile * n, n)], buf, sem)

    def rows(ref, first, count):
        return ref.at[pl.ds(pl.multiple_of(first * nch, nch), count * nch)]

    def row_copy(row, dst):
        return pltpu.make_async_copy(rows(h2_ref, row, 1), rows(xs_hbm, dst, 1), sem_row)

    def issue_rows(first_row, idx):
        def body(r, c):
            for s in range(8):
                t = r * 8 + s
                for k in range(TOP_K):
                    row_copy(first_row + t, idx[t * TOP_K + k]).start(priority=k % 2)
            return c

        lax.fori_loop(0, tm // 8, body, 0)

    def wait_rows():
        pltpu.make_async_copy(rows(xs_hbm, 0, n), rows(xs_hbm, 0, n), sem_row).wait()

    @pl.when(i == 0)
    def _():
        zbuf[...] = jnp.zeros_like(zbuf)
        for e in range(N_EXPERTS):
            fill = pltpu.make_async_copy(zbuf, rows(xs_hbm, zs_ref[e], zrows), sem_z)
            fill.start()
            fill.wait()
        idx_copy(0, idx_a, sem_ia).start()
        idx_copy(1, idx_b, sem_ib).start()

    idx_copy(2 * i, idx_a, sem_ia).wait()
    issue_rows(0, idx_a)
    idx_copy(2 * i + 1, idx_b, sem_ib).wait()
    issue_rows(tm, idx_b)

    @pl.when(i + 1 < n_steps)
    def _():
        idx_copy(2 * i + 2, idx_a, sem_ia).start()
        idx_copy(2 * i + 3, idx_b, sem_ib).start()

    wait_rows()
    wait_rows()


def _dispatch(h2, dest_flat, zero_start, n_rows, zrows, tm):
    nch = D_MODEL // LANES
    n_steps = h2.shape[0] // nch // (2 * tm)
    grid_spec = pltpu.PrefetchScalarGridSpec(
        num_scalar_prefetch=1,
        grid=(n_steps,),
        in_specs=[pl.BlockSpec((2 * tm * nch, LANES), lambda i, zs: (i, 0)), pl.BlockSpec(memory_space=pl.ANY)],
        out_specs=pl.BlockSpec(memory_space=pl.ANY),
        scratch_shapes=[pltpu.VMEM((zrows * nch, LANES), F32), pltpu.SMEM((tm * TOP_K,), I32),
                        pltpu.SMEM((tm * TOP_K,), I32), pltpu.SemaphoreType.DMA, pltpu.SemaphoreType.DMA,
                        pltpu.SemaphoreType.DMA, pltpu.SemaphoreType.DMA],
    )
    return pl.pallas_call(
        functools.partial(_disp_kernel, tm=tm, nch=nch, zrows=zrows, n_steps=n_steps),
        grid_spec=grid_spec,
        out_shape=jax.ShapeDtypeStruct((n_rows * nch, LANES), F32),
        compiler_params=_cparams(("arbitrary",)),
        name="disp",
    )(zero_start, h2, dest_flat)


def _store_token_major(ref, x):
    rows, d = x.shape
    nch = d // LANES
    for j in range(nch):
        ref[pl.ds(j, rows, stride=nch), :] = x[:, j * LANES:(j + 1) * LANES]


def _load_token_major(ref, rows, nch):
    return jnp.concatenate([ref[pl.ds(j, rows, stride=nch), :] for j in range(nch)], axis=1)


def _ffn_kernel(be_ref, nb_ref, nxt_ref, xs_ref, wgu_hbm, bgu_ref, wd_hbm, bd_ref, ys_ref,
                wgu_st, wd_st, wgu_bf, wd_bf, sem_gu, sem_d):
    i = pl.program_id(0)
    live = i < nb_ref[0]
    e = be_ref[i]

    def fetch(ex):
        return (pltpu.make_async_copy(wgu_hbm.at[ex], wgu_st, sem_gu),
                pltpu.make_async_copy(wd_hbm.at[ex], wd_st, sem_d))

    @pl.when(i == 0)
    def _():
        for cp in fetch(e):
            cp.start()

    @pl.when(live & ((i == 0) | (e != be_ref[jnp.maximum(i - 1, 0)])))
    def _():
        for cp in fetch(e):
            cp.wait()
        wgu_bf[...] = wgu_st[...].astype(BF16)
        wd_bf[...] = wd_st[...].astype(BF16)

        @pl.when(nxt_ref[i] >= 0)
        def _():
            for cp in fetch(nxt_ref[i]):
                cp.start()

    @pl.when(live)
    def _():
        xb = _load_token_major(xs_ref, ys_ref.shape[0] // NCH, NCH).astype(BF16)
        gu = jnp.dot(xb, wgu_bf[...], preferred_element_type=F32) + bgu_ref[...]
        gate = jnp.minimum(gu[:, :D_EXPERT], SWIGLU_LIMIT)
        up = jnp.clip(gu[:, D_EXPERT:], -SWIGLU_LIMIT, SWIGLU_LIMIT)
        glu = gate * (1.0 / (1.0 + jnp.exp(-SWIGLU_ALPHA * gate)))
        act = ((up + 1.0) * glu).astype(BF16)
        ys = jnp.dot(act, wd_bf[...], preferred_element_type=F32) + bd_ref[...]
        _store_token_major(ys_ref, ys)

    @pl.when(jnp.logical_not(live))
    def _():
        ys_ref[...] = jnp.zeros_like(ys_ref)


def _ffn(xs, block_e, n_blocks, next_e, w_gu, b_gu, w_down, b_down, tmb):
    D = D_MODEL
    P = xs.shape[0] // NCH
    E = w_gu.shape[0]
    blk = lambda i, be, nb, nx: (jnp.minimum(i, nb[0] - 1), 0)
    wsel = lambda i, be, nb, nx: (be[jnp.minimum(i, nb[0] - 1)], 0, 0)
    grid_spec = pltpu.PrefetchScalarGridSpec(
        num_scalar_prefetch=3,
        grid=(P // tmb,),
        in_specs=[pl.BlockSpec((tmb * NCH, LANES), blk),
                  pl.BlockSpec(memory_space=pl.ANY),
                  pl.BlockSpec((None, 1, 2 * D_EXPERT), wsel),
                  pl.BlockSpec(memory_space=pl.ANY),
                  pl.BlockSpec((None, 1, D), wsel)],
        out_specs=pl.BlockSpec((tmb * (D // LANES), LANES), lambda i, be, nb, nx: (i, 0)),
        scratch_shapes=[pltpu.VMEM((D, 2 * D_EXPERT), F32), pltpu.VMEM((D_EXPERT, D), F32),
                        pltpu.VMEM((D, 2 * D_EXPERT), BF16), pltpu.VMEM((D_EXPERT, D), BF16),
                        pltpu.SemaphoreType.DMA, pltpu.SemaphoreType.DMA],
    )
    return pl.pallas_call(
        _ffn_kernel,
        grid_spec=grid_spec,
        out_shape=jax.ShapeDtypeStruct((P * (D // LANES), LANES), F32),
        compiler_params=_cparams(("arbitrary",)),
        name="ffn",
    )(block_e, n_blocks, next_e, xs, w_gu, b_gu.reshape(E, 1, 2 * D_EXPERT), w_down, b_down.reshape(E, 1, D))


def _comb_kernel(x1_ref, gate_ref, g2_ref, fg_ref, dest_hbm, ys_hbm, o_ref, buf_a, buf_b, idx_a, idx_b,
                 sem_ia, sem_ib, sem_ra, sem_rb, *, tm, n_steps):
    i = pl.program_id(0)
    n = tm * TOP_K

    def idx_copy(tile, buf, sem):
        return pltpu.make_async_copy(dest_hbm.at[pl.ds(tile * n, n)], buf, sem)

    nch = x1_ref.shape[1] // LANES

    def row_copy(src, buf, k, t, sem):
        return pltpu.make_async_copy(ys_hbm.at[pl.ds(pl.multiple_of(src * nch, nch), nch)],
                                     buf.at[k, t[0], :, t[1]], sem)

    def issue_rows(idx, buf, sem):
        def body(r, c):
            for s in range(8):
                for k in range(TOP_K):
                    row_copy(idx[(r * 8 + s) * TOP_K + k], buf, k, (r, s), sem).start(priority=k % 2)
            return c

        lax.fori_loop(0, tm // 8, body, 0)

    def wait_rows(buf, sem):
        pltpu.make_async_copy(buf, buf, sem).wait()

    def finish(buf, lo):
        gates = gate_ref[lo:lo + tm, :]
        y = jnp.zeros((tm, x1_ref.shape[1]), F32)
        for k in range(TOP_K):
            rows = jnp.concatenate([buf[k, :, j].reshape(tm, LANES) for j in range(nch)], axis=1)
            y = y + gates[:, k:k + 1] * rows
        v = x1_ref[lo:lo + tm, :] + g2_ref[...] * y
        o_ref[lo:lo + tm, :] = v * lax.rsqrt(jnp.mean(v * v, axis=-1, keepdims=True) + EPS) * fg_ref[...]

    @pl.when(i == 0)
    def _():
        first = idx_copy(0, idx_a, sem_ia)
        first.start()
        first.wait()
        issue_rows(idx_a, buf_a, sem_ra)
        idx_copy(1, idx_b, sem_ib).start()

    idx_copy(2 * i + 1, idx_b, sem_ib).wait()
    issue_rows(idx_b, buf_b, sem_rb)

    @pl.when(i + 1 < n_steps)
    def _():
        idx_copy(2 * i + 2, idx_a, sem_ia).start()

    wait_rows(buf_a, sem_ra)
    finish(buf_a, 0)

    @pl.when(i + 1 < n_steps)
    def _():
        idx_copy(2 * i + 2, idx_a, sem_ia).wait()
        issue_rows(idx_a, buf_a, sem_ra)
        idx_copy(2 * i + 3, idx_b, sem_ib).start()

    wait_rows(buf_b, sem_rb)
    finish(buf_b, tm)


def _combine(x1, gate_tab, g2, final_g, dest_flat, ys, S, tm):
    N, D = x1.shape
    n_steps = N // (2 * tm)
    per_b = S // (2 * tm)
    return pl.pallas_call(
        functools.partial(_comb_kernel, tm=tm, n_steps=n_steps),
        grid=(n_steps,),
        in_specs=[pl.BlockSpec((2 * tm, D), lambda i: (i, 0)),
                  pl.BlockSpec((2 * tm, LANES), lambda i: (i, 0)),
                  pl.BlockSpec((None, 1, D), lambda i: (i // per_b, 0, 0)),
                  pl.BlockSpec((1, D), lambda i: (0, 0)),
                  pl.BlockSpec(memory_space=pl.ANY),
                  pl.BlockSpec(memory_space=pl.ANY)],
        out_specs=pl.BlockSpec((2 * tm, D), lambda i: (i, 0)),
        out_shape=jax.ShapeDtypeStruct((N, D), F32),
        scratch_shapes=[pltpu.VMEM((TOP_K, tm // 8, D // LANES, 8, LANES), F32),
                        pltpu.VMEM((TOP_K, tm // 8, D // LANES, 8, LANES), F32),
                        pltpu.SMEM((tm * TOP_K,), I32), pltpu.SMEM((tm * TOP_K,), I32),
                        pltpu.SemaphoreType.DMA, pltpu.SemaphoreType.DMA, pltpu.SemaphoreType.DMA,
                        pltpu.SemaphoreType.DMA],
        compiler_params=_cparams(("arbitrary",)),
        name="comb",
    )(x1, gate_tab, g2, final_g.reshape(1, D), dest_flat, ys)


def _tile(n, pref):
    t = min(pref, n)
    assert n % t == 0, (n, t)
    return t


def _layer(x, c, ada_w, ada_b, norm1_g, w_in, mix_scale, w_o, norm2_g,
           router_w, router_b, w_gu, b_gu, w_down, b_down, final_g):
    B, S, D = x.shape
    N = B * S
    mod = _mod(c, ada_w, ada_b).reshape(B, 6, 1, D)
    sh1, sc1, g1, sh2, sc2, g2 = (mod[:, j] for j in range(6))

    w_pad = jnp.pad(w_in, ((0, 0), (0, IN_COLS_PAD - IN_COLS))).astype(BF16)
    rq, rk, rv, rg, aqt, ak, avt, iqt, ik, iwt = _inproj(x, norm1_g, sc1, sh1, w_pad, _tile(S, PROJ_ROWS))
    ms = mix_scale.reshape(1, RET_W + DSA_W)
    ret = _retention(rq, rk, rv, rg, ms[:, :RET_W])
    att = _dsa(iqt, iwt, aqt, ik, ak, avt, ms[:, RET_W:], _tile(S, DSA_QUERIES), DSA_KEYS)

    rw_pad = jnp.pad(router_w, ((0, 0), (0, LANES - N_EXPERTS)))
    rw_hi = rw_pad.astype(BF16)
    rw_pad = jnp.stack([rw_hi, (rw_pad - rw_hi.astype(F32)).astype(BF16)])
    rb_pad = jnp.pad(router_b, (0, LANES - N_EXPERTS)).reshape(1, LANES)
    x1, h2, sel, idx_tab, gate_tab, counts = _oproj(ret, att, x, w_o.astype(BF16), g1, norm2_g, sc2, sh2,
                                                    rw_pad, rb_pad, _tile(S, PROJ_ROWS))

    tmb = FFN_ROWS
    n_rows = (N * TOP_K + N_EXPERTS * (tmb - 1)) // tmb * tmb + tmb
    cnt = counts[0, :N_EXPERTS].astype(I32)
    padded = (cnt + tmb - 1) // tmb * tmb
    ends = jnp.cumsum(padded)
    starts = ends - padded
    pstart = jnp.pad(starts.astype(F32), (0, LANES - N_EXPERTS)).reshape(1, LANES)
    n_blocks = (ends[-1] // tmb).reshape(1)
    first_row = jnp.arange(n_rows // tmb, dtype=I32) * tmb
    block_e = jnp.minimum(jnp.sum((ends[None, :] <= first_row[:, None]).astype(I32), axis=1), N_EXPERTS - 1)

    tmd = _tile(N, MOE_TOKENS)
    dest_tab = _dest(sel, idx_tab, pstart, tmd)
    dest_flat = dest_tab[:, :TOP_K].reshape(N * TOP_K)
    xs = _dispatch(h2, dest_flat, starts + cnt, n_rows, tmb, tmd)
    eid = jnp.arange(N_EXPERTS, dtype=I32)
    later_used = (eid[None, :] > eid[:, None]) & (padded[None, :] > 0)
    next_used = jnp.min(jnp.where(later_used, eid[None, :], N_EXPERTS), axis=1)
    next_e = jnp.where(next_used < N_EXPERTS, next_used, -1)[block_e].astype(I32)
    ys = _ffn(xs, block_e, n_blocks, next_e, w_gu, b_gu, w_down, b_down, tmb)
    out = _combine(x1, gate_tab, g2, final_g, dest_flat, ys, S, _tile(S, MOE_TOKENS))
    return out.reshape(B, S, D)


def kernel(x, c, ada_w, ada_b, norm1_g, w_in, mix_scale, w_o, norm2_g, router_w, router_b, w_gu, b_gu,
           w_down, b_down, final_g):
    assert ada_w.shape[0] == 1, "single-layer stack"
    return _layer(x, c, ada_w[0], ada_b[0], norm1_g[0], w_in[0], mix_scale[0], w_o[0], norm2_g[0],
                  router_w[0], router_b[0], w_gu[0], b_gu[0], w_down[0], b_down[0], final_g)
```

```python
import functools

import numpy as np
import jax
import jax.numpy as jnp
from jax import lax
from jax.experimental import pallas as pl
from jax.experimental.pallas import tpu as pltpu

F32 = jnp.float32
BF16 = jnp.bfloat16
I32 = jnp.int32

D_MODEL = 1024
RET_HEADS = 4
RET_DK = 64
RET_DV = 128
RET_CHUNK = 128
DSA_HEADS = 8
DSA_KV_HEADS = 2
DSA_HD = 64
IDX_HEADS = 8
IDX_HD = 64
TOPK_MAX = 256
N_EXPERTS = 32
TOP_K = 4
D_EXPERT = D_MODEL
SWIGLU_LIMIT = 7.0
SWIGLU_ALPHA = 1.702
EPS = 1e-6

RET_W = RET_HEADS * RET_DV
DSA_W = DSA_HEADS * DSA_HD
IN_COLS = 2888
IN_COLS_PAD = 2944

KAUG = 128
VAUG = 80
ALIBI_SPLIT = 64
ALIBI_TERMS = 3
LOG2E = 1.4426950408889634
PROJ_ROWS = 512
DSA_QUERIES = 512
DSA_KEYS = 128
MOE_TOKENS = 256
RET_CHUNKS_PER_STEP = 4
FFN_ROWS = 512
BISECT_COARSE_STEPS = 8
BISECT_VALUE_STEPS = 8
BISECT_MAX_STEPS = 64

LANES = 128
NCH = D_MODEL // LANES
VMEM_LIMIT = 56 * 1024 * 1024
NEG_BIG = -1e30
F32_LOWEST = float(np.finfo(np.float32).min)


def _cparams(sem):
    return pltpu.CompilerParams(dimension_semantics=sem, vmem_limit_bytes=VMEM_LIMIT)


def _mod_kernel(c_ref, w_ref, b_ref, o_ref):
    c = c_ref[...]
    s = c * (1.0 / (1.0 + jnp.exp(-c)))
    o_ref[...] = jnp.dot(s, w_ref[...], preferred_element_type=F32,
                         precision=lax.Precision.HIGHEST) + b_ref[...]


def _mod(c, ada_w, ada_b):
    B, D = c.shape
    n_out = ada_w.shape[1]
    rows = 8
    c8 = jnp.zeros((rows, D), F32).at[:B].set(c)
    out = pl.pallas_call(
        _mod_kernel,
        grid=(n_out // D,),
        in_specs=[pl.BlockSpec((rows, D), lambda j: (0, 0)),
                  pl.BlockSpec((D, D), lambda j: (0, j)),
                  pl.BlockSpec((1, D), lambda j: (0, j))],
        out_specs=pl.BlockSpec((rows, D), lambda j: (0, j)),
        out_shape=jax.ShapeDtypeStruct((rows, n_out), F32),
        compiler_params=_cparams(("arbitrary",)),
        name="mod",
    )(c8, ada_w, ada_b.reshape(1, n_out))
    return out[:B]


def _inproj_kernel(x_ref, g_ref, sc_ref, sh_ref, w_ref,
                   rq_ref, rk_ref, rv_ref, rg_ref, aqt_ref, ak_ref, avt_ref, iqt_ref, ik_ref, iwt_ref):
    x = x_ref[...]
    ms = jnp.mean(x * x, axis=-1, keepdims=True)
    y = x * lax.rsqrt(ms + EPS) * g_ref[...]
    hb = (y * (1.0 + sc_ref[...]) + sh_ref[...]).astype(BF16)

    def proj(lo, hi):
        return jnp.dot(hb, w_ref[:, lo:hi], preferred_element_type=F32)

    tm = x.shape[0]
    d = DSA_HD
    rq_ref[...] = proj(0, 256).astype(BF16)
    rk_ref[...] = (proj(256, 512) * (RET_DK ** -0.5)).astype(BF16)
    rv_ref[...] = proj(512, 1024).astype(BF16)
    rg_ref[...] = proj(1024, 1536).astype(BF16)
    aqt_ref[...] = (proj(1536, 2048) * (d ** -0.5 * LOG2E)).T.astype(BF16)
    kk = proj(2048, 2176)
    pos = pl.program_id(1) * tm + lax.broadcasted_iota(I32, (tm, d), 0)
    col = lax.broadcasted_iota(I32, (tm, d), 1)
    posblk = jnp.where(col < ALIBI_TERMS, pos // ALIBI_SPLIT,
                       jnp.where(col < 2 * ALIBI_TERMS, pos % ALIBI_SPLIT, 0)).astype(F32)
    for g in range(DSA_KV_HEADS):
        ak_ref[:, g * KAUG:g * KAUG + d] = kk[:, g * d:(g + 1) * d].astype(BF16)
        ak_ref[:, g * KAUG + d:(g + 1) * KAUG] = posblk.astype(BF16)
    vt = proj(2176, 2304).T
    r16 = lax.broadcasted_iota(I32, (VAUG - d, tm), 0)
    onesblk = jnp.where(r16 == 0, 1.0, 0.0).astype(BF16)
    for g in range(DSA_KV_HEADS):
        avt_ref[g * VAUG:g * VAUG + d, :] = vt[g * d:(g + 1) * d, :].astype(BF16)
        avt_ref[g * VAUG + d:(g + 1) * VAUG, :] = onesblk
    iqt_ref[...] = proj(2304, 2816).T.astype(BF16)
    last = proj(2816, 2944)
    ik_ref[...] = last[:, :IDX_HD].astype(BF16)
    iwt_ref[...] = last.T[IDX_HD:IDX_HD + IDX_HEADS, :] * ((IDX_HD ** -0.5) * (IDX_HEADS ** -0.5))


def _inproj(x, norm_g, sc, sh, w_pad, tm):
    B, S, D = x.shape
    row = lambda w: pl.BlockSpec((None, tm, w), lambda b, i: (b, i, 0))
    colT = lambda h: pl.BlockSpec((None, h, tm), lambda b, i: (b, 0, i))
    vec = pl.BlockSpec((None, 1, D), lambda b, i: (b, 0, 0))
    sd = lambda shape, dt: jax.ShapeDtypeStruct(shape, dt)
    G = DSA_KV_HEADS
    return pl.pallas_call(
        _inproj_kernel,
        grid=(B, S // tm),
        in_specs=[row(D), pl.BlockSpec((1, D), lambda b, i: (0, 0)), vec, vec,
                  pl.BlockSpec((D, IN_COLS_PAD), lambda b, i: (0, 0))],
        out_specs=[row(256), row(256), row(512), row(512), colT(DSA_W), row(G * KAUG), colT(G * VAUG),
                   colT(IDX_HEADS * IDX_HD), row(IDX_HD), colT(IDX_HEADS)],
        out_shape=[sd((B, S, 256), BF16), sd((B, S, 256), BF16), sd((B, S, 512), BF16),
                   sd((B, S, 512), BF16), sd((B, DSA_W, S), BF16), sd((B, S, G * KAUG), BF16),
                   sd((B, G * VAUG, S), BF16), sd((B, IDX_HEADS * IDX_HD, S), BF16),
                   sd((B, S, IDX_HD), BF16), sd((B, IDX_HEADS, S), F32)],
        compiler_params=_cparams(("parallel", "parallel")),
        name="inproj",
    )(x, norm_g.reshape(1, D), sc, sh, w_pad)


def _ret_kernel(rq_ref, rk_ref, rv_ref, rg_ref, din_ref, qd_ref, kd_ref, cd_ref, ms_ref, o_ref, state_ref):
    @pl.when(pl.program_id(1) == 0)
    def _():
        state_ref[...] = jnp.zeros_like(state_ref)

    C = din_ref.shape[1]
    for c in range(rq_ref.shape[0] // C):
        rows = slice(c * C, (c + 1) * C)
        for h in range(RET_HEADS):
            q = rq_ref[rows, h * RET_DK:(h + 1) * RET_DK]
            k = rk_ref[rows, h * RET_DK:(h + 1) * RET_DK]
            v = rv_ref[rows, h * RET_DV:(h + 1) * RET_DV]
            r_prev = state_ref[h]
            s = lax.dot_general(q, k, (((1,), (1,)), ((), ())), preferred_element_type=F32) * din_ref[h]
            o = jnp.dot(s.astype(BF16), v, preferred_element_type=F32)
            o = o + jnp.dot(q, r_prev.astype(BF16), preferred_element_type=F32) * qd_ref[h]
            vd = (v.astype(F32) * kd_ref[h]).astype(BF16)
            kv = lax.dot_general(k, vd, (((0,), (0,)), ((), ())), preferred_element_type=F32)
            state_ref[h] = r_prev * cd_ref[h] + kv
            o = o * lax.rsqrt(jnp.mean(o * o, axis=-1, keepdims=True) + EPS)
            g = rg_ref[rows, h * RET_DV:(h + 1) * RET_DV].astype(F32)
            gate = g * (1.0 / (1.0 + jnp.exp(-g)))
            o_ref[rows, h * RET_DV:(h + 1) * RET_DV] = (
                gate * o * ms_ref[:, h * RET_DV:(h + 1) * RET_DV]).astype(BF16)


def _ret_consts(C):
    H = RET_HEADS
    log_g = np.log1p(-np.exp2(-5.0 - np.arange(H, dtype=np.float64)))
    pos = np.arange(C, dtype=np.float64)
    diff = pos[:, None] - pos[None, :]
    d_inner = np.where(diff[None] >= 0, np.exp(np.maximum(diff, 0.0)[None] * log_g[:, None, None]), 0.0)
    q_decay = np.exp((pos + 1.0)[None] * log_g[:, None])
    k_decay = np.exp((C - 1.0 - pos)[None] * log_g[:, None])
    chunk_decay = np.exp(C * log_g)
    qd = np.broadcast_to(q_decay[:, :, None], (H, C, RET_DV))
    kd = np.broadcast_to(k_decay[:, :, None], (H, C, RET_DV))
    cd = np.broadcast_to(chunk_decay[:, None, None], (H, 1, RET_DV))
    f = lambda a: jnp.asarray(np.ascontiguousarray(a), F32)
    return f(d_inner), f(qd), f(kd), f(cd)


def _retention(rq, rk, rv, rg, ms_ret):
    B, S, _ = rq.shape
    C = min(RET_CHUNK, S)
    din, qd, kd, cd = _ret_consts(C)
    rows = _tile(S, RET_CHUNKS_PER_STEP * C)
    row = lambda w: pl.BlockSpec((None, rows, w), lambda b, n: (b, n, 0))
    full = lambda a: pl.BlockSpec(a.shape, lambda b, n: (0,) * a.ndim)
    return pl.pallas_call(
        _ret_kernel,
        grid=(B, S // rows),
        in_specs=[row(256), row(256), row(512), row(512), full(din), full(qd), full(kd), full(cd),
                  pl.BlockSpec((1, RET_W), lambda b, n: (0, 0))],
        out_specs=row(RET_W),
        out_shape=jax.ShapeDtypeStruct((B, S, RET_W), BF16),
        scratch_shapes=[pltpu.VMEM((RET_HEADS, RET_DK, RET_DV), F32)],
        compiler_params=_cparams(("parallel", "arbitrary")),
        name="ret",
    )(rq, rk, rv, rg, din, qd, kd, cd, ms_ret)


def _f32_key(x):
    i = lax.bitcast_convert_type(x, I32)
    return i ^ ((i >> 31) & 0x7FFFFFFF)


def _key_f32(k):
    return lax.bitcast_convert_type(k ^ ((k >> 31) & 0x7FFFFFFF), F32)


def _floor16(x):
    i = lax.bitcast_convert_type(x, I32)
    return lax.bitcast_convert_type((i + ((i >> 31) & 0xFFFF)) & -0x10000, F32)


def _trunc16(x):
    return lax.bitcast_convert_type(lax.bitcast_convert_type(x, I32) & -0x10000, F32)


def _dsa_kernel(iqt_ref, iwt_ref, aqt_ref, ik_ref, ak_ref, avt_ref, ms_ref, o_ref, score_ref, code_ref, qa_ref, sa_ref, sb_ref, mask_ref, *acc_refs,
                tq, tks, n_sel):
    H, G, R, d = DSA_HEADS, DSA_KV_HEADS, DSA_HEADS // DSA_KV_HEADS, DSA_HD
    t0 = pl.program_id(1) * tq
    nsub = (t0 + tq) // tks
    tka = 2 * tks
    npair = ((t0 + tq) // tka + 1) // 2
    kf = float(n_sel)
    qpos = t0 + lax.broadcasted_iota(I32, (1, tq), 1)
    krow = lax.broadcasted_iota(I32, (tks, tq), 0)

    wrow = [iwt_ref[h:h + 1, :] for h in range(IDX_HEADS)]

    def fold8(x, op):
        acc = x[0:8, :]
        for i in range(1, tks // 8):
            acc = op(acc, x[8 * i:8 * (i + 1), :])
        return acc

    def score_pair(i, carry, masked):
        mx, mn, npos, nnon = carry
        for u in range(2 * tka // tks):
            r0 = pl.multiple_of(i * 2 * tka + u * tks, tks)
            kc = ik_ref[pl.ds(r0, tks), :]
            acc = jnp.zeros((tks, tq), F32)
            for h in range(IDX_HEADS):
                rel = jnp.dot(kc, iqt_ref[h * IDX_HD:(h + 1) * IDX_HD, :], preferred_element_type=F32)
                acc = acc + jnp.maximum(rel, 0.0) * wrow[h]
            if masked:
                causal = r0 + krow <= qpos
                sc = jnp.where(causal, acc, -jnp.inf)
                lowest = jnp.where(causal, acc, jnp.inf)
            else:
                sc = lowest = acc
            score_ref[pl.ds(r0, tks), :] = sc
            code_ref[pl.ds(r0, tks), :] = _trunc16(sc).astype(BF16)
            mx = jnp.maximum(mx, fold8(sc, jnp.maximum))
            mn = jnp.minimum(mn, fold8(lowest, jnp.minimum))
            npos = npos + fold8(jnp.where(sc > 0.0, 1.0, 0.0), jnp.add)
            nnon = nnon + fold8(jnp.where(sc >= 0.0, 1.0, 0.0), jnp.add)
        return mx, mn, npos, nnon

    stat0 = (jnp.full((8, tq), -jnp.inf, F32), jnp.full((8, tq), jnp.inf, F32),
             jnp.zeros((8, tq), F32), jnp.zeros((8, tq), F32))
    n_inner = (t0 + 1) // (2 * tka)
    stat = lax.fori_loop(0, n_inner, functools.partial(score_pair, masked=False), stat0)
    mx, mn, npos, nnon = lax.fori_loop(n_inner, npair, functools.partial(score_pair, masked=True), stat)
    top = jnp.max(mx, axis=0, keepdims=True)
    lo0 = jnp.min(mn, axis=0, keepdims=True)
    n_pos = jnp.sum(npos, axis=0, keepdims=True)
    n_nonneg = jnp.sum(nnon, axis=0, keepdims=True)


    def count(th, strict):
        def body(j, acc):
            for u in range(2):
                s = score_ref[pl.ds(pl.multiple_of((2 * j + u) * tks, tks), tks), :]
                hit = (s > th) if strict else (s >= th)
                acc = acc + fold8(jnp.where(hit, 1.0, 0.0), jnp.add)
            return acc

        acc = lax.fori_loop(0, nsub // 2, body, jnp.zeros((8, tq), F32))
        return jnp.sum(acc, axis=0, keepdims=True)

    def probe(lo, hi, it):
        lk, hk = _f32_key(lo), _f32_key(hi)
        mk = (lk >> 1) + (hk >> 1) + (lk & hk & 1)
        mv = lo + (hi - lo) * 0.5
        early = (jnp.zeros((1, tq), I32) + it) < BISECT_VALUE_STEPS
        mid = jnp.where(early & (mv > lo) & (mv < hi), mv, _key_f32(mk))
        return mid, jnp.max(jnp.where(mk != lk, 1.0, 0.0))

    def bis_cond(c):
        return (c[5] > 0.0) & (c[6] < BISECT_MAX_STEPS)

    def bis_body(c):
        lo, hi, c_lo, c_hi, mid, _, it = c
        cnt = count(mid, False)
        ge = cnt >= kf
        up = ge | (cnt == kf)
        dn = (~ge) | (cnt == kf)
        lo, c_lo = jnp.where(up, mid, lo), jnp.where(up, cnt, c_lo)
        hi, c_hi = jnp.where(dn, mid, hi), jnp.where(dn, cnt, c_hi)
        mid, active = probe(lo, hi, it + 1)
        return lo, hi, c_lo, c_hi, mid, active, it + 1

    zero = jnp.zeros((1, tq), F32)
    keep_all = qpos + 1 <= n_sel
    settled = keep_all | ((n_nonneg >= kf) & (n_pos < kf))
    above = n_pos >= kf
    c_lo0 = jnp.where(settled | above, n_nonneg, (qpos + 1).astype(F32))
    c_hi0 = jnp.where(settled | ~above, n_nonneg, zero)
    lo0 = jnp.where(settled | above, zero, lo0)
    hi0 = jnp.where(settled | ~above, zero, _key_f32(_f32_key(top) + 1))
    def count16(m):
        mb = m.astype(BF16)

        def body(j, acc):
            for u in range(2):
                c = code_ref[pl.ds(pl.multiple_of((2 * j + u) * tks, tks), tks), :]
                one = jnp.where(c >= mb, jnp.ones((), BF16), jnp.zeros((), BF16))
                part = one[0:16, :]
                for i in range(1, tks // 16):
                    part = part + one[16 * i:16 * (i + 1), :]
                acc = acc + part.astype(F32)
            return acc

        acc = lax.fori_loop(0, nsub // 2, body, jnp.zeros((16, tq), F32))
        return jnp.sum(acc, axis=0, keepdims=True)

    def coarse_step(_, c):
        lo, hi, c_lo, c_hi = c
        below = _floor16(lo + (hi - lo) * 0.5)
        above = _key_f32(_f32_key(below) + 0x10000)
        m = jnp.where(below > lo, below, above)
        v = jnp.where(m > 0.0, m, _key_f32(_f32_key(m) - 0x10000 + 1))
        ok = (v > lo) & (v < hi)
        cnt = count16(m)
        ge = cnt >= kf
        up = ok & ge
        dn = ok & ((~ge) | (cnt == kf))
        return (jnp.where(up, v, lo), jnp.where(dn, v, hi), jnp.where(up, cnt, c_lo), jnp.where(dn, cnt, c_hi))

    lo0, hi0, c_lo0, c_hi0 = lax.fori_loop(0, BISECT_COARSE_STEPS, coarse_step, (lo0, hi0, c_lo0, c_hi0))
    it0 = jnp.int32(BISECT_VALUE_STEPS)
    mid0, active0 = probe(lo0, hi0, it0)
    lo, hi, c_lo, c_hi, _, _, _ = lax.while_loop(bis_cond, bis_body,
                                                 (lo0, hi0, c_lo0, c_hi0, mid0, active0, it0))
    at_hi = c_hi >= kf
    thr = jnp.where(keep_all, F32_LOWEST, jnp.where(at_hi, hi, lo))
    excess = jnp.where(keep_all, 0.0, jnp.where(at_hi, c_hi, c_lo) - kf)

    @pl.when(jnp.max(excess) > 0.0)
    def _():
        tied_nonzero = jnp.max(jnp.where((excess > 0.0) & (thr != 0.0), 1.0, 0.0)) > 0.0
        n_above = lax.cond(tied_nonzero, lambda: count(thr, True), lambda: n_pos)
        budget = jnp.where(excess > 0.0, kf - jnp.where(thr == 0.0, n_pos, n_above), jnp.inf)
        earlier = lax.broadcasted_iota(I32, (tks, tks), 1) < lax.broadcasted_iota(I32, (tks, tks), 0)
        earlier = jnp.where(earlier, 1.0, 0.0).astype(BF16)

        def fix(j, seen):
            tiles = []
            for u in range(2):
                r0 = pl.multiple_of((2 * j + u) * tks, tks)
                s = score_ref[pl.ds(r0, tks), :]
                eq = s == thr
                eqf = jnp.where(eq, 1.0, 0.0)
                within = jnp.dot(earlier, eqf.astype(BF16), preferred_element_type=F32)
                tiles.append((r0, s, eq, within, jnp.sum(eqf, axis=0, keepdims=True)))
            for r0, s, eq, within, n_eq in tiles:
                score_ref[pl.ds(r0, tks), :] = jnp.where(eq & (within + seen >= budget), -jnp.inf, s)
                seen = seen + n_eq
            return seen

        lax.fori_loop(0, nsub // 2, fix, jnp.zeros((1, tq), F32))

    arow = lax.broadcasted_iota(I32, (KAUG - d, tq), 0)
    for h in range(H):
        terms, rest = [], 2.0 ** (-8.0 * (h + 1) / H) * LOG2E
        for _ in range(ALIBI_TERMS):
            terms.append(float(np.asarray(rest, np.float32).astype(BF16).astype(np.float64)))
            rest -= terms[-1]
        rows = jnp.zeros((KAUG - d, tq), F32)
        for i, c in enumerate(terms):
            rows = jnp.where(arow == i, c * ALIBI_SPLIT, jnp.where(arow == ALIBI_TERMS + i, c, rows))
        qa_ref[h, 0:d, :] = aqt_ref[h * d:(h + 1) * d, :]
        qa_ref[h, d:KAUG, :] = rows.astype(BF16)
    for acc in acc_refs:
        acc[...] = jnp.zeros_like(acc)

    def logits(j, h):
        ka = ak_ref[pl.ds(pl.multiple_of(j * tka, tka), tka), (h // R) * KAUG:(h // R + 1) * KAUG]
        return jnp.dot(ka, qa_ref[h], preferred_element_type=F32)

    def step(j, j_next, cur_ref, next_ref, ms):
        r0 = pl.multiple_of(j * tka, tka)
        mask_ref[...] = jnp.where(score_ref[pl.ds(r0, tka), :] >= thr, 0.0, NEG_BIG)
        nms = []
        ahead = logits(j_next, 0)
        for h in range(H):
            g = h // R
            next_ref[h] = ahead
            if h + 1 < H:
                ahead = logits(j_next, h + 1)
            s = cur_ref[h] + mask_ref[...]
            m_new = jnp.maximum(ms[h], jnp.max(s, axis=0, keepdims=True))
            p = jnp.exp2(s - m_new).astype(BF16)
            va = avt_ref[g * VAUG:(g + 1) * VAUG, pl.ds(r0, tka)]
            acc = acc_refs[h]
            acc[...] = acc[...] * jnp.exp2(ms[h] - m_new) + jnp.dot(va, p, preferred_element_type=F32)
            nms.append(m_new)
        return tuple(nms)

    for h in range(H):
        sa_ref[h] = logits(0, h)

    def att_pair(i, ms):
        ms = step(2 * i, 2 * i + 1, sa_ref, sb_ref, ms)
        return step(2 * i + 1, jnp.minimum(2 * i + 2, 2 * npair - 1), sb_ref, sa_ref, ms)

    lax.fori_loop(0, npair, att_pair, tuple(jnp.full((1, tq), NEG_BIG, F32) for _ in range(H)))
    for h in range(H):
        a = acc_refs[h][...]
        o = a[0:d, :] / a[d:d + 1, :]
        o_ref[h * d:(h + 1) * d, :] = (o * ms_ref[h * d:(h + 1) * d, :]).astype(BF16)


def _dsa(iqt, iwt, aqt, ik, ak, avt, ms_att, tq, tks):
    B, _, S = iqt.shape
    assert S <= ALIBI_SPLIT * 256 and tq % (2 * tks) == 0
    assert (S // (2 * tks)) % 2 == 0
    n_sel = min(TOPK_MAX, S // 4)
    G = DSA_KV_HEADS
    colT = lambda h: pl.BlockSpec((None, h, tq), lambda b, i: (b, 0, i))
    msb = jnp.broadcast_to(ms_att.reshape(DSA_W, 1), (DSA_W, tq))
    return pl.pallas_call(
        functools.partial(_dsa_kernel, tq=tq, tks=tks, n_sel=n_sel),
        grid=(B, S // tq),
        in_specs=[colT(IDX_HEADS * IDX_HD), colT(IDX_HEADS), colT(DSA_W),
                  pl.BlockSpec((None, S, IDX_HD), lambda b, i: (b, 0, 0), pipeline_mode=pl.Buffered(1)),
                  pl.BlockSpec((None, S, G * KAUG), lambda b, i: (b, 0, 0), pipeline_mode=pl.Buffered(1)),
                  pl.BlockSpec((None, G * VAUG, S), lambda b, i: (b, 0, 0), pipeline_mode=pl.Buffered(1)),
                  pl.BlockSpec((DSA_W, tq), lambda b, i: (0, 0))],
        out_specs=colT(DSA_W),
        out_shape=jax.ShapeDtypeStruct((B, DSA_W, S), BF16),
        scratch_shapes=[pltpu.VMEM((S, tq), F32), pltpu.VMEM((S, tq), BF16), pltpu.VMEM((DSA_HEADS, KAUG, tq), BF16),
                        pltpu.VMEM((DSA_HEADS, 2 * tks, tq), F32), pltpu.VMEM((DSA_HEADS, 2 * tks, tq), F32),
                        pltpu.VMEM((2 * tks, tq), F32)]
        + [pltpu.VMEM((VAUG, tq), F32) for _ in range(DSA_HEADS)],
        compiler_params=_cparams(("parallel", "arbitrary")),
        name="dsa",
    )(iqt, iwt, aqt, ik, ak, avt, msb)


def _oproj_kernel(ret_ref, att_ref, x_ref, wo_ref, g1_ref, n2_ref, sc_ref, sh_ref, rw_ref, rb_ref,
                  x1_ref, h2_ref, sel_ref, idx_ref, gate_ref, cnt_ref):
    mixo = jnp.dot(ret_ref[...], wo_ref[:RET_W, :], preferred_element_type=F32)
    mixo = mixo + lax.dot_general(att_ref[...], wo_ref[RET_W:, :], (((0,), (0,)), ((), ())),
                                  preferred_element_type=F32)
    x1 = x_ref[...] + g1_ref[...] * mixo
    x1_ref[...] = x1
    y = x1 * lax.rsqrt(jnp.mean(x1 * x1, axis=-1, keepdims=True) + EPS) * n2_ref[...]
    h2 = y * (1.0 + sc_ref[...]) + sh_ref[...]
    _store_token_major(h2_ref, h2)
    h_hi = h2.astype(BF16)
    h_lo = (h2 - h_hi.astype(F32)).astype(BF16)
    logits = jnp.dot(h_hi, rw_ref[0], preferred_element_type=F32)
    logits = logits + (jnp.dot(h_hi, rw_ref[1], preferred_element_type=F32)
                       + jnp.dot(h_lo, rw_ref[0], preferred_element_type=F32)) + rb_ref[...]
    tm = logits.shape[0]
    lane = lax.broadcasted_iota(I32, (tm, LANES), 1).astype(F32)
    work = jnp.where(lane < N_EXPERTS, logits, -jnp.inf)
    sel = jnp.zeros((tm, LANES), F32)
    idx_tab = jnp.zeros((tm, LANES), F32)
    vals = []
    for k in range(TOP_K):
        m = jnp.max(work, axis=1, keepdims=True)
        idx = jnp.min(jnp.where(work == m, lane, float(LANES)), axis=1, keepdims=True)
        hit = lane == idx
        sel = jnp.where(hit, 1.0, sel)
        idx_tab = jnp.where(lane == k, idx, idx_tab)
        work = jnp.where(hit, -jnp.inf, work)
        vals.append(m)
    es = [jnp.exp(v - vals[0]) for v in vals]
    den = es[0] + es[1] + es[2] + es[3]
    gate_tab = jnp.zeros((tm, LANES), F32)
    for k in range(TOP_K):
        gate_tab = jnp.where(lane == k, es[k] / den, gate_tab)
    sel_ref[...] = sel
    idx_ref[...] = idx_tab
    gate_ref[...] = gate_tab

    @pl.when((pl.program_id(0) == 0) & (pl.program_id(1) == 0))
    def _():
        cnt_ref[...] = jnp.zeros_like(cnt_ref)

    cnt_ref[...] += jnp.sum(sel, axis=0, keepdims=True)


def _oproj(ret, att, x, wo, g1, n2g, sc2, sh2, rw_pad, rb_pad, tm):
    B, S, D = x.shape
    nt = S // tm
    row = lambda w: pl.BlockSpec((None, tm, w), lambda b, i: (b, i, 0))
    flat = lambda w: pl.BlockSpec((tm, w), lambda b, i: (b * nt + i, 0))
    vec = pl.BlockSpec((None, 1, D), lambda b, i: (b, 0, 0))
    cst = lambda shape: pl.BlockSpec(shape, lambda b, i: (0, 0))
    sd = lambda shape, dt: jax.ShapeDtypeStruct(shape, dt)
    N = B * S
    return pl.pallas_call(
        _oproj_kernel,
        grid=(B, nt),
        in_specs=[row(RET_W), pl.BlockSpec((None, DSA_W, tm), lambda b, i: (b, 0, i)), row(D), cst((D, D)), vec,
                  cst((1, D)), vec, vec,
                  pl.BlockSpec((2, D, LANES), lambda b, i: (0, 0, 0)), cst((1, LANES))],
        out_specs=[flat(D), pl.BlockSpec((tm * (D // LANES), LANES), lambda b, i: (b * nt + i, 0)),
                   flat(LANES), flat(LANES), flat(LANES), cst((1, LANES))],
        out_shape=[sd((N, D), F32), sd((N * (D // LANES), LANES), F32), sd((N, LANES), F32), sd((N, LANES), F32),
                   sd((N, LANES), F32), sd((1, LANES), F32)],
        compiler_params=_cparams(("arbitrary", "arbitrary")),
        name="oproj",
    )(ret, att, x, wo, g1, n2g.reshape(1, D), sc2, sh2, rw_pad, rb_pad)


def _dest_kernel(sel_ref, idx_ref, pstart_ref, dest_ref, seen_ref):
    @pl.when(pl.program_id(0) == 0)
    def _():
        seen_ref[...] = jnp.zeros_like(seen_ref)

    sel = sel_ref[...]
    tm = sel.shape[0]
    earlier = lax.broadcasted_iota(I32, (tm, tm), 1) < lax.broadcasted_iota(I32, (tm, tm), 0)
    earlier = jnp.where(earlier, 1.0, 0.0).astype(BF16)
    rank = jnp.dot(earlier, sel.astype(BF16), preferred_element_type=F32) + seen_ref[...]
    dest = pstart_ref[...] + rank
    lane = lax.broadcasted_iota(I32, (tm, LANES), 1).astype(F32)
    idx_tab = idx_ref[...]
    out = jnp.zeros((tm, LANES), F32)
    for k in range(TOP_K):
        e_k = jnp.sum(jnp.where(lane == k, idx_tab, 0.0), axis=1, keepdims=True)
        d_k = jnp.sum(jnp.where(lane == e_k, dest, 0.0), axis=1, keepdims=True)
        out = jnp.where(lane == k, d_k, out)
    dest_ref[...] = out.astype(I32)
    seen_ref[...] += jnp.sum(sel, axis=0, keepdims=True)


def _dest(sel, idx_tab, pstart, tm):
    N = sel.shape[0]
    blk = pl.BlockSpec((tm, LANES), lambda i: (i, 0))
    return pl.pallas_call(
        _dest_kernel,
        grid=(N // tm,),
        in_specs=[blk, blk, pl.BlockSpec((1, LANES), lambda i: (0, 0))],
        out_specs=blk,
        out_shape=jax.ShapeDtypeStruct((N, LANES), I32),
        scratch_shapes=[pltpu.VMEM((1, LANES), F32)],
        compiler_params=_cparams(("arbitrary",)),
        name="dest",
    )(sel, idx_tab, pstart)


def _disp_kernel(zs_ref, h2_ref, dest_hbm, xs_hbm, zbuf, idx_a, idx_b, sem_ia, sem_ib, sem_row, sem_z,
                 *, tm, nch, zrows, n_steps):
    i = pl.program_id(0)
    n = tm * TOP_K

    def idx_copy(tile, buf, sem):
        return pltpu.make_async_copy(dest_hbm.at[pl.ds(tile * n, n)], buf, sem)

    def rows(ref, first, count):
        return ref.at[pl.ds(pl.multiple_of(first * nch, nch), count * nch)]

    def row_copy(row, dst):
        return pltpu.make_async_copy(rows(h2_ref, row, 1), rows(xs_hbm, dst, 1), sem_row)

    def issue_rows(first_row, idx):
        def body(r, c):
            for s in range(8):
                t = r * 8 + s
                for k in range(TOP_K):
                    row_copy(first_row + t, idx[t * TOP_K + k]).start(priority=k % 2)
            return c

        lax.fori_loop(0, tm // 8, body, 0)

    def wait_rows():
        pltpu.make_async_copy(rows(xs_hbm, 0, n), rows(xs_hbm, 0, n), sem_row).wait()

    @pl.when(i == 0)
    def _():
        zbuf[...] = jnp.zeros_like(zbuf)
        for e in range(N_EXPERTS):
            fill = pltpu.make_async_copy(zbuf, rows(xs_hbm, zs_ref[e], zrows), sem_z)
            fill.start()
            fill.wait()
        idx_copy(0, idx_a, sem_ia).start()
        idx_copy(1, idx_b, sem_ib).start()

    idx_copy(2 * i, idx_a, sem_ia).wait()
    issue_rows(0, idx_a)
    idx_copy(2 * i + 1, idx_b, sem_ib).wait()
    issue_rows(tm, idx_b)

    @pl.when(i + 1 < n_steps)
    def _():
        idx_copy(2 * i + 2, idx_a, sem_ia).start()
        idx_copy(2 * i + 3, idx_b, sem_ib).start()

    wait_rows()
    wait_rows()


def _dispatch(h2, dest_flat, zero_start, n_rows, zrows, tm):
    nch = D_MODEL // LANES
    n_steps = h2.shape[0] // nch // (2 * tm)
    grid_spec = pltpu.PrefetchScalarGridSpec(
        num_scalar_prefetch=1,
        grid=(n_steps,),
        in_specs=[pl.BlockSpec((2 * tm * nch, LANES), lambda i, zs: (i, 0)), pl.BlockSpec(memory_space=pl.ANY)],
        out_specs=pl.BlockSpec(memory_space=pl.ANY),
        scratch_shapes=[pltpu.VMEM((zrows * nch, LANES), F32), pltpu.SMEM((tm * TOP_K,), I32),
                        pltpu.SMEM((tm * TOP_K,), I32), pltpu.SemaphoreType.DMA, pltpu.SemaphoreType.DMA,
                        pltpu.SemaphoreType.DMA, pltpu.SemaphoreType.DMA],
    )
    return pl.pallas_call(
        functools.partial(_disp_kernel, tm=tm, nch=nch, zrows=zrows, n_steps=n_steps),
        grid_spec=grid_spec,
        out_shape=jax.ShapeDtypeStruct((n_rows * nch, LANES), F32),
        compiler_params=_cparams(("arbitrary",)),
        name="disp",
    )(zero_start, h2, dest_flat)


def _store_token_major(ref, x):
    rows, d = x.shape
    nch = d // LANES
    for j in range(nch):
        ref[pl.ds(j, rows, stride=nch), :] = x[:, j * LANES:(j + 1) * LANES]


def _load_token_major(ref, rows, nch):
    return jnp.concatenate([ref[pl.ds(j, rows, stride=nch), :] for j in range(nch)], axis=1)


def _ffn_kernel(be_ref, nb_ref, nxt_ref, xs_ref, wgu_hbm, bgu_ref, wd_hbm, bd_ref, ys_ref,
                wgu_st, wd_st, wgu_bf, wd_bf, sem_gu, sem_d):
    i = pl.program_id(0)
    live = i < nb_ref[0]
    e = be_ref[i]

    def fetch(ex):
        return (pltpu.make_async_copy(wgu_hbm.at[ex], wgu_st, sem_gu),
                pltpu.make_async_copy(wd_hbm.at[ex], wd_st, sem_d))

    @pl.when(i == 0)
    def _():
        for cp in fetch(e):
            cp.start()

    @pl.when(live & ((i == 0) | (e != be_ref[jnp.maximum(i - 1, 0)])))
    def _():
        for cp in fetch(e):
            cp.wait()
        wgu_bf[...] = wgu_st[...].astype(BF16)
        wd_bf[...] = wd_st[...].astype(BF16)

        @pl.when(nxt_ref[i] >= 0)
        def _():
            for cp in fetch(nxt_ref[i]):
                cp.start()

    @pl.when(live)
    def _():
        xb = _load_token_major(xs_ref, ys_ref.shape[0] // NCH, NCH).astype(BF16)
        gu = jnp.dot(xb, wgu_bf[...], preferred_element_type=F32) + bgu_ref[...]
        gate = jnp.minimum(gu[:, :D_EXPERT], SWIGLU_LIMIT)
        up = jnp.clip(gu[:, D_EXPERT:], -SWIGLU_LIMIT, SWIGLU_LIMIT)
        glu = gate * (1.0 / (1.0 + jnp.exp(-SWIGLU_ALPHA * gate)))
        act = ((up + 1.0) * glu).astype(BF16)
        ys = jnp.dot(act, wd_bf[...], preferred_element_type=F32) + bd_ref[...]
        _store_token_major(ys_ref, ys)

    @pl.when(jnp.logical_not(live))
    def _():
        ys_ref[...] = jnp.zeros_like(ys_ref)


def _ffn(xs, block_e, n_blocks, next_e, w_gu, b_gu, w_down, b_down, tmb):
    D = D_MODEL
    P = xs.shape[0] // NCH
    E = w_gu.shape[0]
    blk = lambda i, be, nb, nx: (jnp.minimum(i, nb[0] - 1), 0)
    wsel = lambda i, be, nb, nx: (be[jnp.minimum(i, nb[0] - 1)], 0, 0)
    grid_spec = pltpu.PrefetchScalarGridSpec(
        num_scalar_prefetch=3,
        grid=(P // tmb,),
        in_specs=[pl.BlockSpec((tmb * NCH, LANES), blk),
                  pl.BlockSpec(memory_space=pl.ANY),
                  pl.BlockSpec((None, 1, 2 * D_EXPERT), wsel),
                  pl.BlockSpec(memory_space=pl.ANY),
                  pl.BlockSpec((None, 1, D), wsel)],
        out_specs=pl.BlockSpec((tmb * (D // LANES), LANES), lambda i, be, nb, nx: (i, 0)),
        scratch_shapes=[pltpu.VMEM((D, 2 * D_EXPERT), F32), pltpu.VMEM((D_EXPERT, D), F32),
                        pltpu.VMEM((D, 2 * D_EXPERT), BF16), pltpu.VMEM((D_EXPERT, D), BF16),
                        pltpu.SemaphoreType.DMA, pltpu.SemaphoreType.DMA],
    )
    return pl.pallas_call(
        _ffn_kernel,
        grid_spec=grid_spec,
        out_shape=jax.ShapeDtypeStruct((P * (D // LANES), LANES), F32),
        compiler_params=_cparams(("arbitrary",)),
        name="ffn",
    )(block_e, n_blocks, next_e, xs, w_gu, b_gu.reshape(E, 1, 2 * D_EXPERT), w_down, b_down.reshape(E, 1, D))


def _comb_kernel(x1_ref, gate_ref, g2_ref, fg_ref, dest_hbm, ys_hbm, o_ref, buf_a, buf_b, idx_a, idx_b,
                 sem_ia, sem_ib, sem_ra, sem_rb, *, tm, n_steps):
    i = pl.program_id(0)
    n = tm * TOP_K

    def idx_copy(tile, buf, sem):
        return pltpu.make_async_copy(dest_hbm.at[pl.ds(tile * n, n)], buf, sem)

    nch = x1_ref.shape[1] // LANES

    def row_copy(src, buf, k, t, sem):
        return pltpu.make_async_copy(ys_hbm.at[pl.ds(pl.multiple_of(src * nch, nch), nch)],
                                     buf.at[k, t[0], :, t[1]], sem)

    def issue_rows(idx, buf, sem):
        def body(r, c):
            for s in range(8):
                for k in range(TOP_K):
                    row_copy(idx[(r * 8 + s) * TOP_K + k], buf, k, (r, s), sem).start(priority=k % 2)
            return c

        lax.fori_loop(0, tm // 8, body, 0)

    def wait_rows(buf, sem):
        pltpu.make_async_copy(buf, buf, sem).wait()

    def finish(buf, lo):
        gates = gate_ref[lo:lo + tm, :]
        y = jnp.zeros((tm, x1_ref.shape[1]), F32)
        for k in range(TOP_K):
            rows = jnp.concatenate([buf[k, :, j].reshape(tm, LANES) for j in range(nch)], axis=1)
            y = y + gates[:, k:k + 1] * rows
        v = x1_ref[lo:lo + tm, :] + g2_ref[...] * y
        o_ref[lo:lo + tm, :] = v * lax.rsqrt(jnp.mean(v * v, axis=-1, keepdims=True) + EPS) * fg_ref[...]

    @pl.when(i == 0)
    def _():
        first = idx_copy(0, idx_a, sem_ia)
        first.start()
        first.wait()
        issue_rows(idx_a, buf_a, sem_ra)
        idx_copy(1, idx_b, sem_ib).start()

    idx_copy(2 * i + 1, idx_b, sem_ib).wait()
    issue_rows(idx_b, buf_b, sem_rb)

    @pl.when(i + 1 < n_steps)
    def _():
        idx_copy(2 * i + 2, idx_a, sem_ia).start()

    wait_rows(buf_a, sem_ra)
    finish(buf_a, 0)

    @pl.when(i + 1 < n_steps)
    def _():
        idx_copy(2 * i + 2, idx_a, sem_ia).wait()
        issue_rows(idx_a, buf_a, sem_ra)
        idx_copy(2 * i + 3, idx_b, sem_ib).start()

    wait_rows(buf_b, sem_rb)
    finish(buf_b, tm)


def _combine(x1, gate_tab, g2, final_g, dest_flat, ys, S, tm):
    N, D = x1.shape
    n_steps = N // (2 * tm)
    per_b = S // (2 * tm)
    return pl.pallas_call(
        functools.partial(_comb_kernel, tm=tm, n_steps=n_steps),
        grid=(n_steps,),
        in_specs=[pl.BlockSpec((2 * tm, D), lambda i: (i, 0)),
                  pl.BlockSpec((2 * tm, LANES), lambda i: (i, 0)),
                  pl.BlockSpec((None, 1, D), lambda i: (i // per_b, 0, 0)),
                  pl.BlockSpec((1, D), lambda i: (0, 0)),
                  pl.BlockSpec(memory_space=pl.ANY),
                  pl.BlockSpec(memory_space=pl.ANY)],
        out_specs=pl.BlockSpec((2 * tm, D), lambda i: (i, 0)),
        out_shape=jax.ShapeDtypeStruct((N, D), F32),
        scratch_shapes=[pltpu.VMEM((TOP_K, tm // 8, D // LANES, 8, LANES), F32),
                        pltpu.VMEM((TOP_K, tm // 8, D // LANES, 8, LANES), F32),
                        pltpu.SMEM((tm * TOP_K,), I32), pltpu.SMEM((tm * TOP_K,), I32),
                        pltpu.SemaphoreType.DMA, pltpu.SemaphoreType.DMA, pltpu.SemaphoreType.DMA,
                        pltpu.SemaphoreType.DMA],
        compiler_params=_cparams(("arbitrary",)),
        name="comb",
    )(x1, gate_tab, g2, final_g.reshape(1, D), dest_flat, ys)


def _tile(n, pref):
    t = min(pref, n)
    assert n % t == 0, (n, t)
    return t


def _layer(x, c, ada_w, ada_b, norm1_g, w_in, mix_scale, w_o, norm2_g,
           router_w, router_b, w_gu, b_gu, w_down, b_down, final_g):
    B, S, D = x.shape
    N = B * S
    mod = _mod(c, ada_w, ada_b).reshape(B, 6, 1, D)
    sh1, sc1, g1, sh2, sc2, g2 = (mod[:, j] for j in range(6))

    w_pad = jnp.pad(w_in, ((0, 0), (0, IN_COLS_PAD - IN_COLS))).astype(BF16)
    rq, rk, rv, rg, aqt, ak, avt, iqt, ik, iwt = _inproj(x, norm1_g, sc1, sh1, w_pad, _tile(S, PROJ_ROWS))
    ms = mix_scale.reshape(1, RET_W + DSA_W)
    ret = _retention(rq, rk, rv, rg, ms[:, :RET_W])
    att = _dsa(iqt, iwt, aqt, ik, ak, avt, ms[:, RET_W:], _tile(S, DSA_QUERIES), DSA_KEYS)

    rw_pad = jnp.pad(router_w, ((0, 0), (0, LANES - N_EXPERTS)))
    rw_hi = rw_pad.astype(BF16)
    rw_pad = jnp.stack([rw_hi, (rw_pad - rw_hi.astype(F32)).astype(BF16)])
    rb_pad = jnp.pad(router_b, (0, LANES - N_EXPERTS)).reshape(1, LANES)
    x1, h2, sel, idx_tab, gate_tab, counts = _oproj(ret, att, x, w_o.astype(BF16), g1, norm2_g, sc2, sh2,
                                                    rw_pad, rb_pad, _tile(S, PROJ_ROWS))

    tmb = FFN_ROWS
    n_rows = (N * TOP_K + N_EXPERTS * (tmb - 1)) // tmb * tmb + tmb
    cnt = counts[0, :N_EXPERTS].astype(I32)
    padded = (cnt + tmb - 1) // tmb * tmb
    ends = jnp.cumsum(padded)
    starts = ends - padded
    pstart = jnp.pad(starts.astype(F32), (0, LANES - N_EXPERTS)).reshape(1, LANES)
    n_blocks = (ends[-1] // tmb).reshape(1)
    first_row = jnp.arange(n_rows // tmb, dtype=I32) * tmb
    block_e = jnp.minimum(jnp.sum((ends[None, :] <= first_row[:, None]).astype(I32), axis=1), N_EXPERTS - 1)

    tmd = _tile(N, MOE_TOKENS)
    dest_tab = _dest(sel, idx_tab, pstart, tmd)
    dest_flat = dest_tab[:, :TOP_K].reshape(N * TOP_K)
    xs = _dispatch(h2, dest_flat, starts + cnt, n_rows, tmb, tmd)
    eid = jnp.arange(N_EXPERTS, dtype=I32)
    later_used = (eid[None, :] > eid[:, None]) & (padded[None, :] > 0)
    next_used = jnp.min(jnp.where(later_used, eid[None, :], N_EXPERTS), axis=1)
    next_e = jnp.where(next_used < N_EXPERTS, next_used, -1)[block_e].astype(I32)
    ys = _ffn(xs, block_e, n_blocks, next_e, w_gu, b_gu, w_down, b_down, tmb)
    out = _combine(x1, gate_tab, g2, final_g, dest_flat, ys, S, _tile(S, MOE_TOKENS))
    return out.reshape(B, S, D)


def kernel(x, c, ada_w, ada_b, norm1_g, w_in, mix_scale, w_o, norm2_g, router_w, router_b, w_gu, b_gu,
           w_down, b_down, final_g):
    assert ada_w.shape[0] == 1, "single-layer stack"
    return _layer(x, c, ada_w[0], ada_b[0], norm1_g[0], w_in[0], mix_scale[0], w_o[0], norm2_g[0],
                  router_w[0], router_b[0], w_gu[0], b_gu[0], w_down[0], b_down[0], final_g)
```

```python
import functools

import numpy as np
import jax
import jax.numpy as jnp
from jax import lax
from jax.experimental import pallas as pl
from jax.experimental.pallas import tpu as pltpu

F32 = jnp.float32
BF16 = jnp.bfloat16
I32 = jnp.int32

D_MODEL = 1024
RET_HEADS = 4
RET_DK = 64
RET_DV = 128
RET_CHUNK = 128
DSA_HEADS = 8
DSA_KV_HEADS = 2
DSA_HD = 64
IDX_HEADS = 8
IDX_HD = 64
TOPK_MAX = 256
N_EXPERTS = 32
TOP_K = 4
D_EXPERT = D_MODEL
SWIGLU_LIMIT = 7.0
SWIGLU_ALPHA = 1.702
EPS = 1e-6

RET_W = RET_HEADS * RET_DV
DSA_W = DSA_HEADS * DSA_HD
IN_COLS = 2888
IN_COLS_PAD = 2944

KAUG = 128
VAUG = 80
ALIBI_SPLIT = 64
ALIBI_TERMS = 3
LOG2E = 1.4426950408889634
PROJ_ROWS = 512
DSA_QUERIES = 512
DSA_KEYS = 128
MOE_TOKENS = 256
RET_CHUNKS_PER_STEP = 4
FFN_ROWS = 512
FIX_UNROLL = 4
BISECT_COARSE_STEPS = 8
BISECT_VALUE_STEPS = 8
BISECT_MAX_STEPS = 64

LANES = 128
NCH = D_MODEL // LANES
VMEM_LIMIT = 56 * 1024 * 1024
NEG_BIG = -1e30
F32_LOWEST = float(np.finfo(np.float32).min)


def _cparams(sem):
    return pltpu.CompilerParams(dimension_semantics=sem, vmem_limit_bytes=VMEM_LIMIT)


def _mod_kernel(c_ref, w_ref, b_ref, o_ref):
    c = c_ref[...]
    s = c * (1.0 / (1.0 + jnp.exp(-c)))
    o_ref[...] = jnp.dot(s, w_ref[...], preferred_element_type=F32,
                         precision=lax.Precision.HIGHEST) + b_ref[...]


def _mod(c, ada_w, ada_b):
    B, D = c.shape
    n_out = ada_w.shape[1]
    rows = 8
    c8 = jnp.zeros((rows, D), F32).at[:B].set(c)
    out = pl.pallas_call(
        _mod_kernel,
        grid=(n_out // D,),
        in_specs=[pl.BlockSpec((rows, D), lambda j: (0, 0)),
                  pl.BlockSpec((D, D), lambda j: (0, j)),
                  pl.BlockSpec((1, D), lambda j: (0, j))],
        out_specs=pl.BlockSpec((rows, D), lambda j: (0, j)),
        out_shape=jax.ShapeDtypeStruct((rows, n_out), F32),
        compiler_params=_cparams(("arbitrary",)),
        name="mod",
    )(c8, ada_w, ada_b.reshape(1, n_out))
    return out[:B]


def _inproj_kernel(x_ref, g_ref, sc_ref, sh_ref, w_ref,
                   rq_ref, rk_ref, rv_ref, rg_ref, aqt_ref, ak_ref, avt_ref, iqt_ref, ik_ref, iwt_ref):
    x = x_ref[...]
    ms = jnp.mean(x * x, axis=-1, keepdims=True)
    y = x * lax.rsqrt(ms + EPS) * g_ref[...]
    hb = (y * (1.0 + sc_ref[...]) + sh_ref[...]).astype(BF16)

    def proj(lo, hi):
        return jnp.dot(hb, w_ref[:, lo:hi], preferred_element_type=F32)

    tm = x.shape[0]
    d = DSA_HD
    rq_ref[...] = proj(0, 256).astype(BF16)
    rk_ref[...] = (proj(256, 512) * (RET_DK ** -0.5)).astype(BF16)
    rv_ref[...] = proj(512, 1024).astype(BF16)
    rg_ref[...] = proj(1024, 1536).astype(BF16)
    aqt_ref[...] = (proj(1536, 2048) * (d ** -0.5 * LOG2E)).T.astype(BF16)
    kk = proj(2048, 2176)
    pos = pl.program_id(1) * tm + lax.broadcasted_iota(I32, (tm, d), 0)
    col = lax.broadcasted_iota(I32, (tm, d), 1)
    posblk = jnp.where(col < ALIBI_TERMS, pos // ALIBI_SPLIT,
                       jnp.where(col < 2 * ALIBI_TERMS, pos % ALIBI_SPLIT, 0)).astype(F32)
    for g in range(DSA_KV_HEADS):
        ak_ref[:, g * KAUG:g * KAUG + d] = kk[:, g * d:(g + 1) * d].astype(BF16)
        ak_ref[:, g * KAUG + d:(g + 1) * KAUG] = posblk.astype(BF16)
    vt = proj(2176, 2304).T
    r16 = lax.broadcasted_iota(I32, (VAUG - d, tm), 0)
    onesblk = jnp.where(r16 == 0, 1.0, 0.0).astype(BF16)
    for g in range(DSA_KV_HEADS):
        avt_ref[g * VAUG:g * VAUG + d, :] = vt[g * d:(g + 1) * d, :].astype(BF16)
        avt_ref[g * VAUG + d:(g + 1) * VAUG, :] = onesblk
    iqt_ref[...] = proj(2304, 2816).T.astype(BF16)
    last = proj(2816, 2944)
    ik_ref[...] = last[:, :IDX_HD].astype(BF16)
    iwt_ref[...] = last.T[IDX_HD:IDX_HD + IDX_HEADS, :] * ((IDX_HD ** -0.5) * (IDX_HEADS ** -0.5))


def _inproj(x, norm_g, sc, sh, w_pad, tm):
    B, S, D = x.shape
    row = lambda w: pl.BlockSpec((None, tm, w), lambda b, i: (b, i, 0))
    colT = lambda h: pl.BlockSpec((None, h, tm), lambda b, i: (b, 0, i))
    vec = pl.BlockSpec((None, 1, D), lambda b, i: (b, 0, 0))
    sd = lambda shape, dt: jax.ShapeDtypeStruct(shape, dt)
    G = DSA_KV_HEADS
    return pl.pallas_call(
        _inproj_kernel,
        grid=(B, S // tm),
        in_specs=[row(D), pl.BlockSpec((1, D), lambda b, i: (0, 0)), vec, vec,
                  pl.BlockSpec((D, IN_COLS_PAD), lambda b, i: (0, 0))],
        out_specs=[row(256), row(256), row(512), row(512), colT(DSA_W), row(G * KAUG), colT(G * VAUG),
                   colT(IDX_HEADS * IDX_HD), row(IDX_HD), colT(IDX_HEADS)],
        out_shape=[sd((B, S, 256), BF16), sd((B, S, 256), BF16), sd((B, S, 512), BF16),
                   sd((B, S, 512), BF16), sd((B, DSA_W, S), BF16), sd((B, S, G * KAUG), BF16),
                   sd((B, G * VAUG, S), BF16), sd((B, IDX_HEADS * IDX_HD, S), BF16),
                   sd((B, S, IDX_HD), BF16), sd((B, IDX_HEADS, S), F32)],
        compiler_params=_cparams(("parallel", "parallel")),
        name="inproj",
    )(x, norm_g.reshape(1, D), sc, sh, w_pad)


def _ret_kernel(rq_ref, rk_ref, rv_ref, rg_ref, din_ref, qd_ref, kd_ref, cd_ref, ms_ref, o_ref, state_ref):
    @pl.when(pl.program_id(1) == 0)
    def _():
        state_ref[...] = jnp.zeros_like(state_ref)

    C = din_ref.shape[1]
    for c in range(rq_ref.shape[0] // C):
        rows = slice(c * C, (c + 1) * C)
        for h in range(RET_HEADS):
            q = rq_ref[rows, h * RET_DK:(h + 1) * RET_DK]
            k = rk_ref[rows, h * RET_DK:(h + 1) * RET_DK]
            v = rv_ref[rows, h * RET_DV:(h + 1) * RET_DV]
            r_prev = state_ref[h]
            s = lax.dot_general(q, k, (((1,), (1,)), ((), ())), preferred_element_type=F32) * din_ref[h]
            o = jnp.dot(s.astype(BF16), v, preferred_element_type=F32)
            o = o + jnp.dot(q, r_prev.astype(BF16), preferred_element_type=F32) * qd_ref[h]
            vd = (v.astype(F32) * kd_ref[h]).astype(BF16)
            kv = lax.dot_general(k, vd, (((0,), (0,)), ((), ())), preferred_element_type=F32)
            state_ref[h] = r_prev * cd_ref[h] + kv
            o = o * lax.rsqrt(jnp.mean(o * o, axis=-1, keepdims=True) + EPS)
            g = rg_ref[rows, h * RET_DV:(h + 1) * RET_DV].astype(F32)
            gate = g * (1.0 / (1.0 + jnp.exp(-g)))
            o_ref[rows, h * RET_DV:(h + 1) * RET_DV] = (
                gate * o * ms_ref[:, h * RET_DV:(h + 1) * RET_DV]).astype(BF16)


def _ret_consts(C):
    H = RET_HEADS
    log_g = np.log1p(-np.exp2(-5.0 - np.arange(H, dtype=np.float64)))
    pos = np.arange(C, dtype=np.float64)
    diff = pos[:, None] - pos[None, :]
    d_inner = np.where(diff[None] >= 0, np.exp(np.maximum(diff, 0.0)[None] * log_g[:, None, None]), 0.0)
    q_decay = np.exp((pos + 1.0)[None] * log_g[:, None])
    k_decay = np.exp((C - 1.0 - pos)[None] * log_g[:, None])
    chunk_decay = np.exp(C * log_g)
    qd = np.broadcast_to(q_decay[:, :, None], (H, C, RET_DV))
    kd = np.broadcast_to(k_decay[:, :, None], (H, C, RET_DV))
    cd = np.broadcast_to(chunk_decay[:, None, None], (H, 1, RET_DV))
    f = lambda a: jnp.asarray(np.ascontiguousarray(a), F32)
    return f(d_inner), f(qd), f(kd), f(cd)


def _retention(rq, rk, rv, rg, ms_ret):
    B, S, _ = rq.shape
    C = min(RET_CHUNK, S)
    din, qd, kd, cd = _ret_consts(C)
    rows = _tile(S, RET_CHUNKS_PER_STEP * C)
    row = lambda w: pl.BlockSpec((None, rows, w), lambda b, n: (b, n, 0))
    full = lambda a: pl.BlockSpec(a.shape, lambda b, n: (0,) * a.ndim)
    return pl.pallas_call(
        _ret_kernel,
        grid=(B, S // rows),
        in_specs=[row(256), row(256), row(512), row(512), full(din), full(qd), full(kd), full(cd),
                  pl.BlockSpec((1, RET_W), lambda b, n: (0, 0))],
        out_specs=row(RET_W),
        out_shape=jax.ShapeDtypeStruct((B, S, RET_W), BF16),
        scratch_shapes=[pltpu.VMEM((RET_HEADS, RET_DK, RET_DV), F32)],
        compiler_params=_cparams(("parallel", "arbitrary")),
        name="ret",
    )(rq, rk, rv, rg, din, qd, kd, cd, ms_ret)


def _f32_key(x):
    i = lax.bitcast_convert_type(x, I32)
    return i ^ ((i >> 31) & 0x7FFFFFFF)


def _key_f32(k):
    return lax.bitcast_convert_type(k ^ ((k >> 31) & 0x7FFFFFFF), F32)


def _floor16(x):
    i = lax.bitcast_convert_type(x, I32)
    return lax.bitcast_convert_type((i + ((i >> 31) & 0xFFFF)) & -0x10000, F32)


def _trunc16(x):
    return lax.bitcast_convert_type(lax.bitcast_convert_type(x, I32) & -0x10000, F32)


def _dsa_kernel(iqt_ref, iwt_ref, aqt_ref, ik_ref, ak_ref, avt_ref, ms_ref, o_ref, score_ref, code_ref, qa_ref, sa_ref, sb_ref, mask_ref, *acc_refs,
                tq, tks, n_sel):
    H, G, R, d = DSA_HEADS, DSA_KV_HEADS, DSA_HEADS // DSA_KV_HEADS, DSA_HD
    t0 = pl.program_id(1) * tq
    nsub = (t0 + tq) // tks
    tka = 2 * tks
    npair = ((t0 + tq) // tka + 1) // 2
    kf = float(n_sel)
    qpos = t0 + lax.broadcasted_iota(I32, (1, tq), 1)
    krow = lax.broadcasted_iota(I32, (tks, tq), 0)

    wrow = [iwt_ref[h:h + 1, :] for h in range(IDX_HEADS)]

    def fold8(x, op):
        acc = x[0:8, :]
        for i in range(1, tks // 8):
            acc = op(acc, x[8 * i:8 * (i + 1), :])
        return acc

    def score_pair(i, carry, masked):
        mx, mn, npos, nnon = carry
        for u in range(2 * tka // tks):
            r0 = pl.multiple_of(i * 2 * tka + u * tks, tks)
            kc = ik_ref[pl.ds(r0, tks), :]
            acc = jnp.zeros((tks, tq), F32)
            for h in range(IDX_HEADS):
                rel = jnp.dot(kc, iqt_ref[h * IDX_HD:(h + 1) * IDX_HD, :], preferred_element_type=F32)
                acc = acc + jnp.maximum(rel, 0.0) * wrow[h]
            if masked:
                causal = r0 + krow <= qpos
                sc = jnp.where(causal, acc, -jnp.inf)
                lowest = jnp.where(causal, acc, jnp.inf)
            else:
                sc = lowest = acc
            score_ref[pl.ds(r0, tks), :] = sc
            code_ref[pl.ds(r0, tks), :] = _trunc16(sc).astype(BF16)
            mx = jnp.maximum(mx, fold8(sc, jnp.maximum))
            mn = jnp.minimum(mn, fold8(lowest, jnp.minimum))
            npos = npos + fold8(jnp.where(sc > 0.0, 1.0, 0.0), jnp.add)
            nnon = nnon + fold8(jnp.where(sc >= 0.0, 1.0, 0.0), jnp.add)
        return mx, mn, npos, nnon

    stat0 = (jnp.full((8, tq), -jnp.inf, F32), jnp.full((8, tq), jnp.inf, F32),
             jnp.zeros((8, tq), F32), jnp.zeros((8, tq), F32))
    n_inner = (t0 + 1) // (2 * tka)
    stat = lax.fori_loop(0, n_inner, functools.partial(score_pair, masked=False), stat0)
    mx, mn, npos, nnon = lax.fori_loop(n_inner, npair, functools.partial(score_pair, masked=True), stat)
    top = jnp.max(mx, axis=0, keepdims=True)
    lo0 = jnp.min(mn, axis=0, keepdims=True)
    n_pos = jnp.sum(npos, axis=0, keepdims=True)
    n_nonneg = jnp.sum(nnon, axis=0, keepdims=True)


    def count(th, strict):
        def body(j, acc):
            for u in range(2):
                s = score_ref[pl.ds(pl.multiple_of((2 * j + u) * tks, tks), tks), :]
                hit = (s > th) if strict else (s >= th)
                acc = acc + fold8(jnp.where(hit, 1.0, 0.0), jnp.add)
            return acc

        acc = lax.fori_loop(0, nsub // 2, body, jnp.zeros((8, tq), F32))
        return jnp.sum(acc, axis=0, keepdims=True)

    def probe(lo, hi, it):
        lk, hk = _f32_key(lo), _f32_key(hi)
        mk = (lk >> 1) + (hk >> 1) + (lk & hk & 1)
        mv = lo + (hi - lo) * 0.5
        early = (jnp.zeros((1, tq), I32) + it) < BISECT_VALUE_STEPS
        mid = jnp.where(early & (mv > lo) & (mv < hi), mv, _key_f32(mk))
        return mid, jnp.max(jnp.where(mk != lk, 1.0, 0.0))

    def bis_cond(c):
        return (c[5] > 0.0) & (c[6] < BISECT_MAX_STEPS)

    def bis_body(c):
        lo, hi, c_lo, c_hi, mid, _, it = c
        cnt = count(mid, False)
        ge = cnt >= kf
        up = ge | (cnt == kf)
        dn = (~ge) | (cnt == kf)
        lo, c_lo = jnp.where(up, mid, lo), jnp.where(up, cnt, c_lo)
        hi, c_hi = jnp.where(dn, mid, hi), jnp.where(dn, cnt, c_hi)
        mid, active = probe(lo, hi, it + 1)
        return lo, hi, c_lo, c_hi, mid, active, it + 1

    zero = jnp.zeros((1, tq), F32)
    keep_all = qpos + 1 <= n_sel
    settled = keep_all | ((n_nonneg >= kf) & (n_pos < kf))
    above = n_pos >= kf
    c_lo0 = jnp.where(settled | above, n_nonneg, (qpos + 1).astype(F32))
    c_hi0 = jnp.where(settled | ~above, n_nonneg, zero)
    lo0 = jnp.where(settled | above, zero, lo0)
    hi0 = jnp.where(settled | ~above, zero, _key_f32(_f32_key(top) + 1))
    def count16(m):
        mb = m.astype(BF16)

        def body(j, acc):
            for u in range(2):
                c = code_ref[pl.ds(pl.multiple_of((2 * j + u) * tks, tks), tks), :]
                one = jnp.where(c >= mb, jnp.ones((), BF16), jnp.zeros((), BF16))
                part = one[0:16, :]
                for i in range(1, tks // 16):
                    part = part + one[16 * i:16 * (i + 1), :]
                acc = acc + part.astype(F32)
            return acc

        acc = lax.fori_loop(0, nsub // 2, body, jnp.zeros((16, tq), F32))
        return jnp.sum(acc, axis=0, keepdims=True)

    def coarse_step(_, c):
        lo, hi, c_lo, c_hi = c
        below = _floor16(lo + (hi - lo) * 0.5)
        above = _key_f32(_f32_key(below) + 0x10000)
        m = jnp.where(below > lo, below, above)
        v = jnp.where(m > 0.0, m, _key_f32(_f32_key(m) - 0x10000 + 1))
        ok = (v > lo) & (v < hi)
        cnt = count16(m)
        ge = cnt >= kf
        up = ok & ge
        dn = ok & ((~ge) | (cnt == kf))
        return (jnp.where(up, v, lo), jnp.where(dn, v, hi), jnp.where(up, cnt, c_lo), jnp.where(dn, cnt, c_hi))

    lo0, hi0, c_lo0, c_hi0 = lax.fori_loop(0, BISECT_COARSE_STEPS, coarse_step, (lo0, hi0, c_lo0, c_hi0))
    it0 = jnp.int32(BISECT_VALUE_STEPS)
    mid0, active0 = probe(lo0, hi0, it0)
    lo, hi, c_lo, c_hi, _, _, _ = lax.while_loop(bis_cond, bis_body,
                                                 (lo0, hi0, c_lo0, c_hi0, mid0, active0, it0))
    at_hi = c_hi >= kf
    thr = jnp.where(keep_all, F32_LOWEST, jnp.where(at_hi, hi, lo))
    excess = jnp.where(keep_all, 0.0, jnp.where(at_hi, c_hi, c_lo) - kf)

    @pl.when(jnp.max(excess) > 0.0)
    def _():
        tied_nonzero = jnp.max(jnp.where((excess > 0.0) & (thr != 0.0), 1.0, 0.0)) > 0.0
        n_above = lax.cond(tied_nonzero, lambda: count(thr, True), lambda: n_pos)
        budget = jnp.where(excess > 0.0, kf - jnp.where(thr == 0.0, n_pos, n_above), jnp.inf)
        earlier = lax.broadcasted_iota(I32, (tks, tks), 1) < lax.broadcasted_iota(I32, (tks, tks), 0)
        earlier = jnp.where(earlier, 1.0, 0.0).astype(BF16)

        def fix(j, seen):
            tiles = []
            for u in range(FIX_UNROLL):
                r0 = pl.multiple_of((FIX_UNROLL * j + u) * tks, tks)
                s = score_ref[pl.ds(r0, tks), :]
                eq = s == thr
                eqf = jnp.where(eq, 1.0, 0.0)
                within = jnp.dot(earlier, eqf.astype(BF16), preferred_element_type=F32)
                tiles.append((r0, s, eq, within, jnp.sum(eqf, axis=0, keepdims=True)))
            for r0, s, eq, within, n_eq in tiles:
                score_ref[pl.ds(r0, tks), :] = jnp.where(eq & (within + seen >= budget), -jnp.inf, s)
                seen = seen + n_eq
            return seen

        lax.fori_loop(0, nsub // FIX_UNROLL, fix, jnp.zeros((1, tq), F32))

    arow = lax.broadcasted_iota(I32, (KAUG - d, tq), 0)
    for h in range(H):
        terms, rest = [], 2.0 ** (-8.0 * (h + 1) / H) * LOG2E
        for _ in range(ALIBI_TERMS):
            terms.append(float(np.asarray(rest, np.float32).astype(BF16).astype(np.float64)))
            rest -= terms[-1]
        rows = jnp.zeros((KAUG - d, tq), F32)
        for i, c in enumerate(terms):
            rows = jnp.where(arow == i, c * ALIBI_SPLIT, jnp.where(arow == ALIBI_TERMS + i, c, rows))
        qa_ref[h, 0:d, :] = aqt_ref[h * d:(h + 1) * d, :]
        qa_ref[h, d:KAUG, :] = rows.astype(BF16)
    for acc in acc_refs:
        acc[...] = jnp.zeros_like(acc)

    def logits(j, h):
        ka = ak_ref[pl.ds(pl.multiple_of(j * tka, tka), tka), (h // R) * KAUG:(h // R + 1) * KAUG]
        return jnp.dot(ka, qa_ref[h], preferred_element_type=F32)

    def step(j, j_next, cur_ref, next_ref, ms):
        r0 = pl.multiple_of(j * tka, tka)
        mask_ref[...] = jnp.where(score_ref[pl.ds(r0, tka), :] >= thr, 0.0, NEG_BIG)
        nms = []
        ahead = logits(j_next, 0)
        for h in range(H):
            g = h // R
            next_ref[h] = ahead
            if h + 1 < H:
                ahead = logits(j_next, h + 1)
            s = cur_ref[h] + mask_ref[...]
            m_new = jnp.maximum(ms[h], jnp.max(s, axis=0, keepdims=True))
            p = jnp.exp2(s - m_new).astype(BF16)
            va = avt_ref[g * VAUG:(g + 1) * VAUG, pl.ds(r0, tka)]
            acc = acc_refs[h]
            acc[...] = acc[...] * jnp.exp2(ms[h] - m_new) + jnp.dot(va, p, preferred_element_type=F32)
            nms.append(m_new)
        return tuple(nms)

    for h in range(H):
        sa_ref[h] = logits(0, h)

    def att_pair(i, ms):
        ms = step(2 * i, 2 * i + 1, sa_ref, sb_ref, ms)
        return step(2 * i + 1, jnp.minimum(2 * i + 2, 2 * npair - 1), sb_ref, sa_ref, ms)

    lax.fori_loop(0, npair, att_pair, tuple(jnp.full((1, tq), NEG_BIG, F32) for _ in range(H)))
    for h in range(H):
        a = acc_refs[h][...]
        o = a[0:d, :] / a[d:d + 1, :]
        o_ref[h * d:(h + 1) * d, :] = (o * ms_ref[h * d:(h + 1) * d, :]).astype(BF16)


def _dsa(iqt, iwt, aqt, ik, ak, avt, ms_att, tq, tks):
    B, _, S = iqt.shape
    assert S <= ALIBI_SPLIT * 256 and tq % (2 * tks) == 0 and tq % (FIX_UNROLL * tks) == 0
    assert (S // (2 * tks)) % 2 == 0
    n_sel = min(TOPK_MAX, S // 4)
    G = DSA_KV_HEADS
    colT = lambda h: pl.BlockSpec((None, h, tq), lambda b, i: (b, 0, i))
    msb = jnp.broadcast_to(ms_att.reshape(DSA_W, 1), (DSA_W, tq))
    return pl.pallas_call(
        functools.partial(_dsa_kernel, tq=tq, tks=tks, n_sel=n_sel),
        grid=(B, S // tq),
        in_specs=[colT(IDX_HEADS * IDX_HD), colT(IDX_HEADS), colT(DSA_W),
                  pl.BlockSpec((None, S, IDX_HD), lambda b, i: (b, 0, 0), pipeline_mode=pl.Buffered(1)),
                  pl.BlockSpec((None, S, G * KAUG), lambda b, i: (b, 0, 0), pipeline_mode=pl.Buffered(1)),
                  pl.BlockSpec((None, G * VAUG, S), lambda b, i: (b, 0, 0), pipeline_mode=pl.Buffered(1)),
                  pl.BlockSpec((DSA_W, tq), lambda b, i: (0, 0))],
        out_specs=colT(DSA_W),
        out_shape=jax.ShapeDtypeStruct((B, DSA_W, S), BF16),
        scratch_shapes=[pltpu.VMEM((S, tq), F32), pltpu.VMEM((S, tq), BF16), pltpu.VMEM((DSA_HEADS, KAUG, tq), BF16),
                        pltpu.VMEM((DSA_HEADS, 2 * tks, tq), F32), pltpu.VMEM((DSA_HEADS, 2 * tks, tq), F32),
                        pltpu.VMEM((2 * tks, tq), F32)]
        + [pltpu.VMEM((VAUG, tq), F32) for _ in range(DSA_HEADS)],
        compiler_params=_cparams(("parallel", "arbitrary")),
        name="dsa",
    )(iqt, iwt, aqt, ik, ak, avt, msb)


def _oproj_kernel(ret_ref, att_ref, x_ref, wo_ref, g1_ref, n2_ref, sc_ref, sh_ref, rw_ref, rb_ref,
                  x1_ref, h2_ref, sel_ref, idx_ref, gate_ref, cnt_ref):
    mixo = jnp.dot(ret_ref[...], wo_ref[:RET_W, :], preferred_element_type=F32)
    mixo = mixo + lax.dot_general(att_ref[...], wo_ref[RET_W:, :], (((0,), (0,)), ((), ())),
                                  preferred_element_type=F32)
    x1 = x_ref[...] + g1_ref[...] * mixo
    x1_ref[...] = x1
    y = x1 * lax.rsqrt(jnp.mean(x1 * x1, axis=-1, keepdims=True) + EPS) * n2_ref[...]
    h2 = y * (1.0 + sc_ref[...]) + sh_ref[...]
    _store_token_major(h2_ref, h2)
    h_hi = h2.astype(BF16)
    h_lo = (h2 - h_hi.astype(F32)).astype(BF16)
    logits = jnp.dot(h_hi, rw_ref[0], preferred_element_type=F32)
    logits = logits + (jnp.dot(h_hi, rw_ref[1], preferred_element_type=F32)
                       + jnp.dot(h_lo, rw_ref[0], preferred_element_type=F32)) + rb_ref[...]
    tm = logits.shape[0]
    lane = lax.broadcasted_iota(I32, (tm, LANES), 1).astype(F32)
    work = jnp.where(lane < N_EXPERTS, logits, -jnp.inf)
    sel = jnp.zeros((tm, LANES), F32)
    idx_tab = jnp.zeros((tm, LANES), F32)
    vals = []
    for k in range(TOP_K):
        m = jnp.max(work, axis=1, keepdims=True)
        idx = jnp.min(jnp.where(work == m, lane, float(LANES)), axis=1, keepdims=True)
        hit = lane == idx
        sel = jnp.where(hit, 1.0, sel)
        idx_tab = jnp.where(lane == k, idx, idx_tab)
        work = jnp.where(hit, -jnp.inf, work)
        vals.append(m)
    es = [jnp.exp(v - vals[0]) for v in vals]
    den = es[0] + es[1] + es[2] + es[3]
    gate_tab = jnp.zeros((tm, LANES), F32)
    for k in range(TOP_K):
        gate_tab = jnp.where(lane == k, es[k] / den, gate_tab)
    sel_ref[...] = sel
    idx_ref[...] = idx_tab
    gate_ref[...] = gate_tab

    @pl.when((pl.program_id(0) == 0) & (pl.program_id(1) == 0))
    def _():
        cnt_ref[...] = jnp.zeros_like(cnt_ref)

    cnt_ref[...] += jnp.sum(sel, axis=0, keepdims=True)


def _oproj(ret, att, x, wo, g1, n2g, sc2, sh2, rw_pad, rb_pad, tm):
    B, S, D = x.shape
    nt = S // tm
    row = lambda w: pl.BlockSpec((None, tm, w), lambda b, i: (b, i, 0))
    flat = lambda w: pl.BlockSpec((tm, w), lambda b, i: (b * nt + i, 0))
    vec = pl.BlockSpec((None, 1, D), lambda b, i: (b, 0, 0))
    cst = lambda shape: pl.BlockSpec(shape, lambda b, i: (0, 0))
    sd = lambda shape, dt: jax.ShapeDtypeStruct(shape, dt)
    N = B * S
    return pl.pallas_call(
        _oproj_kernel,
        grid=(B, nt),
        in_specs=[row(RET_W), pl.BlockSpec((None, DSA_W, tm), lambda b, i: (b, 0, i)), row(D), cst((D, D)), vec,
                  cst((1, D)), vec, vec,
                  pl.BlockSpec((2, D, LANES), lambda b, i: (0, 0, 0)), cst((1, LANES))],
        out_specs=[flat(D), pl.BlockSpec((tm * (D // LANES), LANES), lambda b, i: (b * nt + i, 0)),
                   flat(LANES), flat(LANES), flat(LANES), cst((1, LANES))],
        out_shape=[sd((N, D), F32), sd((N * (D // LANES), LANES), F32), sd((N, LANES), F32), sd((N, LANES), F32),
                   sd((N, LANES), F32), sd((1, LANES), F32)],
        compiler_params=_cparams(("arbitrary", "arbitrary")),
        name="oproj",
    )(ret, att, x, wo, g1, n2g.reshape(1, D), sc2, sh2, rw_pad, rb_pad)


def _dest_kernel(sel_ref, idx_ref, pstart_ref, dest_ref, seen_ref):
    @pl.when(pl.program_id(0) == 0)
    def _():
        seen_ref[...] = jnp.zeros_like(seen_ref)

    sel = sel_ref[...]
    tm = sel.shape[0]
    earlier = lax.broadcasted_iota(I32, (tm, tm), 1) < lax.broadcasted_iota(I32, (tm, tm), 0)
    earlier = jnp.where(earlier, 1.0, 0.0).astype(BF16)
    rank = jnp.dot(earlier, sel.astype(BF16), preferred_element_type=F32) + seen_ref[...]
    dest = pstart_ref[...] + rank
    lane = lax.broadcasted_iota(I32, (tm, LANES), 1).astype(F32)
    idx_tab = idx_ref[...]
    out = jnp.zeros((tm, LANES), F32)
    for k in range(TOP_K):
        e_k = jnp.sum(jnp.where(lane == k, idx_tab, 0.0), axis=1, keepdims=True)
        d_k = jnp.sum(jnp.where(lane == e_k, dest, 0.0), axis=1, keepdims=True)
        out = jnp.where(lane == k, d_k, out)
    dest_ref[...] = out.astype(I32)
    seen_ref[...] += jnp.sum(sel, axis=0, keepdims=True)


def _dest(sel, idx_tab, pstart, tm):
    N = sel.shape[0]
    blk = pl.BlockSpec((tm, LANES), lambda i: (i, 0))
    return pl.pallas_call(
        _dest_kernel,
        grid=(N // tm,),
        in_specs=[blk, blk, pl.BlockSpec((1, LANES), lambda i: (0, 0))],
        out_specs=blk,
        out_shape=jax.ShapeDtypeStruct((N, LANES), I32),
        scratch_shapes=[pltpu.VMEM((1, LANES), F32)],
        compiler_params=_cparams(("arbitrary",)),
        name="dest",
    )(sel, idx_tab, pstart)


def _disp_kernel(zs_ref, h2_ref, dest_hbm, xs_hbm, zbuf, idx_a, idx_b, sem_ia, sem_ib, sem_row, sem_z,
                 *, tm, nch, zrows, n_steps):
    i = pl.program_id(0)
    n = tm * TOP_K

    def idx_copy(tile, buf, sem):
        return pltpu.make_async_copy(dest_hbm.at[pl.ds(tile * n, n)], buf, sem)

    def rows(ref, first, count):
        return ref.at[pl.ds(pl.multiple_of(first * nch, nch), count * nch)]

    def row_copy(row, dst):
        return pltpu.make_async_copy(rows(h2_ref, row, 1), rows(xs_hbm, dst, 1), sem_row)

    def issue_rows(first_row, idx):
        def body(r, c):
            for s in range(8):
                t = r * 8 + s
                for k in range(TOP_K):
                    row_copy(first_row + t, idx[t * TOP_K + k]).start(priority=k % 2)
            return c

        lax.fori_loop(0, tm // 8, body, 0)

    def wait_rows():
        pltpu.make_async_copy(rows(xs_hbm, 0, n), rows(xs_hbm, 0, n), sem_row).wait()

    @pl.when(i == 0)
    def _():
        zbuf[...] = jnp.zeros_like(zbuf)
        for e in range(N_EXPERTS):
            fill = pltpu.make_async_copy(zbuf, rows(xs_hbm, zs_ref[e], zrows), sem_z)
            fill.start()
            fill.wait()
        idx_copy(0, idx_a, sem_ia).start()
        idx_copy(1, idx_b, sem_ib).start()

    idx_copy(2 * i, idx_a, sem_ia).wait()
    issue_rows(0, idx_a)
    idx_copy(2 * i + 1, idx_b, sem_ib).wait()
    issue_rows(tm, idx_b)

    @pl.when(i + 1 < n_steps)
    def _():
        idx_copy(2 * i + 2, idx_a, sem_ia).start()
        idx_copy(2 * i + 3, idx_b, sem_ib).start()

    wait_rows()
    wait_rows()


def _dispatch(h2, dest_flat, zero_start, n_rows, zrows, tm):
    nch = D_MODEL // LANES
    n_steps = h2.shape[0] // nch // (2 * tm)
    grid_spec = pltpu.PrefetchScalarGridSpec(
        num_scalar_prefetch=1,
        grid=(n_steps,),
        in_specs=[pl.BlockSpec((2 * tm * nch, LANES), lambda i, zs: (i, 0)), pl.BlockSpec(memory_space=pl.ANY)],
        out_specs=pl.BlockSpec(memory_space=pl.ANY),
        scratch_shapes=[pltpu.VMEM((zrows * nch, LANES), F32), pltpu.SMEM((tm * TOP_K,), I32),
                        pltpu.SMEM((tm * TOP_K,), I32), pltpu.SemaphoreType.DMA, pltpu.SemaphoreType.DMA,
                        pltpu.SemaphoreType.DMA, pltpu.SemaphoreType.DMA],
    )
    return pl.pallas_call(
        functools.partial(_disp_kernel, tm=tm, nch=nch, zrows=zrows, n_steps=n_steps),
        grid_spec=grid_spec,
        out_shape=jax.ShapeDtypeStruct((n_rows * nch, LANES), F32),
        compiler_params=_cparams(("arbitrary",)),
        name="disp",
    )(zero_start, h2, dest_flat)


def _store_token_major(ref, x):
    rows, d = x.shape
    nch = d // LANES
    for j in range(nch):
        ref[pl.ds(j, rows, stride=nch), :] = x[:, j * LANES:(j + 1) * LANES]


def _load_token_major(ref, rows, nch):
    return jnp.concatenate([ref[pl.ds(j, rows, stride=nch), :] for j in range(nch)], axis=1)


def _ffn_kernel(be_ref, nb_ref, nxt_ref, xs_ref, wgu_hbm, bgu_ref, wd_hbm, bd_ref, ys_ref,
                wgu_st, wd_st, wgu_bf, wd_bf, sem_gu, sem_d):
    i = pl.program_id(0)
    live = i < nb_ref[0]
    e = be_ref[i]

    def fetch(ex):
        return (pltpu.make_async_copy(wgu_hbm.at[ex], wgu_st, sem_gu),
                pltpu.make_async_copy(wd_hbm.at[ex], wd_st, sem_d))

    @pl.when(i == 0)
    def _():
        for cp in fetch(e):
            cp.start()

    @pl.when(live & ((i == 0) | (e != be_ref[jnp.maximum(i - 1, 0)])))
    def _():
        for cp in fetch(e):
            cp.wait()
        wgu_bf[...] = wgu_st[...].astype(BF16)
        wd_bf[...] = wd_st[...].astype(BF16)

        @pl.when(nxt_ref[i] >= 0)
        def _():
            for cp in fetch(nxt_ref[i]):
                cp.start()

    @pl.when(live)
    def _():
        xb = _load_token_major(xs_ref, ys_ref.shape[0] // NCH, NCH).astype(BF16)
        gu = jnp.dot(xb, wgu_bf[...], preferred_element_type=F32) + bgu_ref[...]
        gate = jnp.minimum(gu[:, :D_EXPERT], SWIGLU_LIMIT)
        up = jnp.clip(gu[:, D_EXPERT:], -SWIGLU_LIMIT, SWIGLU_LIMIT)
        glu = gate * (1.0 / (1.0 + jnp.exp(-SWIGLU_ALPHA * gate)))
        act = ((up + 1.0) * glu).astype(BF16)
        ys = jnp.dot(act, wd_bf[...], preferred_element_type=F32) + bd_ref[...]
        _store_token_major(ys_ref, ys)

    @pl.when(jnp.logical_not(live))
    def _():
        ys_ref[...] = jnp.zeros_like(ys_ref)


def _ffn(xs, block_e, n_blocks, next_e, w_gu, b_gu, w_down, b_down, tmb):
    D = D_MODEL
    P = xs.shape[0] // NCH
    E = w_gu.shape[0]
    blk = lambda i, be, nb, nx: (jnp.minimum(i, nb[0] - 1), 0)
    wsel = lambda i, be, nb, nx: (be[jnp.minimum(i, nb[0] - 1)], 0, 0)
    grid_spec = pltpu.PrefetchScalarGridSpec(
        num_scalar_prefetch=3,
        grid=(P // tmb,),
        in_specs=[pl.BlockSpec((tmb * NCH, LANES), blk),
                  pl.BlockSpec(memory_space=pl.ANY),
                  pl.BlockSpec((None, 1, 2 * D_EXPERT), wsel),
                  pl.BlockSpec(memory_space=pl.ANY),
                  pl.BlockSpec((None, 1, D), wsel)],
        out_specs=pl.BlockSpec((tmb * (D // LANES), LANES), lambda i, be, nb, nx: (i, 0)),
        scratch_shapes=[pltpu.VMEM((D, 2 * D_EXPERT), F32), pltpu.VMEM((D_EXPERT, D), F32),
                        pltpu.VMEM((D, 2 * D_EXPERT), BF16), pltpu.VMEM((D_EXPERT, D), BF16),
                        pltpu.SemaphoreType.DMA, pltpu.SemaphoreType.DMA],
    )
    return pl.pallas_call(
        _ffn_kernel,
        grid_spec=grid_spec,
        out_shape=jax.ShapeDtypeStruct((P * (D // LANES), LANES), F32),
        compiler_params=_cparams(("arbitrary",)),
        name="ffn",
    )(block_e, n_blocks, next_e, xs, w_gu, b_gu.reshape(E, 1, 2 * D_EXPERT), w_down, b_down.reshape(E, 1, D))


def _comb_kernel(x1_ref, gate_ref, g2_ref, fg_ref, dest_hbm, ys_hbm, o_ref, buf_a, buf_b, idx_a, idx_b,
                 sem_ia, sem_ib, sem_ra, sem_rb, *, tm, n_steps):
    i = pl.program_id(0)
    n = tm * TOP_K

    def idx_copy(tile, buf, sem):
        return pltpu.make_async_copy(dest_hbm.at[pl.ds(tile * n, n)], buf, sem)

    nch = x1_ref.shape[1] // LANES

    def row_copy(src, buf, k, t, sem):
        return pltpu.make_async_copy(ys_hbm.at[pl.ds(pl.multiple_of(src * nch, nch), nch)],
                                     buf.at[k, t[0], :, t[1]], sem)

    def issue_rows(idx, buf, sem):
        def body(r, c):
            for s in range(8):
                for k in range(TOP_K):
                    row_copy(idx[(r * 8 + s) * TOP_K + k], buf, k, (r, s), sem).start(priority=k % 2)
            return c

        lax.fori_loop(0, tm // 8, body, 0)

    def wait_rows(buf, sem):
        pltpu.make_async_copy(buf, buf, sem).wait()

    def finish(buf, lo):
        gates = gate_ref[lo:lo + tm, :]
        y = jnp.zeros((tm, x1_ref.shape[1]), F32)
        for k in range(TOP_K):
            rows = jnp.concatenate([buf[k, :, j].reshape(tm, LANES) for j in range(nch)], axis=1)
            y = y + gates[:, k:k + 1] * rows
        v = x1_ref[lo:lo + tm, :] + g2_ref[...] * y
        o_ref[lo:lo + tm, :] = v * lax.rsqrt(jnp.mean(v * v, axis=-1, keepdims=True) + EPS) * fg_ref[...]

    @pl.when(i == 0)
    def _():
        first = idx_copy(0, idx_a, sem_ia)
        first.start()
        first.wait()
        issue_rows(idx_a, buf_a, sem_ra)
        idx_copy(1, idx_b, sem_ib).start()

    idx_copy(2 * i + 1, idx_b, sem_ib).wait()
    issue_rows(idx_b, buf_b, sem_rb)

    @pl.when(i + 1 < n_steps)
    def _():
        idx_copy(2 * i + 2, idx_a, sem_ia).start()

    wait_rows(buf_a, sem_ra)
    finish(buf_a, 0)

    @pl.when(i + 1 < n_steps)
    def _():
        idx_copy(2 * i + 2, idx_a, sem_ia).wait()
        issue_rows(idx_a, buf_a, sem_ra)
        idx_copy(2 * i + 3, idx_b, sem_ib).start()

    wait_rows(buf_b, sem_rb)
    finish(buf_b, tm)


def _combine(x1, gate_tab, g2, final_g, dest_flat, ys, S, tm):
    N, D = x1.shape
    n_steps = N // (2 * tm)
    per_b = S // (2 * tm)
    return pl.pallas_call(
        functools.partial(_comb_kernel, tm=tm, n_steps=n_steps),
        grid=(n_steps,),
        in_specs=[pl.BlockSpec((2 * tm, D), lambda i: (i, 0)),
                  pl.BlockSpec((2 * tm, LANES), lambda i: (i, 0)),
                  pl.BlockSpec((None, 1, D), lambda i: (i // per_b, 0, 0)),
                  pl.BlockSpec((1, D), lambda i: (0, 0)),
                  pl.BlockSpec(memory_space=pl.ANY),
                  pl.BlockSpec(memory_space=pl.ANY)],
        out_specs=pl.BlockSpec((2 * tm, D), lambda i: (i, 0)),
        out_shape=jax.ShapeDtypeStruct((N, D), F32),
        scratch_shapes=[pltpu.VMEM((TOP_K, tm // 8, D // LANES, 8, LANES), F32),
                        pltpu.VMEM((TOP_K, tm // 8, D // LANES, 8, LANES), F32),
                        pltpu.SMEM((tm * TOP_K,), I32), pltpu.SMEM((tm * TOP_K,), I32),
                        pltpu.SemaphoreType.DMA, pltpu.SemaphoreType.DMA, pltpu.SemaphoreType.DMA,
                        pltpu.SemaphoreType.DMA],
        compiler_params=_cparams(("arbitrary",)),
        name="comb",
    )(x1, gate_tab, g2, final_g.reshape(1, D), dest_flat, ys)


def _tile(n, pref):
    t = min(pref, n)
    assert n % t == 0, (n, t)
    return t


def _layer(x, c, ada_w, ada_b, norm1_g, w_in, mix_scale, w_o, norm2_g,
           router_w, router_b, w_gu, b_gu, w_down, b_down, final_g):
    B, S, D = x.shape
    N = B * S
    mod = _mod(c, ada_w, ada_b).reshape(B, 6, 1, D)
    sh1, sc1, g1, sh2, sc2, g2 = (mod[:, j] for j in range(6))

    w_pad = jnp.pad(w_in, ((0, 0), (0, IN_COLS_PAD - IN_COLS))).astype(BF16)
    rq, rk, rv, rg, aqt, ak, avt, iqt, ik, iwt = _inproj(x, norm1_g, sc1, sh1, w_pad, _tile(S, PROJ_ROWS))
    ms = mix_scale.reshape(1, RET_W + DSA_W)
    ret = _retention(rq, rk, rv, rg, ms[:, :RET_W])
    att = _dsa(iqt, iwt, aqt, ik, ak, avt, ms[:, RET_W:], _tile(S, DSA_QUERIES), DSA_KEYS)

    rw_pad = jnp.pad(router_w, ((0, 0), (0, LANES - N_EXPERTS)))
    rw_hi = rw_pad.astype(BF16)
    rw_pad = jnp.stack([rw_hi, (rw_pad - rw_hi.astype(F32)).astype(BF16)])
    rb_pad = jnp.pad(router_b, (0, LANES - N_EXPERTS)).reshape(1, LANES)
    x1, h2, sel, idx_tab, gate_tab, counts = _oproj(ret, att, x, w_o.astype(BF16), g1, norm2_g, sc2, sh2,
                                                    rw_pad, rb_pad, _tile(S, PROJ_ROWS))

    tmb = FFN_ROWS
    n_rows = (N * TOP_K + N_EXPERTS * (tmb - 1)) // tmb * tmb + tmb
    cnt = counts[0, :N_EXPERTS].astype(I32)
    padded = (cnt + tmb - 1) // tmb * tmb
    ends = jnp.cumsum(padded)
    starts = ends - padded
    pstart = jnp.pad(starts.astype(F32), (0, LANES - N_EXPERTS)).reshape(1, LANES)
    n_blocks = (ends[-1] // tmb).reshape(1)
    first_row = jnp.arange(n_rows // tmb, dtype=I32) * tmb
    block_e = jnp.minimum(jnp.sum((ends[None, :] <= first_row[:, None]).astype(I32), axis=1), N_EXPERTS - 1)

    tmd = _tile(N, MOE_TOKENS)
    dest_tab = _dest(sel, idx_tab, pstart, tmd)
    dest_flat = dest_tab[:, :TOP_K].reshape(N * TOP_K)
    xs = _dispatch(h2, dest_flat, starts + cnt, n_rows, tmb, tmd)
    eid = jnp.arange(N_EXPERTS, dtype=I32)
    later_used = (eid[None, :] > eid[:, None]) & (padded[None, :] > 0)
    next_used = jnp.min(jnp.where(later_used, eid[None, :], N_EXPERTS), axis=1)
    next_e = jnp.where(next_used < N_EXPERTS, next_used, -1)[block_e].astype(I32)
    ys = _ffn(xs, block_e, n_blocks, next_e, w_gu, b_gu, w_down, b_down, tmb)
    out = _combine(x1, gate_tab, g2, final_g, dest_flat, ys, S, _tile(S, MOE_TOKENS))
    return out.reshape(B, S, D)


def kernel(x, c, ada_w, ada_b, norm1_g, w_in, mix_scale, w_o, norm2_g, router_w, router_b, w_gu, b_gu,
           w_down, b_down, final_g):
    assert ada_w.shape[0] == 1, "single-layer stack"
    return _layer(x, c, ada_w[0], ada_b[0], norm1_g[0], w_in[0], mix_scale[0], w_o[0], norm2_g[0],
                  router_w[0], router_b[0], w_gu[0], b_gu[0], w_down[0], b_down[0], final_g)
```

```python
import functools

import numpy as np
import jax
import jax.numpy as jnp
from jax import lax
from jax.experimental import pallas as pl
from jax.experimental.pallas import tpu as pltpu

F32 = jnp.float32
BF16 = jnp.bfloat16
I32 = jnp.int32

D_MODEL = 1024
RET_HEADS = 4
RET_DK = 64
RET_DV = 128
RET_CHUNK = 128
DSA_HEADS = 8
DSA_KV_HEADS = 2
DSA_HD = 64
IDX_HEADS = 8
IDX_HD = 64
TOPK_MAX = 256
N_EXPERTS = 32
TOP_K = 4
D_EXPERT = D_MODEL
SWIGLU_LIMIT = 7.0
SWIGLU_ALPHA = 1.702
EPS = 1e-6

RET_W = RET_HEADS * RET_DV
DSA_W = DSA_HEADS * DSA_HD
IN_COLS = 2888
IN_COLS_PAD = 2944

KAUG = 128
VAUG = 80
ALIBI_SPLIT = 64
ALIBI_TERMS = 3
LOG2E = 1.4426950408889634
PROJ_ROWS = 512
DSA_QUERIES = 512
DSA_KEYS = 128
MOE_TOKENS = 256
RET_CHUNKS_PER_STEP = 8
FFN_ROWS = 512
OPROJ_PARTS = 2
FIX_UNROLL = 4
BISECT_COARSE_STEPS = 8
BISECT_VALUE_STEPS = 8
BISECT_MAX_STEPS = 64

LANES = 128
NCH = D_MODEL // LANES
VMEM_LIMIT = 56 * 1024 * 1024
NEG_BIG = -1e30
F32_LOWEST = float(np.finfo(np.float32).min)


def _cparams(sem):
    return pltpu.CompilerParams(dimension_semantics=sem, vmem_limit_bytes=VMEM_LIMIT)


def _mod_kernel(c_ref, w_ref, b_ref, o_ref):
    c = c_ref[...]
    s = c * (1.0 / (1.0 + jnp.exp(-c)))
    o_ref[...] = jnp.dot(s, w_ref[...], preferred_element_type=F32,
                         precision=lax.Precision.HIGHEST) + b_ref[...]


def _mod(c, ada_w, ada_b):
    B, D = c.shape
    n_out = ada_w.shape[1]
    rows = 8
    c8 = jnp.zeros((rows, D), F32).at[:B].set(c)
    out = pl.pallas_call(
        _mod_kernel,
        grid=(n_out // D,),
        in_specs=[pl.BlockSpec((rows, D), lambda j: (0, 0)),
                  pl.BlockSpec((D, D), lambda j: (0, j)),
                  pl.BlockSpec((1, D), lambda j: (0, j))],
        out_specs=pl.BlockSpec((rows, D), lambda j: (0, j)),
        out_shape=jax.ShapeDtypeStruct((rows, n_out), F32),
        compiler_params=_cparams(("arbitrary",)),
        name="mod",
    )(c8, ada_w, ada_b.reshape(1, n_out))
    return out[:B]


def _inproj_kernel(x_ref, g_ref, sc_ref, sh_ref, w_ref,
                   rq_ref, rk_ref, rv_ref, rg_ref, aqt_ref, ak_ref, avt_ref, iqt_ref, ik_ref, iwt_ref):
    x = x_ref[...]
    ms = jnp.mean(x * x, axis=-1, keepdims=True)
    y = x * lax.rsqrt(ms + EPS) * g_ref[...]
    hb = (y * (1.0 + sc_ref[...]) + sh_ref[...]).astype(BF16)

    def proj(lo, hi):
        return jnp.dot(hb, w_ref[:, lo:hi], preferred_element_type=F32)

    tm = x.shape[0]
    d = DSA_HD
    rq_ref[...] = proj(0, 256).astype(BF16)
    rk_ref[...] = (proj(256, 512) * (RET_DK ** -0.5)).astype(BF16)
    rv_ref[...] = proj(512, 1024).astype(BF16)
    rg_ref[...] = proj(1024, 1536).astype(BF16)
    aqt_ref[...] = (proj(1536, 2048) * (d ** -0.5 * LOG2E)).T.astype(BF16)
    kk = proj(2048, 2176)
    pos = pl.program_id(1) * tm + lax.broadcasted_iota(I32, (tm, d), 0)
    col = lax.broadcasted_iota(I32, (tm, d), 1)
    posblk = jnp.where(col < ALIBI_TERMS, pos // ALIBI_SPLIT,
                       jnp.where(col < 2 * ALIBI_TERMS, pos % ALIBI_SPLIT, 0)).astype(F32)
    for g in range(DSA_KV_HEADS):
        ak_ref[:, g * KAUG:g * KAUG + d] = kk[:, g * d:(g + 1) * d].astype(BF16)
        ak_ref[:, g * KAUG + d:(g + 1) * KAUG] = posblk.astype(BF16)
    vt = proj(2176, 2304).T
    r16 = lax.broadcasted_iota(I32, (VAUG - d, tm), 0)
    onesblk = jnp.where(r16 == 0, 1.0, 0.0).astype(BF16)
    for g in range(DSA_KV_HEADS):
        avt_ref[g * VAUG:g * VAUG + d, :] = vt[g * d:(g + 1) * d, :].astype(BF16)
        avt_ref[g * VAUG + d:(g + 1) * VAUG, :] = onesblk
    iqt_ref[...] = proj(2304, 2816).T.astype(BF16)
    last = proj(2816, 2944)
    ik_ref[...] = last[:, :IDX_HD].astype(BF16)
    iwt_ref[...] = last.T[IDX_HD:IDX_HD + IDX_HEADS, :] * ((IDX_HD ** -0.5) * (IDX_HEADS ** -0.5))


def _inproj(x, norm_g, sc, sh, w_pad, tm):
    B, S, D = x.shape
    row = lambda w: pl.BlockSpec((None, tm, w), lambda b, i: (b, i, 0))
    colT = lambda h: pl.BlockSpec((None, h, tm), lambda b, i: (b, 0, i))
    vec = pl.BlockSpec((None, 1, D), lambda b, i: (b, 0, 0))
    sd = lambda shape, dt: jax.ShapeDtypeStruct(shape, dt)
    G = DSA_KV_HEADS
    return pl.pallas_call(
        _inproj_kernel,
        grid=(B, S // tm),
        in_specs=[row(D), pl.BlockSpec((1, D), lambda b, i: (0, 0)), vec, vec,
                  pl.BlockSpec((D, IN_COLS_PAD), lambda b, i: (0, 0))],
        out_specs=[row(256), row(256), row(512), row(512), colT(DSA_W), row(G * KAUG), colT(G * VAUG),
                   colT(IDX_HEADS * IDX_HD), row(IDX_HD), colT(IDX_HEADS)],
        out_shape=[sd((B, S, 256), BF16), sd((B, S, 256), BF16), sd((B, S, 512), BF16),
                   sd((B, S, 512), BF16), sd((B, DSA_W, S), BF16), sd((B, S, G * KAUG), BF16),
                   sd((B, G * VAUG, S), BF16), sd((B, IDX_HEADS * IDX_HD, S), BF16),
                   sd((B, S, IDX_HD), BF16), sd((B, IDX_HEADS, S), F32)],
        compiler_params=_cparams(("parallel", "parallel")),
        name="inproj",
    )(x, norm_g.reshape(1, D), sc, sh, w_pad)


def _ret_kernel(rq_ref, rk_ref, rv_ref, rg_ref, din_ref, qd_ref, kd_ref, cd_ref, ms_ref, o_ref, state_ref):
    @pl.when(pl.program_id(1) == 0)
    def _():
        state_ref[...] = jnp.zeros_like(state_ref)

    C = din_ref.shape[1]
    for c in range(rq_ref.shape[0] // C):
        rows = slice(c * C, (c + 1) * C)
        for h in range(RET_HEADS):
            q = rq_ref[rows, h * RET_DK:(h + 1) * RET_DK]
            k = rk_ref[rows, h * RET_DK:(h + 1) * RET_DK]
            v = rv_ref[rows, h * RET_DV:(h + 1) * RET_DV]
            r_prev = state_ref[h]
            s = lax.dot_general(q, k, (((1,), (1,)), ((), ())), preferred_element_type=F32) * din_ref[h]
            o = jnp.dot(s.astype(BF16), v, preferred_element_type=F32)
            o = o + jnp.dot(q, r_prev.astype(BF16), preferred_element_type=F32) * qd_ref[h]
            vd = (v.astype(F32) * kd_ref[h]).astype(BF16)
            kv = lax.dot_general(k, vd, (((0,), (0,)), ((), ())), preferred_element_type=F32)
            state_ref[h] = r_prev * cd_ref[h] + kv
            o = o * lax.rsqrt(jnp.mean(o * o, axis=-1, keepdims=True) + EPS)
            g = rg_ref[rows, h * RET_DV:(h + 1) * RET_DV].astype(F32)
            gate = g * (1.0 / (1.0 + jnp.exp(-g)))
            o_ref[rows, h * RET_DV:(h + 1) * RET_DV] = (
                gate * o * ms_ref[:, h * RET_DV:(h + 1) * RET_DV]).astype(BF16)


def _ret_consts(C):
    H = RET_HEADS
    log_g = np.log1p(-np.exp2(-5.0 - np.arange(H, dtype=np.float64)))
    pos = np.arange(C, dtype=np.float64)
    diff = pos[:, None] - pos[None, :]
    d_inner = np.where(diff[None] >= 0, np.exp(np.maximum(diff, 0.0)[None] * log_g[:, None, None]), 0.0)
    q_decay = np.exp((pos + 1.0)[None] * log_g[:, None])
    k_decay = np.exp((C - 1.0 - pos)[None] * log_g[:, None])
    chunk_decay = np.exp(C * log_g)
    qd = np.broadcast_to(q_decay[:, :, None], (H, C, RET_DV))
    kd = np.broadcast_to(k_decay[:, :, None], (H, C, RET_DV))
    cd = np.broadcast_to(chunk_decay[:, None, None], (H, 1, RET_DV))
    f = lambda a: jnp.asarray(np.ascontiguousarray(a), F32)
    return f(d_inner), f(qd), f(kd), f(cd)


def _retention(rq, rk, rv, rg, ms_ret):
    B, S, _ = rq.shape
    C = min(RET_CHUNK, S)
    din, qd, kd, cd = _ret_consts(C)
    rows = _tile(S, RET_CHUNKS_PER_STEP * C)
    row = lambda w: pl.BlockSpec((None, rows, w), lambda b, n: (b, n, 0))
    full = lambda a: pl.BlockSpec(a.shape, lambda b, n: (0,) * a.ndim)
    return pl.pallas_call(
        _ret_kernel,
        grid=(B, S // rows),
        in_specs=[row(256), row(256), row(512), row(512), full(din), full(qd), full(kd), full(cd),
                  pl.BlockSpec((1, RET_W), lambda b, n: (0, 0))],
        out_specs=row(RET_W),
        out_shape=jax.ShapeDtypeStruct((B, S, RET_W), BF16),
        scratch_shapes=[pltpu.VMEM((RET_HEADS, RET_DK, RET_DV), F32)],
        compiler_params=_cparams(("parallel", "arbitrary")),
        name="ret",
    )(rq, rk, rv, rg, din, qd, kd, cd, ms_ret)


def _f32_key(x):
    i = lax.bitcast_convert_type(x, I32)
    return i ^ ((i >> 31) & 0x7FFFFFFF)


def _key_f32(k):
    return lax.bitcast_convert_type(k ^ ((k >> 31) & 0x7FFFFFFF), F32)


def _floor16(x):
    i = lax.bitcast_convert_type(x, I32)
    return lax.bitcast_convert_type((i + ((i >> 31) & 0xFFFF)) & -0x10000, F32)


def _trunc16(x):
    return lax.bitcast_convert_type(lax.bitcast_convert_type(x, I32) & -0x10000, F32)


def _dsa_kernel(iqt_ref, iwt_ref, aqt_ref, ik_ref, ak_ref, avt_ref, ms_ref, o_ref, score_ref, code_ref, qa_ref, sa_ref, sb_ref, mask_ref, *acc_refs,
                tq, tks, n_sel):
    H, G, R, d = DSA_HEADS, DSA_KV_HEADS, DSA_HEADS // DSA_KV_HEADS, DSA_HD
    t0 = pl.program_id(1) * tq
    nsub = (t0 + tq) // tks
    tka = 2 * tks
    npair = ((t0 + tq) // tka + 1) // 2
    kf = float(n_sel)
    qpos = t0 + lax.broadcasted_iota(I32, (1, tq), 1)
    krow = lax.broadcasted_iota(I32, (tks, tq), 0)

    wrow = [iwt_ref[h:h + 1, :] for h in range(IDX_HEADS)]

    def fold8(x, op):
        acc = x[0:8, :]
        for i in range(1, tks // 8):
            acc = op(acc, x[8 * i:8 * (i + 1), :])
        return acc

    def score_pair(i, carry, masked):
        mx, mn, npos, nnon = carry
        for u in range(2 * tka // tks):
            r0 = pl.multiple_of(i * 2 * tka + u * tks, tks)
            kc = ik_ref[pl.ds(r0, tks), :]
            acc = jnp.zeros((tks, tq), F32)
            for h in range(IDX_HEADS):
                rel = jnp.dot(kc, iqt_ref[h * IDX_HD:(h + 1) * IDX_HD, :], preferred_element_type=F32)
                acc = acc + jnp.maximum(rel, 0.0) * wrow[h]
            if masked:
                causal = r0 + krow <= qpos
                sc = jnp.where(causal, acc, -jnp.inf)
                lowest = jnp.where(causal, acc, jnp.inf)
            else:
                sc = lowest = acc
            score_ref[pl.ds(r0, tks), :] = sc
            code_ref[pl.ds(r0, tks), :] = _trunc16(sc).astype(BF16)
            mx = jnp.maximum(mx, fold8(sc, jnp.maximum))
            mn = jnp.minimum(mn, fold8(lowest, jnp.minimum))
            npos = npos + fold8(jnp.where(sc > 0.0, 1.0, 0.0), jnp.add)
            nnon = nnon + fold8(jnp.where(sc >= 0.0, 1.0, 0.0), jnp.add)
        return mx, mn, npos, nnon

    stat0 = (jnp.full((8, tq), -jnp.inf, F32), jnp.full((8, tq), jnp.inf, F32),
             jnp.zeros((8, tq), F32), jnp.zeros((8, tq), F32))
    n_inner = (t0 + 1) // (2 * tka)
    stat = lax.fori_loop(0, n_inner, functools.partial(score_pair, masked=False), stat0)
    mx, mn, npos, nnon = lax.fori_loop(n_inner, npair, functools.partial(score_pair, masked=True), stat)
    top = jnp.max(mx, axis=0, keepdims=True)
    lo0 = jnp.min(mn, axis=0, keepdims=True)
    n_pos = jnp.sum(npos, axis=0, keepdims=True)
    n_nonneg = jnp.sum(nnon, axis=0, keepdims=True)


    def count(th, strict):
        def body(j, acc):
            for u in range(2):
                s = score_ref[pl.ds(pl.multiple_of((2 * j + u) * tks, tks), tks), :]
                hit = (s > th) if strict else (s >= th)
                acc = acc + fold8(jnp.where(hit, 1.0, 0.0), jnp.add)
            return acc

        acc = lax.fori_loop(0, nsub // 2, body, jnp.zeros((8, tq), F32))
        return jnp.sum(acc, axis=0, keepdims=True)

    def probe(lo, hi, it):
        lk, hk = _f32_key(lo), _f32_key(hi)
        mk = (lk >> 1) + (hk >> 1) + (lk & hk & 1)
        mv = lo + (hi - lo) * 0.5
        early = (jnp.zeros((1, tq), I32) + it) < BISECT_VALUE_STEPS
        mid = jnp.where(early & (mv > lo) & (mv < hi), mv, _key_f32(mk))
        return mid, jnp.max(jnp.where(mk != lk, 1.0, 0.0))

    def bis_cond(c):
        return (c[5] > 0.0) & (c[6] < BISECT_MAX_STEPS)

    def bis_body(c):
        lo, hi, c_lo, c_hi, mid, _, it = c
        cnt = count(mid, False)
        ge = cnt >= kf
        up = ge | (cnt == kf)
        dn = (~ge) | (cnt == kf)
        lo, c_lo = jnp.where(up, mid, lo), jnp.where(up, cnt, c_lo)
        hi, c_hi = jnp.where(dn, mid, hi), jnp.where(dn, cnt, c_hi)
        mid, active = probe(lo, hi, it + 1)
        return lo, hi, c_lo, c_hi, mid, active, it + 1

    zero = jnp.zeros((1, tq), F32)
    keep_all = qpos + 1 <= n_sel
    settled = keep_all | ((n_nonneg >= kf) & (n_pos < kf))
    above = n_pos >= kf
    c_lo0 = jnp.where(settled | above, n_nonneg, (qpos + 1).astype(F32))
    c_hi0 = jnp.where(settled | ~above, n_nonneg, zero)
    lo0 = jnp.where(settled | above, zero, lo0)
    hi0 = jnp.where(settled | ~above, zero, _key_f32(_f32_key(top) + 1))
    def count16(m):
        mb = m.astype(BF16)

        def body(j, acc):
            for u in range(2):
                c = code_ref[pl.ds(pl.multiple_of((2 * j + u) * tks, tks), tks), :]
                one = jnp.where(c >= mb, jnp.ones((), BF16), jnp.zeros((), BF16))
                part = one[0:16, :]
                for i in range(1, tks // 16):
                    part = part + one[16 * i:16 * (i + 1), :]
                acc = acc + part.astype(F32)
            return acc

        acc = lax.fori_loop(0, nsub // 2, body, jnp.zeros((16, tq), F32))
        return jnp.sum(acc, axis=0, keepdims=True)

    def coarse_step(_, c):
        lo, hi, c_lo, c_hi = c
        below = _floor16(lo + (hi - lo) * 0.5)
        above = _key_f32(_f32_key(below) + 0x10000)
        m = jnp.where(below > lo, below, above)
        v = jnp.where(m > 0.0, m, _key_f32(_f32_key(m) - 0x10000 + 1))
        ok = (v > lo) & (v < hi)
        cnt = count16(m)
        ge = cnt >= kf
        up = ok & ge
        dn = ok & ((~ge) | (cnt == kf))
        return (jnp.where(up, v, lo), jnp.where(dn, v, hi), jnp.where(up, cnt, c_lo), jnp.where(dn, cnt, c_hi))

    lo0, hi0, c_lo0, c_hi0 = lax.fori_loop(0, BISECT_COARSE_STEPS, coarse_step, (lo0, hi0, c_lo0, c_hi0))
    it0 = jnp.int32(BISECT_VALUE_STEPS)
    mid0, active0 = probe(lo0, hi0, it0)
    lo, hi, c_lo, c_hi, _, _, _ = lax.while_loop(bis_cond, bis_body,
                                                 (lo0, hi0, c_lo0, c_hi0, mid0, active0, it0))
    at_hi = c_hi >= kf
    thr = jnp.where(keep_all, F32_LOWEST, jnp.where(at_hi, hi, lo))
    excess = jnp.where(keep_all, 0.0, jnp.where(at_hi, c_hi, c_lo) - kf)

    @pl.when(jnp.max(excess) > 0.0)
    def _():
        tied_nonzero = jnp.max(jnp.where((excess > 0.0) & (thr != 0.0), 1.0, 0.0)) > 0.0
        n_above = lax.cond(tied_nonzero, lambda: count(thr, True), lambda: n_pos)
        budget = jnp.where(excess > 0.0, kf - jnp.where(thr == 0.0, n_pos, n_above), jnp.inf)
        earlier = lax.broadcasted_iota(I32, (tks, tks), 1) < lax.broadcasted_iota(I32, (tks, tks), 0)
        earlier = jnp.where(earlier, 1.0, 0.0).astype(BF16)

        def fix(j, seen):
            tiles = []
            for u in range(FIX_UNROLL):
                r0 = pl.multiple_of((FIX_UNROLL * j + u) * tks, tks)
                s = score_ref[pl.ds(r0, tks), :]
                eq = s == thr
                eqf = jnp.where(eq, 1.0, 0.0)
                within = jnp.dot(earlier, eqf.astype(BF16), preferred_element_type=F32)
                tiles.append((r0, s, eq, within, jnp.sum(eqf, axis=0, keepdims=True)))
            for r0, s, eq, within, n_eq in tiles:
                score_ref[pl.ds(r0, tks), :] = jnp.where(eq & (within + seen >= budget), -jnp.inf, s)
                seen = seen + n_eq
            return seen

        lax.fori_loop(0, nsub // FIX_UNROLL, fix, jnp.zeros((1, tq), F32))

    arow = lax.broadcasted_iota(I32, (KAUG - d, tq), 0)
    for h in range(H):
        terms, rest = [], 2.0 ** (-8.0 * (h + 1) / H) * LOG2E
        for _ in range(ALIBI_TERMS):
            terms.append(float(np.asarray(rest, np.float32).astype(BF16).astype(np.float64)))
            rest -= terms[-1]
        rows = jnp.zeros((KAUG - d, tq), F32)
        for i, c in enumerate(terms):
            rows = jnp.where(arow == i, c * ALIBI_SPLIT, jnp.where(arow == ALIBI_TERMS + i, c, rows))
        qa_ref[h, 0:d, :] = aqt_ref[h * d:(h + 1) * d, :]
        qa_ref[h, d:KAUG, :] = rows.astype(BF16)
    for acc in acc_refs:
        acc[...] = jnp.zeros_like(acc)

    def logits(j, h):
        ka = ak_ref[pl.ds(pl.multiple_of(j * tka, tka), tka), (h // R) * KAUG:(h // R + 1) * KAUG]
        return jnp.dot(ka, qa_ref[h], preferred_element_type=F32)

    def step(j, j_next, cur_ref, next_ref, ms):
        r0 = pl.multiple_of(j * tka, tka)
        mask_ref[...] = jnp.where(score_ref[pl.ds(r0, tka), :] >= thr, 0.0, NEG_BIG)
        nms = []
        ahead = logits(j_next, 0)
        for h in range(H):
            g = h // R
            next_ref[h] = ahead
            if h + 1 < H:
                ahead = logits(j_next, h + 1)
            s = cur_ref[h] + mask_ref[...]
            m_new = jnp.maximum(ms[h], jnp.max(s, axis=0, keepdims=True))
            p = jnp.exp2(s - m_new).astype(BF16)
            va = avt_ref[g * VAUG:(g + 1) * VAUG, pl.ds(r0, tka)]
            acc = acc_refs[h]
            acc[...] = acc[...] * jnp.exp2(ms[h] - m_new) + jnp.dot(va, p, preferred_element_type=F32)
            nms.append(m_new)
        return tuple(nms)

    for h in range(H):
        sa_ref[h] = logits(0, h)

    def att_pair(i, ms):
        ms = step(2 * i, 2 * i + 1, sa_ref, sb_ref, ms)
        return step(2 * i + 1, jnp.minimum(2 * i + 2, 2 * npair - 1), sb_ref, sa_ref, ms)

    lax.fori_loop(0, npair, att_pair, tuple(jnp.full((1, tq), NEG_BIG, F32) for _ in range(H)))
    for h in range(H):
        a = acc_refs[h][...]
        o = a[0:d, :] / a[d:d + 1, :]
        o_ref[h * d:(h + 1) * d, :] = (o * ms_ref[h * d:(h + 1) * d, :]).astype(BF16)


def _dsa(iqt, iwt, aqt, ik, ak, avt, ms_att, tq, tks):
    B, _, S = iqt.shape
    assert S <= ALIBI_SPLIT * 256 and tq % (2 * tks) == 0 and tq % (FIX_UNROLL * tks) == 0
    assert (S // (2 * tks)) % 2 == 0
    n_sel = min(TOPK_MAX, S // 4)
    G = DSA_KV_HEADS
    colT = lambda h: pl.BlockSpec((None, h, tq), lambda b, i: (b, 0, i))
    msb = jnp.broadcast_to(ms_att.reshape(DSA_W, 1), (DSA_W, tq))
    return pl.pallas_call(
        functools.partial(_dsa_kernel, tq=tq, tks=tks, n_sel=n_sel),
        grid=(B, S // tq),
        in_specs=[colT(IDX_HEADS * IDX_HD), colT(IDX_HEADS), colT(DSA_W),
                  pl.BlockSpec((None, S, IDX_HD), lambda b, i: (b, 0, 0), pipeline_mode=pl.Buffered(1)),
                  pl.BlockSpec((None, S, G * KAUG), lambda b, i: (b, 0, 0), pipeline_mode=pl.Buffered(1)),
                  pl.BlockSpec((None, G * VAUG, S), lambda b, i: (b, 0, 0), pipeline_mode=pl.Buffered(1)),
                  pl.BlockSpec((DSA_W, tq), lambda b, i: (0, 0))],
        out_specs=colT(DSA_W),
        out_shape=jax.ShapeDtypeStruct((B, DSA_W, S), BF16),
        scratch_shapes=[pltpu.VMEM((S, tq), F32), pltpu.VMEM((S, tq), BF16), pltpu.VMEM((DSA_HEADS, KAUG, tq), BF16),
                        pltpu.VMEM((DSA_HEADS, 2 * tks, tq), F32), pltpu.VMEM((DSA_HEADS, 2 * tks, tq), F32),
                        pltpu.VMEM((2 * tks, tq), F32)]
        + [pltpu.VMEM((VAUG, tq), F32) for _ in range(DSA_HEADS)],
        compiler_params=_cparams(("parallel", "arbitrary")),
        name="dsa",
    )(iqt, iwt, aqt, ik, ak, avt, msb)


def _oproj_kernel(ret_ref, att_ref, x_ref, wo_ref, g1_ref, n2_ref, sc_ref, sh_ref, rw_ref, rb_ref,
                  x1_ref, h2_ref, sel_ref, idx_ref, gate_ref, cnt_ref):
    tm = x_ref.shape[0]
    parts = [slice(r, r + tm // OPROJ_PARTS) for r in range(0, tm, tm // OPROJ_PARTS)]
    nch = x_ref.shape[1] // LANES

    def mix(rows):
        out = jnp.dot(ret_ref[rows, :], wo_ref[:RET_W, :], preferred_element_type=F32)
        return out + lax.dot_general(att_ref[:, rows], wo_ref[RET_W:, :], (((0,), (0,)), ((), ())),
                                     preferred_element_type=F32)

    def hidden(rows, mixo):
        x1 = x_ref[rows, :] + g1_ref[...] * mixo
        x1_ref[rows, :] = x1
        y = x1 * lax.rsqrt(jnp.mean(x1 * x1, axis=-1, keepdims=True) + EPS) * n2_ref[...]
        h2 = y * (1.0 + sc_ref[...]) + sh_ref[...]
        for j in range(nch):
            h2_ref[pl.ds(rows.start * nch + j, rows.stop - rows.start, stride=nch), :] = h2[:, j * LANES:(j + 1) * LANES]
        return h2

    def router(h2):
        h_hi = h2.astype(BF16)
        h_lo = (h2 - h_hi.astype(F32)).astype(BF16)
        logits = jnp.dot(h_hi, rw_ref[0], preferred_element_type=F32)
        return logits + (jnp.dot(h_hi, rw_ref[1], preferred_element_type=F32)
                         + jnp.dot(h_lo, rw_ref[0], preferred_element_type=F32)) + rb_ref[...]

    def top4(rows, logits):
        n = logits.shape[0]
        lane = lax.broadcasted_iota(I32, (n, LANES), 1).astype(F32)
        work = jnp.where(lane < N_EXPERTS, logits, -jnp.inf)
        sel = jnp.zeros((n, LANES), F32)
        idx_tab = jnp.zeros((n, LANES), F32)
        vals = []
        for k in range(TOP_K):
            m = jnp.max(work, axis=1, keepdims=True)
            idx = jnp.min(jnp.where(work == m, lane, float(LANES)), axis=1, keepdims=True)
            hit = lane == idx
            sel = jnp.where(hit, 1.0, sel)
            idx_tab = jnp.where(lane == k, idx, idx_tab)
            work = jnp.where(hit, -jnp.inf, work)
            vals.append(m)
        es = [jnp.exp(v - vals[0]) for v in vals]
        den = es[0] + es[1] + es[2] + es[3]
        gate_tab = jnp.zeros((n, LANES), F32)
        for k in range(TOP_K):
            gate_tab = jnp.where(lane == k, es[k] / den, gate_tab)
        sel_ref[rows, :] = sel
        idx_ref[rows, :] = idx_tab
        gate_ref[rows, :] = gate_tab
        return jnp.sum(sel, axis=0, keepdims=True)

    mixes = [mix(rows) for rows in parts]
    hiddens = [hidden(rows, m) for rows, m in zip(parts, mixes)]
    logit_parts = [router(h2) for h2 in hiddens]
    counts = [top4(rows, lg) for rows, lg in zip(parts, logit_parts)]

    @pl.when((pl.program_id(0) == 0) & (pl.program_id(1) == 0))
    def _():
        cnt_ref[...] = jnp.zeros_like(cnt_ref)

    cnt_ref[...] += functools.reduce(lambda a, b: a + b, counts)


def _oproj(ret, att, x, wo, g1, n2g, sc2, sh2, rw_pad, rb_pad, tm):
    B, S, D = x.shape
    nt = S // tm
    row = lambda w: pl.BlockSpec((None, tm, w), lambda b, i: (b, i, 0))
    flat = lambda w: pl.BlockSpec((tm, w), lambda b, i: (b * nt + i, 0))
    vec = pl.BlockSpec((None, 1, D), lambda b, i: (b, 0, 0))
    cst = lambda shape: pl.BlockSpec(shape, lambda b, i: (0, 0))
    sd = lambda shape, dt: jax.ShapeDtypeStruct(shape, dt)
    N = B * S
    return pl.pallas_call(
        _oproj_kernel,
        grid=(B, nt),
        in_specs=[row(RET_W), pl.BlockSpec((None, DSA_W, tm), lambda b, i: (b, 0, i)), row(D), cst((D, D)), vec,
                  cst((1, D)), vec, vec,
                  pl.BlockSpec((2, D, LANES), lambda b, i: (0, 0, 0)), cst((1, LANES))],
        out_specs=[flat(D), pl.BlockSpec((tm * (D // LANES), LANES), lambda b, i: (b * nt + i, 0)),
                   flat(LANES), flat(LANES), flat(LANES), cst((1, LANES))],
        out_shape=[sd((N, D), F32), sd((N * (D // LANES), LANES), F32), sd((N, LANES), F32), sd((N, LANES), F32),
                   sd((N, LANES), F32), sd((1, LANES), F32)],
        compiler_params=_cparams(("arbitrary", "arbitrary")),
        name="oproj",
    )(ret, att, x, wo, g1, n2g.reshape(1, D), sc2, sh2, rw_pad, rb_pad)


def _dest_kernel(sel_ref, idx_ref, pstart_ref, dest_ref, seen_ref):
    @pl.when(pl.program_id(0) == 0)
    def _():
        seen_ref[...] = jnp.zeros_like(seen_ref)

    sel = sel_ref[...]
    tm = sel.shape[0]
    earlier = lax.broadcasted_iota(I32, (tm, tm), 1) < lax.broadcasted_iota(I32, (tm, tm), 0)
    earlier = jnp.where(earlier, 1.0, 0.0).astype(BF16)
    rank = jnp.dot(earlier, sel.astype(BF16), preferred_element_type=F32) + seen_ref[...]
    dest = pstart_ref[...] + rank
    lane = lax.broadcasted_iota(I32, (tm, LANES), 1).astype(F32)
    idx_tab = idx_ref[...]
    out = jnp.zeros((tm, LANES), F32)
    for k in range(TOP_K):
        e_k = jnp.sum(jnp.where(lane == k, idx_tab, 0.0), axis=1, keepdims=True)
        d_k = jnp.sum(jnp.where(lane == e_k, dest, 0.0), axis=1, keepdims=True)
        out = jnp.where(lane == k, d_k, out)
    dest_ref[...] = out.astype(I32)
    seen_ref[...] += jnp.sum(sel, axis=0, keepdims=True)


def _dest(sel, idx_tab, pstart, tm):
    N = sel.shape[0]
    blk = pl.BlockSpec((tm, LANES), lambda i: (i, 0))
    return pl.pallas_call(
        _dest_kernel,
        grid=(N // tm,),
        in_specs=[blk, blk, pl.BlockSpec((1, LANES), lambda i: (0, 0))],
        out_specs=blk,
        out_shape=jax.ShapeDtypeStruct((N, LANES), I32),
        scratch_shapes=[pltpu.VMEM((1, LANES), F32)],
        compiler_params=_cparams(("arbitrary",)),
        name="dest",
    )(sel, idx_tab, pstart)


def _disp_kernel(zs_ref, h2_ref, dest_hbm, xs_hbm, zbuf, idx_a, idx_b, sem_ia, sem_ib, sem_row, sem_z,
                 *, tm, nch, zrows, n_steps):
    i = pl.program_id(0)
    n = tm * TOP_K

    def idx_copy(tile, buf, sem):
        return pltpu.make_async_copy(dest_hbm.at[pl.ds(tile * n, n)], buf, sem)

    def rows(ref, first, count):
        return ref.at[pl.ds(pl.multiple_of(first * nch, nch), count * nch)]

    def row_copy(row, dst):
        return pltpu.make_async_copy(rows(h2_ref, row, 1), rows(xs_hbm, dst, 1), sem_row)

    def issue_rows(first_row, idx):
        def body(r, c):
            for s in range(8):
                t = r * 8 + s
                for k in range(TOP_K):
                    row_copy(first_row + t, idx[t * TOP_K + k]).start(priority=k % 2)
            return c

        lax.fori_loop(0, tm // 8, body, 0)

    def wait_rows():
        pltpu.make_async_copy(rows(xs_hbm, 0, n), rows(xs_hbm, 0, n), sem_row).wait()

    @pl.when(i == 0)
    def _():
        zbuf[...] = jnp.zeros_like(zbuf)
        for e in range(N_EXPERTS):
            fill = pltpu.make_async_copy(zbuf, rows(xs_hbm, zs_ref[e], zrows), sem_z)
            fill.start()
            fill.wait()
        idx_copy(0, idx_a, sem_ia).start()
        idx_copy(1, idx_b, sem_ib).start()

    idx_copy(2 * i, idx_a, sem_ia).wait()
    issue_rows(0, idx_a)
    idx_copy(2 * i + 1, idx_b, sem_ib).wait()
    issue_rows(tm, idx_b)

    @pl.when(i + 1 < n_steps)
    def _():
        idx_copy(2 * i + 2, idx_a, sem_ia).start()
        idx_copy(2 * i + 3, idx_b, sem_ib).start()

    wait_rows()
    wait_rows()


def _dispatch(h2, dest_flat, zero_start, n_rows, zrows, tm):
    nch = D_MODEL // LANES
    n_steps = h2.shape[0] // nch // (2 * tm)
    grid_spec = pltpu.PrefetchScalarGridSpec(
        num_scalar_prefetch=1,
        grid=(n_steps,),
        in_specs=[pl.BlockSpec((2 * tm * nch, LANES), lambda i, zs: (i, 0)), pl.BlockSpec(memory_space=pl.ANY)],
        out_specs=pl.BlockSpec(memory_space=pl.ANY),
        scratch_shapes=[pltpu.VMEM((zrows * nch, LANES), F32), pltpu.SMEM((tm * TOP_K,), I32),
                        pltpu.SMEM((tm * TOP_K,), I32), pltpu.SemaphoreType.DMA, pltpu.SemaphoreType.DMA,
                        pltpu.SemaphoreType.DMA, pltpu.SemaphoreType.DMA],
    )
    return pl.pallas_call(
        functools.partial(_disp_kernel, tm=tm, nch=nch, zrows=zrows, n_steps=n_steps),
        grid_spec=grid_spec,
        out_shape=jax.ShapeDtypeStruct((n_rows * nch, LANES), F32),
        compiler_params=_cparams(("arbitrary",)),
        name="disp",
    )(zero_start, h2, dest_flat)


def _store_token_major(ref, x):
    rows, d = x.shape
    nch = d // LANES
    for j in range(nch):
        ref[pl.ds(j, rows, stride=nch), :] = x[:, j * LANES:(j + 1) * LANES]


def _load_token_major(ref, rows, nch):
    return jnp.concatenate([ref[pl.ds(j, rows, stride=nch), :] for j in range(nch)], axis=1)


def _ffn_kernel(be_ref, nb_ref, nxt_ref, xs_ref, wgu_hbm, bgu_ref, wd_hbm, bd_ref, ys_ref,
                wgu_st, wd_st, wgu_bf, wd_bf, sem_gu, sem_d):
    i = pl.program_id(0)
    live = i < nb_ref[0]
    e = be_ref[i]

    def fetch(ex):
        return (pltpu.make_async_copy(wgu_hbm.at[ex], wgu_st, sem_gu),
                pltpu.make_async_copy(wd_hbm.at[ex], wd_st, sem_d))

    @pl.when(i == 0)
    def _():
        for cp in fetch(e):
            cp.start()

    @pl.when(live & ((i == 0) | (e != be_ref[jnp.maximum(i - 1, 0)])))
    def _():
        for cp in fetch(e):
            cp.wait()
        wgu_bf[...] = wgu_st[...].astype(BF16)
        wd_bf[...] = wd_st[...].astype(BF16)

        @pl.when(nxt_ref[i] >= 0)
        def _():
            for cp in fetch(nxt_ref[i]):
                cp.start()

    @pl.when(live)
    def _():
        xb = _load_token_major(xs_ref, ys_ref.shape[0] // NCH, NCH).astype(BF16)
        gu = jnp.dot(xb, wgu_bf[...], preferred_element_type=F32) + bgu_ref[...]
        gate = jnp.minimum(gu[:, :D_EXPERT], SWIGLU_LIMIT)
        up = jnp.clip(gu[:, D_EXPERT:], -SWIGLU_LIMIT, SWIGLU_LIMIT)
        glu = gate * (1.0 / (1.0 + jnp.exp(-SWIGLU_ALPHA * gate)))
        act = ((up + 1.0) * glu).astype(BF16)
        ys = jnp.dot(act, wd_bf[...], preferred_element_type=F32) + bd_ref[...]
        _store_token_major(ys_ref, ys)

    @pl.when(jnp.logical_not(live))
    def _():
        ys_ref[...] = jnp.zeros_like(ys_ref)


def _ffn(xs, block_e, n_blocks, next_e, w_gu, b_gu, w_down, b_down, tmb):
    D = D_MODEL
    P = xs.shape[0] // NCH
    E = w_gu.shape[0]
    blk = lambda i, be, nb, nx: (jnp.minimum(i, nb[0] - 1), 0)
    wsel = lambda i, be, nb, nx: (be[jnp.minimum(i, nb[0] - 1)], 0, 0)
    grid_spec = pltpu.PrefetchScalarGridSpec(
        num_scalar_prefetch=3,
        grid=(P // tmb,),
        in_specs=[pl.BlockSpec((tmb * NCH, LANES), blk),
                  pl.BlockSpec(memory_space=pl.ANY),
                  pl.BlockSpec((None, 1, 2 * D_EXPERT), wsel),
                  pl.BlockSpec(memory_space=pl.ANY),
                  pl.BlockSpec((None, 1, D), wsel)],
        out_specs=pl.BlockSpec((tmb * (D // LANES), LANES), lambda i, be, nb, nx: (i, 0)),
        scratch_shapes=[pltpu.VMEM((D, 2 * D_EXPERT), F32), pltpu.VMEM((D_EXPERT, D), F32),
                        pltpu.VMEM((D, 2 * D_EXPERT), BF16), pltpu.VMEM((D_EXPERT, D), BF16),
                        pltpu.SemaphoreType.DMA, pltpu.SemaphoreType.DMA],
    )
    return pl.pallas_call(
        _ffn_kernel,
        grid_spec=grid_spec,
        out_shape=jax.ShapeDtypeStruct((P * (D // LANES), LANES), F32),
        compiler_params=_cparams(("arbitrary",)),
        name="ffn",
    )(block_e, n_blocks, next_e, xs, w_gu, b_gu.reshape(E, 1, 2 * D_EXPERT), w_down, b_down.reshape(E, 1, D))


def _comb_kernel(x1_ref, gate_ref, g2_ref, fg_ref, dest_hbm, ys_hbm, o_ref, buf_a, buf_b, idx_a, idx_b,
                 sem_ia, sem_ib, sem_ra, sem_rb, *, tm, n_steps):
    i = pl.program_id(0)
    n = tm * TOP_K

    def idx_copy(tile, buf, sem):
        return pltpu.make_async_copy(dest_hbm.at[pl.ds(tile * n, n)], buf, sem)

    nch = x1_ref.shape[1] // LANES

    def row_copy(src, buf, k, t, sem):
        return pltpu.make_async_copy(ys_hbm.at[pl.ds(pl.multiple_of(src * nch, nch), nch)],
                                     buf.at[k, t[0], :, t[1]], sem)

    def issue_rows(idx, buf, sem):
        def body(r, c):
            for s in range(8):
                for k in range(TOP_K):
                    row_copy(idx[(r * 8 + s) * TOP_K + k], buf, k, (r, s), sem).start(priority=k % 2)
            return c

        lax.fori_loop(0, tm // 8, body, 0)

    def wait_rows(buf, sem):
        pltpu.make_async_copy(buf, buf, sem).wait()

    def finish(buf, lo):
        gates = gate_ref[lo:lo + tm, :]
        y = jnp.zeros((tm, x1_ref.shape[1]), F32)
        for k in range(TOP_K):
            rows = jnp.concatenate([buf[k, :, j].reshape(tm, LANES) for j in range(nch)], axis=1)
            y = y + gates[:, k:k + 1] * rows
        v = x1_ref[lo:lo + tm, :] + g2_ref[...] * y
        o_ref[lo:lo + tm, :] = v * lax.rsqrt(jnp.mean(v * v, axis=-1, keepdims=True) + EPS) * fg_ref[...]

    @pl.when(i == 0)
    def _():
        first = idx_copy(0, idx_a, sem_ia)
        first.start()
        first.wait()
        issue_rows(idx_a, buf_a, sem_ra)
        idx_copy(1, idx_b, sem_ib).start()

    idx_copy(2 * i + 1, idx_b, sem_ib).wait()
    issue_rows(idx_b, buf_b, sem_rb)

    @pl.when(i + 1 < n_steps)
    def _():
        idx_copy(2 * i + 2, idx_a, sem_ia).start()

    wait_rows(buf_a, sem_ra)
    finish(buf_a, 0)

    @pl.when(i + 1 < n_steps)
    def _():
        idx_copy(2 * i + 2, idx_a, sem_ia).wait()
        issue_rows(idx_a, buf_a, sem_ra)
        idx_copy(2 * i + 3, idx_b, sem_ib).start()

    wait_rows(buf_b, sem_rb)
    finish(buf_b, tm)


def _combine(x1, gate_tab, g2, final_g, dest_flat, ys, S, tm):
    N, D = x1.shape
    n_steps = N // (2 * tm)
    per_b = S // (2 * tm)
    return pl.pallas_call(
        functools.partial(_comb_kernel, tm=tm, n_steps=n_steps),
        grid=(n_steps,),
        in_specs=[pl.BlockSpec((2 * tm, D), lambda i: (i, 0)),
                  pl.BlockSpec((2 * tm, LANES), lambda i: (i, 0)),
                  pl.BlockSpec((None, 1, D), lambda i: (i // per_b, 0, 0)),
                  pl.BlockSpec((1, D), lambda i: (0, 0)),
                  pl.BlockSpec(memory_space=pl.ANY),
                  pl.BlockSpec(memory_space=pl.ANY)],
        out_specs=pl.BlockSpec((2 * tm, D), lambda i: (i, 0)),
        out_shape=jax.ShapeDtypeStruct((N, D), F32),
        scratch_shapes=[pltpu.VMEM((TOP_K, tm // 8, D // LANES, 8, LANES), F32),
                        pltpu.VMEM((TOP_K, tm // 8, D // LANES, 8, LANES), F32),
                        pltpu.SMEM((tm * TOP_K,), I32), pltpu.SMEM((tm * TOP_K,), I32),
                        pltpu.SemaphoreType.DMA, pltpu.SemaphoreType.DMA, pltpu.SemaphoreType.DMA,
                        pltpu.SemaphoreType.DMA],
        compiler_params=_cparams(("arbitrary",)),
        name="comb",
    )(x1, gate_tab, g2, final_g.reshape(1, D), dest_flat, ys)


def _tile(n, pref):
    t = min(pref, n)
    assert n % t == 0, (n, t)
    return t


def _layer(x, c, ada_w, ada_b, norm1_g, w_in, mix_scale, w_o, norm2_g,
           router_w, router_b, w_gu, b_gu, w_down, b_down, final_g):
    B, S, D = x.shape
    N = B * S
    mod = _mod(c, ada_w, ada_b).reshape(B, 6, 1, D)
    sh1, sc1, g1, sh2, sc2, g2 = (mod[:, j] for j in range(6))

    w_pad = jnp.pad(w_in, ((0, 0), (0, IN_COLS_PAD - IN_COLS))).astype(BF16)
    rq, rk, rv, rg, aqt, ak, avt, iqt, ik, iwt = _inproj(x, norm1_g, sc1, sh1, w_pad, _tile(S, PROJ_ROWS))
    ms = mix_scale.reshape(1, RET_W + DSA_W)
    ret = _retention(rq, rk, rv, rg, ms[:, :RET_W])
    att = _dsa(iqt, iwt, aqt, ik, ak, avt, ms[:, RET_W:], _tile(S, DSA_QUERIES), DSA_KEYS)

    rw_pad = jnp.pad(router_w, ((0, 0), (0, LANES - N_EXPERTS)))
    rw_hi = rw_pad.astype(BF16)
    rw_pad = jnp.stack([rw_hi, (rw_pad - rw_hi.astype(F32)).astype(BF16)])
    rb_pad = jnp.pad(router_b, (0, LANES - N_EXPERTS)).reshape(1, LANES)
    x1, h2, sel, idx_tab, gate_tab, counts = _oproj(ret, att, x, w_o.astype(BF16), g1, norm2_g, sc2, sh2,
                                                    rw_pad, rb_pad, _tile(S, PROJ_ROWS))

    tmb = FFN_ROWS
    n_rows = (N * TOP_K + N_EXPERTS * (tmb - 1)) // tmb * tmb + tmb
    cnt = counts[0, :N_EXPERTS].astype(I32)
    padded = (cnt + tmb - 1) // tmb * tmb
    ends = jnp.cumsum(padded)
    starts = ends - padded
    pstart = jnp.pad(starts.astype(F32), (0, LANES - N_EXPERTS)).reshape(1, LANES)
    n_blocks = (ends[-1] // tmb).reshape(1)
    first_row = jnp.arange(n_rows // tmb, dtype=I32) * tmb
    block_e = jnp.minimum(jnp.sum((ends[None, :] <= first_row[:, None]).astype(I32), axis=1), N_EXPERTS - 1)

    tmd = _tile(N, MOE_TOKENS)
    dest_tab = _dest(sel, idx_tab, pstart, tmd)
    dest_flat = dest_tab[:, :TOP_K].reshape(N * TOP_K)
    xs = _dispatch(h2, dest_flat, starts + cnt, n_rows, tmb, tmd)
    eid = jnp.arange(N_EXPERTS, dtype=I32)
    later_used = (eid[None, :] > eid[:, None]) & (padded[None, :] > 0)
    next_used = jnp.min(jnp.where(later_used, eid[None, :], N_EXPERTS), axis=1)
    next_e = jnp.where(next_used < N_EXPERTS, next_used, -1)[block_e].astype(I32)
    ys = _ffn(xs, block_e, n_blocks, next_e, w_gu, b_gu, w_down, b_down, tmb)
    out = _combine(x1, gate_tab, g2, final_g, dest_flat, ys, S, _tile(S, MOE_TOKENS))
    return out.reshape(B, S, D)


def kernel(x, c, ada_w, ada_b, norm1_g, w_in, mix_scale, w_o, norm2_g, router_w, router_b, w_gu, b_gu,
           w_down, b_down, final_g):
    assert ada_w.shape[0] == 1, "single-layer stack"
    return _layer(x, c, ada_w[0], ada_b[0], norm1_g[0], w_in[0], mix_scale[0], w_o[0], norm2_g[0],
                  router_w[0], router_b[0], w_gu[0], b_gu[0], w_down[0], b_down[0], final_g)
```

```python
import functools

import numpy as np
import jax
import jax.numpy as jnp
from jax import lax
from jax.experimental import pallas as pl
from jax.experimental.pallas import tpu as pltpu

F32 = jnp.float32
BF16 = jnp.bfloat16
I32 = jnp.int32

D_MODEL = 1024
RET_HEADS = 4
RET_DK = 64
RET_DV = 128
RET_CHUNK = 128
DSA_HEADS = 8
DSA_KV_HEADS = 2
DSA_HD = 64
IDX_HEADS = 8
IDX_HD = 64
TOPK_MAX = 256
N_EXPERTS = 32
TOP_K = 4
D_EXPERT = D_MODEL
SWIGLU_LIMIT = 7.0
SWIGLU_ALPHA = 1.702
EPS = 1e-6

RET_W = RET_HEADS * RET_DV
DSA_W = DSA_HEADS * DSA_HD
IN_COLS = 2888
IN_COLS_PAD = 2944

KAUG = 128
VAUG = 80
ALIBI_SPLIT = 64
ALIBI_TERMS = 3
LOG2E = 1.4426950408889634
PROJ_ROWS = 512
DSA_QUERIES = 512
DSA_KEYS = 128
MOE_TOKENS = 512
RET_CHUNKS_PER_STEP = 8
FFN_ROWS = 512
OPROJ_PARTS = 2
FIX_UNROLL = 4
BISECT_COARSE_STEPS = 8
BISECT_VALUE_STEPS = 8
BISECT_MAX_STEPS = 64

LANES = 128
NCH = D_MODEL // LANES
VMEM_LIMIT = 56 * 1024 * 1024
NEG_BIG = -1e30
F32_LOWEST = float(np.finfo(np.float32).min)


def _cparams(sem):
    return pltpu.CompilerParams(dimension_semantics=sem, vmem_limit_bytes=VMEM_LIMIT)


def _mod_kernel(c_ref, w_ref, b_ref, o_ref):
    c = c_ref[...]
    s = c * (1.0 / (1.0 + jnp.exp(-c)))
    o_ref[...] = jnp.dot(s, w_ref[...], preferred_element_type=F32,
                         precision=lax.Precision.HIGHEST) + b_ref[...]


def _mod(c, ada_w, ada_b):
    B, D = c.shape
    n_out = ada_w.shape[1]
    rows = 8
    c8 = jnp.zeros((rows, D), F32).at[:B].set(c)
    out = pl.pallas_call(
        _mod_kernel,
        grid=(n_out // D,),
        in_specs=[pl.BlockSpec((rows, D), lambda j: (0, 0)),
                  pl.BlockSpec((D, D), lambda j: (0, j)),
                  pl.BlockSpec((1, D), lambda j: (0, j))],
        out_specs=pl.BlockSpec((rows, D), lambda j: (0, j)),
        out_shape=jax.ShapeDtypeStruct((rows, n_out), F32),
        compiler_params=_cparams(("arbitrary",)),
        name="mod",
    )(c8, ada_w, ada_b.reshape(1, n_out))
    return out[:B]


def _inproj_kernel(x_ref, g_ref, sc_ref, sh_ref, w_ref,
                   rq_ref, rk_ref, rv_ref, rg_ref, aqt_ref, ak_ref, avt_ref, iqt_ref, ik_ref, iwt_ref):
    x = x_ref[...]
    ms = jnp.mean(x * x, axis=-1, keepdims=True)
    y = x * lax.rsqrt(ms + EPS) * g_ref[...]
    hb = (y * (1.0 + sc_ref[...]) + sh_ref[...]).astype(BF16)

    def proj(lo, hi):
        return jnp.dot(hb, w_ref[:, lo:hi], preferred_element_type=F32)

    tm = x.shape[0]
    d = DSA_HD
    rq_ref[...] = proj(0, 256).astype(BF16)
    rk_ref[...] = (proj(256, 512) * (RET_DK ** -0.5)).astype(BF16)
    rv_ref[...] = proj(512, 1024).astype(BF16)
    rg_ref[...] = proj(1024, 1536).astype(BF16)
    aqt_ref[...] = (proj(1536, 2048) * (d ** -0.5 * LOG2E)).T.astype(BF16)
    kk = proj(2048, 2176)
    pos = pl.program_id(1) * tm + lax.broadcasted_iota(I32, (tm, d), 0)
    col = lax.broadcasted_iota(I32, (tm, d), 1)
    posblk = jnp.where(col < ALIBI_TERMS, pos // ALIBI_SPLIT,
                       jnp.where(col < 2 * ALIBI_TERMS, pos % ALIBI_SPLIT, 0)).astype(F32)
    for g in range(DSA_KV_HEADS):
        ak_ref[:, g * KAUG:g * KAUG + d] = kk[:, g * d:(g + 1) * d].astype(BF16)
        ak_ref[:, g * KAUG + d:(g + 1) * KAUG] = posblk.astype(BF16)
    vt = proj(2176, 2304).T
    r16 = lax.broadcasted_iota(I32, (VAUG - d, tm), 0)
    onesblk = jnp.where(r16 == 0, 1.0, 0.0).astype(BF16)
    for g in range(DSA_KV_HEADS):
        avt_ref[g * VAUG:g * VAUG + d, :] = vt[g * d:(g + 1) * d, :].astype(BF16)
        avt_ref[g * VAUG + d:(g + 1) * VAUG, :] = onesblk
    iqt_ref[...] = proj(2304, 2816).T.astype(BF16)
    last = proj(2816, 2944)
    ik_ref[...] = last[:, :IDX_HD].astype(BF16)
    iwt_ref[...] = last.T[IDX_HD:IDX_HD + IDX_HEADS, :] * ((IDX_HD ** -0.5) * (IDX_HEADS ** -0.5))


def _inproj(x, norm_g, sc, sh, w_pad, tm):
    B, S, D = x.shape
    row = lambda w: pl.BlockSpec((None, tm, w), lambda b, i: (b, i, 0))
    colT = lambda h: pl.BlockSpec((None, h, tm), lambda b, i: (b, 0, i))
    vec = pl.BlockSpec((None, 1, D), lambda b, i: (b, 0, 0))
    sd = lambda shape, dt: jax.ShapeDtypeStruct(shape, dt)
    G = DSA_KV_HEADS
    return pl.pallas_call(
        _inproj_kernel,
        grid=(B, S // tm),
        in_specs=[row(D), pl.BlockSpec((1, D), lambda b, i: (0, 0)), vec, vec,
                  pl.BlockSpec((D, IN_COLS_PAD), lambda b, i: (0, 0))],
        out_specs=[row(256), row(256), row(512), row(512), colT(DSA_W), row(G * KAUG), colT(G * VAUG),
                   colT(IDX_HEADS * IDX_HD), row(IDX_HD), colT(IDX_HEADS)],
        out_shape=[sd((B, S, 256), BF16), sd((B, S, 256), BF16), sd((B, S, 512), BF16),
                   sd((B, S, 512), BF16), sd((B, DSA_W, S), BF16), sd((B, S, G * KAUG), BF16),
                   sd((B, G * VAUG, S), BF16), sd((B, IDX_HEADS * IDX_HD, S), BF16),
                   sd((B, S, IDX_HD), BF16), sd((B, IDX_HEADS, S), F32)],
        compiler_params=_cparams(("parallel", "parallel")),
        name="inproj",
    )(x, norm_g.reshape(1, D), sc, sh, w_pad)


def _ret_kernel(rq_ref, rk_ref, rv_ref, rg_ref, din_ref, qd_ref, kd_ref, cd_ref, ms_ref, o_ref, state_ref):
    @pl.when(pl.program_id(1) == 0)
    def _():
        state_ref[...] = jnp.zeros_like(state_ref)

    C = din_ref.shape[1]
    for c in range(rq_ref.shape[0] // C):
        rows = slice(c * C, (c + 1) * C)
        for h in range(RET_HEADS):
            q = rq_ref[rows, h * RET_DK:(h + 1) * RET_DK]
            k = rk_ref[rows, h * RET_DK:(h + 1) * RET_DK]
            v = rv_ref[rows, h * RET_DV:(h + 1) * RET_DV]
            r_prev = state_ref[h]
            s = lax.dot_general(q, k, (((1,), (1,)), ((), ())), preferred_element_type=F32) * din_ref[h]
            o = jnp.dot(s.astype(BF16), v, preferred_element_type=F32)
            o = o + jnp.dot(q, r_prev.astype(BF16), preferred_element_type=F32) * qd_ref[h]
            vd = (v.astype(F32) * kd_ref[h]).astype(BF16)
            kv = lax.dot_general(k, vd, (((0,), (0,)), ((), ())), preferred_element_type=F32)
            state_ref[h] = r_prev * cd_ref[h] + kv
            o = o * lax.rsqrt(jnp.mean(o * o, axis=-1, keepdims=True) + EPS)
            g = rg_ref[rows, h * RET_DV:(h + 1) * RET_DV].astype(F32)
            gate = g * (1.0 / (1.0 + jnp.exp(-g)))
            o_ref[rows, h * RET_DV:(h + 1) * RET_DV] = (
                gate * o * ms_ref[:, h * RET_DV:(h + 1) * RET_DV]).astype(BF16)


def _ret_consts(C):
    H = RET_HEADS
    log_g = np.log1p(-np.exp2(-5.0 - np.arange(H, dtype=np.float64)))
    pos = np.arange(C, dtype=np.float64)
    diff = pos[:, None] - pos[None, :]
    d_inner = np.where(diff[None] >= 0, np.exp(np.maximum(diff, 0.0)[None] * log_g[:, None, None]), 0.0)
    q_decay = np.exp((pos + 1.0)[None] * log_g[:, None])
    k_decay = np.exp((C - 1.0 - pos)[None] * log_g[:, None])
    chunk_decay = np.exp(C * log_g)
    qd = np.broadcast_to(q_decay[:, :, None], (H, C, RET_DV))
    kd = np.broadcast_to(k_decay[:, :, None], (H, C, RET_DV))
    cd = np.broadcast_to(chunk_decay[:, None, None], (H, 1, RET_DV))
    f = lambda a: jnp.asarray(np.ascontiguousarray(a), F32)
    return f(d_inner), f(qd), f(kd), f(cd)


def _retention(rq, rk, rv, rg, ms_ret):
    B, S, _ = rq.shape
    C = min(RET_CHUNK, S)
    din, qd, kd, cd = _ret_consts(C)
    rows = _tile(S, RET_CHUNKS_PER_STEP * C)
    row = lambda w: pl.BlockSpec((None, rows, w), lambda b, n: (b, n, 0))
    full = lambda a: pl.BlockSpec(a.shape, lambda b, n: (0,) * a.ndim)
    return pl.pallas_call(
        _ret_kernel,
        grid=(B, S // rows),
        in_specs=[row(256), row(256), row(512), row(512), full(din), full(qd), full(kd), full(cd),
                  pl.BlockSpec((1, RET_W), lambda b, n: (0, 0))],
        out_specs=row(RET_W),
        out_shape=jax.ShapeDtypeStruct((B, S, RET_W), BF16),
        scratch_shapes=[pltpu.VMEM((RET_HEADS, RET_DK, RET_DV), F32)],
        compiler_params=_cparams(("parallel", "arbitrary")),
        name="ret",
    )(rq, rk, rv, rg, din, qd, kd, cd, ms_ret)


def _f32_key(x):
    i = lax.bitcast_convert_type(x, I32)
    return i ^ ((i >> 31) & 0x7FFFFFFF)


def _key_f32(k):
    return lax.bitcast_convert_type(k ^ ((k >> 31) & 0x7FFFFFFF), F32)


def _floor16(x):
    i = lax.bitcast_convert_type(x, I32)
    return lax.bitcast_convert_type((i + ((i >> 31) & 0xFFFF)) & -0x10000, F32)


def _trunc16(x):
    return lax.bitcast_convert_type(lax.bitcast_convert_type(x, I32) & -0x10000, F32)


def _dsa_kernel(iqt_ref, iwt_ref, aqt_ref, ik_ref, ak_ref, avt_ref, ms_ref, o_ref, score_ref, code_ref, qa_ref, sa_ref, sb_ref, mask_ref, *acc_refs,
                tq, tks, n_sel):
    H, G, R, d = DSA_HEADS, DSA_KV_HEADS, DSA_HEADS // DSA_KV_HEADS, DSA_HD
    t0 = pl.program_id(1) * tq
    nsub = (t0 + tq) // tks
    tka = 2 * tks
    npair = ((t0 + tq) // tka + 1) // 2
    kf = float(n_sel)
    qpos = t0 + lax.broadcasted_iota(I32, (1, tq), 1)
    krow = lax.broadcasted_iota(I32, (tks, tq), 0)

    wrow = [iwt_ref[h:h + 1, :] for h in range(IDX_HEADS)]

    def fold8(x, op):
        acc = x[0:8, :]
        for i in range(1, tks // 8):
            acc = op(acc, x[8 * i:8 * (i + 1), :])
        return acc

    def score_pair(i, carry, masked):
        mx, mn, npos, nnon = carry
        for u in range(2 * tka // tks):
            r0 = pl.multiple_of(i * 2 * tka + u * tks, tks)
            kc = ik_ref[pl.ds(r0, tks), :]
            acc = jnp.zeros((tks, tq), F32)
            for h in range(IDX_HEADS):
                rel = jnp.dot(kc, iqt_ref[h * IDX_HD:(h + 1) * IDX_HD, :], preferred_element_type=F32)
                acc = acc + jnp.maximum(rel, 0.0) * wrow[h]
            if masked:
                causal = r0 + krow <= qpos
                sc = jnp.where(causal, acc, -jnp.inf)
                lowest = jnp.where(causal, acc, jnp.inf)
            else:
                sc = lowest = acc
            score_ref[pl.ds(r0, tks), :] = sc
            code_ref[pl.ds(r0, tks), :] = _trunc16(sc).astype(BF16)
            mx = jnp.maximum(mx, fold8(sc, jnp.maximum))
            mn = jnp.minimum(mn, fold8(lowest, jnp.minimum))
            npos = npos + fold8(jnp.where(sc > 0.0, 1.0, 0.0), jnp.add)
            nnon = nnon + fold8(jnp.where(sc >= 0.0, 1.0, 0.0), jnp.add)
        return mx, mn, npos, nnon

    stat0 = (jnp.full((8, tq), -jnp.inf, F32), jnp.full((8, tq), jnp.inf, F32),
             jnp.zeros((8, tq), F32), jnp.zeros((8, tq), F32))
    n_inner = (t0 + 1) // (2 * tka)
    stat = lax.fori_loop(0, n_inner, functools.partial(score_pair, masked=False), stat0)
    mx, mn, npos, nnon = lax.fori_loop(n_inner, npair, functools.partial(score_pair, masked=True), stat)
    top = jnp.max(mx, axis=0, keepdims=True)
    lo0 = jnp.min(mn, axis=0, keepdims=True)
    n_pos = jnp.sum(npos, axis=0, keepdims=True)
    n_nonneg = jnp.sum(nnon, axis=0, keepdims=True)


    def count(th, strict):
        def body(j, acc):
            for u in range(2):
                s = score_ref[pl.ds(pl.multiple_of((2 * j + u) * tks, tks), tks), :]
                hit = (s > th) if strict else (s >= th)
                acc = acc + fold8(jnp.where(hit, 1.0, 0.0), jnp.add)
            return acc

        acc = lax.fori_loop(0, nsub // 2, body, jnp.zeros((8, tq), F32))
        return jnp.sum(acc, axis=0, keepdims=True)

    def probe(lo, hi, it):
        lk, hk = _f32_key(lo), _f32_key(hi)
        mk = (lk >> 1) + (hk >> 1) + (lk & hk & 1)
        mv = lo + (hi - lo) * 0.5
        early = (jnp.zeros((1, tq), I32) + it) < BISECT_VALUE_STEPS
        mid = jnp.where(early & (mv > lo) & (mv < hi), mv, _key_f32(mk))
        return mid, jnp.max(jnp.where(mk != lk, 1.0, 0.0))

    def bis_cond(c):
        return (c[5] > 0.0) & (c[6] < BISECT_MAX_STEPS)

    def bis_body(c):
        lo, hi, c_lo, c_hi, mid, _, it = c
        cnt = count(mid, False)
        ge = cnt >= kf
        up = ge | (cnt == kf)
        dn = (~ge) | (cnt == kf)
        lo, c_lo = jnp.where(up, mid, lo), jnp.where(up, cnt, c_lo)
        hi, c_hi = jnp.where(dn, mid, hi), jnp.where(dn, cnt, c_hi)
        mid, active = probe(lo, hi, it + 1)
        return lo, hi, c_lo, c_hi, mid, active, it + 1

    zero = jnp.zeros((1, tq), F32)
    keep_all = qpos + 1 <= n_sel
    settled = keep_all | ((n_nonneg >= kf) & (n_pos < kf))
    above = n_pos >= kf
    c_lo0 = jnp.where(settled | above, n_nonneg, (qpos + 1).astype(F32))
    c_hi0 = jnp.where(settled | ~above, n_nonneg, zero)
    lo0 = jnp.where(settled | above, zero, lo0)
    hi0 = jnp.where(settled | ~above, zero, _key_f32(_f32_key(top) + 1))
    def count16(m):
        mb = m.astype(BF16)

        def body(j, acc):
            for u in range(2):
                c = code_ref[pl.ds(pl.multiple_of((2 * j + u) * tks, tks), tks), :]
                one = jnp.where(c >= mb, jnp.ones((), BF16), jnp.zeros((), BF16))
                part = one[0:16, :]
                for i in range(1, tks // 16):
                    part = part + one[16 * i:16 * (i + 1), :]
                acc = acc + part.astype(F32)
            return acc

        acc = lax.fori_loop(0, nsub // 2, body, jnp.zeros((16, tq), F32))
        return jnp.sum(acc, axis=0, keepdims=True)

    def coarse_step(_, c):
        lo, hi, c_lo, c_hi = c
        below = _floor16(lo + (hi - lo) * 0.5)
        above = _key_f32(_f32_key(below) + 0x10000)
        m = jnp.where(below > lo, below, above)
        v = jnp.where(m > 0.0, m, _key_f32(_f32_key(m) - 0x10000 + 1))
        ok = (v > lo) & (v < hi)
        cnt = count16(m)
        ge = cnt >= kf
        up = ok & ge
        dn = ok & ((~ge) | (cnt == kf))
        return (jnp.where(up, v, lo), jnp.where(dn, v, hi), jnp.where(up, cnt, c_lo), jnp.where(dn, cnt, c_hi))

    lo0, hi0, c_lo0, c_hi0 = lax.fori_loop(0, BISECT_COARSE_STEPS, coarse_step, (lo0, hi0, c_lo0, c_hi0))
    it0 = jnp.int32(BISECT_VALUE_STEPS)
    mid0, active0 = probe(lo0, hi0, it0)
    lo, hi, c_lo, c_hi, _, _, _ = lax.while_loop(bis_cond, bis_body,
                                                 (lo0, hi0, c_lo0, c_hi0, mid0, active0, it0))
    at_hi = c_hi >= kf
    thr = jnp.where(keep_all, F32_LOWEST, jnp.where(at_hi, hi, lo))
    excess = jnp.where(keep_all, 0.0, jnp.where(at_hi, c_hi, c_lo) - kf)

    @pl.when(jnp.max(excess) > 0.0)
    def _():
        tied_nonzero = jnp.max(jnp.where((excess > 0.0) & (thr != 0.0), 1.0, 0.0)) > 0.0
        n_above = lax.cond(tied_nonzero, lambda: count(thr, True), lambda: n_pos)
        budget = jnp.where(excess > 0.0, kf - jnp.where(thr == 0.0, n_pos, n_above), jnp.inf)
        earlier = lax.broadcasted_iota(I32, (tks, tks), 1) < lax.broadcasted_iota(I32, (tks, tks), 0)
        earlier = jnp.where(earlier, 1.0, 0.0).astype(BF16)

        def fix(j, seen):
            tiles = []
            for u in range(FIX_UNROLL):
                r0 = pl.multiple_of((FIX_UNROLL * j + u) * tks, tks)
                s = score_ref[pl.ds(r0, tks), :]
                eq = s == thr
                eqf = jnp.where(eq, 1.0, 0.0)
                within = jnp.dot(earlier, eqf.astype(BF16), preferred_element_type=F32)
                tiles.append((r0, s, eq, within, jnp.sum(eqf, axis=0, keepdims=True)))
            for r0, s, eq, within, n_eq in tiles:
                score_ref[pl.ds(r0, tks), :] = jnp.where(eq & (within + seen >= budget), -jnp.inf, s)
                seen = seen + n_eq
            return seen

        lax.fori_loop(0, nsub // FIX_UNROLL, fix, jnp.zeros((1, tq), F32))

    arow = lax.broadcasted_iota(I32, (KAUG - d, tq), 0)
    for h in range(H):
        terms, rest = [], 2.0 ** (-8.0 * (h + 1) / H) * LOG2E
        for _ in range(ALIBI_TERMS):
            terms.append(float(np.asarray(rest, np.float32).astype(BF16).astype(np.float64)))
            rest -= terms[-1]
        rows = jnp.zeros((KAUG - d, tq), F32)
        for i, c in enumerate(terms):
            rows = jnp.where(arow == i, c * ALIBI_SPLIT, jnp.where(arow == ALIBI_TERMS + i, c, rows))
        qa_ref[h, 0:d, :] = aqt_ref[h * d:(h + 1) * d, :]
        qa_ref[h, d:KAUG, :] = rows.astype(BF16)
    for acc in acc_refs:
        acc[...] = jnp.zeros_like(acc)

    def logits(j, h):
        ka = ak_ref[pl.ds(pl.multiple_of(j * tka, tka), tka), (h // R) * KAUG:(h // R + 1) * KAUG]
        return jnp.dot(ka, qa_ref[h], preferred_element_type=F32)

    def step(j, j_next, cur_ref, next_ref, ms):
        r0 = pl.multiple_of(j * tka, tka)
        mask_ref[...] = jnp.where(score_ref[pl.ds(r0, tka), :] >= thr, 0.0, NEG_BIG)
        nms = []
        ahead = logits(j_next, 0)
        for h in range(H):
            g = h // R
            next_ref[h] = ahead
            if h + 1 < H:
                ahead = logits(j_next, h + 1)
            s = cur_ref[h] + mask_ref[...]
            m_new = jnp.maximum(ms[h], jnp.max(s, axis=0, keepdims=True))
            p = jnp.exp2(s - m_new).astype(BF16)
            va = avt_ref[g * VAUG:(g + 1) * VAUG, pl.ds(r0, tka)]
            acc = acc_refs[h]
            acc[...] = acc[...] * jnp.exp2(ms[h] - m_new) + jnp.dot(va, p, preferred_element_type=F32)
            nms.append(m_new)
        return tuple(nms)

    for h in range(H):
        sa_ref[h] = logits(0, h)

    def att_pair(i, ms):
        ms = step(2 * i, 2 * i + 1, sa_ref, sb_ref, ms)
        return step(2 * i + 1, jnp.minimum(2 * i + 2, 2 * npair - 1), sb_ref, sa_ref, ms)

    lax.fori_loop(0, npair, att_pair, tuple(jnp.full((1, tq), NEG_BIG, F32) for _ in range(H)))
    for h in range(H):
        a = acc_refs[h][...]
        o = a[0:d, :] / a[d:d + 1, :]
        o_ref[h * d:(h + 1) * d, :] = (o * ms_ref[h * d:(h + 1) * d, :]).astype(BF16)


def _dsa(iqt, iwt, aqt, ik, ak, avt, ms_att, tq, tks):
    B, _, S = iqt.shape
    assert S <= ALIBI_SPLIT * 256 and tq % (2 * tks) == 0 and tq % (FIX_UNROLL * tks) == 0
    assert (S // (2 * tks)) % 2 == 0
    n_sel = min(TOPK_MAX, S // 4)
    G = DSA_KV_HEADS
    colT = lambda h: pl.BlockSpec((None, h, tq), lambda b, i: (b, 0, i))
    msb = jnp.broadcast_to(ms_att.reshape(DSA_W, 1), (DSA_W, tq))
    return pl.pallas_call(
        functools.partial(_dsa_kernel, tq=tq, tks=tks, n_sel=n_sel),
        grid=(B, S // tq),
        in_specs=[colT(IDX_HEADS * IDX_HD), colT(IDX_HEADS), colT(DSA_W),
                  pl.BlockSpec((None, S, IDX_HD), lambda b, i: (b, 0, 0), pipeline_mode=pl.Buffered(1)),
                  pl.BlockSpec((None, S, G * KAUG), lambda b, i: (b, 0, 0), pipeline_mode=pl.Buffered(1)),
                  pl.BlockSpec((None, G * VAUG, S), lambda b, i: (b, 0, 0), pipeline_mode=pl.Buffered(1)),
                  pl.BlockSpec((DSA_W, tq), lambda b, i: (0, 0))],
        out_specs=colT(DSA_W),
        out_shape=jax.ShapeDtypeStruct((B, DSA_W, S), BF16),
        scratch_shapes=[pltpu.VMEM((S, tq), F32), pltpu.VMEM((S, tq), BF16), pltpu.VMEM((DSA_HEADS, KAUG, tq), BF16),
                        pltpu.VMEM((DSA_HEADS, 2 * tks, tq), F32), pltpu.VMEM((DSA_HEADS, 2 * tks, tq), F32),
                        pltpu.VMEM((2 * tks, tq), F32)]
        + [pltpu.VMEM((VAUG, tq), F32) for _ in range(DSA_HEADS)],
        compiler_params=_cparams(("parallel", "arbitrary")),
        name="dsa",
    )(iqt, iwt, aqt, ik, ak, avt, msb)


def _oproj_kernel(ret_ref, att_ref, x_ref, wo_ref, g1_ref, n2_ref, sc_ref, sh_ref, rw_ref, rb_ref,
                  x1_ref, h2_ref, sel_ref, idx_ref, gate_ref, cnt_ref):
    tm = x_ref.shape[0]
    parts = [slice(r, r + tm // OPROJ_PARTS) for r in range(0, tm, tm // OPROJ_PARTS)]
    nch = x_ref.shape[1] // LANES

    def mix(rows):
        out = jnp.dot(ret_ref[rows, :], wo_ref[:RET_W, :], preferred_element_type=F32)
        return out + lax.dot_general(att_ref[:, rows], wo_ref[RET_W:, :], (((0,), (0,)), ((), ())),
                                     preferred_element_type=F32)

    def hidden(rows, mixo):
        x1 = x_ref[rows, :] + g1_ref[...] * mixo
        x1_ref[rows, :] = x1
        y = x1 * lax.rsqrt(jnp.mean(x1 * x1, axis=-1, keepdims=True) + EPS) * n2_ref[...]
        h2 = y * (1.0 + sc_ref[...]) + sh_ref[...]
        for j in range(nch):
            h2_ref[pl.ds(rows.start * nch + j, rows.stop - rows.start, stride=nch), :] = h2[:, j * LANES:(j + 1) * LANES]
        return h2

    def router(h2):
        h_hi = h2.astype(BF16)
        h_lo = (h2 - h_hi.astype(F32)).astype(BF16)
        logits = jnp.dot(h_hi, rw_ref[0], preferred_element_type=F32)
        return logits + (jnp.dot(h_hi, rw_ref[1], preferred_element_type=F32)
                         + jnp.dot(h_lo, rw_ref[0], preferred_element_type=F32)) + rb_ref[...]

    def top4(rows, logits):
        n = logits.shape[0]
        lane = lax.broadcasted_iota(I32, (n, LANES), 1).astype(F32)
        work = jnp.where(lane < N_EXPERTS, logits, -jnp.inf)
        sel = jnp.zeros((n, LANES), F32)
        idx_tab = jnp.zeros((n, LANES), F32)
        vals = []
        for k in range(TOP_K):
            m = jnp.max(work, axis=1, keepdims=True)
            idx = jnp.min(jnp.where(work == m, lane, float(LANES)), axis=1, keepdims=True)
            hit = lane == idx
            sel = jnp.where(hit, 1.0, sel)
            idx_tab = jnp.where(lane == k, idx, idx_tab)
            work = jnp.where(hit, -jnp.inf, work)
            vals.append(m)
        es = [jnp.exp(v - vals[0]) for v in vals]
        den = es[0] + es[1] + es[2] + es[3]
        gate_tab = jnp.zeros((n, LANES), F32)
        for k in range(TOP_K):
            gate_tab = jnp.where(lane == k, es[k] / den, gate_tab)
        sel_ref[rows, :] = sel
        idx_ref[rows, :] = idx_tab
        gate_ref[rows, :] = gate_tab
        return jnp.sum(sel, axis=0, keepdims=True)

    mixes = [mix(rows) for rows in parts]
    hiddens = [hidden(rows, m) for rows, m in zip(parts, mixes)]
    logit_parts = [router(h2) for h2 in hiddens]
    counts = [top4(rows, lg) for rows, lg in zip(parts, logit_parts)]

    @pl.when((pl.program_id(0) == 0) & (pl.program_id(1) == 0))
    def _():
        cnt_ref[...] = jnp.zeros_like(cnt_ref)

    cnt_ref[...] += functools.reduce(lambda a, b: a + b, counts)


def _oproj(ret, att, x, wo, g1, n2g, sc2, sh2, rw_pad, rb_pad, tm):
    B, S, D = x.shape
    nt = S // tm
    row = lambda w: pl.BlockSpec((None, tm, w), lambda b, i: (b, i, 0))
    flat = lambda w: pl.BlockSpec((tm, w), lambda b, i: (b * nt + i, 0))
    vec = pl.BlockSpec((None, 1, D), lambda b, i: (b, 0, 0))
    cst = lambda shape: pl.BlockSpec(shape, lambda b, i: (0, 0))
    sd = lambda shape, dt: jax.ShapeDtypeStruct(shape, dt)
    N = B * S
    return pl.pallas_call(
        _oproj_kernel,
        grid=(B, nt),
        in_specs=[row(RET_W), pl.BlockSpec((None, DSA_W, tm), lambda b, i: (b, 0, i)), row(D), cst((D, D)), vec,
                  cst((1, D)), vec, vec,
                  pl.BlockSpec((2, D, LANES), lambda b, i: (0, 0, 0)), cst((1, LANES))],
        out_specs=[flat(D), pl.BlockSpec((tm * (D // LANES), LANES), lambda b, i: (b * nt + i, 0)),
                   flat(LANES), flat(LANES), flat(LANES), cst((1, LANES))],
        out_shape=[sd((N, D), F32), sd((N * (D // LANES), LANES), F32), sd((N, LANES), F32), sd((N, LANES), F32),
                   sd((N, LANES), F32), sd((1, LANES), F32)],
        compiler_params=_cparams(("arbitrary", "arbitrary")),
        name="oproj",
    )(ret, att, x, wo, g1, n2g.reshape(1, D), sc2, sh2, rw_pad, rb_pad)


def _dest_kernel(sel_ref, idx_ref, pstart_ref, dest_ref, seen_ref):
    @pl.when(pl.program_id(0) == 0)
    def _():
        seen_ref[...] = jnp.zeros_like(seen_ref)

    sel = sel_ref[...]
    tm = sel.shape[0]
    earlier = lax.broadcasted_iota(I32, (tm, tm), 1) < lax.broadcasted_iota(I32, (tm, tm), 0)
    earlier = jnp.where(earlier, 1.0, 0.0).astype(BF16)
    rank = jnp.dot(earlier, sel.astype(BF16), preferred_element_type=F32) + seen_ref[...]
    dest = pstart_ref[...] + rank
    lane = lax.broadcasted_iota(I32, (tm, LANES), 1).astype(F32)
    idx_tab = idx_ref[...]
    out = jnp.zeros((tm, LANES), F32)
    for k in range(TOP_K):
        e_k = jnp.sum(jnp.where(lane == k, idx_tab, 0.0), axis=1, keepdims=True)
        d_k = jnp.sum(jnp.where(lane == e_k, dest, 0.0), axis=1, keepdims=True)
        out = jnp.where(lane == k, d_k, out)
    dest_ref[...] = out.astype(I32)
    seen_ref[...] += jnp.sum(sel, axis=0, keepdims=True)


def _dest(sel, idx_tab, pstart, tm):
    N = sel.shape[0]
    blk = pl.BlockSpec((tm, LANES), lambda i: (i, 0))
    return pl.pallas_call(
        _dest_kernel,
        grid=(N // tm,),
        in_specs=[blk, blk, pl.BlockSpec((1, LANES), lambda i: (0, 0))],
        out_specs=blk,
        out_shape=jax.ShapeDtypeStruct((N, LANES), I32),
        scratch_shapes=[pltpu.VMEM((1, LANES), F32)],
        compiler_params=_cparams(("arbitrary",)),
        name="dest",
    )(sel, idx_tab, pstart)


def _disp_kernel(zs_ref, h2_ref, dest_hbm, xs_hbm, zbuf, idx_a, idx_b, sem_ia, sem_ib, sem_row, sem_z,
                 *, tm, nch, zrows, n_steps):
    i = pl.program_id(0)
    n = tm * TOP_K

    def idx_copy(tile, buf, sem):
        return pltpu.make_async_copy(dest_hbm.at[pl.ds(tile * n, n)], buf, sem)

    def rows(ref, first, count):
        return ref.at[pl.ds(pl.multiple_of(first * nch, nch), count * nch)]

    def row_copy(row, dst):
        return pltpu.make_async_copy(rows(h2_ref, row, 1), rows(xs_hbm, dst, 1), sem_row)

    def issue_rows(first_row, idx):
        def body(r, c):
            for s in range(8):
                t = r * 8 + s
                for k in range(TOP_K):
                    row_copy(first_row + t, idx[t * TOP_K + k]).start(priority=k % 2)
            return c

        lax.fori_loop(0, tm // 8, body, 0)

    def wait_rows():
        pltpu.make_async_copy(rows(xs_hbm, 0, n), rows(xs_hbm, 0, n), sem_row).wait()

    @pl.when(i == 0)
    def _():
        zbuf[...] = jnp.zeros_like(zbuf)
        for e in range(N_EXPERTS):
            fill = pltpu.make_async_copy(zbuf, rows(xs_hbm, zs_ref[e], zrows), sem_z)
            fill.start()
            fill.wait()
        idx_copy(0, idx_a, sem_ia).start()
        idx_copy(1, idx_b, sem_ib).start()

    idx_copy(2 * i, idx_a, sem_ia).wait()
    issue_rows(0, idx_a)
    idx_copy(2 * i + 1, idx_b, sem_ib).wait()
    issue_rows(tm, idx_b)

    @pl.when(i + 1 < n_steps)
    def _():
        idx_copy(2 * i + 2, idx_a, sem_ia).start()
        idx_copy(2 * i + 3, idx_b, sem_ib).start()

    wait_rows()
    wait_rows()


def _dispatch(h2, dest_flat, zero_start, n_rows, zrows, tm):
    nch = D_MODEL // LANES
    n_steps = h2.shape[0] // nch // (2 * tm)
    grid_spec = pltpu.PrefetchScalarGridSpec(
        num_scalar_prefetch=1,
        grid=(n_steps,),
        in_specs=[pl.BlockSpec((2 * tm * nch, LANES), lambda i, zs: (i, 0)), pl.BlockSpec(memory_space=pl.ANY)],
        out_specs=pl.BlockSpec(memory_space=pl.ANY),
        scratch_shapes=[pltpu.VMEM((zrows * nch, LANES), F32), pltpu.SMEM((tm * TOP_K,), I32),
                        pltpu.SMEM((tm * TOP_K,), I32), pltpu.SemaphoreType.DMA, pltpu.SemaphoreType.DMA,
                        pltpu.SemaphoreType.DMA, pltpu.SemaphoreType.DMA],
    )
    return pl.pallas_call(
        functools.partial(_disp_kernel, tm=tm, nch=nch, zrows=zrows, n_steps=n_steps),
        grid_spec=grid_spec,
        out_shape=jax.ShapeDtypeStruct((n_rows * nch, LANES), F32),
        compiler_params=_cparams(("arbitrary",)),
        name="disp",
    )(zero_start, h2, dest_flat)


def _store_token_major(ref, x):
    rows, d = x.shape
    nch = d // LANES
    for j in range(nch):
        ref[pl.ds(j, rows, stride=nch), :] = x[:, j * LANES:(j + 1) * LANES]


def _load_token_major(ref, rows, nch):
    return jnp.concatenate([ref[pl.ds(j, rows, stride=nch), :] for j in range(nch)], axis=1)


def _ffn_kernel(be_ref, nb_ref, nxt_ref, xs_ref, wgu_hbm, bgu_ref, wd_hbm, bd_ref, ys_ref,
                wgu_st, wd_st, wgu_bf, wd_bf, sem_gu, sem_d):
    i = pl.program_id(0)
    live = i < nb_ref[0]
    e = be_ref[i]

    def fetch(ex):
        return (pltpu.make_async_copy(wgu_hbm.at[ex], wgu_st, sem_gu),
                pltpu.make_async_copy(wd_hbm.at[ex], wd_st, sem_d))

    @pl.when(i == 0)
    def _():
        for cp in fetch(e):
            cp.start()

    @pl.when(live & ((i == 0) | (e != be_ref[jnp.maximum(i - 1, 0)])))
    def _():
        for cp in fetch(e):
            cp.wait()
        wgu_bf[...] = wgu_st[...].astype(BF16)
        wd_bf[...] = wd_st[...].astype(BF16)

        @pl.when(nxt_ref[i] >= 0)
        def _():
            for cp in fetch(nxt_ref[i]):
                cp.start()

    @pl.when(live)
    def _():
        xb = _load_token_major(xs_ref, ys_ref.shape[0] // NCH, NCH).astype(BF16)
        gu = jnp.dot(xb, wgu_bf[...], preferred_element_type=F32) + bgu_ref[...]
        gate = jnp.minimum(gu[:, :D_EXPERT], SWIGLU_LIMIT)
        up = jnp.clip(gu[:, D_EXPERT:], -SWIGLU_LIMIT, SWIGLU_LIMIT)
        glu = gate * (1.0 / (1.0 + jnp.exp(-SWIGLU_ALPHA * gate)))
        act = ((up + 1.0) * glu).astype(BF16)
        ys = jnp.dot(act, wd_bf[...], preferred_element_type=F32) + bd_ref[...]
        _store_token_major(ys_ref, ys)

    @pl.when(jnp.logical_not(live))
    def _():
        ys_ref[...] = jnp.zeros_like(ys_ref)


def _ffn(xs, block_e, n_blocks, next_e, w_gu, b_gu, w_down, b_down, tmb):
    D = D_MODEL
    P = xs.shape[0] // NCH
    E = w_gu.shape[0]
    blk = lambda i, be, nb, nx: (jnp.minimum(i, nb[0] - 1), 0)
    wsel = lambda i, be, nb, nx: (be[jnp.minimum(i, nb[0] - 1)], 0, 0)
    grid_spec = pltpu.PrefetchScalarGridSpec(
        num_scalar_prefetch=3,
        grid=(P // tmb,),
        in_specs=[pl.BlockSpec((tmb * NCH, LANES), blk),
                  pl.BlockSpec(memory_space=pl.ANY),
                  pl.BlockSpec((None, 1, 2 * D_EXPERT), wsel),
                  pl.BlockSpec(memory_space=pl.ANY),
                  pl.BlockSpec((None, 1, D), wsel)],
        out_specs=pl.BlockSpec((tmb * (D // LANES), LANES), lambda i, be, nb, nx: (i, 0)),
        scratch_shapes=[pltpu.VMEM((D, 2 * D_EXPERT), F32), pltpu.VMEM((D_EXPERT, D), F32),
                        pltpu.VMEM((D, 2 * D_EXPERT), BF16), pltpu.VMEM((D_EXPERT, D), BF16),
                        pltpu.SemaphoreType.DMA, pltpu.SemaphoreType.DMA],
    )
    return pl.pallas_call(
        _ffn_kernel,
        grid_spec=grid_spec,
        out_shape=jax.ShapeDtypeStruct((P * (D // LANES), LANES), F32),
        compiler_params=_cparams(("arbitrary",)),
        name="ffn",
    )(block_e, n_blocks, next_e, xs, w_gu, b_gu.reshape(E, 1, 2 * D_EXPERT), w_down, b_down.reshape(E, 1, D))


def _comb_kernel(x1_ref, gate_ref, g2_ref, fg_ref, dest_hbm, ys_hbm, o_ref, buf_a, buf_b, idx_a, idx_b,
                 sem_ia, sem_ib, sem_ra, sem_rb, *, tm, n_steps):
    i = pl.program_id(0)
    n = tm * TOP_K

    def idx_copy(tile, buf, sem):
        return pltpu.make_async_copy(dest_hbm.at[pl.ds(tile * n, n)], buf, sem)

    nch = x1_ref.shape[1] // LANES

    def row_copy(src, buf, k, t, sem):
        return pltpu.make_async_copy(ys_hbm.at[pl.ds(pl.multiple_of(src * nch, nch), nch)],
                                     buf.at[k, t[0], :, t[1]], sem)

    def issue_rows(idx, buf, sem):
        def body(r, c):
            for s in range(8):
                for k in range(TOP_K):
                    row_copy(idx[(r * 8 + s) * TOP_K + k], buf, k, (r, s), sem).start(priority=k % 2)
            return c

        lax.fori_loop(0, tm // 8, body, 0)

    def wait_rows(buf, sem):
        pltpu.make_async_copy(buf, buf, sem).wait()

    def finish(buf, lo):
        gates = gate_ref[lo:lo + tm, :]
        y = jnp.zeros((tm, x1_ref.shape[1]), F32)
        for k in range(TOP_K):
            rows = jnp.concatenate([buf[k, :, j].reshape(tm, LANES) for j in range(nch)], axis=1)
            y = y + gates[:, k:k + 1] * rows
        v = x1_ref[lo:lo + tm, :] + g2_ref[...] * y
        o_ref[lo:lo + tm, :] = v * lax.rsqrt(jnp.mean(v * v, axis=-1, keepdims=True) + EPS) * fg_ref[...]

    @pl.when(i == 0)
    def _():
        first = idx_copy(0, idx_a, sem_ia)
        first.start()
        first.wait()
        issue_rows(idx_a, buf_a, sem_ra)
        idx_copy(1, idx_b, sem_ib).start()

    idx_copy(2 * i + 1, idx_b, sem_ib).wait()
    issue_rows(idx_b, buf_b, sem_rb)

    @pl.when(i + 1 < n_steps)
    def _():
        idx_copy(2 * i + 2, idx_a, sem_ia).start()

    wait_rows(buf_a, sem_ra)
    finish(buf_a, 0)

    @pl.when(i + 1 < n_steps)
    def _():
        idx_copy(2 * i + 2, idx_a, sem_ia).wait()
        issue_rows(idx_a, buf_a, sem_ra)
        idx_copy(2 * i + 3, idx_b, sem_ib).start()

    wait_rows(buf_b, sem_rb)
    finish(buf_b, tm)


def _combine(x1, gate_tab, g2, final_g, dest_flat, ys, S, tm):
    N, D = x1.shape
    n_steps = N // (2 * tm)
    per_b = S // (2 * tm)
    return pl.pallas_call(
        functools.partial(_comb_kernel, tm=tm, n_steps=n_steps),
        grid=(n_steps,),
        in_specs=[pl.BlockSpec((2 * tm, D), lambda i: (i, 0)),
                  pl.BlockSpec((2 * tm, LANES), lambda i: (i, 0)),
                  pl.BlockSpec((None, 1, D), lambda i: (i // per_b, 0, 0)),
                  pl.BlockSpec((1, D), lambda i: (0, 0)),
                  pl.BlockSpec(memory_space=pl.ANY),
                  pl.BlockSpec(memory_space=pl.ANY)],
        out_specs=pl.BlockSpec((2 * tm, D), lambda i: (i, 0)),
        out_shape=jax.ShapeDtypeStruct((N, D), F32),
        scratch_shapes=[pltpu.VMEM((TOP_K, tm // 8, D // LANES, 8, LANES), F32),
                        pltpu.VMEM((TOP_K, tm // 8, D // LANES, 8, LANES), F32),
                        pltpu.SMEM((tm * TOP_K,), I32), pltpu.SMEM((tm * TOP_K,), I32),
                        pltpu.SemaphoreType.DMA, pltpu.SemaphoreType.DMA, pltpu.SemaphoreType.DMA,
                        pltpu.SemaphoreType.DMA],
        compiler_params=_cparams(("arbitrary",)),
        name="comb",
    )(x1, gate_tab, g2, final_g.reshape(1, D), dest_flat, ys)


def _tile(n, pref):
    t = min(pref, n)
    assert n % t == 0, (n, t)
    return t


def _layer(x, c, ada_w, ada_b, norm1_g, w_in, mix_scale, w_o, norm2_g,
           router_w, router_b, w_gu, b_gu, w_down, b_down, final_g):
    B, S, D = x.shape
    N = B * S
    mod = _mod(c, ada_w, ada_b).reshape(B, 6, 1, D)
    sh1, sc1, g1, sh2, sc2, g2 = (mod[:, j] for j in range(6))

    w_pad = jnp.pad(w_in, ((0, 0), (0, IN_COLS_PAD - IN_COLS))).astype(BF16)
    rq, rk, rv, rg, aqt, ak, avt, iqt, ik, iwt = _inproj(x, norm1_g, sc1, sh1, w_pad, _tile(S, PROJ_ROWS))
    ms = mix_scale.reshape(1, RET_W + DSA_W)
    ret = _retention(rq, rk, rv, rg, ms[:, :RET_W])
    att = _dsa(iqt, iwt, aqt, ik, ak, avt, ms[:, RET_W:], _tile(S, DSA_QUERIES), DSA_KEYS)

    rw_pad = jnp.pad(router_w, ((0, 0), (0, LANES - N_EXPERTS)))
    rw_hi = rw_pad.astype(BF16)
    rw_pad = jnp.stack([rw_hi, (rw_pad - rw_hi.astype(F32)).astype(BF16)])
    rb_pad = jnp.pad(router_b, (0, LANES - N_EXPERTS)).reshape(1, LANES)
    x1, h2, sel, idx_tab, gate_tab, counts = _oproj(ret, att, x, w_o.astype(BF16), g1, norm2_g, sc2, sh2,
                                                    rw_pad, rb_pad, _tile(S, PROJ_ROWS))

    tmb = FFN_ROWS
    n_rows = (N * TOP_K + N_EXPERTS * (tmb - 1)) // tmb * tmb + tmb
    cnt = counts[0, :N_EXPERTS].astype(I32)
    padded = (cnt + tmb - 1) // tmb * tmb
    ends = jnp.cumsum(padded)
    starts = ends - padded
    pstart = jnp.pad(starts.astype(F32), (0, LANES - N_EXPERTS)).reshape(1, LANES)
    n_blocks = (ends[-1] // tmb).reshape(1)
    first_row = jnp.arange(n_rows // tmb, dtype=I32) * tmb
    block_e = jnp.minimum(jnp.sum((ends[None, :] <= first_row[:, None]).astype(I32), axis=1), N_EXPERTS - 1)

    tmd = _tile(N, MOE_TOKENS)
    dest_tab = _dest(sel, idx_tab, pstart, tmd)
    dest_flat = dest_tab[:, :TOP_K].reshape(N * TOP_K)
    xs = _dispatch(h2, dest_flat, starts + cnt, n_rows, tmb, tmd)
    eid = jnp.arange(N_EXPERTS, dtype=I32)
    later_used = (eid[None, :] > eid[:, None]) & (padded[None, :] > 0)
    next_used = jnp.min(jnp.where(later_used, eid[None, :], N_EXPERTS), axis=1)
    next_e = jnp.where(next_used < N_EXPERTS, next_used, -1)[block_e].astype(I32)
    ys = _ffn(xs, block_e, n_blocks, next_e, w_gu, b_gu, w_down, b_down, tmb)
    out = _combine(x1, gate_tab, g2, final_g, dest_flat, ys, S, _tile(S, MOE_TOKENS))
    return out.reshape(B, S, D)


def kernel(x, c, ada_w, ada_b, norm1_g, w_in, mix_scale, w_o, norm2_g, router_w, router_b, w_gu, b_gu,
           w_down, b_down, final_g):
    assert ada_w.shape[0] == 1, "single-layer stack"
    return _layer(x, c, ada_w[0], ada_b[0], norm1_g[0], w_in[0], mix_scale[0], w_o[0], norm2_g[0],
                  router_w[0], router_b[0], w_gu[0], b_gu[0], w_down[0], b_down[0], final_g)
```

```python
import functools

import numpy as np
import jax
import jax.numpy as jnp
from jax import lax
from jax.experimental import pallas as pl
from jax.experimental.pallas import tpu as pltpu

F32 = jnp.float32
BF16 = jnp.bfloat16
I32 = jnp.int32

D_MODEL = 1024
RET_HEADS = 4
RET_DK = 64
RET_DV = 128
RET_CHUNK = 128
DSA_HEADS = 8
DSA_KV_HEADS = 2
DSA_HD = 64
IDX_HEADS = 8
IDX_HD = 64
TOPK_MAX = 256
N_EXPERTS = 32
TOP_K = 4
D_EXPERT = D_MODEL
SWIGLU_LIMIT = 7.0
SWIGLU_ALPHA = 1.702
EPS = 1e-6

RET_W = RET_HEADS * RET_DV
DSA_W = DSA_HEADS * DSA_HD
IN_COLS = 2888
IN_COLS_PAD = 2944

KAUG = 128
VAUG = 80
ALIBI_SPLIT = 64
ALIBI_TERMS = 3
LOG2E = 1.4426950408889634
PROJ_ROWS = 512
DSA_QUERIES = 512
DSA_KEYS = 128
MOE_TOKENS = 512
COMB_TOKENS = 256
RET_CHUNKS_PER_STEP = 8
FFN_ROWS = 512
OPROJ_PARTS = 2
FIX_UNROLL = 4
BISECT_COARSE_STEPS = 8
BISECT_VALUE_STEPS = 8
BISECT_MAX_STEPS = 64

LANES = 128
NCH = D_MODEL // LANES
VMEM_LIMIT = 56 * 1024 * 1024
NEG_BIG = -1e30
F32_LOWEST = float(np.finfo(np.float32).min)


def _cparams(sem):
    return pltpu.CompilerParams(dimension_semantics=sem, vmem_limit_bytes=VMEM_LIMIT)


def _mod_kernel(c_ref, w_ref, b_ref, o_ref):
    c = c_ref[...]
    s = c * (1.0 / (1.0 + jnp.exp(-c)))
    o_ref[...] = jnp.dot(s, w_ref[...], preferred_element_type=F32,
                         precision=lax.Precision.HIGHEST) + b_ref[...]


def _mod(c, ada_w, ada_b):
    B, D = c.shape
    n_out = ada_w.shape[1]
    rows = 8
    c8 = jnp.zeros((rows, D), F32).at[:B].set(c)
    out = pl.pallas_call(
        _mod_kernel,
        grid=(n_out // D,),
        in_specs=[pl.BlockSpec((rows, D), lambda j: (0, 0)),
                  pl.BlockSpec((D, D), lambda j: (0, j)),
                  pl.BlockSpec((1, D), lambda j: (0, j))],
        out_specs=pl.BlockSpec((rows, D), lambda j: (0, j)),
        out_shape=jax.ShapeDtypeStruct((rows, n_out), F32),
        compiler_params=_cparams(("arbitrary",)),
        name="mod",
    )(c8, ada_w, ada_b.reshape(1, n_out))
    return out[:B]


def _inproj_kernel(x_ref, g_ref, sc_ref, sh_ref, w_ref,
                   rq_ref, rk_ref, rv_ref, rg_ref, aqt_ref, ak_ref, avt_ref, iqt_ref, ik_ref, iwt_ref):
    x = x_ref[...]
    ms = jnp.mean(x * x, axis=-1, keepdims=True)
    y = x * lax.rsqrt(ms + EPS) * g_ref[...]
    hb = (y * (1.0 + sc_ref[...]) + sh_ref[...]).astype(BF16)

    def proj(lo, hi):
        return jnp.dot(hb, w_ref[:, lo:hi], preferred_element_type=F32)

    tm = x.shape[0]
    d = DSA_HD
    rq_ref[...] = proj(0, 256).astype(BF16)
    rk_ref[...] = (proj(256, 512) * (RET_DK ** -0.5)).astype(BF16)
    rv_ref[...] = proj(512, 1024).astype(BF16)
    rg_ref[...] = proj(1024, 1536).astype(BF16)
    aqt_ref[...] = (proj(1536, 2048) * (d ** -0.5 * LOG2E)).T.astype(BF16)
    kk = proj(2048, 2176)
    pos = pl.program_id(1) * tm + lax.broadcasted_iota(I32, (tm, d), 0)
    col = lax.broadcasted_iota(I32, (tm, d), 1)
    posblk = jnp.where(col < ALIBI_TERMS, pos // ALIBI_SPLIT,
                       jnp.where(col < 2 * ALIBI_TERMS, pos % ALIBI_SPLIT, 0)).astype(F32)
    for g in range(DSA_KV_HEADS):
        ak_ref[:, g * KAUG:g * KAUG + d] = kk[:, g * d:(g + 1) * d].astype(BF16)
        ak_ref[:, g * KAUG + d:(g + 1) * KAUG] = posblk.astype(BF16)
    vt = proj(2176, 2304).T
    r16 = lax.broadcasted_iota(I32, (VAUG - d, tm), 0)
    onesblk = jnp.where(r16 == 0, 1.0, 0.0).astype(BF16)
    for g in range(DSA_KV_HEADS):
        avt_ref[g * VAUG:g * VAUG + d, :] = vt[g * d:(g + 1) * d, :].astype(BF16)
        avt_ref[g * VAUG + d:(g + 1) * VAUG, :] = onesblk
    iqt_ref[...] = proj(2304, 2816).T.astype(BF16)
    last = proj(2816, 2944)
    ik_ref[...] = last[:, :IDX_HD].astype(BF16)
    iwt_ref[...] = last.T[IDX_HD:IDX_HD + IDX_HEADS, :] * ((IDX_HD ** -0.5) * (IDX_HEADS ** -0.5))


def _inproj(x, norm_g, sc, sh, w_pad, tm):
    B, S, D = x.shape
    row = lambda w: pl.BlockSpec((None, tm, w), lambda b, i: (b, i, 0))
    colT = lambda h: pl.BlockSpec((None, h, tm), lambda b, i: (b, 0, i))
    vec = pl.BlockSpec((None, 1, D), lambda b, i: (b, 0, 0))
    sd = lambda shape, dt: jax.ShapeDtypeStruct(shape, dt)
    G = DSA_KV_HEADS
    return pl.pallas_call(
        _inproj_kernel,
        grid=(B, S // tm),
        in_specs=[row(D), pl.BlockSpec((1, D), lambda b, i: (0, 0)), vec, vec,
                  pl.BlockSpec((D, IN_COLS_PAD), lambda b, i: (0, 0))],
        out_specs=[row(256), row(256), row(512), row(512), colT(DSA_W), row(G * KAUG), colT(G * VAUG),
                   colT(IDX_HEADS * IDX_HD), row(IDX_HD), colT(IDX_HEADS)],
        out_shape=[sd((B, S, 256), BF16), sd((B, S, 256), BF16), sd((B, S, 512), BF16),
                   sd((B, S, 512), BF16), sd((B, DSA_W, S), BF16), sd((B, S, G * KAUG), BF16),
                   sd((B, G * VAUG, S), BF16), sd((B, IDX_HEADS * IDX_HD, S), BF16),
                   sd((B, S, IDX_HD), BF16), sd((B, IDX_HEADS, S), F32)],
        compiler_params=_cparams(("parallel", "parallel")),
        name="inproj",
    )(x, norm_g.reshape(1, D), sc, sh, w_pad)


def _ret_kernel(rq_ref, rk_ref, rv_ref, rg_ref, din_ref, qd_ref, kd_ref, cd_ref, ms_ref, o_ref, state_ref):
    @pl.when(pl.program_id(1) == 0)
    def _():
        state_ref[...] = jnp.zeros_like(state_ref)

    C = din_ref.shape[1]
    for c in range(rq_ref.shape[0] // C):
        rows = slice(c * C, (c + 1) * C)
        for h in range(RET_HEADS):
            q = rq_ref[rows, h * RET_DK:(h + 1) * RET_DK]
            k = rk_ref[rows, h * RET_DK:(h + 1) * RET_DK]
            v = rv_ref[rows, h * RET_DV:(h + 1) * RET_DV]
            r_prev = state_ref[h]
            s = lax.dot_general(q, k, (((1,), (1,)), ((), ())), preferred_element_type=F32) * din_ref[h]
            o = jnp.dot(s.astype(BF16), v, preferred_element_type=F32)
            o = o + jnp.dot(q, r_prev.astype(BF16), preferred_element_type=F32) * qd_ref[h]
            vd = (v.astype(F32) * kd_ref[h]).astype(BF16)
            kv = lax.dot_general(k, vd, (((0,), (0,)), ((), ())), preferred_element_type=F32)
            state_ref[h] = r_prev * cd_ref[h] + kv
            o = o * lax.rsqrt(jnp.mean(o * o, axis=-1, keepdims=True) + EPS)
            g = rg_ref[rows, h * RET_DV:(h + 1) * RET_DV].astype(F32)
            gate = g * (1.0 / (1.0 + jnp.exp(-g)))
            o_ref[rows, h * RET_DV:(h + 1) * RET_DV] = (
                gate * o * ms_ref[:, h * RET_DV:(h + 1) * RET_DV]).astype(BF16)


def _ret_consts(C):
    H = RET_HEADS
    log_g = np.log1p(-np.exp2(-5.0 - np.arange(H, dtype=np.float64)))
    pos = np.arange(C, dtype=np.float64)
    diff = pos[:, None] - pos[None, :]
    d_inner = np.where(diff[None] >= 0, np.exp(np.maximum(diff, 0.0)[None] * log_g[:, None, None]), 0.0)
    q_decay = np.exp((pos + 1.0)[None] * log_g[:, None])
    k_decay = np.exp((C - 1.0 - pos)[None] * log_g[:, None])
    chunk_decay = np.exp(C * log_g)
    qd = np.broadcast_to(q_decay[:, :, None], (H, C, RET_DV))
    kd = np.broadcast_to(k_decay[:, :, None], (H, C, RET_DV))
    cd = np.broadcast_to(chunk_decay[:, None, None], (H, 1, RET_DV))
    f = lambda a: jnp.asarray(np.ascontiguousarray(a), F32)
    return f(d_inner), f(qd), f(kd), f(cd)


def _retention(rq, rk, rv, rg, ms_ret):
    B, S, _ = rq.shape
    C = min(RET_CHUNK, S)
    din, qd, kd, cd = _ret_consts(C)
    rows = _tile(S, RET_CHUNKS_PER_STEP * C)
    row = lambda w: pl.BlockSpec((None, rows, w), lambda b, n: (b, n, 0))
    full = lambda a: pl.BlockSpec(a.shape, lambda b, n: (0,) * a.ndim)
    return pl.pallas_call(
        _ret_kernel,
        grid=(B, S // rows),
        in_specs=[row(256), row(256), row(512), row(512), full(din), full(qd), full(kd), full(cd),
                  pl.BlockSpec((1, RET_W), lambda b, n: (0, 0))],
        out_specs=row(RET_W),
        out_shape=jax.ShapeDtypeStruct((B, S, RET_W), BF16),
        scratch_shapes=[pltpu.VMEM((RET_HEADS, RET_DK, RET_DV), F32)],
        compiler_params=_cparams(("parallel", "arbitrary")),
        name="ret",
    )(rq, rk, rv, rg, din, qd, kd, cd, ms_ret)


def _f32_key(x):
    i = lax.bitcast_convert_type(x, I32)
    return i ^ ((i >> 31) & 0x7FFFFFFF)


def _key_f32(k):
    return lax.bitcast_convert_type(k ^ ((k >> 31) & 0x7FFFFFFF), F32)


def _floor16(x):
    i = lax.bitcast_convert_type(x, I32)
    return lax.bitcast_convert_type((i + ((i >> 31) & 0xFFFF)) & -0x10000, F32)


def _trunc16(x):
    return lax.bitcast_convert_type(lax.bitcast_convert_type(x, I32) & -0x10000, F32)


def _dsa_kernel(iqt_ref, iwt_ref, aqt_ref, ik_ref, ak_ref, avt_ref, ms_ref, o_ref, score_ref, code_ref, qa_ref, sa_ref, sb_ref, mask_ref, *acc_refs,
                tq, tks, n_sel):
    H, G, R, d = DSA_HEADS, DSA_KV_HEADS, DSA_HEADS // DSA_KV_HEADS, DSA_HD
    t0 = pl.program_id(1) * tq
    nsub = (t0 + tq) // tks
    tka = 2 * tks
    npair = ((t0 + tq) // tka + 1) // 2
    kf = float(n_sel)
    qpos = t0 + lax.broadcasted_iota(I32, (1, tq), 1)
    krow = lax.broadcasted_iota(I32, (tks, tq), 0)

    wrow = [iwt_ref[h:h + 1, :] for h in range(IDX_HEADS)]

    def fold8(x, op):
        acc = x[0:8, :]
        for i in range(1, tks // 8):
            acc = op(acc, x[8 * i:8 * (i + 1), :])
        return acc

    def score_pair(i, carry, masked):
        mx, mn, npos, nnon = carry
        for u in range(2 * tka // tks):
            r0 = pl.multiple_of(i * 2 * tka + u * tks, tks)
            kc = ik_ref[pl.ds(r0, tks), :]
            acc = jnp.zeros((tks, tq), F32)
            for h in range(IDX_HEADS):
                rel = jnp.dot(kc, iqt_ref[h * IDX_HD:(h + 1) * IDX_HD, :], preferred_element_type=F32)
                acc = acc + jnp.maximum(rel, 0.0) * wrow[h]
            if masked:
                causal = r0 + krow <= qpos
                sc = jnp.where(causal, acc, -jnp.inf)
                lowest = jnp.where(causal, acc, jnp.inf)
            else:
                sc = lowest = acc
            score_ref[pl.ds(r0, tks), :] = sc
            code_ref[pl.ds(r0, tks), :] = _trunc16(sc).astype(BF16)
            mx = jnp.maximum(mx, fold8(sc, jnp.maximum))
            mn = jnp.minimum(mn, fold8(lowest, jnp.minimum))
            npos = npos + fold8(jnp.where(sc > 0.0, 1.0, 0.0), jnp.add)
            nnon = nnon + fold8(jnp.where(sc >= 0.0, 1.0, 0.0), jnp.add)
        return mx, mn, npos, nnon

    stat0 = (jnp.full((8, tq), -jnp.inf, F32), jnp.full((8, tq), jnp.inf, F32),
             jnp.zeros((8, tq), F32), jnp.zeros((8, tq), F32))
    n_inner = (t0 + 1) // (2 * tka)
    stat = lax.fori_loop(0, n_inner, functools.partial(score_pair, masked=False), stat0)
    mx, mn, npos, nnon = lax.fori_loop(n_inner, npair, functools.partial(score_pair, masked=True), stat)
    top = jnp.max(mx, axis=0, keepdims=True)
    lo0 = jnp.min(mn, axis=0, keepdims=True)
    n_pos = jnp.sum(npos, axis=0, keepdims=True)
    n_nonneg = jnp.sum(nnon, axis=0, keepdims=True)


    def count(th, strict):
        def body(j, acc):
            for u in range(2):
                s = score_ref[pl.ds(pl.multiple_of((2 * j + u) * tks, tks), tks), :]
                hit = (s > th) if strict else (s >= th)
                acc = acc + fold8(jnp.where(hit, 1.0, 0.0), jnp.add)
            return acc

        acc = lax.fori_loop(0, nsub // 2, body, jnp.zeros((8, tq), F32))
        return jnp.sum(acc, axis=0, keepdims=True)

    def probe(lo, hi, it):
        lk, hk = _f32_key(lo), _f32_key(hi)
        mk = (lk >> 1) + (hk >> 1) + (lk & hk & 1)
        mv = lo + (hi - lo) * 0.5
        early = (jnp.zeros((1, tq), I32) + it) < BISECT_VALUE_STEPS
        mid = jnp.where(early & (mv > lo) & (mv < hi), mv, _key_f32(mk))
        return mid, jnp.max(jnp.where(mk != lk, 1.0, 0.0))

    def bis_cond(c):
        return (c[5] > 0.0) & (c[6] < BISECT_MAX_STEPS)

    def bis_body(c):
        lo, hi, c_lo, c_hi, mid, _, it = c
        cnt = count(mid, False)
        ge = cnt >= kf
        up = ge | (cnt == kf)
        dn = (~ge) | (cnt == kf)
        lo, c_lo = jnp.where(up, mid, lo), jnp.where(up, cnt, c_lo)
        hi, c_hi = jnp.where(dn, mid, hi), jnp.where(dn, cnt, c_hi)
        mid, active = probe(lo, hi, it + 1)
        return lo, hi, c_lo, c_hi, mid, active, it + 1

    zero = jnp.zeros((1, tq), F32)
    keep_all = qpos + 1 <= n_sel
    settled = keep_all | ((n_nonneg >= kf) & (n_pos < kf))
    above = n_pos >= kf
    c_lo0 = jnp.where(settled | above, n_nonneg, (qpos + 1).astype(F32))
    c_hi0 = jnp.where(settled | ~above, n_nonneg, zero)
    lo0 = jnp.where(settled | above, zero, lo0)
    hi0 = jnp.where(settled | ~above, zero, _key_f32(_f32_key(top) + 1))
    def count16(m):
        mb = m.astype(BF16)

        def body(j, acc):
            for u in range(2):
                c = code_ref[pl.ds(pl.multiple_of((2 * j + u) * tks, tks), tks), :]
                one = jnp.where(c >= mb, jnp.ones((), BF16), jnp.zeros((), BF16))
                part = one[0:16, :]
                for i in range(1, tks // 16):
                    part = part + one[16 * i:16 * (i + 1), :]
                acc = acc + part.astype(F32)
            return acc

        acc = lax.fori_loop(0, nsub // 2, body, jnp.zeros((16, tq), F32))
        return jnp.sum(acc, axis=0, keepdims=True)

    def coarse_step(_, c):
        lo, hi, c_lo, c_hi = c
        below = _floor16(lo + (hi - lo) * 0.5)
        above = _key_f32(_f32_key(below) + 0x10000)
        m = jnp.where(below > lo, below, above)
        v = jnp.where(m > 0.0, m, _key_f32(_f32_key(m) - 0x10000 + 1))
        ok = (v > lo) & (v < hi)
        cnt = count16(m)
        ge = cnt >= kf
        up = ok & ge
        dn = ok & ((~ge) | (cnt == kf))
        return (jnp.where(up, v, lo), jnp.where(dn, v, hi), jnp.where(up, cnt, c_lo), jnp.where(dn, cnt, c_hi))

    lo0, hi0, c_lo0, c_hi0 = lax.fori_loop(0, BISECT_COARSE_STEPS, coarse_step, (lo0, hi0, c_lo0, c_hi0))
    it0 = jnp.int32(BISECT_VALUE_STEPS)
    mid0, active0 = probe(lo0, hi0, it0)
    lo, hi, c_lo, c_hi, _, _, _ = lax.while_loop(bis_cond, bis_body,
                                                 (lo0, hi0, c_lo0, c_hi0, mid0, active0, it0))
    at_hi = c_hi >= kf
    thr = jnp.where(keep_all, F32_LOWEST, jnp.where(at_hi, hi, lo))
    excess = jnp.where(keep_all, 0.0, jnp.where(at_hi, c_hi, c_lo) - kf)

    @pl.when(jnp.max(excess) > 0.0)
    def _():
        tied_nonzero = jnp.max(jnp.where((excess > 0.0) & (thr != 0.0), 1.0, 0.0)) > 0.0
        n_above = lax.cond(tied_nonzero, lambda: count(thr, True), lambda: n_pos)
        budget = jnp.where(excess > 0.0, kf - jnp.where(thr == 0.0, n_pos, n_above), jnp.inf)
        earlier = lax.broadcasted_iota(I32, (tks, tks), 1) < lax.broadcasted_iota(I32, (tks, tks), 0)
        earlier = jnp.where(earlier, 1.0, 0.0).astype(BF16)

        def fix(j, seen):
            tiles = []
            for u in range(FIX_UNROLL):
                r0 = pl.multiple_of((FIX_UNROLL * j + u) * tks, tks)
                s = score_ref[pl.ds(r0, tks), :]
                eq = s == thr
                eqf = jnp.where(eq, 1.0, 0.0)
                within = jnp.dot(earlier, eqf.astype(BF16), preferred_element_type=F32)
                tiles.append((r0, s, eq, within, jnp.sum(eqf, axis=0, keepdims=True)))
            for r0, s, eq, within, n_eq in tiles:
                score_ref[pl.ds(r0, tks), :] = jnp.where(eq & (within + seen >= budget), -jnp.inf, s)
                seen = seen + n_eq
            return seen

        lax.fori_loop(0, nsub // FIX_UNROLL, fix, jnp.zeros((1, tq), F32))

    arow = lax.broadcasted_iota(I32, (KAUG - d, tq), 0)
    for h in range(H):
        terms, rest = [], 2.0 ** (-8.0 * (h + 1) / H) * LOG2E
        for _ in range(ALIBI_TERMS):
            terms.append(float(np.asarray(rest, np.float32).astype(BF16).astype(np.float64)))
            rest -= terms[-1]
        rows = jnp.zeros((KAUG - d, tq), F32)
        for i, c in enumerate(terms):
            rows = jnp.where(arow == i, c * ALIBI_SPLIT, jnp.where(arow == ALIBI_TERMS + i, c, rows))
        qa_ref[h, 0:d, :] = aqt_ref[h * d:(h + 1) * d, :]
        qa_ref[h, d:KAUG, :] = rows.astype(BF16)
    for acc in acc_refs:
        acc[...] = jnp.zeros_like(acc)

    def logits(j, h):
        ka = ak_ref[pl.ds(pl.multiple_of(j * tka, tka), tka), (h // R) * KAUG:(h // R + 1) * KAUG]
        return jnp.dot(ka, qa_ref[h], preferred_element_type=F32)

    def step(j, j_next, cur_ref, next_ref, ms):
        r0 = pl.multiple_of(j * tka, tka)
        mask_ref[...] = jnp.where(score_ref[pl.ds(r0, tka), :] >= thr, 0.0, NEG_BIG)
        nms = []
        ahead = logits(j_next, 0)
        for h in range(H):
            g = h // R
            next_ref[h] = ahead
            if h + 1 < H:
                ahead = logits(j_next, h + 1)
            s = cur_ref[h] + mask_ref[...]
            m_new = jnp.maximum(ms[h], jnp.max(s, axis=0, keepdims=True))
            p = jnp.exp2(s - m_new).astype(BF16)
            va = avt_ref[g * VAUG:(g + 1) * VAUG, pl.ds(r0, tka)]
            acc = acc_refs[h]
            acc[...] = acc[...] * jnp.exp2(ms[h] - m_new) + jnp.dot(va, p, preferred_element_type=F32)
            nms.append(m_new)
        return tuple(nms)

    for h in range(H):
        sa_ref[h] = logits(0, h)

    def att_pair(i, ms):
        ms = step(2 * i, 2 * i + 1, sa_ref, sb_ref, ms)
        return step(2 * i + 1, jnp.minimum(2 * i + 2, 2 * npair - 1), sb_ref, sa_ref, ms)

    lax.fori_loop(0, npair, att_pair, tuple(jnp.full((1, tq), NEG_BIG, F32) for _ in range(H)))
    for h in range(H):
        a = acc_refs[h][...]
        o = a[0:d, :] / a[d:d + 1, :]
        o_ref[h * d:(h + 1) * d, :] = (o * ms_ref[h * d:(h + 1) * d, :]).astype(BF16)


def _dsa(iqt, iwt, aqt, ik, ak, avt, ms_att, tq, tks):
    B, _, S = iqt.shape
    assert S <= ALIBI_SPLIT * 256 and tq % (2 * tks) == 0 and tq % (FIX_UNROLL * tks) == 0
    assert (S // (2 * tks)) % 2 == 0
    n_sel = min(TOPK_MAX, S // 4)
    G = DSA_KV_HEADS
    colT = lambda h: pl.BlockSpec((None, h, tq), lambda b, i: (b, 0, i))
    msb = jnp.broadcast_to(ms_att.reshape(DSA_W, 1), (DSA_W, tq))
    return pl.pallas_call(
        functools.partial(_dsa_kernel, tq=tq, tks=tks, n_sel=n_sel),
        grid=(B, S // tq),
        in_specs=[colT(IDX_HEADS * IDX_HD), colT(IDX_HEADS), colT(DSA_W),
                  pl.BlockSpec((None, S, IDX_HD), lambda b, i: (b, 0, 0), pipeline_mode=pl.Buffered(1)),
                  pl.BlockSpec((None, S, G * KAUG), lambda b, i: (b, 0, 0), pipeline_mode=pl.Buffered(1)),
                  pl.BlockSpec((None, G * VAUG, S), lambda b, i: (b, 0, 0), pipeline_mode=pl.Buffered(1)),
                  pl.BlockSpec((DSA_W, tq), lambda b, i: (0, 0))],
        out_specs=colT(DSA_W),
        out_shape=jax.ShapeDtypeStruct((B, DSA_W, S), BF16),
        scratch_shapes=[pltpu.VMEM((S, tq), F32), pltpu.VMEM((S, tq), BF16), pltpu.VMEM((DSA_HEADS, KAUG, tq), BF16),
                        pltpu.VMEM((DSA_HEADS, 2 * tks, tq), F32), pltpu.VMEM((DSA_HEADS, 2 * tks, tq), F32),
                        pltpu.VMEM((2 * tks, tq), F32)]
        + [pltpu.VMEM((VAUG, tq), F32) for _ in range(DSA_HEADS)],
        compiler_params=_cparams(("parallel", "arbitrary")),
        name="dsa",
    )(iqt, iwt, aqt, ik, ak, avt, msb)


def _oproj_kernel(ret_ref, att_ref, x_ref, wo_ref, g1_ref, n2_ref, sc_ref, sh_ref, rw_ref, rb_ref,
                  x1_ref, h2_ref, sel_ref, idx_ref, gate_ref, cnt_ref):
    tm = x_ref.shape[0]
    parts = [slice(r, r + tm // OPROJ_PARTS) for r in range(0, tm, tm // OPROJ_PARTS)]
    nch = x_ref.shape[1] // LANES

    def mix(rows):
        out = jnp.dot(ret_ref[rows, :], wo_ref[:RET_W, :], preferred_element_type=F32)
        return out + lax.dot_general(att_ref[:, rows], wo_ref[RET_W:, :], (((0,), (0,)), ((), ())),
                                     preferred_element_type=F32)

    def hidden(rows, mixo):
        x1 = x_ref[rows, :] + g1_ref[...] * mixo
        x1_ref[rows, :] = x1
        y = x1 * lax.rsqrt(jnp.mean(x1 * x1, axis=-1, keepdims=True) + EPS) * n2_ref[...]
        h2 = y * (1.0 + sc_ref[...]) + sh_ref[...]
        for j in range(nch):
            h2_ref[pl.ds(rows.start * nch + j, rows.stop - rows.start, stride=nch), :] = h2[:, j * LANES:(j + 1) * LANES]
        return h2

    def router(h2):
        h_hi = h2.astype(BF16)
        h_lo = (h2 - h_hi.astype(F32)).astype(BF16)
        logits = jnp.dot(h_hi, rw_ref[0], preferred_element_type=F32)
        return logits + (jnp.dot(h_hi, rw_ref[1], preferred_element_type=F32)
                         + jnp.dot(h_lo, rw_ref[0], preferred_element_type=F32)) + rb_ref[...]

    def top4(rows, logits):
        n = logits.shape[0]
        lane = lax.broadcasted_iota(I32, (n, LANES), 1).astype(F32)
        work = jnp.where(lane < N_EXPERTS, logits, -jnp.inf)
        sel = jnp.zeros((n, LANES), F32)
        idx_tab = jnp.zeros((n, LANES), F32)
        vals = []
        for k in range(TOP_K):
            m = jnp.max(work, axis=1, keepdims=True)
            idx = jnp.min(jnp.where(work == m, lane, float(LANES)), axis=1, keepdims=True)
            hit = lane == idx
            sel = jnp.where(hit, 1.0, sel)
            idx_tab = jnp.where(lane == k, idx, idx_tab)
            work = jnp.where(hit, -jnp.inf, work)
            vals.append(m)
        es = [jnp.exp(v - vals[0]) for v in vals]
        den = es[0] + es[1] + es[2] + es[3]
        gate_tab = jnp.zeros((n, LANES), F32)
        for k in range(TOP_K):
            gate_tab = jnp.where(lane == k, es[k] / den, gate_tab)
        sel_ref[rows, :] = sel
        idx_ref[rows, :] = idx_tab
        gate_ref[rows, :] = gate_tab
        return jnp.sum(sel, axis=0, keepdims=True)

    mixes = [mix(rows) for rows in parts]
    hiddens = [hidden(rows, m) for rows, m in zip(parts, mixes)]
    logit_parts = [router(h2) for h2 in hiddens]
    counts = [top4(rows, lg) for rows, lg in zip(parts, logit_parts)]

    @pl.when((pl.program_id(0) == 0) & (pl.program_id(1) == 0))
    def _():
        cnt_ref[...] = jnp.zeros_like(cnt_ref)

    cnt_ref[...] += functools.reduce(lambda a, b: a + b, counts)


def _oproj(ret, att, x, wo, g1, n2g, sc2, sh2, rw_pad, rb_pad, tm):
    B, S, D = x.shape
    nt = S // tm
    row = lambda w: pl.BlockSpec((None, tm, w), lambda b, i: (b, i, 0))
    flat = lambda w: pl.BlockSpec((tm, w), lambda b, i: (b * nt + i, 0))
    vec = pl.BlockSpec((None, 1, D), lambda b, i: (b, 0, 0))
    cst = lambda shape: pl.BlockSpec(shape, lambda b, i: (0, 0))
    sd = lambda shape, dt: jax.ShapeDtypeStruct(shape, dt)
    N = B * S
    return pl.pallas_call(
        _oproj_kernel,
        grid=(B, nt),
        in_specs=[row(RET_W), pl.BlockSpec((None, DSA_W, tm), lambda b, i: (b, 0, i)), row(D), cst((D, D)), vec,
                  cst((1, D)), vec, vec,
                  pl.BlockSpec((2, D, LANES), lambda b, i: (0, 0, 0)), cst((1, LANES))],
        out_specs=[flat(D), pl.BlockSpec((tm * (D // LANES), LANES), lambda b, i: (b * nt + i, 0)),
                   flat(LANES), flat(LANES), flat(LANES), cst((1, LANES))],
        out_shape=[sd((N, D), F32), sd((N * (D // LANES), LANES), F32), sd((N, LANES), F32), sd((N, LANES), F32),
                   sd((N, LANES), F32), sd((1, LANES), F32)],
        compiler_params=_cparams(("arbitrary", "arbitrary")),
        name="oproj",
    )(ret, att, x, wo, g1, n2g.reshape(1, D), sc2, sh2, rw_pad, rb_pad)


def _dest_kernel(sel_ref, idx_ref, pstart_ref, dest_ref, seen_ref):
    @pl.when(pl.program_id(0) == 0)
    def _():
        seen_ref[...] = jnp.zeros_like(seen_ref)

    sel = sel_ref[...]
    tm = sel.shape[0]
    earlier = lax.broadcasted_iota(I32, (tm, tm), 1) < lax.broadcasted_iota(I32, (tm, tm), 0)
    earlier = jnp.where(earlier, 1.0, 0.0).astype(BF16)
    rank = jnp.dot(earlier, sel.astype(BF16), preferred_element_type=F32) + seen_ref[...]
    dest = pstart_ref[...] + rank
    lane = lax.broadcasted_iota(I32, (tm, LANES), 1).astype(F32)
    idx_tab = idx_ref[...]
    out = jnp.zeros((tm, LANES), F32)
    for k in range(TOP_K):
        e_k = jnp.sum(jnp.where(lane == k, idx_tab, 0.0), axis=1, keepdims=True)
        d_k = jnp.sum(jnp.where(lane == e_k, dest, 0.0), axis=1, keepdims=True)
        out = jnp.where(lane == k, d_k, out)
    dest_ref[...] = out.astype(I32)
    seen_ref[...] += jnp.sum(sel, axis=0, keepdims=True)


def _dest(sel, idx_tab, pstart, tm):
    N = sel.shape[0]
    blk = pl.BlockSpec((tm, LANES), lambda i: (i, 0))
    return pl.pallas_call(
        _dest_kernel,
        grid=(N // tm,),
        in_specs=[blk, blk, pl.BlockSpec((1, LANES), lambda i: (0, 0))],
        out_specs=blk,
        out_shape=jax.ShapeDtypeStruct((N, LANES), I32),
        scratch_shapes=[pltpu.VMEM((1, LANES), F32)],
        compiler_params=_cparams(("arbitrary",)),
        name="dest",
    )(sel, idx_tab, pstart)


def _disp_kernel(zs_ref, h2_ref, dest_hbm, xs_hbm, zbuf, idx_a, idx_b, sem_ia, sem_ib, sem_row, sem_z,
                 *, tm, nch, zrows, n_steps):
    i = pl.program_id(0)
    n = tm * TOP_K

    def idx_copy(tile, buf, sem):
        return pltpu.make_async_copy(dest_hbm.at[pl.ds(tile * n, n)], buf, sem)

    def rows(ref, first, count):
        return ref.at[pl.ds(pl.multiple_of(first * nch, nch), count * nch)]

    def row_copy(row, dst):
        return pltpu.make_async_copy(rows(h2_ref, row, 1), rows(xs_hbm, dst, 1), sem_row)

    def issue_rows(first_row, idx):
        def body(r, c):
            for s in range(8):
                t = r * 8 + s
                for k in range(TOP_K):
                    row_copy(first_row + t, idx[t * TOP_K + k]).start(priority=k % 2)
            return c

        lax.fori_loop(0, tm // 8, body, 0)

    def wait_rows():
        pltpu.make_async_copy(rows(xs_hbm, 0, n), rows(xs_hbm, 0, n), sem_row).wait()

    @pl.when(i == 0)
    def _():
        zbuf[...] = jnp.zeros_like(zbuf)
        for e in range(N_EXPERTS):
            fill = pltpu.make_async_copy(zbuf, rows(xs_hbm, zs_ref[e], zrows), sem_z)
            fill.start()
            fill.wait()
        idx_copy(0, idx_a, sem_ia).start()
        idx_copy(1, idx_b, sem_ib).start()

    idx_copy(2 * i, idx_a, sem_ia).wait()
    issue_rows(0, idx_a)
    idx_copy(2 * i + 1, idx_b, sem_ib).wait()
    issue_rows(tm, idx_b)

    @pl.when(i + 1 < n_steps)
    def _():
        idx_copy(2 * i + 2, idx_a, sem_ia).start()
        idx_copy(2 * i + 3, idx_b, sem_ib).start()

    wait_rows()
    wait_rows()


def _dispatch(h2, dest_flat, zero_start, n_rows, zrows, tm):
    nch = D_MODEL // LANES
    n_steps = h2.shape[0] // nch // (2 * tm)
    grid_spec = pltpu.PrefetchScalarGridSpec(
        num_scalar_prefetch=1,
        grid=(n_steps,),
        in_specs=[pl.BlockSpec((2 * tm * nch, LANES), lambda i, zs: (i, 0)), pl.BlockSpec(memory_space=pl.ANY)],
        out_specs=pl.BlockSpec(memory_space=pl.ANY),
        scratch_shapes=[pltpu.VMEM((zrows * nch, LANES), F32), pltpu.SMEM((tm * TOP_K,), I32),
                        pltpu.SMEM((tm * TOP_K,), I32), pltpu.SemaphoreType.DMA, pltpu.SemaphoreType.DMA,
                        pltpu.SemaphoreType.DMA, pltpu.SemaphoreType.DMA],
    )
    return pl.pallas_call(
        functools.partial(_disp_kernel, tm=tm, nch=nch, zrows=zrows, n_steps=n_steps),
        grid_spec=grid_spec,
        out_shape=jax.ShapeDtypeStruct((n_rows * nch, LANES), F32),
        compiler_params=_cparams(("arbitrary",)),
        name="disp",
    )(zero_start, h2, dest_flat)


def _store_token_major(ref, x):
    rows, d = x.shape
    nch = d // LANES
    for j in range(nch):
        ref[pl.ds(j, rows, stride=nch), :] = x[:, j * LANES:(j + 1) * LANES]


def _load_token_major(ref, rows, nch):
    return jnp.concatenate([ref[pl.ds(j, rows, stride=nch), :] for j in range(nch)], axis=1)


def _ffn_kernel(be_ref, nb_ref, nxt_ref, xs_ref, wgu_hbm, bgu_ref, wd_hbm, bd_ref, ys_ref,
                wgu_st, wd_st, wgu_bf, wd_bf, sem_gu, sem_d):
    i = pl.program_id(0)
    live = i < nb_ref[0]
    e = be_ref[i]

    def fetch(ex):
        return (pltpu.make_async_copy(wgu_hbm.at[ex], wgu_st, sem_gu),
                pltpu.make_async_copy(wd_hbm.at[ex], wd_st, sem_d))

    @pl.when(i == 0)
    def _():
        for cp in fetch(e):
            cp.start()

    @pl.when(live & ((i == 0) | (e != be_ref[jnp.maximum(i - 1, 0)])))
    def _():
        for cp in fetch(e):
            cp.wait()
        wgu_bf[...] = wgu_st[...].astype(BF16)
        wd_bf[...] = wd_st[...].astype(BF16)

        @pl.when(nxt_ref[i] >= 0)
        def _():
            for cp in fetch(nxt_ref[i]):
                cp.start()

    @pl.when(live)
    def _():
        xb = _load_token_major(xs_ref, ys_ref.shape[0] // NCH, NCH).astype(BF16)
        gu = jnp.dot(xb, wgu_bf[...], preferred_element_type=F32) + bgu_ref[...]
        gate = jnp.minimum(gu[:, :D_EXPERT], SWIGLU_LIMIT)
        up = jnp.clip(gu[:, D_EXPERT:], -SWIGLU_LIMIT, SWIGLU_LIMIT)
        glu = gate * (1.0 / (1.0 + jnp.exp(-SWIGLU_ALPHA * gate)))
        act = ((up + 1.0) * glu).astype(BF16)
        ys = jnp.dot(act, wd_bf[...], preferred_element_type=F32) + bd_ref[...]
        _store_token_major(ys_ref, ys)

    @pl.when(jnp.logical_not(live))
    def _():
        ys_ref[...] = jnp.zeros_like(ys_ref)


def _ffn(xs, block_e, n_blocks, next_e, w_gu, b_gu, w_down, b_down, tmb):
    D = D_MODEL
    P = xs.shape[0] // NCH
    E = w_gu.shape[0]
    blk = lambda i, be, nb, nx: (jnp.minimum(i, nb[0] - 1), 0)
    wsel = lambda i, be, nb, nx: (be[jnp.minimum(i, nb[0] - 1)], 0, 0)
    grid_spec = pltpu.PrefetchScalarGridSpec(
        num_scalar_prefetch=3,
        grid=(P // tmb,),
        in_specs=[pl.BlockSpec((tmb * NCH, LANES), blk),
                  pl.BlockSpec(memory_space=pl.ANY),
                  pl.BlockSpec((None, 1, 2 * D_EXPERT), wsel),
                  pl.BlockSpec(memory_space=pl.ANY),
                  pl.BlockSpec((None, 1, D), wsel)],
        out_specs=pl.BlockSpec((tmb * (D // LANES), LANES), lambda i, be, nb, nx: (i, 0)),
        scratch_shapes=[pltpu.VMEM((D, 2 * D_EXPERT), F32), pltpu.VMEM((D_EXPERT, D), F32),
                        pltpu.VMEM((D, 2 * D_EXPERT), BF16), pltpu.VMEM((D_EXPERT, D), BF16),
                        pltpu.SemaphoreType.DMA, pltpu.SemaphoreType.DMA],
    )
    return pl.pallas_call(
        _ffn_kernel,
        grid_spec=grid_spec,
        out_shape=jax.ShapeDtypeStruct((P * (D // LANES), LANES), F32),
        compiler_params=_cparams(("arbitrary",)),
        name="ffn",
    )(block_e, n_blocks, next_e, xs, w_gu, b_gu.reshape(E, 1, 2 * D_EXPERT), w_down, b_down.reshape(E, 1, D))


def _comb_kernel(x1_ref, gate_ref, g2_ref, fg_ref, dest_hbm, ys_hbm, o_ref, buf_a, buf_b, idx_a, idx_b,
                 sem_ia, sem_ib, sem_ra, sem_rb, *, tm, n_steps):
    i = pl.program_id(0)
    n = tm * TOP_K

    def idx_copy(tile, buf, sem):
        return pltpu.make_async_copy(dest_hbm.at[pl.ds(tile * n, n)], buf, sem)

    nch = x1_ref.shape[1] // LANES

    def row_copy(src, buf, k, t, sem):
        return pltpu.make_async_copy(ys_hbm.at[pl.ds(pl.multiple_of(src * nch, nch), nch)],
                                     buf.at[k, t[0], :, t[1]], sem)

    def issue_rows(idx, buf, sem):
        def body(r, c):
            for s in range(8):
                for k in range(TOP_K):
                    row_copy(idx[(r * 8 + s) * TOP_K + k], buf, k, (r, s), sem).start(priority=k % 2)
            return c

        lax.fori_loop(0, tm // 8, body, 0)

    def wait_rows(buf, sem):
        pltpu.make_async_copy(buf, buf, sem).wait()

    def finish(buf, lo):
        gates = gate_ref[lo:lo + tm, :]
        y = jnp.zeros((tm, x1_ref.shape[1]), F32)
        for k in range(TOP_K):
            rows = jnp.concatenate([buf[k, :, j].reshape(tm, LANES) for j in range(nch)], axis=1)
            y = y + gates[:, k:k + 1] * rows
        v = x1_ref[lo:lo + tm, :] + g2_ref[...] * y
        o_ref[lo:lo + tm, :] = v * lax.rsqrt(jnp.mean(v * v, axis=-1, keepdims=True) + EPS) * fg_ref[...]

    @pl.when(i == 0)
    def _():
        first = idx_copy(0, idx_a, sem_ia)
        first.start()
        first.wait()
        issue_rows(idx_a, buf_a, sem_ra)
        idx_copy(1, idx_b, sem_ib).start()

    idx_copy(2 * i + 1, idx_b, sem_ib).wait()
    issue_rows(idx_b, buf_b, sem_rb)

    @pl.when(i + 1 < n_steps)
    def _():
        idx_copy(2 * i + 2, idx_a, sem_ia).start()

    wait_rows(buf_a, sem_ra)
    finish(buf_a, 0)

    @pl.when(i + 1 < n_steps)
    def _():
        idx_copy(2 * i + 2, idx_a, sem_ia).wait()
        issue_rows(idx_a, buf_a, sem_ra)
        idx_copy(2 * i + 3, idx_b, sem_ib).start()

    wait_rows(buf_b, sem_rb)
    finish(buf_b, tm)


def _combine(x1, gate_tab, g2, final_g, dest_flat, ys, S, tm):
    N, D = x1.shape
    n_steps = N // (2 * tm)
    per_b = S // (2 * tm)
    return pl.pallas_call(
        functools.partial(_comb_kernel, tm=tm, n_steps=n_steps),
        grid=(n_steps,),
        in_specs=[pl.BlockSpec((2 * tm, D), lambda i: (i, 0)),
                  pl.BlockSpec((2 * tm, LANES), lambda i: (i, 0)),
                  pl.BlockSpec((None, 1, D), lambda i: (i // per_b, 0, 0)),
                  pl.BlockSpec((1, D), lambda i: (0, 0)),
                  pl.BlockSpec(memory_space=pl.ANY),
                  pl.BlockSpec(memory_space=pl.ANY)],
        out_specs=pl.BlockSpec((2 * tm, D), lambda i: (i, 0)),
        out_shape=jax.ShapeDtypeStruct((N, D), F32),
        scratch_shapes=[pltpu.VMEM((TOP_K, tm // 8, D // LANES, 8, LANES), F32),
                        pltpu.VMEM((TOP_K, tm // 8, D // LANES, 8, LANES), F32),
                        pltpu.SMEM((tm * TOP_K,), I32), pltpu.SMEM((tm * TOP_K,), I32),
                        pltpu.SemaphoreType.DMA, pltpu.SemaphoreType.DMA, pltpu.SemaphoreType.DMA,
                        pltpu.SemaphoreType.DMA],
        compiler_params=_cparams(("arbitrary",)),
        name="comb",
    )(x1, gate_tab, g2, final_g.reshape(1, D), dest_flat, ys)


def _tile(n, pref):
    t = min(pref, n)
    assert n % t == 0, (n, t)
    return t


def _layer(x, c, ada_w, ada_b, norm1_g, w_in, mix_scale, w_o, norm2_g,
           router_w, router_b, w_gu, b_gu, w_down, b_down, final_g):
    B, S, D = x.shape
    N = B * S
    mod = _mod(c, ada_w, ada_b).reshape(B, 6, 1, D)
    sh1, sc1, g1, sh2, sc2, g2 = (mod[:, j] for j in range(6))

    w_pad = jnp.pad(w_in, ((0, 0), (0, IN_COLS_PAD - IN_COLS))).astype(BF16)
    rq, rk, rv, rg, aqt, ak, avt, iqt, ik, iwt = _inproj(x, norm1_g, sc1, sh1, w_pad, _tile(S, PROJ_ROWS))
    ms = mix_scale.reshape(1, RET_W + DSA_W)
    ret = _retention(rq, rk, rv, rg, ms[:, :RET_W])
    att = _dsa(iqt, iwt, aqt, ik, ak, avt, ms[:, RET_W:], _tile(S, DSA_QUERIES), DSA_KEYS)

    rw_pad = jnp.pad(router_w, ((0, 0), (0, LANES - N_EXPERTS)))
    rw_hi = rw_pad.astype(BF16)
    rw_pad = jnp.stack([rw_hi, (rw_pad - rw_hi.astype(F32)).astype(BF16)])
    rb_pad = jnp.pad(router_b, (0, LANES - N_EXPERTS)).reshape(1, LANES)
    x1, h2, sel, idx_tab, gate_tab, counts = _oproj(ret, att, x, w_o.astype(BF16), g1, norm2_g, sc2, sh2,
                                                    rw_pad, rb_pad, _tile(S, PROJ_ROWS))

    tmb = FFN_ROWS
    n_rows = (N * TOP_K + N_EXPERTS * (tmb - 1)) // tmb * tmb + tmb
    cnt = counts[0, :N_EXPERTS].astype(I32)
    padded = (cnt + tmb - 1) // tmb * tmb
    ends = jnp.cumsum(padded)
    starts = ends - padded
    pstart = jnp.pad(starts.astype(F32), (0, LANES - N_EXPERTS)).reshape(1, LANES)
    n_blocks = (ends[-1] // tmb).reshape(1)
    first_row = jnp.arange(n_rows // tmb, dtype=I32) * tmb
    block_e = jnp.minimum(jnp.sum((ends[None, :] <= first_row[:, None]).astype(I32), axis=1), N_EXPERTS - 1)

    tmd = _tile(N, MOE_TOKENS)
    dest_tab = _dest(sel, idx_tab, pstart, tmd)
    dest_flat = dest_tab[:, :TOP_K].reshape(N * TOP_K)
    xs = _dispatch(h2, dest_flat, starts + cnt, n_rows, tmb, tmd)
    eid = jnp.arange(N_EXPERTS, dtype=I32)
    later_used = (eid[None, :] > eid[:, None]) & (padded[None, :] > 0)
    next_used = jnp.min(jnp.where(later_used, eid[None, :], N_EXPERTS), axis=1)
    next_e = jnp.where(next_used < N_EXPERTS, next_used, -1)[block_e].astype(I32)
    ys = _ffn(xs, block_e, n_blocks, next_e, w_gu, b_gu, w_down, b_down, tmb)
    out = _combine(x1, gate_tab, g2, final_g, dest_flat, ys, S, _tile(S, COMB_TOKENS))
    return out.reshape(B, S, D)


def kernel(x, c, ada_w, ada_b, norm1_g, w_in, mix_scale, w_o, norm2_g, router_w, router_b, w_gu, b_gu,
           w_down, b_down, final_g):
    assert ada_w.shape[0] == 1, "single-layer stack"
    return _layer(x, c, ada_w[0], ada_b[0], norm1_g[0], w_in[0], mix_scale[0], w_o[0], norm2_g[0],
                  router_w[0], router_b[0], w_gu[0], b_gu[0], w_down[0], b_down[0], final_g)
```

```python
import functools

import numpy as np
import jax
import jax.numpy as jnp
from jax import lax
from jax.experimental import pallas as pl
from jax.experimental.pallas import tpu as pltpu

F32 = jnp.float32
BF16 = jnp.bfloat16
I32 = jnp.int32

D_MODEL = 1024
RET_HEADS = 4
RET_DK = 64
RET_DV = 128
RET_CHUNK = 128
DSA_HEADS = 8
DSA_KV_HEADS = 2
DSA_HD = 64
IDX_HEADS = 8
IDX_HD = 64
TOPK_MAX = 256
N_EXPERTS = 32
TOP_K = 4
D_EXPERT = D_MODEL
SWIGLU_LIMIT = 7.0
SWIGLU_ALPHA = 1.702
EPS = 1e-6

RET_W = RET_HEADS * RET_DV
DSA_W = DSA_HEADS * DSA_HD
IN_COLS = 2888
IN_COLS_PAD = 2944

KAUG = 128
VAUG = 80
ALIBI_SPLIT = 64
ALIBI_TERMS = 3
LOG2E = 1.4426950408889634
PROJ_ROWS = 512
DSA_QUERIES = 512
DSA_KEYS = 128
MOE_TOKENS = 512
COMB_TOKENS = 256
RET_CHUNKS_PER_STEP = 8
FFN_ROWS = 512
OPROJ_PARTS = 2
FIX_UNROLL = 4
BISECT_COARSE_STEPS = 8
BISECT_MAX_STEPS = 40

LANES = 128
NCH = D_MODEL // LANES
VMEM_LIMIT = 56 * 1024 * 1024
NEG_BIG = -1e30
F32_LOWEST = float(np.finfo(np.float32).min)


def _cparams(sem):
    return pltpu.CompilerParams(dimension_semantics=sem, vmem_limit_bytes=VMEM_LIMIT)


def _mod_kernel(c_ref, w_ref, b_ref, o_ref):
    c = c_ref[...]
    s = c * (1.0 / (1.0 + jnp.exp(-c)))
    o_ref[...] = jnp.dot(s, w_ref[...], preferred_element_type=F32,
                         precision=lax.Precision.HIGHEST) + b_ref[...]


def _mod(c, ada_w, ada_b):
    B, D = c.shape
    n_out = ada_w.shape[1]
    rows = 8
    c8 = jnp.zeros((rows, D), F32).at[:B].set(c)
    out = pl.pallas_call(
        _mod_kernel,
        grid=(n_out // D,),
        in_specs=[pl.BlockSpec((rows, D), lambda j: (0, 0)),
                  pl.BlockSpec((D, D), lambda j: (0, j)),
                  pl.BlockSpec((1, D), lambda j: (0, j))],
        out_specs=pl.BlockSpec((rows, D), lambda j: (0, j)),
        out_shape=jax.ShapeDtypeStruct((rows, n_out), F32),
        compiler_params=_cparams(("arbitrary",)),
        name="mod",
    )(c8, ada_w, ada_b.reshape(1, n_out))
    return out[:B]


def _inproj_kernel(x_ref, g_ref, sc_ref, sh_ref, w_ref,
                   rq_ref, rk_ref, rv_ref, rg_ref, aqt_ref, ak_ref, avt_ref, iqt_ref, ik_ref, iwt_ref):
    x = x_ref[...]
    ms = jnp.mean(x * x, axis=-1, keepdims=True)
    y = x * lax.rsqrt(ms + EPS) * g_ref[...]
    hb = (y * (1.0 + sc_ref[...]) + sh_ref[...]).astype(BF16)

    def proj(lo, hi):
        return jnp.dot(hb, w_ref[:, lo:hi], preferred_element_type=F32)

    tm = x.shape[0]
    d = DSA_HD
    rq_ref[...] = proj(0, 256).astype(BF16)
    rk_ref[...] = (proj(256, 512) * (RET_DK ** -0.5)).astype(BF16)
    rv_ref[...] = proj(512, 1024).astype(BF16)
    rg_ref[...] = proj(1024, 1536).astype(BF16)
    aqt_ref[...] = (proj(1536, 2048) * (d ** -0.5 * LOG2E)).T.astype(BF16)
    kk = proj(2048, 2176)
    pos = pl.program_id(1) * tm + lax.broadcasted_iota(I32, (tm, d), 0)
    col = lax.broadcasted_iota(I32, (tm, d), 1)
    posblk = jnp.where(col < ALIBI_TERMS, pos // ALIBI_SPLIT,
                       jnp.where(col < 2 * ALIBI_TERMS, pos % ALIBI_SPLIT, 0)).astype(F32)
    for g in range(DSA_KV_HEADS):
        ak_ref[:, g * KAUG:g * KAUG + d] = kk[:, g * d:(g + 1) * d].astype(BF16)
        ak_ref[:, g * KAUG + d:(g + 1) * KAUG] = posblk.astype(BF16)
    vt = proj(2176, 2304).T
    r16 = lax.broadcasted_iota(I32, (VAUG - d, tm), 0)
    onesblk = jnp.where(r16 == 0, 1.0, 0.0).astype(BF16)
    for g in range(DSA_KV_HEADS):
        avt_ref[g * VAUG:g * VAUG + d, :] = vt[g * d:(g + 1) * d, :].astype(BF16)
        avt_ref[g * VAUG + d:(g + 1) * VAUG, :] = onesblk
    iqt_ref[...] = proj(2304, 2816).T.astype(BF16)
    last = proj(2816, 2944)
    ik_ref[...] = last[:, :IDX_HD].astype(BF16)
    iwt_ref[...] = last.T[IDX_HD:IDX_HD + IDX_HEADS, :] * ((IDX_HD ** -0.5) * (IDX_HEADS ** -0.5))


def _inproj(x, norm_g, sc, sh, w_pad, tm):
    B, S, D = x.shape
    row = lambda w: pl.BlockSpec((None, tm, w), lambda b, i: (b, i, 0))
    colT = lambda h: pl.BlockSpec((None, h, tm), lambda b, i: (b, 0, i))
    vec = pl.BlockSpec((None, 1, D), lambda b, i: (b, 0, 0))
    sd = lambda shape, dt: jax.ShapeDtypeStruct(shape, dt)
    G = DSA_KV_HEADS
    return pl.pallas_call(
        _inproj_kernel,
        grid=(B, S // tm),
        in_specs=[row(D), pl.BlockSpec((1, D), lambda b, i: (0, 0)), vec, vec,
                  pl.BlockSpec((D, IN_COLS_PAD), lambda b, i: (0, 0))],
        out_specs=[row(256), row(256), row(512), row(512), colT(DSA_W), row(G * KAUG), colT(G * VAUG),
                   colT(IDX_HEADS * IDX_HD), row(IDX_HD), colT(IDX_HEADS)],
        out_shape=[sd((B, S, 256), BF16), sd((B, S, 256), BF16), sd((B, S, 512), BF16),
                   sd((B, S, 512), BF16), sd((B, DSA_W, S), BF16), sd((B, S, G * KAUG), BF16),
                   sd((B, G * VAUG, S), BF16), sd((B, IDX_HEADS * IDX_HD, S), BF16),
                   sd((B, S, IDX_HD), BF16), sd((B, IDX_HEADS, S), F32)],
        compiler_params=_cparams(("parallel", "parallel")),
        name="inproj",
    )(x, norm_g.reshape(1, D), sc, sh, w_pad)


def _ret_kernel(rq_ref, rk_ref, rv_ref, rg_ref, din_ref, qd_ref, kd_ref, cd_ref, ms_ref, o_ref, state_ref):
    @pl.when(pl.program_id(1) == 0)
    def _():
        state_ref[...] = jnp.zeros_like(state_ref)

    C = din_ref.shape[1]
    for c in range(rq_ref.shape[0] // C):
        rows = slice(c * C, (c + 1) * C)
        for h in range(RET_HEADS):
            q = rq_ref[rows, h * RET_DK:(h + 1) * RET_DK]
            k = rk_ref[rows, h * RET_DK:(h + 1) * RET_DK]
            v = rv_ref[rows, h * RET_DV:(h + 1) * RET_DV]
            r_prev = state_ref[h]
            s = lax.dot_general(q, k, (((1,), (1,)), ((), ())), preferred_element_type=F32) * din_ref[h]
            o = jnp.dot(s.astype(BF16), v, preferred_element_type=F32)
            o = o + jnp.dot(q, r_prev.astype(BF16), preferred_element_type=F32) * qd_ref[h]
            vd = (v.astype(F32) * kd_ref[h]).astype(BF16)
            kv = lax.dot_general(k, vd, (((0,), (0,)), ((), ())), preferred_element_type=F32)
            state_ref[h] = r_prev * cd_ref[h] + kv
            o = o * lax.rsqrt(jnp.mean(o * o, axis=-1, keepdims=True) + EPS)
            g = rg_ref[rows, h * RET_DV:(h + 1) * RET_DV].astype(F32)
            gate = g * (1.0 / (1.0 + jnp.exp(-g)))
            o_ref[rows, h * RET_DV:(h + 1) * RET_DV] = (
                gate * o * ms_ref[:, h * RET_DV:(h + 1) * RET_DV]).astype(BF16)


def _ret_consts(C):
    H = RET_HEADS
    log_g = np.log1p(-np.exp2(-5.0 - np.arange(H, dtype=np.float64)))
    pos = np.arange(C, dtype=np.float64)
    diff = pos[:, None] - pos[None, :]
    d_inner = np.where(diff[None] >= 0, np.exp(np.maximum(diff, 0.0)[None] * log_g[:, None, None]), 0.0)
    q_decay = np.exp((pos + 1.0)[None] * log_g[:, None])
    k_decay = np.exp((C - 1.0 - pos)[None] * log_g[:, None])
    chunk_decay = np.exp(C * log_g)
    qd = np.broadcast_to(q_decay[:, :, None], (H, C, RET_DV))
    kd = np.broadcast_to(k_decay[:, :, None], (H, C, RET_DV))
    cd = np.broadcast_to(chunk_decay[:, None, None], (H, 1, RET_DV))
    f = lambda a: jnp.asarray(np.ascontiguousarray(a), F32)
    return f(d_inner), f(qd), f(kd), f(cd)


def _retention(rq, rk, rv, rg, ms_ret):
    B, S, _ = rq.shape
    C = min(RET_CHUNK, S)
    din, qd, kd, cd = _ret_consts(C)
    rows = _tile(S, RET_CHUNKS_PER_STEP * C)
    row = lambda w: pl.BlockSpec((None, rows, w), lambda b, n: (b, n, 0))
    full = lambda a: pl.BlockSpec(a.shape, lambda b, n: (0,) * a.ndim)
    return pl.pallas_call(
        _ret_kernel,
        grid=(B, S // rows),
        in_specs=[row(256), row(256), row(512), row(512), full(din), full(qd), full(kd), full(cd),
                  pl.BlockSpec((1, RET_W), lambda b, n: (0, 0))],
        out_specs=row(RET_W),
        out_shape=jax.ShapeDtypeStruct((B, S, RET_W), BF16),
        scratch_shapes=[pltpu.VMEM((RET_HEADS, RET_DK, RET_DV), F32)],
        compiler_params=_cparams(("parallel", "arbitrary")),
        name="ret",
    )(rq, rk, rv, rg, din, qd, kd, cd, ms_ret)


def _f32_key(x):
    i = lax.bitcast_convert_type(x, I32)
    return i ^ ((i >> 31) & 0x7FFFFFFF)


def _key_f32(k):
    return lax.bitcast_convert_type(k ^ ((k >> 31) & 0x7FFFFFFF), F32)


def _floor16(x):
    i = lax.bitcast_convert_type(x, I32)
    return lax.bitcast_convert_type((i + ((i >> 31) & 0xFFFF)) & -0x10000, F32)


def _trunc16(x):
    return lax.bitcast_convert_type(lax.bitcast_convert_type(x, I32) & -0x10000, F32)


def _dsa_kernel(iqt_ref, iwt_ref, aqt_ref, ik_ref, ak_ref, avt_ref, ms_ref, o_ref, score_ref, code_ref, qa_ref, sa_ref, sb_ref, mask_ref, *acc_refs,
                tq, tks, n_sel):
    H, G, R, d = DSA_HEADS, DSA_KV_HEADS, DSA_HEADS // DSA_KV_HEADS, DSA_HD
    t0 = pl.program_id(1) * tq
    nsub = (t0 + tq) // tks
    tka = 2 * tks
    npair = ((t0 + tq) // tka + 1) // 2
    kf = float(n_sel)
    qpos = t0 + lax.broadcasted_iota(I32, (1, tq), 1)
    krow = lax.broadcasted_iota(I32, (tks, tq), 0)

    wrow = [iwt_ref[h:h + 1, :] for h in range(IDX_HEADS)]

    def fold8(x, op):
        acc = x[0:8, :]
        for i in range(1, tks // 8):
            acc = op(acc, x[8 * i:8 * (i + 1), :])
        return acc

    def score_pair(i, carry, masked):
        mx, mn, npos, nnon = carry
        for u in range(2 * tka // tks):
            r0 = pl.multiple_of(i * 2 * tka + u * tks, tks)
            kc = ik_ref[pl.ds(r0, tks), :]
            acc = jnp.zeros((tks, tq), F32)
            for h in range(IDX_HEADS):
                rel = jnp.dot(kc, iqt_ref[h * IDX_HD:(h + 1) * IDX_HD, :], preferred_element_type=F32)
                acc = acc + jnp.maximum(rel, 0.0) * wrow[h]
            if masked:
                causal = r0 + krow <= qpos
                sc = jnp.where(causal, acc, -jnp.inf)
                lowest = jnp.where(causal, acc, jnp.inf)
            else:
                sc = lowest = acc
            score_ref[pl.ds(r0, tks), :] = sc
            code_ref[pl.ds(r0, tks), :] = _trunc16(sc).astype(BF16)
            mx = jnp.maximum(mx, fold8(sc, jnp.maximum))
            mn = jnp.minimum(mn, fold8(lowest, jnp.minimum))
            npos = npos + fold8(jnp.where(sc > 0.0, 1.0, 0.0), jnp.add)
            nnon = nnon + fold8(jnp.where(sc >= 0.0, 1.0, 0.0), jnp.add)
        return mx, mn, npos, nnon

    stat0 = (jnp.full((8, tq), -jnp.inf, F32), jnp.full((8, tq), jnp.inf, F32),
             jnp.zeros((8, tq), F32), jnp.zeros((8, tq), F32))
    n_inner = (t0 + 1) // (2 * tka)
    stat = lax.fori_loop(0, n_inner, functools.partial(score_pair, masked=False), stat0)
    mx, mn, npos, nnon = lax.fori_loop(n_inner, npair, functools.partial(score_pair, masked=True), stat)
    top = jnp.max(mx, axis=0, keepdims=True)
    lo0 = jnp.min(mn, axis=0, keepdims=True)
    n_pos = jnp.sum(npos, axis=0, keepdims=True)
    n_nonneg = jnp.sum(nnon, axis=0, keepdims=True)


    def count(th, strict):
        def body(j, acc):
            for u in range(2):
                s = score_ref[pl.ds(pl.multiple_of((2 * j + u) * tks, tks), tks), :]
                hit = (s > th) if strict else (s >= th)
                acc = acc + fold8(jnp.where(hit, 1.0, 0.0), jnp.add)
            return acc

        acc = lax.fori_loop(0, nsub // 2, body, jnp.zeros((8, tq), F32))
        return jnp.sum(acc, axis=0, keepdims=True)

    def probe(lo, hi):
        lk, hk = _f32_key(lo), _f32_key(hi)
        mk = (lk >> 1) + (hk >> 1) + (lk & hk & 1)
        return _key_f32(mk), jnp.max(jnp.where(mk != lk, 1.0, 0.0))

    def bis_cond(c):
        return (c[5] > 0.0) & (c[6] < BISECT_MAX_STEPS)

    def bis_body(c):
        lo, hi, c_lo, c_hi, mid, _, it = c
        cnt = count(mid, False)
        ge = cnt >= kf
        up = ge | (cnt == kf)
        dn = (~ge) | (cnt == kf)
        lo, c_lo = jnp.where(up, mid, lo), jnp.where(up, cnt, c_lo)
        hi, c_hi = jnp.where(dn, mid, hi), jnp.where(dn, cnt, c_hi)
        mid, active = probe(lo, hi)
        return lo, hi, c_lo, c_hi, mid, active, it + 1

    zero = jnp.zeros((1, tq), F32)
    keep_all = qpos + 1 <= n_sel
    settled = keep_all | ((n_nonneg >= kf) & (n_pos < kf))
    above = n_pos >= kf
    c_lo0 = jnp.where(settled | above, n_nonneg, (qpos + 1).astype(F32))
    c_hi0 = jnp.where(settled | ~above, n_nonneg, zero)
    lo0 = jnp.where(settled | above, zero, lo0)
    hi0 = jnp.where(settled | ~above, zero, _key_f32(_f32_key(top) + 1))
    def count16(m):
        mb = m.astype(BF16)

        def body(j, acc):
            for u in range(2):
                c = code_ref[pl.ds(pl.multiple_of((2 * j + u) * tks, tks), tks), :]
                one = jnp.where(c >= mb, jnp.ones((), BF16), jnp.zeros((), BF16))
                part = one[0:16, :]
                for i in range(1, tks // 16):
                    part = part + one[16 * i:16 * (i + 1), :]
                acc = acc + part.astype(F32)
            return acc

        acc = lax.fori_loop(0, nsub // 2, body, jnp.zeros((16, tq), F32))
        return jnp.sum(acc, axis=0, keepdims=True)

    def coarse_step(_, c):
        lo, hi, c_lo, c_hi = c
        below = _floor16(lo + (hi - lo) * 0.5)
        above = _key_f32(_f32_key(below) + 0x10000)
        m = jnp.where(below > lo, below, above)
        v = jnp.where(m > 0.0, m, _key_f32(_f32_key(m) - 0x10000 + 1))
        ok = (v > lo) & (v < hi)
        cnt = count16(m)
        ge = cnt >= kf
        up = ok & ge
        dn = ok & ((~ge) | (cnt == kf))
        return (jnp.where(up, v, lo), jnp.where(dn, v, hi), jnp.where(up, cnt, c_lo), jnp.where(dn, cnt, c_hi))

    lo0, hi0, c_lo0, c_hi0 = lax.fori_loop(0, BISECT_COARSE_STEPS, coarse_step, (lo0, hi0, c_lo0, c_hi0))
    mid0, active0 = probe(lo0, hi0)
    lo, hi, c_lo, c_hi, _, _, _ = lax.while_loop(bis_cond, bis_body,
                                                 (lo0, hi0, c_lo0, c_hi0, mid0, active0, jnp.int32(0)))
    at_hi = c_hi >= kf
    thr = jnp.where(keep_all, F32_LOWEST, jnp.where(at_hi, hi, lo))
    excess = jnp.where(keep_all, 0.0, jnp.where(at_hi, c_hi, c_lo) - kf)

    @pl.when(jnp.max(excess) > 0.0)
    def _():
        tied_nonzero = jnp.max(jnp.where((excess > 0.0) & (thr != 0.0), 1.0, 0.0)) > 0.0
        n_above = lax.cond(tied_nonzero, lambda: count(thr, True), lambda: n_pos)
        budget = jnp.where(excess > 0.0, kf - jnp.where(thr == 0.0, n_pos, n_above), jnp.inf)
        earlier = lax.broadcasted_iota(I32, (tks, tks), 1) < lax.broadcasted_iota(I32, (tks, tks), 0)
        earlier = jnp.where(earlier, 1.0, 0.0).astype(BF16)

        def fix(j, seen):
            tiles = []
            for u in range(FIX_UNROLL):
                r0 = pl.multiple_of((FIX_UNROLL * j + u) * tks, tks)
                s = score_ref[pl.ds(r0, tks), :]
                eq = s == thr
                eqf = jnp.where(eq, 1.0, 0.0)
                within = jnp.dot(earlier, eqf.astype(BF16), preferred_element_type=F32)
                tiles.append((r0, s, eq, within, jnp.sum(eqf, axis=0, keepdims=True)))
            for r0, s, eq, within, n_eq in tiles:
                score_ref[pl.ds(r0, tks), :] = jnp.where(eq & (within + seen >= budget), -jnp.inf, s)
                seen = seen + n_eq
            return seen

        lax.fori_loop(0, nsub // FIX_UNROLL, fix, jnp.zeros((1, tq), F32))

    arow = lax.broadcasted_iota(I32, (KAUG - d, tq), 0)
    for h in range(H):
        terms, rest = [], 2.0 ** (-8.0 * (h + 1) / H) * LOG2E
        for _ in range(ALIBI_TERMS):
            terms.append(float(np.asarray(rest, np.float32).astype(BF16).astype(np.float64)))
            rest -= terms[-1]
        rows = jnp.zeros((KAUG - d, tq), F32)
        for i, c in enumerate(terms):
            rows = jnp.where(arow == i, c * ALIBI_SPLIT, jnp.where(arow == ALIBI_TERMS + i, c, rows))
        qa_ref[h, 0:d, :] = aqt_ref[h * d:(h + 1) * d, :]
        qa_ref[h, d:KAUG, :] = rows.astype(BF16)
    for acc in acc_refs:
        acc[...] = jnp.zeros_like(acc)

    def logits(j, h):
        ka = ak_ref[pl.ds(pl.multiple_of(j * tka, tka), tka), (h // R) * KAUG:(h // R + 1) * KAUG]
        return jnp.dot(ka, qa_ref[h], preferred_element_type=F32)

    def step(j, j_next, cur_ref, next_ref, ms):
        r0 = pl.multiple_of(j * tka, tka)
        mask_ref[...] = jnp.where(score_ref[pl.ds(r0, tka), :] >= thr, 0.0, NEG_BIG)
        nms = []
        ahead = logits(j_next, 0)
        for h in range(H):
            g = h // R
            next_ref[h] = ahead
            if h + 1 < H:
                ahead = logits(j_next, h + 1)
            s = cur_ref[h] + mask_ref[...]
            m_new = jnp.maximum(ms[h], jnp.max(s, axis=0, keepdims=True))
            p = jnp.exp2(s - m_new).astype(BF16)
            va = avt_ref[g * VAUG:(g + 1) * VAUG, pl.ds(r0, tka)]
            acc = acc_refs[h]
            acc[...] = acc[...] * jnp.exp2(ms[h] - m_new) + jnp.dot(va, p, preferred_element_type=F32)
            nms.append(m_new)
        return tuple(nms)

    for h in range(H):
        sa_ref[h] = logits(0, h)

    def att_pair(i, ms):
        ms = step(2 * i, 2 * i + 1, sa_ref, sb_ref, ms)
        return step(2 * i + 1, jnp.minimum(2 * i + 2, 2 * npair - 1), sb_ref, sa_ref, ms)

    lax.fori_loop(0, npair, att_pair, tuple(jnp.full((1, tq), NEG_BIG, F32) for _ in range(H)))
    for h in range(H):
        a = acc_refs[h][...]
        o = a[0:d, :] / a[d:d + 1, :]
        o_ref[h * d:(h + 1) * d, :] = (o * ms_ref[h * d:(h + 1) * d, :]).astype(BF16)


def _dsa(iqt, iwt, aqt, ik, ak, avt, ms_att, tq, tks):
    B, _, S = iqt.shape
    assert S <= ALIBI_SPLIT * 256 and tq % (2 * tks) == 0 and tq % (FIX_UNROLL * tks) == 0
    assert (S // (2 * tks)) % 2 == 0
    n_sel = min(TOPK_MAX, S // 4)
    G = DSA_KV_HEADS
    colT = lambda h: pl.BlockSpec((None, h, tq), lambda b, i: (b, 0, i))
    msb = jnp.broadcast_to(ms_att.reshape(DSA_W, 1), (DSA_W, tq))
    return pl.pallas_call(
        functools.partial(_dsa_kernel, tq=tq, tks=tks, n_sel=n_sel),
        grid=(B, S // tq),
        in_specs=[colT(IDX_HEADS * IDX_HD), colT(IDX_HEADS), colT(DSA_W),
                  pl.BlockSpec((None, S, IDX_HD), lambda b, i: (b, 0, 0), pipeline_mode=pl.Buffered(1)),
                  pl.BlockSpec((None, S, G * KAUG), lambda b, i: (b, 0, 0), pipeline_mode=pl.Buffered(1)),
                  pl.BlockSpec((None, G * VAUG, S), lambda b, i: (b, 0, 0), pipeline_mode=pl.Buffered(1)),
                  pl.BlockSpec((DSA_W, tq), lambda b, i: (0, 0))],
        out_specs=colT(DSA_W),
        out_shape=jax.ShapeDtypeStruct((B, DSA_W, S), BF16),
        scratch_shapes=[pltpu.VMEM((S, tq), F32), pltpu.VMEM((S, tq), BF16), pltpu.VMEM((DSA_HEADS, KAUG, tq), BF16),
                        pltpu.VMEM((DSA_HEADS, 2 * tks, tq), F32), pltpu.VMEM((DSA_HEADS, 2 * tks, tq), F32),
                        pltpu.VMEM((2 * tks, tq), F32)]
        + [pltpu.VMEM((VAUG, tq), F32) for _ in range(DSA_HEADS)],
        compiler_params=_cparams(("parallel", "arbitrary")),
        name="dsa",
    )(iqt, iwt, aqt, ik, ak, avt, msb)


def _oproj_kernel(ret_ref, att_ref, x_ref, wo_ref, g1_ref, n2_ref, sc_ref, sh_ref, rw_ref, rb_ref,
                  x1_ref, h2_ref, sel_ref, idx_ref, gate_ref, cnt_ref):
    tm = x_ref.shape[0]
    parts = [slice(r, r + tm // OPROJ_PARTS) for r in range(0, tm, tm // OPROJ_PARTS)]
    nch = x_ref.shape[1] // LANES

    def mix(rows):
        out = jnp.dot(ret_ref[rows, :], wo_ref[:RET_W, :], preferred_element_type=F32)
        return out + lax.dot_general(att_ref[:, rows], wo_ref[RET_W:, :], (((0,), (0,)), ((), ())),
                                     preferred_element_type=F32)

    def hidden(rows, mixo):
        x1 = x_ref[rows, :] + g1_ref[...] * mixo
        x1_ref[rows, :] = x1
        y = x1 * lax.rsqrt(jnp.mean(x1 * x1, axis=-1, keepdims=True) + EPS) * n2_ref[...]
        h2 = y * (1.0 + sc_ref[...]) + sh_ref[...]
        for j in range(nch):
            h2_ref[pl.ds(rows.start * nch + j, rows.stop - rows.start, stride=nch), :] = h2[:, j * LANES:(j + 1) * LANES]
        return h2

    def router(h2):
        h_hi = h2.astype(BF16)
        h_lo = (h2 - h_hi.astype(F32)).astype(BF16)
        logits = jnp.dot(h_hi, rw_ref[0], preferred_element_type=F32)
        return logits + (jnp.dot(h_hi, rw_ref[1], preferred_element_type=F32)
                         + jnp.dot(h_lo, rw_ref[0], preferred_element_type=F32)) + rb_ref[...]

    def top4(rows, logits):
        n = logits.shape[0]
        lane = lax.broadcasted_iota(I32, (n, LANES), 1).astype(F32)
        work = jnp.where(lane < N_EXPERTS, logits, -jnp.inf)
        sel = jnp.zeros((n, LANES), F32)
        idx_tab = jnp.zeros((n, LANES), F32)
        vals = []
        for k in range(TOP_K):
            m = jnp.max(work, axis=1, keepdims=True)
            idx = jnp.min(jnp.where(work == m, lane, float(LANES)), axis=1, keepdims=True)
            hit = lane == idx
            sel = jnp.where(hit, 1.0, sel)
            idx_tab = jnp.where(lane == k, idx, idx_tab)
            work = jnp.where(hit, -jnp.inf, work)
            vals.append(m)
        es = [jnp.exp(v - vals[0]) for v in vals]
        den = es[0] + es[1] + es[2] + es[3]
        gate_tab = jnp.zeros((n, LANES), F32)
        for k in range(TOP_K):
            gate_tab = jnp.where(lane == k, es[k] / den, gate_tab)
        sel_ref[rows, :] = sel
        idx_ref[rows, :] = idx_tab
        gate_ref[rows, :] = gate_tab
        return jnp.sum(sel, axis=0, keepdims=True)

    mixes = [mix(rows) for rows in parts]
    hiddens = [hidden(rows, m) for rows, m in zip(parts, mixes)]
    logit_parts = [router(h2) for h2 in hiddens]
    counts = [top4(rows, lg) for rows, lg in zip(parts, logit_parts)]

    @pl.when((pl.program_id(0) == 0) & (pl.program_id(1) == 0))
    def _():
        cnt_ref[...] = jnp.zeros_like(cnt_ref)

    cnt_ref[...] += functools.reduce(lambda a, b: a + b, counts)


def _oproj(ret, att, x, wo, g1, n2g, sc2, sh2, rw_pad, rb_pad, tm):
    B, S, D = x.shape
    nt = S // tm
    row = lambda w: pl.BlockSpec((None, tm, w), lambda b, i: (b, i, 0))
    flat = lambda w: pl.BlockSpec((tm, w), lambda b, i: (b * nt + i, 0))
    vec = pl.BlockSpec((None, 1, D), lambda b, i: (b, 0, 0))
    cst = lambda shape: pl.BlockSpec(shape, lambda b, i: (0, 0))
    sd = lambda shape, dt: jax.ShapeDtypeStruct(shape, dt)
    N = B * S
    return pl.pallas_call(
        _oproj_kernel,
        grid=(B, nt),
        in_specs=[row(RET_W), pl.BlockSpec((None, DSA_W, tm), lambda b, i: (b, 0, i)), row(D), cst((D, D)), vec,
                  cst((1, D)), vec, vec,
                  pl.BlockSpec((2, D, LANES), lambda b, i: (0, 0, 0)), cst((1, LANES))],
        out_specs=[flat(D), pl.BlockSpec((tm * (D // LANES), LANES), lambda b, i: (b * nt + i, 0)),
                   flat(LANES), flat(LANES), flat(LANES), cst((1, LANES))],
        out_shape=[sd((N, D), F32), sd((N * (D // LANES), LANES), F32), sd((N, LANES), F32), sd((N, LANES), F32),
                   sd((N, LANES), F32), sd((1, LANES), F32)],
        compiler_params=_cparams(("arbitrary", "arbitrary")),
        name="oproj",
    )(ret, att, x, wo, g1, n2g.reshape(1, D), sc2, sh2, rw_pad, rb_pad)


def _dest_kernel(sel_ref, idx_ref, pstart_ref, dest_ref, seen_ref):
    @pl.when(pl.program_id(0) == 0)
    def _():
        seen_ref[...] = jnp.zeros_like(seen_ref)

    sel = sel_ref[...]
    tm = sel.shape[0]
    earlier = lax.broadcasted_iota(I32, (tm, tm), 1) < lax.broadcasted_iota(I32, (tm, tm), 0)
    earlier = jnp.where(earlier, 1.0, 0.0).astype(BF16)
    rank = jnp.dot(earlier, sel.astype(BF16), preferred_element_type=F32) + seen_ref[...]
    dest = pstart_ref[...] + rank
    lane = lax.broadcasted_iota(I32, (tm, LANES), 1).astype(F32)
    idx_tab = idx_ref[...]
    out = jnp.zeros((tm, LANES), F32)
    for k in range(TOP_K):
        e_k = jnp.sum(jnp.where(lane == k, idx_tab, 0.0), axis=1, keepdims=True)
        d_k = jnp.sum(jnp.where(lane == e_k, dest, 0.0), axis=1, keepdims=True)
        out = jnp.where(lane == k, d_k, out)
    dest_ref[...] = out.astype(I32)
    seen_ref[...] += jnp.sum(sel, axis=0, keepdims=True)


def _dest(sel, idx_tab, pstart, tm):
    N = sel.shape[0]
    blk = pl.BlockSpec((tm, LANES), lambda i: (i, 0))
    return pl.pallas_call(
        _dest_kernel,
        grid=(N // tm,),
        in_specs=[blk, blk, pl.BlockSpec((1, LANES), lambda i: (0, 0))],
        out_specs=blk,
        out_shape=jax.ShapeDtypeStruct((N, LANES), I32),
        scratch_shapes=[pltpu.VMEM((1, LANES), F32)],
        compiler_params=_cparams(("arbitrary",)),
        name="dest",
    )(sel, idx_tab, pstart)


def _disp_kernel(zs_ref, h2_ref, dest_hbm, xs_hbm, zbuf, idx_a, idx_b, sem_ia, sem_ib, sem_row, sem_z,
                 *, tm, nch, zrows, n_steps):
    i = pl.program_id(0)
    n = tm * TOP_K

    def idx_copy(tile, buf, sem):
        return pltpu.make_async_copy(dest_hbm.at[pl.ds(tile * n, n)], buf, sem)

    def rows(ref, first, count):
        return ref.at[pl.ds(pl.multiple_of(first * nch, nch), count * nch)]

    def row_copy(row, dst):
        return pltpu.make_async_copy(rows(h2_ref, row, 1), rows(xs_hbm, dst, 1), sem_row)

    def issue_rows(first_row, idx):
        def body(r, c):
            for s in range(8):
                t = r * 8 + s
                for k in range(TOP_K):
                    row_copy(first_row + t, idx[t * TOP_K + k]).start(priority=k % 2)
            return c

        lax.fori_loop(0, tm // 8, body, 0)

    def wait_rows():
        pltpu.make_async_copy(rows(xs_hbm, 0, n), rows(xs_hbm, 0, n), sem_row).wait()

    @pl.when(i == 0)
    def _():
        zbuf[...] = jnp.zeros_like(zbuf)
        for e in range(N_EXPERTS):
            fill = pltpu.make_async_copy(zbuf, rows(xs_hbm, zs_ref[e], zrows), sem_z)
            fill.start()
            fill.wait()
        idx_copy(0, idx_a, sem_ia).start()
        idx_copy(1, idx_b, sem_ib).start()

    idx_copy(2 * i, idx_a, sem_ia).wait()
    issue_rows(0, idx_a)
    idx_copy(2 * i + 1, idx_b, sem_ib).wait()
    issue_rows(tm, idx_b)

    @pl.when(i + 1 < n_steps)
    def _():
        idx_copy(2 * i + 2, idx_a, sem_ia).start()
        idx_copy(2 * i + 3, idx_b, sem_ib).start()

    wait_rows()
    wait_rows()


def _dispatch(h2, dest_flat, zero_start, n_rows, zrows, tm):
    nch = D_MODEL // LANES
    n_steps = h2.shape[0] // nch // (2 * tm)
    grid_spec = pltpu.PrefetchScalarGridSpec(
        num_scalar_prefetch=1,
        grid=(n_steps,),
        in_specs=[pl.BlockSpec((2 * tm * nch, LANES), lambda i, zs: (i, 0)), pl.BlockSpec(memory_space=pl.ANY)],
        out_specs=pl.BlockSpec(memory_space=pl.ANY),
        scratch_shapes=[pltpu.VMEM((zrows * nch, LANES), F32), pltpu.SMEM((tm * TOP_K,), I32),
                        pltpu.SMEM((tm * TOP_K,), I32), pltpu.SemaphoreType.DMA, pltpu.SemaphoreType.DMA,
                        pltpu.SemaphoreType.DMA, pltpu.SemaphoreType.DMA],
    )
    return pl.pallas_call(
        functools.partial(_disp_kernel, tm=tm, nch=nch, zrows=zrows, n_steps=n_steps),
        grid_spec=grid_spec,
        out_shape=jax.ShapeDtypeStruct((n_rows * nch, LANES), F32),
        compiler_params=_cparams(("arbitrary",)),
        name="disp",
    )(zero_start, h2, dest_flat)


def _store_token_major(ref, x):
    rows, d = x.shape
    nch = d // LANES
    for j in range(nch):
        ref[pl.ds(j, rows, stride=nch), :] = x[:, j * LANES:(j + 1) * LANES]


def _load_token_major(ref, rows, nch):
    return jnp.concatenate([ref[pl.ds(j, rows, stride=nch), :] for j in range(nch)], axis=1)


def _ffn_kernel(be_ref, nb_ref, nxt_ref, xs_ref, wgu_hbm, bgu_ref, wd_hbm, bd_ref, ys_ref,
                wgu_st, wd_st, wgu_bf, wd_bf, sem_gu, sem_d):
    i = pl.program_id(0)
    live = i < nb_ref[0]
    e = be_ref[i]

    def fetch(ex):
        return (pltpu.make_async_copy(wgu_hbm.at[ex], wgu_st, sem_gu),
                pltpu.make_async_copy(wd_hbm.at[ex], wd_st, sem_d))

    @pl.when(i == 0)
    def _():
        for cp in fetch(e):
            cp.start()

    @pl.when(live & ((i == 0) | (e != be_ref[jnp.maximum(i - 1, 0)])))
    def _():
        for cp in fetch(e):
            cp.wait()
        wgu_bf[...] = wgu_st[...].astype(BF16)
        wd_bf[...] = wd_st[...].astype(BF16)

        @pl.when(nxt_ref[i] >= 0)
        def _():
            for cp in fetch(nxt_ref[i]):
                cp.start()

    @pl.when(live)
    def _():
        xb = _load_token_major(xs_ref, ys_ref.shape[0] // NCH, NCH).astype(BF16)
        gu = jnp.dot(xb, wgu_bf[...], preferred_element_type=F32) + bgu_ref[...]
        gate = jnp.minimum(gu[:, :D_EXPERT], SWIGLU_LIMIT)
        up = jnp.clip(gu[:, D_EXPERT:], -SWIGLU_LIMIT, SWIGLU_LIMIT)
        glu = gate * (1.0 / (1.0 + jnp.exp(-SWIGLU_ALPHA * gate)))
        act = ((up + 1.0) * glu).astype(BF16)
        ys = jnp.dot(act, wd_bf[...], preferred_element_type=F32) + bd_ref[...]
        _store_token_major(ys_ref, ys)

    @pl.when(jnp.logical_not(live))
    def _():
        ys_ref[...] = jnp.zeros_like(ys_ref)


def _ffn(xs, block_e, n_blocks, next_e, w_gu, b_gu, w_down, b_down, tmb):
    D = D_MODEL
    P = xs.shape[0] // NCH
    E = w_gu.shape[0]
    blk = lambda i, be, nb, nx: (jnp.minimum(i, nb[0] - 1), 0)
    wsel = lambda i, be, nb, nx: (be[jnp.minimum(i, nb[0] - 1)], 0, 0)
    grid_spec = pltpu.PrefetchScalarGridSpec(
        num_scalar_prefetch=3,
        grid=(P // tmb,),
        in_specs=[pl.BlockSpec((tmb * NCH, LANES), blk),
                  pl.BlockSpec(memory_space=pl.ANY),
                  pl.BlockSpec((None, 1, 2 * D_EXPERT), wsel),
                  pl.BlockSpec(memory_space=pl.ANY),
                  pl.BlockSpec((None, 1, D), wsel)],
        out_specs=pl.BlockSpec((tmb * (D // LANES), LANES), lambda i, be, nb, nx: (i, 0)),
        scratch_shapes=[pltpu.VMEM((D, 2 * D_EXPERT), F32), pltpu.VMEM((D_EXPERT, D), F32),
                        pltpu.VMEM((D, 2 * D_EXPERT), BF16), pltpu.VMEM((D_EXPERT, D), BF16),
                        pltpu.SemaphoreType.DMA, pltpu.SemaphoreType.DMA],
    )
    return pl.pallas_call(
        _ffn_kernel,
        grid_spec=grid_spec,
        out_shape=jax.ShapeDtypeStruct((P * (D // LANES), LANES), F32),
        compiler_params=_cparams(("arbitrary",)),
        name="ffn",
    )(block_e, n_blocks, next_e, xs, w_gu, b_gu.reshape(E, 1, 2 * D_EXPERT), w_down, b_down.reshape(E, 1, D))


def _comb_kernel(x1_ref, gate_ref, g2_ref, fg_ref, dest_hbm, ys_hbm, o_ref, buf_a, buf_b, idx_a, idx_b,
                 sem_ia, sem_ib, sem_ra, sem_rb, *, tm, n_steps):
    i = pl.program_id(0)
    n = tm * TOP_K

    def idx_copy(tile, buf, sem):
        return pltpu.make_async_copy(dest_hbm.at[pl.ds(tile * n, n)], buf, sem)

    nch = x1_ref.shape[1] // LANES

    def row_copy(src, buf, k, t, sem):
        return pltpu.make_async_copy(ys_hbm.at[pl.ds(pl.multiple_of(src * nch, nch), nch)],
                                     buf.at[k, t[0], :, t[1]], sem)

    def issue_rows(idx, buf, sem):
        def body(r, c):
            for s in range(8):
                for k in range(TOP_K):
                    row_copy(idx[(r * 8 + s) * TOP_K + k], buf, k, (r, s), sem).start(priority=k % 2)
            return c

        lax.fori_loop(0, tm // 8, body, 0)

    def wait_rows(buf, sem):
        pltpu.make_async_copy(buf, buf, sem).wait()

    def finish(buf, lo):
        gates = gate_ref[lo:lo + tm, :]
        y = jnp.zeros((tm, x1_ref.shape[1]), F32)
        for k in range(TOP_K):
            rows = jnp.concatenate([buf[k, :, j].reshape(tm, LANES) for j in range(nch)], axis=1)
            y = y + gates[:, k:k + 1] * rows
        v = x1_ref[lo:lo + tm, :] + g2_ref[...] * y
        o_ref[lo:lo + tm, :] = v * lax.rsqrt(jnp.mean(v * v, axis=-1, keepdims=True) + EPS) * fg_ref[...]

    @pl.when(i == 0)
    def _():
        first = idx_copy(0, idx_a, sem_ia)
        first.start()
        first.wait()
        issue_rows(idx_a, buf_a, sem_ra)
        idx_copy(1, idx_b, sem_ib).start()

    idx_copy(2 * i + 1, idx_b, sem_ib).wait()
    issue_rows(idx_b, buf_b, sem_rb)

    @pl.when(i + 1 < n_steps)
    def _():
        idx_copy(2 * i + 2, idx_a, sem_ia).start()

    wait_rows(buf_a, sem_ra)
    finish(buf_a, 0)

    @pl.when(i + 1 < n_steps)
    def _():
        idx_copy(2 * i + 2, idx_a, sem_ia).wait()
        issue_rows(idx_a, buf_a, sem_ra)
        idx_copy(2 * i + 3, idx_b, sem_ib).start()

    wait_rows(buf_b, sem_rb)
    finish(buf_b, tm)


def _combine(x1, gate_tab, g2, final_g, dest_flat, ys, S, tm):
    N, D = x1.shape
    n_steps = N // (2 * tm)
    per_b = S // (2 * tm)
    return pl.pallas_call(
        functools.partial(_comb_kernel, tm=tm, n_steps=n_steps),
        grid=(n_steps,),
        in_specs=[pl.BlockSpec((2 * tm, D), lambda i: (i, 0)),
                  pl.BlockSpec((2 * tm, LANES), lambda i: (i, 0)),
                  pl.BlockSpec((None, 1, D), lambda i: (i // per_b, 0, 0)),
                  pl.BlockSpec((1, D), lambda i: (0, 0)),
                  pl.BlockSpec(memory_space=pl.ANY),
                  pl.BlockSpec(memory_space=pl.ANY)],
        out_specs=pl.BlockSpec((2 * tm, D), lambda i: (i, 0)),
        out_shape=jax.ShapeDtypeStruct((N, D), F32),
        scratch_shapes=[pltpu.VMEM((TOP_K, tm // 8, D // LANES, 8, LANES), F32),
                        pltpu.VMEM((TOP_K, tm // 8, D // LANES, 8, LANES), F32),
                        pltpu.SMEM((tm * TOP_K,), I32), pltpu.SMEM((tm * TOP_K,), I32),
                        pltpu.SemaphoreType.DMA, pltpu.SemaphoreType.DMA, pltpu.SemaphoreType.DMA,
                        pltpu.SemaphoreType.DMA],
        compiler_params=_cparams(("arbitrary",)),
        name="comb",
    )(x1, gate_tab, g2, final_g.reshape(1, D), dest_flat, ys)


def _tile(n, pref):
    t = min(pref, n)
    assert n % t == 0, (n, t)
    return t


def _layer(x, c, ada_w, ada_b, norm1_g, w_in, mix_scale, w_o, norm2_g,
           router_w, router_b, w_gu, b_gu, w_down, b_down, final_g):
    B, S, D = x.shape
    N = B * S
    mod = _mod(c, ada_w, ada_b).reshape(B, 6, 1, D)
    sh1, sc1, g1, sh2, sc2, g2 = (mod[:, j] for j in range(6))

    w_pad = jnp.pad(w_in, ((0, 0), (0, IN_COLS_PAD - IN_COLS))).astype(BF16)
    rq, rk, rv, rg, aqt, ak, avt, iqt, ik, iwt = _inproj(x, norm1_g, sc1, sh1, w_pad, _tile(S, PROJ_ROWS))
    ms = mix_scale.reshape(1, RET_W + DSA_W)
    ret = _retention(rq, rk, rv, rg, ms[:, :RET_W])
    att = _dsa(iqt, iwt, aqt, ik, ak, avt, ms[:, RET_W:], _tile(S, DSA_QUERIES), DSA_KEYS)

    rw_pad = jnp.pad(router_w, ((0, 0), (0, LANES - N_EXPERTS)))
    rw_hi = rw_pad.astype(BF16)
    rw_pad = jnp.stack([rw_hi, (rw_pad - rw_hi.astype(F32)).astype(BF16)])
    rb_pad = jnp.pad(router_b, (0, LANES - N_EXPERTS)).reshape(1, LANES)
    x1, h2, sel, idx_tab, gate_tab, counts = _oproj(ret, att, x, w_o.astype(BF16), g1, norm2_g, sc2, sh2,
                                                    rw_pad, rb_pad, _tile(S, PROJ_ROWS))

    tmb = FFN_ROWS
    n_rows = (N * TOP_K + N_EXPERTS * (tmb - 1)) // tmb * tmb + tmb
    cnt = counts[0, :N_EXPERTS].astype(I32)
    padded = (cnt + tmb - 1) // tmb * tmb
    ends = jnp.cumsum(padded)
    starts = ends - padded
    pstart = jnp.pad(starts.astype(F32), (0, LANES - N_EXPERTS)).reshape(1, LANES)
    n_blocks = (ends[-1] // tmb).reshape(1)
    first_row = jnp.arange(n_rows // tmb, dtype=I32) * tmb
    block_e = jnp.minimum(jnp.sum((ends[None, :] <= first_row[:, None]).astype(I32), axis=1), N_EXPERTS - 1)

    tmd = _tile(N, MOE_TOKENS)
    dest_tab = _dest(sel, idx_tab, pstart, tmd)
    dest_flat = dest_tab[:, :TOP_K].reshape(N * TOP_K)
    xs = _dispatch(h2, dest_flat, starts + cnt, n_rows, tmb, tmd)
    eid = jnp.arange(N_EXPERTS, dtype=I32)
    later_used = (eid[None, :] > eid[:, None]) & (padded[None, :] > 0)
    next_used = jnp.min(jnp.where(later_used, eid[None, :], N_EXPERTS), axis=1)
    next_e = jnp.where(next_used < N_EXPERTS, next_used, -1)[block_e].astype(I32)
    ys = _ffn(xs, block_e, n_blocks, next_e, w_gu, b_gu, w_down, b_down, tmb)
    out = _combine(x1, gate_tab, g2, final_g, dest_flat, ys, S, _tile(S, COMB_TOKENS))
    return out.reshape(B, S, D)


def kernel(x, c, ada_w, ada_b, norm1_g, w_in, mix_scale, w_o, norm2_g, router_w, router_b, w_gu, b_gu,
           w_down, b_down, final_g):
    assert ada_w.shape[0] == 1, "single-layer stack"
    return _layer(x, c, ada_w[0], ada_b[0], norm1_g[0], w_in[0], mix_scale[0], w_o[0], norm2_g[0],
                  router_w[0], router_b[0], w_gu[0], b_gu[0], w_down[0], b_down[0], final_g)
```

```python
import functools

import numpy as np
import jax
import jax.numpy as jnp
from jax import lax
from jax.experimental import pallas as pl
from jax.experimental.pallas import tpu as pltpu

F32 = jnp.float32
BF16 = jnp.bfloat16
I32 = jnp.int32

D_MODEL = 1024
RET_HEADS = 4
RET_DK = 64
RET_DV = 128
RET_CHUNK = 128
DSA_HEADS = 8
DSA_KV_HEADS = 2
DSA_HD = 64
IDX_HEADS = 8
IDX_HD = 64
TOPK_MAX = 256
N_EXPERTS = 32
TOP_K = 4
D_EXPERT = D_MODEL
SWIGLU_LIMIT = 7.0
SWIGLU_ALPHA = 1.702
EPS = 1e-6

RET_W = RET_HEADS * RET_DV
DSA_W = DSA_HEADS * DSA_HD
IN_COLS = 2888
IN_COLS_PAD = 2944

KAUG = 128
VAUG = 80
ALIBI_SPLIT = 64
ALIBI_TERMS = 3
LOG2E = 1.4426950408889634
PROJ_ROWS = 512
DSA_QUERIES = 512
DSA_KEYS = 128
MOE_TOKENS = 512
COMB_TOKENS = 256
RET_CHUNKS_PER_STEP = 8
FFN_ROWS = 512
OPROJ_PARTS = 2
FIX_UNROLL = 4
BISECT_COARSE_STEPS = 8
BISECT_MAX_STEPS = 40

LANES = 128
NCH = D_MODEL // LANES
VMEM_LIMIT = 56 * 1024 * 1024
NEG_BIG = -1e30
F32_LOWEST = float(np.finfo(np.float32).min)


def _cparams(sem):
    return pltpu.CompilerParams(dimension_semantics=sem, vmem_limit_bytes=VMEM_LIMIT)


def _mod_kernel(c_ref, w_ref, b_ref, o_ref):
    c = c_ref[...]
    s = c * (1.0 / (1.0 + jnp.exp(-c)))
    o_ref[...] = jnp.dot(s, w_ref[...], preferred_element_type=F32,
                         precision=lax.Precision.HIGHEST) + b_ref[...]


def _mod(c, ada_w, ada_b):
    B, D = c.shape
    n_out = ada_w.shape[1]
    rows = 8
    c8 = jnp.zeros((rows, D), F32).at[:B].set(c)
    out = pl.pallas_call(
        _mod_kernel,
        grid=(n_out // D,),
        in_specs=[pl.BlockSpec((rows, D), lambda j: (0, 0)),
                  pl.BlockSpec((D, D), lambda j: (0, j)),
                  pl.BlockSpec((1, D), lambda j: (0, j))],
        out_specs=pl.BlockSpec((rows, D), lambda j: (0, j)),
        out_shape=jax.ShapeDtypeStruct((rows, n_out), F32),
        compiler_params=_cparams(("arbitrary",)),
        name="mod",
    )(c8, ada_w, ada_b.reshape(1, n_out))
    return out[:B]


def _inproj_kernel(x_ref, g_ref, sc_ref, sh_ref, w_ref,
                   rq_ref, rk_ref, rv_ref, rg_ref, aqt_ref, ak_ref, avt_ref, iqt_ref, ik_ref, iwt_ref):
    x = x_ref[...]
    ms = jnp.mean(x * x, axis=-1, keepdims=True)
    y = x * lax.rsqrt(ms + EPS) * g_ref[...]
    hb = (y * (1.0 + sc_ref[...]) + sh_ref[...]).astype(BF16)

    def proj(lo, hi):
        return jnp.dot(hb, w_ref[:, lo:hi], preferred_element_type=F32)

    tm = x.shape[0]
    d = DSA_HD
    rq_ref[...] = proj(0, 256).astype(BF16)
    rk_ref[...] = (proj(256, 512) * (RET_DK ** -0.5)).astype(BF16)
    rv_ref[...] = proj(512, 1024).astype(BF16)
    rg_ref[...] = proj(1024, 1536).astype(BF16)
    aqt_ref[...] = (proj(1536, 2048) * (d ** -0.5 * LOG2E)).T.astype(BF16)
    kk = proj(2048, 2176)
    pos = pl.program_id(1) * tm + lax.broadcasted_iota(I32, (tm, d), 0)
    col = lax.broadcasted_iota(I32, (tm, d), 1)
    posblk = jnp.where(col < ALIBI_TERMS, pos // ALIBI_SPLIT,
                       jnp.where(col < 2 * ALIBI_TERMS, pos % ALIBI_SPLIT, 0)).astype(F32)
    for g in range(DSA_KV_HEADS):
        ak_ref[:, g * KAUG:g * KAUG + d] = kk[:, g * d:(g + 1) * d].astype(BF16)
        ak_ref[:, g * KAUG + d:(g + 1) * KAUG] = posblk.astype(BF16)
    vt = proj(2176, 2304).T
    r16 = lax.broadcasted_iota(I32, (VAUG - d, tm), 0)
    onesblk = jnp.where(r16 == 0, 1.0, 0.0).astype(BF16)
    for g in range(DSA_KV_HEADS):
        avt_ref[g * VAUG:g * VAUG + d, :] = vt[g * d:(g + 1) * d, :].astype(BF16)
        avt_ref[g * VAUG + d:(g + 1) * VAUG, :] = onesblk
    iqt_ref[...] = proj(2304, 2816).T.astype(BF16)
    last = proj(2816, 2944)
    ik_ref[...] = last[:, :IDX_HD].astype(BF16)
    iwt_ref[...] = last.T[IDX_HD:IDX_HD + IDX_HEADS, :] * ((IDX_HD ** -0.5) * (IDX_HEADS ** -0.5))


def _inproj(x, norm_g, sc, sh, w_pad, tm):
    B, S, D = x.shape
    row = lambda w: pl.BlockSpec((None, tm, w), lambda b, i: (b, i, 0))
    colT = lambda h: pl.BlockSpec((None, h, tm), lambda b, i: (b, 0, i))
    vec = pl.BlockSpec((None, 1, D), lambda b, i: (b, 0, 0))
    sd = lambda shape, dt: jax.ShapeDtypeStruct(shape, dt)
    G = DSA_KV_HEADS
    return pl.pallas_call(
        _inproj_kernel,
        grid=(B, S // tm),
        in_specs=[row(D), pl.BlockSpec((1, D), lambda b, i: (0, 0)), vec, vec,
                  pl.BlockSpec((D, IN_COLS_PAD), lambda b, i: (0, 0))],
        out_specs=[row(256), row(256), row(512), row(512), colT(DSA_W), row(G * KAUG), colT(G * VAUG),
                   colT(IDX_HEADS * IDX_HD), row(IDX_HD), colT(IDX_HEADS)],
        out_shape=[sd((B, S, 256), BF16), sd((B, S, 256), BF16), sd((B, S, 512), BF16),
                   sd((B, S, 512), BF16), sd((B, DSA_W, S), BF16), sd((B, S, G * KAUG), BF16),
                   sd((B, G * VAUG, S), BF16), sd((B, IDX_HEADS * IDX_HD, S), BF16),
                   sd((B, S, IDX_HD), BF16), sd((B, IDX_HEADS, S), F32)],
        compiler_params=_cparams(("parallel", "parallel")),
        name="inproj",
    )(x, norm_g.reshape(1, D), sc, sh, w_pad)


def _ret_kernel(rq_ref, rk_ref, rv_ref, rg_ref, din_ref, qd_ref, kd_ref, cd_ref, ms_ref, o_ref, state_ref):
    @pl.when(pl.program_id(1) == 0)
    def _():
        state_ref[...] = jnp.zeros_like(state_ref)

    C = din_ref.shape[1]
    chunks = [slice(c * C, (c + 1) * C) for c in range(rq_ref.shape[0] // C)]
    for h in range(RET_HEADS):
        qcol = slice(h * RET_DK, (h + 1) * RET_DK)
        vcol = slice(h * RET_DV, (h + 1) * RET_DV)
        local, kvs = [], []
        for rows in chunks:
            q, k, v = rq_ref[rows, qcol], rk_ref[rows, qcol], rv_ref[rows, vcol]
            s = lax.dot_general(q, k, (((1,), (1,)), ((), ())), preferred_element_type=F32) * din_ref[h]
            local.append(s.astype(BF16))
            vd = (v.astype(F32) * kd_ref[h]).astype(BF16)
            kvs.append(lax.dot_general(k, vd, (((0,), (0,)), ((), ())), preferred_element_type=F32))
        states, r = [], state_ref[h]
        for kv in kvs:
            states.append(r.astype(BF16))
            r = r * cd_ref[h] + kv
        state_ref[h] = r
        for rows, s, r_prev in zip(chunks, local, states):
            q, v = rq_ref[rows, qcol], rv_ref[rows, vcol]
            o = jnp.dot(s, v, preferred_element_type=F32)
            o = o + jnp.dot(q, r_prev, preferred_element_type=F32) * qd_ref[h]
            o = o * lax.rsqrt(jnp.mean(o * o, axis=-1, keepdims=True) + EPS)
            g = rg_ref[rows, vcol].astype(F32)
            gate = g * (1.0 / (1.0 + jnp.exp(-g)))
            o_ref[rows, vcol] = (gate * o * ms_ref[:, vcol]).astype(BF16)


def _ret_consts(C):
    H = RET_HEADS
    log_g = np.log1p(-np.exp2(-5.0 - np.arange(H, dtype=np.float64)))
    pos = np.arange(C, dtype=np.float64)
    diff = pos[:, None] - pos[None, :]
    d_inner = np.where(diff[None] >= 0, np.exp(np.maximum(diff, 0.0)[None] * log_g[:, None, None]), 0.0)
    q_decay = np.exp((pos + 1.0)[None] * log_g[:, None])
    k_decay = np.exp((C - 1.0 - pos)[None] * log_g[:, None])
    chunk_decay = np.exp(C * log_g)
    qd = np.broadcast_to(q_decay[:, :, None], (H, C, RET_DV))
    kd = np.broadcast_to(k_decay[:, :, None], (H, C, RET_DV))
    cd = np.broadcast_to(chunk_decay[:, None, None], (H, 1, RET_DV))
    f = lambda a: jnp.asarray(np.ascontiguousarray(a), F32)
    return f(d_inner), f(qd), f(kd), f(cd)


def _retention(rq, rk, rv, rg, ms_ret):
    B, S, _ = rq.shape
    C = min(RET_CHUNK, S)
    din, qd, kd, cd = _ret_consts(C)
    rows = _tile(S, RET_CHUNKS_PER_STEP * C)
    row = lambda w: pl.BlockSpec((None, rows, w), lambda b, n: (b, n, 0))
    full = lambda a: pl.BlockSpec(a.shape, lambda b, n: (0,) * a.ndim)
    return pl.pallas_call(
        _ret_kernel,
        grid=(B, S // rows),
        in_specs=[row(256), row(256), row(512), row(512), full(din), full(qd), full(kd), full(cd),
                  pl.BlockSpec((1, RET_W), lambda b, n: (0, 0))],
        out_specs=row(RET_W),
        out_shape=jax.ShapeDtypeStruct((B, S, RET_W), BF16),
        scratch_shapes=[pltpu.VMEM((RET_HEADS, RET_DK, RET_DV), F32)],
        compiler_params=_cparams(("parallel", "arbitrary")),
        name="ret",
    )(rq, rk, rv, rg, din, qd, kd, cd, ms_ret)


def _f32_key(x):
    i = lax.bitcast_convert_type(x, I32)
    return i ^ ((i >> 31) & 0x7FFFFFFF)


def _key_f32(k):
    return lax.bitcast_convert_type(k ^ ((k >> 31) & 0x7FFFFFFF), F32)


def _floor16(x):
    i = lax.bitcast_convert_type(x, I32)
    return lax.bitcast_convert_type((i + ((i >> 31) & 0xFFFF)) & -0x10000, F32)


def _trunc16(x):
    return lax.bitcast_convert_type(lax.bitcast_convert_type(x, I32) & -0x10000, F32)


def _dsa_kernel(iqt_ref, iwt_ref, aqt_ref, ik_ref, ak_ref, avt_ref, ms_ref, o_ref, score_ref, code_ref, qa_ref, sa_ref, sb_ref, mask_ref, *acc_refs,
                tq, tks, n_sel):
    H, G, R, d = DSA_HEADS, DSA_KV_HEADS, DSA_HEADS // DSA_KV_HEADS, DSA_HD
    t0 = pl.program_id(1) * tq
    nsub = (t0 + tq) // tks
    tka = 2 * tks
    npair = ((t0 + tq) // tka + 1) // 2
    kf = float(n_sel)
    qpos = t0 + lax.broadcasted_iota(I32, (1, tq), 1)
    krow = lax.broadcasted_iota(I32, (tks, tq), 0)

    wrow = [iwt_ref[h:h + 1, :] for h in range(IDX_HEADS)]

    def fold8(x, op):
        acc = x[0:8, :]
        for i in range(1, tks // 8):
            acc = op(acc, x[8 * i:8 * (i + 1), :])
        return acc

    def score_pair(i, carry, masked):
        mx, mn, npos, nnon = carry
        for u in range(2 * tka // tks):
            r0 = pl.multiple_of(i * 2 * tka + u * tks, tks)
            kc = ik_ref[pl.ds(r0, tks), :]
            acc = jnp.zeros((tks, tq), F32)
            for h in range(IDX_HEADS):
                rel = jnp.dot(kc, iqt_ref[h * IDX_HD:(h + 1) * IDX_HD, :], preferred_element_type=F32)
                acc = acc + jnp.maximum(rel, 0.0) * wrow[h]
            if masked:
                causal = r0 + krow <= qpos
                sc = jnp.where(causal, acc, -jnp.inf)
                lowest = jnp.where(causal, acc, jnp.inf)
            else:
                sc = lowest = acc
            score_ref[pl.ds(r0, tks), :] = sc
            code_ref[pl.ds(r0, tks), :] = _trunc16(sc).astype(BF16)
            mx = jnp.maximum(mx, fold8(sc, jnp.maximum))
            mn = jnp.minimum(mn, fold8(lowest, jnp.minimum))
            npos = npos + fold8(jnp.where(sc > 0.0, 1.0, 0.0), jnp.add)
            nnon = nnon + fold8(jnp.where(sc >= 0.0, 1.0, 0.0), jnp.add)
        return mx, mn, npos, nnon

    stat0 = (jnp.full((8, tq), -jnp.inf, F32), jnp.full((8, tq), jnp.inf, F32),
             jnp.zeros((8, tq), F32), jnp.zeros((8, tq), F32))
    n_inner = (t0 + 1) // (2 * tka)
    stat = lax.fori_loop(0, n_inner, functools.partial(score_pair, masked=False), stat0)
    mx, mn, npos, nnon = lax.fori_loop(n_inner, npair, functools.partial(score_pair, masked=True), stat)
    top = jnp.max(mx, axis=0, keepdims=True)
    lo0 = jnp.min(mn, axis=0, keepdims=True)
    n_pos = jnp.sum(npos, axis=0, keepdims=True)
    n_nonneg = jnp.sum(nnon, axis=0, keepdims=True)


    def count(th, strict):
        def body(j, acc):
            for u in range(2):
                s = score_ref[pl.ds(pl.multiple_of((2 * j + u) * tks, tks), tks), :]
                hit = (s > th) if strict else (s >= th)
                acc = acc + fold8(jnp.where(hit, 1.0, 0.0), jnp.add)
            return acc

        acc = lax.fori_loop(0, nsub // 2, body, jnp.zeros((8, tq), F32))
        return jnp.sum(acc, axis=0, keepdims=True)

    def probe(lo, hi):
        lk, hk = _f32_key(lo), _f32_key(hi)
        mk = (lk >> 1) + (hk >> 1) + (lk & hk & 1)
        return _key_f32(mk), jnp.max(jnp.where(mk != lk, 1.0, 0.0))

    def bis_cond(c):
        return (c[5] > 0.0) & (c[6] < BISECT_MAX_STEPS)

    def bis_body(c):
        lo, hi, c_lo, c_hi, mid, _, it = c
        cnt = count(mid, False)
        ge = cnt >= kf
        up = ge | (cnt == kf)
        dn = (~ge) | (cnt == kf)
        lo, c_lo = jnp.where(up, mid, lo), jnp.where(up, cnt, c_lo)
        hi, c_hi = jnp.where(dn, mid, hi), jnp.where(dn, cnt, c_hi)
        mid, active = probe(lo, hi)
        return lo, hi, c_lo, c_hi, mid, active, it + 1

    zero = jnp.zeros((1, tq), F32)
    keep_all = qpos + 1 <= n_sel
    settled = keep_all | ((n_nonneg >= kf) & (n_pos < kf))
    above = n_pos >= kf
    c_lo0 = jnp.where(settled | above, n_nonneg, (qpos + 1).astype(F32))
    c_hi0 = jnp.where(settled | ~above, n_nonneg, zero)
    lo0 = jnp.where(settled | above, zero, lo0)
    hi0 = jnp.where(settled | ~above, zero, _key_f32(_f32_key(top) + 1))
    def count16(m):
        mb = m.astype(BF16)

        def body(j, acc):
            for u in range(2):
                c = code_ref[pl.ds(pl.multiple_of((2 * j + u) * tks, tks), tks), :]
                one = jnp.where(c >= mb, jnp.ones((), BF16), jnp.zeros((), BF16))
                part = one[0:16, :]
                for i in range(1, tks // 16):
                    part = part + one[16 * i:16 * (i + 1), :]
                acc = acc + part.astype(F32)
            return acc

        acc = lax.fori_loop(0, nsub // 2, body, jnp.zeros((16, tq), F32))
        return jnp.sum(acc, axis=0, keepdims=True)

    def coarse_step(_, c):
        lo, hi, c_lo, c_hi = c
        below = _floor16(lo + (hi - lo) * 0.5)
        above = _key_f32(_f32_key(below) + 0x10000)
        m = jnp.where(below > lo, below, above)
        v = jnp.where(m > 0.0, m, _key_f32(_f32_key(m) - 0x10000 + 1))
        ok = (v > lo) & (v < hi)
        cnt = count16(m)
        ge = cnt >= kf
        up = ok & ge
        dn = ok & ((~ge) | (cnt == kf))
        return (jnp.where(up, v, lo), jnp.where(dn, v, hi), jnp.where(up, cnt, c_lo), jnp.where(dn, cnt, c_hi))

    lo0, hi0, c_lo0, c_hi0 = lax.fori_loop(0, BISECT_COARSE_STEPS, coarse_step, (lo0, hi0, c_lo0, c_hi0))
    mid0, active0 = probe(lo0, hi0)
    lo, hi, c_lo, c_hi, _, _, _ = lax.while_loop(bis_cond, bis_body,
                                                 (lo0, hi0, c_lo0, c_hi0, mid0, active0, jnp.int32(0)))
    at_hi = c_hi >= kf
    thr = jnp.where(keep_all, F32_LOWEST, jnp.where(at_hi, hi, lo))
    excess = jnp.where(keep_all, 0.0, jnp.where(at_hi, c_hi, c_lo) - kf)

    @pl.when(jnp.max(excess) > 0.0)
    def _():
        tied_nonzero = jnp.max(jnp.where((excess > 0.0) & (thr != 0.0), 1.0, 0.0)) > 0.0
        n_above = lax.cond(tied_nonzero, lambda: count(thr, True), lambda: n_pos)
        budget = jnp.where(excess > 0.0, kf - jnp.where(thr == 0.0, n_pos, n_above), jnp.inf)
        earlier = lax.broadcasted_iota(I32, (tks, tks), 1) < lax.broadcasted_iota(I32, (tks, tks), 0)
        earlier = jnp.where(earlier, 1.0, 0.0).astype(BF16)

        def fix(j, seen):
            tiles = []
            for u in range(FIX_UNROLL):
                r0 = pl.multiple_of((FIX_UNROLL * j + u) * tks, tks)
                s = score_ref[pl.ds(r0, tks), :]
                eq = s == thr
                eqf = jnp.where(eq, 1.0, 0.0)
                within = jnp.dot(earlier, eqf.astype(BF16), preferred_element_type=F32)
                tiles.append((r0, s, eq, within, jnp.sum(eqf, axis=0, keepdims=True)))
            for r0, s, eq, within, n_eq in tiles:
                score_ref[pl.ds(r0, tks), :] = jnp.where(eq & (within + seen >= budget), -jnp.inf, s)
                seen = seen + n_eq
            return seen

        lax.fori_loop(0, nsub // FIX_UNROLL, fix, jnp.zeros((1, tq), F32))

    arow = lax.broadcasted_iota(I32, (KAUG - d, tq), 0)
    for h in range(H):
        terms, rest = [], 2.0 ** (-8.0 * (h + 1) / H) * LOG2E
        for _ in range(ALIBI_TERMS):
            terms.append(float(np.asarray(rest, np.float32).astype(BF16).astype(np.float64)))
            rest -= terms[-1]
        rows = jnp.zeros((KAUG - d, tq), F32)
        for i, c in enumerate(terms):
            rows = jnp.where(arow == i, c * ALIBI_SPLIT, jnp.where(arow == ALIBI_TERMS + i, c, rows))
        qa_ref[h, 0:d, :] = aqt_ref[h * d:(h + 1) * d, :]
        qa_ref[h, d:KAUG, :] = rows.astype(BF16)
    for acc in acc_refs:
        acc[...] = jnp.zeros_like(acc)

    def logits(j, h):
        ka = ak_ref[pl.ds(pl.multiple_of(j * tka, tka), tka), (h // R) * KAUG:(h // R + 1) * KAUG]
        return jnp.dot(ka, qa_ref[h], preferred_element_type=F32)

    def step(j, j_next, cur_ref, next_ref, ms):
        r0 = pl.multiple_of(j * tka, tka)
        mask_ref[...] = jnp.where(score_ref[pl.ds(r0, tka), :] >= thr, 0.0, NEG_BIG)
        nms = []
        ahead = logits(j_next, 0)
        for h in range(H):
            g = h // R
            next_ref[h] = ahead
            if h + 1 < H:
                ahead = logits(j_next, h + 1)
            s = cur_ref[h] + mask_ref[...]
            m_new = jnp.maximum(ms[h], jnp.max(s, axis=0, keepdims=True))
            p = jnp.exp2(s - m_new).astype(BF16)
            va = avt_ref[g * VAUG:(g + 1) * VAUG, pl.ds(r0, tka)]
            acc = acc_refs[h]
            acc[...] = acc[...] * jnp.exp2(ms[h] - m_new) + jnp.dot(va, p, preferred_element_type=F32)
            nms.append(m_new)
        return tuple(nms)

    for h in range(H):
        sa_ref[h] = logits(0, h)

    def att_pair(i, ms):
        ms = step(2 * i, 2 * i + 1, sa_ref, sb_ref, ms)
        return step(2 * i + 1, jnp.minimum(2 * i + 2, 2 * npair - 1), sb_ref, sa_ref, ms)

    lax.fori_loop(0, npair, att_pair, tuple(jnp.full((1, tq), NEG_BIG, F32) for _ in range(H)))
    for h in range(H):
        a = acc_refs[h][...]
        o = a[0:d, :] / a[d:d + 1, :]
        o_ref[h * d:(h + 1) * d, :] = (o * ms_ref[h * d:(h + 1) * d, :]).astype(BF16)


def _dsa(iqt, iwt, aqt, ik, ak, avt, ms_att, tq, tks):
    B, _, S = iqt.shape
    assert S <= ALIBI_SPLIT * 256 and tq % (2 * tks) == 0 and tq % (FIX_UNROLL * tks) == 0
    assert (S // (2 * tks)) % 2 == 0
    n_sel = min(TOPK_MAX, S // 4)
    G = DSA_KV_HEADS
    colT = lambda h: pl.BlockSpec((None, h, tq), lambda b, i: (b, 0, i))
    msb = jnp.broadcast_to(ms_att.reshape(DSA_W, 1), (DSA_W, tq))
    return pl.pallas_call(
        functools.partial(_dsa_kernel, tq=tq, tks=tks, n_sel=n_sel),
        grid=(B, S // tq),
        in_specs=[colT(IDX_HEADS * IDX_HD), colT(IDX_HEADS), colT(DSA_W),
                  pl.BlockSpec((None, S, IDX_HD), lambda b, i: (b, 0, 0), pipeline_mode=pl.Buffered(1)),
                  pl.BlockSpec((None, S, G * KAUG), lambda b, i: (b, 0, 0), pipeline_mode=pl.Buffered(1)),
                  pl.BlockSpec((None, G * VAUG, S), lambda b, i: (b, 0, 0), pipeline_mode=pl.Buffered(1)),
                  pl.BlockSpec((DSA_W, tq), lambda b, i: (0, 0))],
        out_specs=colT(DSA_W),
        out_shape=jax.ShapeDtypeStruct((B, DSA_W, S), BF16),
        scratch_shapes=[pltpu.VMEM((S, tq), F32), pltpu.VMEM((S, tq), BF16), pltpu.VMEM((DSA_HEADS, KAUG, tq), BF16),
                        pltpu.VMEM((DSA_HEADS, 2 * tks, tq), F32), pltpu.VMEM((DSA_HEADS, 2 * tks, tq), F32),
                        pltpu.VMEM((2 * tks, tq), F32)]
        + [pltpu.VMEM((VAUG, tq), F32) for _ in range(DSA_HEADS)],
        compiler_params=_cparams(("parallel", "arbitrary")),
        name="dsa",
    )(iqt, iwt, aqt, ik, ak, avt, msb)


def _oproj_kernel(ret_ref, att_ref, x_ref, wo_ref, g1_ref, n2_ref, sc_ref, sh_ref, rw_ref, rb_ref,
                  x1_ref, h2_ref, sel_ref, idx_ref, gate_ref, cnt_ref):
    tm = x_ref.shape[0]
    parts = [slice(r, r + tm // OPROJ_PARTS) for r in range(0, tm, tm // OPROJ_PARTS)]
    nch = x_ref.shape[1] // LANES

    def mix(rows):
        out = jnp.dot(ret_ref[rows, :], wo_ref[:RET_W, :], preferred_element_type=F32)
        return out + lax.dot_general(att_ref[:, rows], wo_ref[RET_W:, :], (((0,), (0,)), ((), ())),
                                     preferred_element_type=F32)

    def hidden(rows, mixo):
        x1 = x_ref[rows, :] + g1_ref[...] * mixo
        x1_ref[rows, :] = x1
        y = x1 * lax.rsqrt(jnp.mean(x1 * x1, axis=-1, keepdims=True) + EPS) * n2_ref[...]
        h2 = y * (1.0 + sc_ref[...]) + sh_ref[...]
        for j in range(nch):
            h2_ref[pl.ds(rows.start * nch + j, rows.stop - rows.start, stride=nch), :] = h2[:, j * LANES:(j + 1) * LANES]
        return h2

    def router(h2):
        h_hi = h2.astype(BF16)
        h_lo = (h2 - h_hi.astype(F32)).astype(BF16)
        logits = jnp.dot(h_hi, rw_ref[0], preferred_element_type=F32)
        return logits + (jnp.dot(h_hi, rw_ref[1], preferred_element_type=F32)
                         + jnp.dot(h_lo, rw_ref[0], preferred_element_type=F32)) + rb_ref[...]

    def top4(rows, logits):
        n = logits.shape[0]
        lane = lax.broadcasted_iota(I32, (n, LANES), 1).astype(F32)
        work = jnp.where(lane < N_EXPERTS, logits, -jnp.inf)
        sel = jnp.zeros((n, LANES), F32)
        idx_tab = jnp.zeros((n, LANES), F32)
        vals = []
        for k in range(TOP_K):
            m = jnp.max(work, axis=1, keepdims=True)
            idx = jnp.min(jnp.where(work == m, lane, float(LANES)), axis=1, keepdims=True)
            hit = lane == idx
            sel = jnp.where(hit, 1.0, sel)
            idx_tab = jnp.where(lane == k, idx, idx_tab)
            work = jnp.where(hit, -jnp.inf, work)
            vals.append(m)
        es = [jnp.exp(v - vals[0]) for v in vals]
        den = es[0] + es[1] + es[2] + es[3]
        gate_tab = jnp.zeros((n, LANES), F32)
        for k in range(TOP_K):
            gate_tab = jnp.where(lane == k, es[k] / den, gate_tab)
        sel_ref[rows, :] = sel
        idx_ref[rows, :] = idx_tab
        gate_ref[rows, :] = gate_tab
        return jnp.sum(sel, axis=0, keepdims=True)

    mixes = [mix(rows) for rows in parts]
    hiddens = [hidden(rows, m) for rows, m in zip(parts, mixes)]
    logit_parts = [router(h2) for h2 in hiddens]
    counts = [top4(rows, lg) for rows, lg in zip(parts, logit_parts)]

    @pl.when((pl.program_id(0) == 0) & (pl.program_id(1) == 0))
    def _():
        cnt_ref[...] = jnp.zeros_like(cnt_ref)

    cnt_ref[...] += functools.reduce(lambda a, b: a + b, counts)


def _oproj(ret, att, x, wo, g1, n2g, sc2, sh2, rw_pad, rb_pad, tm):
    B, S, D = x.shape
    nt = S // tm
    row = lambda w: pl.BlockSpec((None, tm, w), lambda b, i: (b, i, 0))
    flat = lambda w: pl.BlockSpec((tm, w), lambda b, i: (b * nt + i, 0))
    vec = pl.BlockSpec((None, 1, D), lambda b, i: (b, 0, 0))
    cst = lambda shape: pl.BlockSpec(shape, lambda b, i: (0, 0))
    sd = lambda shape, dt: jax.ShapeDtypeStruct(shape, dt)
    N = B * S
    return pl.pallas_call(
        _oproj_kernel,
        grid=(B, nt),
        in_specs=[row(RET_W), pl.BlockSpec((None, DSA_W, tm), lambda b, i: (b, 0, i)), row(D), cst((D, D)), vec,
                  cst((1, D)), vec, vec,
                  pl.BlockSpec((2, D, LANES), lambda b, i: (0, 0, 0)), cst((1, LANES))],
        out_specs=[flat(D), pl.BlockSpec((tm * (D // LANES), LANES), lambda b, i: (b * nt + i, 0)),
                   flat(LANES), flat(LANES), flat(LANES), cst((1, LANES))],
        out_shape=[sd((N, D), F32), sd((N * (D // LANES), LANES), F32), sd((N, LANES), F32), sd((N, LANES), F32),
                   sd((N, LANES), F32), sd((1, LANES), F32)],
        compiler_params=_cparams(("arbitrary", "arbitrary")),
        name="oproj",
    )(ret, att, x, wo, g1, n2g.reshape(1, D), sc2, sh2, rw_pad, rb_pad)


def _dest_kernel(sel_ref, idx_ref, pstart_ref, dest_ref, seen_ref):
    @pl.when(pl.program_id(0) == 0)
    def _():
        seen_ref[...] = jnp.zeros_like(seen_ref)

    sel = sel_ref[...]
    tm = sel.shape[0]
    earlier = lax.broadcasted_iota(I32, (tm, tm), 1) < lax.broadcasted_iota(I32, (tm, tm), 0)
    earlier = jnp.where(earlier, 1.0, 0.0).astype(BF16)
    rank = jnp.dot(earlier, sel.astype(BF16), preferred_element_type=F32) + seen_ref[...]
    dest = pstart_ref[...] + rank
    lane = lax.broadcasted_iota(I32, (tm, LANES), 1).astype(F32)
    idx_tab = idx_ref[...]
    out = jnp.zeros((tm, LANES), F32)
    for k in range(TOP_K):
        e_k = jnp.sum(jnp.where(lane == k, idx_tab, 0.0), axis=1, keepdims=True)
        d_k = jnp.sum(jnp.where(lane == e_k, dest, 0.0), axis=1, keepdims=True)
        out = jnp.where(lane == k, d_k, out)
    dest_ref[...] = out.astype(I32)
    seen_ref[...] += jnp.sum(sel, axis=0, keepdims=True)


def _dest(sel, idx_tab, pstart, tm):
    N = sel.shape[0]
    blk = pl.BlockSpec((tm, LANES), lambda i: (i, 0))
    return pl.pallas_call(
        _dest_kernel,
        grid=(N // tm,),
        in_specs=[blk, blk, pl.BlockSpec((1, LANES), lambda i: (0, 0))],
        out_specs=blk,
        out_shape=jax.ShapeDtypeStruct((N, LANES), I32),
        scratch_shapes=[pltpu.VMEM((1, LANES), F32)],
        compiler_params=_cparams(("arbitrary",)),
        name="dest",
    )(sel, idx_tab, pstart)


def _disp_kernel(zs_ref, h2_ref, dest_hbm, xs_hbm, zbuf, idx_a, idx_b, sem_ia, sem_ib, sem_row, sem_z,
                 *, tm, nch, zrows, n_steps):
    i = pl.program_id(0)
    n = tm * TOP_K

    def idx_copy(tile, buf, sem):
        return pltpu.make_async_copy(dest_hbm.at[pl.ds(tile * n, n)], buf, sem)

    def rows(ref, first, count):
        return ref.at[pl.ds(pl.multiple_of(first * nch, nch), count * nch)]

    def row_copy(row, dst):
        return pltpu.make_async_copy(rows(h2_ref, row, 1), rows(xs_hbm, dst, 1), sem_row)

    def issue_rows(first_row, idx):
        def body(r, c):
            for s in range(8):
                t = r * 8 + s
                for k in range(TOP_K):
                    row_copy(first_row + t, idx[t * TOP_K + k]).start(priority=k % 2)
            return c

        lax.fori_loop(0, tm // 8, body, 0)

    def wait_rows():
        pltpu.make_async_copy(rows(xs_hbm, 0, n), rows(xs_hbm, 0, n), sem_row).wait()

    @pl.when(i == 0)
    def _():
        zbuf[...] = jnp.zeros_like(zbuf)
        for e in range(N_EXPERTS):
            fill = pltpu.make_async_copy(zbuf, rows(xs_hbm, zs_ref[e], zrows), sem_z)
            fill.start()
            fill.wait()
        idx_copy(0, idx_a, sem_ia).start()
        idx_copy(1, idx_b, sem_ib).start()

    idx_copy(2 * i, idx_a, sem_ia).wait()
    issue_rows(0, idx_a)
    idx_copy(2 * i + 1, idx_b, sem_ib).wait()
    issue_rows(tm, idx_b)

    @pl.when(i + 1 < n_steps)
    def _():
        idx_copy(2 * i + 2, idx_a, sem_ia).start()
        idx_copy(2 * i + 3, idx_b, sem_ib).start()

    wait_rows()
    wait_rows()


def _dispatch(h2, dest_flat, zero_start, n_rows, zrows, tm):
    nch = D_MODEL // LANES
    n_steps = h2.shape[0] // nch // (2 * tm)
    grid_spec = pltpu.PrefetchScalarGridSpec(
        num_scalar_prefetch=1,
        grid=(n_steps,),
        in_specs=[pl.BlockSpec((2 * tm * nch, LANES), lambda i, zs: (i, 0)), pl.BlockSpec(memory_space=pl.ANY)],
        out_specs=pl.BlockSpec(memory_space=pl.ANY),
        scratch_shapes=[pltpu.VMEM((zrows * nch, LANES), F32), pltpu.SMEM((tm * TOP_K,), I32),
                        pltpu.SMEM((tm * TOP_K,), I32), pltpu.SemaphoreType.DMA, pltpu.SemaphoreType.DMA,
                        pltpu.SemaphoreType.DMA, pltpu.SemaphoreType.DMA],
    )
    return pl.pallas_call(
        functools.partial(_disp_kernel, tm=tm, nch=nch, zrows=zrows, n_steps=n_steps),
        grid_spec=grid_spec,
        out_shape=jax.ShapeDtypeStruct((n_rows * nch, LANES), F32),
        compiler_params=_cparams(("arbitrary",)),
        name="disp",
    )(zero_start, h2, dest_flat)


def _store_token_major(ref, x):
    rows, d = x.shape
    nch = d // LANES
    for j in range(nch):
        ref[pl.ds(j, rows, stride=nch), :] = x[:, j * LANES:(j + 1) * LANES]


def _load_token_major(ref, rows, nch):
    return jnp.concatenate([ref[pl.ds(j, rows, stride=nch), :] for j in range(nch)], axis=1)


def _ffn_kernel(be_ref, nb_ref, nxt_ref, xs_ref, wgu_hbm, bgu_ref, wd_hbm, bd_ref, ys_ref,
                wgu_st, wd_st, wgu_bf, wd_bf, sem_gu, sem_d):
    i = pl.program_id(0)
    live = i < nb_ref[0]
    e = be_ref[i]

    def fetch(ex):
        return (pltpu.make_async_copy(wgu_hbm.at[ex], wgu_st, sem_gu),
                pltpu.make_async_copy(wd_hbm.at[ex], wd_st, sem_d))

    @pl.when(i == 0)
    def _():
        for cp in fetch(e):
            cp.start()

    @pl.when(live & ((i == 0) | (e != be_ref[jnp.maximum(i - 1, 0)])))
    def _():
        for cp in fetch(e):
            cp.wait()
        wgu_bf[...] = wgu_st[...].astype(BF16)
        wd_bf[...] = wd_st[...].astype(BF16)

        @pl.when(nxt_ref[i] >= 0)
        def _():
            for cp in fetch(nxt_ref[i]):
                cp.start()

    @pl.when(live)
    def _():
        xb = _load_token_major(xs_ref, ys_ref.shape[0] // NCH, NCH).astype(BF16)
        gu = jnp.dot(xb, wgu_bf[...], preferred_element_type=F32) + bgu_ref[...]
        gate = jnp.minimum(gu[:, :D_EXPERT], SWIGLU_LIMIT)
        up = jnp.clip(gu[:, D_EXPERT:], -SWIGLU_LIMIT, SWIGLU_LIMIT)
        glu = gate * (1.0 / (1.0 + jnp.exp(-SWIGLU_ALPHA * gate)))
        act = ((up + 1.0) * glu).astype(BF16)
        ys = jnp.dot(act, wd_bf[...], preferred_element_type=F32) + bd_ref[...]
        _store_token_major(ys_ref, ys)

    @pl.when(jnp.logical_not(live))
    def _():
        ys_ref[...] = jnp.zeros_like(ys_ref)


def _ffn(xs, block_e, n_blocks, next_e, w_gu, b_gu, w_down, b_down, tmb):
    D = D_MODEL
    P = xs.shape[0] // NCH
    E = w_gu.shape[0]
    blk = lambda i, be, nb, nx: (jnp.minimum(i, nb[0] - 1), 0)
    wsel = lambda i, be, nb, nx: (be[jnp.minimum(i, nb[0] - 1)], 0, 0)
    grid_spec = pltpu.PrefetchScalarGridSpec(
        num_scalar_prefetch=3,
        grid=(P // tmb,),
        in_specs=[pl.BlockSpec((tmb * NCH, LANES), blk),
                  pl.BlockSpec(memory_space=pl.ANY),
                  pl.BlockSpec((None, 1, 2 * D_EXPERT), wsel),
                  pl.BlockSpec(memory_space=pl.ANY),
                  pl.BlockSpec((None, 1, D), wsel)],
        out_specs=pl.BlockSpec((tmb * (D // LANES), LANES), lambda i, be, nb, nx: (i, 0)),
        scratch_shapes=[pltpu.VMEM((D, 2 * D_EXPERT), F32), pltpu.VMEM((D_EXPERT, D), F32),
                        pltpu.VMEM((D, 2 * D_EXPERT), BF16), pltpu.VMEM((D_EXPERT, D), BF16),
                        pltpu.SemaphoreType.DMA, pltpu.SemaphoreType.DMA],
    )
    return pl.pallas_call(
        _ffn_kernel,
        grid_spec=grid_spec,
        out_shape=jax.ShapeDtypeStruct((P * (D // LANES), LANES), F32),
        compiler_params=_cparams(("arbitrary",)),
        name="ffn",
    )(block_e, n_blocks, next_e, xs, w_gu, b_gu.reshape(E, 1, 2 * D_EXPERT), w_down, b_down.reshape(E, 1, D))


def _comb_kernel(x1_ref, gate_ref, g2_ref, fg_ref, dest_hbm, ys_hbm, o_ref, buf_a, buf_b, idx_a, idx_b,
                 sem_ia, sem_ib, sem_ra, sem_rb, *, tm, n_steps):
    i = pl.program_id(0)
    n = tm * TOP_K

    def idx_copy(tile, buf, sem):
        return pltpu.make_async_copy(dest_hbm.at[pl.ds(tile * n, n)], buf, sem)

    nch = x1_ref.shape[1] // LANES

    def row_copy(src, buf, k, t, sem):
        return pltpu.make_async_copy(ys_hbm.at[pl.ds(pl.multiple_of(src * nch, nch), nch)],
                                     buf.at[k, t[0], :, t[1]], sem)

    def issue_rows(idx, buf, sem):
        def body(r, c):
            for s in range(8):
                for k in range(TOP_K):
                    row_copy(idx[(r * 8 + s) * TOP_K + k], buf, k, (r, s), sem).start(priority=k % 2)
            return c

        lax.fori_loop(0, tm // 8, body, 0)

    def wait_rows(buf, sem):
        pltpu.make_async_copy(buf, buf, sem).wait()

    def finish(buf, lo):
        gates = gate_ref[lo:lo + tm, :]
        y = jnp.zeros((tm, x1_ref.shape[1]), F32)
        for k in range(TOP_K):
            rows = jnp.concatenate([buf[k, :, j].reshape(tm, LANES) for j in range(nch)], axis=1)
            y = y + gates[:, k:k + 1] * rows
        v = x1_ref[lo:lo + tm, :] + g2_ref[...] * y
        o_ref[lo:lo + tm, :] = v * lax.rsqrt(jnp.mean(v * v, axis=-1, keepdims=True) + EPS) * fg_ref[...]

    @pl.when(i == 0)
    def _():
        first = idx_copy(0, idx_a, sem_ia)
        first.start()
        first.wait()
        issue_rows(idx_a, buf_a, sem_ra)
        idx_copy(1, idx_b, sem_ib).start()

    idx_copy(2 * i + 1, idx_b, sem_ib).wait()
    issue_rows(idx_b, buf_b, sem_rb)

    @pl.when(i + 1 < n_steps)
    def _():
        idx_copy(2 * i + 2, idx_a, sem_ia).start()

    wait_rows(buf_a, sem_ra)
    finish(buf_a, 0)

    @pl.when(i + 1 < n_steps)
    def _():
        idx_copy(2 * i + 2, idx_a, sem_ia).wait()
        issue_rows(idx_a, buf_a, sem_ra)
        idx_copy(2 * i + 3, idx_b, sem_ib).start()

    wait_rows(buf_b, sem_rb)
    finish(buf_b, tm)


def _combine(x1, gate_tab, g2, final_g, dest_flat, ys, S, tm):
    N, D = x1.shape
    n_steps = N // (2 * tm)
    per_b = S // (2 * tm)
    return pl.pallas_call(
        functools.partial(_comb_kernel, tm=tm, n_steps=n_steps),
        grid=(n_steps,),
        in_specs=[pl.BlockSpec((2 * tm, D), lambda i: (i, 0)),
                  pl.BlockSpec((2 * tm, LANES), lambda i: (i, 0)),
                  pl.BlockSpec((None, 1, D), lambda i: (i // per_b, 0, 0)),
                  pl.BlockSpec((1, D), lambda i: (0, 0)),
                  pl.BlockSpec(memory_space=pl.ANY),
                  pl.BlockSpec(memory_space=pl.ANY)],
        out_specs=pl.BlockSpec((2 * tm, D), lambda i: (i, 0)),
        out_shape=jax.ShapeDtypeStruct((N, D), F32),
        scratch_shapes=[pltpu.VMEM((TOP_K, tm // 8, D // LANES, 8, LANES), F32),
                        pltpu.VMEM((TOP_K, tm // 8, D // LANES, 8, LANES), F32),
                        pltpu.SMEM((tm * TOP_K,), I32), pltpu.SMEM((tm * TOP_K,), I32),
                        pltpu.SemaphoreType.DMA, pltpu.SemaphoreType.DMA, pltpu.SemaphoreType.DMA,
                        pltpu.SemaphoreType.DMA],
        compiler_params=_cparams(("arbitrary",)),
        name="comb",
    )(x1, gate_tab, g2, final_g.reshape(1, D), dest_flat, ys)


def _tile(n, pref):
    t = min(pref, n)
    assert n % t == 0, (n, t)
    return t


def _layer(x, c, ada_w, ada_b, norm1_g, w_in, mix_scale, w_o, norm2_g,
           router_w, router_b, w_gu, b_gu, w_down, b_down, final_g):
    B, S, D = x.shape
    N = B * S
    mod = _mod(c, ada_w, ada_b).reshape(B, 6, 1, D)
    sh1, sc1, g1, sh2, sc2, g2 = (mod[:, j] for j in range(6))

    w_pad = jnp.pad(w_in, ((0, 0), (0, IN_COLS_PAD - IN_COLS))).astype(BF16)
    rq, rk, rv, rg, aqt, ak, avt, iqt, ik, iwt = _inproj(x, norm1_g, sc1, sh1, w_pad, _tile(S, PROJ_ROWS))
    ms = mix_scale.reshape(1, RET_W + DSA_W)
    ret = _retention(rq, rk, rv, rg, ms[:, :RET_W])
    att = _dsa(iqt, iwt, aqt, ik, ak, avt, ms[:, RET_W:], _tile(S, DSA_QUERIES), DSA_KEYS)

    rw_pad = jnp.pad(router_w, ((0, 0), (0, LANES - N_EXPERTS)))
    rw_hi = rw_pad.astype(BF16)
    rw_pad = jnp.stack([rw_hi, (rw_pad - rw_hi.astype(F32)).astype(BF16)])
    rb_pad = jnp.pad(router_b, (0, LANES - N_EXPERTS)).reshape(1, LANES)
    x1, h2, sel, idx_tab, gate_tab, counts = _oproj(ret, att, x, w_o.astype(BF16), g1, norm2_g, sc2, sh2,
                                                    rw_pad, rb_pad, _tile(S, PROJ_ROWS))

    tmb = FFN_ROWS
    n_rows = (N * TOP_K + N_EXPERTS * (tmb - 1)) // tmb * tmb + tmb
    cnt = counts[0, :N_EXPERTS].astype(I32)
    padded = (cnt + tmb - 1) // tmb * tmb
    ends = jnp.cumsum(padded)
    starts = ends - padded
    pstart = jnp.pad(starts.astype(F32), (0, LANES - N_EXPERTS)).reshape(1, LANES)
    n_blocks = (ends[-1] // tmb).reshape(1)
    first_row = jnp.arange(n_rows // tmb, dtype=I32) * tmb
    block_e = jnp.minimum(jnp.sum((ends[None, :] <= first_row[:, None]).astype(I32), axis=1), N_EXPERTS - 1)

    tmd = _tile(N, MOE_TOKENS)
    dest_tab = _dest(sel, idx_tab, pstart, tmd)
    dest_flat = dest_tab[:, :TOP_K].reshape(N * TOP_K)
    xs = _dispatch(h2, dest_flat, starts + cnt, n_rows, tmb, tmd)
    eid = jnp.arange(N_EXPERTS, dtype=I32)
    later_used = (eid[None, :] > eid[:, None]) & (padded[None, :] > 0)
    next_used = jnp.min(jnp.where(later_used, eid[None, :], N_EXPERTS), axis=1)
    next_e = jnp.where(next_used < N_EXPERTS, next_used, -1)[block_e].astype(I32)
    ys = _ffn(xs, block_e, n_blocks, next_e, w_gu, b_gu, w_down, b_down, tmb)
    out = _combine(x1, gate_tab, g2, final_g, dest_flat, ys, S, _tile(S, COMB_TOKENS))
    return out.reshape(B, S, D)


def kernel(x, c, ada_w, ada_b, norm1_g, w_in, mix_scale, w_o, norm2_g, router_w, router_b, w_gu, b_gu,
           w_down, b_down, final_g):
    assert ada_w.shape[0] == 1, "single-layer stack"
    return _layer(x, c, ada_w[0], ada_b[0], norm1_g[0], w_in[0], mix_scale[0], w_o[0], norm2_g[0],
                  router_w[0], router_b[0], w_gu[0], b_gu[0], w_down[0], b_down[0], final_g)
```

```python
import functools

import numpy as np
import jax
import jax.numpy as jnp
from jax import lax
from jax.experimental import pallas as pl
from jax.experimental.pallas import tpu as pltpu

F32 = jnp.float32
BF16 = jnp.bfloat16
I32 = jnp.int32

D_MODEL = 1024
RET_HEADS = 4
RET_DK = 64
RET_DV = 128
RET_CHUNK = 128
DSA_HEADS = 8
DSA_KV_HEADS = 2
DSA_HD = 64
IDX_HEADS = 8
IDX_HD = 64
TOPK_MAX = 256
N_EXPERTS = 32
TOP_K = 4
D_EXPERT = D_MODEL
SWIGLU_LIMIT = 7.0
SWIGLU_ALPHA = 1.702
EPS = 1e-6

RET_W = RET_HEADS * RET_DV
DSA_W = DSA_HEADS * DSA_HD
IN_COLS = 2888
IN_COLS_PAD = 2944

KAUG = 128
VAUG = 80
ALIBI_SPLIT = 64
ALIBI_TERMS = 3
LOG2E = 1.4426950408889634
PROJ_ROWS = 512
DSA_QUERIES = 512
DSA_KEYS = 128
MOE_TOKENS = 512
DISP_TOKENS = 1024
COMB_TOKENS = 256
RET_CHUNKS_PER_STEP = 8
FFN_ROWS = 512
OPROJ_PARTS = 2
FIX_UNROLL = 4
BISECT_COARSE_STEPS = 8
BISECT_MAX_STEPS = 40

LANES = 128
NCH = D_MODEL // LANES
VMEM_LIMIT = 56 * 1024 * 1024
NEG_BIG = -1e30
F32_LOWEST = float(np.finfo(np.float32).min)


def _cparams(sem):
    return pltpu.CompilerParams(dimension_semantics=sem, vmem_limit_bytes=VMEM_LIMIT)


def _mod_kernel(c_ref, w_ref, b_ref, o_ref):
    c = c_ref[...]
    s = c * (1.0 / (1.0 + jnp.exp(-c)))
    o_ref[...] = jnp.dot(s, w_ref[...], preferred_element_type=F32,
                         precision=lax.Precision.HIGHEST) + b_ref[...]


def _mod(c, ada_w, ada_b):
    B, D = c.shape
    n_out = ada_w.shape[1]
    rows = 8
    c8 = jnp.zeros((rows, D), F32).at[:B].set(c)
    out = pl.pallas_call(
        _mod_kernel,
        grid=(n_out // D,),
        in_specs=[pl.BlockSpec((rows, D), lambda j: (0, 0)),
                  pl.BlockSpec((D, D), lambda j: (0, j)),
                  pl.BlockSpec((1, D), lambda j: (0, j))],
        out_specs=pl.BlockSpec((rows, D), lambda j: (0, j)),
        out_shape=jax.ShapeDtypeStruct((rows, n_out), F32),
        compiler_params=_cparams(("arbitrary",)),
        name="mod",
    )(c8, ada_w, ada_b.reshape(1, n_out))
    return out[:B]


def _inproj_kernel(x_ref, g_ref, sc_ref, sh_ref, w_ref,
                   rq_ref, rk_ref, rv_ref, rg_ref, aqt_ref, ak_ref, avt_ref, iqt_ref, ik_ref, iwt_ref):
    x = x_ref[...]
    ms = jnp.mean(x * x, axis=-1, keepdims=True)
    y = x * lax.rsqrt(ms + EPS) * g_ref[...]
    hb = (y * (1.0 + sc_ref[...]) + sh_ref[...]).astype(BF16)

    def proj(lo, hi):
        return jnp.dot(hb, w_ref[:, lo:hi], preferred_element_type=F32)

    tm = x.shape[0]
    d = DSA_HD
    rq_ref[...] = proj(0, 256).astype(BF16)
    rk_ref[...] = (proj(256, 512) * (RET_DK ** -0.5)).astype(BF16)
    rv_ref[...] = proj(512, 1024).astype(BF16)
    rg_ref[...] = proj(1024, 1536).astype(BF16)
    aqt_ref[...] = (proj(1536, 2048) * (d ** -0.5 * LOG2E)).T.astype(BF16)
    kk = proj(2048, 2176)
    pos = pl.program_id(1) * tm + lax.broadcasted_iota(I32, (tm, d), 0)
    col = lax.broadcasted_iota(I32, (tm, d), 1)
    posblk = jnp.where(col < ALIBI_TERMS, pos // ALIBI_SPLIT,
                       jnp.where(col < 2 * ALIBI_TERMS, pos % ALIBI_SPLIT, 0)).astype(F32)
    for g in range(DSA_KV_HEADS):
        ak_ref[:, g * KAUG:g * KAUG + d] = kk[:, g * d:(g + 1) * d].astype(BF16)
        ak_ref[:, g * KAUG + d:(g + 1) * KAUG] = posblk.astype(BF16)
    vt = proj(2176, 2304).T
    r16 = lax.broadcasted_iota(I32, (VAUG - d, tm), 0)
    onesblk = jnp.where(r16 == 0, 1.0, 0.0).astype(BF16)
    for g in range(DSA_KV_HEADS):
        avt_ref[g * VAUG:g * VAUG + d, :] = vt[g * d:(g + 1) * d, :].astype(BF16)
        avt_ref[g * VAUG + d:(g + 1) * VAUG, :] = onesblk
    iqt_ref[...] = proj(2304, 2816).T.astype(BF16)
    last = proj(2816, 2944)
    ik_ref[...] = last[:, :IDX_HD].astype(BF16)
    iwt_ref[...] = last.T[IDX_HD:IDX_HD + IDX_HEADS, :] * ((IDX_HD ** -0.5) * (IDX_HEADS ** -0.5))


def _inproj(x, norm_g, sc, sh, w_pad, tm):
    B, S, D = x.shape
    row = lambda w: pl.BlockSpec((None, tm, w), lambda b, i: (b, i, 0))
    colT = lambda h: pl.BlockSpec((None, h, tm), lambda b, i: (b, 0, i))
    vec = pl.BlockSpec((None, 1, D), lambda b, i: (b, 0, 0))
    sd = lambda shape, dt: jax.ShapeDtypeStruct(shape, dt)
    G = DSA_KV_HEADS
    return pl.pallas_call(
        _inproj_kernel,
        grid=(B, S // tm),
        in_specs=[row(D), pl.BlockSpec((1, D), lambda b, i: (0, 0)), vec, vec,
                  pl.BlockSpec((D, IN_COLS_PAD), lambda b, i: (0, 0))],
        out_specs=[row(256), row(256), row(512), row(512), colT(DSA_W), row(G * KAUG), colT(G * VAUG),
                   colT(IDX_HEADS * IDX_HD), row(IDX_HD), colT(IDX_HEADS)],
        out_shape=[sd((B, S, 256), BF16), sd((B, S, 256), BF16), sd((B, S, 512), BF16),
                   sd((B, S, 512), BF16), sd((B, DSA_W, S), BF16), sd((B, S, G * KAUG), BF16),
                   sd((B, G * VAUG, S), BF16), sd((B, IDX_HEADS * IDX_HD, S), BF16),
                   sd((B, S, IDX_HD), BF16), sd((B, IDX_HEADS, S), F32)],
        compiler_params=_cparams(("parallel", "parallel")),
        name="inproj",
    )(x, norm_g.reshape(1, D), sc, sh, w_pad)


def _ret_kernel(rq_ref, rk_ref, rv_ref, rg_ref, din_ref, qd_ref, kd_ref, cd_ref, ms_ref, o_ref, state_ref):
    @pl.when(pl.program_id(1) == 0)
    def _():
        state_ref[...] = jnp.zeros_like(state_ref)

    C = din_ref.shape[1]
    chunks = [slice(c * C, (c + 1) * C) for c in range(rq_ref.shape[0] // C)]
    for h in range(RET_HEADS):
        qcol = slice(h * RET_DK, (h + 1) * RET_DK)
        vcol = slice(h * RET_DV, (h + 1) * RET_DV)
        local, kvs = [], []
        for rows in chunks:
            q, k, v = rq_ref[rows, qcol], rk_ref[rows, qcol], rv_ref[rows, vcol]
            s = lax.dot_general(q, k, (((1,), (1,)), ((), ())), preferred_element_type=F32) * din_ref[h]
            local.append(s.astype(BF16))
            vd = (v.astype(F32) * kd_ref[h]).astype(BF16)
            kvs.append(lax.dot_general(k, vd, (((0,), (0,)), ((), ())), preferred_element_type=F32))
        states, r = [], state_ref[h]
        for kv in kvs:
            states.append(r.astype(BF16))
            r = r * cd_ref[h] + kv
        state_ref[h] = r
        for rows, s, r_prev in zip(chunks, local, states):
            q, v = rq_ref[rows, qcol], rv_ref[rows, vcol]
            o = jnp.dot(s, v, preferred_element_type=F32)
            o = o + jnp.dot(q, r_prev, preferred_element_type=F32) * qd_ref[h]
            o = o * lax.rsqrt(jnp.mean(o * o, axis=-1, keepdims=True) + EPS)
            g = rg_ref[rows, vcol].astype(F32)
            gate = g * (1.0 / (1.0 + jnp.exp(-g)))
            o_ref[rows, vcol] = (gate * o * ms_ref[:, vcol]).astype(BF16)


def _ret_consts(C):
    H = RET_HEADS
    log_g = np.log1p(-np.exp2(-5.0 - np.arange(H, dtype=np.float64)))
    pos = np.arange(C, dtype=np.float64)
    diff = pos[:, None] - pos[None, :]
    d_inner = np.where(diff[None] >= 0, np.exp(np.maximum(diff, 0.0)[None] * log_g[:, None, None]), 0.0)
    q_decay = np.exp((pos + 1.0)[None] * log_g[:, None])
    k_decay = np.exp((C - 1.0 - pos)[None] * log_g[:, None])
    chunk_decay = np.exp(C * log_g)
    qd = np.broadcast_to(q_decay[:, :, None], (H, C, RET_DV))
    kd = np.broadcast_to(k_decay[:, :, None], (H, C, RET_DV))
    cd = np.broadcast_to(chunk_decay[:, None, None], (H, 1, RET_DV))
    f = lambda a: jnp.asarray(np.ascontiguousarray(a), F32)
    return f(d_inner), f(qd), f(kd), f(cd)


def _retention(rq, rk, rv, rg, ms_ret):
    B, S, _ = rq.shape
    C = min(RET_CHUNK, S)
    din, qd, kd, cd = _ret_consts(C)
    rows = _tile(S, RET_CHUNKS_PER_STEP * C)
    row = lambda w: pl.BlockSpec((None, rows, w), lambda b, n: (b, n, 0))
    full = lambda a: pl.BlockSpec(a.shape, lambda b, n: (0,) * a.ndim)
    return pl.pallas_call(
        _ret_kernel,
        grid=(B, S // rows),
        in_specs=[row(256), row(256), row(512), row(512), full(din), full(qd), full(kd), full(cd),
                  pl.BlockSpec((1, RET_W), lambda b, n: (0, 0))],
        out_specs=row(RET_W),
        out_shape=jax.ShapeDtypeStruct((B, S, RET_W), BF16),
        scratch_shapes=[pltpu.VMEM((RET_HEADS, RET_DK, RET_DV), F32)],
        compiler_params=_cparams(("parallel", "arbitrary")),
        name="ret",
    )(rq, rk, rv, rg, din, qd, kd, cd, ms_ret)


def _f32_key(x):
    i = lax.bitcast_convert_type(x, I32)
    return i ^ ((i >> 31) & 0x7FFFFFFF)


def _key_f32(k):
    return lax.bitcast_convert_type(k ^ ((k >> 31) & 0x7FFFFFFF), F32)


def _floor16(x):
    i = lax.bitcast_convert_type(x, I32)
    return lax.bitcast_convert_type((i + ((i >> 31) & 0xFFFF)) & -0x10000, F32)


def _trunc16(x):
    return lax.bitcast_convert_type(lax.bitcast_convert_type(x, I32) & -0x10000, F32)


def _dsa_kernel(iqt_ref, iwt_ref, aqt_ref, ik_ref, ak_ref, avt_ref, ms_ref, o_ref, score_ref, code_ref, qa_ref, sa_ref, sb_ref, mask_ref, *acc_refs,
                tq, tks, n_sel):
    H, G, R, d = DSA_HEADS, DSA_KV_HEADS, DSA_HEADS // DSA_KV_HEADS, DSA_HD
    t0 = pl.program_id(1) * tq
    nsub = (t0 + tq) // tks
    tka = 2 * tks
    npair = ((t0 + tq) // tka + 1) // 2
    kf = float(n_sel)
    qpos = t0 + lax.broadcasted_iota(I32, (1, tq), 1)
    krow = lax.broadcasted_iota(I32, (tks, tq), 0)

    wrow = [iwt_ref[h:h + 1, :] for h in range(IDX_HEADS)]

    def fold8(x, op):
        acc = x[0:8, :]
        for i in range(1, tks // 8):
            acc = op(acc, x[8 * i:8 * (i + 1), :])
        return acc

    def score_pair(i, carry, masked):
        mx, mn, npos, nnon = carry
        for u in range(2 * tka // tks):
            r0 = pl.multiple_of(i * 2 * tka + u * tks, tks)
            kc = ik_ref[pl.ds(r0, tks), :]
            acc = jnp.zeros((tks, tq), F32)
            for h in range(IDX_HEADS):
                rel = jnp.dot(kc, iqt_ref[h * IDX_HD:(h + 1) * IDX_HD, :], preferred_element_type=F32)
                acc = acc + jnp.maximum(rel, 0.0) * wrow[h]
            if masked:
                causal = r0 + krow <= qpos
                sc = jnp.where(causal, acc, -jnp.inf)
                lowest = jnp.where(causal, acc, jnp.inf)
            else:
                sc = lowest = acc
            score_ref[pl.ds(r0, tks), :] = sc
            code_ref[pl.ds(r0, tks), :] = _trunc16(sc).astype(BF16)
            mx = jnp.maximum(mx, fold8(sc, jnp.maximum))
            mn = jnp.minimum(mn, fold8(lowest, jnp.minimum))
            npos = npos + fold8(jnp.where(sc > 0.0, 1.0, 0.0), jnp.add)
            nnon = nnon + fold8(jnp.where(sc >= 0.0, 1.0, 0.0), jnp.add)
        return mx, mn, npos, nnon

    stat0 = (jnp.full((8, tq), -jnp.inf, F32), jnp.full((8, tq), jnp.inf, F32),
             jnp.zeros((8, tq), F32), jnp.zeros((8, tq), F32))
    n_inner = (t0 + 1) // (2 * tka)
    stat = lax.fori_loop(0, n_inner, functools.partial(score_pair, masked=False), stat0)
    mx, mn, npos, nnon = lax.fori_loop(n_inner, npair, functools.partial(score_pair, masked=True), stat)
    top = jnp.max(mx, axis=0, keepdims=True)
    lo0 = jnp.min(mn, axis=0, keepdims=True)
    n_pos = jnp.sum(npos, axis=0, keepdims=True)
    n_nonneg = jnp.sum(nnon, axis=0, keepdims=True)


    def count(th, strict):
        def body(j, acc):
            for u in range(2):
                s = score_ref[pl.ds(pl.multiple_of((2 * j + u) * tks, tks), tks), :]
                hit = (s > th) if strict else (s >= th)
                acc = acc + fold8(jnp.where(hit, 1.0, 0.0), jnp.add)
            return acc

        acc = lax.fori_loop(0, nsub // 2, body, jnp.zeros((8, tq), F32))
        return jnp.sum(acc, axis=0, keepdims=True)

    def probe(lo, hi):
        lk, hk = _f32_key(lo), _f32_key(hi)
        mk = (lk >> 1) + (hk >> 1) + (lk & hk & 1)
        return _key_f32(mk), jnp.max(jnp.where(mk != lk, 1.0, 0.0))

    def bis_cond(c):
        return (c[5] > 0.0) & (c[6] < BISECT_MAX_STEPS)

    def bis_body(c):
        lo, hi, c_lo, c_hi, mid, _, it = c
        cnt = count(mid, False)
        ge = cnt >= kf
        up = ge | (cnt == kf)
        dn = (~ge) | (cnt == kf)
        lo, c_lo = jnp.where(up, mid, lo), jnp.where(up, cnt, c_lo)
        hi, c_hi = jnp.where(dn, mid, hi), jnp.where(dn, cnt, c_hi)
        mid, active = probe(lo, hi)
        return lo, hi, c_lo, c_hi, mid, active, it + 1

    zero = jnp.zeros((1, tq), F32)
    keep_all = qpos + 1 <= n_sel
    settled = keep_all | ((n_nonneg >= kf) & (n_pos < kf))
    above = n_pos >= kf
    c_lo0 = jnp.where(settled | above, n_nonneg, (qpos + 1).astype(F32))
    c_hi0 = jnp.where(settled | ~above, n_nonneg, zero)
    lo0 = jnp.where(settled | above, zero, lo0)
    hi0 = jnp.where(settled | ~above, zero, _key_f32(_f32_key(top) + 1))
    def count16(m):
        mb = m.astype(BF16)

        def body(j, acc):
            for u in range(2):
                c = code_ref[pl.ds(pl.multiple_of((2 * j + u) * tks, tks), tks), :]
                one = jnp.where(c >= mb, jnp.ones((), BF16), jnp.zeros((), BF16))
                part = one[0:16, :]
                for i in range(1, tks // 16):
                    part = part + one[16 * i:16 * (i + 1), :]
                acc = acc + part.astype(F32)
            return acc

        acc = lax.fori_loop(0, nsub // 2, body, jnp.zeros((16, tq), F32))
        return jnp.sum(acc, axis=0, keepdims=True)

    def coarse_step(_, c):
        lo, hi, c_lo, c_hi = c
        below = _floor16(lo + (hi - lo) * 0.5)
        above = _key_f32(_f32_key(below) + 0x10000)
        m = jnp.where(below > lo, below, above)
        v = jnp.where(m > 0.0, m, _key_f32(_f32_key(m) - 0x10000 + 1))
        ok = (v > lo) & (v < hi)
        cnt = count16(m)
        ge = cnt >= kf
        up = ok & ge
        dn = ok & ((~ge) | (cnt == kf))
        return (jnp.where(up, v, lo), jnp.where(dn, v, hi), jnp.where(up, cnt, c_lo), jnp.where(dn, cnt, c_hi))

    lo0, hi0, c_lo0, c_hi0 = lax.fori_loop(0, BISECT_COARSE_STEPS, coarse_step, (lo0, hi0, c_lo0, c_hi0))
    mid0, active0 = probe(lo0, hi0)
    lo, hi, c_lo, c_hi, _, _, _ = lax.while_loop(bis_cond, bis_body,
                                                 (lo0, hi0, c_lo0, c_hi0, mid0, active0, jnp.int32(0)))
    at_hi = c_hi >= kf
    thr = jnp.where(keep_all, F32_LOWEST, jnp.where(at_hi, hi, lo))
    excess = jnp.where(keep_all, 0.0, jnp.where(at_hi, c_hi, c_lo) - kf)

    @pl.when(jnp.max(excess) > 0.0)
    def _():
        tied_nonzero = jnp.max(jnp.where((excess > 0.0) & (thr != 0.0), 1.0, 0.0)) > 0.0
        n_above = lax.cond(tied_nonzero, lambda: count(thr, True), lambda: n_pos)
        budget = jnp.where(excess > 0.0, kf - jnp.where(thr == 0.0, n_pos, n_above), jnp.inf)
        earlier = lax.broadcasted_iota(I32, (tks, tks), 1) < lax.broadcasted_iota(I32, (tks, tks), 0)
        earlier = jnp.where(earlier, 1.0, 0.0).astype(BF16)

        def fix(j, seen):
            tiles = []
            for u in range(FIX_UNROLL):
                r0 = pl.multiple_of((FIX_UNROLL * j + u) * tks, tks)
                s = score_ref[pl.ds(r0, tks), :]
                eq = s == thr
                eqf = jnp.where(eq, 1.0, 0.0)
                within = jnp.dot(earlier, eqf.astype(BF16), preferred_element_type=F32)
                tiles.append((r0, s, eq, within, jnp.sum(eqf, axis=0, keepdims=True)))
            for r0, s, eq, within, n_eq in tiles:
                score_ref[pl.ds(r0, tks), :] = jnp.where(eq & (within + seen >= budget), -jnp.inf, s)
                seen = seen + n_eq
            return seen

        lax.fori_loop(0, nsub // FIX_UNROLL, fix, jnp.zeros((1, tq), F32))

    arow = lax.broadcasted_iota(I32, (KAUG - d, tq), 0)
    for h in range(H):
        terms, rest = [], 2.0 ** (-8.0 * (h + 1) / H) * LOG2E
        for _ in range(ALIBI_TERMS):
            terms.append(float(np.asarray(rest, np.float32).astype(BF16).astype(np.float64)))
            rest -= terms[-1]
        rows = jnp.zeros((KAUG - d, tq), F32)
        for i, c in enumerate(terms):
            rows = jnp.where(arow == i, c * ALIBI_SPLIT, jnp.where(arow == ALIBI_TERMS + i, c, rows))
        qa_ref[h, 0:d, :] = aqt_ref[h * d:(h + 1) * d, :]
        qa_ref[h, d:KAUG, :] = rows.astype(BF16)
    for acc in acc_refs:
        acc[...] = jnp.zeros_like(acc)

    def logits(j, h):
        ka = ak_ref[pl.ds(pl.multiple_of(j * tka, tka), tka), (h // R) * KAUG:(h // R + 1) * KAUG]
        return jnp.dot(ka, qa_ref[h], preferred_element_type=F32)

    def step(j, j_next, cur_ref, next_ref, ms):
        r0 = pl.multiple_of(j * tka, tka)
        mask_ref[...] = jnp.where(score_ref[pl.ds(r0, tka), :] >= thr, 0.0, NEG_BIG)
        nms = []
        ahead = logits(j_next, 0)
        for h in range(H):
            g = h // R
            next_ref[h] = ahead
            if h + 1 < H:
                ahead = logits(j_next, h + 1)
            s = cur_ref[h] + mask_ref[...]
            m_new = jnp.maximum(ms[h], jnp.max(s, axis=0, keepdims=True))
            p = jnp.exp2(s - m_new).astype(BF16)
            va = avt_ref[g * VAUG:(g + 1) * VAUG, pl.ds(r0, tka)]
            acc = acc_refs[h]
            acc[...] = acc[...] * jnp.exp2(ms[h] - m_new) + jnp.dot(va, p, preferred_element_type=F32)
            nms.append(m_new)
        return tuple(nms)

    for h in range(H):
        sa_ref[h] = logits(0, h)

    def att_pair(i, ms):
        ms = step(2 * i, 2 * i + 1, sa_ref, sb_ref, ms)
        return step(2 * i + 1, jnp.minimum(2 * i + 2, 2 * npair - 1), sb_ref, sa_ref, ms)

    lax.fori_loop(0, npair, att_pair, tuple(jnp.full((1, tq), NEG_BIG, F32) for _ in range(H)))
    for h in range(H):
        a = acc_refs[h][...]
        o = a[0:d, :] / a[d:d + 1, :]
        o_ref[h * d:(h + 1) * d, :] = (o * ms_ref[h * d:(h + 1) * d, :]).astype(BF16)


def _dsa(iqt, iwt, aqt, ik, ak, avt, ms_att, tq, tks):
    B, _, S = iqt.shape
    assert S <= ALIBI_SPLIT * 256 and tq % (2 * tks) == 0 and tq % (FIX_UNROLL * tks) == 0
    assert (S // (2 * tks)) % 2 == 0
    n_sel = min(TOPK_MAX, S // 4)
    G = DSA_KV_HEADS
    colT = lambda h: pl.BlockSpec((None, h, tq), lambda b, i: (b, 0, i))
    msb = jnp.broadcast_to(ms_att.reshape(DSA_W, 1), (DSA_W, tq))
    return pl.pallas_call(
        functools.partial(_dsa_kernel, tq=tq, tks=tks, n_sel=n_sel),
        grid=(B, S // tq),
        in_specs=[colT(IDX_HEADS * IDX_HD), colT(IDX_HEADS), colT(DSA_W),
                  pl.BlockSpec((None, S, IDX_HD), lambda b, i: (b, 0, 0), pipeline_mode=pl.Buffered(1)),
                  pl.BlockSpec((None, S, G * KAUG), lambda b, i: (b, 0, 0), pipeline_mode=pl.Buffered(1)),
                  pl.BlockSpec((None, G * VAUG, S), lambda b, i: (b, 0, 0), pipeline_mode=pl.Buffered(1)),
                  pl.BlockSpec((DSA_W, tq), lambda b, i: (0, 0))],
        out_specs=colT(DSA_W),
        out_shape=jax.ShapeDtypeStruct((B, DSA_W, S), BF16),
        scratch_shapes=[pltpu.VMEM((S, tq), F32), pltpu.VMEM((S, tq), BF16), pltpu.VMEM((DSA_HEADS, KAUG, tq), BF16),
                        pltpu.VMEM((DSA_HEADS, 2 * tks, tq), F32), pltpu.VMEM((DSA_HEADS, 2 * tks, tq), F32),
                        pltpu.VMEM((2 * tks, tq), F32)]
        + [pltpu.VMEM((VAUG, tq), F32) for _ in range(DSA_HEADS)],
        compiler_params=_cparams(("parallel", "arbitrary")),
        name="dsa",
    )(iqt, iwt, aqt, ik, ak, avt, msb)


def _oproj_kernel(ret_ref, att_ref, x_ref, wo_ref, g1_ref, n2_ref, sc_ref, sh_ref, rw_ref, rb_ref,
                  x1_ref, h2_ref, sel_ref, idx_ref, gate_ref, cnt_ref):
    tm = x_ref.shape[0]
    parts = [slice(r, r + tm // OPROJ_PARTS) for r in range(0, tm, tm // OPROJ_PARTS)]
    nch = x_ref.shape[1] // LANES

    def mix(rows):
        out = jnp.dot(ret_ref[rows, :], wo_ref[:RET_W, :], preferred_element_type=F32)
        return out + lax.dot_general(att_ref[:, rows], wo_ref[RET_W:, :], (((0,), (0,)), ((), ())),
                                     preferred_element_type=F32)

    def hidden(rows, mixo):
        x1 = x_ref[rows, :] + g1_ref[...] * mixo
        x1_ref[rows, :] = x1
        y = x1 * lax.rsqrt(jnp.mean(x1 * x1, axis=-1, keepdims=True) + EPS) * n2_ref[...]
        h2 = y * (1.0 + sc_ref[...]) + sh_ref[...]
        for j in range(nch):
            h2_ref[pl.ds(rows.start * nch + j, rows.stop - rows.start, stride=nch), :] = h2[:, j * LANES:(j + 1) * LANES]
        return h2

    def router(h2):
        h_hi = h2.astype(BF16)
        h_lo = (h2 - h_hi.astype(F32)).astype(BF16)
        logits = jnp.dot(h_hi, rw_ref[0], preferred_element_type=F32)
        return logits + (jnp.dot(h_hi, rw_ref[1], preferred_element_type=F32)
                         + jnp.dot(h_lo, rw_ref[0], preferred_element_type=F32)) + rb_ref[...]

    def top4(rows, logits):
        n = logits.shape[0]
        lane = lax.broadcasted_iota(I32, (n, LANES), 1).astype(F32)
        work = jnp.where(lane < N_EXPERTS, logits, -jnp.inf)
        sel = jnp.zeros((n, LANES), F32)
        idx_tab = jnp.zeros((n, LANES), F32)
        vals = []
        for k in range(TOP_K):
            m = jnp.max(work, axis=1, keepdims=True)
            idx = jnp.min(jnp.where(work == m, lane, float(LANES)), axis=1, keepdims=True)
            hit = lane == idx
            sel = jnp.where(hit, 1.0, sel)
            idx_tab = jnp.where(lane == k, idx, idx_tab)
            work = jnp.where(hit, -jnp.inf, work)
            vals.append(m)
        es = [jnp.exp(v - vals[0]) for v in vals]
        den = es[0] + es[1] + es[2] + es[3]
        gate_tab = jnp.zeros((n, LANES), F32)
        for k in range(TOP_K):
            gate_tab = jnp.where(lane == k, es[k] / den, gate_tab)
        sel_ref[rows, :] = sel
        idx_ref[rows, :] = idx_tab
        gate_ref[rows, :] = gate_tab
        return jnp.sum(sel, axis=0, keepdims=True)

    mixes = [mix(rows) for rows in parts]
    hiddens = [hidden(rows, m) for rows, m in zip(parts, mixes)]
    logit_parts = [router(h2) for h2 in hiddens]
    counts = [top4(rows, lg) for rows, lg in zip(parts, logit_parts)]

    @pl.when((pl.program_id(0) == 0) & (pl.program_id(1) == 0))
    def _():
        cnt_ref[...] = jnp.zeros_like(cnt_ref)

    cnt_ref[...] += functools.reduce(lambda a, b: a + b, counts)


def _oproj(ret, att, x, wo, g1, n2g, sc2, sh2, rw_pad, rb_pad, tm):
    B, S, D = x.shape
    nt = S // tm
    row = lambda w: pl.BlockSpec((None, tm, w), lambda b, i: (b, i, 0))
    flat = lambda w: pl.BlockSpec((tm, w), lambda b, i: (b * nt + i, 0))
    vec = pl.BlockSpec((None, 1, D), lambda b, i: (b, 0, 0))
    cst = lambda shape: pl.BlockSpec(shape, lambda b, i: (0, 0))
    sd = lambda shape, dt: jax.ShapeDtypeStruct(shape, dt)
    N = B * S
    return pl.pallas_call(
        _oproj_kernel,
        grid=(B, nt),
        in_specs=[row(RET_W), pl.BlockSpec((None, DSA_W, tm), lambda b, i: (b, 0, i)), row(D), cst((D, D)), vec,
                  cst((1, D)), vec, vec,
                  pl.BlockSpec((2, D, LANES), lambda b, i: (0, 0, 0)), cst((1, LANES))],
        out_specs=[flat(D), pl.BlockSpec((tm * (D // LANES), LANES), lambda b, i: (b * nt + i, 0)),
                   flat(LANES), flat(LANES), flat(LANES), cst((1, LANES))],
        out_shape=[sd((N, D), F32), sd((N * (D // LANES), LANES), F32), sd((N, LANES), F32), sd((N, LANES), F32),
                   sd((N, LANES), F32), sd((1, LANES), F32)],
        compiler_params=_cparams(("arbitrary", "arbitrary")),
        name="oproj",
    )(ret, att, x, wo, g1, n2g.reshape(1, D), sc2, sh2, rw_pad, rb_pad)


def _dest_kernel(sel_ref, idx_ref, pstart_ref, dest_ref, seen_ref):
    @pl.when(pl.program_id(0) == 0)
    def _():
        seen_ref[...] = jnp.zeros_like(seen_ref)

    sel = sel_ref[...]
    tm = sel.shape[0]
    earlier = lax.broadcasted_iota(I32, (tm, tm), 1) < lax.broadcasted_iota(I32, (tm, tm), 0)
    earlier = jnp.where(earlier, 1.0, 0.0).astype(BF16)
    rank = jnp.dot(earlier, sel.astype(BF16), preferred_element_type=F32) + seen_ref[...]
    dest = pstart_ref[...] + rank
    lane = lax.broadcasted_iota(I32, (tm, LANES), 1).astype(F32)
    idx_tab = idx_ref[...]
    out = jnp.zeros((tm, LANES), F32)
    for k in range(TOP_K):
        e_k = jnp.sum(jnp.where(lane == k, idx_tab, 0.0), axis=1, keepdims=True)
        d_k = jnp.sum(jnp.where(lane == e_k, dest, 0.0), axis=1, keepdims=True)
        out = jnp.where(lane == k, d_k, out)
    dest_ref[...] = out.astype(I32)
    seen_ref[...] += jnp.sum(sel, axis=0, keepdims=True)


def _dest(sel, idx_tab, pstart, tm):
    N = sel.shape[0]
    blk = pl.BlockSpec((tm, LANES), lambda i: (i, 0))
    return pl.pallas_call(
        _dest_kernel,
        grid=(N // tm,),
        in_specs=[blk, blk, pl.BlockSpec((1, LANES), lambda i: (0, 0))],
        out_specs=blk,
        out_shape=jax.ShapeDtypeStruct((N, LANES), I32),
        scratch_shapes=[pltpu.VMEM((1, LANES), F32)],
        compiler_params=_cparams(("arbitrary",)),
        name="dest",
    )(sel, idx_tab, pstart)


def _disp_kernel(zs_ref, h2_ref, dest_hbm, xs_hbm, zbuf, idx_a, idx_b, sem_ia, sem_ib, sem_row, sem_z,
                 *, tm, nch, zrows, n_steps):
    i = pl.program_id(0)
    n = tm * TOP_K

    def idx_copy(tile, buf, sem):
        return pltpu.make_async_copy(dest_hbm.at[pl.ds(tile * n, n)], buf, sem)

    def rows(ref, first, count):
        return ref.at[pl.ds(pl.multiple_of(first * nch, nch), count * nch)]

    def row_copy(row, dst):
        return pltpu.make_async_copy(rows(h2_ref, row, 1), rows(xs_hbm, dst, 1), sem_row)

    def issue_rows(first_row, idx):
        def body(r, c):
            for s in range(8):
                t = r * 8 + s
                for k in range(TOP_K):
                    row_copy(first_row + t, idx[t * TOP_K + k]).start(priority=k % 2)
            return c

        lax.fori_loop(0, tm // 8, body, 0)

    def wait_rows():
        pltpu.make_async_copy(rows(xs_hbm, 0, n), rows(xs_hbm, 0, n), sem_row).wait()

    @pl.when(i == 0)
    def _():
        zbuf[...] = jnp.zeros_like(zbuf)
        for e in range(N_EXPERTS):
            fill = pltpu.make_async_copy(zbuf, rows(xs_hbm, zs_ref[e], zrows), sem_z)
            fill.start()
            fill.wait()
        idx_copy(0, idx_a, sem_ia).start()
        idx_copy(1, idx_b, sem_ib).start()

    idx_copy(2 * i, idx_a, sem_ia).wait()
    issue_rows(0, idx_a)
    idx_copy(2 * i + 1, idx_b, sem_ib).wait()
    issue_rows(tm, idx_b)

    @pl.when(i + 1 < n_steps)
    def _():
        idx_copy(2 * i + 2, idx_a, sem_ia).start()
        idx_copy(2 * i + 3, idx_b, sem_ib).start()

    wait_rows()
    wait_rows()


def _dispatch(h2, dest_flat, zero_start, n_rows, zrows, tm):
    nch = D_MODEL // LANES
    n_steps = h2.shape[0] // nch // (2 * tm)
    grid_spec = pltpu.PrefetchScalarGridSpec(
        num_scalar_prefetch=1,
        grid=(n_steps,),
        in_specs=[pl.BlockSpec((2 * tm * nch, LANES), lambda i, zs: (i, 0)), pl.BlockSpec(memory_space=pl.ANY)],
        out_specs=pl.BlockSpec(memory_space=pl.ANY),
        scratch_shapes=[pltpu.VMEM((zrows * nch, LANES), F32), pltpu.SMEM((tm * TOP_K,), I32),
                        pltpu.SMEM((tm * TOP_K,), I32), pltpu.SemaphoreType.DMA, pltpu.SemaphoreType.DMA,
                        pltpu.SemaphoreType.DMA, pltpu.SemaphoreType.DMA],
    )
    return pl.pallas_call(
        functools.partial(_disp_kernel, tm=tm, nch=nch, zrows=zrows, n_steps=n_steps),
        grid_spec=grid_spec,
        out_shape=jax.ShapeDtypeStruct((n_rows * nch, LANES), F32),
        compiler_params=_cparams(("arbitrary",)),
        name="disp",
    )(zero_start, h2, dest_flat)


def _store_token_major(ref, x):
    rows, d = x.shape
    nch = d // LANES
    for j in range(nch):
        ref[pl.ds(j, rows, stride=nch), :] = x[:, j * LANES:(j + 1) * LANES]


def _load_token_major(ref, rows, nch):
    return jnp.concatenate([ref[pl.ds(j, rows, stride=nch), :] for j in range(nch)], axis=1)


def _ffn_kernel(be_ref, nb_ref, nxt_ref, xs_ref, wgu_hbm, bgu_ref, wd_hbm, bd_ref, ys_ref,
                wgu_st, wd_st, wgu_bf, wd_bf, sem_gu, sem_d):
    i = pl.program_id(0)
    live = i < nb_ref[0]
    e = be_ref[i]

    def fetch(ex):
        return (pltpu.make_async_copy(wgu_hbm.at[ex], wgu_st, sem_gu),
                pltpu.make_async_copy(wd_hbm.at[ex], wd_st, sem_d))

    @pl.when(i == 0)
    def _():
        for cp in fetch(e):
            cp.start()

    @pl.when(live & ((i == 0) | (e != be_ref[jnp.maximum(i - 1, 0)])))
    def _():
        for cp in fetch(e):
            cp.wait()
        wgu_bf[...] = wgu_st[...].astype(BF16)
        wd_bf[...] = wd_st[...].astype(BF16)

        @pl.when(nxt_ref[i] >= 0)
        def _():
            for cp in fetch(nxt_ref[i]):
                cp.start()

    @pl.when(live)
    def _():
        xb = _load_token_major(xs_ref, ys_ref.shape[0] // NCH, NCH).astype(BF16)
        gu = jnp.dot(xb, wgu_bf[...], preferred_element_type=F32) + bgu_ref[...]
        gate = jnp.minimum(gu[:, :D_EXPERT], SWIGLU_LIMIT)
        up = jnp.clip(gu[:, D_EXPERT:], -SWIGLU_LIMIT, SWIGLU_LIMIT)
        glu = gate * (1.0 / (1.0 + jnp.exp(-SWIGLU_ALPHA * gate)))
        act = ((up + 1.0) * glu).astype(BF16)
        ys = jnp.dot(act, wd_bf[...], preferred_element_type=F32) + bd_ref[...]
        _store_token_major(ys_ref, ys)

    @pl.when(jnp.logical_not(live))
    def _():
        ys_ref[...] = jnp.zeros_like(ys_ref)


def _ffn(xs, block_e, n_blocks, next_e, w_gu, b_gu, w_down, b_down, tmb):
    D = D_MODEL
    P = xs.shape[0] // NCH
    E = w_gu.shape[0]
    blk = lambda i, be, nb, nx: (jnp.minimum(i, nb[0] - 1), 0)
    wsel = lambda i, be, nb, nx: (be[jnp.minimum(i, nb[0] - 1)], 0, 0)
    grid_spec = pltpu.PrefetchScalarGridSpec(
        num_scalar_prefetch=3,
        grid=(P // tmb,),
        in_specs=[pl.BlockSpec((tmb * NCH, LANES), blk),
                  pl.BlockSpec(memory_space=pl.ANY),
                  pl.BlockSpec((None, 1, 2 * D_EXPERT), wsel),
                  pl.BlockSpec(memory_space=pl.ANY),
                  pl.BlockSpec((None, 1, D), wsel)],
        out_specs=pl.BlockSpec((tmb * (D // LANES), LANES), lambda i, be, nb, nx: (i, 0)),
        scratch_shapes=[pltpu.VMEM((D, 2 * D_EXPERT), F32), pltpu.VMEM((D_EXPERT, D), F32),
                        pltpu.VMEM((D, 2 * D_EXPERT), BF16), pltpu.VMEM((D_EXPERT, D), BF16),
                        pltpu.SemaphoreType.DMA, pltpu.SemaphoreType.DMA],
    )
    return pl.pallas_call(
        _ffn_kernel,
        grid_spec=grid_spec,
        out_shape=jax.ShapeDtypeStruct((P * (D // LANES), LANES), F32),
        compiler_params=_cparams(("arbitrary",)),
        name="ffn",
    )(block_e, n_blocks, next_e, xs, w_gu, b_gu.reshape(E, 1, 2 * D_EXPERT), w_down, b_down.reshape(E, 1, D))


def _comb_kernel(x1_ref, gate_ref, g2_ref, fg_ref, dest_hbm, ys_hbm, o_ref, buf_a, buf_b, idx_a, idx_b,
                 sem_ia, sem_ib, sem_ra, sem_rb, *, tm, n_steps):
    i = pl.program_id(0)
    n = tm * TOP_K

    def idx_copy(tile, buf, sem):
        return pltpu.make_async_copy(dest_hbm.at[pl.ds(tile * n, n)], buf, sem)

    nch = x1_ref.shape[1] // LANES

    def row_copy(src, buf, k, t, sem):
        return pltpu.make_async_copy(ys_hbm.at[pl.ds(pl.multiple_of(src * nch, nch), nch)],
                                     buf.at[k, t[0], :, t[1]], sem)

    def issue_rows(idx, buf, sem):
        def body(r, c):
            for s in range(8):
                for k in range(TOP_K):
                    row_copy(idx[(r * 8 + s) * TOP_K + k], buf, k, (r, s), sem).start(priority=k % 2)
            return c

        lax.fori_loop(0, tm // 8, body, 0)

    def wait_rows(buf, sem):
        pltpu.make_async_copy(buf, buf, sem).wait()

    def finish(buf, lo):
        gates = gate_ref[lo:lo + tm, :]
        y = jnp.zeros((tm, x1_ref.shape[1]), F32)
        for k in range(TOP_K):
            rows = jnp.concatenate([buf[k, :, j].reshape(tm, LANES) for j in range(nch)], axis=1)
            y = y + gates[:, k:k + 1] * rows
        v = x1_ref[lo:lo + tm, :] + g2_ref[...] * y
        o_ref[lo:lo + tm, :] = v * lax.rsqrt(jnp.mean(v * v, axis=-1, keepdims=True) + EPS) * fg_ref[...]

    @pl.when(i == 0)
    def _():
        first = idx_copy(0, idx_a, sem_ia)
        first.start()
        first.wait()
        issue_rows(idx_a, buf_a, sem_ra)
        idx_copy(1, idx_b, sem_ib).start()

    idx_copy(2 * i + 1, idx_b, sem_ib).wait()
    issue_rows(idx_b, buf_b, sem_rb)

    @pl.when(i + 1 < n_steps)
    def _():
        idx_copy(2 * i + 2, idx_a, sem_ia).start()

    wait_rows(buf_a, sem_ra)
    finish(buf_a, 0)

    @pl.when(i + 1 < n_steps)
    def _():
        idx_copy(2 * i + 2, idx_a, sem_ia).wait()
        issue_rows(idx_a, buf_a, sem_ra)
        idx_copy(2 * i + 3, idx_b, sem_ib).start()

    wait_rows(buf_b, sem_rb)
    finish(buf_b, tm)


def _combine(x1, gate_tab, g2, final_g, dest_flat, ys, S, tm):
    N, D = x1.shape
    n_steps = N // (2 * tm)
    per_b = S // (2 * tm)
    return pl.pallas_call(
        functools.partial(_comb_kernel, tm=tm, n_steps=n_steps),
        grid=(n_steps,),
        in_specs=[pl.BlockSpec((2 * tm, D), lambda i: (i, 0)),
                  pl.BlockSpec((2 * tm, LANES), lambda i: (i, 0)),
                  pl.BlockSpec((None, 1, D), lambda i: (i // per_b, 0, 0)),
                  pl.BlockSpec((1, D), lambda i: (0, 0)),
                  pl.BlockSpec(memory_space=pl.ANY),
                  pl.BlockSpec(memory_space=pl.ANY)],
        out_specs=pl.BlockSpec((2 * tm, D), lambda i: (i, 0)),
        out_shape=jax.ShapeDtypeStruct((N, D), F32),
        scratch_shapes=[pltpu.VMEM((TOP_K, tm // 8, D // LANES, 8, LANES), F32),
                        pltpu.VMEM((TOP_K, tm // 8, D // LANES, 8, LANES), F32),
                        pltpu.SMEM((tm * TOP_K,), I32), pltpu.SMEM((tm * TOP_K,), I32),
                        pltpu.SemaphoreType.DMA, pltpu.SemaphoreType.DMA, pltpu.SemaphoreType.DMA,
                        pltpu.SemaphoreType.DMA],
        compiler_params=_cparams(("arbitrary",)),
        name="comb",
    )(x1, gate_tab, g2, final_g.reshape(1, D), dest_flat, ys)


def _tile(n, pref):
    t = min(pref, n)
    assert n % t == 0, (n, t)
    return t


def _layer(x, c, ada_w, ada_b, norm1_g, w_in, mix_scale, w_o, norm2_g,
           router_w, router_b, w_gu, b_gu, w_down, b_down, final_g):
    B, S, D = x.shape
    N = B * S
    mod = _mod(c, ada_w, ada_b).reshape(B, 6, 1, D)
    sh1, sc1, g1, sh2, sc2, g2 = (mod[:, j] for j in range(6))

    w_pad = jnp.pad(w_in, ((0, 0), (0, IN_COLS_PAD - IN_COLS))).astype(BF16)
    rq, rk, rv, rg, aqt, ak, avt, iqt, ik, iwt = _inproj(x, norm1_g, sc1, sh1, w_pad, _tile(S, PROJ_ROWS))
    ms = mix_scale.reshape(1, RET_W + DSA_W)
    ret = _retention(rq, rk, rv, rg, ms[:, :RET_W])
    att = _dsa(iqt, iwt, aqt, ik, ak, avt, ms[:, RET_W:], _tile(S, DSA_QUERIES), DSA_KEYS)

    rw_pad = jnp.pad(router_w, ((0, 0), (0, LANES - N_EXPERTS)))
    rw_hi = rw_pad.astype(BF16)
    rw_pad = jnp.stack([rw_hi, (rw_pad - rw_hi.astype(F32)).astype(BF16)])
    rb_pad = jnp.pad(router_b, (0, LANES - N_EXPERTS)).reshape(1, LANES)
    x1, h2, sel, idx_tab, gate_tab, counts = _oproj(ret, att, x, w_o.astype(BF16), g1, norm2_g, sc2, sh2,
                                                    rw_pad, rb_pad, _tile(S, PROJ_ROWS))

    tmb = FFN_ROWS
    n_rows = (N * TOP_K + N_EXPERTS * (tmb - 1)) // tmb * tmb + tmb
    cnt = counts[0, :N_EXPERTS].astype(I32)
    padded = (cnt + tmb - 1) // tmb * tmb
    ends = jnp.cumsum(padded)
    starts = ends - padded
    pstart = jnp.pad(starts.astype(F32), (0, LANES - N_EXPERTS)).reshape(1, LANES)
    n_blocks = (ends[-1] // tmb).reshape(1)
    first_row = jnp.arange(n_rows // tmb, dtype=I32) * tmb
    block_e = jnp.minimum(jnp.sum((ends[None, :] <= first_row[:, None]).astype(I32), axis=1), N_EXPERTS - 1)

    tmd = _tile(N, MOE_TOKENS)
    dest_tab = _dest(sel, idx_tab, pstart, tmd)
    dest_flat = dest_tab[:, :TOP_K].reshape(N * TOP_K)
    xs = _dispatch(h2, dest_flat, starts + cnt, n_rows, tmb, _tile(N, DISP_TOKENS))
    eid = jnp.arange(N_EXPERTS, dtype=I32)
    later_used = (eid[None, :] > eid[:, None]) & (padded[None, :] > 0)
    next_used = jnp.min(jnp.where(later_used, eid[None, :], N_EXPERTS), axis=1)
    next_e = jnp.where(next_used < N_EXPERTS, next_used, -1)[block_e].astype(I32)
    ys = _ffn(xs, block_e, n_blocks, next_e, w_gu, b_gu, w_down, b_down, tmb)
    out = _combine(x1, gate_tab, g2, final_g, dest_flat, ys, S, _tile(S, COMB_TOKENS))
    return out.reshape(B, S, D)


def kernel(x, c, ada_w, ada_b, norm1_g, w_in, mix_scale, w_o, norm2_g, router_w, router_b, w_gu, b_gu,
           w_down, b_down, final_g):
    assert ada_w.shape[0] == 1, "single-layer stack"
    return _layer(x, c, ada_w[0], ada_b[0], norm1_g[0], w_in[0], mix_scale[0], w_o[0], norm2_g[0],
                  router_w[0], router_b[0], w_gu[0], b_gu[0], w_down[0], b_down[0], final_g)
```
